```python
import jax, jax.numpy as jnp
from jax import lax
import numpy as np

D_MODEL = 1024
BATCH = 8
SEQ = 16384
DEPTH = 1

HEAD_DIM = 64
N_HEADS = D_MODEL // HEAD_DIM
N_SB_HEADS = N_HEADS // 2
N_FOX_HEADS = N_HEADS - N_SB_HEADS
SB_WIDTH = N_SB_HEADS * HEAD_DIM
FOX_WIDTH = N_FOX_HEADS * HEAD_DIM
IN_WIDTH = 3 * SB_WIDTH + 3 * FOX_WIDTH + N_FOX_HEADS
D_FF = ((-(-8 * D_MODEL // 3) + 255) // 256) * 256
BLOCK_Q = 128
DEEPNORM_ALPHA = (2 * DEPTH) ** 0.25
DEEPNORM_BETA = (8 * DEPTH) ** -0.25
LN_EPS = 1e-5
RMS_EPS = 1e-6

kernel_name = "hybrid_stickbreaking_forgetting_deepnorm"


def layer_norm(x, g, b):
    xf = x.astype(jnp.float32)
    mu = jnp.mean(xf, axis=-1, keepdims=True)
    var = jnp.mean(jnp.square(xf - mu), axis=-1, keepdims=True)
    return ((xf - mu) * lax.rsqrt(var + LN_EPS) * g + b).astype(x.dtype)


def head_rmsnorm(o, g):
    B, H, S, d = o.shape
    of = o.astype(jnp.float32)
    of = of * lax.rsqrt(jnp.mean(jnp.square(of), axis=-1, keepdims=True) + RMS_EPS)
    of = of * g.reshape(1, H, 1, d).astype(jnp.float32)
    return of.transpose(0, 2, 1, 3).reshape(B, S, H * d).astype(o.dtype)


def stick_breaking_attention(q, k, v):
    B, H, S, d = q.shape
    nb = S // BLOCK_Q
    scale = d ** -0.5
    qb = q.reshape(B, H, nb, BLOCK_Q, d).transpose(2, 0, 1, 3, 4)
    k_pos = jnp.arange(S)

    def block(args):
        qi, i = args
        z = jnp.einsum('bhqd,bhkd->bhqk', qi, k, preferred_element_type=jnp.float32) * scale
        q_pos = i * BLOCK_Q + jnp.arange(BLOCK_Q)
        mask = k_pos[None, :] < q_pos[:, None]
        log_keep = jnp.where(mask, jax.nn.log_sigmoid(-z), 0.0)
        log_after = lax.cumsum(log_keep, axis=3, reverse=True) - log_keep
        w = jnp.where(mask, jnp.exp(jax.nn.log_sigmoid(z) + log_after), 0.0)
        return jnp.einsum('bhqk,bhkd->bhqd', w.astype(v.dtype), v,
                          preferred_element_type=jnp.float32).astype(v.dtype)

    o = lax.map(block, (qb, jnp.arange(nb)))
    return o.transpose(1, 2, 0, 3, 4).reshape(B, H, S, d)


def forgetting_attention(q, k, v, c):
    B, H, S, d = q.shape
    nb = S // BLOCK_Q
    scale = d ** -0.5
    qb = q.reshape(B, H, nb, BLOCK_Q, d).transpose(2, 0, 1, 3, 4)
    cb = c.reshape(B, H, nb, BLOCK_Q).transpose(2, 0, 1, 3)
    k_pos = jnp.arange(S)

    def block(args):
        qi, ci, i = args
        z = jnp.einsum('bhqd,bhkd->bhqk', qi, k, preferred_element_type=jnp.float32) * scale
        z = z + ci[..., :, None] - c[:, :, None, :]
        q_pos = i * BLOCK_Q + jnp.arange(BLOCK_Q)
        mask = k_pos[None, :] <= q_pos[:, None]
        p = jax.nn.softmax(jnp.where(mask, z, -jnp.inf), axis=-1)
        return jnp.einsum('bhqk,bhkd->bhqd', p.astype(v.dtype), v,
                          preferred_element_type=jnp.float32).astype(v.dtype)

    o = lax.map(block, (qb, cb, jnp.arange(nb)))
    return o.transpose(1, 2, 0, 3, 4).reshape(B, H, S, d)


def hybrid_mixer(x, w_in, b_f, g_sb, g_fox, w_out):
    B, S, _ = x.shape
    proj = jnp.einsum('bsd,de->bse', x, w_in)
    splits = [int(s) for s in np.cumsum([SB_WIDTH] * 3 + [FOX_WIDTH] * 3)]
    q_sb, k_sb, v_sb, q_fx, k_fx, v_fx, f_logit = jnp.split(proj, splits, axis=-1)

    def heads(t, h):
        return t.reshape(B, S, h, HEAD_DIM).transpose(0, 2, 1, 3)

    o_sb = stick_breaking_attention(heads(q_sb, N_SB_HEADS), heads(k_sb, N_SB_HEADS),
                                    heads(v_sb, N_SB_HEADS))
    log_f = jax.nn.log_sigmoid((f_logit + b_f).astype(jnp.float32))
    c = jnp.cumsum(log_f, axis=1).transpose(0, 2, 1)
    o_fx = forgetting_attention(heads(q_fx, N_FOX_HEADS), heads(k_fx, N_FOX_HEADS),
                                heads(v_fx, N_FOX_HEADS), c)
    o = jnp.concatenate([head_rmsnorm(o_sb, g_sb), head_rmsnorm(o_fx, g_fox)], axis=-1)
    return jnp.einsum('bse,ed->bsd', o, w_out)


def swiglu(h, w_gate_up, w_down):
    gu = jnp.einsum('bsd,df->bsf', h, w_gate_up)
    gate, up = jnp.split(gu, 2, axis=-1)
    return jnp.einsum('bsf,fd->bsd', jax.nn.silu(gate) * up, w_down)


def _fwd_setup_inputs(seed: int = 0) -> dict:
    key = jax.random.key(seed)
    ks = jax.random.split(key, 12)
    f32 = jnp.float32
    x = jax.random.normal(ks[0], (BATCH, SEQ, D_MODEL), f32)
    col_scale = np.ones((IN_WIDTH,), np.float32)
    col_scale[2 * SB_WIDTH:3 * SB_WIDTH] = DEEPNORM_BETA
    col_scale[3 * SB_WIDTH + 2 * FOX_WIDTH:3 * SB_WIDTH + 3 * FOX_WIDTH] = DEEPNORM_BETA
    w_in = (jax.random.normal(ks[1], (DEPTH, D_MODEL, IN_WIDTH), f32)
            * (D_MODEL ** -0.5) * jnp.asarray(col_scale))
    b_f = jax.random.uniform(ks[2], (DEPTH, N_FOX_HEADS), f32, 1.0, 4.0)
    g_sb = 1.0 + 0.02 * jax.random.normal(ks[3], (DEPTH, SB_WIDTH), f32)
    g_fox = 1.0 + 0.02 * jax.random.normal(ks[4], (DEPTH, FOX_WIDTH), f32)
    w_out = jax.random.normal(ks[5], (DEPTH, D_MODEL, D_MODEL), f32) * (D_MODEL ** -0.5) * DEEPNORM_BETA
    ln1_g = 1.0 + 0.02 * jax.random.normal(ks[6], (DEPTH, D_MODEL), f32)
    ln1_b = 0.02 * jax.random.normal(ks[7], (DEPTH, D_MODEL), f32)
    ln2_g = 1.0 + 0.02 * jax.random.normal(ks[8], (DEPTH, D_MODEL), f32)
    ln2_b = 0.02 * jax.random.normal(ks[9], (DEPTH, D_MODEL), f32)
    w_gate_up = jax.random.normal(ks[10], (DEPTH, D_MODEL, 2 * D_FF), f32) * (D_MODEL ** -0.5) * DEEPNORM_BETA
    w_down = jax.random.normal(ks[11], (DEPTH, D_FF, D_MODEL), f32) * (D_FF ** -0.5) * DEEPNORM_BETA
    return {"x": x, "w_in": w_in, "b_f": b_f, "g_sb": g_sb, "g_fox": g_fox, "w_out": w_out,
            "ln1_g": ln1_g, "ln1_b": ln1_b, "ln2_g": ln2_g, "ln2_b": ln2_b,
            "w_gate_up": w_gate_up, "w_down": w_down}


def _fwd_reference(x, w_in, b_f, g_sb, g_fox, w_out, ln1_g, ln1_b, ln2_g, ln2_b, w_gate_up, w_down):
    h = x
    for l in range(DEPTH):
        mix = hybrid_mixer(h, w_in[l], b_f[l], g_sb[l], g_fox[l], w_out[l])
        h = layer_norm(DEEPNORM_ALPHA * h + mix, ln1_g[l], ln1_b[l])
        ff = swiglu(h, w_gate_up[l], w_down[l])
        h = layer_norm(DEEPNORM_ALPHA * h + ff, ln2_g[l], ln2_b[l])
    return h


import jax as _jax
import jax.numpy as _jnp

TWIN_FORMAT = 'train_step'
FWD_PARAMS = ['x', 'w_in', 'b_f', 'g_sb', 'g_fox', 'w_out', 'ln1_g', 'ln1_b', 'ln2_g', 'ln2_b', 'w_gate_up', 'w_down']
TWIN_WEIGHTS = ['w_in', 'b_f', 'g_sb', 'g_fox', 'w_out', 'ln1_g', 'ln1_b', 'ln2_g', 'ln2_b', 'w_gate_up', 'w_down']
TWIN_DIFF_INPUT = 'x'
TWIN_INPUTS = ['x', 'w_in', 'b_f', 'g_sb', 'g_fox', 'w_out', 'ln1_g', 'ln1_b', 'ln2_g', 'ln2_b', 'w_gate_up', 'w_down', 'loss_target', 'm_w_in', 'm_b_f', 'm_g_sb', 'm_g_fox', 'm_w_out', 'm_ln1_g', 'm_ln1_b', 'm_ln2_g', 'm_ln2_b', 'm_w_gate_up', 'm_w_down', 'v_w_in', 'v_b_f', 'v_g_sb', 'v_g_fox', 'v_w_out', 'v_ln1_g', 'v_ln1_b', 'v_ln2_g', 'v_ln2_b', 'v_w_gate_up', 'v_w_down']
TWIN_OUTPUTS = ['loss', 'grad_x', 'grad_w_in', 'grad_b_f', 'grad_g_sb', 'grad_g_fox', 'grad_w_out', 'grad_ln1_g', 'grad_ln1_b', 'grad_ln2_g', 'grad_ln2_b', 'grad_w_gate_up', 'grad_w_down', 'delta_w_in', 'delta_b_f', 'delta_g_sb', 'delta_g_fox', 'delta_w_out', 'delta_ln1_g', 'delta_ln1_b', 'delta_ln2_g', 'delta_ln2_b', 'delta_w_gate_up', 'delta_w_down', 'new_m_w_in', 'new_m_b_f', 'new_m_g_sb', 'new_m_g_fox', 'new_m_w_out', 'new_m_ln1_g', 'new_m_ln1_b', 'new_m_ln2_g', 'new_m_ln2_b', 'new_m_w_gate_up', 'new_m_w_down', 'new_v_w_in', 'new_v_b_f', 'new_v_g_sb', 'new_v_g_fox', 'new_v_w_out', 'new_v_ln1_g', 'new_v_ln1_b', 'new_v_ln2_g', 'new_v_ln2_b', 'new_v_w_gate_up', 'new_v_w_down']
TWIN_LEAF_KINDS = {'loss': 'loss', 'grad_x': 'grad_x', 'grad_w_in': 'grad_w', 'grad_b_f': 'grad_w', 'grad_g_sb': 'grad_w', 'grad_g_fox': 'grad_w', 'grad_w_out': 'grad_w', 'grad_ln1_g': 'grad_w', 'grad_ln1_b': 'grad_w', 'grad_ln2_g': 'grad_w', 'grad_ln2_b': 'grad_w', 'grad_w_gate_up': 'grad_w', 'grad_w_down': 'grad_w', 'delta_w_in': 'delta_w', 'delta_b_f': 'delta_w', 'delta_g_sb': 'delta_w', 'delta_g_fox': 'delta_w', 'delta_w_out': 'delta_w', 'delta_ln1_g': 'delta_w', 'delta_ln1_b': 'delta_w', 'delta_ln2_g': 'delta_w', 'delta_ln2_b': 'delta_w', 'delta_w_gate_up': 'delta_w', 'delta_w_down': 'delta_w', 'new_m_w_in': 'new_m', 'new_m_b_f': 'new_m', 'new_m_g_sb': 'new_m', 'new_m_g_fox': 'new_m', 'new_m_w_out': 'new_m', 'new_m_ln1_g': 'new_m', 'new_m_ln1_b': 'new_m', 'new_m_ln2_g': 'new_m', 'new_m_ln2_b': 'new_m', 'new_m_w_gate_up': 'new_m', 'new_m_w_down': 'new_m', 'new_v_w_in': 'new_v', 'new_v_b_f': 'new_v', 'new_v_g_sb': 'new_v', 'new_v_g_fox': 'new_v', 'new_v_w_out': 'new_v', 'new_v_ln1_g': 'new_v', 'new_v_ln1_b': 'new_v', 'new_v_ln2_g': 'new_v', 'new_v_ln2_b': 'new_v', 'new_v_w_gate_up': 'new_v', 'new_v_w_down': 'new_v'}


def _forward(args):
    return _fwd_reference(*[args[k] for k in FWD_PARAMS])


def _output_shape():
    def fwd():
        inp = _fwd_setup_inputs(0)
        return _fwd_reference(*[inp[k] for k in FWD_PARAMS])
    out = _jax.eval_shape(fwd)
    return out.shape, out.dtype

N_MICROBATCH = 1
ADAM_LR = 0.001
ADAM_B1 = 0.9
ADAM_B2 = 0.999
ADAM_EPS = 1e-08
ADAM_WD = 0.01
ADAM_STEP = 10
PER_EXAMPLE_BATCH_AXIS = {'x': 0, 'loss_target': 0}
SHARED_INPUTS = []
_WEIGHT_DTYPES = {'w_in': _jnp.float32, 'b_f': _jnp.float32, 'g_sb': _jnp.float32, 'g_fox': _jnp.float32, 'w_out': _jnp.float32, 'ln1_g': _jnp.float32, 'ln1_b': _jnp.float32, 'ln2_g': _jnp.float32, 'ln2_b': _jnp.float32, 'w_gate_up': _jnp.float32, 'w_down': _jnp.float32}
MOMENT_SCALE = {'w_in': 1.793379e-01, 'b_f': 1.183624e+00, 'g_sb': 1.595760e-01, 'g_fox': 1.608634e-01, 'w_out': 2.693380e-01, 'ln1_g': 4.087270e+00, 'ln1_b': 1.883208e+00, 'ln2_g': 1.280871e+02, 'ln2_b': 2.240917e+00, 'w_gate_up': 3.520833e-02, 'w_down': 5.766219e-02}


def _to_microbatches(a, axis):
    t = _jnp.moveaxis(a, axis, 0)
    t = t.reshape((N_MICROBATCH, t.shape[0] // N_MICROBATCH) + t.shape[1:])
    return _jnp.moveaxis(t, 1, axis + 1)


def setup_inputs(seed: int = 0) -> dict:
    inp = _fwd_setup_inputs(seed)
    key = _jax.random.fold_in(_jax.random.key(seed), 7919)
    shape, _ = _output_shape()
    out = dict(inp)
    out["loss_target"] = _jax.random.normal(_jax.random.fold_in(key, 0), shape, _jnp.float32)
    for i, name in enumerate(TWIN_WEIGHTS):
        w = inp[name].astype(_jnp.float32)
        if MOMENT_SCALE is None:
            s = _jnp.sqrt(_jnp.mean(_jnp.square(w)) + 1e-30)
        else:
            s = MOMENT_SCALE[name]
        km, kv = _jax.random.split(_jax.random.fold_in(key, i + 1))
        out[name] = w
        out["m_" + name] = s * _jax.random.normal(km, w.shape, _jnp.float32)
        out["v_" + name] = (s * s) * _jax.random.uniform(kv, w.shape, _jnp.float32, 0.5, 1.5)
    if N_MICROBATCH > 1:
        for name, axis in PER_EXAMPLE_BATCH_AXIS.items():
            out[name] = _to_microbatches(out[name], axis)
    return {'x': out['x'], 'w_in': out['w_in'], 'b_f': out['b_f'], 'g_sb': out['g_sb'], 'g_fox': out['g_fox'], 'w_out': out['w_out'], 'ln1_g': out['ln1_g'], 'ln1_b': out['ln1_b'], 'ln2_g': out['ln2_g'], 'ln2_b': out['ln2_b'], 'w_gate_up': out['w_gate_up'], 'w_down': out['w_down'], 'loss_target': out['loss_target'], 'm_w_in': out['m_w_in'], 'm_b_f': out['m_b_f'], 'm_g_sb': out['m_g_sb'], 'm_g_fox': out['m_g_fox'], 'm_w_out': out['m_w_out'], 'm_ln1_g': out['m_ln1_g'], 'm_ln1_b': out['m_ln1_b'], 'm_ln2_g': out['m_ln2_g'], 'm_ln2_b': out['m_ln2_b'], 'm_w_gate_up': out['m_w_gate_up'], 'm_w_down': out['m_w_down'], 'v_w_in': out['v_w_in'], 'v_b_f': out['v_b_f'], 'v_g_sb': out['v_g_sb'], 'v_g_fox': out['v_g_fox'], 'v_w_out': out['v_w_out'], 'v_ln1_g': out['v_ln1_g'], 'v_ln1_b': out['v_ln1_b'], 'v_ln2_g': out['v_ln2_g'], 'v_ln2_b': out['v_ln2_b'], 'v_w_gate_up': out['v_w_gate_up'], 'v_w_down': out['v_w_down']}


def _loss(weights, diff, rest, loss_target):
    with _jax.named_scope("forward"):
        args = {**rest, TWIN_DIFF_INPUT: diff, **{k: w.astype(_WEIGHT_DTYPES[k]) for k, w in weights.items()}}
        y = _forward(args)
    with _jax.named_scope("loss_head"):
        err = _jnp.square(y.astype(_jnp.float32) - loss_target)
        return 0.5 * _jnp.sum(_jnp.mean(err, axis=-1)) if err.ndim else 0.5 * err


def _adamw(w, g, m, v):
    m = ADAM_B1 * m + (1.0 - ADAM_B1) * g
    v = ADAM_B2 * v + (1.0 - ADAM_B2) * _jnp.square(g)
    m_hat = m / (1.0 - ADAM_B1 ** ADAM_STEP)
    v_hat = v / (1.0 - ADAM_B2 ** ADAM_STEP)
    delta = -ADAM_LR * (m_hat / (_jnp.sqrt(v_hat) + ADAM_EPS) + ADAM_WD * w)
    return delta, m, v


def reference(x, w_in, b_f, g_sb, g_fox, w_out, ln1_g, ln1_b, ln2_g, ln2_b, w_gate_up, w_down, loss_target, m_w_in, m_b_f, m_g_sb, m_g_fox, m_w_out, m_ln1_g, m_ln1_b, m_ln2_g, m_ln2_b, m_w_gate_up, m_w_down, v_w_in, v_b_f, v_g_sb, v_g_fox, v_w_out, v_ln1_g, v_ln1_b, v_ln2_g, v_ln2_b, v_w_gate_up, v_w_down):
    given = dict(x=x, w_in=w_in, b_f=b_f, g_sb=g_sb, g_fox=g_fox, w_out=w_out, ln1_g=ln1_g, ln1_b=ln1_b, ln2_g=ln2_g, ln2_b=ln2_b, w_gate_up=w_gate_up, w_down=w_down, loss_target=loss_target, m_w_in=m_w_in, m_b_f=m_b_f, m_g_sb=m_g_sb, m_g_fox=m_g_fox, m_w_out=m_w_out, m_ln1_g=m_ln1_g, m_ln1_b=m_ln1_b, m_ln2_g=m_ln2_g, m_ln2_b=m_ln2_b, m_w_gate_up=m_w_gate_up, m_w_down=m_w_down, v_w_in=v_w_in, v_b_f=v_b_f, v_g_sb=v_g_sb, v_g_fox=v_g_fox, v_w_out=v_w_out, v_ln1_g=v_ln1_g, v_ln1_b=v_ln1_b, v_ln2_g=v_ln2_g, v_ln2_b=v_ln2_b, v_w_gate_up=v_w_gate_up, v_w_down=v_w_down)
    weights = {n: given[n] for n in TWIN_WEIGHTS}
    shared = {n: given[n] for n in SHARED_INPUTS}
    per_example = {n: given[n] for n in ['x']}
    grad_fn = _jax.value_and_grad(_loss, argnums=(0, 1))

    def one_microbatch(ex, loss_target):
        ex = dict(ex)
        diff = ex.pop(TWIN_DIFF_INPUT)
        return grad_fn(weights, diff, {**shared, **ex}, loss_target)

    if N_MICROBATCH == 1:
        loss, (grad_w, grad_x) = one_microbatch(per_example, given["loss_target"])
    else:
        def body(carry, xs):
            loss_sum, grad_sum = carry
            l_k, (gw_k, gx_k) = one_microbatch(xs[0], xs[1])
            with _jax.named_scope("update"):
                return (loss_sum + l_k, _jax.tree.map(_jnp.add, grad_sum, gw_k)), gx_k

        init = (_jnp.zeros((), _jnp.float32), _jax.tree.map(_jnp.zeros_like, weights))
        (loss, grad_w), grad_x = _jax.lax.scan(body, init, (per_example, given["loss_target"]))
    with _jax.named_scope("update"):
        delta_w, new_m, new_v = {}, {}, {}
        for n in TWIN_WEIGHTS:
            delta_w[n], new_m[n], new_v[n] = _adamw(weights[n], grad_w[n], given["m_" + n], given["v_" + n])
    return (loss, grad_x, *[grad_w[n] for n in TWIN_WEIGHTS], *[delta_w[n] for n in TWIN_WEIGHTS],
            *[new_m[n] for n in TWIN_WEIGHTS], *[new_v[n] for n in TWIN_WEIGHTS])
```

```python
import functools

import numpy as np
import jax
import jax.numpy as jnp
from jax import lax
from jax.experimental import pallas as pl
from jax.experimental.pallas import tpu as pltpu

F32 = jnp.float32
BF16 = jnp.bfloat16

D_MODEL = 1024
HEAD_DIM = 64
LANES = 128
N_PAIRS = 4
GROUP_W = 512
QKV_W = 3072
D_FF = 2816
N_FOX = 8
ALPHA = 2.0 ** 0.25
LN_EPS = 1e-5
RMS_EPS = 1e-6
SCALE = HEAD_DIM ** -0.5
NEG_BIG = -1e30
ADAM_LR, ADAM_B1, ADAM_B2, ADAM_EPS, ADAM_WD, ADAM_STEP = 0.001, 0.9, 0.999, 1e-08, 0.01, 10
VMEM_BIG = 56 * 1024 * 1024
MESH = pl.DeviceIdType.MESH

_NN = (((1,), (0,)), ((), ()))
_NT = (((1,), (1,)), ((), ()))
_TN = (((0,), (0,)), ((), ()))


def _dot(a, b, dims=_NN):
    return lax.dot_general(a, b, dims, preferred_element_type=F32)


def _split_dot(x, t):
    hi = x.astype(BF16)
    lo = (x - hi.astype(F32)).astype(BF16)
    return _dot(hi, t) + _dot(lo, t)


def _softplus(z):
    return jnp.maximum(z, 0.0) + jnp.log1p(jnp.exp(-jnp.abs(z)))


def _col(v, h):
    lane = lax.broadcasted_iota(jnp.int32, v.shape, 1)
    return jnp.sum(jnp.where(lane == h, v, 0.0), axis=1, keepdims=True)


def _two_sum(hi, lo, b):
    s = hi + b
    bb = s - hi
    err = (hi - (s - bb)) + (b - bb)
    return s, lo + err


def _params(vmem=None):
    return pltpu.CompilerParams(vmem_limit_bytes=vmem) if vmem else None


def _matmul(a, b, *, mode, name, tm, tn, tk, outs, extras=(), epilogue=None, vmem=None):
    if mode == "nn":
        (M, K), (_, N) = a.shape, b.shape
    elif mode == "nt":
        (M, K), (N, _) = a.shape, b.shape
    else:
        (K, M), (_, N) = a.shape, b.shape
    tm, tn, tk = min(tm, M), min(tn, N), min(tk, K)
    assert M % tm == 0 and N % tn == 0 and K % tk == 0, (name, M, N, K, tm, tn, tk)
    nk = K // tk
    dims = {"nn": _NN, "nt": _NT, "tn": _TN}[mode]
    if mode == "tn":
        a_spec = pl.BlockSpec((tk, tm), lambda i, j, k: (k, i))
    else:
        a_spec = pl.BlockSpec((tm, tk), lambda i, j, k: (i, k))
    if mode == "nt":
        b_spec = pl.BlockSpec((tn, tk), lambda i, j, k: (j, k))
    else:
        b_spec = pl.BlockSpec((tk, tn), lambda i, j, k: (k, j))
    ex_specs = [pl.BlockSpec(bs, (lambda i, j, k, f=f: f(i, j))) for (_, bs, f) in extras]
    ne, no = len(extras), len(outs)
    if epilogue is None:
        epilogue = lambda acc: (acc,)

    def body(a_ref, b_ref, *rest):
        ex_refs, out_refs, acc = rest[:ne], rest[ne:ne + no], rest[-1]
        k = pl.program_id(2)

        @pl.when(k == 0)
        def _():
            acc[...] = jnp.zeros_like(acc)

        acc[...] += _dot(a_ref[...].astype(BF16), b_ref[...].astype(BF16), dims)

        @pl.when(k == nk - 1)
        def _():
            res = epilogue(acc[...], *[e[...] for e in ex_refs])
            for r, o in zip(res, out_refs):
                o[...] = r.astype(o.dtype)

    res = pl.pallas_call(
        body, name=name, grid=(M // tm, N // tn, nk),
        in_specs=[a_spec, b_spec] + ex_specs,
        out_specs=[pl.BlockSpec((tm, tn), lambda i, j, k: (i, j)) for _ in outs],
        out_shape=[jax.ShapeDtypeStruct((M, N), d) for d in outs],
        scratch_shapes=[pltpu.VMEM((tm, tn), F32)],
        compiler_params=_params(vmem),
    )(a, b, *[e[0] for e in extras])
    return res[0] if no == 1 else res


def _tile_ij(i, j):
    return (i, j)


def _rowwise(fn, name, rows, tm, ins, outs, vmem=None):
    tm = min(tm, rows)
    assert rows % tm == 0

    def spec(shape, kind):
        if kind == "t":
            return pl.BlockSpec((tm,) + tuple(shape[1:]), lambda i: (i,) + (0,) * (len(shape) - 1))
        return pl.BlockSpec(tuple(shape), lambda i: (0,) * len(shape))

    def body(*refs):
        fn(pl.program_id(0), *refs)

    return pl.pallas_call(
        body, name=name, grid=(rows // tm,),
        in_specs=[spec(a.shape, k) for a, k in ins],
        out_specs=[spec(s, k) for s, _, k in outs],
        out_shape=[jax.ShapeDtypeStruct(s, d) for s, d, _ in outs],
        compiler_params=_params(vmem),
    )(*[a for a, _ in ins])


def _ln_stats(u):
    mu = jnp.mean(u, axis=-1, keepdims=True)
    d = u - mu
    var = jnp.mean(d * d, axis=-1, keepdims=True)
    r = lax.rsqrt(var + LN_EPS)
    return d * r, r


def _ln_bwd(dh, xh, r, g):
    dxh = dh * g
    m1 = jnp.mean(dxh, axis=-1, keepdims=True)
    m2 = jnp.mean(dxh * xh, axis=-1, keepdims=True)
    return r * (dxh - m1 - xh * m2)


def _acc_rows(i, ref, rows):
    @pl.when(i == 0)
    def _():
        ref[...] = jnp.zeros_like(ref)
    for r, v in rows.items():
        ref[pl.ds(r, 1), :] += v


def _head_sums(v, bd):
    return _split_dot(v, bd)


def _fgate_fwd(x, wft, bf_col, tm):
    S = x.shape[0]
    tm = min(tm, S)

    def body(wft_ref, bf_ref, x_ref, lf_ref):
        f = _dot(wft_ref[...], x_ref[...].astype(BF16), _NT) + bf_ref[...]
        lf_ref[...] = -_softplus(-f)

    return pl.pallas_call(
        body, name="fgate_fwd", grid=(S // tm,),
        in_specs=[pl.BlockSpec((N_FOX, D_MODEL), lambda i: (0, 0)), pl.BlockSpec((N_FOX, 1), lambda i: (0, 0)),
                  pl.BlockSpec((tm, D_MODEL), lambda i: (i, 0))],
        out_specs=pl.BlockSpec((N_FOX, tm), lambda i: (0, i)),
        out_shape=jax.ShapeDtypeStruct((N_FOX, S), F32),
    )(wft, bf_col, x)


def _chunk_scan(v, reverse):
    lane = lax.broadcasted_iota(jnp.int32, v.shape, 1)
    sh = 1
    while sh < LANES:
        if reverse:
            v = v + jnp.where(lane < LANES - sh, pltpu.roll(v, LANES - sh, 1), 0.0)
        else:
            v = v + jnp.where(lane >= sh, pltpu.roll(v, sh, 1), 0.0)
        sh *= 2
    return v


def _cumsum_fwd(lf):
    n, S = lf.shape
    nc = S // LANES

    def body(lf_ref, c_ref):
        def step(ci, carry):
            sl = pl.ds(pl.multiple_of(ci * LANES, LANES), LANES)
            v = _chunk_scan(lf_ref[:, sl], False) + carry
            c_ref[:, sl] = v
            return _col(v, LANES - 1)
        lax.fori_loop(0, nc, step, jnp.zeros((n, 1), F32))

    return pl.pallas_call(body, name="cumsum_fwd", out_shape=jax.ShapeDtypeStruct((n, S), F32))(lf)


def _fgate_bwd(dc, lf):
    n, S = dc.shape
    nc = S // LANES

    def body(dc_ref, lf_ref, dfl_ref, dbf_ref):
        def step(t, carry):
            car, tot = carry
            ci = nc - 1 - t
            sl = pl.ds(pl.multiple_of(ci * LANES, LANES), LANES)
            dlf = _chunk_scan(dc_ref[:, sl], True) + car
            dfl = dlf * (1.0 - jnp.exp(lf_ref[:, sl]))
            dfl_ref[:, sl] = dfl
            return _col(dlf, 0), tot + jnp.sum(dfl, axis=1, keepdims=True)
        _, tot = lax.fori_loop(0, nc, step, (jnp.zeros((n, 1), F32), jnp.zeros((n, 1), F32)))
        dbf_ref[...] = tot

    return pl.pallas_call(body, name="fgate_bwd",
                          out_shape=[jax.ShapeDtypeStruct((n, S), F32), jax.ShapeDtypeStruct((n, 1), F32)])(dc, lf)


def _causal_tables(nq, descending):
    qi, kj = [], []
    for i in range(nq):
        js = range(i, -1, -1) if descending else range(i + 1)
        for j in js:
            qi.append(i)
            kj.append(j)
    return jnp.asarray(np.array(qi, np.int32)), jnp.asarray(np.array(kj, np.int32))


def _tri_matrices(b):
    r = np.arange(b)
    tfwd = (r[:, None] <= r[None, :]).astype(np.float32)
    return jnp.asarray(tfwd, BF16), jnp.asarray(tfwd.T, BF16)


def _masked_pair(v, lane_is_a):
    v = v.astype(F32)
    return jnp.where(lane_is_a, v, 0.0).astype(BF16), jnp.where(lane_is_a, 0.0, v).astype(BF16)


def _sb_fwd(proj, col0, bq):
    S = proj.shape[0]
    bq = min(bq, S)
    nq = S // bq
    qi, kj = _causal_tables(nq, True)
    _, trev = _tri_matrices(bq)

    def body(qi_ref, kj_ref, q_ref, k_ref, v_ref, trev_ref, o_ref, st_ref, acc_a, acc_b, qa, qb, rs):
        t = pl.program_id(1)
        i, j = qi_ref[t], kj_ref[t]
        is_a = lax.broadcasted_iota(jnp.int32, (bq, LANES), 1) < HEAD_DIM

        @pl.when(j == i)
        def _():
            acc_a[...] = jnp.zeros_like(acc_a)
            acc_b[...] = jnp.zeros_like(acc_b)
            rs[...] = jnp.zeros_like(rs)
            qa[...], qb[...] = _masked_pair(q_ref[...], is_a)

        def tile(masked):
            k, v, trev_m = k_ref[...], v_ref[...], trev_ref[...]
            if masked:
                tri = lax.broadcasted_iota(jnp.int32, (bq, bq), 0) > lax.broadcasted_iota(jnp.int32, (bq, bq), 1)
            for h, (qh, acc) in enumerate(((qa, acc_a), (qb, acc_b))):
                z = _dot(qh[...], k, _NT) * SCALE
                lk = -_softplus(z)
                if masked:
                    lk = jnp.where(tri, lk, 0.0)
                r_hi, r_lo = rs[2 * h], rs[2 * h + 1]
                w = jnp.exp(z + _split_dot(lk, trev_m) + (r_hi + r_lo))
                if masked:
                    w = jnp.where(tri, w, 0.0)
                acc[...] += _dot(w.astype(BF16), v)
                rs[2 * h], rs[2 * h + 1] = _two_sum(r_hi, r_lo, jnp.sum(lk, axis=1, keepdims=True))

        pl.when(j == i)(functools.partial(tile, True))
        pl.when(j < i)(functools.partial(tile, False))

        @pl.when(j == 0)
        def _():
            o_ref[...] = jnp.where(is_a, acc_a[...], acc_b[...])
            lane8 = lax.broadcasted_iota(jnp.int32, (bq, 8), 1)
            st = jnp.zeros((bq, 8), F32)
            for c, src in enumerate((0, 2, 1, 3)):
                st = jnp.where(lane8 == c, rs[src], st)
            st_ref[0] = st

    grid_spec = pltpu.PrefetchScalarGridSpec(
        num_scalar_prefetch=2, grid=(N_PAIRS, int(qi.shape[0])),
        in_specs=[pl.BlockSpec((bq, LANES), lambda p, t, qi, kj: (qi[t], col0 + p)),
                  pl.BlockSpec((bq, LANES), lambda p, t, qi, kj: (kj[t], col0 + 4 + p)),
                  pl.BlockSpec((bq, LANES), lambda p, t, qi, kj: (kj[t], col0 + 8 + p)),
                  pl.BlockSpec((bq, bq), lambda p, t, qi, kj: (0, 0))],
        out_specs=[pl.BlockSpec((bq, LANES), lambda p, t, qi, kj: (qi[t], p)),
                   pl.BlockSpec((1, bq, 8), lambda p, t, qi, kj: (p, qi[t], 0))],
        scratch_shapes=[pltpu.VMEM((bq, LANES), F32), pltpu.VMEM((bq, LANES), F32),
                        pltpu.VMEM((bq, LANES), BF16), pltpu.VMEM((bq, LANES), BF16),
                        pltpu.VMEM((4, bq, 1), F32)])
    return pl.pallas_call(
        body, name="sb_fwd", grid_spec=grid_spec,
        out_shape=[jax.ShapeDtypeStruct((S, GROUP_W), F32), jax.ShapeDtypeStruct((N_PAIRS, S, 8), F32)],
    )(qi, kj, proj, proj, proj, trev)


def _sb_bwd(proj, col0, do, st, bq):
    S = proj.shape[0]
    bq = min(bq, S)
    nq = S // bq
    qi, kj = _causal_tables(nq, False)
    tfwd, trev = _tri_matrices(bq)

    def body(qi_ref, kj_ref, q_ref, k_ref, v_ref, do_ref, st_ref, tfwd_ref, trev_ref,
             dq_ref, dk_ref, dv_ref, dq_a, dq_b, qa, qb, doa, dob, rs):
        t = pl.program_id(1)
        i, j = qi_ref[t], kj_ref[t]
        is_a = lax.broadcasted_iota(jnp.int32, (bq, LANES), 1) < HEAD_DIM

        @pl.when(t == 0)
        def _():
            dk_ref[...] = jnp.zeros_like(dk_ref)
            dv_ref[...] = jnp.zeros_like(dv_ref)

        @pl.when(j == 0)
        def _():
            dq_a[...] = jnp.zeros_like(dq_a)
            dq_b[...] = jnp.zeros_like(dq_b)
            rs[...] = jnp.zeros_like(rs)
            qa[...], qb[...] = _masked_pair(q_ref[...], is_a)
            doa[...], dob[...] = _masked_pair(do_ref[...], is_a)

        def tile(masked):
            k, v = k_ref[...], v_ref[...]
            tfwd_m, trev_m = tfwd_ref[...], trev_ref[...]
            st_v = st_ref[0]
            if masked:
                tri = lax.broadcasted_iota(jnp.int32, (bq, bq), 0) > lax.broadcasted_iota(jnp.int32, (bq, bq), 1)
            dzs, ws = [], []
            for h, (qh, doh, dq) in enumerate(((qa, doa, dq_a), (qb, dob, dq_b))):
                z = _dot(qh[...], k, _NT) * SCALE
                lk = -_softplus(z)
                if masked:
                    lk = jnp.where(tri, lk, 0.0)
                p_hi, p_lo = _two_sum(rs[3 * h], rs[3 * h + 1], jnp.sum(lk, axis=1, keepdims=True))
                rs[3 * h], rs[3 * h + 1] = p_hi, p_lo
                right = (_col(st_v, h) - p_hi) + (_col(st_v, 2 + h) - p_lo)
                w = jnp.exp(z + _split_dot(lk, trev_m) + right)
                if masked:
                    w = jnp.where(tri, w, 0.0)
                g = _dot(doh[...], v, _NT) * w
                g_left = rs[3 * h + 2]
                dz = g - jnp.exp(z + lk) * (_split_dot(g, tfwd_m) + g_left)
                if masked:
                    dz = jnp.where(tri, dz, 0.0)
                rs[3 * h + 2] = g_left + jnp.sum(g, axis=1, keepdims=True)
                dzb = (dz * SCALE).astype(BF16)
                dq[...] += _dot(dzb, k)
                dzs.append(dzb)
                ws.append(w.astype(BF16))
            rows = pl.ds(pl.multiple_of(j * bq, bq), bq)
            dk_ref[rows, :] += _dot(dzs[0], qa[...], _TN) + _dot(dzs[1], qb[...], _TN)
            dv_ref[rows, :] += _dot(ws[0], doa[...], _TN) + _dot(ws[1], dob[...], _TN)

        pl.when(j == i)(functools.partial(tile, True))
        pl.when(j < i)(functools.partial(tile, False))

        @pl.when(j == i)
        def _():
            dq_ref[...] = jnp.where(is_a, dq_a[...], dq_b[...])

    grid_spec = pltpu.PrefetchScalarGridSpec(
        num_scalar_prefetch=2, grid=(N_PAIRS, int(qi.shape[0])),
        in_specs=[pl.BlockSpec((bq, LANES), lambda p, t, qi, kj: (qi[t], col0 + p)),
                  pl.BlockSpec((bq, LANES), lambda p, t, qi, kj: (kj[t], col0 + 4 + p)),
                  pl.BlockSpec((bq, LANES), lambda p, t, qi, kj: (kj[t], col0 + 8 + p)),
                  pl.BlockSpec((bq, LANES), lambda p, t, qi, kj: (qi[t], p)),
                  pl.BlockSpec((1, bq, 8), lambda p, t, qi, kj: (p, qi[t], 0)),
                  pl.BlockSpec((bq, bq), lambda p, t, qi, kj: (0, 0)),
                  pl.BlockSpec((bq, bq), lambda p, t, qi, kj: (0, 0))],
        out_specs=[pl.BlockSpec((bq, LANES), lambda p, t, qi, kj: (qi[t], p)),
                   pl.BlockSpec((S, LANES), lambda p, t, qi, kj: (0, p)),
                   pl.BlockSpec((S, LANES), lambda p, t, qi, kj: (0, p))],
        scratch_shapes=[pltpu.VMEM((bq, LANES), F32), pltpu.VMEM((bq, LANES), F32)]
        + [pltpu.VMEM((bq, LANES), BF16)] * 4 + [pltpu.VMEM((6, bq, 1), F32)])
    return pl.pallas_call(
        body, name="sb_bwd", grid_spec=grid_spec,
        out_shape=[jax.ShapeDtypeStruct((S, GROUP_W), F32)] * 3,
        compiler_params=_params(VMEM_BIG),
    )(qi, kj, proj, proj, proj, do, st, tfwd, trev)


def _fox_fwd(proj, col0, c_col, c_row, bq):
    S = proj.shape[0]
    bq = min(bq, S)
    nq = S // bq
    qi, kj = _causal_tables(nq, False)

    def body(qi_ref, kj_ref, q_ref, k_ref, v_ref, cc_ref, cr_ref, o_ref, st_ref, acc_a, acc_b, qa, qb, ml):
        t = pl.program_id(1)
        i, j = qi_ref[t], kj_ref[t]
        is_a = lax.broadcasted_iota(jnp.int32, (bq, LANES), 1) < HEAD_DIM

        @pl.when(j == 0)
        def _():
            acc_a[...] = jnp.zeros_like(acc_a)
            acc_b[...] = jnp.zeros_like(acc_b)
            ml[0] = jnp.full((bq, 1), NEG_BIG, F32)
            ml[2] = jnp.full((bq, 1), NEG_BIG, F32)
            ml[1] = jnp.zeros((bq, 1), F32)
            ml[3] = jnp.zeros((bq, 1), F32)
            qa[...], qb[...] = _masked_pair(q_ref[...], is_a)

        def tile(masked):
            k, v, cc = k_ref[...], v_ref[...], cc_ref[0]
            if masked:
                tri = lax.broadcasted_iota(jnp.int32, (bq, bq), 0) >= lax.broadcasted_iota(jnp.int32, (bq, bq), 1)
            for h, (qh, acc) in enumerate(((qa, acc_a), (qb, acc_b))):
                s = _dot(qh[...], k, _NT) * SCALE + (_col(cc, h) - cr_ref[0, pl.ds(h, 1), :])
                if masked:
                    s = jnp.where(tri, s, NEG_BIG)
                m_prev, l_prev = ml[2 * h], ml[2 * h + 1]
                m_new = jnp.maximum(m_prev, jnp.max(s, axis=1, keepdims=True))
                a = jnp.exp(m_prev - m_new)
                p = jnp.exp(s - m_new)
                ml[2 * h] = m_new
                ml[2 * h + 1] = a * l_prev + jnp.sum(p, axis=1, keepdims=True)
                acc[...] = a * acc[...] + _dot(p.astype(BF16), v)

        pl.when(j == i)(functools.partial(tile, True))
        pl.when(j < i)(functools.partial(tile, False))

        @pl.when(j == i)
        def _():
            o_ref[...] = jnp.where(is_a, acc_a[...] / ml[1], acc_b[...] / ml[3])
            lane8 = lax.broadcasted_iota(jnp.int32, (bq, 8), 1)
            st = jnp.where(lane8 == 0, ml[0] + jnp.log(ml[1]), 0.0)
            st_ref[0] = jnp.where(lane8 == 1, ml[2] + jnp.log(ml[3]), st)

    grid_spec = pltpu.PrefetchScalarGridSpec(
        num_scalar_prefetch=2, grid=(N_PAIRS, int(qi.shape[0])),
        in_specs=[pl.BlockSpec((bq, LANES), lambda p, t, qi, kj: (qi[t], col0 + p)),
                  pl.BlockSpec((bq, LANES), lambda p, t, qi, kj: (kj[t], col0 + 4 + p)),
                  pl.BlockSpec((bq, LANES), lambda p, t, qi, kj: (kj[t], col0 + 8 + p)),
                  pl.BlockSpec((1, bq, 8), lambda p, t, qi, kj: (p, qi[t], 0)),
                  pl.BlockSpec((1, 8, bq), lambda p, t, qi, kj: (p, 0, kj[t]))],
        out_specs=[pl.BlockSpec((bq, LANES), lambda p, t, qi, kj: (qi[t], p)),
                   pl.BlockSpec((1, bq, 8), lambda p, t, qi, kj: (p, qi[t], 0))],
        scratch_shapes=[pltpu.VMEM((bq, LANES), F32), pltpu.VMEM((bq, LANES), F32),
                        pltpu.VMEM((bq, LANES), BF16), pltpu.VMEM((bq, LANES), BF16),
                        pltpu.VMEM((4, bq, 1), F32)])
    return pl.pallas_call(
        body, name="fox_fwd", grid_spec=grid_spec,
        out_shape=[jax.ShapeDtypeStruct((S, GROUP_W), F32), jax.ShapeDtypeStruct((N_PAIRS, S, 8), F32)],
    )(qi, kj, proj, proj, proj, c_col, c_row)


def _fox_bwd(proj, col0, do, o, st, c_col, c_row, bq):
    S = proj.shape[0]
    bq = min(bq, S)
    nq = S // bq
    qi, kj = _causal_tables(nq, False)

    def body(qi_ref, kj_ref, q_ref, k_ref, v_ref, do_ref, o_ref, st_ref, cc_ref, cr_ref,
             dq_ref, dk_ref, dv_ref, dc_ref, dcq_ref, dq_a, dq_b, qa, qb, doa, dob, dd):
        t = pl.program_id(1)
        i, j = qi_ref[t], kj_ref[t]
        is_a = lax.broadcasted_iota(jnp.int32, (bq, LANES), 1) < HEAD_DIM

        @pl.when(t == 0)
        def _():
            dk_ref[...] = jnp.zeros_like(dk_ref)
            dv_ref[...] = jnp.zeros_like(dv_ref)
            dc_ref[...] = jnp.zeros_like(dc_ref)

        @pl.when(j == 0)
        def _():
            dq_a[...] = jnp.zeros_like(dq_a)
            dq_b[...] = jnp.zeros_like(dq_b)
            qa[...], qb[...] = _masked_pair(q_ref[...], is_a)
            dov = do_ref[...]
            doa[...], dob[...] = _masked_pair(dov, is_a)
            prod = dov * o_ref[...]
            dd[0] = jnp.sum(jnp.where(is_a, prod, 0.0), axis=1, keepdims=True)
            dd[1] = jnp.sum(jnp.where(is_a, 0.0, prod), axis=1, keepdims=True)
            dd[2] = jnp.zeros((bq, 1), F32)
            dd[3] = jnp.zeros((bq, 1), F32)

        def tile(masked):
            k, v, cc, st_v = k_ref[...], v_ref[...], cc_ref[0], st_ref[0]
            if masked:
                tri = lax.broadcasted_iota(jnp.int32, (bq, bq), 0) >= lax.broadcasted_iota(jnp.int32, (bq, bq), 1)
            cols = pl.ds(pl.multiple_of(j * bq, bq), bq)
            dss, ps = [], []
            for h, (qh, doh, dq) in enumerate(((qa, doa, dq_a), (qb, dob, dq_b))):
                s = _dot(qh[...], k, _NT) * SCALE + (_col(cc, h) - cr_ref[0, pl.ds(h, 1), :])
                p = jnp.exp(s - _col(st_v, h))
                if masked:
                    p = jnp.where(tri, p, 0.0)
                ds = p * (_dot(doh[...], v, _NT) - dd[h])
                dc_ref[0, pl.ds(h, 1), cols] -= jnp.sum(ds, axis=0, keepdims=True)
                dd[2 + h] += jnp.sum(ds, axis=1, keepdims=True)
                dsb = (ds * SCALE).astype(BF16)
                dq[...] += _dot(dsb, k)
                dss.append(dsb)
                ps.append(p.astype(BF16))
            dk_ref[cols, :] += _dot(dss[0], qa[...], _TN) + _dot(dss[1], qb[...], _TN)
            dv_ref[cols, :] += _dot(ps[0], doa[...], _TN) + _dot(ps[1], dob[...], _TN)

        pl.when(j == i)(functools.partial(tile, True))
        pl.when(j < i)(functools.partial(tile, False))

        @pl.when(j == i)
        def _():
            dq_ref[...] = jnp.where(is_a, dq_a[...], dq_b[...])
            lane8 = lax.broadcasted_iota(jnp.int32, (bq, 8), 1)
            dcq_ref[0] = jnp.where(lane8 == 0, dd[2], jnp.where(lane8 == 1, dd[3], 0.0))

    grid_spec = pltpu.PrefetchScalarGridSpec(
        num_scalar_prefetch=2, grid=(N_PAIRS, int(qi.shape[0])),
        in_specs=[pl.BlockSpec((bq, LANES), lambda p, t, qi, kj: (qi[t], col0 + p)),
                  pl.BlockSpec((bq, LANES), lambda p, t, qi, kj: (kj[t], col0 + 4 + p)),
                  pl.BlockSpec((bq, LANES), lambda p, t, qi, kj: (kj[t], col0 + 8 + p)),
                  pl.BlockSpec((bq, LANES), lambda p, t, qi, kj: (qi[t], p)),
                  pl.BlockSpec((bq, LANES), lambda p, t, qi, kj: (qi[t], p)),
                  pl.BlockSpec((1, bq, 8), lambda p, t, qi, kj: (p, qi[t], 0)),
                  pl.BlockSpec((1, bq, 8), lambda p, t, qi, kj: (p, qi[t], 0)),
                  pl.BlockSpec((1, 8, bq), lambda p, t, qi, kj: (p, 0, kj[t]))],
        out_specs=[pl.BlockSpec((bq, LANES), lambda p, t, qi, kj: (qi[t], p)),
                   pl.BlockSpec((S, LANES), lambda p, t, qi, kj: (0, p)),
                   pl.BlockSpec((S, LANES), lambda p, t, qi, kj: (0, p)),
                   pl.BlockSpec((1, 8, S), lambda p, t, qi, kj: (p, 0, 0)),
                   pl.BlockSpec((1, bq, 8), lambda p, t, qi, kj: (p, qi[t], 0))],
        scratch_shapes=[pltpu.VMEM((bq, LANES), F32), pltpu.VMEM((bq, LANES), F32)]
        + [pltpu.VMEM((bq, LANES), BF16)] * 4 + [pltpu.VMEM((4, bq, 1), F32)])
    return pl.pallas_call(
        body, name="fox_bwd", grid_spec=grid_spec,
        out_shape=[jax.ShapeDtypeStruct((S, GROUP_W), F32)] * 3
        + [jax.ShapeDtypeStruct((N_PAIRS, 8, S), F32), jax.ShapeDtypeStruct((N_PAIRS, S, 8), F32)],
        compiler_params=_params(VMEM_BIG),
    )(qi, kj, proj, proj, proj, do, o, st, c_col, c_row)


_HBM = pl.BlockSpec(memory_space=pltpu.HBM)


def _coords():
    return lax.axis_index("x"), lax.axis_index("y"), lax.axis_index("c")


def _allgather_chips(shards):
    n = len(shards)

    def body(*refs):
        ins, outs = refs[:n], refs[n:2 * n]
        send_sems, recv_sems, loc_sems = refs[2 * n:]
        x, y, c = _coords()
        mine = 2 * x + y
        chips = [(1 - x, y), (x, 1 - y), (1 - x, 1 - y)]
        local = [pltpu.make_async_copy(ins[w], outs[w].at[mine], loc_sems.at[w]) for w in range(n)]
        for cp in local:
            cp.start()

        def copy(w, r, slab, to):
            return pltpu.make_async_remote_copy(
                src_ref=ins[w], dst_ref=outs[w].at[slab], send_sem=send_sems.at[3 * w + r],
                recv_sem=recv_sems.at[3 * w + r], device_id=to, device_id_type=MESH)

        sends = [copy(w, r, mine, (cx, cy, c)) for w in range(n) for r, (cx, cy) in enumerate(chips)]
        for cp in sends:
            cp.start()
        for w in range(n):
            for r, (cx, cy) in enumerate(chips):
                copy(w, r, 2 * cx + cy, (cx, cy, c)).wait_recv()
        for cp in sends:
            cp.wait_send()
        for cp in local:
            cp.wait()

    return pl.pallas_call(
        body, name="allgather_weights",
        in_specs=[_HBM] * n, out_specs=[_HBM] * n,
        out_shape=[jax.ShapeDtypeStruct((4,) + s.shape, s.dtype) for s in shards],
        scratch_shapes=[pltpu.SemaphoreType.DMA((3 * n,)), pltpu.SemaphoreType.DMA((3 * n,)),
                        pltpu.SemaphoreType.DMA((n,))],
    )(*shards)


def _exchange(parts, per_chip):
    n = len(parts)

    def body(*refs):
        ins, outs = refs[:n], refs[n:2 * n]
        send_sems, recv_sems, loc_sems = refs[2 * n:]
        x, y, c = _coords()
        me = 4 * x + 2 * y + c
        peers = [(x ^ fx, y ^ fy, c ^ fc) for fx in (0, 1) for fy in (0, 1) for fc in (0, 1)][1:]

        def src(w, dev):
            return ins[w].at[2 * dev[0] + dev[1]] if per_chip else ins[w]

        local = [pltpu.make_async_copy(src(w, (x, y, c)), outs[w].at[me], loc_sems.at[w]) for w in range(n)]
        for cp in local:
            cp.start()

        def copy(w, r, source, slab, to):
            return pltpu.make_async_remote_copy(
                src_ref=source, dst_ref=outs[w].at[slab], send_sem=send_sems.at[7 * w + r],
                recv_sem=recv_sems.at[7 * w + r], device_id=to, device_id_type=MESH)

        sends = [copy(w, r, src(w, dev), me, dev) for w in range(n) for r, dev in enumerate(peers)]
        for cp in sends:
            cp.start()
        for w in range(n):
            for r, dev in enumerate(peers):
                copy(w, r, src(w, dev), 4 * dev[0] + 2 * dev[1] + dev[2], dev).wait_recv()
        for cp in sends:
            cp.wait_send()
        for cp in local:
            cp.wait()

    return pl.pallas_call(
        body, name="exchange_per_chip" if per_chip else "exchange_all",
        in_specs=[_HBM] * n, out_specs=[_HBM] * n,
        out_shape=[jax.ShapeDtypeStruct((8,) + p.shape[(1 if per_chip else 0):], p.dtype) for p in parts],
        scratch_shapes=[pltpu.SemaphoreType.DMA((7 * n,)), pltpu.SemaphoreType.DMA((7 * n,)),
                        pltpu.SemaphoreType.DMA((n,))],
    )(*parts)


def _adamw(w, g, m, v):
    m = ADAM_B1 * m + (1.0 - ADAM_B1) * g
    v = ADAM_B2 * v + (1.0 - ADAM_B2) * (g * g)
    m_hat = m / (1.0 - ADAM_B1 ** ADAM_STEP)
    v_hat = v / (1.0 - ADAM_B2 ** ADAM_STEP)
    delta = -ADAM_LR * (m_hat / (jnp.sqrt(v_hat) + ADAM_EPS) + ADAM_WD * w)
    return delta, m, v


def _sum_adamw(parts, w, m, v, name, tr):
    R, C = w.shape
    tr = min(tr, R)
    assert R % tr == 0

    def body(p_ref, w_ref, m_ref, v_ref, g_ref, d_ref, nm_ref, nv_ref):
        g = p_ref[0].astype(F32)
        for d in range(1, 8):
            g = g + p_ref[d].astype(F32)
        g_ref[...] = g
        d_ref[...], nm_ref[...], nv_ref[...] = _adamw(w_ref[...], g, m_ref[...], v_ref[...])

    tile = pl.BlockSpec((tr, C), lambda i: (i, 0))
    return pl.pallas_call(
        body, name=name, grid=(R // tr,),
        in_specs=[pl.BlockSpec((8, tr, C), lambda i: (0, i, 0)), tile, tile, tile],
        out_specs=[tile] * 4, out_shape=[jax.ShapeDtypeStruct((R, C), F32)] * 4,
    )(parts, w, m, v)


def _sum_adamw_small(parts, w, m, v):
    def body(p_ref, w_ref, m_ref, v_ref, g_ref, d_ref, nm_ref, nv_ref, loss_ref):
        g = p_ref[0]
        for d in range(1, 8):
            g = g + p_ref[d]
        g_ref[...] = g
        d_ref[...], nm_ref[...], nv_ref[...] = _adamw(w_ref[...], g, m_ref[...], v_ref[...])
        row = lax.broadcasted_iota(jnp.int32, g.shape, 0)
        per_row = jnp.sum(jnp.where(row == 6, g, 0.0), axis=1, keepdims=True)
        loss_ref[...] = jnp.zeros((8, LANES), F32) + jnp.sum(per_row, axis=0, keepdims=True)

    return pl.pallas_call(
        body, name="sum_adamw_small",
        out_shape=[jax.ShapeDtypeStruct((8, D_MODEL), F32)] * 4 + [jax.ShapeDtypeStruct((8, LANES), F32)],
    )(parts, w, m, v)


def _pack_small(ln1_g, ln1_b, ln2_g, ln2_b, g_sb, g_fox, b_f):
    row5 = jnp.pad(b_f.reshape(1, N_FOX), ((0, 0), (0, D_MODEL - N_FOX)))
    rows = [ln1_g.reshape(1, -1), ln1_b.reshape(1, -1), ln2_g.reshape(1, -1), ln2_b.reshape(1, -1),
            jnp.concatenate([g_sb.reshape(1, -1), g_fox.reshape(1, -1)], axis=1), row5,
            jnp.zeros((2, D_MODEL), F32)]
    return jnp.concatenate(rows, axis=0)


def _unpack_small(p):
    return {"ln1_g": p[0:1], "ln1_b": p[1:2], "ln2_g": p[2:3], "ln2_b": p[3:4], "g_sb": p[4:5, :GROUP_W],
            "g_fox": p[4:5, GROUP_W:], "b_f": p[5:6, :N_FOX]}


def kernel(x, w_in, b_f, g_sb, g_fox, w_out, ln1_g, ln1_b, ln2_g, ln2_b, w_gate_up, w_down, loss_target, m_w_in, m_b_f, m_g_sb, m_g_fox, m_w_out, m_ln1_g, m_ln1_b, m_ln2_g, m_ln2_b, m_w_gate_up, m_w_down, v_w_in, v_b_f, v_g_sb, v_g_fox, v_w_out, v_ln1_g, v_ln1_b, v_ln2_g, v_ln2_b, v_w_gate_up, v_w_down):
    S = x.shape[1]
    x2 = x.reshape(S, D_MODEL)
    tgt = loss_target.reshape(S, D_MODEL)
    TM = 1024
    TR = 512
    BQ = 256
    in_w = w_in.shape[2]
    gu_w = w_gate_up.shape[2]

    shards = [w_in[0].astype(BF16), w_out[0].astype(BF16), w_gate_up[0].astype(BF16), w_down[0].astype(BF16)]
    wi_s, wo_s, wgu_s, wd_s = _allgather_chips(shards)
    wi = wi_s.transpose(1, 0, 2).reshape(D_MODEL, 4 * in_w)
    w_sb, w_fx = wi[:, :QKV_W // 2], wi[:, QKV_W // 2:QKV_W]
    wqkv = wi[:, :QKV_W]
    wft = wi[:, QKV_W:].T
    wo = wo_s.reshape(D_MODEL, D_MODEL)
    wgu = wgu_s.transpose(1, 0, 2).reshape(D_MODEL, 2 * D_FF)
    wg, wu = wgu[:, :D_FF], wgu[:, D_FF:]
    wd = wd_s.reshape(D_FF, D_MODEL)
    g_row = jnp.concatenate([g_sb, g_fox], axis=1)
    hid = np.arange(D_MODEL) // HEAD_DIM
    bd = jnp.asarray((hid[:, None] == hid[None, :]).astype(np.float32), BF16)

    proj = _matmul(x2, wqkv, mode="nn", name="proj", tm=TM, tn=512, tk=D_MODEL, outs=[BF16])
    lf = _fgate_fwd(x2, wft, b_f.reshape(N_FOX, 1), TM)
    c = _cumsum_fwd(lf)
    c_pair = c.reshape(N_PAIRS, 2, S)
    c_row = jnp.pad(c_pair, ((0, 0), (0, 6), (0, 0)))
    c_col = jnp.pad(c_pair.transpose(0, 2, 1), ((0, 0), (0, 0), (0, 6)))

    o_sb, st_sb = _sb_fwd(proj, 0, BQ)
    o_fx, st_fx = _fox_fwd(proj, 12, c_col, c_row, BQ)

    def attn_post(i, osb_ref, ofx_ref, g_ref, bd_ref, on_ref):
        o = jnp.concatenate([osb_ref[...], ofx_ref[...]], axis=1)
        ms = _head_sums(o * o, bd_ref[...]) * (1.0 / HEAD_DIM)
        on_ref[...] = (o * lax.rsqrt(ms + RMS_EPS) * g_ref[...]).astype(BF16)

    (on,) = _rowwise(attn_post, "attn_post", S, TR, [(o_sb, "t"), (o_fx, "t"), (g_row, "f"), (bd, "f")],
                     [((S, D_MODEL), BF16, "t")])

    u1 = _matmul(on, wo, mode="nn", name="mix", tm=TM, tn=D_MODEL, tk=D_MODEL, outs=[F32],
                 extras=[(x2, (TM if S >= TM else S, D_MODEL), _tile_ij)],
                 epilogue=lambda acc, xv: (ALPHA * xv + acc,))

    def ln1_fwd(i, u_ref, g_ref, b_ref, h_ref):
        xh, _ = _ln_stats(u_ref[...])
        h_ref[...] = xh * g_ref[...] + b_ref[...]

    (h1,) = _rowwise(ln1_fwd, "ln1_fwd", S, TR, [(u1, "t"), (ln1_g, "f"), (ln1_b, "f")], [((S, D_MODEL), F32, "t")])

    gu = _matmul(h1, wgu, mode="nn", name="gate_up", tm=TM, tn=512, tk=D_MODEL, outs=[F32])

    tmr = min(TR, S)
    n_ff = D_FF // 256

    def swiglu_body(g_ref, u_ref, a_ref):
        g = g_ref[...]
        a_ref[...] = (g / (1.0 + jnp.exp(-g)) * u_ref[...]).astype(BF16)

    act = pl.pallas_call(
        swiglu_body, name="swiglu", grid=(S // tmr, n_ff),
        in_specs=[pl.BlockSpec((tmr, 256), lambda i, j: (i, j)), pl.BlockSpec((tmr, 256), lambda i, j: (i, j + n_ff))],
        out_specs=pl.BlockSpec((tmr, 256), lambda i, j: (i, j)),
        out_shape=jax.ShapeDtypeStruct((S, D_FF), BF16))(gu, gu)

    u2 = _matmul(act, wd, mode="nn", name="ffn_down", tm=TM, tn=D_MODEL, tk=D_FF, outs=[F32],
                 extras=[(h1, (TM if S >= TM else S, D_MODEL), _tile_ij)],
                 epilogue=lambda acc, hv: (ALPHA * hv + acc,))

    def ln2_loss(i, u_ref, t_ref, g_ref, b_ref, du_ref, acc_ref):
        xh, r = _ln_stats(u_ref[...])
        g = g_ref[...]
        err = xh * g + b_ref[...] - t_ref[...]
        dy = err * (1.0 / D_MODEL)
        du_ref[...] = _ln_bwd(dy, xh, r, g)
        _acc_rows(i, acc_ref, {2: jnp.sum(dy * xh, axis=0, keepdims=True), 3: jnp.sum(dy, axis=0, keepdims=True),
                               6: jnp.sum(err * err, axis=0, keepdims=True) * (0.5 / D_MODEL)})

    du2, acc_ln2 = _rowwise(ln2_loss, "ln2_loss", S, TR, [(u2, "t"), (tgt, "t"), (ln2_g, "f"), (ln2_b, "f")],
                            [((S, D_MODEL), F32, "t"), ((8, D_MODEL), F32, "f")])

    d_wd = _matmul(act, du2, mode="tn", name="dw_down", tm=1408, tn=D_MODEL, tk=TM, outs=[F32])

    half = D_FF // 1408

    def dgu_epilogue(da, g, u):
        s = 1.0 / (1.0 + jnp.exp(-g))
        return da * u * (s * (1.0 + g * (1.0 - s))), da * (g * s)

    tm_e = TM if S >= TM else S
    dgate, dup = _matmul(du2, wd, mode="nt", name="d_act", tm=TM, tn=1408, tk=D_MODEL, outs=[BF16, BF16],
                         extras=[(gu, (tm_e, 1408), _tile_ij), (gu, (tm_e, 1408), lambda i, j: (i, j + half))],
                         epilogue=dgu_epilogue)
    d_wg = _matmul(h1, dgate, mode="tn", name="dw_gate", tm=D_MODEL, tn=1408, tk=TM, outs=[F32])
    d_wu = _matmul(h1, dup, mode="tn", name="dw_up", tm=D_MODEL, tn=1408, tk=TM, outs=[F32])
    dh1 = _matmul(dgate, wg, mode="nt", name="dh1_gate", tm=TM, tn=D_MODEL, tk=D_FF, outs=[F32],
                  extras=[(du2, (tm_e, D_MODEL), _tile_ij)], epilogue=lambda acc, e: (ALPHA * e + acc,))
    dh1 = _matmul(dup, wu, mode="nt", name="dh1_up", tm=TM, tn=D_MODEL, tk=D_FF, outs=[F32],
                  extras=[(dh1, (tm_e, D_MODEL), _tile_ij)], epilogue=lambda acc, e: (e + acc,))

    def ln1_bwd(i, dh_ref, u_ref, g_ref, du_ref, acc_ref):
        xh, r = _ln_stats(u_ref[...])
        dh = dh_ref[...]
        du_ref[...] = _ln_bwd(dh, xh, r, g_ref[...])
        _acc_rows(i, acc_ref, {0: jnp.sum(dh * xh, axis=0, keepdims=True), 1: jnp.sum(dh, axis=0, keepdims=True)})

    du1, acc_ln1 = _rowwise(ln1_bwd, "ln1_bwd", S, TR, [(dh1, "t"), (u1, "t"), (ln1_g, "f")],
                            [((S, D_MODEL), F32, "t"), ((8, D_MODEL), F32, "f")])
    d_wo = _matmul(on, du1, mode="tn", name="dw_out", tm=D_MODEL, tn=D_MODEL, tk=TM, outs=[F32])
    don = _matmul(du1, wo, mode="nt", name="d_on", tm=TM, tn=D_MODEL, tk=D_MODEL, outs=[F32])

    def rms_bwd(i, don_ref, osb_ref, ofx_ref, g_ref, bd_ref, dosb_ref, dofx_ref, acc_ref):
        o = jnp.concatenate([osb_ref[...], ofx_ref[...]], axis=1)
        bdv = bd_ref[...]
        r = lax.rsqrt(_head_sums(o * o, bdv) * (1.0 / HEAD_DIM) + RMS_EPS)
        dn = don_ref[...]
        dg = dn * g_ref[...]
        do = r * dg - o * (r * r * r) * (_head_sums(dg * o, bdv) * (1.0 / HEAD_DIM))
        dosb_ref[...] = do[:, :GROUP_W]
        dofx_ref[...] = do[:, GROUP_W:]
        _acc_rows(i, acc_ref, {4: jnp.sum(dn * o * r, axis=0, keepdims=True)})

    do_sb, do_fx, acc_rms = _rowwise(
        rms_bwd, "rms_bwd", S, TR, [(don, "t"), (o_sb, "t"), (o_fx, "t"), (g_row, "f"), (bd, "f")],
        [((S, GROUP_W), F32, "t"), ((S, GROUP_W), F32, "t"), ((8, D_MODEL), F32, "f")])

    dq_sb, dk_sb, dv_sb = _sb_bwd(proj, 0, do_sb, st_sb, BQ)
    dq_fx, dk_fx, dv_fx, dc, dcq = _fox_bwd(proj, 12, do_fx, o_fx, st_fx, c_col, c_row, BQ)
    dc = dc[:, :2, :] + dcq[:, :, :2].transpose(0, 2, 1)
    dfl, dbf = _fgate_bwd(dc.reshape(N_FOX, S), lf)
    dp_sb = jnp.concatenate([dq_sb, dk_sb, dv_sb], axis=1).astype(BF16)
    dp_fx = jnp.concatenate([dq_fx, dk_fx, dv_fx], axis=1).astype(BF16)

    d_wsb = _matmul(x2, dp_sb, mode="tn", name="dw_in_sb", tm=D_MODEL, tn=QKV_W // 2, tk=TM, outs=[F32])
    d_wfx = _matmul(x2, dp_fx, mode="tn", name="dw_in_fx", tm=D_MODEL, tn=QKV_W // 2, tk=TM, outs=[F32])
    d_wft = _matmul(dfl, x2, mode="nn", name="dw_in_f", tm=N_FOX, tn=D_MODEL, tk=TM, outs=[F32])
    dx = _matmul(dp_sb, w_sb, mode="nt", name="dx_sb", tm=TM, tn=D_MODEL, tk=QKV_W // 2, outs=[F32],
                 extras=[(du1, (tm_e, D_MODEL), _tile_ij)], epilogue=lambda acc, e: (ALPHA * e + acc,))
    dx = _matmul(dp_fx, w_fx, mode="nt", name="dx_fx", tm=TM, tn=D_MODEL, tk=QKV_W // 2, outs=[F32],
                 extras=[(dx, (tm_e, D_MODEL), _tile_ij)], epilogue=lambda acc, e: (e + acc,))
    dx = _matmul(dfl, wft, mode="tn", name="dx_f", tm=TM, tn=D_MODEL, tk=N_FOX, outs=[F32],
                 extras=[(dx, (tm_e, D_MODEL), _tile_ij)], epilogue=lambda acc, e: (e + acc,))

    d_wi = jnp.concatenate([d_wsb, d_wfx, d_wft.T], axis=1)
    d_wgu = jnp.concatenate([d_wg, d_wu], axis=1)
    parts = [d_wi.reshape(D_MODEL, 4, in_w).transpose(1, 0, 2).astype(BF16),
             d_wo.reshape(4, D_MODEL // 4, D_MODEL).astype(BF16),
             d_wgu.reshape(D_MODEL, 4, gu_w).transpose(1, 0, 2).astype(BF16),
             d_wd.reshape(4, D_FF // 4, D_MODEL).astype(BF16)]
    got = _exchange(parts, True)
    big = {}
    for nm, p, w, m, v, tr in (("w_in", got[0], w_in, m_w_in, v_w_in, 256), ("w_out", got[1], w_out, m_w_out, v_w_out, 256),
                               ("w_gate_up", got[2], w_gate_up, m_w_gate_up, v_w_gate_up, 128),
                               ("w_down", got[3], w_down, m_w_down, v_w_down, 176)):
        big[nm] = [r[None] for r in _sum_adamw(p, w[0], m[0], v[0], "sum_adamw_" + nm, tr)]

    small = acc_ln2 + acc_ln1 + acc_rms
    small = small + jnp.pad(dbf.reshape(1, N_FOX), ((5, 2), (0, D_MODEL - N_FOX)))
    (small_all,) = _exchange([small], False)
    sw = _pack_small(ln1_g, ln1_b, ln2_g, ln2_b, g_sb, g_fox, b_f)
    sm = _pack_small(m_ln1_g, m_ln1_b, m_ln2_g, m_ln2_b, m_g_sb, m_g_fox, m_b_f)
    sv = _pack_small(v_ln1_g, v_ln1_b, v_ln2_g, v_ln2_b, v_g_sb, v_g_fox, v_b_f)
    sg, sd, snm, snv, loss_blk = _sum_adamw_small(small_all, sw, sm, sv)
    sg, sd, snm, snv = _unpack_small(sg), _unpack_small(sd), _unpack_small(snm), _unpack_small(snv)

    names = ["w_in", "b_f", "g_sb", "g_fox", "w_out", "ln1_g", "ln1_b", "ln2_g", "ln2_b", "w_gate_up", "w_down"]
    outs = [loss_blk[0, 0], dx.reshape(1, S, D_MODEL)]
    for k, table in enumerate((sg, sd, snm, snv)):
        outs += [big[n][k] if n in big else table[n] for n in names]
    return tuple(outs)
```

```python
import functools

import numpy as np
import jax
import jax.numpy as jnp
from jax import lax
from jax.experimental import pallas as pl
from jax.experimental.pallas import tpu as pltpu

F32 = jnp.float32
BF16 = jnp.bfloat16

D_MODEL = 1024
HEAD_DIM = 64
LANES = 128
N_PAIRS = 4
GROUP_W = 512
QKV_W = 3072
D_FF = 2816
N_FOX = 8
ALPHA = 2.0 ** 0.25
LN_EPS = 1e-5
RMS_EPS = 1e-6
SCALE = HEAD_DIM ** -0.5
NEG_BIG = -1e30
FOX_SKIP = 60.0
SB_STOP = -105.0
ADAM_LR, ADAM_B1, ADAM_B2, ADAM_EPS, ADAM_WD, ADAM_STEP = 0.001, 0.9, 0.999, 1e-08, 0.01, 10
ATTN_BLOCK = 256
VMEM_BIG = 56 * 1024 * 1024
MESH = pl.DeviceIdType.MESH

_NN = (((1,), (0,)), ((), ()))
_NT = (((1,), (1,)), ((), ()))
_TN = (((0,), (0,)), ((), ()))


def _dot(a, b, dims=_NN):
    return lax.dot_general(a, b, dims, preferred_element_type=F32)


def _split_dot(x, t):
    hi = x.astype(BF16)
    lo = (x - hi.astype(F32)).astype(BF16)
    return _dot(hi, t) + _dot(lo, t)


def _softplus(z):
    return jnp.maximum(z, 0.0) + jnp.log1p(jnp.exp(-jnp.abs(z)))


def _col(v, h):
    lane = lax.broadcasted_iota(jnp.int32, v.shape, 1)
    return jnp.sum(jnp.where(lane == h, v, 0.0), axis=1, keepdims=True)


def _two_sum(hi, lo, b):
    s = hi + b
    bb = s - hi
    err = (hi - (s - bb)) + (b - bb)
    return s, lo + err


def _params(vmem=None):
    return pltpu.CompilerParams(vmem_limit_bytes=vmem) if vmem else None


def _matmul(a, b, *, mode, name, tm, tn, tk, outs, extras=(), epilogue=None, vmem=None):
    if mode == "nn":
        (M, K), (_, N) = a.shape, b.shape
    elif mode == "nt":
        (M, K), (N, _) = a.shape, b.shape
    else:
        (K, M), (_, N) = a.shape, b.shape
    tm, tn, tk = min(tm, M), min(tn, N), min(tk, K)
    assert M % tm == 0 and N % tn == 0 and K % tk == 0, (name, M, N, K, tm, tn, tk)
    nk = K // tk
    dims = {"nn": _NN, "nt": _NT, "tn": _TN}[mode]
    if mode == "tn":
        a_spec = pl.BlockSpec((tk, tm), lambda i, j, k: (k, i))
    else:
        a_spec = pl.BlockSpec((tm, tk), lambda i, j, k: (i, k))
    if mode == "nt":
        b_spec = pl.BlockSpec((tn, tk), lambda i, j, k: (j, k))
    else:
        b_spec = pl.BlockSpec((tk, tn), lambda i, j, k: (k, j))
    ex_specs = [pl.BlockSpec(bs, (lambda i, j, k, f=f: f(i, j))) for (_, bs, f) in extras]
    ne, no = len(extras), len(outs)
    if epilogue is None:
        epilogue = lambda acc: (acc,)

    def body(a_ref, b_ref, *rest):
        ex_refs, out_refs, acc = rest[:ne], rest[ne:ne + no], rest[-1]
        k = pl.program_id(2)

        @pl.when(k == 0)
        def _():
            acc[...] = jnp.zeros_like(acc)

        acc[...] += _dot(a_ref[...].astype(BF16), b_ref[...].astype(BF16), dims)

        @pl.when(k == nk - 1)
        def _():
            res = epilogue(acc[...], *[e[...] for e in ex_refs])
            for r, o in zip(res, out_refs):
                o[...] = r.astype(o.dtype)

    res = pl.pallas_call(
        body, name=name, grid=(M // tm, N // tn, nk),
        in_specs=[a_spec, b_spec] + ex_specs,
        out_specs=[pl.BlockSpec((tm, tn), lambda i, j, k: (i, j)) for _ in outs],
        out_shape=[jax.ShapeDtypeStruct((M, N), d) for d in outs],
        scratch_shapes=[pltpu.VMEM((tm, tn), F32)],
        compiler_params=_params(vmem),
    )(a, b, *[e[0] for e in extras])
    return res[0] if no == 1 else res


def _tile_ij(i, j):
    return (i, j)


def _rowwise(fn, name, rows, tm, ins, outs, vmem=None):
    tm = min(tm, rows)
    assert rows % tm == 0

    def spec(shape, kind):
        if kind == "t":
            return pl.BlockSpec((tm,) + tuple(shape[1:]), lambda i: (i,) + (0,) * (len(shape) - 1))
        return pl.BlockSpec(tuple(shape), lambda i: (0,) * len(shape))

    def body(*refs):
        fn(pl.program_id(0), *refs)

    return pl.pallas_call(
        body, name=name, grid=(rows // tm,),
        in_specs=[spec(a.shape, k) for a, k in ins],
        out_specs=[spec(s, k) for s, _, k in outs],
        out_shape=[jax.ShapeDtypeStruct(s, d) for s, d, _ in outs],
        compiler_params=_params(vmem),
    )(*[a for a, _ in ins])


def _ln_stats(u):
    mu = jnp.mean(u, axis=-1, keepdims=True)
    d = u - mu
    var = jnp.mean(d * d, axis=-1, keepdims=True)
    r = lax.rsqrt(var + LN_EPS)
    return d * r, r


def _ln_bwd(dh, xh, r, g):
    dxh = dh * g
    m1 = jnp.mean(dxh, axis=-1, keepdims=True)
    m2 = jnp.mean(dxh * xh, axis=-1, keepdims=True)
    return r * (dxh - m1 - xh * m2)


def _acc_rows(i, ref, rows):
    @pl.when(i == 0)
    def _():
        ref[...] = jnp.zeros_like(ref)
    for r, v in rows.items():
        ref[pl.ds(r, 1), :] += v


def _head_sums(v, bd):
    return _split_dot(v, bd)


def _fgate_fwd(x, wft, bf_col, tm):
    S = x.shape[0]
    tm = min(tm, S)

    def body(wft_ref, bf_ref, x_ref, lf_ref):
        f = _dot(wft_ref[...], x_ref[...].astype(BF16), _NT) + bf_ref[...]
        lf_ref[...] = -_softplus(-f)

    return pl.pallas_call(
        body, name="fgate_fwd", grid=(S // tm,),
        in_specs=[pl.BlockSpec((N_FOX, D_MODEL), lambda i: (0, 0)), pl.BlockSpec((N_FOX, 1), lambda i: (0, 0)),
                  pl.BlockSpec((tm, D_MODEL), lambda i: (i, 0))],
        out_specs=pl.BlockSpec((N_FOX, tm), lambda i: (0, i)),
        out_shape=jax.ShapeDtypeStruct((N_FOX, S), F32),
    )(wft, bf_col, x)


def _chunk_scan(v, reverse):
    lane = lax.broadcasted_iota(jnp.int32, v.shape, 1)
    sh = 1
    while sh < LANES:
        if reverse:
            v = v + jnp.where(lane < LANES - sh, pltpu.roll(v, LANES - sh, 1), 0.0)
        else:
            v = v + jnp.where(lane >= sh, pltpu.roll(v, sh, 1), 0.0)
        sh *= 2
    return v


def _cumsum_fwd(lf):
    n, S = lf.shape
    nc = S // LANES

    def body(lf_ref, c_ref):
        def step(ci, carry):
            sl = pl.ds(pl.multiple_of(ci * LANES, LANES), LANES)
            v = _chunk_scan(lf_ref[:, sl], False) + carry
            c_ref[:, sl] = v
            return _col(v, LANES - 1)
        lax.fori_loop(0, nc, step, jnp.zeros((n, 1), F32))

    return pl.pallas_call(body, name="cumsum_fwd", out_shape=jax.ShapeDtypeStruct((n, S), F32))(lf)


def _fgate_bwd(dc, lf):
    n, S = dc.shape
    nc = S // LANES

    def body(dc_ref, lf_ref, dfl_ref, dbf_ref):
        def step(t, carry):
            car, tot = carry
            ci = nc - 1 - t
            sl = pl.ds(pl.multiple_of(ci * LANES, LANES), LANES)
            dlf = _chunk_scan(dc_ref[:, sl], True) + car
            dfl = dlf * (1.0 - jnp.exp(lf_ref[:, sl]))
            dfl_ref[:, sl] = dfl
            return _col(dlf, 0), tot + jnp.sum(dfl, axis=1, keepdims=True)
        _, tot = lax.fori_loop(0, nc, step, (jnp.zeros((n, 1), F32), jnp.zeros((n, 1), F32)))
        dbf_ref[...] = tot

    return pl.pallas_call(body, name="fgate_bwd",
                          out_shape=[jax.ShapeDtypeStruct((n, S), F32), jax.ShapeDtypeStruct((n, 1), F32)])(dc, lf)


def _tri_matrices(b):
    r = np.arange(b)
    tfwd = (r[:, None] <= r[None, :]).astype(np.float32)
    return jnp.asarray(tfwd, BF16), jnp.asarray(tfwd.T, BF16)


def _kv_copies(kv_hbm, kbuf, vbuf, sems, pair_col, bq, j, slot):
    rows = pl.ds(pl.multiple_of(j * bq, bq), bq)

    def cols(c):
        return pl.ds(pl.multiple_of((pair_col + c) * LANES, LANES), LANES)

    return (pltpu.make_async_copy(kv_hbm.at[rows, cols(4)], kbuf.at[slot], sems.at[0, slot]),
            pltpu.make_async_copy(kv_hbm.at[rows, cols(8)], vbuf.at[slot], sems.at[1, slot]))


def _masked_pair(v, lane_is_a):
    v = v.astype(F32)
    return jnp.where(lane_is_a, v, 0.0).astype(BF16), jnp.where(lane_is_a, 0.0, v).astype(BF16)


def _sb_fwd(proj, col0, bq):
    S = proj.shape[0]
    bq = min(bq, S)
    nq = S // bq
    _, trev = _tri_matrices(bq)

    def body(q_ref, kv_hbm, trev_ref, o_ref, st_ref, jmin_ref, acc_a, acc_b, qa, qb, rs, kbuf, vbuf, sems):
        p, i = pl.program_id(0), pl.program_id(1)
        is_a = lax.broadcasted_iota(jnp.int32, (bq, LANES), 1) < HEAD_DIM
        acc_a[...] = jnp.zeros_like(acc_a)
        acc_b[...] = jnp.zeros_like(acc_b)
        rs[...] = jnp.zeros_like(rs)
        qa[...], qb[...] = _masked_pair(q_ref[...], is_a)
        fetch = functools.partial(_kv_copies, kv_hbm, kbuf, vbuf, sems, col0 + p, bq)

        def tile(slot, masked):
            k, v, trev_m = kbuf[slot], vbuf[slot], trev_ref[...]
            if masked:
                tri = lax.broadcasted_iota(jnp.int32, (bq, bq), 0) > lax.broadcasted_iota(jnp.int32, (bq, bq), 1)
            for h, (qh, acc) in enumerate(((qa, acc_a), (qb, acc_b))):
                z = _dot(qh[...], k, _NT) * SCALE
                lk = -_softplus(z)
                if masked:
                    lk = jnp.where(tri, lk, 0.0)
                r_hi, r_lo = rs[2 * h], rs[2 * h + 1]
                w = jnp.exp(z + _split_dot(lk, trev_m) + (r_hi + r_lo))
                if masked:
                    w = jnp.where(tri, w, 0.0)
                acc[...] += _dot(w.astype(BF16), v)
                rs[2 * h], rs[2 * h + 1] = _two_sum(r_hi, r_lo, jnp.sum(lk, axis=1, keepdims=True))

        for cp in fetch(i, 0):
            cp.start()

        def step(carry):
            j, _ = carry
            slot = lax.rem(i - j, 2)
            for cp in fetch(j, slot):
                cp.wait()

            @pl.when(j > 0)
            def _():
                for cp in fetch(j - 1, 1 - slot):
                    cp.start()

            pl.when(j == i)(functools.partial(tile, slot, True))
            pl.when(j < i)(functools.partial(tile, slot, False))
            live = jnp.max(jnp.maximum(rs[0], rs[2])) > SB_STOP
            return j - 1, live.astype(jnp.int32)

        j_end, _ = lax.while_loop(lambda c: jnp.logical_and(c[0] >= 0, c[1] > 0), step, (i, jnp.int32(1)))

        @pl.when(j_end >= 0)
        def _():
            for cp in fetch(j_end, lax.rem(i - j_end, 2)):
                cp.wait()

        jmin_ref[p, i] = j_end + 1
        o_ref[...] = jnp.where(is_a, acc_a[...], acc_b[...])
        lane8 = lax.broadcasted_iota(jnp.int32, (bq, 8), 1)
        st = jnp.zeros((bq, 8), F32)
        for c, src in enumerate((0, 2, 1, 3)):
            st = jnp.where(lane8 == c, rs[src], st)
        st_ref[0] = st

    return pl.pallas_call(
        body, name="sb_fwd", grid=(N_PAIRS, nq),
        in_specs=[pl.BlockSpec((bq, LANES), lambda p, i: (i, col0 + p)),
                  pl.BlockSpec(memory_space=pl.ANY),
                  pl.BlockSpec((bq, bq), lambda p, i: (0, 0))],
        out_specs=[pl.BlockSpec((bq, LANES), lambda p, i: (i, p)),
                   pl.BlockSpec((1, bq, 8), lambda p, i: (p, i, 0)),
                   pl.BlockSpec(memory_space=pltpu.SMEM)],
        out_shape=[jax.ShapeDtypeStruct((S, GROUP_W), F32), jax.ShapeDtypeStruct((N_PAIRS, S, 8), F32),
                   jax.ShapeDtypeStruct((N_PAIRS, nq), jnp.int32)],
        scratch_shapes=[pltpu.VMEM((bq, LANES), F32), pltpu.VMEM((bq, LANES), F32),
                        pltpu.VMEM((bq, LANES), BF16), pltpu.VMEM((bq, LANES), BF16),
                        pltpu.VMEM((4, bq, 1), F32),
                        pltpu.VMEM((2, bq, LANES), BF16), pltpu.VMEM((2, bq, LANES), BF16),
                        pltpu.SemaphoreType.DMA((2, 2))],
    )(proj, proj, trev)


def _sb_bwd(proj, col0, do, st, jmin, bq):
    S = proj.shape[0]
    bq = min(bq, S)
    nq = S // bq
    tfwd, trev = _tri_matrices(bq)

    def body(jmin_ref, q_ref, kv_hbm, do_ref, st_ref, tfwd_ref, trev_ref,
             dq_ref, dk_ref, dv_ref, dq_a, dq_b, qa, qb, doa, dob, rs, kbuf, vbuf, sems):
        p, i = pl.program_id(0), pl.program_id(1)
        is_a = lax.broadcasted_iota(jnp.int32, (bq, LANES), 1) < HEAD_DIM

        @pl.when(i == 0)
        def _():
            dk_ref[...] = jnp.zeros_like(dk_ref)
            dv_ref[...] = jnp.zeros_like(dv_ref)

        dq_a[...] = jnp.zeros_like(dq_a)
        dq_b[...] = jnp.zeros_like(dq_b)
        rs[...] = jnp.zeros_like(rs)
        qa[...], qb[...] = _masked_pair(q_ref[...], is_a)
        doa[...], dob[...] = _masked_pair(do_ref[...], is_a)
        fetch = functools.partial(_kv_copies, kv_hbm, kbuf, vbuf, sems, col0 + p, bq)
        j0 = jmin_ref[p, i]

        def tile(j, slot, masked):
            k, v = kbuf[slot], vbuf[slot]
            tfwd_m, trev_m = tfwd_ref[...], trev_ref[...]
            st_v = st_ref[0]
            if masked:
                tri = lax.broadcasted_iota(jnp.int32, (bq, bq), 0) > lax.broadcasted_iota(jnp.int32, (bq, bq), 1)
            dzs, ws = [], []
            for h, (qh, doh, dq) in enumerate(((qa, doa, dq_a), (qb, dob, dq_b))):
                z = _dot(qh[...], k, _NT) * SCALE
                lk = -_softplus(z)
                if masked:
                    lk = jnp.where(tri, lk, 0.0)
                p_hi, p_lo = _two_sum(rs[3 * h], rs[3 * h + 1], jnp.sum(lk, axis=1, keepdims=True))
                rs[3 * h], rs[3 * h + 1] = p_hi, p_lo
                right = (_col(st_v, h) - p_hi) + (_col(st_v, 2 + h) - p_lo)
                w = jnp.exp(z + _split_dot(lk, trev_m) + right)
                if masked:
                    w = jnp.where(tri, w, 0.0)
                g = _dot(doh[...], v, _NT) * w
                g_left = rs[3 * h + 2]
                dz = g - jnp.exp(z + lk) * (_split_dot(g, tfwd_m) + g_left)
                if masked:
                    dz = jnp.where(tri, dz, 0.0)
                rs[3 * h + 2] = g_left + jnp.sum(g, axis=1, keepdims=True)
                dzb = (dz * SCALE).astype(BF16)
                dq[...] += _dot(dzb, k)
                dzs.append(dzb)
                ws.append(w.astype(BF16))
            rows = pl.ds(pl.multiple_of(j * bq, bq), bq)
            dk_ref[rows, :] += _dot(dzs[0], qa[...], _TN) + _dot(dzs[1], qb[...], _TN)
            dv_ref[rows, :] += _dot(ws[0], doa[...], _TN) + _dot(ws[1], dob[...], _TN)

        _walk_up(fetch, j0, i, tile)
        dq_ref[...] = jnp.where(is_a, dq_a[...], dq_b[...])

    grid_spec = pltpu.PrefetchScalarGridSpec(
        num_scalar_prefetch=1, grid=(N_PAIRS, nq),
        in_specs=[pl.BlockSpec((bq, LANES), lambda p, i, jm: (i, col0 + p)),
                  pl.BlockSpec(memory_space=pl.ANY),
                  pl.BlockSpec((bq, LANES), lambda p, i, jm: (i, p)),
                  pl.BlockSpec((1, bq, 8), lambda p, i, jm: (p, i, 0)),
                  pl.BlockSpec((bq, bq), lambda p, i, jm: (0, 0)),
                  pl.BlockSpec((bq, bq), lambda p, i, jm: (0, 0))],
        out_specs=[pl.BlockSpec((bq, LANES), lambda p, i, jm: (i, p)),
                   pl.BlockSpec((S, LANES), lambda p, i, jm: (0, p)),
                   pl.BlockSpec((S, LANES), lambda p, i, jm: (0, p))],
        scratch_shapes=[pltpu.VMEM((bq, LANES), F32), pltpu.VMEM((bq, LANES), F32)]
        + [pltpu.VMEM((bq, LANES), BF16)] * 4 + [pltpu.VMEM((6, bq, 1), F32)]
        + [pltpu.VMEM((2, bq, LANES), BF16), pltpu.VMEM((2, bq, LANES), BF16), pltpu.SemaphoreType.DMA((2, 2))])
    return pl.pallas_call(
        body, name="sb_bwd", grid_spec=grid_spec,
        out_shape=[jax.ShapeDtypeStruct((S, GROUP_W), F32)] * 3,
        compiler_params=_params(VMEM_BIG),
    )(jmin, proj, proj, do, st, tfwd, trev)


def _walk_up(fetch, j0, i, tile):
    for cp in fetch(j0, 0):
        cp.start()

    def step(j, carry):
        slot = lax.rem(j - j0, 2)
        for cp in fetch(j, slot):
            cp.wait()

        @pl.when(j < i)
        def _():
            for cp in fetch(j + 1, 1 - slot):
                cp.start()

        pl.when(j == i)(functools.partial(tile, j, slot, True))
        pl.when(j < i)(functools.partial(tile, j, slot, False))
        return carry

    lax.fori_loop(j0, i + 1, step, 0)


def _fox_start_blocks(proj, col0, c, bq):
    S = proj.shape[0]
    nq = S // bq

    def norms(first):
        t = proj[:, first * LANES:(first + N_PAIRS) * LANES].astype(F32).reshape(S, 2 * N_PAIRS, HEAD_DIM)
        return jnp.sqrt(jnp.sum(t * t, axis=-1))

    qmax = norms(col0).reshape(nq, bq, 2 * N_PAIRS).max(axis=1)
    kmax = norms(col0 + 4).max(axis=0)
    zb = (2.0 * SCALE) * qmax * kmax[None, :]
    c_first = c[:, ::bq].T
    c_last = c[:, bq - 1::bq].T
    live = (zb + c_first)[:, None, :] - c_last[None, :, :] >= -FOX_SKIP
    live = live.reshape(nq, nq, N_PAIRS, 2).any(axis=-1)
    first = jnp.where(live.any(axis=1), jnp.argmax(live, axis=1), nq)
    return jnp.minimum(first, jnp.arange(nq)[:, None]).T.astype(jnp.int32)


def _fox_fwd(proj, col0, c_col, c_row, jstart, bq):
    S = proj.shape[0]
    bq = min(bq, S)
    nq = S // bq

    def body(js_ref, q_ref, kv_hbm, cc_ref, cr_ref, o_ref, st_ref, acc_a, acc_b, qa, qb, ml, kbuf, vbuf, sems):
        p, i = pl.program_id(0), pl.program_id(1)
        is_a = lax.broadcasted_iota(jnp.int32, (bq, LANES), 1) < HEAD_DIM
        acc_a[...] = jnp.zeros_like(acc_a)
        acc_b[...] = jnp.zeros_like(acc_b)
        ml[0] = jnp.full((bq, 1), NEG_BIG, F32)
        ml[2] = jnp.full((bq, 1), NEG_BIG, F32)
        ml[1] = jnp.zeros((bq, 1), F32)
        ml[3] = jnp.zeros((bq, 1), F32)
        qa[...], qb[...] = _masked_pair(q_ref[...], is_a)

        def tile(j, slot, masked):
            k, v, cc = kbuf[slot], vbuf[slot], cc_ref[0]
            cols = pl.ds(pl.multiple_of(j * bq, bq), bq)
            if masked:
                tri = lax.broadcasted_iota(jnp.int32, (bq, bq), 0) >= lax.broadcasted_iota(jnp.int32, (bq, bq), 1)
            for h, (qh, acc) in enumerate(((qa, acc_a), (qb, acc_b))):
                s = _dot(qh[...], k, _NT) * SCALE + (_col(cc, h) - cr_ref[0, pl.ds(h, 1), cols])
                if masked:
                    s = jnp.where(tri, s, NEG_BIG)
                m_prev, l_prev = ml[2 * h], ml[2 * h + 1]
                m_new = jnp.maximum(m_prev, jnp.max(s, axis=1, keepdims=True))
                a = jnp.exp(m_prev - m_new)
                p = jnp.exp(s - m_new)
                ml[2 * h] = m_new
                ml[2 * h + 1] = a * l_prev + jnp.sum(p, axis=1, keepdims=True)
                acc[...] = a * acc[...] + _dot(p.astype(BF16), v)

        _walk_up(functools.partial(_kv_copies, kv_hbm, kbuf, vbuf, sems, col0 + p, bq), js_ref[p, i], i, tile)
        o_ref[...] = jnp.where(is_a, acc_a[...] / ml[1], acc_b[...] / ml[3])
        lane8 = lax.broadcasted_iota(jnp.int32, (bq, 8), 1)
        st = jnp.where(lane8 == 0, ml[0] + jnp.log(ml[1]), 0.0)
        st_ref[0] = jnp.where(lane8 == 1, ml[2] + jnp.log(ml[3]), st)

    grid_spec = pltpu.PrefetchScalarGridSpec(
        num_scalar_prefetch=1, grid=(N_PAIRS, nq),
        in_specs=[pl.BlockSpec((bq, LANES), lambda p, i, js: (i, col0 + p)),
                  pl.BlockSpec(memory_space=pl.ANY),
                  pl.BlockSpec((1, bq, 8), lambda p, i, js: (p, i, 0)),
                  pl.BlockSpec((1, 8, S), lambda p, i, js: (p, 0, 0))],
        out_specs=[pl.BlockSpec((bq, LANES), lambda p, i, js: (i, p)),
                   pl.BlockSpec((1, bq, 8), lambda p, i, js: (p, i, 0))],
        scratch_shapes=[pltpu.VMEM((bq, LANES), F32), pltpu.VMEM((bq, LANES), F32),
                        pltpu.VMEM((bq, LANES), BF16), pltpu.VMEM((bq, LANES), BF16),
                        pltpu.VMEM((4, bq, 1), F32),
                        pltpu.VMEM((2, bq, LANES), BF16), pltpu.VMEM((2, bq, LANES), BF16),
                        pltpu.SemaphoreType.DMA((2, 2))])
    return pl.pallas_call(
        body, name="fox_fwd", grid_spec=grid_spec,
        out_shape=[jax.ShapeDtypeStruct((S, GROUP_W), F32), jax.ShapeDtypeStruct((N_PAIRS, S, 8), F32)],
    )(jstart, proj, proj, c_col, c_row)


def _fox_bwd(proj, col0, do, o, st, c_col, c_row, jstart, bq):
    S = proj.shape[0]
    bq = min(bq, S)
    nq = S // bq

    def body(js_ref, q_ref, kv_hbm, do_ref, o_ref, st_ref, cc_ref, cr_ref,
             dq_ref, dk_ref, dv_ref, dc_ref, dcq_ref, dq_a, dq_b, qa, qb, doa, dob, dd, kbuf, vbuf, sems):
        p, i = pl.program_id(0), pl.program_id(1)
        is_a = lax.broadcasted_iota(jnp.int32, (bq, LANES), 1) < HEAD_DIM

        @pl.when(i == 0)
        def _():
            dk_ref[...] = jnp.zeros_like(dk_ref)
            dv_ref[...] = jnp.zeros_like(dv_ref)
            dc_ref[...] = jnp.zeros_like(dc_ref)

        dq_a[...] = jnp.zeros_like(dq_a)
        dq_b[...] = jnp.zeros_like(dq_b)
        qa[...], qb[...] = _masked_pair(q_ref[...], is_a)
        dov = do_ref[...]
        doa[...], dob[...] = _masked_pair(dov, is_a)
        prod = dov * o_ref[...]
        dd[0] = jnp.sum(jnp.where(is_a, prod, 0.0), axis=1, keepdims=True)
        dd[1] = jnp.sum(jnp.where(is_a, 0.0, prod), axis=1, keepdims=True)
        dd[2] = jnp.zeros((bq, 1), F32)
        dd[3] = jnp.zeros((bq, 1), F32)

        def tile(j, slot, masked):
            k, v, cc, st_v = kbuf[slot], vbuf[slot], cc_ref[0], st_ref[0]
            if masked:
                tri = lax.broadcasted_iota(jnp.int32, (bq, bq), 0) >= lax.broadcasted_iota(jnp.int32, (bq, bq), 1)
            cols = pl.ds(pl.multiple_of(j * bq, bq), bq)
            dss, ps = [], []
            for h, (qh, doh, dq) in enumerate(((qa, doa, dq_a), (qb, dob, dq_b))):
                s = _dot(qh[...], k, _NT) * SCALE + (_col(cc, h) - cr_ref[0, pl.ds(h, 1), cols])
                p = jnp.exp(s - _col(st_v, h))
                if masked:
                    p = jnp.where(tri, p, 0.0)
                ds = p * (_dot(doh[...], v, _NT) - dd[h])
                dc_ref[0, pl.ds(h, 1), cols] -= jnp.sum(ds, axis=0, keepdims=True)
                dd[2 + h] += jnp.sum(ds, axis=1, keepdims=True)
                dsb = (ds * SCALE).astype(BF16)
                dq[...] += _dot(dsb, k)
                dss.append(dsb)
                ps.append(p.astype(BF16))
            dk_ref[cols, :] += _dot(dss[0], qa[...], _TN) + _dot(dss[1], qb[...], _TN)
            dv_ref[cols, :] += _dot(ps[0], doa[...], _TN) + _dot(ps[1], dob[...], _TN)

        _walk_up(functools.partial(_kv_copies, kv_hbm, kbuf, vbuf, sems, col0 + p, bq), js_ref[p, i], i, tile)
        dq_ref[...] = jnp.where(is_a, dq_a[...], dq_b[...])
        lane8 = lax.broadcasted_iota(jnp.int32, (bq, 8), 1)
        dcq_ref[0] = jnp.where(lane8 == 0, dd[2], jnp.where(lane8 == 1, dd[3], 0.0))

    grid_spec = pltpu.PrefetchScalarGridSpec(
        num_scalar_prefetch=1, grid=(N_PAIRS, nq),
        in_specs=[pl.BlockSpec((bq, LANES), lambda p, i, js: (i, col0 + p)),
                  pl.BlockSpec(memory_space=pl.ANY),
                  pl.BlockSpec((bq, LANES), lambda p, i, js: (i, p)),
                  pl.BlockSpec((bq, LANES), lambda p, i, js: (i, p)),
                  pl.BlockSpec((1, bq, 8), lambda p, i, js: (p, i, 0)),
                  pl.BlockSpec((1, bq, 8), lambda p, i, js: (p, i, 0)),
                  pl.BlockSpec((1, 8, S), lambda p, i, js: (p, 0, 0))],
        out_specs=[pl.BlockSpec((bq, LANES), lambda p, i, js: (i, p)),
                   pl.BlockSpec((S, LANES), lambda p, i, js: (0, p)),
                   pl.BlockSpec((S, LANES), lambda p, i, js: (0, p)),
                   pl.BlockSpec((1, 8, S), lambda p, i, js: (p, 0, 0)),
                   pl.BlockSpec((1, bq, 8), lambda p, i, js: (p, i, 0))],
        scratch_shapes=[pltpu.VMEM((bq, LANES), F32), pltpu.VMEM((bq, LANES), F32)]
        + [pltpu.VMEM((bq, LANES), BF16)] * 4 + [pltpu.VMEM((4, bq, 1), F32)]
        + [pltpu.VMEM((2, bq, LANES), BF16), pltpu.VMEM((2, bq, LANES), BF16), pltpu.SemaphoreType.DMA((2, 2))])
    return pl.pallas_call(
        body, name="fox_bwd", grid_spec=grid_spec,
        out_shape=[jax.ShapeDtypeStruct((S, GROUP_W), F32)] * 3
        + [jax.ShapeDtypeStruct((N_PAIRS, 8, S), F32), jax.ShapeDtypeStruct((N_PAIRS, S, 8), F32)],
        compiler_params=_params(VMEM_BIG),
    )(jstart, proj, proj, do, o, st, c_col, c_row)


_HBM = pl.BlockSpec(memory_space=pltpu.HBM)


def _coords():
    return lax.axis_index("x"), lax.axis_index("y"), lax.axis_index("c")


def _allgather_chips(shards):
    n = len(shards)

    def body(*refs):
        ins, outs = refs[:n], refs[n:2 * n]
        send_sems, recv_sems, loc_sems = refs[2 * n:]
        x, y, c = _coords()
        mine = 2 * x + y
        chips = [(1 - x, y), (x, 1 - y), (1 - x, 1 - y)]
        local = [pltpu.make_async_copy(ins[w], outs[w].at[mine], loc_sems.at[w]) for w in range(n)]
        for cp in local:
            cp.start()

        def copy(w, r, slab, to):
            return pltpu.make_async_remote_copy(
                src_ref=ins[w], dst_ref=outs[w].at[slab], send_sem=send_sems.at[3 * w + r],
                recv_sem=recv_sems.at[3 * w + r], device_id=to, device_id_type=MESH)

        sends = [copy(w, r, mine, (cx, cy, c)) for w in range(n) for r, (cx, cy) in enumerate(chips)]
        for cp in sends:
            cp.start()
        for w in range(n):
            for r, (cx, cy) in enumerate(chips):
                copy(w, r, 2 * cx + cy, (cx, cy, c)).wait_recv()
        for cp in sends:
            cp.wait_send()
        for cp in local:
            cp.wait()

    return pl.pallas_call(
        body, name="allgather_weights",
        in_specs=[_HBM] * n, out_specs=[_HBM] * n,
        out_shape=[jax.ShapeDtypeStruct((4,) + s.shape, s.dtype) for s in shards],
        scratch_shapes=[pltpu.SemaphoreType.DMA((3 * n,)), pltpu.SemaphoreType.DMA((3 * n,)),
                        pltpu.SemaphoreType.DMA((n,))],
    )(*shards)


def _exchange(parts, per_chip):
    n = len(parts)

    def body(*refs):
        ins, outs = refs[:n], refs[n:2 * n]
        send_sems, recv_sems, loc_sems = refs[2 * n:]
        x, y, c = _coords()
        me = 4 * x + 2 * y + c
        peers = [(x ^ fx, y ^ fy, c ^ fc) for fx in (0, 1) for fy in (0, 1) for fc in (0, 1)][1:]

        def src(w, dev):
            return ins[w].at[2 * dev[0] + dev[1]] if per_chip else ins[w]

        local = [pltpu.make_async_copy(src(w, (x, y, c)), outs[w].at[me], loc_sems.at[w]) for w in range(n)]
        for cp in local:
            cp.start()

        def copy(w, r, source, slab, to):
            return pltpu.make_async_remote_copy(
                src_ref=source, dst_ref=outs[w].at[slab], send_sem=send_sems.at[7 * w + r],
                recv_sem=recv_sems.at[7 * w + r], device_id=to, device_id_type=MESH)

        sends = [copy(w, r, src(w, dev), me, dev) for w in range(n) for r, dev in enumerate(peers)]
        for cp in sends:
            cp.start()
        for w in range(n):
            for r, dev in enumerate(peers):
                copy(w, r, src(w, dev), 4 * dev[0] + 2 * dev[1] + dev[2], dev).wait_recv()
        for cp in sends:
            cp.wait_send()
        for cp in local:
            cp.wait()

    return pl.pallas_call(
        body, name="exchange_per_chip" if per_chip else "exchange_all",
        in_specs=[_HBM] * n, out_specs=[_HBM] * n,
        out_shape=[jax.ShapeDtypeStruct((8,) + p.shape[(1 if per_chip else 0):], p.dtype) for p in parts],
        scratch_shapes=[pltpu.SemaphoreType.DMA((7 * n,)), pltpu.SemaphoreType.DMA((7 * n,)),
                        pltpu.SemaphoreType.DMA((n,))],
    )(*parts)


def _adamw(w, g, m, v):
    m = ADAM_B1 * m + (1.0 - ADAM_B1) * g
    v = ADAM_B2 * v + (1.0 - ADAM_B2) * (g * g)
    m_hat = m / (1.0 - ADAM_B1 ** ADAM_STEP)
    v_hat = v / (1.0 - ADAM_B2 ** ADAM_STEP)
    delta = -ADAM_LR * (m_hat / (jnp.sqrt(v_hat) + ADAM_EPS) + ADAM_WD * w)
    return delta, m, v


def _sum_adamw(parts, w, m, v, name, tr):
    R, C = w.shape
    tr = min(tr, R)
    assert R % tr == 0

    def body(p_ref, w_ref, m_ref, v_ref, g_ref, d_ref, nm_ref, nv_ref):
        g = p_ref[0].astype(F32)
        for d in range(1, 8):
            g = g + p_ref[d].astype(F32)
        g_ref[...] = g
        d_ref[...], nm_ref[...], nv_ref[...] = _adamw(w_ref[...], g, m_ref[...], v_ref[...])

    tile = pl.BlockSpec((tr, C), lambda i: (i, 0))
    return pl.pallas_call(
        body, name=name, grid=(R // tr,),
        in_specs=[pl.BlockSpec((8, tr, C), lambda i: (0, i, 0)), tile, tile, tile],
        out_specs=[tile] * 4, out_shape=[jax.ShapeDtypeStruct((R, C), F32)] * 4,
    )(parts, w, m, v)


def _sum_adamw_small(parts, w, m, v):
    def body(p_ref, w_ref, m_ref, v_ref, g_ref, d_ref, nm_ref, nv_ref, loss_ref):
        g = p_ref[0]
        for d in range(1, 8):
            g = g + p_ref[d]
        g_ref[...] = g
        d_ref[...], nm_ref[...], nv_ref[...] = _adamw(w_ref[...], g, m_ref[...], v_ref[...])
        row = lax.broadcasted_iota(jnp.int32, g.shape, 0)
        per_row = jnp.sum(jnp.where(row == 6, g, 0.0), axis=1, keepdims=True)
        loss_ref[...] = jnp.zeros((8, LANES), F32) + jnp.sum(per_row, axis=0, keepdims=True)

    return pl.pallas_call(
        body, name="sum_adamw_small",
        out_shape=[jax.ShapeDtypeStruct((8, D_MODEL), F32)] * 4 + [jax.ShapeDtypeStruct((8, LANES), F32)],
    )(parts, w, m, v)


def _pack_small(ln1_g, ln1_b, ln2_g, ln2_b, g_sb, g_fox, b_f):
    row5 = jnp.pad(b_f.reshape(1, N_FOX), ((0, 0), (0, D_MODEL - N_FOX)))
    rows = [ln1_g.reshape(1, -1), ln1_b.reshape(1, -1), ln2_g.reshape(1, -1), ln2_b.reshape(1, -1),
            jnp.concatenate([g_sb.reshape(1, -1), g_fox.reshape(1, -1)], axis=1), row5,
            jnp.zeros((2, D_MODEL), F32)]
    return jnp.concatenate(rows, axis=0)


def _unpack_small(p):
    return {"ln1_g": p[0:1], "ln1_b": p[1:2], "ln2_g": p[2:3], "ln2_b": p[3:4], "g_sb": p[4:5, :GROUP_W],
            "g_fox": p[4:5, GROUP_W:], "b_f": p[5:6, :N_FOX]}


def kernel(x, w_in, b_f, g_sb, g_fox, w_out, ln1_g, ln1_b, ln2_g, ln2_b, w_gate_up, w_down, loss_target, m_w_in, m_b_f, m_g_sb, m_g_fox, m_w_out, m_ln1_g, m_ln1_b, m_ln2_g, m_ln2_b, m_w_gate_up, m_w_down, v_w_in, v_b_f, v_g_sb, v_g_fox, v_w_out, v_ln1_g, v_ln1_b, v_ln2_g, v_ln2_b, v_w_gate_up, v_w_down):
    S = x.shape[1]
    x2 = x.reshape(S, D_MODEL)
    tgt = loss_target.reshape(S, D_MODEL)
    TM = 1024
    TR = 512
    BQ = ATTN_BLOCK
    in_w = w_in.shape[2]
    gu_w = w_gate_up.shape[2]

    shards = [w_in[0].astype(BF16), w_out[0].astype(BF16), w_gate_up[0].astype(BF16), w_down[0].astype(BF16)]
    wi_s, wo_s, wgu_s, wd_s = _allgather_chips(shards)
    wi = wi_s.transpose(1, 0, 2).reshape(D_MODEL, 4 * in_w)
    w_sb, w_fx = wi[:, :QKV_W // 2], wi[:, QKV_W // 2:QKV_W]
    wqkv = wi[:, :QKV_W]
    wft = wi[:, QKV_W:].T
    wo = wo_s.reshape(D_MODEL, D_MODEL)
    wgu = wgu_s.transpose(1, 0, 2).reshape(D_MODEL, 2 * D_FF)
    wg, wu = wgu[:, :D_FF], wgu[:, D_FF:]
    wd = wd_s.reshape(D_FF, D_MODEL)
    g_row = jnp.concatenate([g_sb, g_fox], axis=1)
    hid = np.arange(D_MODEL) // HEAD_DIM
    bd = jnp.asarray((hid[:, None] == hid[None, :]).astype(np.float32), BF16)

    proj = _matmul(x2, wqkv, mode="nn", name="proj", tm=TM, tn=512, tk=D_MODEL, outs=[BF16])
    lf = _fgate_fwd(x2, wft, b_f.reshape(N_FOX, 1), TM)
    c = _cumsum_fwd(lf)
    c_pair = c.reshape(N_PAIRS, 2, S)
    c_row = jnp.pad(c_pair, ((0, 0), (0, 6), (0, 0)))
    c_col = jnp.pad(c_pair.transpose(0, 2, 1), ((0, 0), (0, 0), (0, 6)))

    o_sb, st_sb, jmin_sb = _sb_fwd(proj, 0, BQ)
    jstart_fx = _fox_start_blocks(proj, 12, c, min(BQ, S))
    o_fx, st_fx = _fox_fwd(proj, 12, c_col, c_row, jstart_fx, BQ)

    def attn_post(i, osb_ref, ofx_ref, g_ref, bd_ref, on_ref):
        o = jnp.concatenate([osb_ref[...], ofx_ref[...]], axis=1)
        ms = _head_sums(o * o, bd_ref[...]) * (1.0 / HEAD_DIM)
        on_ref[...] = (o * lax.rsqrt(ms + RMS_EPS) * g_ref[...]).astype(BF16)

    (on,) = _rowwise(attn_post, "attn_post", S, TR, [(o_sb, "t"), (o_fx, "t"), (g_row, "f"), (bd, "f")],
                     [((S, D_MODEL), BF16, "t")])

    u1 = _matmul(on, wo, mode="nn", name="mix", tm=TM, tn=D_MODEL, tk=D_MODEL, outs=[F32],
                 extras=[(x2, (TM if S >= TM else S, D_MODEL), _tile_ij)],
                 epilogue=lambda acc, xv: (ALPHA * xv + acc,))

    def ln1_fwd(i, u_ref, g_ref, b_ref, h_ref):
        xh, _ = _ln_stats(u_ref[...])
        h_ref[...] = xh * g_ref[...] + b_ref[...]

    (h1,) = _rowwise(ln1_fwd, "ln1_fwd", S, TR, [(u1, "t"), (ln1_g, "f"), (ln1_b, "f")], [((S, D_MODEL), F32, "t")])

    gu = _matmul(h1, wgu, mode="nn", name="gate_up", tm=TM, tn=512, tk=D_MODEL, outs=[F32])

    tmr = min(TR, S)
    n_ff = D_FF // 256

    def swiglu_body(g_ref, u_ref, a_ref):
        g = g_ref[...]
        a_ref[...] = (g / (1.0 + jnp.exp(-g)) * u_ref[...]).astype(BF16)

    act = pl.pallas_call(
        swiglu_body, name="swiglu", grid=(S // tmr, n_ff),
        in_specs=[pl.BlockSpec((tmr, 256), lambda i, j: (i, j)), pl.BlockSpec((tmr, 256), lambda i, j: (i, j + n_ff))],
        out_specs=pl.BlockSpec((tmr, 256), lambda i, j: (i, j)),
        out_shape=jax.ShapeDtypeStruct((S, D_FF), BF16))(gu, gu)

    u2 = _matmul(act, wd, mode="nn", name="ffn_down", tm=TM, tn=D_MODEL, tk=D_FF, outs=[F32],
                 extras=[(h1, (TM if S >= TM else S, D_MODEL), _tile_ij)],
                 epilogue=lambda acc, hv: (ALPHA * hv + acc,))

    def ln2_loss(i, u_ref, t_ref, g_ref, b_ref, du_ref, acc_ref):
        xh, r = _ln_stats(u_ref[...])
        g = g_ref[...]
        err = xh * g + b_ref[...] - t_ref[...]
        dy = err * (1.0 / D_MODEL)
        du_ref[...] = _ln_bwd(dy, xh, r, g)
        _acc_rows(i, acc_ref, {2: jnp.sum(dy * xh, axis=0, keepdims=True), 3: jnp.sum(dy, axis=0, keepdims=True),
                               6: jnp.sum(err * err, axis=0, keepdims=True) * (0.5 / D_MODEL)})

    du2, acc_ln2 = _rowwise(ln2_loss, "ln2_loss", S, TR, [(u2, "t"), (tgt, "t"), (ln2_g, "f"), (ln2_b, "f")],
                            [((S, D_MODEL), F32, "t"), ((8, D_MODEL), F32, "f")])

    d_wd = _matmul(act, du2, mode="tn", name="dw_down", tm=1408, tn=D_MODEL, tk=TM, outs=[F32])

    half = D_FF // 1408

    def dgu_epilogue(da, g, u):
        s = 1.0 / (1.0 + jnp.exp(-g))
        return da * u * (s * (1.0 + g * (1.0 - s))), da * (g * s)

    tm_e = TM if S >= TM else S
    dgate, dup = _matmul(du2, wd, mode="nt", name="d_act", tm=TM, tn=1408, tk=D_MODEL, outs=[BF16, BF16],
                         extras=[(gu, (tm_e, 1408), _tile_ij), (gu, (tm_e, 1408), lambda i, j: (i, j + half))],
                         epilogue=dgu_epilogue)
    d_wg = _matmul(h1, dgate, mode="tn", name="dw_gate", tm=D_MODEL, tn=1408, tk=TM, outs=[F32])
    d_wu = _matmul(h1, dup, mode="tn", name="dw_up", tm=D_MODEL, tn=1408, tk=TM, outs=[F32])
    dh1 = _matmul(dgate, wg, mode="nt", name="dh1_gate", tm=TM, tn=D_MODEL, tk=D_FF, outs=[F32],
                  extras=[(du2, (tm_e, D_MODEL), _tile_ij)], epilogue=lambda acc, e: (ALPHA * e + acc,))
    dh1 = _matmul(dup, wu, mode="nt", name="dh1_up", tm=TM, tn=D_MODEL, tk=D_FF, outs=[F32],
                  extras=[(dh1, (tm_e, D_MODEL), _tile_ij)], epilogue=lambda acc, e: (e + acc,))

    def ln1_bwd(i, dh_ref, u_ref, g_ref, du_ref, acc_ref):
        xh, r = _ln_stats(u_ref[...])
        dh = dh_ref[...]
        du_ref[...] = _ln_bwd(dh, xh, r, g_ref[...])
        _acc_rows(i, acc_ref, {0: jnp.sum(dh * xh, axis=0, keepdims=True), 1: jnp.sum(dh, axis=0, keepdims=True)})

    du1, acc_ln1 = _rowwise(ln1_bwd, "ln1_bwd", S, TR, [(dh1, "t"), (u1, "t"), (ln1_g, "f")],
                            [((S, D_MODEL), F32, "t"), ((8, D_MODEL), F32, "f")])
    d_wo = _matmul(on, du1, mode="tn", name="dw_out", tm=D_MODEL, tn=D_MODEL, tk=TM, outs=[F32])
    don = _matmul(du1, wo, mode="nt", name="d_on", tm=TM, tn=D_MODEL, tk=D_MODEL, outs=[F32])

    def rms_bwd(i, don_ref, osb_ref, ofx_ref, g_ref, bd_ref, dosb_ref, dofx_ref, acc_ref):
        o = jnp.concatenate([osb_ref[...], ofx_ref[...]], axis=1)
        bdv = bd_ref[...]
        r = lax.rsqrt(_head_sums(o * o, bdv) * (1.0 / HEAD_DIM) + RMS_EPS)
        dn = don_ref[...]
        dg = dn * g_ref[...]
        do = r * dg - o * (r * r * r) * (_head_sums(dg * o, bdv) * (1.0 / HEAD_DIM))
        dosb_ref[...] = do[:, :GROUP_W]
        dofx_ref[...] = do[:, GROUP_W:]
        _acc_rows(i, acc_ref, {4: jnp.sum(dn * o * r, axis=0, keepdims=True)})

    do_sb, do_fx, acc_rms = _rowwise(
        rms_bwd, "rms_bwd", S, TR, [(don, "t"), (o_sb, "t"), (o_fx, "t"), (g_row, "f"), (bd, "f")],
        [((S, GROUP_W), F32, "t"), ((S, GROUP_W), F32, "t"), ((8, D_MODEL), F32, "f")])

    dq_sb, dk_sb, dv_sb = _sb_bwd(proj, 0, do_sb, st_sb, jmin_sb, BQ)
    dq_fx, dk_fx, dv_fx, dc, dcq = _fox_bwd(proj, 12, do_fx, o_fx, st_fx, c_col, c_row, jstart_fx, BQ)
    dc = dc[:, :2, :] + dcq[:, :, :2].transpose(0, 2, 1)
    dfl, dbf = _fgate_bwd(dc.reshape(N_FOX, S), lf)
    dp_sb = jnp.concatenate([dq_sb, dk_sb, dv_sb], axis=1).astype(BF16)
    dp_fx = jnp.concatenate([dq_fx, dk_fx, dv_fx], axis=1).astype(BF16)

    d_wsb = _matmul(x2, dp_sb, mode="tn", name="dw_in_sb", tm=D_MODEL, tn=QKV_W // 2, tk=TM, outs=[F32])
    d_wfx = _matmul(x2, dp_fx, mode="tn", name="dw_in_fx", tm=D_MODEL, tn=QKV_W // 2, tk=TM, outs=[F32])
    d_wft = _matmul(dfl, x2, mode="nn", name="dw_in_f", tm=N_FOX, tn=D_MODEL, tk=TM, outs=[F32])
    dx = _matmul(dp_sb, w_sb, mode="nt", name="dx_sb", tm=TM, tn=D_MODEL, tk=QKV_W // 2, outs=[F32],
                 extras=[(du1, (tm_e, D_MODEL), _tile_ij)], epilogue=lambda acc, e: (ALPHA * e + acc,))
    dx = _matmul(dp_fx, w_fx, mode="nt", name="dx_fx", tm=TM, tn=D_MODEL, tk=QKV_W // 2, outs=[F32],
                 extras=[(dx, (tm_e, D_MODEL), _tile_ij)], epilogue=lambda acc, e: (e + acc,))
    dx = _matmul(dfl, wft, mode="tn", name="dx_f", tm=TM, tn=D_MODEL, tk=N_FOX, outs=[F32],
                 extras=[(dx, (tm_e, D_MODEL), _tile_ij)], epilogue=lambda acc, e: (e + acc,))

    d_wi = jnp.concatenate([d_wsb, d_wfx, d_wft.T], axis=1)
    d_wgu = jnp.concatenate([d_wg, d_wu], axis=1)
    parts = [d_wi.reshape(D_MODEL, 4, in_w).transpose(1, 0, 2).astype(BF16),
             d_wo.reshape(4, D_MODEL // 4, D_MODEL).astype(BF16),
             d_wgu.reshape(D_MODEL, 4, gu_w).transpose(1, 0, 2).astype(BF16),
             d_wd.reshape(4, D_FF // 4, D_MODEL).astype(BF16)]
    got = _exchange(parts, True)
    big = {}
    for nm, p, w, m, v, tr in (("w_in", got[0], w_in, m_w_in, v_w_in, 256), ("w_out", got[1], w_out, m_w_out, v_w_out, 256),
                               ("w_gate_up", got[2], w_gate_up, m_w_gate_up, v_w_gate_up, 128),
                               ("w_down", got[3], w_down, m_w_down, v_w_down, 176)):
        big[nm] = [r[None] for r in _sum_adamw(p, w[0], m[0], v[0], "sum_adamw_" + nm, tr)]

    small = acc_ln2 + acc_ln1 + acc_rms
    small = small + jnp.pad(dbf.reshape(1, N_FOX), ((5, 2), (0, D_MODEL - N_FOX)))
    (small_all,) = _exchange([small], False)
    sw = _pack_small(ln1_g, ln1_b, ln2_g, ln2_b, g_sb, g_fox, b_f)
    sm = _pack_small(m_ln1_g, m_ln1_b, m_ln2_g, m_ln2_b, m_g_sb, m_g_fox, m_b_f)
    sv = _pack_small(v_ln1_g, v_ln1_b, v_ln2_g, v_ln2_b, v_g_sb, v_g_fox, v_b_f)
    sg, sd, snm, snv, loss_blk = _sum_adamw_small(small_all, sw, sm, sv)
    sg, sd, snm, snv = _unpack_small(sg), _unpack_small(sd), _unpack_small(snm), _unpack_small(snv)

    names = ["w_in", "b_f", "g_sb", "g_fox", "w_out", "ln1_g", "ln1_b", "ln2_g", "ln2_b", "w_gate_up", "w_down"]
    outs = [loss_blk[0, 0], dx.reshape(1, S, D_MODEL)]
    for k, table in enumerate((sg, sd, snm, snv)):
        outs += [big[n][k] if n in big else table[n] for n in names]
    return tuple(outs)
```

```python
import functools

import numpy as np
import jax
import jax.numpy as jnp
from jax import lax
from jax.experimental import pallas as pl
from jax.experimental.pallas import tpu as pltpu

F32 = jnp.float32
BF16 = jnp.bfloat16

D_MODEL = 1024
HEAD_DIM = 64
LANES = 128
N_PAIRS = 4
GROUP_W = 512
QKV_W = 3072
D_FF = 2816
N_FOX = 8
ALPHA = 2.0 ** 0.25
LN_EPS = 1e-5
RMS_EPS = 1e-6
SCALE = HEAD_DIM ** -0.5
NEG_BIG = -1e30
FOX_SKIP = 30.0
SB_STOP = -105.0
ADAM_LR, ADAM_B1, ADAM_B2, ADAM_EPS, ADAM_WD, ADAM_STEP = 0.001, 0.9, 0.999, 1e-08, 0.01, 10
ATTN_BLOCK = 256
VMEM_BIG = 56 * 1024 * 1024
MESH = pl.DeviceIdType.MESH

_NN = (((1,), (0,)), ((), ()))
_NT = (((1,), (1,)), ((), ()))
_TN = (((0,), (0,)), ((), ()))


def _dot(a, b, dims=_NN):
    return lax.dot_general(a, b, dims, preferred_element_type=F32)


def _split_dot(x, t):
    hi = x.astype(BF16)
    lo = (x - hi.astype(F32)).astype(BF16)
    return _dot(hi, t) + _dot(lo, t)


def _softplus(z):
    return jnp.maximum(z, 0.0) + jnp.log1p(jnp.exp(-jnp.abs(z)))


def _col(v, h):
    lane = lax.broadcasted_iota(jnp.int32, v.shape, 1)
    return jnp.sum(jnp.where(lane == h, v, 0.0), axis=1, keepdims=True)


def _two_sum(hi, lo, b):
    s = hi + b
    bb = s - hi
    err = (hi - (s - bb)) + (b - bb)
    return s, lo + err


def _params(vmem=None):
    return pltpu.CompilerParams(vmem_limit_bytes=vmem) if vmem else None


def _matmul(a, b, *, mode, name, tm, tn, tk, outs, extras=(), epilogue=None, vmem=None):
    if mode == "nn":
        (M, K), (_, N) = a.shape, b.shape
    elif mode == "nt":
        (M, K), (N, _) = a.shape, b.shape
    else:
        (K, M), (_, N) = a.shape, b.shape
    tm, tn, tk = min(tm, M), min(tn, N), min(tk, K)
    assert M % tm == 0 and N % tn == 0 and K % tk == 0, (name, M, N, K, tm, tn, tk)
    nk = K // tk
    dims = {"nn": _NN, "nt": _NT, "tn": _TN}[mode]
    if mode == "tn":
        a_spec = pl.BlockSpec((tk, tm), lambda i, j, k: (k, i))
    else:
        a_spec = pl.BlockSpec((tm, tk), lambda i, j, k: (i, k))
    if mode == "nt":
        b_spec = pl.BlockSpec((tn, tk), lambda i, j, k: (j, k))
    else:
        b_spec = pl.BlockSpec((tk, tn), lambda i, j, k: (k, j))
    ex_specs = [pl.BlockSpec(bs, (lambda i, j, k, f=f: f(i, j))) for (_, bs, f) in extras]
    ne, no = len(extras), len(outs)
    if epilogue is None:
        epilogue = lambda acc: (acc,)

    def body(a_ref, b_ref, *rest):
        ex_refs, out_refs, acc = rest[:ne], rest[ne:ne + no], rest[-1]
        k = pl.program_id(2)

        @pl.when(k == 0)
        def _():
            acc[...] = jnp.zeros_like(acc)

        acc[...] += _dot(a_ref[...].astype(BF16), b_ref[...].astype(BF16), dims)

        @pl.when(k == nk - 1)
        def _():
            res = epilogue(acc[...], *[e[...] for e in ex_refs])
            for r, o in zip(res, out_refs):
                o[...] = r.astype(o.dtype)

    res = pl.pallas_call(
        body, name=name, grid=(M // tm, N // tn, nk),
        in_specs=[a_spec, b_spec] + ex_specs,
        out_specs=[pl.BlockSpec((tm, tn), lambda i, j, k: (i, j)) for _ in outs],
        out_shape=[jax.ShapeDtypeStruct((M, N), d) for d in outs],
        scratch_shapes=[pltpu.VMEM((tm, tn), F32)],
        compiler_params=_params(vmem),
    )(a, b, *[e[0] for e in extras])
    return res[0] if no == 1 else res


def _tile_ij(i, j):
    return (i, j)


def _rowwise(fn, name, rows, tm, ins, outs, vmem=None):
    tm = min(tm, rows)
    assert rows % tm == 0

    def spec(shape, kind):
        if kind == "t":
            return pl.BlockSpec((tm,) + tuple(shape[1:]), lambda i: (i,) + (0,) * (len(shape) - 1))
        return pl.BlockSpec(tuple(shape), lambda i: (0,) * len(shape))

    def body(*refs):
        fn(pl.program_id(0), *refs)

    return pl.pallas_call(
        body, name=name, grid=(rows // tm,),
        in_specs=[spec(a.shape, k) for a, k in ins],
        out_specs=[spec(s, k) for s, _, k in outs],
        out_shape=[jax.ShapeDtypeStruct(s, d) for s, d, _ in outs],
        compiler_params=_params(vmem),
    )(*[a for a, _ in ins])


def _ln_stats(u):
    mu = jnp.mean(u, axis=-1, keepdims=True)
    d = u - mu
    var = jnp.mean(d * d, axis=-1, keepdims=True)
    r = lax.rsqrt(var + LN_EPS)
    return d * r, r


def _ln_bwd(dh, xh, r, g):
    dxh = dh * g
    m1 = jnp.mean(dxh, axis=-1, keepdims=True)
    m2 = jnp.mean(dxh * xh, axis=-1, keepdims=True)
    return r * (dxh - m1 - xh * m2)


def _acc_rows(i, ref, rows):
    @pl.when(i == 0)
    def _():
        ref[...] = jnp.zeros_like(ref)
    for r, v in rows.items():
        ref[pl.ds(r, 1), :] += v


def _head_sums(v, bd):
    return _split_dot(v, bd)


def _fgate_fwd(x, wft, bf_col, tm):
    S = x.shape[0]
    tm = min(tm, S)

    def body(wft_ref, bf_ref, x_ref, lf_ref):
        f = _dot(wft_ref[...], x_ref[...].astype(BF16), _NT) + bf_ref[...]
        lf_ref[...] = -_softplus(-f)

    return pl.pallas_call(
        body, name="fgate_fwd", grid=(S // tm,),
        in_specs=[pl.BlockSpec((N_FOX, D_MODEL), lambda i: (0, 0)), pl.BlockSpec((N_FOX, 1), lambda i: (0, 0)),
                  pl.BlockSpec((tm, D_MODEL), lambda i: (i, 0))],
        out_specs=pl.BlockSpec((N_FOX, tm), lambda i: (0, i)),
        out_shape=jax.ShapeDtypeStruct((N_FOX, S), F32),
    )(wft, bf_col, x)


def _chunk_scan(v, reverse):
    lane = lax.broadcasted_iota(jnp.int32, v.shape, 1)
    sh = 1
    while sh < LANES:
        if reverse:
            v = v + jnp.where(lane < LANES - sh, pltpu.roll(v, LANES - sh, 1), 0.0)
        else:
            v = v + jnp.where(lane >= sh, pltpu.roll(v, sh, 1), 0.0)
        sh *= 2
    return v


def _cumsum_fwd(lf):
    n, S = lf.shape
    nc = S // LANES

    def body(lf_ref, c_ref):
        def step(ci, carry):
            sl = pl.ds(pl.multiple_of(ci * LANES, LANES), LANES)
            v = _chunk_scan(lf_ref[:, sl], False) + carry
            c_ref[:, sl] = v
            return _col(v, LANES - 1)
        lax.fori_loop(0, nc, step, jnp.zeros((n, 1), F32))

    return pl.pallas_call(body, name="cumsum_fwd", out_shape=jax.ShapeDtypeStruct((n, S), F32))(lf)


def _fgate_bwd(dc, lf):
    n, S = dc.shape
    nc = S // LANES

    def body(dc_ref, lf_ref, dfl_ref, dbf_ref):
        def step(t, carry):
            car, tot = carry
            ci = nc - 1 - t
            sl = pl.ds(pl.multiple_of(ci * LANES, LANES), LANES)
            dlf = _chunk_scan(dc_ref[:, sl], True) + car
            dfl = dlf * (1.0 - jnp.exp(lf_ref[:, sl]))
            dfl_ref[:, sl] = dfl
            return _col(dlf, 0), tot + jnp.sum(dfl, axis=1, keepdims=True)
        _, tot = lax.fori_loop(0, nc, step, (jnp.zeros((n, 1), F32), jnp.zeros((n, 1), F32)))
        dbf_ref[...] = tot

    return pl.pallas_call(body, name="fgate_bwd",
                          out_shape=[jax.ShapeDtypeStruct((n, S), F32), jax.ShapeDtypeStruct((n, 1), F32)])(dc, lf)


def _tri_matrices(b):
    r = np.arange(b)
    tfwd = (r[:, None] <= r[None, :]).astype(np.float32)
    return jnp.asarray(tfwd, BF16), jnp.asarray(tfwd.T, BF16)


def _kv_copies(kv_hbm, kbuf, vbuf, sems, pair_col, bq, j, slot):
    rows = pl.ds(pl.multiple_of(j * bq, bq), bq)

    def cols(c):
        return pl.ds(pl.multiple_of((pair_col + c) * LANES, LANES), LANES)

    return (pltpu.make_async_copy(kv_hbm.at[rows, cols(4)], kbuf.at[slot], sems.at[0, slot]),
            pltpu.make_async_copy(kv_hbm.at[rows, cols(8)], vbuf.at[slot], sems.at[1, slot]))


def _masked_pair(v, lane_is_a, scale=1.0):
    v = v.astype(F32) * scale
    return jnp.where(lane_is_a, v, 0.0).astype(BF16), jnp.where(lane_is_a, 0.0, v).astype(BF16)


def _sb_fwd(proj, col0, bq):
    S = proj.shape[0]
    bq = min(bq, S)
    nq = S // bq
    _, trev = _tri_matrices(bq)

    def body(q_ref, kv_hbm, trev_ref, o_ref, st_ref, jmin_ref, acc_a, acc_b, qa, qb, rs, kbuf, vbuf, sems):
        p, i = pl.program_id(0), pl.program_id(1)
        is_a = lax.broadcasted_iota(jnp.int32, (bq, LANES), 1) < HEAD_DIM
        acc_a[...] = jnp.zeros_like(acc_a)
        acc_b[...] = jnp.zeros_like(acc_b)
        rs[...] = jnp.zeros_like(rs)
        qa[...], qb[...] = _masked_pair(q_ref[...], is_a, SCALE)
        fetch = functools.partial(_kv_copies, kv_hbm, kbuf, vbuf, sems, col0 + p, bq)

        def tile(slot, masked):
            k, v, trev_m = kbuf[slot], vbuf[slot], trev_ref[...]
            if masked:
                tri = lax.broadcasted_iota(jnp.int32, (bq, bq), 0) > lax.broadcasted_iota(jnp.int32, (bq, bq), 1)
            for h, (qh, acc) in enumerate(((qa, acc_a), (qb, acc_b))):
                z = _dot(qh[...], k, _NT)
                lk = -_softplus(z)
                if masked:
                    lk = jnp.where(tri, lk, 0.0)
                r_hi, r_lo = rs[2 * h], rs[2 * h + 1]
                w = jnp.exp(z + _split_dot(lk, trev_m) + (r_hi + r_lo))
                if masked:
                    w = jnp.where(tri, w, 0.0)
                acc[...] += _dot(w.astype(BF16), v)
                rs[2 * h], rs[2 * h + 1] = _two_sum(r_hi, r_lo, jnp.sum(lk, axis=1, keepdims=True))

        for cp in fetch(i, 0):
            cp.start()

        def step(carry):
            j, _ = carry
            slot = lax.rem(i - j, 2)
            for cp in fetch(j, slot):
                cp.wait()

            @pl.when(j > 0)
            def _():
                for cp in fetch(j - 1, 1 - slot):
                    cp.start()

            pl.when(j == i)(functools.partial(tile, slot, True))
            pl.when(j < i)(functools.partial(tile, slot, False))
            live = jnp.max(jnp.maximum(rs[0], rs[2])) > SB_STOP
            return j - 1, live.astype(jnp.int32)

        j_end, _ = lax.while_loop(lambda c: jnp.logical_and(c[0] >= 0, c[1] > 0), step, (i, jnp.int32(1)))

        @pl.when(j_end >= 0)
        def _():
            for cp in fetch(j_end, lax.rem(i - j_end, 2)):
                cp.wait()

        jmin_ref[p, i] = j_end + 1
        o_ref[...] = jnp.where(is_a, acc_a[...], acc_b[...])
        lane8 = lax.broadcasted_iota(jnp.int32, (bq, 8), 1)
        st = jnp.zeros((bq, 8), F32)
        for c, src in enumerate((0, 2, 1, 3)):
            st = jnp.where(lane8 == c, rs[src], st)
        st_ref[0] = st

    return pl.pallas_call(
        body, name="sb_fwd", grid=(N_PAIRS, nq),
        in_specs=[pl.BlockSpec((bq, LANES), lambda p, i: (i, col0 + p)),
                  pl.BlockSpec(memory_space=pl.ANY),
                  pl.BlockSpec((bq, bq), lambda p, i: (0, 0))],
        out_specs=[pl.BlockSpec((bq, LANES), lambda p, i: (i, p)),
                   pl.BlockSpec((1, bq, 8), lambda p, i: (p, i, 0)),
                   pl.BlockSpec(memory_space=pltpu.SMEM)],
        out_shape=[jax.ShapeDtypeStruct((S, GROUP_W), F32), jax.ShapeDtypeStruct((N_PAIRS, S, 8), F32),
                   jax.ShapeDtypeStruct((N_PAIRS, nq), jnp.int32)],
        scratch_shapes=[pltpu.VMEM((bq, LANES), F32), pltpu.VMEM((bq, LANES), F32),
                        pltpu.VMEM((bq, LANES), BF16), pltpu.VMEM((bq, LANES), BF16),
                        pltpu.VMEM((4, bq, 1), F32),
                        pltpu.VMEM((2, bq, LANES), BF16), pltpu.VMEM((2, bq, LANES), BF16),
                        pltpu.SemaphoreType.DMA((2, 2))],
    )(proj, proj, trev)


def _sb_bwd(proj, col0, do, st, jmin, bq):
    S = proj.shape[0]
    bq = min(bq, S)
    nq = S // bq
    tfwd, trev = _tri_matrices(bq)

    def body(jmin_ref, q_ref, kv_hbm, do_ref, st_ref, tfwd_ref, trev_ref,
             dq_ref, dk_ref, dv_ref, dq_a, dq_b, qa, qb, doa, dob, rs, kbuf, vbuf, sems):
        p, i = pl.program_id(0), pl.program_id(1)
        is_a = lax.broadcasted_iota(jnp.int32, (bq, LANES), 1) < HEAD_DIM

        @pl.when(i == 0)
        def _():
            dk_ref[...] = jnp.zeros_like(dk_ref)
            dv_ref[...] = jnp.zeros_like(dv_ref)

        dq_a[...] = jnp.zeros_like(dq_a)
        dq_b[...] = jnp.zeros_like(dq_b)
        rs[...] = jnp.zeros_like(rs)
        st_v = st_ref[0]
        for h in range(2):
            rs[6 + 2 * h], rs[7 + 2 * h] = _col(st_v, h), _col(st_v, 2 + h)
        qa[...], qb[...] = _masked_pair(q_ref[...], is_a, SCALE)
        doa[...], dob[...] = _masked_pair(do_ref[...], is_a)
        fetch = functools.partial(_kv_copies, kv_hbm, kbuf, vbuf, sems, col0 + p, bq)
        j0 = jmin_ref[p, i]

        def tile(j, slot, masked):
            k, v = kbuf[slot], vbuf[slot]
            tfwd_m, trev_m = tfwd_ref[...], trev_ref[...]
            if masked:
                tri = lax.broadcasted_iota(jnp.int32, (bq, bq), 0) > lax.broadcasted_iota(jnp.int32, (bq, bq), 1)
            dzs, ws = [], []
            for h, (qh, doh, dq) in enumerate(((qa, doa, dq_a), (qb, dob, dq_b))):
                z = _dot(qh[...], k, _NT)
                lk = -_softplus(z)
                if masked:
                    lk = jnp.where(tri, lk, 0.0)
                p_hi, p_lo = _two_sum(rs[3 * h], rs[3 * h + 1], jnp.sum(lk, axis=1, keepdims=True))
                rs[3 * h], rs[3 * h + 1] = p_hi, p_lo
                right = (rs[6 + 2 * h] - p_hi) + (rs[7 + 2 * h] - p_lo)
                w = jnp.exp(z + _split_dot(lk, trev_m) + right)
                if masked:
                    w = jnp.where(tri, w, 0.0)
                g = _dot(doh[...], v, _NT) * w
                g_left = rs[3 * h + 2]
                dz = g - jnp.exp(z + lk) * (_split_dot(g, tfwd_m) + g_left)
                if masked:
                    dz = jnp.where(tri, dz, 0.0)
                rs[3 * h + 2] = g_left + jnp.sum(g, axis=1, keepdims=True)
                dzb = dz.astype(BF16)
                dq[...] += _dot(dzb, k)
                dzs.append(dzb)
                ws.append(w.astype(BF16))
            rows = pl.ds(pl.multiple_of(j * bq, bq), bq)
            dk_ref[rows, :] += _dot(dzs[0], qa[...], _TN) + _dot(dzs[1], qb[...], _TN)
            dv_ref[rows, :] += _dot(ws[0], doa[...], _TN) + _dot(ws[1], dob[...], _TN)

        _walk_up(fetch, j0, i, tile)
        dq_ref[...] = jnp.where(is_a, dq_a[...], dq_b[...]) * SCALE

    grid_spec = pltpu.PrefetchScalarGridSpec(
        num_scalar_prefetch=1, grid=(N_PAIRS, nq),
        in_specs=[pl.BlockSpec((bq, LANES), lambda p, i, jm: (i, col0 + p)),
                  pl.BlockSpec(memory_space=pl.ANY),
                  pl.BlockSpec((bq, LANES), lambda p, i, jm: (i, p)),
                  pl.BlockSpec((1, bq, 8), lambda p, i, jm: (p, i, 0)),
                  pl.BlockSpec((bq, bq), lambda p, i, jm: (0, 0)),
                  pl.BlockSpec((bq, bq), lambda p, i, jm: (0, 0))],
        out_specs=[pl.BlockSpec((bq, LANES), lambda p, i, jm: (i, p)),
                   pl.BlockSpec((S, LANES), lambda p, i, jm: (0, p)),
                   pl.BlockSpec((S, LANES), lambda p, i, jm: (0, p))],
        scratch_shapes=[pltpu.VMEM((bq, LANES), F32), pltpu.VMEM((bq, LANES), F32)]
        + [pltpu.VMEM((bq, LANES), BF16)] * 4 + [pltpu.VMEM((10, bq, 1), F32)]
        + [pltpu.VMEM((2, bq, LANES), BF16), pltpu.VMEM((2, bq, LANES), BF16), pltpu.SemaphoreType.DMA((2, 2))])
    return pl.pallas_call(
        body, name="sb_bwd", grid_spec=grid_spec,
        out_shape=[jax.ShapeDtypeStruct((S, GROUP_W), F32)] * 3,
        compiler_params=_params(VMEM_BIG),
    )(jmin, proj, proj, do, st, tfwd, trev)


def _walk_up(fetch, j0, i, tile):
    for cp in fetch(j0, 0):
        cp.start()

    def step(j, carry):
        slot = lax.rem(j - j0, 2)
        for cp in fetch(j, slot):
            cp.wait()

        @pl.when(j < i)
        def _():
            for cp in fetch(j + 1, 1 - slot):
                cp.start()

        pl.when(j == i)(functools.partial(tile, j, slot, True))
        pl.when(j < i)(functools.partial(tile, j, slot, False))
        return carry

    lax.fori_loop(j0, i + 1, step, 0)


def _fox_start_blocks(proj, col0, c, bq):
    S = proj.shape[0]
    nq = S // bq

    def norms(first):
        t = proj[:, first * LANES:(first + N_PAIRS) * LANES].astype(F32).reshape(S, 2 * N_PAIRS, HEAD_DIM)
        return jnp.sqrt(jnp.sum(t * t, axis=-1))

    qmax = norms(col0).reshape(nq, bq, 2 * N_PAIRS).max(axis=1)
    kmax = norms(col0 + 4).max(axis=0)
    zb = (2.0 * SCALE) * qmax * kmax[None, :]
    c_first = c[:, ::bq].T
    c_last = c[:, bq - 1::bq].T
    live = (zb + c_first)[:, None, :] - c_last[None, :, :] >= -FOX_SKIP
    live = live.reshape(nq, nq, N_PAIRS, 2).any(axis=-1)
    first = jnp.where(live.any(axis=1), jnp.argmax(live, axis=1), nq)
    return jnp.minimum(first, jnp.arange(nq)[:, None]).T.astype(jnp.int32)


def _fox_fwd(proj, col0, c_col, c_row, jstart, bq):
    S = proj.shape[0]
    bq = min(bq, S)
    nq = S // bq

    def body(js_ref, q_ref, kv_hbm, cc_ref, cr_ref, o_ref, st_ref, acc_a, acc_b, qa, qb, ml, kbuf, vbuf, sems):
        p, i = pl.program_id(0), pl.program_id(1)
        is_a = lax.broadcasted_iota(jnp.int32, (bq, LANES), 1) < HEAD_DIM
        acc_a[...] = jnp.zeros_like(acc_a)
        acc_b[...] = jnp.zeros_like(acc_b)
        ml[0] = jnp.full((bq, 1), NEG_BIG, F32)
        ml[2] = jnp.full((bq, 1), NEG_BIG, F32)
        ml[1] = jnp.zeros((bq, 1), F32)
        ml[3] = jnp.zeros((bq, 1), F32)
        cc = cc_ref[0]
        ml[4], ml[5] = _col(cc, 0), _col(cc, 1)
        qa[...], qb[...] = _masked_pair(q_ref[...], is_a, SCALE)

        def tile(j, slot, masked):
            k, v = kbuf[slot], vbuf[slot]
            cols = pl.ds(pl.multiple_of(j * bq, bq), bq)
            if masked:
                tri = lax.broadcasted_iota(jnp.int32, (bq, bq), 0) >= lax.broadcasted_iota(jnp.int32, (bq, bq), 1)
            for h, (qh, acc) in enumerate(((qa, acc_a), (qb, acc_b))):
                s = _dot(qh[...], k, _NT) - cr_ref[0, pl.ds(h, 1), cols]
                if masked:
                    s = jnp.where(tri, s, NEG_BIG)
                m_prev, l_prev, cq = ml[2 * h], ml[2 * h + 1], ml[4 + h]
                m_new = jnp.maximum(m_prev, jnp.max(s, axis=1, keepdims=True) + cq)
                a = jnp.exp(m_prev - m_new)
                p = jnp.exp(s - (m_new - cq))
                ml[2 * h] = m_new
                ml[2 * h + 1] = a * l_prev + jnp.sum(p, axis=1, keepdims=True)
                acc[...] = a * acc[...] + _dot(p.astype(BF16), v)

        _walk_up(functools.partial(_kv_copies, kv_hbm, kbuf, vbuf, sems, col0 + p, bq), js_ref[p, i], i, tile)
        o_ref[...] = jnp.where(is_a, acc_a[...] / ml[1], acc_b[...] / ml[3])
        lane8 = lax.broadcasted_iota(jnp.int32, (bq, 8), 1)
        st = jnp.where(lane8 == 0, ml[0] + jnp.log(ml[1]), 0.0)
        st_ref[0] = jnp.where(lane8 == 1, ml[2] + jnp.log(ml[3]), st)

    grid_spec = pltpu.PrefetchScalarGridSpec(
        num_scalar_prefetch=1, grid=(N_PAIRS, nq),
        in_specs=[pl.BlockSpec((bq, LANES), lambda p, i, js: (i, col0 + p)),
                  pl.BlockSpec(memory_space=pl.ANY),
                  pl.BlockSpec((1, bq, 8), lambda p, i, js: (p, i, 0)),
                  pl.BlockSpec((1, 8, S), lambda p, i, js: (p, 0, 0))],
        out_specs=[pl.BlockSpec((bq, LANES), lambda p, i, js: (i, p)),
                   pl.BlockSpec((1, bq, 8), lambda p, i, js: (p, i, 0))],
        scratch_shapes=[pltpu.VMEM((bq, LANES), F32), pltpu.VMEM((bq, LANES), F32),
                        pltpu.VMEM((bq, LANES), BF16), pltpu.VMEM((bq, LANES), BF16),
                        pltpu.VMEM((6, bq, 1), F32),
                        pltpu.VMEM((2, bq, LANES), BF16), pltpu.VMEM((2, bq, LANES), BF16),
                        pltpu.SemaphoreType.DMA((2, 2))])
    return pl.pallas_call(
        body, name="fox_fwd", grid_spec=grid_spec,
        out_shape=[jax.ShapeDtypeStruct((S, GROUP_W), F32), jax.ShapeDtypeStruct((N_PAIRS, S, 8), F32)],
    )(jstart, proj, proj, c_col, c_row)


def _fox_bwd(proj, col0, do, o, st, c_col, c_row, jstart, bq):
    S = proj.shape[0]
    bq = min(bq, S)
    nq = S // bq

    def body(js_ref, q_ref, kv_hbm, do_ref, o_ref, st_ref, cc_ref, cr_ref,
             dq_ref, dk_ref, dv_ref, dc_ref, dcq_ref, dq_a, dq_b, qa, qb, doa, dob, dd, kbuf, vbuf, sems):
        p, i = pl.program_id(0), pl.program_id(1)
        is_a = lax.broadcasted_iota(jnp.int32, (bq, LANES), 1) < HEAD_DIM

        @pl.when(i == 0)
        def _():
            dk_ref[...] = jnp.zeros_like(dk_ref)
            dv_ref[...] = jnp.zeros_like(dv_ref)
            dc_ref[...] = jnp.zeros_like(dc_ref)

        dq_a[...] = jnp.zeros_like(dq_a)
        dq_b[...] = jnp.zeros_like(dq_b)
        qa[...], qb[...] = _masked_pair(q_ref[...], is_a, SCALE)
        dov = do_ref[...]
        doa[...], dob[...] = _masked_pair(dov, is_a)
        prod = dov * o_ref[...]
        dd[0] = jnp.sum(jnp.where(is_a, prod, 0.0), axis=1, keepdims=True)
        dd[1] = jnp.sum(jnp.where(is_a, 0.0, prod), axis=1, keepdims=True)
        dd[2] = jnp.zeros((bq, 1), F32)
        dd[3] = jnp.zeros((bq, 1), F32)
        cc, st_v = cc_ref[0], st_ref[0]
        dd[4], dd[5] = _col(cc, 0) - _col(st_v, 0), _col(cc, 1) - _col(st_v, 1)

        def tile(j, slot, masked):
            k, v = kbuf[slot], vbuf[slot]
            if masked:
                tri = lax.broadcasted_iota(jnp.int32, (bq, bq), 0) >= lax.broadcasted_iota(jnp.int32, (bq, bq), 1)
            cols = pl.ds(pl.multiple_of(j * bq, bq), bq)
            dss, ps = [], []
            for h, (qh, doh, dq) in enumerate(((qa, doa, dq_a), (qb, dob, dq_b))):
                p = jnp.exp(_dot(qh[...], k, _NT) - cr_ref[0, pl.ds(h, 1), cols] + dd[4 + h])
                if masked:
                    p = jnp.where(tri, p, 0.0)
                ds = p * (_dot(doh[...], v, _NT) - dd[h])
                dc_ref[0, pl.ds(h, 1), cols] -= jnp.sum(ds, axis=0, keepdims=True)
                dd[2 + h] += jnp.sum(ds, axis=1, keepdims=True)
                dsb = ds.astype(BF16)
                dq[...] += _dot(dsb, k)
                dss.append(dsb)
                ps.append(p.astype(BF16))
            dk_ref[cols, :] += _dot(dss[0], qa[...], _TN) + _dot(dss[1], qb[...], _TN)
            dv_ref[cols, :] += _dot(ps[0], doa[...], _TN) + _dot(ps[1], dob[...], _TN)

        _walk_up(functools.partial(_kv_copies, kv_hbm, kbuf, vbuf, sems, col0 + p, bq), js_ref[p, i], i, tile)
        dq_ref[...] = jnp.where(is_a, dq_a[...], dq_b[...]) * SCALE
        lane8 = lax.broadcasted_iota(jnp.int32, (bq, 8), 1)
        dcq_ref[0] = jnp.where(lane8 == 0, dd[2], jnp.where(lane8 == 1, dd[3], 0.0))

    grid_spec = pltpu.PrefetchScalarGridSpec(
        num_scalar_prefetch=1, grid=(N_PAIRS, nq),
        in_specs=[pl.BlockSpec((bq, LANES), lambda p, i, js: (i, col0 + p)),
                  pl.BlockSpec(memory_space=pl.ANY),
                  pl.BlockSpec((bq, LANES), lambda p, i, js: (i, p)),
                  pl.BlockSpec((bq, LANES), lambda p, i, js: (i, p)),
                  pl.BlockSpec((1, bq, 8), lambda p, i, js: (p, i, 0)),
                  pl.BlockSpec((1, bq, 8), lambda p, i, js: (p, i, 0)),
                  pl.BlockSpec((1, 8, S), lambda p, i, js: (p, 0, 0))],
        out_specs=[pl.BlockSpec((bq, LANES), lambda p, i, js: (i, p)),
                   pl.BlockSpec((S, LANES), lambda p, i, js: (0, p)),
                   pl.BlockSpec((S, LANES), lambda p, i, js: (0, p)),
                   pl.BlockSpec((1, 8, S), lambda p, i, js: (p, 0, 0)),
                   pl.BlockSpec((1, bq, 8), lambda p, i, js: (p, i, 0))],
        scratch_shapes=[pltpu.VMEM((bq, LANES), F32), pltpu.VMEM((bq, LANES), F32)]
        + [pltpu.VMEM((bq, LANES), BF16)] * 4 + [pltpu.VMEM((6, bq, 1), F32)]
        + [pltpu.VMEM((2, bq, LANES), BF16), pltpu.VMEM((2, bq, LANES), BF16), pltpu.SemaphoreType.DMA((2, 2))])
    return pl.pallas_call(
        body, name="fox_bwd", grid_spec=grid_spec,
        out_shape=[jax.ShapeDtypeStruct((S, GROUP_W), F32)] * 3
        + [jax.ShapeDtypeStruct((N_PAIRS, 8, S), F32), jax.ShapeDtypeStruct((N_PAIRS, S, 8), F32)],
        compiler_params=_params(VMEM_BIG),
    )(jstart, proj, proj, do, o, st, c_col, c_row)


_HBM = pl.BlockSpec(memory_space=pltpu.HBM)


def _coords():
    return lax.axis_index("x"), lax.axis_index("y"), lax.axis_index("c")


def _allgather_chips(shards):
    n = len(shards)

    def body(*refs):
        ins, outs = refs[:n], refs[n:2 * n]
        send_sems, recv_sems, loc_sems = refs[2 * n:]
        x, y, c = _coords()
        mine = 2 * x + y
        chips = [(1 - x, y), (x, 1 - y), (1 - x, 1 - y)]
        local = [pltpu.make_async_copy(ins[w], outs[w].at[mine], loc_sems.at[w]) for w in range(n)]
        for cp in local:
            cp.start()

        def copy(w, r, slab, to):
            return pltpu.make_async_remote_copy(
                src_ref=ins[w], dst_ref=outs[w].at[slab], send_sem=send_sems.at[3 * w + r],
                recv_sem=recv_sems.at[3 * w + r], device_id=to, device_id_type=MESH)

        sends = [copy(w, r, mine, (cx, cy, c)) for w in range(n) for r, (cx, cy) in enumerate(chips)]
        for cp in sends:
            cp.start()
        for w in range(n):
            for r, (cx, cy) in enumerate(chips):
                copy(w, r, 2 * cx + cy, (cx, cy, c)).wait_recv()
        for cp in sends:
            cp.wait_send()
        for cp in local:
            cp.wait()

    return pl.pallas_call(
        body, name="allgather_weights",
        in_specs=[_HBM] * n, out_specs=[_HBM] * n,
        out_shape=[jax.ShapeDtypeStruct((4,) + s.shape, s.dtype) for s in shards],
        scratch_shapes=[pltpu.SemaphoreType.DMA((3 * n,)), pltpu.SemaphoreType.DMA((3 * n,)),
                        pltpu.SemaphoreType.DMA((n,))],
    )(*shards)


def _exchange(parts, per_chip):
    n = len(parts)

    def body(*refs):
        ins, outs = refs[:n], refs[n:2 * n]
        send_sems, recv_sems, loc_sems = refs[2 * n:]
        x, y, c = _coords()
        me = 4 * x + 2 * y + c
        peers = [(x ^ fx, y ^ fy, c ^ fc) for fx in (0, 1) for fy in (0, 1) for fc in (0, 1)][1:]

        def src(w, dev):
            return ins[w].at[2 * dev[0] + dev[1]] if per_chip else ins[w]

        local = [pltpu.make_async_copy(src(w, (x, y, c)), outs[w].at[me], loc_sems.at[w]) for w in range(n)]
        for cp in local:
            cp.start()

        def copy(w, r, source, slab, to):
            return pltpu.make_async_remote_copy(
                src_ref=source, dst_ref=outs[w].at[slab], send_sem=send_sems.at[7 * w + r],
                recv_sem=recv_sems.at[7 * w + r], device_id=to, device_id_type=MESH)

        sends = [copy(w, r, src(w, dev), me, dev) for w in range(n) for r, dev in enumerate(peers)]
        for cp in sends:
            cp.start()
        for w in range(n):
            for r, dev in enumerate(peers):
                copy(w, r, src(w, dev), 4 * dev[0] + 2 * dev[1] + dev[2], dev).wait_recv()
        for cp in sends:
            cp.wait_send()
        for cp in local:
            cp.wait()

    return pl.pallas_call(
        body, name="exchange_per_chip" if per_chip else "exchange_all",
        in_specs=[_HBM] * n, out_specs=[_HBM] * n,
        out_shape=[jax.ShapeDtypeStruct((8,) + p.shape[(1 if per_chip else 0):], p.dtype) for p in parts],
        scratch_shapes=[pltpu.SemaphoreType.DMA((7 * n,)), pltpu.SemaphoreType.DMA((7 * n,)),
                        pltpu.SemaphoreType.DMA((n,))],
    )(*parts)


def _adamw(w, g, m, v):
    m = ADAM_B1 * m + (1.0 - ADAM_B1) * g
    v = ADAM_B2 * v + (1.0 - ADAM_B2) * (g * g)
    m_hat = m / (1.0 - ADAM_B1 ** ADAM_STEP)
    v_hat = v / (1.0 - ADAM_B2 ** ADAM_STEP)
    delta = -ADAM_LR * (m_hat / (jnp.sqrt(v_hat) + ADAM_EPS) + ADAM_WD * w)
    return delta, m, v


def _sum_adamw(parts, w, m, v, name, tr):
    R, C = w.shape
    tr = min(tr, R)
    assert R % tr == 0

    def body(p_ref, w_ref, m_ref, v_ref, g_ref, d_ref, nm_ref, nv_ref):
        g = p_ref[0].astype(F32)
        for d in range(1, 8):
            g = g + p_ref[d].astype(F32)
        g_ref[...] = g
        d_ref[...], nm_ref[...], nv_ref[...] = _adamw(w_ref[...], g, m_ref[...], v_ref[...])

    tile = pl.BlockSpec((tr, C), lambda i: (i, 0))
    return pl.pallas_call(
        body, name=name, grid=(R // tr,),
        in_specs=[pl.BlockSpec((8, tr, C), lambda i: (0, i, 0)), tile, tile, tile],
        out_specs=[tile] * 4, out_shape=[jax.ShapeDtypeStruct((R, C), F32)] * 4,
    )(parts, w, m, v)


def _sum_adamw_small(parts, w, m, v):
    def body(p_ref, w_ref, m_ref, v_ref, g_ref, d_ref, nm_ref, nv_ref, loss_ref):
        g = p_ref[0]
        for d in range(1, 8):
            g = g + p_ref[d]
        g_ref[...] = g
        d_ref[...], nm_ref[...], nv_ref[...] = _adamw(w_ref[...], g, m_ref[...], v_ref[...])
        row = lax.broadcasted_iota(jnp.int32, g.shape, 0)
        per_row = jnp.sum(jnp.where(row == 6, g, 0.0), axis=1, keepdims=True)
        loss_ref[...] = jnp.zeros((8, LANES), F32) + jnp.sum(per_row, axis=0, keepdims=True)

    return pl.pallas_call(
        body, name="sum_adamw_small",
        out_shape=[jax.ShapeDtypeStruct((8, D_MODEL), F32)] * 4 + [jax.ShapeDtypeStruct((8, LANES), F32)],
    )(parts, w, m, v)


def _pack_small(ln1_g, ln1_b, ln2_g, ln2_b, g_sb, g_fox, b_f):
    row5 = jnp.pad(b_f.reshape(1, N_FOX), ((0, 0), (0, D_MODEL - N_FOX)))
    rows = [ln1_g.reshape(1, -1), ln1_b.reshape(1, -1), ln2_g.reshape(1, -1), ln2_b.reshape(1, -1),
            jnp.concatenate([g_sb.reshape(1, -1), g_fox.reshape(1, -1)], axis=1), row5,
            jnp.zeros((2, D_MODEL), F32)]
    return jnp.concatenate(rows, axis=0)


def _unpack_small(p):
    return {"ln1_g": p[0:1], "ln1_b": p[1:2], "ln2_g": p[2:3], "ln2_b": p[3:4], "g_sb": p[4:5, :GROUP_W],
            "g_fox": p[4:5, GROUP_W:], "b_f": p[5:6, :N_FOX]}


def kernel(x, w_in, b_f, g_sb, g_fox, w_out, ln1_g, ln1_b, ln2_g, ln2_b, w_gate_up, w_down, loss_target, m_w_in, m_b_f, m_g_sb, m_g_fox, m_w_out, m_ln1_g, m_ln1_b, m_ln2_g, m_ln2_b, m_w_gate_up, m_w_down, v_w_in, v_b_f, v_g_sb, v_g_fox, v_w_out, v_ln1_g, v_ln1_b, v_ln2_g, v_ln2_b, v_w_gate_up, v_w_down):
    S = x.shape[1]
    x2 = x.reshape(S, D_MODEL)
    tgt = loss_target.reshape(S, D_MODEL)
    TM = 1024
    TR = 512
    BQ = ATTN_BLOCK
    in_w = w_in.shape[2]
    gu_w = w_gate_up.shape[2]

    shards = [w_in[0].astype(BF16), w_out[0].astype(BF16), w_gate_up[0].astype(BF16), w_down[0].astype(BF16)]
    wi_s, wo_s, wgu_s, wd_s = _allgather_chips(shards)
    wi = wi_s.transpose(1, 0, 2).reshape(D_MODEL, 4 * in_w)
    w_sb, w_fx = wi[:, :QKV_W // 2], wi[:, QKV_W // 2:QKV_W]
    wqkv = wi[:, :QKV_W]
    wft = wi[:, QKV_W:].T
    wo = wo_s.reshape(D_MODEL, D_MODEL)
    wgu = wgu_s.transpose(1, 0, 2).reshape(D_MODEL, 2 * D_FF)
    wg, wu = wgu[:, :D_FF], wgu[:, D_FF:]
    wd = wd_s.reshape(D_FF, D_MODEL)
    g_row = jnp.concatenate([g_sb, g_fox], axis=1)
    hid = np.arange(D_MODEL) // HEAD_DIM
    bd = jnp.asarray((hid[:, None] == hid[None, :]).astype(np.float32), BF16)

    proj = _matmul(x2, wqkv, mode="nn", name="proj", tm=TM, tn=512, tk=D_MODEL, outs=[BF16])
    lf = _fgate_fwd(x2, wft, b_f.reshape(N_FOX, 1), TM)
    c = _cumsum_fwd(lf)
    c_pair = c.reshape(N_PAIRS, 2, S)
    c_row = jnp.pad(c_pair, ((0, 0), (0, 6), (0, 0)))
    c_col = jnp.pad(c_pair.transpose(0, 2, 1), ((0, 0), (0, 0), (0, 6)))

    o_sb, st_sb, jmin_sb = _sb_fwd(proj, 0, BQ)
    jstart_fx = _fox_start_blocks(proj, 12, c, min(BQ, S))
    o_fx, st_fx = _fox_fwd(proj, 12, c_col, c_row, jstart_fx, BQ)

    def attn_post(i, osb_ref, ofx_ref, g_ref, bd_ref, on_ref):
        o = jnp.concatenate([osb_ref[...], ofx_ref[...]], axis=1)
        ms = _head_sums(o * o, bd_ref[...]) * (1.0 / HEAD_DIM)
        on_ref[...] = (o * lax.rsqrt(ms + RMS_EPS) * g_ref[...]).astype(BF16)

    (on,) = _rowwise(attn_post, "attn_post", S, TR, [(o_sb, "t"), (o_fx, "t"), (g_row, "f"), (bd, "f")],
                     [((S, D_MODEL), BF16, "t")])

    u1 = _matmul(on, wo, mode="nn", name="mix", tm=TM, tn=D_MODEL, tk=D_MODEL, outs=[F32],
                 extras=[(x2, (TM if S >= TM else S, D_MODEL), _tile_ij)],
                 epilogue=lambda acc, xv: (ALPHA * xv + acc,))

    def ln1_fwd(i, u_ref, g_ref, b_ref, h_ref):
        xh, _ = _ln_stats(u_ref[...])
        h_ref[...] = xh * g_ref[...] + b_ref[...]

    (h1,) = _rowwise(ln1_fwd, "ln1_fwd", S, TR, [(u1, "t"), (ln1_g, "f"), (ln1_b, "f")], [((S, D_MODEL), F32, "t")])

    gu = _matmul(h1, wgu, mode="nn", name="gate_up", tm=TM, tn=512, tk=D_MODEL, outs=[F32])

    tmr = min(TR, S)
    n_ff = D_FF // 256

    def swiglu_body(g_ref, u_ref, a_ref):
        g = g_ref[...]
        a_ref[...] = (g / (1.0 + jnp.exp(-g)) * u_ref[...]).astype(BF16)

    act = pl.pallas_call(
        swiglu_body, name="swiglu", grid=(S // tmr, n_ff),
        in_specs=[pl.BlockSpec((tmr, 256), lambda i, j: (i, j)), pl.BlockSpec((tmr, 256), lambda i, j: (i, j + n_ff))],
        out_specs=pl.BlockSpec((tmr, 256), lambda i, j: (i, j)),
        out_shape=jax.ShapeDtypeStruct((S, D_FF), BF16))(gu, gu)

    u2 = _matmul(act, wd, mode="nn", name="ffn_down", tm=TM, tn=D_MODEL, tk=D_FF, outs=[F32],
                 extras=[(h1, (TM if S >= TM else S, D_MODEL), _tile_ij)],
                 epilogue=lambda acc, hv: (ALPHA * hv + acc,))

    def ln2_loss(i, u_ref, t_ref, g_ref, b_ref, du_ref, acc_ref):
        xh, r = _ln_stats(u_ref[...])
        g = g_ref[...]
        err = xh * g + b_ref[...] - t_ref[...]
        dy = err * (1.0 / D_MODEL)
        du_ref[...] = _ln_bwd(dy, xh, r, g)
        _acc_rows(i, acc_ref, {2: jnp.sum(dy * xh, axis=0, keepdims=True), 3: jnp.sum(dy, axis=0, keepdims=True),
                               6: jnp.sum(err * err, axis=0, keepdims=True) * (0.5 / D_MODEL)})

    du2, acc_ln2 = _rowwise(ln2_loss, "ln2_loss", S, TR, [(u2, "t"), (tgt, "t"), (ln2_g, "f"), (ln2_b, "f")],
                            [((S, D_MODEL), F32, "t"), ((8, D_MODEL), F32, "f")])

    d_wd = _matmul(act, du2, mode="tn", name="dw_down", tm=1408, tn=D_MODEL, tk=TM, outs=[F32])

    half = D_FF // 1408

    def dgu_epilogue(da, g, u):
        s = 1.0 / (1.0 + jnp.exp(-g))
        return da * u * (s * (1.0 + g * (1.0 - s))), da * (g * s)

    tm_e = TM if S >= TM else S
    dgate, dup = _matmul(du2, wd, mode="nt", name="d_act", tm=TM, tn=1408, tk=D_MODEL, outs=[BF16, BF16],
                         extras=[(gu, (tm_e, 1408), _tile_ij), (gu, (tm_e, 1408), lambda i, j: (i, j + half))],
                         epilogue=dgu_epilogue)
    d_wg = _matmul(h1, dgate, mode="tn", name="dw_gate", tm=D_MODEL, tn=1408, tk=TM, outs=[F32])
    d_wu = _matmul(h1, dup, mode="tn", name="dw_up", tm=D_MODEL, tn=1408, tk=TM, outs=[F32])
    dh1 = _matmul(dgate, wg, mode="nt", name="dh1_gate", tm=TM, tn=D_MODEL, tk=D_FF, outs=[F32],
                  extras=[(du2, (tm_e, D_MODEL), _tile_ij)], epilogue=lambda acc, e: (ALPHA * e + acc,))
    dh1 = _matmul(dup, wu, mode="nt", name="dh1_up", tm=TM, tn=D_MODEL, tk=D_FF, outs=[F32],
                  extras=[(dh1, (tm_e, D_MODEL), _tile_ij)], epilogue=lambda acc, e: (e + acc,))

    def ln1_bwd(i, dh_ref, u_ref, g_ref, du_ref, acc_ref):
        xh, r = _ln_stats(u_ref[...])
        dh = dh_ref[...]
        du_ref[...] = _ln_bwd(dh, xh, r, g_ref[...])
        _acc_rows(i, acc_ref, {0: jnp.sum(dh * xh, axis=0, keepdims=True), 1: jnp.sum(dh, axis=0, keepdims=True)})

    du1, acc_ln1 = _rowwise(ln1_bwd, "ln1_bwd", S, TR, [(dh1, "t"), (u1, "t"), (ln1_g, "f")],
                            [((S, D_MODEL), F32, "t"), ((8, D_MODEL), F32, "f")])
    d_wo = _matmul(on, du1, mode="tn", name="dw_out", tm=D_MODEL, tn=D_MODEL, tk=TM, outs=[F32])
    don = _matmul(du1, wo, mode="nt", name="d_on", tm=TM, tn=D_MODEL, tk=D_MODEL, outs=[F32])

    def rms_bwd(i, don_ref, osb_ref, ofx_ref, g_ref, bd_ref, dosb_ref, dofx_ref, acc_ref):
        o = jnp.concatenate([osb_ref[...], ofx_ref[...]], axis=1)
        bdv = bd_ref[...]
        r = lax.rsqrt(_head_sums(o * o, bdv) * (1.0 / HEAD_DIM) + RMS_EPS)
        dn = don_ref[...]
        dg = dn * g_ref[...]
        do = r * dg - o * (r * r * r) * (_head_sums(dg * o, bdv) * (1.0 / HEAD_DIM))
        dosb_ref[...] = do[:, :GROUP_W]
        dofx_ref[...] = do[:, GROUP_W:]
        _acc_rows(i, acc_ref, {4: jnp.sum(dn * o * r, axis=0, keepdims=True)})

    do_sb, do_fx, acc_rms = _rowwise(
        rms_bwd, "rms_bwd", S, TR, [(don, "t"), (o_sb, "t"), (o_fx, "t"), (g_row, "f"), (bd, "f")],
        [((S, GROUP_W), F32, "t"), ((S, GROUP_W), F32, "t"), ((8, D_MODEL), F32, "f")])

    dq_sb, dk_sb, dv_sb = _sb_bwd(proj, 0, do_sb, st_sb, jmin_sb, BQ)
    dq_fx, dk_fx, dv_fx, dc, dcq = _fox_bwd(proj, 12, do_fx, o_fx, st_fx, c_col, c_row, jstart_fx, BQ)
    dc = dc[:, :2, :] + dcq[:, :, :2].transpose(0, 2, 1)
    dfl, dbf = _fgate_bwd(dc.reshape(N_FOX, S), lf)
    dp_sb = jnp.concatenate([dq_sb, dk_sb, dv_sb], axis=1).astype(BF16)
    dp_fx = jnp.concatenate([dq_fx, dk_fx, dv_fx], axis=1).astype(BF16)

    d_wsb = _matmul(x2, dp_sb, mode="tn", name="dw_in_sb", tm=D_MODEL, tn=QKV_W // 2, tk=TM, outs=[F32])
    d_wfx = _matmul(x2, dp_fx, mode="tn", name="dw_in_fx", tm=D_MODEL, tn=QKV_W // 2, tk=TM, outs=[F32])
    d_wft = _matmul(dfl, x2, mode="nn", name="dw_in_f", tm=N_FOX, tn=D_MODEL, tk=TM, outs=[F32])
    dx = _matmul(dp_sb, w_sb, mode="nt", name="dx_sb", tm=TM, tn=D_MODEL, tk=QKV_W // 2, outs=[F32],
                 extras=[(du1, (tm_e, D_MODEL), _tile_ij)], epilogue=lambda acc, e: (ALPHA * e + acc,))
    dx = _matmul(dp_fx, w_fx, mode="nt", name="dx_fx", tm=TM, tn=D_MODEL, tk=QKV_W // 2, outs=[F32],
                 extras=[(dx, (tm_e, D_MODEL), _tile_ij)], epilogue=lambda acc, e: (e + acc,))
    dx = _matmul(dfl, wft, mode="tn", name="dx_f", tm=TM, tn=D_MODEL, tk=N_FOX, outs=[F32],
                 extras=[(dx, (tm_e, D_MODEL), _tile_ij)], epilogue=lambda acc, e: (e + acc,))

    d_wi = jnp.concatenate([d_wsb, d_wfx, d_wft.T], axis=1)
    d_wgu = jnp.concatenate([d_wg, d_wu], axis=1)
    parts = [d_wi.reshape(D_MODEL, 4, in_w).transpose(1, 0, 2).astype(BF16),
             d_wo.reshape(4, D_MODEL // 4, D_MODEL).astype(BF16),
             d_wgu.reshape(D_MODEL, 4, gu_w).transpose(1, 0, 2).astype(BF16),
             d_wd.reshape(4, D_FF // 4, D_MODEL).astype(BF16)]
    got = _exchange(parts, True)
    big = {}
    for nm, p, w, m, v, tr in (("w_in", got[0], w_in, m_w_in, v_w_in, 256), ("w_out", got[1], w_out, m_w_out, v_w_out, 256),
                               ("w_gate_up", got[2], w_gate_up, m_w_gate_up, v_w_gate_up, 128),
                               ("w_down", got[3], w_down, m_w_down, v_w_down, 176)):
        big[nm] = [r[None] for r in _sum_adamw(p, w[0], m[0], v[0], "sum_adamw_" + nm, tr)]

    small = acc_ln2 + acc_ln1 + acc_rms
    small = small + jnp.pad(dbf.reshape(1, N_FOX), ((5, 2), (0, D_MODEL - N_FOX)))
    (small_all,) = _exchange([small], False)
    sw = _pack_small(ln1_g, ln1_b, ln2_g, ln2_b, g_sb, g_fox, b_f)
    sm = _pack_small(m_ln1_g, m_ln1_b, m_ln2_g, m_ln2_b, m_g_sb, m_g_fox, m_b_f)
    sv = _pack_small(v_ln1_g, v_ln1_b, v_ln2_g, v_ln2_b, v_g_sb, v_g_fox, v_b_f)
    sg, sd, snm, snv, loss_blk = _sum_adamw_small(small_all, sw, sm, sv)
    sg, sd, snm, snv = _unpack_small(sg), _unpack_small(sd), _unpack_small(snm), _unpack_small(snv)

    names = ["w_in", "b_f", "g_sb", "g_fox", "w_out", "ln1_g", "ln1_b", "ln2_g", "ln2_b", "w_gate_up", "w_down"]
    outs = [loss_blk[0, 0], dx.reshape(1, S, D_MODEL)]
    for k, table in enumerate((sg, sd, snm, snv)):
        outs += [big[n][k] if n in big else table[n] for n in names]
    return tuple(outs)
```

```python
import functools

import numpy as np
import jax
import jax.numpy as jnp
from jax import lax
from jax.experimental import pallas as pl
from jax.experimental.pallas import tpu as pltpu

F32 = jnp.float32
BF16 = jnp.bfloat16

D_MODEL = 1024
HEAD_DIM = 64
LANES = 128
N_PAIRS = 4
GROUP_W = 512
QKV_W = 3072
D_FF = 2816
N_FOX = 8
ALPHA = 2.0 ** 0.25
LN_EPS = 1e-5
RMS_EPS = 1e-6
SCALE = HEAD_DIM ** -0.5
NEG_BIG = -1e30
FOX_SKIP = 30.0
SB_STOP = -105.0
ADAM_LR, ADAM_B1, ADAM_B2, ADAM_EPS, ADAM_WD, ADAM_STEP = 0.001, 0.9, 0.999, 1e-08, 0.01, 10
ATTN_BLOCK = 256
VMEM_BIG = 56 * 1024 * 1024
MESH = pl.DeviceIdType.MESH

_NN = (((1,), (0,)), ((), ()))
_NT = (((1,), (1,)), ((), ()))
_TN = (((0,), (0,)), ((), ()))


def _dot(a, b, dims=_NN):
    return lax.dot_general(a, b, dims, preferred_element_type=F32)


def _split_dot(x, t):
    hi = x.astype(BF16)
    lo = (x - hi.astype(F32)).astype(BF16)
    return _dot(hi, t) + _dot(lo, t)


def _softplus(z):
    return jnp.maximum(z, 0.0) + jnp.log1p(jnp.exp(-jnp.abs(z)))


def _col(v, h):
    lane = lax.broadcasted_iota(jnp.int32, v.shape, 1)
    return jnp.sum(jnp.where(lane == h, v, 0.0), axis=1, keepdims=True)


def _two_sum(hi, lo, b):
    s = hi + b
    bb = s - hi
    err = (hi - (s - bb)) + (b - bb)
    return s, lo + err


def _params(vmem=None):
    return pltpu.CompilerParams(vmem_limit_bytes=vmem) if vmem else None


def _matmul(a, b, *, mode, name, tm, tn, tk, outs, extras=(), epilogue=None, vmem=None):
    if mode == "nn":
        (M, K), (_, N) = a.shape, b.shape
    elif mode == "nt":
        (M, K), (N, _) = a.shape, b.shape
    else:
        (K, M), (_, N) = a.shape, b.shape
    tm, tn, tk = min(tm, M), min(tn, N), min(tk, K)
    assert M % tm == 0 and N % tn == 0 and K % tk == 0, (name, M, N, K, tm, tn, tk)
    nk = K // tk
    dims = {"nn": _NN, "nt": _NT, "tn": _TN}[mode]
    if mode == "tn":
        a_spec = pl.BlockSpec((tk, tm), lambda i, j, k: (k, i))
    else:
        a_spec = pl.BlockSpec((tm, tk), lambda i, j, k: (i, k))
    if mode == "nt":
        b_spec = pl.BlockSpec((tn, tk), lambda i, j, k: (j, k))
    else:
        b_spec = pl.BlockSpec((tk, tn), lambda i, j, k: (k, j))
    ex_specs = [pl.BlockSpec(bs, (lambda i, j, k, f=f: f(i, j))) for (_, bs, f) in extras]
    ne, no = len(extras), len(outs)
    if epilogue is None:
        epilogue = lambda acc: (acc,)

    def body(a_ref, b_ref, *rest):
        ex_refs, out_refs, acc = rest[:ne], rest[ne:ne + no], rest[-1]
        k = pl.program_id(2)

        @pl.when(k == 0)
        def _():
            acc[...] = jnp.zeros_like(acc)

        acc[...] += _dot(a_ref[...].astype(BF16), b_ref[...].astype(BF16), dims)

        @pl.when(k == nk - 1)
        def _():
            res = epilogue(acc[...], *[e[...] for e in ex_refs])
            for r, o in zip(res, out_refs):
                o[...] = r.astype(o.dtype)

    res = pl.pallas_call(
        body, name=name, grid=(M // tm, N // tn, nk),
        in_specs=[a_spec, b_spec] + ex_specs,
        out_specs=[pl.BlockSpec((tm, tn), lambda i, j, k: (i, j)) for _ in outs],
        out_shape=[jax.ShapeDtypeStruct((M, N), d) for d in outs],
        scratch_shapes=[pltpu.VMEM((tm, tn), F32)],
        compiler_params=_params(vmem),
    )(a, b, *[e[0] for e in extras])
    return res[0] if no == 1 else res


def _tile_ij(i, j):
    return (i, j)


def _rowwise(fn, name, rows, tm, ins, outs, vmem=None):
    tm = min(tm, rows)
    assert rows % tm == 0

    def spec(shape, kind):
        if kind == "t":
            return pl.BlockSpec((tm,) + tuple(shape[1:]), lambda i: (i,) + (0,) * (len(shape) - 1))
        return pl.BlockSpec(tuple(shape), lambda i: (0,) * len(shape))

    def body(*refs):
        fn(pl.program_id(0), *refs)

    return pl.pallas_call(
        body, name=name, grid=(rows // tm,),
        in_specs=[spec(a.shape, k) for a, k in ins],
        out_specs=[spec(s, k) for s, _, k in outs],
        out_shape=[jax.ShapeDtypeStruct(s, d) for s, d, _ in outs],
        compiler_params=_params(vmem),
    )(*[a for a, _ in ins])


def _ln_stats(u):
    mu = jnp.mean(u, axis=-1, keepdims=True)
    d = u - mu
    var = jnp.mean(d * d, axis=-1, keepdims=True)
    r = lax.rsqrt(var + LN_EPS)
    return d * r, r


def _ln_bwd(dh, xh, r, g):
    dxh = dh * g
    m1 = jnp.mean(dxh, axis=-1, keepdims=True)
    m2 = jnp.mean(dxh * xh, axis=-1, keepdims=True)
    return r * (dxh - m1 - xh * m2)


def _acc_rows(i, ref, rows):
    @pl.when(i == 0)
    def _():
        ref[...] = jnp.zeros_like(ref)
    for r, v in rows.items():
        ref[pl.ds(r, 1), :] += v


def _head_sums(v, he, het):
    return _split_dot(_split_dot(v, he), het)


def _fgate_fwd(x, wft, bf_col, tm):
    S = x.shape[0]
    tm = min(tm, S)

    def body(wft_ref, bf_ref, x_ref, lf_ref):
        f = _dot(wft_ref[...], x_ref[...].astype(BF16), _NT) + bf_ref[...]
        lf_ref[...] = -_softplus(-f)

    return pl.pallas_call(
        body, name="fgate_fwd", grid=(S // tm,),
        in_specs=[pl.BlockSpec((N_FOX, D_MODEL), lambda i: (0, 0)), pl.BlockSpec((N_FOX, 1), lambda i: (0, 0)),
                  pl.BlockSpec((tm, D_MODEL), lambda i: (i, 0))],
        out_specs=pl.BlockSpec((N_FOX, tm), lambda i: (0, i)),
        out_shape=jax.ShapeDtypeStruct((N_FOX, S), F32),
    )(wft, bf_col, x)


def _chunk_scan(v, reverse):
    lane = lax.broadcasted_iota(jnp.int32, v.shape, 1)
    sh = 1
    while sh < LANES:
        if reverse:
            v = v + jnp.where(lane < LANES - sh, pltpu.roll(v, LANES - sh, 1), 0.0)
        else:
            v = v + jnp.where(lane >= sh, pltpu.roll(v, sh, 1), 0.0)
        sh *= 2
    return v


def _cumsum_fwd(lf):
    n, S = lf.shape
    nc = S // LANES

    def body(lf_ref, c_ref):
        def step(ci, carry):
            sl = pl.ds(pl.multiple_of(ci * LANES, LANES), LANES)
            v = _chunk_scan(lf_ref[:, sl], False) + carry
            c_ref[:, sl] = v
            return _col(v, LANES - 1)
        lax.fori_loop(0, nc, step, jnp.zeros((n, 1), F32))

    return pl.pallas_call(body, name="cumsum_fwd", out_shape=jax.ShapeDtypeStruct((n, S), F32))(lf)


def _fgate_bwd(dc, lf):
    n, S = dc.shape
    nc = S // LANES

    def body(dc_ref, lf_ref, dfl_ref, dbf_ref):
        def step(t, carry):
            car, tot = carry
            ci = nc - 1 - t
            sl = pl.ds(pl.multiple_of(ci * LANES, LANES), LANES)
            dlf = _chunk_scan(dc_ref[:, sl], True) + car
            dfl = dlf * (1.0 - jnp.exp(lf_ref[:, sl]))
            dfl_ref[:, sl] = dfl
            return _col(dlf, 0), tot + jnp.sum(dfl, axis=1, keepdims=True)
        _, tot = lax.fori_loop(0, nc, step, (jnp.zeros((n, 1), F32), jnp.zeros((n, 1), F32)))
        dbf_ref[...] = tot

    return pl.pallas_call(body, name="fgate_bwd",
                          out_shape=[jax.ShapeDtypeStruct((n, S), F32), jax.ShapeDtypeStruct((n, 1), F32)])(dc, lf)


def _tri_matrices(b):
    r = np.arange(b)
    tfwd = (r[:, None] <= r[None, :]).astype(np.float32)
    return jnp.asarray(tfwd, BF16), jnp.asarray(tfwd.T, BF16)


def _kv_copies(kv_hbm, kbuf, vbuf, sems, pair_col, bq, j, slot):
    rows = pl.ds(pl.multiple_of(j * bq, bq), bq)

    def cols(c):
        return pl.ds(pl.multiple_of((pair_col + c) * LANES, LANES), LANES)

    return (pltpu.make_async_copy(kv_hbm.at[rows, cols(4)], kbuf.at[slot], sems.at[0, slot]),
            pltpu.make_async_copy(kv_hbm.at[rows, cols(8)], vbuf.at[slot], sems.at[1, slot]))


def _masked_pair(v, lane_is_a, scale=1.0):
    v = v.astype(F32) * scale
    return jnp.where(lane_is_a, v, 0.0).astype(BF16), jnp.where(lane_is_a, 0.0, v).astype(BF16)


def _sb_fwd(proj, col0, bq):
    S = proj.shape[0]
    bq = min(bq, S)
    nq = S // bq
    _, trev = _tri_matrices(bq)

    def body(q_ref, kv_hbm, trev_ref, o_ref, st_ref, jmin_ref, acc_a, acc_b, qa, qb, rs, kbuf, vbuf, sems):
        p, i = pl.program_id(0), pl.program_id(1)
        fetch = functools.partial(_kv_copies, kv_hbm, kbuf, vbuf, sems, col0 + p, bq)
        for cp in fetch(i, 0):
            cp.start()
        is_a = lax.broadcasted_iota(jnp.int32, (bq, LANES), 1) < HEAD_DIM
        acc_a[...] = jnp.zeros_like(acc_a)
        acc_b[...] = jnp.zeros_like(acc_b)
        rs[...] = jnp.zeros_like(rs)
        qa[...], qb[...] = _masked_pair(q_ref[...], is_a, SCALE)

        def tile(slot, masked):
            k, v, trev_m = kbuf[slot], vbuf[slot], trev_ref[...]
            if masked:
                tri = lax.broadcasted_iota(jnp.int32, (bq, bq), 0) > lax.broadcasted_iota(jnp.int32, (bq, bq), 1)
            for h, (qh, acc) in enumerate(((qa, acc_a), (qb, acc_b))):
                z = _dot(qh[...], k, _NT)
                lk = -_softplus(z)
                if masked:
                    lk = jnp.where(tri, lk, 0.0)
                r_hi, r_lo = rs[2 * h], rs[2 * h + 1]
                w = jnp.exp(z + _split_dot(lk, trev_m) + (r_hi + r_lo))
                if masked:
                    w = jnp.where(tri, w, 0.0)
                acc[...] += _dot(w.astype(BF16), v)
                rs[2 * h], rs[2 * h + 1] = _two_sum(r_hi, r_lo, jnp.sum(lk, axis=1, keepdims=True))

        def step(carry):
            j, _ = carry
            slot = lax.rem(i - j, 2)
            for cp in fetch(j, slot):
                cp.wait()

            @pl.when(j > 0)
            def _():
                for cp in fetch(j - 1, 1 - slot):
                    cp.start()

            pl.when(j == i)(functools.partial(tile, slot, True))
            pl.when(j < i)(functools.partial(tile, slot, False))
            live = jnp.max(jnp.maximum(rs[0], rs[2])) > SB_STOP
            return j - 1, live.astype(jnp.int32)

        j_end, _ = lax.while_loop(lambda c: jnp.logical_and(c[0] >= 0, c[1] > 0), step, (i, jnp.int32(1)))

        @pl.when(j_end >= 0)
        def _():
            for cp in fetch(j_end, lax.rem(i - j_end, 2)):
                cp.wait()

        jmin_ref[p, i] = j_end + 1
        o_ref[...] = jnp.where(is_a, acc_a[...], acc_b[...])
        lane8 = lax.broadcasted_iota(jnp.int32, (bq, 8), 1)
        st = jnp.zeros((bq, 8), F32)
        for c, src in enumerate((0, 2, 1, 3)):
            st = jnp.where(lane8 == c, rs[src], st)
        st_ref[0] = st

    return pl.pallas_call(
        body, name="sb_fwd", grid=(N_PAIRS, nq),
        in_specs=[pl.BlockSpec((bq, LANES), lambda p, i: (i, col0 + p)),
                  pl.BlockSpec(memory_space=pl.ANY),
                  pl.BlockSpec((bq, bq), lambda p, i: (0, 0))],
        out_specs=[pl.BlockSpec((bq, LANES), lambda p, i: (i, p)),
                   pl.BlockSpec((1, bq, 8), lambda p, i: (p, i, 0)),
                   pl.BlockSpec(memory_space=pltpu.SMEM)],
        out_shape=[jax.ShapeDtypeStruct((S, GROUP_W), F32), jax.ShapeDtypeStruct((N_PAIRS, S, 8), F32),
                   jax.ShapeDtypeStruct((N_PAIRS, nq), jnp.int32)],
        scratch_shapes=[pltpu.VMEM((bq, LANES), F32), pltpu.VMEM((bq, LANES), F32),
                        pltpu.VMEM((bq, LANES), BF16), pltpu.VMEM((bq, LANES), BF16),
                        pltpu.VMEM((4, bq, 1), F32),
                        pltpu.VMEM((2, bq, LANES), BF16), pltpu.VMEM((2, bq, LANES), BF16),
                        pltpu.SemaphoreType.DMA((2, 2))],
    )(proj, proj, trev)


def _sb_bwd(proj, col0, do, st, jmin, bq):
    S = proj.shape[0]
    bq = min(bq, S)
    nq = S // bq
    tfwd, trev = _tri_matrices(bq)

    def body(jmin_ref, q_ref, kv_hbm, do_ref, st_ref, tfwd_ref, trev_ref,
             dq_ref, dk_ref, dv_ref, dq_a, dq_b, qa, qb, doa, dob, rs, kbuf, vbuf, sems):
        p, i = pl.program_id(0), pl.program_id(1)
        fetch = functools.partial(_kv_copies, kv_hbm, kbuf, vbuf, sems, col0 + p, bq)
        j0 = jmin_ref[p, i]
        for cp in fetch(j0, 0):
            cp.start()
        is_a = lax.broadcasted_iota(jnp.int32, (bq, LANES), 1) < HEAD_DIM

        @pl.when(i == 0)
        def _():
            dk_ref[...] = jnp.zeros_like(dk_ref)
            dv_ref[...] = jnp.zeros_like(dv_ref)

        dq_a[...] = jnp.zeros_like(dq_a)
        dq_b[...] = jnp.zeros_like(dq_b)
        rs[...] = jnp.zeros_like(rs)
        st_v = st_ref[0]
        for h in range(2):
            rs[6 + 2 * h], rs[7 + 2 * h] = _col(st_v, h), _col(st_v, 2 + h)
        qa[...], qb[...] = _masked_pair(q_ref[...], is_a, SCALE)
        doa[...], dob[...] = _masked_pair(do_ref[...], is_a)

        def tile(j, slot, masked):
            k, v = kbuf[slot], vbuf[slot]
            tfwd_m, trev_m = tfwd_ref[...], trev_ref[...]
            if masked:
                tri = lax.broadcasted_iota(jnp.int32, (bq, bq), 0) > lax.broadcasted_iota(jnp.int32, (bq, bq), 1)
            dzs, ws = [], []
            for h, (qh, doh, dq) in enumerate(((qa, doa, dq_a), (qb, dob, dq_b))):
                z = _dot(qh[...], k, _NT)
                lk = -_softplus(z)
                if masked:
                    lk = jnp.where(tri, lk, 0.0)
                p_hi, p_lo = _two_sum(rs[3 * h], rs[3 * h + 1], jnp.sum(lk, axis=1, keepdims=True))
                rs[3 * h], rs[3 * h + 1] = p_hi, p_lo
                right = (rs[6 + 2 * h] - p_hi) + (rs[7 + 2 * h] - p_lo)
                w = jnp.exp(z + _split_dot(lk, trev_m) + right)
                if masked:
                    w = jnp.where(tri, w, 0.0)
                g = _dot(doh[...], v, _NT) * w
                g_left = rs[3 * h + 2]
                dz = g - jnp.exp(z + lk) * (_split_dot(g, tfwd_m) + g_left)
                if masked:
                    dz = jnp.where(tri, dz, 0.0)
                rs[3 * h + 2] = g_left + jnp.sum(g, axis=1, keepdims=True)
                dzb = dz.astype(BF16)
                dq[...] += _dot(dzb, k)
                dzs.append(dzb)
                ws.append(w.astype(BF16))
            rows = pl.ds(pl.multiple_of(j * bq, bq), bq)
            dk_ref[rows, :] += _dot(dzs[0], qa[...], _TN) + _dot(dzs[1], qb[...], _TN)
            dv_ref[rows, :] += _dot(ws[0], doa[...], _TN) + _dot(ws[1], dob[...], _TN)

        _walk_up(fetch, j0, i, tile)
        dq_ref[...] = jnp.where(is_a, dq_a[...], dq_b[...]) * SCALE

    grid_spec = pltpu.PrefetchScalarGridSpec(
        num_scalar_prefetch=1, grid=(N_PAIRS, nq),
        in_specs=[pl.BlockSpec((bq, LANES), lambda p, i, jm: (i, col0 + p)),
                  pl.BlockSpec(memory_space=pl.ANY),
                  pl.BlockSpec((bq, LANES), lambda p, i, jm: (i, p)),
                  pl.BlockSpec((1, bq, 8), lambda p, i, jm: (p, i, 0)),
                  pl.BlockSpec((bq, bq), lambda p, i, jm: (0, 0)),
                  pl.BlockSpec((bq, bq), lambda p, i, jm: (0, 0))],
        out_specs=[pl.BlockSpec((bq, LANES), lambda p, i, jm: (i, p)),
                   pl.BlockSpec((S, LANES), lambda p, i, jm: (0, p)),
                   pl.BlockSpec((S, LANES), lambda p, i, jm: (0, p))],
        scratch_shapes=[pltpu.VMEM((bq, LANES), F32), pltpu.VMEM((bq, LANES), F32)]
        + [pltpu.VMEM((bq, LANES), BF16)] * 4 + [pltpu.VMEM((10, bq, 1), F32)]
        + [pltpu.VMEM((2, bq, LANES), BF16), pltpu.VMEM((2, bq, LANES), BF16), pltpu.SemaphoreType.DMA((2, 2))])
    return pl.pallas_call(
        body, name="sb_bwd", grid_spec=grid_spec,
        out_shape=[jax.ShapeDtypeStruct((S, GROUP_W), F32)] * 3,
        compiler_params=_params(VMEM_BIG),
    )(jmin, proj, proj, do, st, tfwd, trev)


def _walk_up(fetch, j0, i, tile):
    def step(j, carry):
        slot = lax.rem(j - j0, 2)
        for cp in fetch(j, slot):
            cp.wait()

        @pl.when(j < i)
        def _():
            for cp in fetch(j + 1, 1 - slot):
                cp.start()

        pl.when(j == i)(functools.partial(tile, j, slot, True))
        pl.when(j < i)(functools.partial(tile, j, slot, False))
        return carry

    lax.fori_loop(j0, i + 1, step, 0)


def _fox_start_blocks(proj, col0, c, bq):
    S = proj.shape[0]
    nq = S // bq

    def norms(first):
        t = proj[:, first * LANES:(first + N_PAIRS) * LANES].astype(F32).reshape(S, 2 * N_PAIRS, HEAD_DIM)
        return jnp.sqrt(jnp.sum(t * t, axis=-1))

    qmax = norms(col0).reshape(nq, bq, 2 * N_PAIRS).max(axis=1)
    kmax = norms(col0 + 4).max(axis=0)
    zb = (2.0 * SCALE) * qmax * kmax[None, :]
    c_first = c[:, ::bq].T
    c_last = c[:, bq - 1::bq].T
    live = (zb + c_first)[:, None, :] - c_last[None, :, :] >= -FOX_SKIP
    live = live.reshape(nq, nq, N_PAIRS, 2).any(axis=-1)
    first = jnp.where(live.any(axis=1), jnp.argmax(live, axis=1), nq)
    return jnp.minimum(first, jnp.arange(nq)[:, None]).T.astype(jnp.int32)


def _fox_fwd(proj, col0, c_col, c_row, jstart, bq):
    S = proj.shape[0]
    bq = min(bq, S)
    nq = S // bq

    def body(js_ref, q_ref, kv_hbm, cc_ref, cr_ref, o_ref, st_ref, acc_a, acc_b, qa, qb, ml, kbuf, vbuf, sems):
        p, i = pl.program_id(0), pl.program_id(1)
        fetch = functools.partial(_kv_copies, kv_hbm, kbuf, vbuf, sems, col0 + p, bq)
        j0 = js_ref[p, i]
        for cp in fetch(j0, 0):
            cp.start()
        is_a = lax.broadcasted_iota(jnp.int32, (bq, LANES), 1) < HEAD_DIM
        acc_a[...] = jnp.zeros_like(acc_a)
        acc_b[...] = jnp.zeros_like(acc_b)
        ml[0] = jnp.full((bq, 1), NEG_BIG, F32)
        ml[2] = jnp.full((bq, 1), NEG_BIG, F32)
        ml[1] = jnp.zeros((bq, 1), F32)
        ml[3] = jnp.zeros((bq, 1), F32)
        cc = cc_ref[0]
        ml[4], ml[5] = _col(cc, 0), _col(cc, 1)
        qa[...], qb[...] = _masked_pair(q_ref[...], is_a, SCALE)

        def tile(j, slot, masked):
            k, v = kbuf[slot], vbuf[slot]
            cols = pl.ds(pl.multiple_of(j * bq, bq), bq)
            if masked:
                tri = lax.broadcasted_iota(jnp.int32, (bq, bq), 0) >= lax.broadcasted_iota(jnp.int32, (bq, bq), 1)
            for h, (qh, acc) in enumerate(((qa, acc_a), (qb, acc_b))):
                s = _dot(qh[...], k, _NT) - cr_ref[0, pl.ds(h, 1), cols]
                if masked:
                    s = jnp.where(tri, s, NEG_BIG)
                m_prev, l_prev, cq = ml[2 * h], ml[2 * h + 1], ml[4 + h]
                m_new = jnp.maximum(m_prev, jnp.max(s, axis=1, keepdims=True) + cq)
                a = jnp.exp(m_prev - m_new)
                p = jnp.exp(s - (m_new - cq))
                ml[2 * h] = m_new
                ml[2 * h + 1] = a * l_prev + jnp.sum(p, axis=1, keepdims=True)
                acc[...] = a * acc[...] + _dot(p.astype(BF16), v)

        _walk_up(fetch, j0, i, tile)
        o_ref[...] = jnp.where(is_a, acc_a[...] / ml[1], acc_b[...] / ml[3])
        lane8 = lax.broadcasted_iota(jnp.int32, (bq, 8), 1)
        st = jnp.where(lane8 == 0, ml[0] + jnp.log(ml[1]), 0.0)
        st_ref[0] = jnp.where(lane8 == 1, ml[2] + jnp.log(ml[3]), st)

    grid_spec = pltpu.PrefetchScalarGridSpec(
        num_scalar_prefetch=1, grid=(N_PAIRS, nq),
        in_specs=[pl.BlockSpec((bq, LANES), lambda p, i, js: (i, col0 + p)),
                  pl.BlockSpec(memory_space=pl.ANY),
                  pl.BlockSpec((1, bq, 8), lambda p, i, js: (p, i, 0)),
                  pl.BlockSpec((1, 8, S), lambda p, i, js: (p, 0, 0))],
        out_specs=[pl.BlockSpec((bq, LANES), lambda p, i, js: (i, p)),
                   pl.BlockSpec((1, bq, 8), lambda p, i, js: (p, i, 0))],
        scratch_shapes=[pltpu.VMEM((bq, LANES), F32), pltpu.VMEM((bq, LANES), F32),
                        pltpu.VMEM((bq, LANES), BF16), pltpu.VMEM((bq, LANES), BF16),
                        pltpu.VMEM((6, bq, 1), F32),
                        pltpu.VMEM((2, bq, LANES), BF16), pltpu.VMEM((2, bq, LANES), BF16),
                        pltpu.SemaphoreType.DMA((2, 2))])
    return pl.pallas_call(
        body, name="fox_fwd", grid_spec=grid_spec,
        out_shape=[jax.ShapeDtypeStruct((S, GROUP_W), F32), jax.ShapeDtypeStruct((N_PAIRS, S, 8), F32)],
    )(jstart, proj, proj, c_col, c_row)


def _fox_bwd(proj, col0, do, o, st, c_col, c_row, jstart, bq):
    S = proj.shape[0]
    bq = min(bq, S)
    nq = S // bq

    def body(js_ref, q_ref, kv_hbm, do_ref, o_ref, st_ref, cc_ref, cr_ref,
             dq_ref, dk_ref, dv_ref, dc_ref, dcq_ref, dq_a, dq_b, qa, qb, doa, dob, dd, kbuf, vbuf, sems):
        p, i = pl.program_id(0), pl.program_id(1)
        fetch = functools.partial(_kv_copies, kv_hbm, kbuf, vbuf, sems, col0 + p, bq)
        j0 = js_ref[p, i]
        for cp in fetch(j0, 0):
            cp.start()
        is_a = lax.broadcasted_iota(jnp.int32, (bq, LANES), 1) < HEAD_DIM

        @pl.when(i == 0)
        def _():
            dk_ref[...] = jnp.zeros_like(dk_ref)
            dv_ref[...] = jnp.zeros_like(dv_ref)
            dc_ref[...] = jnp.zeros_like(dc_ref)

        dq_a[...] = jnp.zeros_like(dq_a)
        dq_b[...] = jnp.zeros_like(dq_b)
        qa[...], qb[...] = _masked_pair(q_ref[...], is_a, SCALE)
        dov = do_ref[...]
        doa[...], dob[...] = _masked_pair(dov, is_a)
        prod = dov * o_ref[...]
        dd[0] = jnp.sum(jnp.where(is_a, prod, 0.0), axis=1, keepdims=True)
        dd[1] = jnp.sum(jnp.where(is_a, 0.0, prod), axis=1, keepdims=True)
        dd[2] = jnp.zeros((bq, 1), F32)
        dd[3] = jnp.zeros((bq, 1), F32)
        cc, st_v = cc_ref[0], st_ref[0]
        dd[4], dd[5] = _col(cc, 0) - _col(st_v, 0), _col(cc, 1) - _col(st_v, 1)

        def tile(j, slot, masked):
            k, v = kbuf[slot], vbuf[slot]
            if masked:
                tri = lax.broadcasted_iota(jnp.int32, (bq, bq), 0) >= lax.broadcasted_iota(jnp.int32, (bq, bq), 1)
            cols = pl.ds(pl.multiple_of(j * bq, bq), bq)
            dss, ps = [], []
            for h, (qh, doh, dq) in enumerate(((qa, doa, dq_a), (qb, dob, dq_b))):
                p = jnp.exp(_dot(qh[...], k, _NT) - cr_ref[0, pl.ds(h, 1), cols] + dd[4 + h])
                if masked:
                    p = jnp.where(tri, p, 0.0)
                ds = p * (_dot(doh[...], v, _NT) - dd[h])
                dc_ref[0, pl.ds(h, 1), cols] -= jnp.sum(ds, axis=0, keepdims=True)
                dd[2 + h] += jnp.sum(ds, axis=1, keepdims=True)
                dsb = ds.astype(BF16)
                dq[...] += _dot(dsb, k)
                dss.append(dsb)
                ps.append(p.astype(BF16))
            dk_ref[cols, :] += _dot(dss[0], qa[...], _TN) + _dot(dss[1], qb[...], _TN)
            dv_ref[cols, :] += _dot(ps[0], doa[...], _TN) + _dot(ps[1], dob[...], _TN)

        _walk_up(fetch, j0, i, tile)
        dq_ref[...] = jnp.where(is_a, dq_a[...], dq_b[...]) * SCALE
        lane8 = lax.broadcasted_iota(jnp.int32, (bq, 8), 1)
        dcq_ref[0] = jnp.where(lane8 == 0, dd[2], jnp.where(lane8 == 1, dd[3], 0.0))

    grid_spec = pltpu.PrefetchScalarGridSpec(
        num_scalar_prefetch=1, grid=(N_PAIRS, nq),
        in_specs=[pl.BlockSpec((bq, LANES), lambda p, i, js: (i, col0 + p)),
                  pl.BlockSpec(memory_space=pl.ANY),
                  pl.BlockSpec((bq, LANES), lambda p, i, js: (i, p)),
                  pl.BlockSpec((bq, LANES), lambda p, i, js: (i, p)),
                  pl.BlockSpec((1, bq, 8), lambda p, i, js: (p, i, 0)),
                  pl.BlockSpec((1, bq, 8), lambda p, i, js: (p, i, 0)),
                  pl.BlockSpec((1, 8, S), lambda p, i, js: (p, 0, 0))],
        out_specs=[pl.BlockSpec((bq, LANES), lambda p, i, js: (i, p)),
                   pl.BlockSpec((S, LANES), lambda p, i, js: (0, p)),
                   pl.BlockSpec((S, LANES), lambda p, i, js: (0, p)),
                   pl.BlockSpec((1, 8, S), lambda p, i, js: (p, 0, 0)),
                   pl.BlockSpec((1, bq, 8), lambda p, i, js: (p, i, 0))],
        scratch_shapes=[pltpu.VMEM((bq, LANES), F32), pltpu.VMEM((bq, LANES), F32)]
        + [pltpu.VMEM((bq, LANES), BF16)] * 4 + [pltpu.VMEM((6, bq, 1), F32)]
        + [pltpu.VMEM((2, bq, LANES), BF16), pltpu.VMEM((2, bq, LANES), BF16), pltpu.SemaphoreType.DMA((2, 2))])
    return pl.pallas_call(
        body, name="fox_bwd", grid_spec=grid_spec,
        out_shape=[jax.ShapeDtypeStruct((S, GROUP_W), F32)] * 3
        + [jax.ShapeDtypeStruct((N_PAIRS, 8, S), F32), jax.ShapeDtypeStruct((N_PAIRS, S, 8), F32)],
        compiler_params=_params(VMEM_BIG),
    )(jstart, proj, proj, do, o, st, c_col, c_row)


_HBM = pl.BlockSpec(memory_space=pltpu.HBM)


def _coords():
    return lax.axis_index("x"), lax.axis_index("y"), lax.axis_index("c")


def _allgather_chips(shards):
    n = len(shards)

    def body(*refs):
        ins, outs = refs[:n], refs[n:2 * n]
        send_sems, recv_sems, loc_sems = refs[2 * n:]
        x, y, c = _coords()
        mine = 2 * x + y
        chips = [(1 - x, y), (x, 1 - y), (1 - x, 1 - y)]
        local = [pltpu.make_async_copy(ins[w], outs[w].at[mine], loc_sems.at[w]) for w in range(n)]
        for cp in local:
            cp.start()

        def copy(w, r, slab, to):
            return pltpu.make_async_remote_copy(
                src_ref=ins[w], dst_ref=outs[w].at[slab], send_sem=send_sems.at[3 * w + r],
                recv_sem=recv_sems.at[3 * w + r], device_id=to, device_id_type=MESH)

        sends = [copy(w, r, mine, (cx, cy, c)) for w in range(n) for r, (cx, cy) in enumerate(chips)]
        for cp in sends:
            cp.start()
        for w in range(n):
            for r, (cx, cy) in enumerate(chips):
                copy(w, r, 2 * cx + cy, (cx, cy, c)).wait_recv()
        for cp in sends:
            cp.wait_send()
        for cp in local:
            cp.wait()

    return pl.pallas_call(
        body, name="allgather_weights",
        in_specs=[_HBM] * n, out_specs=[_HBM] * n,
        out_shape=[jax.ShapeDtypeStruct((4,) + s.shape, s.dtype) for s in shards],
        scratch_shapes=[pltpu.SemaphoreType.DMA((3 * n,)), pltpu.SemaphoreType.DMA((3 * n,)),
                        pltpu.SemaphoreType.DMA((n,))],
    )(*shards)


def _exchange(parts, per_chip):
    n = len(parts)
    half = [p.shape[1] // 2 for p in parts] if per_chip else None

    def body(*refs):
        ins, outs = refs[:n], refs[n:2 * n]
        send_sems, recv_sems, loc_sems = refs[2 * n:]
        x, y, c = _coords()
        me = 4 * x + 2 * y + c
        peers = [(x ^ fx, y ^ fy, c ^ fc) for fx in (0, 1) for fy in (0, 1) for fc in (0, 1)][1:]

        def src(w, dev):
            if not per_chip:
                return ins[w]
            return ins[w].at[2 * dev[0] + dev[1], pl.ds(pl.multiple_of(dev[2] * half[w], 16), half[w]), :]

        local = [pltpu.make_async_copy(src(w, (x, y, c)), outs[w].at[me], loc_sems.at[w]) for w in range(n)]
        for cp in local:
            cp.start()

        def copy(w, r, source, slab, to):
            return pltpu.make_async_remote_copy(
                src_ref=source, dst_ref=outs[w].at[slab], send_sem=send_sems.at[7 * w + r],
                recv_sem=recv_sems.at[7 * w + r], device_id=to, device_id_type=MESH)

        sends = [copy(w, r, src(w, dev), me, dev) for w in range(n) for r, dev in enumerate(peers)]
        for cp in sends:
            cp.start()
        for w in range(n):
            for r, dev in enumerate(peers):
                copy(w, r, src(w, dev), 4 * dev[0] + 2 * dev[1] + dev[2], dev).wait_recv()
        for cp in sends:
            cp.wait_send()
        for cp in local:
            cp.wait()

    return pl.pallas_call(
        body, name="exchange_per_chip" if per_chip else "exchange_all",
        in_specs=[_HBM] * n, out_specs=[_HBM] * n,
        out_shape=[jax.ShapeDtypeStruct((8, half[w], p.shape[2]) if per_chip else (8,) + p.shape, p.dtype)
                   for w, p in enumerate(parts)],
        scratch_shapes=[pltpu.SemaphoreType.DMA((7 * n,)), pltpu.SemaphoreType.DMA((7 * n,)),
                        pltpu.SemaphoreType.DMA((n,))],
    )(*parts)


def _sibling_swap(halves):
    n = len(halves)

    def body(*refs):
        ins, outs = refs[:n], refs[n:2 * n]
        send_sems, recv_sems, loc_sems = refs[2 * n:]
        x, y, c = _coords()

        def rows(w, core):
            rh = halves[w].shape[0]
            return outs[w].at[pl.ds(pl.multiple_of(core * rh, 8), rh), :]

        def copy(w, core):
            return pltpu.make_async_remote_copy(
                src_ref=ins[w], dst_ref=rows(w, core), send_sem=send_sems.at[w], recv_sem=recv_sems.at[w],
                device_id=(x, y, 1 - c), device_id_type=MESH)

        local = [pltpu.make_async_copy(ins[w], rows(w, c), loc_sems.at[w]) for w in range(n)]
        sends = [copy(w, c) for w in range(n)]
        for cp in local + sends:
            cp.start()
        for w in range(n):
            copy(w, 1 - c).wait_recv()
        for cp in sends:
            cp.wait_send()
        for cp in local:
            cp.wait()

    return pl.pallas_call(
        body, name="sibling_swap", in_specs=[_HBM] * n, out_specs=[_HBM] * n,
        out_shape=[jax.ShapeDtypeStruct((2 * h.shape[0], h.shape[1]), h.dtype) for h in halves],
        scratch_shapes=[pltpu.SemaphoreType.DMA((n,)), pltpu.SemaphoreType.DMA((n,)), pltpu.SemaphoreType.DMA((n,))],
    )(*halves)


def _adamw(w, g, m, v):
    m = ADAM_B1 * m + (1.0 - ADAM_B1) * g
    v = ADAM_B2 * v + (1.0 - ADAM_B2) * (g * g)
    m_hat = m / (1.0 - ADAM_B1 ** ADAM_STEP)
    v_hat = v / (1.0 - ADAM_B2 ** ADAM_STEP)
    delta = -ADAM_LR * (m_hat / (jnp.sqrt(v_hat) + ADAM_EPS) + ADAM_WD * w)
    return delta, m, v


def _sum_parts(parts, name, tr):
    _, R, C = parts.shape
    assert R % tr == 0

    def body(p_ref, g_ref):
        g = p_ref[0].astype(F32)
        for d in range(1, 8):
            g = g + p_ref[d].astype(F32)
        g_ref[...] = g

    return pl.pallas_call(
        body, name=name, grid=(R // tr,),
        in_specs=[pl.BlockSpec((8, tr, C), lambda i: (0, i, 0))],
        out_specs=pl.BlockSpec((tr, C), lambda i: (i, 0)), out_shape=jax.ShapeDtypeStruct((R, C), F32),
    )(parts)


def _adamw_call(g, w, m, v, name, tr):
    R, C = w.shape
    assert R % tr == 0

    def body(g_ref, w_ref, m_ref, v_ref, d_ref, nm_ref, nv_ref):
        d_ref[...], nm_ref[...], nv_ref[...] = _adamw(w_ref[...], g_ref[...], m_ref[...], v_ref[...])

    tile = pl.BlockSpec((tr, C), lambda i: (i, 0))
    return pl.pallas_call(
        body, name=name, grid=(R // tr,), in_specs=[tile] * 4,
        out_specs=[tile] * 3, out_shape=[jax.ShapeDtypeStruct((R, C), F32)] * 3,
    )(g, w, m, v)


def _sum_adamw_small(parts, w, m, v):
    def body(p_ref, w_ref, m_ref, v_ref, g_ref, d_ref, nm_ref, nv_ref, loss_ref):
        g = p_ref[0]
        for d in range(1, 8):
            g = g + p_ref[d]
        g_ref[...] = g
        d_ref[...], nm_ref[...], nv_ref[...] = _adamw(w_ref[...], g, m_ref[...], v_ref[...])
        row = lax.broadcasted_iota(jnp.int32, g.shape, 0)
        per_row = jnp.sum(jnp.where(row == 6, g, 0.0), axis=1, keepdims=True)
        loss_ref[...] = jnp.zeros((8, LANES), F32) + jnp.sum(per_row, axis=0, keepdims=True)

    return pl.pallas_call(
        body, name="sum_adamw_small",
        out_shape=[jax.ShapeDtypeStruct((8, D_MODEL), F32)] * 4 + [jax.ShapeDtypeStruct((8, LANES), F32)],
    )(parts, w, m, v)


def _pack_small(ln1_g, ln1_b, ln2_g, ln2_b, g_sb, g_fox, b_f):
    row5 = jnp.pad(b_f.reshape(1, N_FOX), ((0, 0), (0, D_MODEL - N_FOX)))
    rows = [ln1_g.reshape(1, -1), ln1_b.reshape(1, -1), ln2_g.reshape(1, -1), ln2_b.reshape(1, -1),
            jnp.concatenate([g_sb.reshape(1, -1), g_fox.reshape(1, -1)], axis=1), row5,
            jnp.zeros((2, D_MODEL), F32)]
    return jnp.concatenate(rows, axis=0)


def _unpack_small(p):
    return {"ln1_g": p[0:1], "ln1_b": p[1:2], "ln2_g": p[2:3], "ln2_b": p[3:4], "g_sb": p[4:5, :GROUP_W],
            "g_fox": p[4:5, GROUP_W:], "b_f": p[5:6, :N_FOX]}


def kernel(x, w_in, b_f, g_sb, g_fox, w_out, ln1_g, ln1_b, ln2_g, ln2_b, w_gate_up, w_down, loss_target, m_w_in, m_b_f, m_g_sb, m_g_fox, m_w_out, m_ln1_g, m_ln1_b, m_ln2_g, m_ln2_b, m_w_gate_up, m_w_down, v_w_in, v_b_f, v_g_sb, v_g_fox, v_w_out, v_ln1_g, v_ln1_b, v_ln2_g, v_ln2_b, v_w_gate_up, v_w_down):
    S = x.shape[1]
    x2 = x.reshape(S, D_MODEL)
    tgt = loss_target.reshape(S, D_MODEL)
    TM = 1024
    TR = 512
    BQ = ATTN_BLOCK
    in_w = w_in.shape[2]
    gu_w = w_gate_up.shape[2]

    shards = [w_in[0].astype(BF16), w_out[0].astype(BF16), w_gate_up[0].astype(BF16), w_down[0].astype(BF16)]
    wi_s, wo_s, wgu_s, wd_s = _allgather_chips(shards)
    wi = wi_s.transpose(1, 0, 2).reshape(D_MODEL, 4 * in_w)
    w_sb, w_fx = wi[:, :QKV_W // 2], wi[:, QKV_W // 2:QKV_W]
    wqkv = wi[:, :QKV_W]
    wft = wi[:, QKV_W:].T
    wo = wo_s.reshape(D_MODEL, D_MODEL)
    wgu = wgu_s.transpose(1, 0, 2).reshape(D_MODEL, 2 * D_FF)
    wg, wu = wgu[:, :D_FF], wgu[:, D_FF:]
    wd = wd_s.reshape(D_FF, D_MODEL)
    g_row = jnp.concatenate([g_sb, g_fox], axis=1)
    hid = np.arange(D_MODEL) // HEAD_DIM
    he_np = (hid[:, None] == np.arange(LANES)[None, :]).astype(np.float32)
    he, het = jnp.asarray(he_np, BF16), jnp.asarray(he_np.T, BF16)

    proj = _matmul(x2, wqkv, mode="nn", name="proj", tm=TM, tn=512, tk=D_MODEL, outs=[BF16])
    lf = _fgate_fwd(x2, wft, b_f.reshape(N_FOX, 1), TM)
    c = _cumsum_fwd(lf)
    c_pair = c.reshape(N_PAIRS, 2, S)
    c_row = jnp.pad(c_pair, ((0, 0), (0, 6), (0, 0)))
    c_col = jnp.pad(c_pair.transpose(0, 2, 1), ((0, 0), (0, 0), (0, 6)))

    o_sb, st_sb, jmin_sb = _sb_fwd(proj, 0, BQ)
    jstart_fx = _fox_start_blocks(proj, 12, c, min(BQ, S))
    o_fx, st_fx = _fox_fwd(proj, 12, c_col, c_row, jstart_fx, BQ)

    def attn_post(i, osb_ref, ofx_ref, g_ref, he_ref, het_ref, on_ref):
        o = jnp.concatenate([osb_ref[...], ofx_ref[...]], axis=1)
        ms = _head_sums(o * o, he_ref[...], het_ref[...]) * (1.0 / HEAD_DIM)
        on_ref[...] = (o * lax.rsqrt(ms + RMS_EPS) * g_ref[...]).astype(BF16)

    (on,) = _rowwise(attn_post, "attn_post", S, TR,
                     [(o_sb, "t"), (o_fx, "t"), (g_row, "f"), (he, "f"), (het, "f")],
                     [((S, D_MODEL), BF16, "t")])

    u1 = _matmul(on, wo, mode="nn", name="mix", tm=TM, tn=D_MODEL, tk=D_MODEL, outs=[F32],
                 extras=[(x2, (TM if S >= TM else S, D_MODEL), _tile_ij)],
                 epilogue=lambda acc, xv: (ALPHA * xv + acc,))

    def ln1_fwd(i, u_ref, g_ref, b_ref, h_ref):
        xh, _ = _ln_stats(u_ref[...])
        h_ref[...] = xh * g_ref[...] + b_ref[...]

    (h1,) = _rowwise(ln1_fwd, "ln1_fwd", S, TR, [(u1, "t"), (ln1_g, "f"), (ln1_b, "f")], [((S, D_MODEL), F32, "t")])

    tm_e = TM if S >= TM else S
    n_ff = D_FF // 256

    def gate_up_body(h_ref, wg_ref, wu_ref, g_ref, u_ref, a_ref):
        h = h_ref[...].astype(BF16)
        g, u = _dot(h, wg_ref[...]), _dot(h, wu_ref[...])
        g_ref[...] = g.astype(BF16)
        u_ref[...] = u.astype(BF16)
        a_ref[...] = (g / (1.0 + jnp.exp(-g)) * u).astype(BF16)

    ff_tile = pl.BlockSpec((tm_e, 256), lambda i, j: (i, j))
    gate, up, act = pl.pallas_call(
        gate_up_body, name="gate_up_act", grid=(S // tm_e, n_ff),
        in_specs=[pl.BlockSpec((tm_e, D_MODEL), lambda i, j: (i, 0)),
                  pl.BlockSpec((D_MODEL, 256), lambda i, j: (0, j)),
                  pl.BlockSpec((D_MODEL, 256), lambda i, j: (0, j + n_ff))],
        out_specs=[ff_tile] * 3, out_shape=[jax.ShapeDtypeStruct((S, D_FF), BF16)] * 3)(h1, wgu, wgu)

    u2 = _matmul(act, wd, mode="nn", name="ffn_down", tm=TM, tn=D_MODEL, tk=D_FF, outs=[F32],
                 extras=[(h1, (TM if S >= TM else S, D_MODEL), _tile_ij)],
                 epilogue=lambda acc, hv: (ALPHA * hv + acc,))

    def ln2_loss(i, u_ref, t_ref, g_ref, b_ref, du_ref, acc_ref):
        xh, r = _ln_stats(u_ref[...])
        g = g_ref[...]
        err = xh * g + b_ref[...] - t_ref[...]
        dy = err * (1.0 / D_MODEL)
        du_ref[...] = _ln_bwd(dy, xh, r, g)
        _acc_rows(i, acc_ref, {2: jnp.sum(dy * xh, axis=0, keepdims=True), 3: jnp.sum(dy, axis=0, keepdims=True),
                               6: jnp.sum(err * err, axis=0, keepdims=True) * (0.5 / D_MODEL)})

    du2, acc_ln2 = _rowwise(ln2_loss, "ln2_loss", S, TR, [(u2, "t"), (tgt, "t"), (ln2_g, "f"), (ln2_b, "f")],
                            [((S, D_MODEL), F32, "t"), ((8, D_MODEL), F32, "f")])

    d_wd = _matmul(act, du2, mode="tn", name="dw_down", tm=1408, tn=D_MODEL, tk=TM, outs=[F32])

    def dgu_epilogue(da, g, u):
        g, u = g.astype(F32), u.astype(F32)
        s = 1.0 / (1.0 + jnp.exp(-g))
        return da * u * (s * (1.0 + g * (1.0 - s))), da * (g * s)

    dgate, dup = _matmul(du2, wd, mode="nt", name="d_act", tm=TM, tn=1408, tk=D_MODEL, outs=[BF16, BF16],
                         extras=[(gate, (tm_e, 1408), _tile_ij), (up, (tm_e, 1408), _tile_ij)],
                         epilogue=dgu_epilogue)
    d_wg = _matmul(h1, dgate, mode="tn", name="dw_gate", tm=D_MODEL, tn=1408, tk=TM, outs=[F32])
    d_wu = _matmul(h1, dup, mode="tn", name="dw_up", tm=D_MODEL, tn=1408, tk=TM, outs=[F32])
    dh1 = _matmul(dgate, wg, mode="nt", name="dh1_gate", tm=TM, tn=D_MODEL, tk=D_FF, outs=[F32],
                  extras=[(du2, (tm_e, D_MODEL), _tile_ij)], epilogue=lambda acc, e: (ALPHA * e + acc,))
    dh1 = _matmul(dup, wu, mode="nt", name="dh1_up", tm=TM, tn=D_MODEL, tk=D_FF, outs=[F32],
                  extras=[(dh1, (tm_e, D_MODEL), _tile_ij)], epilogue=lambda acc, e: (e + acc,))

    def ln1_bwd(i, dh_ref, u_ref, g_ref, du_ref, acc_ref):
        xh, r = _ln_stats(u_ref[...])
        dh = dh_ref[...]
        du_ref[...] = _ln_bwd(dh, xh, r, g_ref[...])
        _acc_rows(i, acc_ref, {0: jnp.sum(dh * xh, axis=0, keepdims=True), 1: jnp.sum(dh, axis=0, keepdims=True)})

    du1, acc_ln1 = _rowwise(ln1_bwd, "ln1_bwd", S, TR, [(dh1, "t"), (u1, "t"), (ln1_g, "f")],
                            [((S, D_MODEL), F32, "t"), ((8, D_MODEL), F32, "f")])
    d_wo = _matmul(on, du1, mode="tn", name="dw_out", tm=D_MODEL, tn=D_MODEL, tk=TM, outs=[F32])
    don = _matmul(du1, wo, mode="nt", name="d_on", tm=TM, tn=D_MODEL, tk=D_MODEL, outs=[F32])

    def rms_bwd(i, don_ref, osb_ref, ofx_ref, g_ref, he_ref, het_ref, dosb_ref, dofx_ref, acc_ref):
        o = jnp.concatenate([osb_ref[...], ofx_ref[...]], axis=1)
        hev, hetv = he_ref[...], het_ref[...]
        r = lax.rsqrt(_head_sums(o * o, hev, hetv) * (1.0 / HEAD_DIM) + RMS_EPS)
        dn = don_ref[...]
        dg = dn * g_ref[...]
        do = r * dg - o * (r * r * r) * (_head_sums(dg * o, hev, hetv) * (1.0 / HEAD_DIM))
        dosb_ref[...] = do[:, :GROUP_W]
        dofx_ref[...] = do[:, GROUP_W:]
        _acc_rows(i, acc_ref, {4: jnp.sum(dn * o * r, axis=0, keepdims=True)})

    do_sb, do_fx, acc_rms = _rowwise(
        rms_bwd, "rms_bwd", S, TR, [(don, "t"), (o_sb, "t"), (o_fx, "t"), (g_row, "f"), (he, "f"), (het, "f")],
        [((S, GROUP_W), F32, "t"), ((S, GROUP_W), F32, "t"), ((8, D_MODEL), F32, "f")])

    dq_sb, dk_sb, dv_sb = _sb_bwd(proj, 0, do_sb, st_sb, jmin_sb, BQ)
    dq_fx, dk_fx, dv_fx, dc, dcq = _fox_bwd(proj, 12, do_fx, o_fx, st_fx, c_col, c_row, jstart_fx, BQ)
    dc = dc[:, :2, :] + dcq[:, :, :2].transpose(0, 2, 1)
    dfl, dbf = _fgate_bwd(dc.reshape(N_FOX, S), lf)
    dp_sb = jnp.concatenate([dq_sb, dk_sb, dv_sb], axis=1).astype(BF16)
    dp_fx = jnp.concatenate([dq_fx, dk_fx, dv_fx], axis=1).astype(BF16)

    d_wsb = _matmul(x2, dp_sb, mode="tn", name="dw_in_sb", tm=D_MODEL, tn=QKV_W // 2, tk=TM, outs=[F32])
    d_wfx = _matmul(x2, dp_fx, mode="tn", name="dw_in_fx", tm=D_MODEL, tn=QKV_W // 2, tk=TM, outs=[F32])
    d_wft = _matmul(dfl, x2, mode="nn", name="dw_in_f", tm=N_FOX, tn=D_MODEL, tk=TM, outs=[F32])
    dx = _matmul(dp_sb, w_sb, mode="nt", name="dx_sb", tm=TM, tn=D_MODEL, tk=QKV_W // 2, outs=[F32],
                 extras=[(du1, (tm_e, D_MODEL), _tile_ij)], epilogue=lambda acc, e: (ALPHA * e + acc,))
    dx = _matmul(dp_fx, w_fx, mode="nt", name="dx_fx", tm=TM, tn=D_MODEL, tk=QKV_W // 2, outs=[F32],
                 extras=[(dx, (tm_e, D_MODEL), _tile_ij)], epilogue=lambda acc, e: (e + acc,))
    dx = _matmul(dfl, wft, mode="tn", name="dx_f", tm=TM, tn=D_MODEL, tk=N_FOX, outs=[F32],
                 extras=[(dx, (tm_e, D_MODEL), _tile_ij)], epilogue=lambda acc, e: (e + acc,))

    d_wi = jnp.concatenate([d_wsb, d_wfx, d_wft.T], axis=1)
    d_wgu = jnp.concatenate([d_wg, d_wu], axis=1)
    parts = [d_wi.reshape(D_MODEL, 4, in_w).transpose(1, 0, 2).astype(BF16),
             d_wo.reshape(4, D_MODEL // 4, D_MODEL).astype(BF16),
             d_wgu.reshape(D_MODEL, 4, gu_w).transpose(1, 0, 2).astype(BF16),
             d_wd.reshape(4, D_FF // 4, D_MODEL).astype(BF16)]
    got = _exchange(parts, True)
    big_names = ("w_in", "w_out", "w_gate_up", "w_down")
    halves = [_sum_parts(p, "sum_" + nm, tr) for nm, p, tr in zip(big_names, got, (256, 128, 128, 176))]
    grads = _sibling_swap(halves)
    big = {}
    for nm, g, w, m, v, tr in zip(big_names, grads, (w_in, w_out, w_gate_up, w_down),
                                  (m_w_in, m_w_out, m_w_gate_up, m_w_down),
                                  (v_w_in, v_w_out, v_w_gate_up, v_w_down), (256, 256, 256, 176)):
        big[nm] = [r[None] for r in [g] + list(_adamw_call(g, w[0], m[0], v[0], "adamw_" + nm, tr))]

    small = acc_ln2 + acc_ln1 + acc_rms
    small = small + jnp.pad(dbf.reshape(1, N_FOX), ((5, 2), (0, D_MODEL - N_FOX)))
    (small_all,) = _exchange([small], False)
    sw = _pack_small(ln1_g, ln1_b, ln2_g, ln2_b, g_sb, g_fox, b_f)
    sm = _pack_small(m_ln1_g, m_ln1_b, m_ln2_g, m_ln2_b, m_g_sb, m_g_fox, m_b_f)
    sv = _pack_small(v_ln1_g, v_ln1_b, v_ln2_g, v_ln2_b, v_g_sb, v_g_fox, v_b_f)
    sg, sd, snm, snv, loss_blk = _sum_adamw_small(small_all, sw, sm, sv)
    sg, sd, snm, snv = _unpack_small(sg), _unpack_small(sd), _unpack_small(snm), _unpack_small(snv)

    names = ["w_in", "b_f", "g_sb", "g_fox", "w_out", "ln1_g", "ln1_b", "ln2_g", "ln2_b", "w_gate_up", "w_down"]
    outs = [loss_blk[0, 0], dx.reshape(1, S, D_MODEL)]
    for k, table in enumerate((sg, sd, snm, snv)):
        outs += [big[n][k] if n in big else table[n] for n in names]
    return tuple(outs)
```

```python
import functools

import numpy as np
import jax
import jax.numpy as jnp
from jax import lax
from jax.experimental import pallas as pl
from jax.experimental.pallas import tpu as pltpu

F32 = jnp.float32
BF16 = jnp.bfloat16

D_MODEL = 1024
HEAD_DIM = 64
LANES = 128
N_PAIRS = 4
GROUP_W = 512
QKV_W = 3072
D_FF = 2816
N_FOX = 8
ALPHA = 2.0 ** 0.25
LN_EPS = 1e-5
RMS_EPS = 1e-6
SCALE = HEAD_DIM ** -0.5
NEG_BIG = -1e30
FOX_SKIP = 30.0
SB_STOP = -105.0
ADAM_LR, ADAM_B1, ADAM_B2, ADAM_EPS, ADAM_WD, ADAM_STEP = 0.001, 0.9, 0.999, 1e-08, 0.01, 10
ATTN_BLOCK = 256
VMEM_BIG = 56 * 1024 * 1024
MESH = pl.DeviceIdType.MESH

_NN = (((1,), (0,)), ((), ()))
_NT = (((1,), (1,)), ((), ()))
_TN = (((0,), (0,)), ((), ()))


def _dot(a, b, dims=_NN):
    return lax.dot_general(a, b, dims, preferred_element_type=F32)


def _split_dot(x, t):
    hi = x.astype(BF16)
    lo = (x - hi.astype(F32)).astype(BF16)
    return _dot(hi, t) + _dot(lo, t)


def _softplus(z):
    return jnp.maximum(z, 0.0) + jnp.log1p(jnp.exp(-jnp.abs(z)))


def _col(v, h):
    lane = lax.broadcasted_iota(jnp.int32, v.shape, 1)
    return jnp.sum(jnp.where(lane == h, v, 0.0), axis=1, keepdims=True)


def _two_sum(hi, lo, b):
    s = hi + b
    bb = s - hi
    err = (hi - (s - bb)) + (b - bb)
    return s, lo + err


def _params(vmem=None):
    return pltpu.CompilerParams(vmem_limit_bytes=vmem) if vmem else None


def _matmul(a, b, *, mode, name, tm, tn, tk, outs, extras=(), epilogue=None, vmem=None):
    if mode == "nn":
        (M, K), (_, N) = a.shape, b.shape
    elif mode == "nt":
        (M, K), (N, _) = a.shape, b.shape
    else:
        (K, M), (_, N) = a.shape, b.shape
    tm, tn, tk = min(tm, M), min(tn, N), min(tk, K)
    assert M % tm == 0 and N % tn == 0 and K % tk == 0, (name, M, N, K, tm, tn, tk)
    nk = K // tk
    dims = {"nn": _NN, "nt": _NT, "tn": _TN}[mode]
    if mode == "tn":
        a_spec = pl.BlockSpec((tk, tm), lambda i, j, k: (k, i))
    else:
        a_spec = pl.BlockSpec((tm, tk), lambda i, j, k: (i, k))
    if mode == "nt":
        b_spec = pl.BlockSpec((tn, tk), lambda i, j, k: (j, k))
    else:
        b_spec = pl.BlockSpec((tk, tn), lambda i, j, k: (k, j))
    ex_specs = [pl.BlockSpec(bs, (lambda i, j, k, f=f: f(i, j))) for (_, bs, f) in extras]
    ne, no = len(extras), len(outs)
    if epilogue is None:
        epilogue = lambda acc: (acc,)

    def body(a_ref, b_ref, *rest):
        ex_refs, out_refs, acc = rest[:ne], rest[ne:ne + no], rest[-1]
        k = pl.program_id(2)

        @pl.when(k == 0)
        def _():
            acc[...] = jnp.zeros_like(acc)

        acc[...] += _dot(a_ref[...].astype(BF16), b_ref[...].astype(BF16), dims)

        @pl.when(k == nk - 1)
        def _():
            res = epilogue(acc[...], *[e[...] for e in ex_refs])
            for r, o in zip(res, out_refs):
                o[...] = r.astype(o.dtype)

    res = pl.pallas_call(
        body, name=name, grid=(M // tm, N // tn, nk),
        in_specs=[a_spec, b_spec] + ex_specs,
        out_specs=[pl.BlockSpec((tm, tn), lambda i, j, k: (i, j)) for _ in outs],
        out_shape=[jax.ShapeDtypeStruct((M, N), d) for d in outs],
        scratch_shapes=[pltpu.VMEM((tm, tn), F32)],
        compiler_params=_params(vmem),
    )(a, b, *[e[0] for e in extras])
    return res[0] if no == 1 else res


def _tile_ij(i, j):
    return (i, j)


def _rowwise(fn, name, rows, tm, ins, outs, vmem=None):
    tm = min(tm, rows)
    assert rows % tm == 0

    def spec(shape, kind):
        if kind == "t":
            return pl.BlockSpec((tm,) + tuple(shape[1:]), lambda i: (i,) + (0,) * (len(shape) - 1))
        return pl.BlockSpec(tuple(shape), lambda i: (0,) * len(shape))

    def body(*refs):
        fn(pl.program_id(0), *refs)

    return pl.pallas_call(
        body, name=name, grid=(rows // tm,),
        in_specs=[spec(a.shape, k) for a, k in ins],
        out_specs=[spec(s, k) for s, _, k in outs],
        out_shape=[jax.ShapeDtypeStruct(s, d) for s, d, _ in outs],
        compiler_params=_params(vmem),
    )(*[a for a, _ in ins])


def _ln_stats(u):
    mu = jnp.mean(u, axis=-1, keepdims=True)
    d = u - mu
    var = jnp.mean(d * d, axis=-1, keepdims=True)
    r = lax.rsqrt(var + LN_EPS)
    return d * r, r


def _ln_bwd(dh, xh, r, g):
    dxh = dh * g
    m1 = jnp.mean(dxh, axis=-1, keepdims=True)
    m2 = jnp.mean(dxh * xh, axis=-1, keepdims=True)
    return r * (dxh - m1 - xh * m2)


def _acc_rows(i, ref, rows):
    @pl.when(i == 0)
    def _():
        ref[...] = jnp.zeros_like(ref)
    for r, v in rows.items():
        ref[pl.ds(r, 1), :] += v


def _head_sums(v, he, het):
    return _split_dot(_split_dot(v, he), het)


def _fgate_fwd(x, wft, bf_col, tm):
    S = x.shape[0]
    tm = min(tm, S)

    def body(wft_ref, bf_ref, x_ref, lf_ref):
        f = _dot(wft_ref[...], x_ref[...].astype(BF16), _NT) + bf_ref[...]
        lf_ref[...] = -_softplus(-f)

    return pl.pallas_call(
        body, name="fgate_fwd", grid=(S // tm,),
        in_specs=[pl.BlockSpec((N_FOX, D_MODEL), lambda i: (0, 0)), pl.BlockSpec((N_FOX, 1), lambda i: (0, 0)),
                  pl.BlockSpec((tm, D_MODEL), lambda i: (i, 0))],
        out_specs=pl.BlockSpec((N_FOX, tm), lambda i: (0, i)),
        out_shape=jax.ShapeDtypeStruct((N_FOX, S), F32),
    )(wft, bf_col, x)


def _chunk_scan(v, reverse):
    lane = lax.broadcasted_iota(jnp.int32, v.shape, 1)
    sh = 1
    while sh < LANES:
        if reverse:
            v = v + jnp.where(lane < LANES - sh, pltpu.roll(v, LANES - sh, 1), 0.0)
        else:
            v = v + jnp.where(lane >= sh, pltpu.roll(v, sh, 1), 0.0)
        sh *= 2
    return v


def _cumsum_fwd(lf):
    n, S = lf.shape
    nc = S // LANES

    def body(lf_ref, c_ref):
        def step(ci, carry):
            sl = pl.ds(pl.multiple_of(ci * LANES, LANES), LANES)
            v = _chunk_scan(lf_ref[:, sl], False) + carry
            c_ref[:, sl] = v
            return _col(v, LANES - 1)
        lax.fori_loop(0, nc, step, jnp.zeros((n, 1), F32))

    return pl.pallas_call(body, name="cumsum_fwd", out_shape=jax.ShapeDtypeStruct((n, S), F32))(lf)


def _fgate_bwd(dc, lf):
    n, S = dc.shape
    nc = S // LANES

    def body(dc_ref, lf_ref, dfl_ref, dbf_ref):
        def step(t, carry):
            car, tot = carry
            ci = nc - 1 - t
            sl = pl.ds(pl.multiple_of(ci * LANES, LANES), LANES)
            dlf = _chunk_scan(dc_ref[:, sl], True) + car
            dfl = dlf * (1.0 - jnp.exp(lf_ref[:, sl]))
            dfl_ref[:, sl] = dfl
            return _col(dlf, 0), tot + jnp.sum(dfl, axis=1, keepdims=True)
        _, tot = lax.fori_loop(0, nc, step, (jnp.zeros((n, 1), F32), jnp.zeros((n, 1), F32)))
        dbf_ref[...] = tot

    return pl.pallas_call(body, name="fgate_bwd",
                          out_shape=[jax.ShapeDtypeStruct((n, S), F32), jax.ShapeDtypeStruct((n, 1), F32)])(dc, lf)


def _tri_matrices(b):
    r = np.arange(b)
    tfwd = (r[:, None] <= r[None, :]).astype(np.float32)
    return jnp.asarray(tfwd, BF16), jnp.asarray(tfwd.T, BF16)


def _kv_copies(kv_hbm, kbuf, vbuf, sems, pair_col, bq, j, slot):
    rows = pl.ds(pl.multiple_of(j * bq, bq), bq)

    def cols(c):
        return pl.ds(pl.multiple_of((pair_col + c) * LANES, LANES), LANES)

    return (pltpu.make_async_copy(kv_hbm.at[rows, cols(4)], kbuf.at[slot], sems.at[0, slot]),
            pltpu.make_async_copy(kv_hbm.at[rows, cols(8)], vbuf.at[slot], sems.at[1, slot]))


def _masked_pair(v, lane_is_a, scale=1.0):
    v = v.astype(F32) * scale
    return jnp.where(lane_is_a, v, 0.0).astype(BF16), jnp.where(lane_is_a, 0.0, v).astype(BF16)


def _sb_fwd(proj, col0, bq):
    S = proj.shape[0]
    bq = min(bq, S)
    nq = S // bq
    _, trev = _tri_matrices(bq)

    def body(q_ref, kv_hbm, trev_ref, o_ref, st_ref, jmin_ref, acc_a, acc_b, qa, qb, rs, kbuf, vbuf, sems):
        p, i = pl.program_id(0), pl.program_id(1)
        fetch = functools.partial(_kv_copies, kv_hbm, kbuf, vbuf, sems, col0 + p, bq)
        for cp in fetch(i, 0):
            cp.start()
        is_a = lax.broadcasted_iota(jnp.int32, (bq, LANES), 1) < HEAD_DIM
        acc_a[...] = jnp.zeros_like(acc_a)
        acc_b[...] = jnp.zeros_like(acc_b)
        rs[...] = jnp.zeros_like(rs)
        qa[...], qb[...] = _masked_pair(q_ref[...], is_a, SCALE)

        def tile(slot, masked):
            k, v, trev_m = kbuf[slot], vbuf[slot], trev_ref[...]
            if masked:
                tri = lax.broadcasted_iota(jnp.int32, (bq, bq), 0) > lax.broadcasted_iota(jnp.int32, (bq, bq), 1)
            for h, (qh, acc) in enumerate(((qa, acc_a), (qb, acc_b))):
                z = _dot(qh[...], k, _NT)
                lk = -_softplus(z)
                if masked:
                    lk = jnp.where(tri, lk, 0.0)
                r_hi, r_lo = rs[2 * h], rs[2 * h + 1]
                w = jnp.exp(z + _split_dot(lk, trev_m) + (r_hi + r_lo))
                if masked:
                    w = jnp.where(tri, w, 0.0)
                acc[...] += _dot(w.astype(BF16), v)
                rs[2 * h], rs[2 * h + 1] = _two_sum(r_hi, r_lo, jnp.sum(lk, axis=1, keepdims=True))

        def step(carry):
            j, _ = carry
            slot = lax.rem(i - j, 2)
            for cp in fetch(j, slot):
                cp.wait()

            @pl.when(j > 0)
            def _():
                for cp in fetch(j - 1, 1 - slot):
                    cp.start()

            pl.when(j == i)(functools.partial(tile, slot, True))
            pl.when(j < i)(functools.partial(tile, slot, False))
            live = jnp.max(jnp.maximum(rs[0], rs[2])) > SB_STOP
            return j - 1, live.astype(jnp.int32)

        j_end, _ = lax.while_loop(lambda c: jnp.logical_and(c[0] >= 0, c[1] > 0), step, (i, jnp.int32(1)))

        @pl.when(j_end >= 0)
        def _():
            for cp in fetch(j_end, lax.rem(i - j_end, 2)):
                cp.wait()

        jmin_ref[p, i] = j_end + 1
        o_ref[...] = jnp.where(is_a, acc_a[...], acc_b[...])
        lane8 = lax.broadcasted_iota(jnp.int32, (bq, 8), 1)
        st = jnp.zeros((bq, 8), F32)
        for c, src in enumerate((0, 2, 1, 3)):
            st = jnp.where(lane8 == c, rs[src], st)
        st_ref[0] = st

    return pl.pallas_call(
        body, name="sb_fwd", grid=(N_PAIRS, nq),
        in_specs=[pl.BlockSpec((bq, LANES), lambda p, i: (i, col0 + p)),
                  pl.BlockSpec(memory_space=pl.ANY),
                  pl.BlockSpec((bq, bq), lambda p, i: (0, 0))],
        out_specs=[pl.BlockSpec((bq, LANES), lambda p, i: (i, p)),
                   pl.BlockSpec((1, bq, 8), lambda p, i: (p, i, 0)),
                   pl.BlockSpec(memory_space=pltpu.SMEM)],
        out_shape=[jax.ShapeDtypeStruct((S, GROUP_W), F32), jax.ShapeDtypeStruct((N_PAIRS, S, 8), F32),
                   jax.ShapeDtypeStruct((N_PAIRS, nq), jnp.int32)],
        scratch_shapes=[pltpu.VMEM((bq, LANES), F32), pltpu.VMEM((bq, LANES), F32),
                        pltpu.VMEM((bq, LANES), BF16), pltpu.VMEM((bq, LANES), BF16),
                        pltpu.VMEM((4, bq, 1), F32),
                        pltpu.VMEM((2, bq, LANES), BF16), pltpu.VMEM((2, bq, LANES), BF16),
                        pltpu.SemaphoreType.DMA((2, 2))],
    )(proj, proj, trev)


def _sb_bwd(proj, col0, do, st, jmin, bq):
    S = proj.shape[0]
    bq = min(bq, S)
    nq = S // bq
    tfwd, trev = _tri_matrices(bq)

    def body(jmin_ref, q_ref, kv_hbm, do_ref, st_ref, tfwd_ref, trev_ref,
             dq_ref, dk_ref, dv_ref, dq_a, dq_b, qa, qb, doa, dob, rs, kbuf, vbuf, sems):
        p, i = pl.program_id(0), pl.program_id(1)
        fetch = functools.partial(_kv_copies, kv_hbm, kbuf, vbuf, sems, col0 + p, bq)
        j0 = jmin_ref[p, i]
        for cp in fetch(j0, 0):
            cp.start()
        is_a = lax.broadcasted_iota(jnp.int32, (bq, LANES), 1) < HEAD_DIM

        @pl.when(i == 0)
        def _():
            dk_ref[...] = jnp.zeros_like(dk_ref)
            dv_ref[...] = jnp.zeros_like(dv_ref)

        dq_a[...] = jnp.zeros_like(dq_a)
        dq_b[...] = jnp.zeros_like(dq_b)
        rs[...] = jnp.zeros_like(rs)
        st_v = st_ref[0]
        for h in range(2):
            rs[6 + 2 * h], rs[7 + 2 * h] = _col(st_v, h), _col(st_v, 2 + h)
        qa[...], qb[...] = _masked_pair(q_ref[...], is_a, SCALE)
        doa[...], dob[...] = _masked_pair(do_ref[...], is_a)

        def tile(j, slot, masked):
            k, v = kbuf[slot], vbuf[slot]
            tfwd_m, trev_m = tfwd_ref[...], trev_ref[...]
            if masked:
                tri = lax.broadcasted_iota(jnp.int32, (bq, bq), 0) > lax.broadcasted_iota(jnp.int32, (bq, bq), 1)
            dzs, ws = [], []
            for h, (qh, doh, dq) in enumerate(((qa, doa, dq_a), (qb, dob, dq_b))):
                z = _dot(qh[...], k, _NT)
                lk = -_softplus(z)
                if masked:
                    lk = jnp.where(tri, lk, 0.0)
                p_hi, p_lo = _two_sum(rs[3 * h], rs[3 * h + 1], jnp.sum(lk, axis=1, keepdims=True))
                rs[3 * h], rs[3 * h + 1] = p_hi, p_lo
                right = (rs[6 + 2 * h] - p_hi) + (rs[7 + 2 * h] - p_lo)
                w = jnp.exp(z + _split_dot(lk, trev_m) + right)
                if masked:
                    w = jnp.where(tri, w, 0.0)
                g = _dot(doh[...], v, _NT) * w
                g_left = rs[3 * h + 2]
                dz = g - jnp.exp(z + lk) * (_split_dot(g, tfwd_m) + g_left)
                if masked:
                    dz = jnp.where(tri, dz, 0.0)
                rs[3 * h + 2] = g_left + jnp.sum(g, axis=1, keepdims=True)
                dzb = dz.astype(BF16)
                dq[...] += _dot(dzb, k)
                dzs.append(dzb)
                ws.append(w.astype(BF16))
            rows = pl.ds(pl.multiple_of(j * bq, bq), bq)
            dk_ref[rows, :] += _dot(dzs[0], qa[...], _TN) + _dot(dzs[1], qb[...], _TN)
            dv_ref[rows, :] += _dot(ws[0], doa[...], _TN) + _dot(ws[1], dob[...], _TN)

        _walk_up(fetch, j0, i, tile)
        dq_ref[...] = jnp.where(is_a, dq_a[...], dq_b[...]) * SCALE

    grid_spec = pltpu.PrefetchScalarGridSpec(
        num_scalar_prefetch=1, grid=(N_PAIRS, nq),
        in_specs=[pl.BlockSpec((bq, LANES), lambda p, i, jm: (i, col0 + p)),
                  pl.BlockSpec(memory_space=pl.ANY),
                  pl.BlockSpec((bq, LANES), lambda p, i, jm: (i, p)),
                  pl.BlockSpec((1, bq, 8), lambda p, i, jm: (p, i, 0)),
                  pl.BlockSpec((bq, bq), lambda p, i, jm: (0, 0)),
                  pl.BlockSpec((bq, bq), lambda p, i, jm: (0, 0))],
        out_specs=[pl.BlockSpec((bq, LANES), lambda p, i, jm: (i, p)),
                   pl.BlockSpec((S, LANES), lambda p, i, jm: (0, p)),
                   pl.BlockSpec((S, LANES), lambda p, i, jm: (0, p))],
        scratch_shapes=[pltpu.VMEM((bq, LANES), F32), pltpu.VMEM((bq, LANES), F32)]
        + [pltpu.VMEM((bq, LANES), BF16)] * 4 + [pltpu.VMEM((10, bq, 1), F32)]
        + [pltpu.VMEM((2, bq, LANES), BF16), pltpu.VMEM((2, bq, LANES), BF16), pltpu.SemaphoreType.DMA((2, 2))])
    return pl.pallas_call(
        body, name="sb_bwd", grid_spec=grid_spec,
        out_shape=[jax.ShapeDtypeStruct((S, GROUP_W), F32)] * 3,
        compiler_params=_params(VMEM_BIG),
    )(jmin, proj, proj, do, st, tfwd, trev)


def _walk_up(fetch, j0, i, tile):
    def step(j, carry):
        slot = lax.rem(j - j0, 2)
        for cp in fetch(j, slot):
            cp.wait()

        @pl.when(j < i)
        def _():
            for cp in fetch(j + 1, 1 - slot):
                cp.start()

        pl.when(j == i)(functools.partial(tile, j, slot, True))
        pl.when(j < i)(functools.partial(tile, j, slot, False))
        return carry

    lax.fori_loop(j0, i + 1, step, 0)


def _fox_start_blocks(proj, col0, c, bq):
    S = proj.shape[0]
    nq = S // bq
    nh = 2 * N_PAIRS

    def heads(first):
        return proj[:, first * LANES:(first + N_PAIRS) * LANES].astype(F32).reshape(S, nh, HEAD_DIM)

    q, k = heads(col0), heads(col0 + 4)
    qn = jnp.sqrt(jnp.sum(q * q, axis=-1))
    kmax = jnp.sqrt(jnp.sum(k * k, axis=-1)).max(axis=0)
    top = SCALE * (qn * kmax[None, :] - jnp.sum(q * k, axis=-1)) + c.T
    top = top.reshape(nq, bq, nh).max(axis=1)
    c_last = c[:, bq - 1::bq].T
    live = top[:, None, :] - c_last[None, :, :] >= -FOX_SKIP

    def first_block(lv):
        first = jnp.where(lv.any(axis=1), jnp.argmax(lv, axis=1), nq)
        return jnp.minimum(first, jnp.arange(nq)[:, None]).T.astype(jnp.int32)

    return jnp.concatenate([first_block(live.reshape(nq, nq, N_PAIRS, 2).any(axis=-1)), first_block(live)], axis=0)


def _fox_fwd(proj, col0, c_col, c_row, jstart, bq):
    S = proj.shape[0]
    bq = min(bq, S)
    nq = S // bq

    def body(js_ref, q_ref, kv_hbm, cc_ref, cr_ref, o_ref, st_ref, acc_a, acc_b, qa, qb, ml, kbuf, vbuf, sems):
        p, i = pl.program_id(0), pl.program_id(1)
        fetch = functools.partial(_kv_copies, kv_hbm, kbuf, vbuf, sems, col0 + p, bq)
        j0 = js_ref[p, i]
        for cp in fetch(j0, 0):
            cp.start()
        is_a = lax.broadcasted_iota(jnp.int32, (bq, LANES), 1) < HEAD_DIM
        acc_a[...] = jnp.zeros_like(acc_a)
        acc_b[...] = jnp.zeros_like(acc_b)
        ml[0] = jnp.full((bq, 1), NEG_BIG, F32)
        ml[2] = jnp.full((bq, 1), NEG_BIG, F32)
        ml[1] = jnp.zeros((bq, 1), F32)
        ml[3] = jnp.zeros((bq, 1), F32)
        cc = cc_ref[0]
        ml[4], ml[5] = _col(cc, 0), _col(cc, 1)
        qa[...], qb[...] = _masked_pair(q_ref[...], is_a, SCALE)

        def tile(j, slot, masked):
            k, v = kbuf[slot], vbuf[slot]
            cols = pl.ds(pl.multiple_of(j * bq, bq), bq)
            if masked:
                tri = lax.broadcasted_iota(jnp.int32, (bq, bq), 0) >= lax.broadcasted_iota(jnp.int32, (bq, bq), 1)
            def head(h, qh, acc):
                s = _dot(qh[...], k, _NT) - cr_ref[0, pl.ds(h, 1), cols]
                if masked:
                    s = jnp.where(tri, s, NEG_BIG)
                m_prev, l_prev, cq = ml[2 * h], ml[2 * h + 1], ml[4 + h]
                m_new = jnp.maximum(m_prev, jnp.max(s, axis=1, keepdims=True) + cq)
                a = jnp.exp(m_prev - m_new)
                p = jnp.exp(s - (m_new - cq))
                ml[2 * h] = m_new
                ml[2 * h + 1] = a * l_prev + jnp.sum(p, axis=1, keepdims=True)
                acc[...] = a * acc[...] + _dot(p.astype(BF16), v)

            for h, (qh, acc) in enumerate(((qa, acc_a), (qb, acc_b))):
                pl.when(j >= js_ref[N_PAIRS + 2 * p + h, i])(functools.partial(head, h, qh, acc))

        _walk_up(fetch, j0, i, tile)
        o_ref[...] = jnp.where(is_a, acc_a[...] / ml[1], acc_b[...] / ml[3])
        lane8 = lax.broadcasted_iota(jnp.int32, (bq, 8), 1)
        st = jnp.where(lane8 == 0, ml[0] + jnp.log(ml[1]), 0.0)
        st_ref[0] = jnp.where(lane8 == 1, ml[2] + jnp.log(ml[3]), st)

    grid_spec = pltpu.PrefetchScalarGridSpec(
        num_scalar_prefetch=1, grid=(N_PAIRS, nq),
        in_specs=[pl.BlockSpec((bq, LANES), lambda p, i, js: (i, col0 + p)),
                  pl.BlockSpec(memory_space=pl.ANY),
                  pl.BlockSpec((1, bq, 8), lambda p, i, js: (p, i, 0)),
                  pl.BlockSpec((1, 8, S), lambda p, i, js: (p, 0, 0))],
        out_specs=[pl.BlockSpec((bq, LANES), lambda p, i, js: (i, p)),
                   pl.BlockSpec((1, bq, 8), lambda p, i, js: (p, i, 0))],
        scratch_shapes=[pltpu.VMEM((bq, LANES), F32), pltpu.VMEM((bq, LANES), F32),
                        pltpu.VMEM((bq, LANES), BF16), pltpu.VMEM((bq, LANES), BF16),
                        pltpu.VMEM((6, bq, 1), F32),
                        pltpu.VMEM((2, bq, LANES), BF16), pltpu.VMEM((2, bq, LANES), BF16),
                        pltpu.SemaphoreType.DMA((2, 2))])
    return pl.pallas_call(
        body, name="fox_fwd", grid_spec=grid_spec,
        out_shape=[jax.ShapeDtypeStruct((S, GROUP_W), F32), jax.ShapeDtypeStruct((N_PAIRS, S, 8), F32)],
    )(jstart, proj, proj, c_col, c_row)


def _fox_bwd(proj, col0, do, o, st, c_col, c_row, jstart, bq):
    S = proj.shape[0]
    bq = min(bq, S)
    nq = S // bq

    def body(js_ref, q_ref, kv_hbm, do_ref, o_ref, st_ref, cc_ref, cr_ref,
             dq_ref, dk_ref, dv_ref, dc_ref, dcq_ref, dq_a, dq_b, qa, qb, doa, dob, dd, kbuf, vbuf, sems):
        p, i = pl.program_id(0), pl.program_id(1)
        fetch = functools.partial(_kv_copies, kv_hbm, kbuf, vbuf, sems, col0 + p, bq)
        j0 = js_ref[p, i]
        for cp in fetch(j0, 0):
            cp.start()
        is_a = lax.broadcasted_iota(jnp.int32, (bq, LANES), 1) < HEAD_DIM

        @pl.when(i == 0)
        def _():
            dk_ref[...] = jnp.zeros_like(dk_ref)
            dv_ref[...] = jnp.zeros_like(dv_ref)
            dc_ref[...] = jnp.zeros_like(dc_ref)

        dq_a[...] = jnp.zeros_like(dq_a)
        dq_b[...] = jnp.zeros_like(dq_b)
        qa[...], qb[...] = _masked_pair(q_ref[...], is_a, SCALE)
        dov = do_ref[...]
        doa[...], dob[...] = _masked_pair(dov, is_a)
        prod = dov * o_ref[...]
        dd[0] = jnp.sum(jnp.where(is_a, prod, 0.0), axis=1, keepdims=True)
        dd[1] = jnp.sum(jnp.where(is_a, 0.0, prod), axis=1, keepdims=True)
        dd[2] = jnp.zeros((bq, 1), F32)
        dd[3] = jnp.zeros((bq, 1), F32)
        cc, st_v = cc_ref[0], st_ref[0]
        dd[4], dd[5] = _col(cc, 0) - _col(st_v, 0), _col(cc, 1) - _col(st_v, 1)

        def tile(j, slot, masked):
            k, v = kbuf[slot], vbuf[slot]
            if masked:
                tri = lax.broadcasted_iota(jnp.int32, (bq, bq), 0) >= lax.broadcasted_iota(jnp.int32, (bq, bq), 1)
            cols = pl.ds(pl.multiple_of(j * bq, bq), bq)

            def head(h, qh, doh, dq):
                pr = jnp.exp(_dot(qh[...], k, _NT) - cr_ref[0, pl.ds(h, 1), cols] + dd[4 + h])
                if masked:
                    pr = jnp.where(tri, pr, 0.0)
                ds = pr * (_dot(doh[...], v, _NT) - dd[h])
                dc_ref[0, pl.ds(h, 1), cols] -= jnp.sum(ds, axis=0, keepdims=True)
                dd[2 + h] += jnp.sum(ds, axis=1, keepdims=True)
                dsb = ds.astype(BF16)
                dq[...] += _dot(dsb, k)
                dk_ref[cols, :] += _dot(dsb, qh[...], _TN)
                dv_ref[cols, :] += _dot(pr.astype(BF16), doh[...], _TN)

            for h, (qh, doh, dq) in enumerate(((qa, doa, dq_a), (qb, dob, dq_b))):
                pl.when(j >= js_ref[N_PAIRS + 2 * p + h, i])(functools.partial(head, h, qh, doh, dq))

        _walk_up(fetch, j0, i, tile)
        dq_ref[...] = jnp.where(is_a, dq_a[...], dq_b[...]) * SCALE
        lane8 = lax.broadcasted_iota(jnp.int32, (bq, 8), 1)
        dcq_ref[0] = jnp.where(lane8 == 0, dd[2], jnp.where(lane8 == 1, dd[3], 0.0))

    grid_spec = pltpu.PrefetchScalarGridSpec(
        num_scalar_prefetch=1, grid=(N_PAIRS, nq),
        in_specs=[pl.BlockSpec((bq, LANES), lambda p, i, js: (i, col0 + p)),
                  pl.BlockSpec(memory_space=pl.ANY),
                  pl.BlockSpec((bq, LANES), lambda p, i, js: (i, p)),
                  pl.BlockSpec((bq, LANES), lambda p, i, js: (i, p)),
                  pl.BlockSpec((1, bq, 8), lambda p, i, js: (p, i, 0)),
                  pl.BlockSpec((1, bq, 8), lambda p, i, js: (p, i, 0)),
                  pl.BlockSpec((1, 8, S), lambda p, i, js: (p, 0, 0))],
        out_specs=[pl.BlockSpec((bq, LANES), lambda p, i, js: (i, p)),
                   pl.BlockSpec((S, LANES), lambda p, i, js: (0, p)),
                   pl.BlockSpec((S, LANES), lambda p, i, js: (0, p)),
                   pl.BlockSpec((1, 8, S), lambda p, i, js: (p, 0, 0)),
                   pl.BlockSpec((1, bq, 8), lambda p, i, js: (p, i, 0))],
        scratch_shapes=[pltpu.VMEM((bq, LANES), F32), pltpu.VMEM((bq, LANES), F32)]
        + [pltpu.VMEM((bq, LANES), BF16)] * 4 + [pltpu.VMEM((6, bq, 1), F32)]
        + [pltpu.VMEM((2, bq, LANES), BF16), pltpu.VMEM((2, bq, LANES), BF16), pltpu.SemaphoreType.DMA((2, 2))])
    return pl.pallas_call(
        body, name="fox_bwd", grid_spec=grid_spec,
        out_shape=[jax.ShapeDtypeStruct((S, GROUP_W), F32)] * 3
        + [jax.ShapeDtypeStruct((N_PAIRS, 8, S), F32), jax.ShapeDtypeStruct((N_PAIRS, S, 8), F32)],
        compiler_params=_params(VMEM_BIG),
    )(jstart, proj, proj, do, o, st, c_col, c_row)


_HBM = pl.BlockSpec(memory_space=pltpu.HBM)


def _coords():
    return lax.axis_index("x"), lax.axis_index("y"), lax.axis_index("c")


def _gather_copies(ins, outs, send_sems, recv_sems, loc_sems):
    n = len(ins)
    x, y, c = _coords()
    mine = 2 * x + y
    chips = [(1 - x, y), (x, 1 - y), (1 - x, 1 - y)]

    def copy(w, r, slab, to):
        return pltpu.make_async_remote_copy(
            src_ref=ins[w], dst_ref=outs[w].at[slab], send_sem=send_sems.at[3 * w + r],
            recv_sem=recv_sems.at[3 * w + r], device_id=to, device_id_type=MESH)

    def own():
        local = [pltpu.make_async_copy(ins[w], outs[w].at[mine], loc_sems.at[w]) for w in range(n)]
        return local, [copy(w, r, mine, (cx, cy, c)) for w in range(n) for r, (cx, cy) in enumerate(chips)]

    def start():
        local, sends = own()
        for cp in local + sends:
            cp.start()

    def wait():
        local, sends = own()
        for w in range(n):
            for r, (cx, cy) in enumerate(chips):
                copy(w, r, 2 * cx + cy, (cx, cy, c)).wait_recv()
        for cp in sends:
            cp.wait_send()
        for cp in local:
            cp.wait()

    return start, wait


def _gather_shapes(shards):
    n = len(shards)
    return ([jax.ShapeDtypeStruct((4,) + s.shape, s.dtype) for s in shards],
            [pltpu.SemaphoreType.DMA((3 * n,)), pltpu.SemaphoreType.DMA((3 * n,)), pltpu.SemaphoreType.DMA((n,))])


def _allgather_chips(shards):
    n = len(shards)

    def body(*refs):
        start, wait = _gather_copies(refs[:n], refs[n:2 * n], *refs[2 * n:])
        start()
        wait()

    out_shape, sems = _gather_shapes(shards)
    return pl.pallas_call(body, name="allgather_weights", in_specs=[_HBM] * n, out_specs=[_HBM] * n,
                          out_shape=out_shape, scratch_shapes=sems)(*shards)


def _proj_gather(x, w, shards, tm, tn):
    (M, K), N, n = x.shape, w.shape[1], len(shards)
    tm = min(tm, M)
    gi, gj = M // tm, N // tn

    def body(a_ref, b_ref, *rest):
        o_ref = rest[n]
        start, wait = _gather_copies(rest[:n], rest[n + 1:2 * n + 1], *rest[2 * n + 1:])
        i, j = pl.program_id(0), pl.program_id(1)
        pl.when(jnp.logical_and(i == 0, j == 0))(start)
        o_ref[...] = _dot(a_ref[...].astype(BF16), b_ref[...]).astype(o_ref.dtype)
        pl.when(jnp.logical_and(i == gi - 1, j == gj - 1))(wait)

    out_shape, sems = _gather_shapes(shards)
    return pl.pallas_call(
        body, name="proj_gather", grid=(gi, gj),
        in_specs=[pl.BlockSpec((tm, K), lambda i, j: (i, 0)), pl.BlockSpec((K, tn), lambda i, j: (0, j))] + [_HBM] * n,
        out_specs=[pl.BlockSpec((tm, tn), lambda i, j: (i, j))] + [_HBM] * n,
        out_shape=[jax.ShapeDtypeStruct((M, N), BF16)] + out_shape, scratch_shapes=sems,
    )(x, w, *shards)


def _exchange(parts, per_chip):
    n = len(parts)
    half = [p.shape[1] // 2 for p in parts] if per_chip else None

    def body(*refs):
        ins, outs = refs[:n], refs[n:2 * n]
        send_sems, recv_sems, loc_sems = refs[2 * n:]
        x, y, c = _coords()
        me = 4 * x + 2 * y + c
        peers = [(x ^ fx, y ^ fy, c ^ fc) for fx in (0, 1) for fy in (0, 1) for fc in (0, 1)][1:]

        def src(w, dev):
            if not per_chip:
                return ins[w]
            return ins[w].at[2 * dev[0] + dev[1], pl.ds(pl.multiple_of(dev[2] * half[w], 16), half[w]), :]

        local = [pltpu.make_async_copy(src(w, (x, y, c)), outs[w].at[me], loc_sems.at[w]) for w in range(n)]
        for cp in local:
            cp.start()

        def copy(w, r, source, slab, to):
            return pltpu.make_async_remote_copy(
                src_ref=source, dst_ref=outs[w].at[slab], send_sem=send_sems.at[7 * w + r],
                recv_sem=recv_sems.at[7 * w + r], device_id=to, device_id_type=MESH)

        sends = [copy(w, r, src(w, dev), me, dev) for w in range(n) for r, dev in enumerate(peers)]
        for cp in sends:
            cp.start()
        for w in range(n):
            for r, dev in enumerate(peers):
                copy(w, r, src(w, dev), 4 * dev[0] + 2 * dev[1] + dev[2], dev).wait_recv()
        for cp in sends:
            cp.wait_send()
        for cp in local:
            cp.wait()

    return pl.pallas_call(
        body, name="exchange_per_chip" if per_chip else "exchange_all",
        in_specs=[_HBM] * n, out_specs=[_HBM] * n,
        out_shape=[jax.ShapeDtypeStruct((8, half[w], p.shape[2]) if per_chip else (8,) + p.shape, p.dtype)
                   for w, p in enumerate(parts)],
        scratch_shapes=[pltpu.SemaphoreType.DMA((7 * n,)), pltpu.SemaphoreType.DMA((7 * n,)),
                        pltpu.SemaphoreType.DMA((n,))],
    )(*parts)


def _sibling_swap(halves):
    n = len(halves)

    def body(*refs):
        ins, outs = refs[:n], refs[n:2 * n]
        send_sems, recv_sems, loc_sems = refs[2 * n:]
        x, y, c = _coords()

        def rows(w, core):
            rh = halves[w].shape[0]
            return outs[w].at[pl.ds(pl.multiple_of(core * rh, 8), rh), :]

        def copy(w, core):
            return pltpu.make_async_remote_copy(
                src_ref=ins[w], dst_ref=rows(w, core), send_sem=send_sems.at[w], recv_sem=recv_sems.at[w],
                device_id=(x, y, 1 - c), device_id_type=MESH)

        local = [pltpu.make_async_copy(ins[w], rows(w, c), loc_sems.at[w]) for w in range(n)]
        sends = [copy(w, c) for w in range(n)]
        for cp in local + sends:
            cp.start()
        for w in range(n):
            copy(w, 1 - c).wait_recv()
        for cp in sends:
            cp.wait_send()
        for cp in local:
            cp.wait()

    vmem = pl.BlockSpec(memory_space=pltpu.VMEM)
    return pl.pallas_call(
        body, name="sibling_swap", in_specs=[vmem] * n, out_specs=[vmem] * n,
        out_shape=[jax.ShapeDtypeStruct((2 * h.shape[0], h.shape[1]), h.dtype) for h in halves],
        scratch_shapes=[pltpu.SemaphoreType.DMA((n,)), pltpu.SemaphoreType.DMA((n,)), pltpu.SemaphoreType.DMA((n,))],
    )(*halves)


def _adamw(w, g, m, v):
    m = ADAM_B1 * m + (1.0 - ADAM_B1) * g
    v = ADAM_B2 * v + (1.0 - ADAM_B2) * (g * g)
    m_hat = m / (1.0 - ADAM_B1 ** ADAM_STEP)
    v_hat = v / (1.0 - ADAM_B2 ** ADAM_STEP)
    delta = -ADAM_LR * (m_hat / (jnp.sqrt(v_hat) + ADAM_EPS) + ADAM_WD * w)
    return delta, m, v


def _sum_parts(parts, name, tr):
    _, R, C = parts.shape
    assert R % tr == 0

    def body(p_ref, g_ref):
        g = p_ref[0].astype(F32)
        for d in range(1, 8):
            g = g + p_ref[d].astype(F32)
        g_ref[...] = g

    return pl.pallas_call(
        body, name=name, grid=(R // tr,),
        in_specs=[pl.BlockSpec((8, tr, C), lambda i: (0, i, 0))],
        out_specs=pl.BlockSpec((tr, C), lambda i: (i, 0)), out_shape=jax.ShapeDtypeStruct((R, C), F32),
    )(parts)


def _adamw_call(g, w, m, v, name, tr):
    R, C = w.shape
    assert R % tr == 0

    def body(g_ref, w_ref, m_ref, v_ref, d_ref, nm_ref, nv_ref):
        d_ref[...], nm_ref[...], nv_ref[...] = _adamw(w_ref[...], g_ref[...], m_ref[...], v_ref[...])

    tile = pl.BlockSpec((tr, C), lambda i: (i, 0))
    return pl.pallas_call(
        body, name=name, grid=(R // tr,), in_specs=[tile] * 4,
        out_specs=[tile] * 3, out_shape=[jax.ShapeDtypeStruct((R, C), F32)] * 3,
    )(g, w, m, v)


def _sum_adamw_small(parts, w, m, v):
    def body(p_ref, w_ref, m_ref, v_ref, g_ref, d_ref, nm_ref, nv_ref, loss_ref):
        g = p_ref[0]
        for d in range(1, 8):
            g = g + p_ref[d]
        g_ref[...] = g
        d_ref[...], nm_ref[...], nv_ref[...] = _adamw(w_ref[...], g, m_ref[...], v_ref[...])
        row = lax.broadcasted_iota(jnp.int32, g.shape, 0)
        per_row = jnp.sum(jnp.where(row == 6, g, 0.0), axis=1, keepdims=True)
        loss_ref[...] = jnp.zeros((8, LANES), F32) + jnp.sum(per_row, axis=0, keepdims=True)

    return pl.pallas_call(
        body, name="sum_adamw_small",
        out_shape=[jax.ShapeDtypeStruct((8, D_MODEL), F32)] * 4 + [jax.ShapeDtypeStruct((8, LANES), F32)],
    )(parts, w, m, v)


def _pack_small(ln1_g, ln1_b, ln2_g, ln2_b, g_sb, g_fox, b_f):
    row5 = jnp.pad(b_f.reshape(1, N_FOX), ((0, 0), (0, D_MODEL - N_FOX)))
    rows = [ln1_g.reshape(1, -1), ln1_b.reshape(1, -1), ln2_g.reshape(1, -1), ln2_b.reshape(1, -1),
            jnp.concatenate([g_sb.reshape(1, -1), g_fox.reshape(1, -1)], axis=1), row5,
            jnp.zeros((2, D_MODEL), F32)]
    return jnp.concatenate(rows, axis=0)


def _unpack_small(p):
    return {"ln1_g": p[0:1], "ln1_b": p[1:2], "ln2_g": p[2:3], "ln2_b": p[3:4], "g_sb": p[4:5, :GROUP_W],
            "g_fox": p[4:5, GROUP_W:], "b_f": p[5:6, :N_FOX]}


def kernel(x, w_in, b_f, g_sb, g_fox, w_out, ln1_g, ln1_b, ln2_g, ln2_b, w_gate_up, w_down, loss_target, m_w_in, m_b_f, m_g_sb, m_g_fox, m_w_out, m_ln1_g, m_ln1_b, m_ln2_g, m_ln2_b, m_w_gate_up, m_w_down, v_w_in, v_b_f, v_g_sb, v_g_fox, v_w_out, v_ln1_g, v_ln1_b, v_ln2_g, v_ln2_b, v_w_gate_up, v_w_down):
    S = x.shape[1]
    x2 = x.reshape(S, D_MODEL)
    tgt = loss_target.reshape(S, D_MODEL)
    TM = 1024
    TR = 512
    BQ = ATTN_BLOCK
    in_w = w_in.shape[2]
    gu_w = w_gate_up.shape[2]

    shards = [w_in[0].astype(BF16), w_out[0].astype(BF16), w_gate_up[0].astype(BF16), w_down[0].astype(BF16)]
    (wi_s,) = _allgather_chips(shards[:1])
    wi = wi_s.transpose(1, 0, 2).reshape(D_MODEL, 4 * in_w)
    w_sb, w_fx = wi[:, :QKV_W // 2], wi[:, QKV_W // 2:QKV_W]
    wqkv = wi[:, :QKV_W]
    wft = wi[:, QKV_W:].T
    proj, wo_s, wgu_s, wd_s = _proj_gather(x2, wqkv, shards[1:], TM, 512)
    wo = wo_s.reshape(D_MODEL, D_MODEL)
    wgu = wgu_s.transpose(1, 0, 2).reshape(D_MODEL, 2 * D_FF)
    wg, wu = wgu[:, :D_FF], wgu[:, D_FF:]
    wd = wd_s.reshape(D_FF, D_MODEL)
    g_row = jnp.concatenate([g_sb, g_fox], axis=1)
    hid = np.arange(D_MODEL) // HEAD_DIM
    he_np = (hid[:, None] == np.arange(LANES)[None, :]).astype(np.float32)
    he, het = jnp.asarray(he_np, BF16), jnp.asarray(he_np.T, BF16)

    lf = _fgate_fwd(x2, wft, b_f.reshape(N_FOX, 1), TM)
    c = _cumsum_fwd(lf)
    c_pair = c.reshape(N_PAIRS, 2, S)
    c_row = jnp.pad(c_pair, ((0, 0), (0, 6), (0, 0)))
    c_col = jnp.pad(c_pair.transpose(0, 2, 1), ((0, 0), (0, 0), (0, 6)))

    o_sb, st_sb, jmin_sb = _sb_fwd(proj, 0, BQ)
    jstart_fx = _fox_start_blocks(proj, 12, c, min(BQ, S))
    o_fx, st_fx = _fox_fwd(proj, 12, c_col, c_row, jstart_fx, BQ)

    def attn_post(i, osb_ref, ofx_ref, g_ref, he_ref, het_ref, on_ref):
        o = jnp.concatenate([osb_ref[...], ofx_ref[...]], axis=1)
        ms = _head_sums(o * o, he_ref[...], het_ref[...]) * (1.0 / HEAD_DIM)
        on_ref[...] = (o * lax.rsqrt(ms + RMS_EPS) * g_ref[...]).astype(BF16)

    (on,) = _rowwise(attn_post, "attn_post", S, TR,
                     [(o_sb, "t"), (o_fx, "t"), (g_row, "f"), (he, "f"), (het, "f")],
                     [((S, D_MODEL), BF16, "t")])

    u1 = _matmul(on, wo, mode="nn", name="mix", tm=TM, tn=D_MODEL, tk=D_MODEL, outs=[F32],
                 extras=[(x2, (TM if S >= TM else S, D_MODEL), _tile_ij)],
                 epilogue=lambda acc, xv: (ALPHA * xv + acc,))

    def ln1_fwd(i, u_ref, g_ref, b_ref, h_ref):
        xh, _ = _ln_stats(u_ref[...])
        h_ref[...] = xh * g_ref[...] + b_ref[...]

    (h1,) = _rowwise(ln1_fwd, "ln1_fwd", S, TR, [(u1, "t"), (ln1_g, "f"), (ln1_b, "f")], [((S, D_MODEL), F32, "t")])

    tm_e = TM if S >= TM else S
    n_ff = D_FF // 256

    def gate_up_body(h_ref, wg_ref, wu_ref, g_ref, u_ref, a_ref):
        h = h_ref[...].astype(BF16)
        g, u = _dot(h, wg_ref[...]), _dot(h, wu_ref[...])
        g_ref[...] = g.astype(BF16)
        u_ref[...] = u.astype(BF16)
        a_ref[...] = (g / (1.0 + jnp.exp(-g)) * u).astype(BF16)

    ff_tile = pl.BlockSpec((tm_e, 256), lambda i, j: (i, j))
    gate, up, act = pl.pallas_call(
        gate_up_body, name="gate_up_act", grid=(S // tm_e, n_ff),
        in_specs=[pl.BlockSpec((tm_e, D_MODEL), lambda i, j: (i, 0)),
                  pl.BlockSpec((D_MODEL, 256), lambda i, j: (0, j)),
                  pl.BlockSpec((D_MODEL, 256), lambda i, j: (0, j + n_ff))],
        out_specs=[ff_tile] * 3, out_shape=[jax.ShapeDtypeStruct((S, D_FF), BF16)] * 3)(h1, wgu, wgu)

    u2 = _matmul(act, wd, mode="nn", name="ffn_down", tm=TM, tn=D_MODEL, tk=D_FF, outs=[F32],
                 extras=[(h1, (TM if S >= TM else S, D_MODEL), _tile_ij)],
                 epilogue=lambda acc, hv: (ALPHA * hv + acc,))

    def ln2_loss(i, u_ref, t_ref, g_ref, b_ref, du_ref, acc_ref):
        xh, r = _ln_stats(u_ref[...])
        g = g_ref[...]
        err = xh * g + b_ref[...] - t_ref[...]
        dy = err * (1.0 / D_MODEL)
        du_ref[...] = _ln_bwd(dy, xh, r, g)
        _acc_rows(i, acc_ref, {2: jnp.sum(dy * xh, axis=0, keepdims=True), 3: jnp.sum(dy, axis=0, keepdims=True),
                               6: jnp.sum(err * err, axis=0, keepdims=True) * (0.5 / D_MODEL)})

    du2, acc_ln2 = _rowwise(ln2_loss, "ln2_loss", S, TR, [(u2, "t"), (tgt, "t"), (ln2_g, "f"), (ln2_b, "f")],
                            [((S, D_MODEL), F32, "t"), ((8, D_MODEL), F32, "f")])

    d_wd = _matmul(act, du2, mode="tn", name="dw_down", tm=1408, tn=D_MODEL, tk=TM, outs=[F32])

    def dgu_epilogue(da, g, u):
        g, u = g.astype(F32), u.astype(F32)
        s = 1.0 / (1.0 + jnp.exp(-g))
        return da * u * (s * (1.0 + g * (1.0 - s))), da * (g * s)

    dgate, dup = _matmul(du2, wd, mode="nt", name="d_act", tm=TM, tn=1408, tk=D_MODEL, outs=[BF16, BF16],
                         extras=[(gate, (tm_e, 1408), _tile_ij), (up, (tm_e, 1408), _tile_ij)],
                         epilogue=dgu_epilogue)
    d_wg = _matmul(h1, dgate, mode="tn", name="dw_gate", tm=D_MODEL, tn=1408, tk=TM, outs=[F32])
    d_wu = _matmul(h1, dup, mode="tn", name="dw_up", tm=D_MODEL, tn=1408, tk=TM, outs=[F32])
    dh1 = _matmul(dgate, wg, mode="nt", name="dh1_gate", tm=TM, tn=D_MODEL, tk=D_FF, outs=[F32],
                  extras=[(du2, (tm_e, D_MODEL), _tile_ij)], epilogue=lambda acc, e: (ALPHA * e + acc,))
    dh1 = _matmul(dup, wu, mode="nt", name="dh1_up", tm=TM, tn=D_MODEL, tk=D_FF, outs=[F32],
                  extras=[(dh1, (tm_e, D_MODEL), _tile_ij)], epilogue=lambda acc, e: (e + acc,))

    def ln1_bwd(i, dh_ref, u_ref, g_ref, du_ref, acc_ref):
        xh, r = _ln_stats(u_ref[...])
        dh = dh_ref[...]
        du_ref[...] = _ln_bwd(dh, xh, r, g_ref[...])
        _acc_rows(i, acc_ref, {0: jnp.sum(dh * xh, axis=0, keepdims=True), 1: jnp.sum(dh, axis=0, keepdims=True)})

    du1, acc_ln1 = _rowwise(ln1_bwd, "ln1_bwd", S, TR, [(dh1, "t"), (u1, "t"), (ln1_g, "f")],
                            [((S, D_MODEL), F32, "t"), ((8, D_MODEL), F32, "f")])
    d_wo = _matmul(on, du1, mode="tn", name="dw_out", tm=D_MODEL, tn=D_MODEL, tk=TM, outs=[F32])
    don = _matmul(du1, wo, mode="nt", name="d_on", tm=TM, tn=D_MODEL, tk=D_MODEL, outs=[F32])

    def rms_bwd(i, don_ref, osb_ref, ofx_ref, g_ref, he_ref, het_ref, dosb_ref, dofx_ref, acc_ref):
        o = jnp.concatenate([osb_ref[...], ofx_ref[...]], axis=1)
        hev, hetv = he_ref[...], het_ref[...]
        r = lax.rsqrt(_head_sums(o * o, hev, hetv) * (1.0 / HEAD_DIM) + RMS_EPS)
        dn = don_ref[...]
        dg = dn * g_ref[...]
        do = r * dg - o * (r * r * r) * (_head_sums(dg * o, hev, hetv) * (1.0 / HEAD_DIM))
        dosb_ref[...] = do[:, :GROUP_W]
        dofx_ref[...] = do[:, GROUP_W:]
        _acc_rows(i, acc_ref, {4: jnp.sum(dn * o * r, axis=0, keepdims=True)})

    do_sb, do_fx, acc_rms = _rowwise(
        rms_bwd, "rms_bwd", S, TR, [(don, "t"), (o_sb, "t"), (o_fx, "t"), (g_row, "f"), (he, "f"), (het, "f")],
        [((S, GROUP_W), F32, "t"), ((S, GROUP_W), F32, "t"), ((8, D_MODEL), F32, "f")])

    dq_sb, dk_sb, dv_sb = _sb_bwd(proj, 0, do_sb, st_sb, jmin_sb, BQ)
    dq_fx, dk_fx, dv_fx, dc, dcq = _fox_bwd(proj, 12, do_fx, o_fx, st_fx, c_col, c_row, jstart_fx, BQ)
    dc = dc[:, :2, :] + dcq[:, :, :2].transpose(0, 2, 1)
    dfl, dbf = _fgate_bwd(dc.reshape(N_FOX, S), lf)
    dp_sb = jnp.concatenate([dq_sb, dk_sb, dv_sb], axis=1).astype(BF16)
    dp_fx = jnp.concatenate([dq_fx, dk_fx, dv_fx], axis=1).astype(BF16)

    d_wsb = _matmul(x2, dp_sb, mode="tn", name="dw_in_sb", tm=D_MODEL, tn=QKV_W // 2, tk=TM, outs=[F32])
    d_wfx = _matmul(x2, dp_fx, mode="tn", name="dw_in_fx", tm=D_MODEL, tn=QKV_W // 2, tk=TM, outs=[F32])
    d_wft = _matmul(dfl, x2, mode="nn", name="dw_in_f", tm=N_FOX, tn=D_MODEL, tk=TM, outs=[F32])
    dx = _matmul(dp_sb, w_sb, mode="nt", name="dx_sb", tm=TM, tn=D_MODEL, tk=QKV_W // 2, outs=[F32],
                 extras=[(du1, (tm_e, D_MODEL), _tile_ij)], epilogue=lambda acc, e: (ALPHA * e + acc,))
    dx = _matmul(dp_fx, w_fx, mode="nt", name="dx_fx", tm=TM, tn=D_MODEL, tk=QKV_W // 2, outs=[F32],
                 extras=[(dx, (tm_e, D_MODEL), _tile_ij)], epilogue=lambda acc, e: (e + acc,))
    dx = _matmul(dfl, wft, mode="tn", name="dx_f", tm=TM, tn=D_MODEL, tk=N_FOX, outs=[F32],
                 extras=[(dx, (tm_e, D_MODEL), _tile_ij)], epilogue=lambda acc, e: (e + acc,))

    d_wi = jnp.concatenate([d_wsb, d_wfx, d_wft.T], axis=1)
    d_wgu = jnp.concatenate([d_wg, d_wu], axis=1)
    parts = [d_wi.reshape(D_MODEL, 4, in_w).transpose(1, 0, 2).astype(BF16),
             d_wo.reshape(4, D_MODEL // 4, D_MODEL).astype(BF16),
             d_wgu.reshape(D_MODEL, 4, gu_w).transpose(1, 0, 2).astype(BF16),
             d_wd.reshape(4, D_FF // 4, D_MODEL).astype(BF16)]
    got = _exchange(parts, True)
    big_names = ("w_in", "w_out", "w_gate_up", "w_down")
    halves = [_sum_parts(p, "sum_" + nm, tr) for nm, p, tr in zip(big_names, got, (256, 128, 128, 176))]
    grads = _sibling_swap(halves)
    big = {}
    for nm, g, w, m, v, tr in zip(big_names, grads, (w_in, w_out, w_gate_up, w_down),
                                  (m_w_in, m_w_out, m_w_gate_up, m_w_down),
                                  (v_w_in, v_w_out, v_w_gate_up, v_w_down), (256, 256, 256, 176)):
        big[nm] = [r[None] for r in [g] + list(_adamw_call(g, w[0], m[0], v[0], "adamw_" + nm, tr))]

    small = acc_ln2 + acc_ln1 + acc_rms
    small = small + jnp.pad(dbf.reshape(1, N_FOX), ((5, 2), (0, D_MODEL - N_FOX)))
    (small_all,) = _exchange([small], False)
    sw = _pack_small(ln1_g, ln1_b, ln2_g, ln2_b, g_sb, g_fox, b_f)
    sm = _pack_small(m_ln1_g, m_ln1_b, m_ln2_g, m_ln2_b, m_g_sb, m_g_fox, m_b_f)
    sv = _pack_small(v_ln1_g, v_ln1_b, v_ln2_g, v_ln2_b, v_g_sb, v_g_fox, v_b_f)
    sg, sd, snm, snv, loss_blk = _sum_adamw_small(small_all, sw, sm, sv)
    sg, sd, snm, snv = _unpack_small(sg), _unpack_small(sd), _unpack_small(snm), _unpack_small(snv)

    names = ["w_in", "b_f", "g_sb", "g_fox", "w_out", "ln1_g", "ln1_b", "ln2_g", "ln2_b", "w_gate_up", "w_down"]
    outs = [loss_blk[0, 0], dx.reshape(1, S, D_MODEL)]
    for k, table in enumerate((sg, sd, snm, snv)):
        outs += [big[n][k] if n in big else table[n] for n in names]
    return tuple(outs)
```

```python
import functools

import numpy as np
import jax
import jax.numpy as jnp
from jax import lax
from jax.experimental import pallas as pl
from jax.experimental.pallas import tpu as pltpu

F32 = jnp.float32
BF16 = jnp.bfloat16

D_MODEL = 1024
HEAD_DIM = 64
LANES = 128
N_PAIRS = 4
GROUP_W = 512
QKV_W = 3072
D_FF = 2816
N_FOX = 8
ALPHA = 2.0 ** 0.25
LN_EPS = 1e-5
RMS_EPS = 1e-6
SCALE = HEAD_DIM ** -0.5
NEG_BIG = -1e30
FOX_SKIP = 30.0
SB_STOP = -105.0
ADAM_LR, ADAM_B1, ADAM_B2, ADAM_EPS, ADAM_WD, ADAM_STEP = 0.001, 0.9, 0.999, 1e-08, 0.01, 10
SCAN_GROUP = 8
ATTN_BLOCK = 256
VMEM_BIG = 56 * 1024 * 1024
MESH = pl.DeviceIdType.MESH

_NN = (((1,), (0,)), ((), ()))
_NT = (((1,), (1,)), ((), ()))
_TN = (((0,), (0,)), ((), ()))


def _dot(a, b, dims=_NN):
    return lax.dot_general(a, b, dims, preferred_element_type=F32)


def _split_dot(x, t):
    hi = x.astype(BF16)
    lo = (x - hi.astype(F32)).astype(BF16)
    return _dot(hi, t) + _dot(lo, t)


def _softplus(z):
    return jnp.maximum(z, 0.0) + jnp.log1p(jnp.exp(-jnp.abs(z)))


def _col(v, h):
    lane = lax.broadcasted_iota(jnp.int32, v.shape, 1)
    return jnp.sum(jnp.where(lane == h, v, 0.0), axis=1, keepdims=True)


def _two_sum(hi, lo, b):
    s = hi + b
    bb = s - hi
    err = (hi - (s - bb)) + (b - bb)
    return s, lo + err


def _params(vmem=None):
    return pltpu.CompilerParams(vmem_limit_bytes=vmem) if vmem else None


def _matmul(a, b, *, mode, name, tm, tn, tk, outs, extras=(), epilogue=None, vmem=None):
    if mode == "nn":
        (M, K), (_, N) = a.shape, b.shape
    elif mode == "nt":
        (M, K), (N, _) = a.shape, b.shape
    else:
        (K, M), (_, N) = a.shape, b.shape
    tm, tn, tk = min(tm, M), min(tn, N), min(tk, K)
    assert M % tm == 0 and N % tn == 0 and K % tk == 0, (name, M, N, K, tm, tn, tk)
    nk = K // tk
    dims = {"nn": _NN, "nt": _NT, "tn": _TN}[mode]
    if mode == "tn":
        a_spec = pl.BlockSpec((tk, tm), lambda i, j, k: (k, i))
    else:
        a_spec = pl.BlockSpec((tm, tk), lambda i, j, k: (i, k))
    if mode == "nt":
        b_spec = pl.BlockSpec((tn, tk), lambda i, j, k: (j, k))
    else:
        b_spec = pl.BlockSpec((tk, tn), lambda i, j, k: (k, j))
    ex_specs = [pl.BlockSpec(bs, (lambda i, j, k, f=f: f(i, j))) for (_, bs, f) in extras]
    ne, no = len(extras), len(outs)
    if epilogue is None:
        epilogue = lambda acc: (acc,)

    def body(a_ref, b_ref, *rest):
        ex_refs, out_refs, acc = rest[:ne], rest[ne:ne + no], rest[-1]
        k = pl.program_id(2)

        @pl.when(k == 0)
        def _():
            acc[...] = jnp.zeros_like(acc)

        acc[...] += _dot(a_ref[...].astype(BF16), b_ref[...].astype(BF16), dims)

        @pl.when(k == nk - 1)
        def _():
            res = epilogue(acc[...], *[e[...] for e in ex_refs])
            for r, o in zip(res, out_refs):
                o[...] = r.astype(o.dtype)

    res = pl.pallas_call(
        body, name=name, grid=(M // tm, N // tn, nk),
        in_specs=[a_spec, b_spec] + ex_specs,
        out_specs=[pl.BlockSpec((tm, tn), lambda i, j, k: (i, j)) for _ in outs],
        out_shape=[jax.ShapeDtypeStruct((M, N), d) for d in outs],
        scratch_shapes=[pltpu.VMEM((tm, tn), F32)],
        compiler_params=_params(vmem),
    )(a, b, *[e[0] for e in extras])
    return res[0] if no == 1 else res


def _tile_ij(i, j):
    return (i, j)


def _rowwise(fn, name, rows, tm, ins, outs, vmem=None):
    tm = min(tm, rows)
    assert rows % tm == 0

    def spec(shape, kind):
        if kind == "t":
            return pl.BlockSpec((tm,) + tuple(shape[1:]), lambda i: (i,) + (0,) * (len(shape) - 1))
        return pl.BlockSpec(tuple(shape), lambda i: (0,) * len(shape))

    def body(*refs):
        fn(pl.program_id(0), *refs)

    return pl.pallas_call(
        body, name=name, grid=(rows // tm,),
        in_specs=[spec(a.shape, k) for a, k in ins],
        out_specs=[spec(s, k) for s, _, k in outs],
        out_shape=[jax.ShapeDtypeStruct(s, d) for s, d, _ in outs],
        compiler_params=_params(vmem),
    )(*[a for a, _ in ins])


def _ln_stats(u):
    mu = jnp.mean(u, axis=-1, keepdims=True)
    d = u - mu
    var = jnp.mean(d * d, axis=-1, keepdims=True)
    r = lax.rsqrt(var + LN_EPS)
    return d * r, r


def _ln_bwd(dh, xh, r, g):
    dxh = dh * g
    m1 = jnp.mean(dxh, axis=-1, keepdims=True)
    m2 = jnp.mean(dxh * xh, axis=-1, keepdims=True)
    return r * (dxh - m1 - xh * m2)


def _acc_rows(i, ref, rows):
    @pl.when(i == 0)
    def _():
        ref[...] = jnp.zeros_like(ref)
    for r, v in rows.items():
        ref[pl.ds(r, 1), :] += v


def _head_sums(v, he, het):
    return _split_dot(_split_dot(v, he), het)


def _fgate_fwd(x, wft, bf_col, tm):
    S = x.shape[0]
    tm = min(tm, S)

    def body(wft_ref, bf_ref, x_ref, lf_ref):
        f = _dot(wft_ref[...], x_ref[...].astype(BF16), _NT) + bf_ref[...]
        lf_ref[...] = -_softplus(-f)

    return pl.pallas_call(
        body, name="fgate_fwd", grid=(S // tm,),
        in_specs=[pl.BlockSpec((N_FOX, D_MODEL), lambda i: (0, 0)), pl.BlockSpec((N_FOX, 1), lambda i: (0, 0)),
                  pl.BlockSpec((tm, D_MODEL), lambda i: (i, 0))],
        out_specs=pl.BlockSpec((N_FOX, tm), lambda i: (0, i)),
        out_shape=jax.ShapeDtypeStruct((N_FOX, S), F32),
    )(wft, bf_col, x)


def _chunk_scan(v, reverse):
    lane = lax.broadcasted_iota(jnp.int32, v.shape, 1)
    sh = 1
    while sh < LANES:
        if reverse:
            v = v + jnp.where(lane < LANES - sh, pltpu.roll(v, LANES - sh, 1), 0.0)
        else:
            v = v + jnp.where(lane >= sh, pltpu.roll(v, sh, 1), 0.0)
        sh *= 2
    return v


def _cumsum_fwd(lf):
    n, S = lf.shape
    nc = S // LANES

    grp = min(SCAN_GROUP, nc)

    def body(lf_ref, c_ref):
        def step(gi, carry):
            sls = [pl.ds(pl.multiple_of((gi * grp + g) * LANES, LANES), LANES) for g in range(grp)]
            vs = [_chunk_scan(lf_ref[:, sl], False) for sl in sls]
            tots = [_col(v, LANES - 1) for v in vs]
            for sl, v, t in zip(sls, vs, tots):
                c_ref[:, sl] = v + carry
                carry = carry + t
            return carry
        lax.fori_loop(0, nc // grp, step, jnp.zeros((n, 1), F32))

    return pl.pallas_call(body, name="cumsum_fwd", out_shape=jax.ShapeDtypeStruct((n, S), F32))(lf)


def _fgate_bwd(dc, lf):
    n, S = dc.shape
    nc = S // LANES

    grp = min(SCAN_GROUP, nc)

    def body(dc_ref, lf_ref, dfl_ref, dbf_ref):
        def step(t, carry):
            car, tot = carry
            gi = nc // grp - 1 - t
            sls = [pl.ds(pl.multiple_of((gi * grp + g) * LANES, LANES), LANES) for g in range(grp)]
            vs = [_chunk_scan(dc_ref[:, sl], True) for sl in sls]
            firsts = [_col(v, 0) for v in vs]
            for sl, v, f in reversed(list(zip(sls, vs, firsts))):
                dfl = (v + car) * (1.0 - jnp.exp(lf_ref[:, sl]))
                dfl_ref[:, sl] = dfl
                tot = tot + jnp.sum(dfl, axis=1, keepdims=True)
                car = car + f
            return car, tot
        _, tot = lax.fori_loop(0, nc // grp, step, (jnp.zeros((n, 1), F32), jnp.zeros((n, 1), F32)))
        dbf_ref[...] = tot

    return pl.pallas_call(body, name="fgate_bwd",
                          out_shape=[jax.ShapeDtypeStruct((n, S), F32), jax.ShapeDtypeStruct((n, 1), F32)])(dc, lf)


def _tri_matrices(b):
    r = np.arange(b)
    tfwd = (r[:, None] <= r[None, :]).astype(np.float32)
    return jnp.asarray(tfwd, BF16), jnp.asarray(tfwd.T, BF16)


def _kv_copies(kv_hbm, kbuf, vbuf, sems, pair_col, bq, j, slot):
    rows = pl.ds(pl.multiple_of(j * bq, bq), bq)

    def cols(c):
        return pl.ds(pl.multiple_of((pair_col + c) * LANES, LANES), LANES)

    return (pltpu.make_async_copy(kv_hbm.at[rows, cols(4)], kbuf.at[slot], sems.at[0, slot]),
            pltpu.make_async_copy(kv_hbm.at[rows, cols(8)], vbuf.at[slot], sems.at[1, slot]))


def _kv_fetcher(kv_hbm, kbuf, vbuf, sems, col0, bq, p, i, j0):
    def fetch(j, slot, pair=p):
        return _kv_copies(kv_hbm, kbuf, vbuf, sems, col0 + pair, bq, j, slot)

    @pl.when(jnp.logical_and(p == 0, i == 0))
    def _():
        for cp in fetch(j0, 0):
            cp.start()

    return fetch


def _prefetch_next(fetch, p, i, nq, first_block):
    wrap = i == nq - 1

    @pl.when(jnp.logical_not(jnp.logical_and(wrap, p == N_PAIRS - 1)))
    def _():
        pair, blk = jnp.where(wrap, p + 1, p), jnp.where(wrap, 0, i + 1)
        for cp in fetch(first_block(pair, blk), 0, pair):
            cp.start()


def _masked_pair(v, lane_is_a, scale=1.0):
    v = v.astype(F32) * scale
    return jnp.where(lane_is_a, v, 0.0).astype(BF16), jnp.where(lane_is_a, 0.0, v).astype(BF16)


def _sb_fwd(proj, col0, bq):
    S = proj.shape[0]
    bq = min(bq, S)
    nq = S // bq
    _, trev = _tri_matrices(bq)

    def body(q_ref, kv_hbm, trev_ref, o_ref, st_ref, jmin_ref, acc_a, acc_b, qa, qb, rs, kbuf, vbuf, sems):
        p, i = pl.program_id(0), pl.program_id(1)
        fetch = _kv_fetcher(kv_hbm, kbuf, vbuf, sems, col0, bq, p, i, i)
        is_a = lax.broadcasted_iota(jnp.int32, (bq, LANES), 1) < HEAD_DIM
        acc_a[...] = jnp.zeros_like(acc_a)
        acc_b[...] = jnp.zeros_like(acc_b)
        rs[...] = jnp.zeros_like(rs)
        qa[...], qb[...] = _masked_pair(q_ref[...], is_a, SCALE)

        def tile(slot, masked):
            k, v, trev_m = kbuf[slot], vbuf[slot], trev_ref[...]
            if masked:
                tri = lax.broadcasted_iota(jnp.int32, (bq, bq), 0) > lax.broadcasted_iota(jnp.int32, (bq, bq), 1)
            for h, (qh, acc) in enumerate(((qa, acc_a), (qb, acc_b))):
                z = _dot(qh[...], k, _NT)
                lk = -_softplus(z)
                if masked:
                    lk = jnp.where(tri, lk, 0.0)
                r_hi, r_lo = rs[2 * h], rs[2 * h + 1]
                w = jnp.exp(z + _split_dot(lk, trev_m) + (r_hi + r_lo))
                if masked:
                    w = jnp.where(tri, w, 0.0)
                acc[...] += _dot(w.astype(BF16), v)
                rs[2 * h], rs[2 * h + 1] = _two_sum(r_hi, r_lo, jnp.sum(lk, axis=1, keepdims=True))

        def step(carry):
            j, _ = carry
            slot = lax.rem(i - j, 2)
            for cp in fetch(j, slot):
                cp.wait()

            @pl.when(j > 0)
            def _():
                for cp in fetch(j - 1, 1 - slot):
                    cp.start()

            pl.when(j == i)(functools.partial(tile, slot, True))
            pl.when(j < i)(functools.partial(tile, slot, False))
            live = jnp.max(jnp.maximum(rs[0], rs[2])) > SB_STOP
            return j - 1, live.astype(jnp.int32)

        j_end, _ = lax.while_loop(lambda c: jnp.logical_and(c[0] >= 0, c[1] > 0), step, (i, jnp.int32(1)))

        @pl.when(j_end >= 0)
        def _():
            for cp in fetch(j_end, lax.rem(i - j_end, 2)):
                cp.wait()

        _prefetch_next(fetch, p, i, nq, lambda pair, blk: blk)
        jmin_ref[p, i] = j_end + 1
        o_ref[...] = jnp.where(is_a, acc_a[...], acc_b[...])
        lane8 = lax.broadcasted_iota(jnp.int32, (bq, 8), 1)
        st = jnp.zeros((bq, 8), F32)
        for c, src in enumerate((0, 2, 1, 3)):
            st = jnp.where(lane8 == c, rs[src], st)
        st_ref[0] = st

    return pl.pallas_call(
        body, name="sb_fwd", grid=(N_PAIRS, nq),
        in_specs=[pl.BlockSpec((bq, LANES), lambda p, i: (i, col0 + p)),
                  pl.BlockSpec(memory_space=pl.ANY),
                  pl.BlockSpec((bq, bq), lambda p, i: (0, 0))],
        out_specs=[pl.BlockSpec((bq, LANES), lambda p, i: (i, p)),
                   pl.BlockSpec((1, bq, 8), lambda p, i: (p, i, 0)),
                   pl.BlockSpec(memory_space=pltpu.SMEM)],
        out_shape=[jax.ShapeDtypeStruct((S, GROUP_W), F32), jax.ShapeDtypeStruct((N_PAIRS, S, 8), F32),
                   jax.ShapeDtypeStruct((N_PAIRS, nq), jnp.int32)],
        scratch_shapes=[pltpu.VMEM((bq, LANES), F32), pltpu.VMEM((bq, LANES), F32),
                        pltpu.VMEM((bq, LANES), BF16), pltpu.VMEM((bq, LANES), BF16),
                        pltpu.VMEM((4, bq, 1), F32),
                        pltpu.VMEM((2, bq, LANES), BF16), pltpu.VMEM((2, bq, LANES), BF16),
                        pltpu.SemaphoreType.DMA((2, 2))],
    )(proj, proj, trev)


def _sb_bwd(proj, col0, do, st, jmin, bq):
    S = proj.shape[0]
    bq = min(bq, S)
    nq = S // bq
    tfwd, trev = _tri_matrices(bq)

    def body(jmin_ref, q_ref, kv_hbm, do_ref, st_ref, tfwd_ref, trev_ref,
             dq_ref, dk_ref, dv_ref, dq_a, dq_b, qa, qb, doa, dob, rs, kbuf, vbuf, sems):
        p, i = pl.program_id(0), pl.program_id(1)
        j0 = jmin_ref[p, i]
        fetch = _kv_fetcher(kv_hbm, kbuf, vbuf, sems, col0, bq, p, i, j0)
        is_a = lax.broadcasted_iota(jnp.int32, (bq, LANES), 1) < HEAD_DIM

        @pl.when(i == 0)
        def _():
            dk_ref[...] = jnp.zeros_like(dk_ref)
            dv_ref[...] = jnp.zeros_like(dv_ref)

        dq_a[...] = jnp.zeros_like(dq_a)
        dq_b[...] = jnp.zeros_like(dq_b)
        rs[...] = jnp.zeros_like(rs)
        st_v = st_ref[0]
        for h in range(2):
            rs[6 + 2 * h], rs[7 + 2 * h] = _col(st_v, h), _col(st_v, 2 + h)
        qa[...], qb[...] = _masked_pair(q_ref[...], is_a, SCALE)
        doa[...], dob[...] = _masked_pair(do_ref[...], is_a)

        def tile(j, slot, masked):
            k, v = kbuf[slot], vbuf[slot]
            tfwd_m, trev_m = tfwd_ref[...], trev_ref[...]
            if masked:
                tri = lax.broadcasted_iota(jnp.int32, (bq, bq), 0) > lax.broadcasted_iota(jnp.int32, (bq, bq), 1)
            dzs, ws = [], []
            for h, (qh, doh, dq) in enumerate(((qa, doa, dq_a), (qb, dob, dq_b))):
                z = _dot(qh[...], k, _NT)
                lk = -_softplus(z)
                if masked:
                    lk = jnp.where(tri, lk, 0.0)
                p_hi, p_lo = _two_sum(rs[3 * h], rs[3 * h + 1], jnp.sum(lk, axis=1, keepdims=True))
                rs[3 * h], rs[3 * h + 1] = p_hi, p_lo
                right = (rs[6 + 2 * h] - p_hi) + (rs[7 + 2 * h] - p_lo)
                w = jnp.exp(z + _split_dot(lk, trev_m) + right)
                if masked:
                    w = jnp.where(tri, w, 0.0)
                g = _dot(doh[...], v, _NT) * w
                g_left = rs[3 * h + 2]
                dz = g - jnp.exp(z + lk) * (_split_dot(g, tfwd_m) + g_left)
                if masked:
                    dz = jnp.where(tri, dz, 0.0)
                rs[3 * h + 2] = g_left + jnp.sum(g, axis=1, keepdims=True)
                dzb = dz.astype(BF16)
                dq[...] += _dot(dzb, k)
                dzs.append(dzb)
                ws.append(w.astype(BF16))
            rows = pl.ds(pl.multiple_of(j * bq, bq), bq)
            dk_ref[rows, :] += _dot(dzs[0], qa[...], _TN) + _dot(dzs[1], qb[...], _TN)
            dv_ref[rows, :] += _dot(ws[0], doa[...], _TN) + _dot(ws[1], dob[...], _TN)

        _walk_up(fetch, j0, i, tile)
        _prefetch_next(fetch, p, i, nq, lambda pair, blk: jmin_ref[pair, blk])
        dq_ref[...] = jnp.where(is_a, dq_a[...], dq_b[...]) * SCALE

    grid_spec = pltpu.PrefetchScalarGridSpec(
        num_scalar_prefetch=1, grid=(N_PAIRS, nq),
        in_specs=[pl.BlockSpec((bq, LANES), lambda p, i, jm: (i, col0 + p)),
                  pl.BlockSpec(memory_space=pl.ANY),
                  pl.BlockSpec((bq, LANES), lambda p, i, jm: (i, p)),
                  pl.BlockSpec((1, bq, 8), lambda p, i, jm: (p, i, 0)),
                  pl.BlockSpec((bq, bq), lambda p, i, jm: (0, 0)),
                  pl.BlockSpec((bq, bq), lambda p, i, jm: (0, 0))],
        out_specs=[pl.BlockSpec((bq, LANES), lambda p, i, jm: (i, p)),
                   pl.BlockSpec((S, LANES), lambda p, i, jm: (0, p)),
                   pl.BlockSpec((S, LANES), lambda p, i, jm: (0, p))],
        scratch_shapes=[pltpu.VMEM((bq, LANES), F32), pltpu.VMEM((bq, LANES), F32)]
        + [pltpu.VMEM((bq, LANES), BF16)] * 4 + [pltpu.VMEM((10, bq, 1), F32)]
        + [pltpu.VMEM((2, bq, LANES), BF16), pltpu.VMEM((2, bq, LANES), BF16), pltpu.SemaphoreType.DMA((2, 2))])
    return pl.pallas_call(
        body, name="sb_bwd", grid_spec=grid_spec,
        out_shape=[jax.ShapeDtypeStruct((S, GROUP_W), F32)] * 3,
        compiler_params=_params(VMEM_BIG),
    )(jmin, proj, proj, do, st, tfwd, trev)


def _walk_up(fetch, j0, i, tile):
    def step(j, carry):
        slot = lax.rem(j - j0, 2)
        for cp in fetch(j, slot):
            cp.wait()

        @pl.when(j < i)
        def _():
            for cp in fetch(j + 1, 1 - slot):
                cp.start()

        pl.when(j == i)(functools.partial(tile, j, slot, True))
        pl.when(j < i)(functools.partial(tile, j, slot, False))
        return carry

    lax.fori_loop(j0, i + 1, step, 0)


def _fox_start_blocks(proj, col0, c, bq):
    S = proj.shape[0]
    nq = S // bq
    nh = 2 * N_PAIRS

    def heads(first):
        return proj[:, first * LANES:(first + N_PAIRS) * LANES].astype(F32).reshape(S, nh, HEAD_DIM)

    q, k = heads(col0), heads(col0 + 4)
    qn = jnp.sqrt(jnp.sum(q * q, axis=-1))
    kmax = jnp.sqrt(jnp.sum(k * k, axis=-1)).max(axis=0)
    top = SCALE * (qn * kmax[None, :] - jnp.sum(q * k, axis=-1)) + c.T
    top = top.reshape(nq, bq, nh).max(axis=1)
    c_last = c[:, bq - 1::bq].T
    live = top[:, None, :] - c_last[None, :, :] >= -FOX_SKIP

    def first_block(lv):
        first = jnp.where(lv.any(axis=1), jnp.argmax(lv, axis=1), nq)
        return jnp.minimum(first, jnp.arange(nq)[:, None]).T.astype(jnp.int32)

    return jnp.concatenate([first_block(live.reshape(nq, nq, N_PAIRS, 2).any(axis=-1)), first_block(live)], axis=0)


def _fox_fwd(proj, col0, c_col, c_row, jstart, bq):
    S = proj.shape[0]
    bq = min(bq, S)
    nq = S // bq

    def body(js_ref, q_ref, kv_hbm, cc_ref, cr_ref, o_ref, st_ref, acc_a, acc_b, qa, qb, ml, kbuf, vbuf, sems):
        p, i = pl.program_id(0), pl.program_id(1)
        j0 = js_ref[p, i]
        fetch = _kv_fetcher(kv_hbm, kbuf, vbuf, sems, col0, bq, p, i, j0)
        is_a = lax.broadcasted_iota(jnp.int32, (bq, LANES), 1) < HEAD_DIM
        acc_a[...] = jnp.zeros_like(acc_a)
        acc_b[...] = jnp.zeros_like(acc_b)
        ml[0] = jnp.full((bq, 1), NEG_BIG, F32)
        ml[2] = jnp.full((bq, 1), NEG_BIG, F32)
        ml[1] = jnp.zeros((bq, 1), F32)
        ml[3] = jnp.zeros((bq, 1), F32)
        cc = cc_ref[0]
        ml[4], ml[5] = _col(cc, 0), _col(cc, 1)
        qa[...], qb[...] = _masked_pair(q_ref[...], is_a, SCALE)

        def tile(j, slot, masked):
            k, v = kbuf[slot], vbuf[slot]
            cols = pl.ds(pl.multiple_of(j * bq, bq), bq)
            if masked:
                tri = lax.broadcasted_iota(jnp.int32, (bq, bq), 0) >= lax.broadcasted_iota(jnp.int32, (bq, bq), 1)
            def head(h, qh, acc):
                s = _dot(qh[...], k, _NT) - cr_ref[0, pl.ds(h, 1), cols]
                if masked:
                    s = jnp.where(tri, s, NEG_BIG)
                m_prev, l_prev, cq = ml[2 * h], ml[2 * h + 1], ml[4 + h]
                m_new = jnp.maximum(m_prev, jnp.max(s, axis=1, keepdims=True) + cq)
                a = jnp.exp(m_prev - m_new)
                p = jnp.exp(s - (m_new - cq))
                ml[2 * h] = m_new
                ml[2 * h + 1] = a * l_prev + jnp.sum(p, axis=1, keepdims=True)
                acc[...] = a * acc[...] + _dot(p.astype(BF16), v)

            for h, (qh, acc) in enumerate(((qa, acc_a), (qb, acc_b))):
                pl.when(j >= js_ref[N_PAIRS + 2 * p + h, i])(functools.partial(head, h, qh, acc))

        _walk_up(fetch, j0, i, tile)
        _prefetch_next(fetch, p, i, nq, lambda pair, blk: js_ref[pair, blk])
        o_ref[...] = jnp.where(is_a, acc_a[...] / ml[1], acc_b[...] / ml[3])
        lane8 = lax.broadcasted_iota(jnp.int32, (bq, 8), 1)
        st = jnp.where(lane8 == 0, ml[0] + jnp.log(ml[1]), 0.0)
        st_ref[0] = jnp.where(lane8 == 1, ml[2] + jnp.log(ml[3]), st)

    grid_spec = pltpu.PrefetchScalarGridSpec(
        num_scalar_prefetch=1, grid=(N_PAIRS, nq),
        in_specs=[pl.BlockSpec((bq, LANES), lambda p, i, js: (i, col0 + p)),
                  pl.BlockSpec(memory_space=pl.ANY),
                  pl.BlockSpec((1, bq, 8), lambda p, i, js: (p, i, 0)),
                  pl.BlockSpec((1, 8, S), lambda p, i, js: (p, 0, 0))],
        out_specs=[pl.BlockSpec((bq, LANES), lambda p, i, js: (i, p)),
                   pl.BlockSpec((1, bq, 8), lambda p, i, js: (p, i, 0))],
        scratch_shapes=[pltpu.VMEM((bq, LANES), F32), pltpu.VMEM((bq, LANES), F32),
                        pltpu.VMEM((bq, LANES), BF16), pltpu.VMEM((bq, LANES), BF16),
                        pltpu.VMEM((6, bq, 1), F32),
                        pltpu.VMEM((2, bq, LANES), BF16), pltpu.VMEM((2, bq, LANES), BF16),
                        pltpu.SemaphoreType.DMA((2, 2))])
    return pl.pallas_call(
        body, name="fox_fwd", grid_spec=grid_spec,
        out_shape=[jax.ShapeDtypeStruct((S, GROUP_W), F32), jax.ShapeDtypeStruct((N_PAIRS, S, 8), F32)],
    )(jstart, proj, proj, c_col, c_row)


def _fox_bwd(proj, col0, do, o, st, c_col, c_row, jstart, bq):
    S = proj.shape[0]
    bq = min(bq, S)
    nq = S // bq

    def body(js_ref, q_ref, kv_hbm, do_ref, o_ref, st_ref, cc_ref, cr_ref,
             dq_ref, dk_ref, dv_ref, dc_ref, dcq_ref, dq_a, dq_b, qa, qb, doa, dob, dd, kbuf, vbuf, sems):
        p, i = pl.program_id(0), pl.program_id(1)
        j0 = js_ref[p, i]
        fetch = _kv_fetcher(kv_hbm, kbuf, vbuf, sems, col0, bq, p, i, j0)
        is_a = lax.broadcasted_iota(jnp.int32, (bq, LANES), 1) < HEAD_DIM

        @pl.when(i == 0)
        def _():
            dk_ref[...] = jnp.zeros_like(dk_ref)
            dv_ref[...] = jnp.zeros_like(dv_ref)
            dc_ref[...] = jnp.zeros_like(dc_ref)

        dq_a[...] = jnp.zeros_like(dq_a)
        dq_b[...] = jnp.zeros_like(dq_b)
        qa[...], qb[...] = _masked_pair(q_ref[...], is_a, SCALE)
        dov = do_ref[...]
        doa[...], dob[...] = _masked_pair(dov, is_a)
        prod = dov * o_ref[...]
        dd[0] = jnp.sum(jnp.where(is_a, prod, 0.0), axis=1, keepdims=True)
        dd[1] = jnp.sum(jnp.where(is_a, 0.0, prod), axis=1, keepdims=True)
        dd[2] = jnp.zeros((bq, 1), F32)
        dd[3] = jnp.zeros((bq, 1), F32)
        cc, st_v = cc_ref[0], st_ref[0]
        dd[4], dd[5] = _col(cc, 0) - _col(st_v, 0), _col(cc, 1) - _col(st_v, 1)

        def tile(j, slot, masked):
            k, v = kbuf[slot], vbuf[slot]
            if masked:
                tri = lax.broadcasted_iota(jnp.int32, (bq, bq), 0) >= lax.broadcasted_iota(jnp.int32, (bq, bq), 1)
            cols = pl.ds(pl.multiple_of(j * bq, bq), bq)

            def head(h, qh, doh, dq):
                pr = jnp.exp(_dot(qh[...], k, _NT) - cr_ref[0, pl.ds(h, 1), cols] + dd[4 + h])
                if masked:
                    pr = jnp.where(tri, pr, 0.0)
                ds = pr * (_dot(doh[...], v, _NT) - dd[h])
                dc_ref[0, pl.ds(h, 1), cols] -= jnp.sum(ds, axis=0, keepdims=True)
                dd[2 + h] += jnp.sum(ds, axis=1, keepdims=True)
                dsb = ds.astype(BF16)
                dq[...] += _dot(dsb, k)
                dk_ref[cols, :] += _dot(dsb, qh[...], _TN)
                dv_ref[cols, :] += _dot(pr.astype(BF16), doh[...], _TN)

            for h, (qh, doh, dq) in enumerate(((qa, doa, dq_a), (qb, dob, dq_b))):
                pl.when(j >= js_ref[N_PAIRS + 2 * p + h, i])(functools.partial(head, h, qh, doh, dq))

        _walk_up(fetch, j0, i, tile)
        _prefetch_next(fetch, p, i, nq, lambda pair, blk: js_ref[pair, blk])
        dq_ref[...] = jnp.where(is_a, dq_a[...], dq_b[...]) * SCALE
        lane8 = lax.broadcasted_iota(jnp.int32, (bq, 8), 1)
        dcq_ref[0] = jnp.where(lane8 == 0, dd[2], jnp.where(lane8 == 1, dd[3], 0.0))

    grid_spec = pltpu.PrefetchScalarGridSpec(
        num_scalar_prefetch=1, grid=(N_PAIRS, nq),
        in_specs=[pl.BlockSpec((bq, LANES), lambda p, i, js: (i, col0 + p)),
                  pl.BlockSpec(memory_space=pl.ANY),
                  pl.BlockSpec((bq, LANES), lambda p, i, js: (i, p)),
                  pl.BlockSpec((bq, LANES), lambda p, i, js: (i, p)),
                  pl.BlockSpec((1, bq, 8), lambda p, i, js: (p, i, 0)),
                  pl.BlockSpec((1, bq, 8), lambda p, i, js: (p, i, 0)),
                  pl.BlockSpec((1, 8, S), lambda p, i, js: (p, 0, 0))],
        out_specs=[pl.BlockSpec((bq, LANES), lambda p, i, js: (i, p)),
                   pl.BlockSpec((S, LANES), lambda p, i, js: (0, p)),
                   pl.BlockSpec((S, LANES), lambda p, i, js: (0, p)),
                   pl.BlockSpec((1, 8, S), lambda p, i, js: (p, 0, 0)),
                   pl.BlockSpec((1, bq, 8), lambda p, i, js: (p, i, 0))],
        scratch_shapes=[pltpu.VMEM((bq, LANES), F32), pltpu.VMEM((bq, LANES), F32)]
        + [pltpu.VMEM((bq, LANES), BF16)] * 4 + [pltpu.VMEM((6, bq, 1), F32)]
        + [pltpu.VMEM((2, bq, LANES), BF16), pltpu.VMEM((2, bq, LANES), BF16), pltpu.SemaphoreType.DMA((2, 2))])
    return pl.pallas_call(
        body, name="fox_bwd", grid_spec=grid_spec,
        out_shape=[jax.ShapeDtypeStruct((S, GROUP_W), F32)] * 3
        + [jax.ShapeDtypeStruct((N_PAIRS, 8, S), F32), jax.ShapeDtypeStruct((N_PAIRS, S, 8), F32)],
        compiler_params=_params(VMEM_BIG),
    )(jstart, proj, proj, do, o, st, c_col, c_row)


_HBM = pl.BlockSpec(memory_space=pltpu.HBM)


def _coords():
    return lax.axis_index("x"), lax.axis_index("y"), lax.axis_index("c")


def _gather_copies(ins, outs, send_sems, recv_sems, loc_sems):
    n = len(ins)
    x, y, c = _coords()
    mine = 2 * x + y
    chips = [(1 - x, y), (x, 1 - y), (1 - x, 1 - y)]

    def copy(w, r, slab, to):
        return pltpu.make_async_remote_copy(
            src_ref=ins[w], dst_ref=outs[w].at[slab], send_sem=send_sems.at[3 * w + r],
            recv_sem=recv_sems.at[3 * w + r], device_id=to, device_id_type=MESH)

    def own():
        local = [pltpu.make_async_copy(ins[w], outs[w].at[mine], loc_sems.at[w]) for w in range(n)]
        return local, [copy(w, r, mine, (cx, cy, c)) for w in range(n) for r, (cx, cy) in enumerate(chips)]

    def start():
        local, sends = own()
        for cp in local + sends:
            cp.start()

    def wait():
        local, sends = own()
        for w in range(n):
            for r, (cx, cy) in enumerate(chips):
                copy(w, r, 2 * cx + cy, (cx, cy, c)).wait_recv()
        for cp in sends:
            cp.wait_send()
        for cp in local:
            cp.wait()

    return start, wait


def _gather_shapes(shards):
    n = len(shards)
    return ([jax.ShapeDtypeStruct((4,) + s.shape, s.dtype) for s in shards],
            [pltpu.SemaphoreType.DMA((3 * n,)), pltpu.SemaphoreType.DMA((3 * n,)), pltpu.SemaphoreType.DMA((n,))])


def _allgather_chips(shards):
    n = len(shards)

    def body(*refs):
        start, wait = _gather_copies(refs[:n], refs[n:2 * n], *refs[2 * n:])
        start()
        wait()

    out_shape, sems = _gather_shapes(shards)
    return pl.pallas_call(body, name="allgather_weights", in_specs=[_HBM] * n, out_specs=[_HBM] * n,
                          out_shape=out_shape, scratch_shapes=sems)(*shards)


def _proj_gather(x, w, shards, tm, tn):
    (M, K), N, n = x.shape, w.shape[1], len(shards)
    tm = min(tm, M)
    gi, gj = M // tm, N // tn

    def body(a_ref, b_ref, *rest):
        o_ref = rest[n]
        start, wait = _gather_copies(rest[:n], rest[n + 1:2 * n + 1], *rest[2 * n + 1:])
        i, j = pl.program_id(0), pl.program_id(1)
        pl.when(jnp.logical_and(i == 0, j == 0))(start)
        o_ref[...] = _dot(a_ref[...].astype(BF16), b_ref[...]).astype(o_ref.dtype)
        pl.when(jnp.logical_and(i == gi - 1, j == gj - 1))(wait)

    out_shape, sems = _gather_shapes(shards)
    return pl.pallas_call(
        body, name="proj_gather", grid=(gi, gj),
        in_specs=[pl.BlockSpec((tm, K), lambda i, j: (i, 0)), pl.BlockSpec((K, tn), lambda i, j: (0, j))] + [_HBM] * n,
        out_specs=[pl.BlockSpec((tm, tn), lambda i, j: (i, j))] + [_HBM] * n,
        out_shape=[jax.ShapeDtypeStruct((M, N), BF16)] + out_shape, scratch_shapes=sems,
    )(x, w, *shards)


def _exchange(parts, per_chip):
    n = len(parts)
    half = [p.shape[1] // 2 for p in parts] if per_chip else None

    def body(*refs):
        ins, outs = refs[:n], refs[n:2 * n]
        send_sems, recv_sems, loc_sems = refs[2 * n:]
        x, y, c = _coords()
        me = 4 * x + 2 * y + c
        peers = [(x ^ fx, y ^ fy, c ^ fc) for fx in (0, 1) for fy in (0, 1) for fc in (0, 1)][1:]

        def src(w, dev):
            if not per_chip:
                return ins[w]
            return ins[w].at[2 * dev[0] + dev[1], pl.ds(pl.multiple_of(dev[2] * half[w], 16), half[w]), :]

        local = [pltpu.make_async_copy(src(w, (x, y, c)), outs[w].at[me], loc_sems.at[w]) for w in range(n)]
        for cp in local:
            cp.start()

        def copy(w, r, source, slab, to):
            return pltpu.make_async_remote_copy(
                src_ref=source, dst_ref=outs[w].at[slab], send_sem=send_sems.at[7 * w + r],
                recv_sem=recv_sems.at[7 * w + r], device_id=to, device_id_type=MESH)

        sends = [copy(w, r, src(w, dev), me, dev) for w in range(n) for r, dev in enumerate(peers)]
        for cp in sends:
            cp.start()
        for w in range(n):
            for r, dev in enumerate(peers):
                copy(w, r, src(w, dev), 4 * dev[0] + 2 * dev[1] + dev[2], dev).wait_recv()
        for cp in sends:
            cp.wait_send()
        for cp in local:
            cp.wait()

    return pl.pallas_call(
        body, name="exchange_per_chip" if per_chip else "exchange_all",
        in_specs=[_HBM] * n, out_specs=[_HBM] * n,
        out_shape=[jax.ShapeDtypeStruct((8, half[w], p.shape[2]) if per_chip else (8,) + p.shape, p.dtype)
                   for w, p in enumerate(parts)],
        scratch_shapes=[pltpu.SemaphoreType.DMA((7 * n,)), pltpu.SemaphoreType.DMA((7 * n,)),
                        pltpu.SemaphoreType.DMA((n,))],
    )(*parts)


def _sibling_swap(halves):
    n = len(halves)

    def body(*refs):
        ins, outs = refs[:n], refs[n:2 * n]
        send_sems, recv_sems, loc_sems = refs[2 * n:]
        x, y, c = _coords()

        def rows(w, core):
            rh = halves[w].shape[0]
            return outs[w].at[pl.ds(pl.multiple_of(core * rh, 8), rh), :]

        def copy(w, core):
            return pltpu.make_async_remote_copy(
                src_ref=ins[w], dst_ref=rows(w, core), send_sem=send_sems.at[w], recv_sem=recv_sems.at[w],
                device_id=(x, y, 1 - c), device_id_type=MESH)

        local = [pltpu.make_async_copy(ins[w], rows(w, c), loc_sems.at[w]) for w in range(n)]
        sends = [copy(w, c) for w in range(n)]
        for cp in local + sends:
            cp.start()
        for w in range(n):
            copy(w, 1 - c).wait_recv()
        for cp in sends:
            cp.wait_send()
        for cp in local:
            cp.wait()

    vmem = pl.BlockSpec(memory_space=pltpu.VMEM)
    return pl.pallas_call(
        body, name="sibling_swap", in_specs=[vmem] * n, out_specs=[vmem] * n,
        out_shape=[jax.ShapeDtypeStruct((2 * h.shape[0], h.shape[1]), h.dtype) for h in halves],
        scratch_shapes=[pltpu.SemaphoreType.DMA((n,)), pltpu.SemaphoreType.DMA((n,)), pltpu.SemaphoreType.DMA((n,))],
    )(*halves)


def _adamw(w, g, m, v):
    m = ADAM_B1 * m + (1.0 - ADAM_B1) * g
    v = ADAM_B2 * v + (1.0 - ADAM_B2) * (g * g)
    m_hat = m / (1.0 - ADAM_B1 ** ADAM_STEP)
    v_hat = v / (1.0 - ADAM_B2 ** ADAM_STEP)
    delta = -ADAM_LR * (m_hat / (jnp.sqrt(v_hat) + ADAM_EPS) + ADAM_WD * w)
    return delta, m, v


def _sum_parts(parts, name, tr):
    _, R, C = parts.shape
    assert R % tr == 0

    def body(p_ref, g_ref):
        g = p_ref[0].astype(F32)
        for d in range(1, 8):
            g = g + p_ref[d].astype(F32)
        g_ref[...] = g

    return pl.pallas_call(
        body, name=name, grid=(R // tr,),
        in_specs=[pl.BlockSpec((8, tr, C), lambda i: (0, i, 0))],
        out_specs=pl.BlockSpec((tr, C), lambda i: (i, 0)), out_shape=jax.ShapeDtypeStruct((R, C), F32),
    )(parts)


def _adamw_call(g, w, m, v, name, tr):
    R, C = w.shape
    assert R % tr == 0

    def body(g_ref, w_ref, m_ref, v_ref, d_ref, nm_ref, nv_ref):
        d_ref[...], nm_ref[...], nv_ref[...] = _adamw(w_ref[...], g_ref[...], m_ref[...], v_ref[...])

    tile = pl.BlockSpec((tr, C), lambda i: (i, 0))
    return pl.pallas_call(
        body, name=name, grid=(R // tr,), in_specs=[tile] * 4,
        out_specs=[tile] * 3, out_shape=[jax.ShapeDtypeStruct((R, C), F32)] * 3,
    )(g, w, m, v)


def _sum_adamw_small(parts, w, m, v):
    def body(p_ref, w_ref, m_ref, v_ref, g_ref, d_ref, nm_ref, nv_ref, loss_ref):
        g = p_ref[0]
        for d in range(1, 8):
            g = g + p_ref[d]
        g_ref[...] = g
        d_ref[...], nm_ref[...], nv_ref[...] = _adamw(w_ref[...], g, m_ref[...], v_ref[...])
        row = lax.broadcasted_iota(jnp.int32, g.shape, 0)
        per_row = jnp.sum(jnp.where(row == 6, g, 0.0), axis=1, keepdims=True)
        loss_ref[...] = jnp.zeros((8, LANES), F32) + jnp.sum(per_row, axis=0, keepdims=True)

    return pl.pallas_call(
        body, name="sum_adamw_small",
        out_shape=[jax.ShapeDtypeStruct((8, D_MODEL), F32)] * 4 + [jax.ShapeDtypeStruct((8, LANES), F32)],
    )(parts, w, m, v)


def _pack_small(ln1_g, ln1_b, ln2_g, ln2_b, g_sb, g_fox, b_f):
    row5 = jnp.pad(b_f.reshape(1, N_FOX), ((0, 0), (0, D_MODEL - N_FOX)))
    rows = [ln1_g.reshape(1, -1), ln1_b.reshape(1, -1), ln2_g.reshape(1, -1), ln2_b.reshape(1, -1),
            jnp.concatenate([g_sb.reshape(1, -1), g_fox.reshape(1, -1)], axis=1), row5,
            jnp.zeros((2, D_MODEL), F32)]
    return jnp.concatenate(rows, axis=0)


def _unpack_small(p):
    return {"ln1_g": p[0:1], "ln1_b": p[1:2], "ln2_g": p[2:3], "ln2_b": p[3:4], "g_sb": p[4:5, :GROUP_W],
            "g_fox": p[4:5, GROUP_W:], "b_f": p[5:6, :N_FOX]}


def kernel(x, w_in, b_f, g_sb, g_fox, w_out, ln1_g, ln1_b, ln2_g, ln2_b, w_gate_up, w_down, loss_target, m_w_in, m_b_f, m_g_sb, m_g_fox, m_w_out, m_ln1_g, m_ln1_b, m_ln2_g, m_ln2_b, m_w_gate_up, m_w_down, v_w_in, v_b_f, v_g_sb, v_g_fox, v_w_out, v_ln1_g, v_ln1_b, v_ln2_g, v_ln2_b, v_w_gate_up, v_w_down):
    S = x.shape[1]
    x2 = x.reshape(S, D_MODEL)
    tgt = loss_target.reshape(S, D_MODEL)
    TM = 1024
    TR = 512
    BQ = ATTN_BLOCK
    in_w = w_in.shape[2]
    gu_w = w_gate_up.shape[2]

    shards = [w_in[0].astype(BF16), w_out[0].astype(BF16), w_gate_up[0].astype(BF16), w_down[0].astype(BF16)]
    (wi_s,) = _allgather_chips(shards[:1])
    wi = wi_s.transpose(1, 0, 2).reshape(D_MODEL, 4 * in_w)
    w_sb, w_fx = wi[:, :QKV_W // 2], wi[:, QKV_W // 2:QKV_W]
    wqkv = wi[:, :QKV_W]
    wft = wi[:, QKV_W:].T
    proj, wo_s, wgu_s, wd_s = _proj_gather(x2, wqkv, shards[1:], TM, 512)
    wo = wo_s.reshape(D_MODEL, D_MODEL)
    wgu = wgu_s.transpose(1, 0, 2).reshape(D_MODEL, 2 * D_FF)
    wg, wu = wgu[:, :D_FF], wgu[:, D_FF:]
    wd = wd_s.reshape(D_FF, D_MODEL)
    g_row = jnp.concatenate([g_sb, g_fox], axis=1)
    hid = np.arange(D_MODEL) // HEAD_DIM
    he_np = (hid[:, None] == np.arange(LANES)[None, :]).astype(np.float32)
    he, het = jnp.asarray(he_np, BF16), jnp.asarray(he_np.T, BF16)

    lf = _fgate_fwd(x2, wft, b_f.reshape(N_FOX, 1), TM)
    c = _cumsum_fwd(lf)
    c_pair = c.reshape(N_PAIRS, 2, S)
    c_row = jnp.pad(c_pair, ((0, 0), (0, 6), (0, 0)))
    c_col = jnp.pad(c_pair.transpose(0, 2, 1), ((0, 0), (0, 0), (0, 6)))

    o_sb, st_sb, jmin_sb = _sb_fwd(proj, 0, BQ)
    jstart_fx = _fox_start_blocks(proj, 12, c, min(BQ, S))
    o_fx, st_fx = _fox_fwd(proj, 12, c_col, c_row, jstart_fx, BQ)

    def attn_post(i, osb_ref, ofx_ref, g_ref, he_ref, het_ref, on_ref):
        o = jnp.concatenate([osb_ref[...], ofx_ref[...]], axis=1)
        ms = _head_sums(o * o, he_ref[...], het_ref[...]) * (1.0 / HEAD_DIM)
        on_ref[...] = (o * lax.rsqrt(ms + RMS_EPS) * g_ref[...]).astype(BF16)

    (on,) = _rowwise(attn_post, "attn_post", S, TR,
                     [(o_sb, "t"), (o_fx, "t"), (g_row, "f"), (he, "f"), (het, "f")],
                     [((S, D_MODEL), BF16, "t")])

    u1 = _matmul(on, wo, mode="nn", name="mix", tm=TM, tn=D_MODEL, tk=D_MODEL, outs=[F32],
                 extras=[(x2, (TM if S >= TM else S, D_MODEL), _tile_ij)],
                 epilogue=lambda acc, xv: (ALPHA * xv + acc,))

    def ln1_fwd(i, u_ref, g_ref, b_ref, h_ref):
        xh, _ = _ln_stats(u_ref[...])
        h_ref[...] = xh * g_ref[...] + b_ref[...]

    (h1,) = _rowwise(ln1_fwd, "ln1_fwd", S, TR, [(u1, "t"), (ln1_g, "f"), (ln1_b, "f")], [((S, D_MODEL), F32, "t")])

    tm_e = TM if S >= TM else S
    n_ff = D_FF // 256

    def gate_up_body(h_ref, wg_ref, wu_ref, g_ref, u_ref, a_ref):
        h = h_ref[...].astype(BF16)
        g, u = _dot(h, wg_ref[...]), _dot(h, wu_ref[...])
        g_ref[...] = g.astype(BF16)
        u_ref[...] = u.astype(BF16)
        a_ref[...] = (g / (1.0 + jnp.exp(-g)) * u).astype(BF16)

    ff_tile = pl.BlockSpec((tm_e, 256), lambda i, j: (i, j))
    gate, up, act = pl.pallas_call(
        gate_up_body, name="gate_up_act", grid=(S // tm_e, n_ff),
        in_specs=[pl.BlockSpec((tm_e, D_MODEL), lambda i, j: (i, 0)),
                  pl.BlockSpec((D_MODEL, 256), lambda i, j: (0, j)),
                  pl.BlockSpec((D_MODEL, 256), lambda i, j: (0, j + n_ff))],
        out_specs=[ff_tile] * 3, out_shape=[jax.ShapeDtypeStruct((S, D_FF), BF16)] * 3)(h1, wgu, wgu)

    u2 = _matmul(act, wd, mode="nn", name="ffn_down", tm=TM, tn=D_MODEL, tk=D_FF, outs=[F32],
                 extras=[(h1, (TM if S >= TM else S, D_MODEL), _tile_ij)],
                 epilogue=lambda acc, hv: (ALPHA * hv + acc,))

    def ln2_loss(i, u_ref, t_ref, g_ref, b_ref, du_ref, acc_ref):
        xh, r = _ln_stats(u_ref[...])
        g = g_ref[...]
        err = xh * g + b_ref[...] - t_ref[...]
        dy = err * (1.0 / D_MODEL)
        du_ref[...] = _ln_bwd(dy, xh, r, g)
        _acc_rows(i, acc_ref, {2: jnp.sum(dy * xh, axis=0, keepdims=True), 3: jnp.sum(dy, axis=0, keepdims=True),
                               6: jnp.sum(err * err, axis=0, keepdims=True) * (0.5 / D_MODEL)})

    du2, acc_ln2 = _rowwise(ln2_loss, "ln2_loss", S, TR, [(u2, "t"), (tgt, "t"), (ln2_g, "f"), (ln2_b, "f")],
                            [((S, D_MODEL), F32, "t"), ((8, D_MODEL), F32, "f")])

    d_wd = _matmul(act, du2, mode="tn", name="dw_down", tm=1408, tn=D_MODEL, tk=TM, outs=[F32])

    def dgu_epilogue(da, g, u):
        g, u = g.astype(F32), u.astype(F32)
        s = 1.0 / (1.0 + jnp.exp(-g))
        return da * u * (s * (1.0 + g * (1.0 - s))), da * (g * s)

    dgate, dup = _matmul(du2, wd, mode="nt", name="d_act", tm=TM, tn=1408, tk=D_MODEL, outs=[BF16, BF16],
                         extras=[(gate, (tm_e, 1408), _tile_ij), (up, (tm_e, 1408), _tile_ij)],
                         epilogue=dgu_epilogue)
    d_wg = _matmul(h1, dgate, mode="tn", name="dw_gate", tm=D_MODEL, tn=1408, tk=TM, outs=[F32])
    d_wu = _matmul(h1, dup, mode="tn", name="dw_up", tm=D_MODEL, tn=1408, tk=TM, outs=[F32])
    dh1 = _matmul(dgate, wg, mode="nt", name="dh1_gate", tm=TM, tn=D_MODEL, tk=D_FF, outs=[F32],
                  extras=[(du2, (tm_e, D_MODEL), _tile_ij)], epilogue=lambda acc, e: (ALPHA * e + acc,))
    dh1 = _matmul(dup, wu, mode="nt", name="dh1_up", tm=TM, tn=D_MODEL, tk=D_FF, outs=[F32],
                  extras=[(dh1, (tm_e, D_MODEL), _tile_ij)], epilogue=lambda acc, e: (e + acc,))

    def ln1_bwd(i, dh_ref, u_ref, g_ref, du_ref, acc_ref):
        xh, r = _ln_stats(u_ref[...])
        dh = dh_ref[...]
        du_ref[...] = _ln_bwd(dh, xh, r, g_ref[...])
        _acc_rows(i, acc_ref, {0: jnp.sum(dh * xh, axis=0, keepdims=True), 1: jnp.sum(dh, axis=0, keepdims=True)})

    du1, acc_ln1 = _rowwise(ln1_bwd, "ln1_bwd", S, TR, [(dh1, "t"), (u1, "t"), (ln1_g, "f")],
                            [((S, D_MODEL), F32, "t"), ((8, D_MODEL), F32, "f")])
    d_wo = _matmul(on, du1, mode="tn", name="dw_out", tm=D_MODEL, tn=D_MODEL, tk=TM, outs=[F32])
    don = _matmul(du1, wo, mode="nt", name="d_on", tm=TM, tn=D_MODEL, tk=D_MODEL, outs=[F32])

    def rms_bwd(i, don_ref, osb_ref, ofx_ref, g_ref, he_ref, het_ref, dosb_ref, dofx_ref, acc_ref):
        o = jnp.concatenate([osb_ref[...], ofx_ref[...]], axis=1)
        hev, hetv = he_ref[...], het_ref[...]
        r = lax.rsqrt(_head_sums(o * o, hev, hetv) * (1.0 / HEAD_DIM) + RMS_EPS)
        dn = don_ref[...]
        dg = dn * g_ref[...]
        do = r * dg - o * (r * r * r) * (_head_sums(dg * o, hev, hetv) * (1.0 / HEAD_DIM))
        dosb_ref[...] = do[:, :GROUP_W]
        dofx_ref[...] = do[:, GROUP_W:]
        _acc_rows(i, acc_ref, {4: jnp.sum(dn * o * r, axis=0, keepdims=True)})

    do_sb, do_fx, acc_rms = _rowwise(
        rms_bwd, "rms_bwd", S, TR, [(don, "t"), (o_sb, "t"), (o_fx, "t"), (g_row, "f"), (he, "f"), (het, "f")],
        [((S, GROUP_W), F32, "t"), ((S, GROUP_W), F32, "t"), ((8, D_MODEL), F32, "f")])

    dq_sb, dk_sb, dv_sb = _sb_bwd(proj, 0, do_sb, st_sb, jmin_sb, BQ)
    dq_fx, dk_fx, dv_fx, dc, dcq = _fox_bwd(proj, 12, do_fx, o_fx, st_fx, c_col, c_row, jstart_fx, BQ)
    dc = dc[:, :2, :] + dcq[:, :, :2].transpose(0, 2, 1)
    dfl, dbf = _fgate_bwd(dc.reshape(N_FOX, S), lf)
    dp_sb = jnp.concatenate([dq_sb, dk_sb, dv_sb], axis=1).astype(BF16)
    dp_fx = jnp.concatenate([dq_fx, dk_fx, dv_fx], axis=1).astype(BF16)

    d_wsb = _matmul(x2, dp_sb, mode="tn", name="dw_in_sb", tm=D_MODEL, tn=QKV_W // 2, tk=TM, outs=[F32])
    d_wfx = _matmul(x2, dp_fx, mode="tn", name="dw_in_fx", tm=D_MODEL, tn=QKV_W // 2, tk=TM, outs=[F32])
    d_wft = _matmul(dfl, x2, mode="nn", name="dw_in_f", tm=N_FOX, tn=D_MODEL, tk=TM, outs=[F32])
    dx = _matmul(dp_sb, w_sb, mode="nt", name="dx_sb", tm=TM, tn=D_MODEL, tk=QKV_W // 2, outs=[F32],
                 extras=[(du1, (tm_e, D_MODEL), _tile_ij)], epilogue=lambda acc, e: (ALPHA * e + acc,))
    dx = _matmul(dp_fx, w_fx, mode="nt", name="dx_fx", tm=TM, tn=D_MODEL, tk=QKV_W // 2, outs=[F32],
                 extras=[(dx, (tm_e, D_MODEL), _tile_ij)], epilogue=lambda acc, e: (e + acc,))
    dx = _matmul(dfl, wft, mode="tn", name="dx_f", tm=TM, tn=D_MODEL, tk=N_FOX, outs=[F32],
                 extras=[(dx, (tm_e, D_MODEL), _tile_ij)], epilogue=lambda acc, e: (e + acc,))

    d_wi = jnp.concatenate([d_wsb, d_wfx, d_wft.T], axis=1)
    d_wgu = jnp.concatenate([d_wg, d_wu], axis=1)
    parts = [d_wi.reshape(D_MODEL, 4, in_w).transpose(1, 0, 2).astype(BF16),
             d_wo.reshape(4, D_MODEL // 4, D_MODEL).astype(BF16),
             d_wgu.reshape(D_MODEL, 4, gu_w).transpose(1, 0, 2).astype(BF16),
             d_wd.reshape(4, D_FF // 4, D_MODEL).astype(BF16)]
    got = _exchange(parts, True)
    big_names = ("w_in", "w_out", "w_gate_up", "w_down")
    halves = [_sum_parts(p, "sum_" + nm, tr) for nm, p, tr in zip(big_names, got, (256, 128, 128, 176))]
    grads = _sibling_swap(halves)
    big = {}
    for nm, g, w, m, v, tr in zip(big_names, grads, (w_in, w_out, w_gate_up, w_down),
                                  (m_w_in, m_w_out, m_w_gate_up, m_w_down),
                                  (v_w_in, v_w_out, v_w_gate_up, v_w_down), (256, 256, 256, 176)):
        big[nm] = [r[None] for r in [g] + list(_adamw_call(g, w[0], m[0], v[0], "adamw_" + nm, tr))]

    small = acc_ln2 + acc_ln1 + acc_rms
    small = small + jnp.pad(dbf.reshape(1, N_FOX), ((5, 2), (0, D_MODEL - N_FOX)))
    (small_all,) = _exchange([small], False)
    sw = _pack_small(ln1_g, ln1_b, ln2_g, ln2_b, g_sb, g_fox, b_f)
    sm = _pack_small(m_ln1_g, m_ln1_b, m_ln2_g, m_ln2_b, m_g_sb, m_g_fox, m_b_f)
    sv = _pack_small(v_ln1_g, v_ln1_b, v_ln2_g, v_ln2_b, v_g_sb, v_g_fox, v_b_f)
    sg, sd, snm, snv, loss_blk = _sum_adamw_small(small_all, sw, sm, sv)
    sg, sd, snm, snv = _unpack_small(sg), _unpack_small(sd), _unpack_small(snm), _unpack_small(snv)

    names = ["w_in", "b_f", "g_sb", "g_fox", "w_out", "ln1_g", "ln1_b", "ln2_g", "ln2_b", "w_gate_up", "w_down"]
    outs = [loss_blk[0, 0], dx.reshape(1, S, D_MODEL)]
    for k, table in enumerate((sg, sd, snm, snv)):
        outs += [big[n][k] if n in big else table[n] for n in names]
    return tuple(outs)
```

```python
import functools

import numpy as np
import jax
import jax.numpy as jnp
from jax import lax
from jax.experimental import pallas as pl
from jax.experimental.pallas import tpu as pltpu

F32 = jnp.float32
BF16 = jnp.bfloat16

D_MODEL = 1024
HEAD_DIM = 64
LANES = 128
N_PAIRS = 4
GROUP_W = 512
QKV_W = 3072
D_FF = 2816
N_FOX = 8
ALPHA = 2.0 ** 0.25
LN_EPS = 1e-5
RMS_EPS = 1e-6
SCALE = HEAD_DIM ** -0.5
NEG_BIG = -1e30
FOX_SKIP = 30.0
SB_STOP = -105.0
ADAM_LR, ADAM_B1, ADAM_B2, ADAM_EPS, ADAM_WD, ADAM_STEP = 0.001, 0.9, 0.999, 1e-08, 0.01, 10
SCAN_GROUP = 8
ATTN_BLOCK = 256
VMEM_BIG = 56 * 1024 * 1024
MESH = pl.DeviceIdType.MESH

_NN = (((1,), (0,)), ((), ()))
_NT = (((1,), (1,)), ((), ()))
_TN = (((0,), (0,)), ((), ()))


def _dot(a, b, dims=_NN):
    return lax.dot_general(a, b, dims, preferred_element_type=F32)


def _split_dot(x, t):
    hi = x.astype(BF16)
    lo = (x - hi.astype(F32)).astype(BF16)
    return _dot(hi, t) + _dot(lo, t)


def _softplus(z):
    return jnp.maximum(z, 0.0) + jnp.log1p(jnp.exp(-jnp.abs(z)))


def _col(v, h):
    lane = lax.broadcasted_iota(jnp.int32, v.shape, 1)
    return jnp.sum(jnp.where(lane == h, v, 0.0), axis=1, keepdims=True)


def _two_sum(hi, lo, b):
    s = hi + b
    bb = s - hi
    err = (hi - (s - bb)) + (b - bb)
    return s, lo + err


def _params(vmem=None):
    return pltpu.CompilerParams(vmem_limit_bytes=vmem) if vmem else None


def _matmul(a, b, *, mode, name, tm, tn, tk, outs, extras=(), epilogue=None, vmem=None):
    if mode == "nn":
        (M, K), (_, N) = a.shape, b.shape
    elif mode == "nt":
        (M, K), (N, _) = a.shape, b.shape
    else:
        (K, M), (_, N) = a.shape, b.shape
    tm, tn, tk = min(tm, M), min(tn, N), min(tk, K)
    assert M % tm == 0 and N % tn == 0 and K % tk == 0, (name, M, N, K, tm, tn, tk)
    nk = K // tk
    dims = {"nn": _NN, "nt": _NT, "tn": _TN}[mode]
    if mode == "tn":
        a_spec = pl.BlockSpec((tk, tm), lambda i, j, k: (k, i))
    else:
        a_spec = pl.BlockSpec((tm, tk), lambda i, j, k: (i, k))
    if mode == "nt":
        b_spec = pl.BlockSpec((tn, tk), lambda i, j, k: (j, k))
    else:
        b_spec = pl.BlockSpec((tk, tn), lambda i, j, k: (k, j))
    ex_specs = [pl.BlockSpec(bs, (lambda i, j, k, f=f: f(i, j))) for (_, bs, f) in extras]
    ne, no = len(extras), len(outs)
    if epilogue is None:
        epilogue = lambda acc: (acc,)

    def body(a_ref, b_ref, *rest):
        ex_refs, out_refs, acc = rest[:ne], rest[ne:ne + no], rest[-1]
        k = pl.program_id(2)

        @pl.when(k == 0)
        def _():
            acc[...] = jnp.zeros_like(acc)

        acc[...] += _dot(a_ref[...].astype(BF16), b_ref[...].astype(BF16), dims)

        @pl.when(k == nk - 1)
        def _():
            res = epilogue(acc[...], *[e[...] for e in ex_refs])
            for r, o in zip(res, out_refs):
                o[...] = r.astype(o.dtype)

    res = pl.pallas_call(
        body, name=name, grid=(M // tm, N // tn, nk),
        in_specs=[a_spec, b_spec] + ex_specs,
        out_specs=[pl.BlockSpec((tm, tn), lambda i, j, k: (i, j)) for _ in outs],
        out_shape=[jax.ShapeDtypeStruct((M, N), d) for d in outs],
        scratch_shapes=[pltpu.VMEM((tm, tn), F32)],
        compiler_params=_params(vmem),
    )(a, b, *[e[0] for e in extras])
    return res[0] if no == 1 else res


def _tile_ij(i, j):
    return (i, j)


def _rowwise(fn, name, rows, tm, ins, outs, vmem=None):
    tm = min(tm, rows)
    assert rows % tm == 0

    def spec(shape, kind):
        if kind == "t":
            return pl.BlockSpec((tm,) + tuple(shape[1:]), lambda i: (i,) + (0,) * (len(shape) - 1))
        return pl.BlockSpec(tuple(shape), lambda i: (0,) * len(shape))

    def body(*refs):
        fn(pl.program_id(0), *refs)

    return pl.pallas_call(
        body, name=name, grid=(rows // tm,),
        in_specs=[spec(a.shape, k) for a, k in ins],
        out_specs=[spec(s, k) for s, _, k in outs],
        out_shape=[jax.ShapeDtypeStruct(s, d) for s, d, _ in outs],
        compiler_params=_params(vmem),
    )(*[a for a, _ in ins])


def _ln_stats(u):
    mu = jnp.mean(u, axis=-1, keepdims=True)
    d = u - mu
    var = jnp.mean(d * d, axis=-1, keepdims=True)
    r = lax.rsqrt(var + LN_EPS)
    return d * r, r


def _ln_bwd(dh, xh, r, g):
    dxh = dh * g
    m1 = jnp.mean(dxh, axis=-1, keepdims=True)
    m2 = jnp.mean(dxh * xh, axis=-1, keepdims=True)
    return r * (dxh - m1 - xh * m2)


def _acc_rows(i, ref, rows):
    @pl.when(i == 0)
    def _():
        ref[...] = jnp.zeros_like(ref)
    for r, v in rows.items():
        ref[pl.ds(r, 1), :] += v


def _head_sums(v, he, het):
    return _split_dot(_split_dot(v, he), het)


def _fgate_fwd(x, wft, bf_col, tm):
    S = x.shape[0]
    tm = min(tm, S)

    def body(wft_ref, bf_ref, x_ref, lf_ref):
        f = _dot(wft_ref[...], x_ref[...].astype(BF16), _NT) + bf_ref[...]
        lf_ref[...] = -_softplus(-f)

    return pl.pallas_call(
        body, name="fgate_fwd", grid=(S // tm,),
        in_specs=[pl.BlockSpec((N_FOX, D_MODEL), lambda i: (0, 0)), pl.BlockSpec((N_FOX, 1), lambda i: (0, 0)),
                  pl.BlockSpec((tm, D_MODEL), lambda i: (i, 0))],
        out_specs=pl.BlockSpec((N_FOX, tm), lambda i: (0, i)),
        out_shape=jax.ShapeDtypeStruct((N_FOX, S), F32),
    )(wft, bf_col, x)


def _chunk_scan(v, reverse):
    lane = lax.broadcasted_iota(jnp.int32, v.shape, 1)
    sh = 1
    while sh < LANES:
        if reverse:
            v = v + jnp.where(lane < LANES - sh, pltpu.roll(v, LANES - sh, 1), 0.0)
        else:
            v = v + jnp.where(lane >= sh, pltpu.roll(v, sh, 1), 0.0)
        sh *= 2
    return v


def _cumsum_fwd(lf):
    n, S = lf.shape
    nc = S // LANES

    grp = min(SCAN_GROUP, nc)

    def body(lf_ref, c_ref):
        def step(gi, carry):
            sls = [pl.ds(pl.multiple_of((gi * grp + g) * LANES, LANES), LANES) for g in range(grp)]
            vs = [_chunk_scan(lf_ref[:, sl], False) for sl in sls]
            tots = [_col(v, LANES - 1) for v in vs]
            for sl, v, t in zip(sls, vs, tots):
                c_ref[:, sl] = v + carry
                carry = carry + t
            return carry
        lax.fori_loop(0, nc // grp, step, jnp.zeros((n, 1), F32))

    return pl.pallas_call(body, name="cumsum_fwd", out_shape=jax.ShapeDtypeStruct((n, S), F32))(lf)


def _fgate_bwd(dc, lf):
    n, S = dc.shape
    nc = S // LANES

    grp = min(SCAN_GROUP, nc)

    def body(dc_ref, lf_ref, dfl_ref, dbf_ref):
        def step(t, carry):
            car, tot = carry
            gi = nc // grp - 1 - t
            sls = [pl.ds(pl.multiple_of((gi * grp + g) * LANES, LANES), LANES) for g in range(grp)]
            vs = [_chunk_scan(dc_ref[:, sl], True) for sl in sls]
            firsts = [_col(v, 0) for v in vs]
            for sl, v, f in reversed(list(zip(sls, vs, firsts))):
                dfl = (v + car) * (1.0 - jnp.exp(lf_ref[:, sl]))
                dfl_ref[:, sl] = dfl
                tot = tot + jnp.sum(dfl, axis=1, keepdims=True)
                car = car + f
            return car, tot
        _, tot = lax.fori_loop(0, nc // grp, step, (jnp.zeros((n, 1), F32), jnp.zeros((n, 1), F32)))
        dbf_ref[...] = tot

    return pl.pallas_call(body, name="fgate_bwd",
                          out_shape=[jax.ShapeDtypeStruct((n, S), F32), jax.ShapeDtypeStruct((n, 1), F32)])(dc, lf)


def _tri_matrices(b):
    r = np.arange(b)
    tfwd = (r[:, None] <= r[None, :]).astype(np.float32)
    return jnp.asarray(tfwd, BF16), jnp.asarray(tfwd.T, BF16)


def _kv_copies(kv_hbm, kbuf, vbuf, sems, pair_col, bq, j, slot):
    rows = pl.ds(pl.multiple_of(j * bq, bq), bq)

    def cols(c):
        return pl.ds(pl.multiple_of((pair_col + c) * LANES, LANES), LANES)

    return (pltpu.make_async_copy(kv_hbm.at[rows, cols(4)], kbuf.at[slot], sems.at[0, slot]),
            pltpu.make_async_copy(kv_hbm.at[rows, cols(8)], vbuf.at[slot], sems.at[1, slot]))


def _kv_fetcher(kv_hbm, kbuf, vbuf, sems, col0, bq, p, i, j0):
    def fetch(j, slot, pair=p):
        return _kv_copies(kv_hbm, kbuf, vbuf, sems, col0 + pair, bq, j, slot)

    @pl.when(jnp.logical_and(p == 0, i == 0))
    def _():
        for cp in fetch(j0, 0):
            cp.start()

    return fetch


def _prefetch_next(fetch, p, i, nq, first_block):
    wrap = i == nq - 1

    @pl.when(jnp.logical_not(jnp.logical_and(wrap, p == N_PAIRS - 1)))
    def _():
        pair, blk = jnp.where(wrap, p + 1, p), jnp.where(wrap, 0, i + 1)
        for cp in fetch(first_block(pair, blk), 0, pair):
            cp.start()


def _masked_pair(v, lane_is_a, scale=1.0):
    v = v.astype(F32) * scale
    return jnp.where(lane_is_a, v, 0.0).astype(BF16), jnp.where(lane_is_a, 0.0, v).astype(BF16)


def _sb_fwd(proj, col0, bq):
    S = proj.shape[0]
    bq = min(bq, S)
    nq = S // bq
    _, trev = _tri_matrices(bq)

    def body(q_ref, kv_hbm, trev_ref, o_ref, st_ref, jmin_ref, acc_a, acc_b, qa, qb, rs, kbuf, vbuf, sems):
        p, i = pl.program_id(0), pl.program_id(1)
        fetch = _kv_fetcher(kv_hbm, kbuf, vbuf, sems, col0, bq, p, i, i)
        is_a = lax.broadcasted_iota(jnp.int32, (bq, LANES), 1) < HEAD_DIM
        acc_a[...] = jnp.zeros_like(acc_a)
        acc_b[...] = jnp.zeros_like(acc_b)
        rs[...] = jnp.zeros_like(rs)
        qa[...], qb[...] = _masked_pair(q_ref[...], is_a, SCALE)

        def tile(slot, masked):
            k, v, trev_m = kbuf[slot], vbuf[slot], trev_ref[...]
            if masked:
                tri = lax.broadcasted_iota(jnp.int32, (bq, bq), 0) > lax.broadcasted_iota(jnp.int32, (bq, bq), 1)
            hs, qs, accs = (0, 1), (qa, qb), (acc_a, acc_b)
            z = {h: _dot(qs[h][...], k, _NT) for h in hs}
            lk = {h: -_softplus(z[h]) for h in hs}
            if masked:
                lk = {h: jnp.where(tri, lk[h], 0.0) for h in hs}
            suf = {h: _split_dot(lk[h], trev_m) for h in hs}
            w = {h: jnp.exp(z[h] + suf[h] + (rs[2 * h] + rs[2 * h + 1])) for h in hs}
            if masked:
                w = {h: jnp.where(tri, w[h], 0.0) for h in hs}
            tot = {h: jnp.sum(lk[h], axis=1, keepdims=True) for h in hs}
            pv = {h: _dot(w[h].astype(BF16), v) for h in hs}
            for h in hs:
                accs[h][...] += pv[h]
                rs[2 * h], rs[2 * h + 1] = _two_sum(rs[2 * h], rs[2 * h + 1], tot[h])

        def step(carry):
            j, _ = carry
            slot = lax.rem(i - j, 2)
            for cp in fetch(j, slot):
                cp.wait()

            @pl.when(j > 0)
            def _():
                for cp in fetch(j - 1, 1 - slot):
                    cp.start()

            pl.when(j == i)(functools.partial(tile, slot, True))
            pl.when(j < i)(functools.partial(tile, slot, False))
            live = jnp.max(jnp.maximum(rs[0], rs[2])) > SB_STOP
            return j - 1, live.astype(jnp.int32)

        j_end, _ = lax.while_loop(lambda c: jnp.logical_and(c[0] >= 0, c[1] > 0), step, (i, jnp.int32(1)))

        @pl.when(j_end >= 0)
        def _():
            for cp in fetch(j_end, lax.rem(i - j_end, 2)):
                cp.wait()

        _prefetch_next(fetch, p, i, nq, lambda pair, blk: blk)
        jmin_ref[p, i] = j_end + 1
        o_ref[...] = jnp.where(is_a, acc_a[...], acc_b[...])
        lane8 = lax.broadcasted_iota(jnp.int32, (bq, 8), 1)
        st = jnp.zeros((bq, 8), F32)
        for c, src in enumerate((0, 2, 1, 3)):
            st = jnp.where(lane8 == c, rs[src], st)
        st_ref[0] = st

    return pl.pallas_call(
        body, name="sb_fwd", grid=(N_PAIRS, nq),
        in_specs=[pl.BlockSpec((bq, LANES), lambda p, i: (i, col0 + p)),
                  pl.BlockSpec(memory_space=pl.ANY),
                  pl.BlockSpec((bq, bq), lambda p, i: (0, 0))],
        out_specs=[pl.BlockSpec((bq, LANES), lambda p, i: (i, p)),
                   pl.BlockSpec((1, bq, 8), lambda p, i: (p, i, 0)),
                   pl.BlockSpec(memory_space=pltpu.SMEM)],
        out_shape=[jax.ShapeDtypeStruct((S, GROUP_W), F32), jax.ShapeDtypeStruct((N_PAIRS, S, 8), F32),
                   jax.ShapeDtypeStruct((N_PAIRS, nq), jnp.int32)],
        scratch_shapes=[pltpu.VMEM((bq, LANES), F32), pltpu.VMEM((bq, LANES), F32),
                        pltpu.VMEM((bq, LANES), BF16), pltpu.VMEM((bq, LANES), BF16),
                        pltpu.VMEM((4, bq, 1), F32),
                        pltpu.VMEM((2, bq, LANES), BF16), pltpu.VMEM((2, bq, LANES), BF16),
                        pltpu.SemaphoreType.DMA((2, 2))],
    )(proj, proj, trev)


def _sb_bwd(proj, col0, do, st, jmin, bq):
    S = proj.shape[0]
    bq = min(bq, S)
    nq = S // bq
    tfwd, trev = _tri_matrices(bq)

    def body(jmin_ref, q_ref, kv_hbm, do_ref, st_ref, tfwd_ref, trev_ref,
             dq_ref, dk_ref, dv_ref, dq_a, dq_b, qa, qb, doa, dob, rs, kbuf, vbuf, sems):
        p, i = pl.program_id(0), pl.program_id(1)
        j0 = jmin_ref[p, i]
        fetch = _kv_fetcher(kv_hbm, kbuf, vbuf, sems, col0, bq, p, i, j0)
        is_a = lax.broadcasted_iota(jnp.int32, (bq, LANES), 1) < HEAD_DIM

        @pl.when(i == 0)
        def _():
            dk_ref[...] = jnp.zeros_like(dk_ref)
            dv_ref[...] = jnp.zeros_like(dv_ref)

        dq_a[...] = jnp.zeros_like(dq_a)
        dq_b[...] = jnp.zeros_like(dq_b)
        rs[...] = jnp.zeros_like(rs)
        st_v = st_ref[0]
        for h in range(2):
            rs[6 + 2 * h], rs[7 + 2 * h] = _col(st_v, h), _col(st_v, 2 + h)
        qa[...], qb[...] = _masked_pair(q_ref[...], is_a, SCALE)
        doa[...], dob[...] = _masked_pair(do_ref[...], is_a)

        def tile(j, slot, masked):
            k, v = kbuf[slot], vbuf[slot]
            tfwd_m, trev_m = tfwd_ref[...], trev_ref[...]
            if masked:
                tri = lax.broadcasted_iota(jnp.int32, (bq, bq), 0) > lax.broadcasted_iota(jnp.int32, (bq, bq), 1)
            hs, qs, dos, dqs = (0, 1), (qa, qb), (doa, dob), (dq_a, dq_b)
            z = {h: _dot(qs[h][...], k, _NT) for h in hs}
            lk = {h: -_softplus(z[h]) for h in hs}
            if masked:
                lk = {h: jnp.where(tri, lk[h], 0.0) for h in hs}
            suf = {h: _split_dot(lk[h], trev_m) for h in hs}
            dw = {h: _dot(dos[h][...], v, _NT) for h in hs}
            pre = {h: _two_sum(rs[3 * h], rs[3 * h + 1], jnp.sum(lk[h], axis=1, keepdims=True)) for h in hs}
            right = {h: (rs[6 + 2 * h] - pre[h][0]) + (rs[7 + 2 * h] - pre[h][1]) for h in hs}
            w = {h: jnp.exp(z[h] + suf[h] + right[h]) for h in hs}
            if masked:
                w = {h: jnp.where(tri, w[h], 0.0) for h in hs}
            g = {h: dw[h] * w[h] for h in hs}
            gpre = {h: _split_dot(g[h], tfwd_m) for h in hs}
            dz = {h: g[h] - jnp.exp(z[h] + lk[h]) * (gpre[h] + rs[3 * h + 2]) for h in hs}
            if masked:
                dz = {h: jnp.where(tri, dz[h], 0.0) for h in hs}
            gtot = {h: jnp.sum(g[h], axis=1, keepdims=True) for h in hs}
            dzb = {h: dz[h].astype(BF16) for h in hs}
            wb = {h: w[h].astype(BF16) for h in hs}
            dqc = {h: _dot(dzb[h], k) for h in hs}
            dkc = {h: _dot(dzb[h], qs[h][...], _TN) for h in hs}
            dvc = {h: _dot(wb[h], dos[h][...], _TN) for h in hs}
            for h in hs:
                rs[3 * h], rs[3 * h + 1] = pre[h]
                rs[3 * h + 2] += gtot[h]
                dqs[h][...] += dqc[h]
            rows = pl.ds(pl.multiple_of(j * bq, bq), bq)
            dk_ref[rows, :] += dkc[0] + dkc[1]
            dv_ref[rows, :] += dvc[0] + dvc[1]

        _walk_up(fetch, j0, i, tile)
        _prefetch_next(fetch, p, i, nq, lambda pair, blk: jmin_ref[pair, blk])
        dq_ref[...] = jnp.where(is_a, dq_a[...], dq_b[...]) * SCALE

    grid_spec = pltpu.PrefetchScalarGridSpec(
        num_scalar_prefetch=1, grid=(N_PAIRS, nq),
        in_specs=[pl.BlockSpec((bq, LANES), lambda p, i, jm: (i, col0 + p)),
                  pl.BlockSpec(memory_space=pl.ANY),
                  pl.BlockSpec((bq, LANES), lambda p, i, jm: (i, p)),
                  pl.BlockSpec((1, bq, 8), lambda p, i, jm: (p, i, 0)),
                  pl.BlockSpec((bq, bq), lambda p, i, jm: (0, 0)),
                  pl.BlockSpec((bq, bq), lambda p, i, jm: (0, 0))],
        out_specs=[pl.BlockSpec((bq, LANES), lambda p, i, jm: (i, p)),
                   pl.BlockSpec((S, LANES), lambda p, i, jm: (0, p)),
                   pl.BlockSpec((S, LANES), lambda p, i, jm: (0, p))],
        scratch_shapes=[pltpu.VMEM((bq, LANES), F32), pltpu.VMEM((bq, LANES), F32)]
        + [pltpu.VMEM((bq, LANES), BF16)] * 4 + [pltpu.VMEM((10, bq, 1), F32)]
        + [pltpu.VMEM((2, bq, LANES), BF16), pltpu.VMEM((2, bq, LANES), BF16), pltpu.SemaphoreType.DMA((2, 2))])
    return pl.pallas_call(
        body, name="sb_bwd", grid_spec=grid_spec,
        out_shape=[jax.ShapeDtypeStruct((S, GROUP_W), F32)] * 3,
        compiler_params=_params(VMEM_BIG),
    )(jmin, proj, proj, do, st, tfwd, trev)


def _walk_up(fetch, j0, i, tile):
    def step(j, carry):
        slot = lax.rem(j - j0, 2)
        for cp in fetch(j, slot):
            cp.wait()

        @pl.when(j < i)
        def _():
            for cp in fetch(j + 1, 1 - slot):
                cp.start()

        pl.when(j == i)(functools.partial(tile, j, slot, True))
        pl.when(j < i)(functools.partial(tile, j, slot, False))
        return carry

    lax.fori_loop(j0, i + 1, step, 0)


def _by_heads(j, first_a, first_b, heads):
    on_a, on_b = j >= first_a, j >= first_b
    pl.when(jnp.logical_and(on_a, on_b))(functools.partial(heads, (0, 1)))
    pl.when(jnp.logical_and(on_a, jnp.logical_not(on_b)))(functools.partial(heads, (0,)))
    pl.when(jnp.logical_and(on_b, jnp.logical_not(on_a)))(functools.partial(heads, (1,)))


def _fox_start_blocks(proj, col0, c, bq):
    S = proj.shape[0]
    nq = S // bq
    nh = 2 * N_PAIRS

    def heads(first):
        return proj[:, first * LANES:(first + N_PAIRS) * LANES].astype(F32).reshape(S, nh, HEAD_DIM)

    q, k = heads(col0), heads(col0 + 4)
    qn = jnp.sqrt(jnp.sum(q * q, axis=-1))
    kmax = jnp.sqrt(jnp.sum(k * k, axis=-1)).max(axis=0)
    top = SCALE * (qn * kmax[None, :] - jnp.sum(q * k, axis=-1)) + c.T
    top = top.reshape(nq, bq, nh).max(axis=1)
    c_last = c[:, bq - 1::bq].T
    live = top[:, None, :] - c_last[None, :, :] >= -FOX_SKIP

    def first_block(lv):
        first = jnp.where(lv.any(axis=1), jnp.argmax(lv, axis=1), nq)
        return jnp.minimum(first, jnp.arange(nq)[:, None]).T.astype(jnp.int32)

    return jnp.concatenate([first_block(live.reshape(nq, nq, N_PAIRS, 2).any(axis=-1)), first_block(live)], axis=0)


def _fox_fwd(proj, col0, c_col, c_row, jstart, bq):
    S = proj.shape[0]
    bq = min(bq, S)
    nq = S // bq

    def body(js_ref, q_ref, kv_hbm, cc_ref, cr_ref, o_ref, st_ref, acc_a, acc_b, qa, qb, ml, kbuf, vbuf, sems):
        p, i = pl.program_id(0), pl.program_id(1)
        j0 = js_ref[p, i]
        fetch = _kv_fetcher(kv_hbm, kbuf, vbuf, sems, col0, bq, p, i, j0)
        is_a = lax.broadcasted_iota(jnp.int32, (bq, LANES), 1) < HEAD_DIM
        acc_a[...] = jnp.zeros_like(acc_a)
        acc_b[...] = jnp.zeros_like(acc_b)
        ml[0] = jnp.full((bq, 1), NEG_BIG, F32)
        ml[2] = jnp.full((bq, 1), NEG_BIG, F32)
        ml[1] = jnp.zeros((bq, 1), F32)
        ml[3] = jnp.zeros((bq, 1), F32)
        cc = cc_ref[0]
        ml[4], ml[5] = _col(cc, 0), _col(cc, 1)
        qa[...], qb[...] = _masked_pair(q_ref[...], is_a, SCALE)

        def tile(j, slot, masked):
            k, v = kbuf[slot], vbuf[slot]
            cols = pl.ds(pl.multiple_of(j * bq, bq), bq)
            if masked:
                tri = lax.broadcasted_iota(jnp.int32, (bq, bq), 0) >= lax.broadcasted_iota(jnp.int32, (bq, bq), 1)
            def heads(hs):
                qs, accs = (qa, qb), (acc_a, acc_b)
                s = {h: _dot(qs[h][...], k, _NT) - cr_ref[0, pl.ds(h, 1), cols] for h in hs}
                if masked:
                    s = {h: jnp.where(tri, s[h], NEG_BIG) for h in hs}
                top = {h: jnp.max(s[h], axis=1, keepdims=True) for h in hs}
                m_new = {h: jnp.maximum(ml[2 * h], top[h] + ml[4 + h]) for h in hs}
                a = {h: jnp.exp(ml[2 * h] - m_new[h]) for h in hs}
                pr = {h: jnp.exp(s[h] - (m_new[h] - ml[4 + h])) for h in hs}
                tot = {h: jnp.sum(pr[h], axis=1, keepdims=True) for h in hs}
                pv = {h: _dot(pr[h].astype(BF16), v) for h in hs}
                for h in hs:
                    ml[2 * h] = m_new[h]
                    ml[2 * h + 1] = a[h] * ml[2 * h + 1] + tot[h]
                    accs[h][...] = a[h] * accs[h][...] + pv[h]

            _by_heads(j, js_ref[N_PAIRS + 2 * p, i], js_ref[N_PAIRS + 2 * p + 1, i], heads)

        _walk_up(fetch, j0, i, tile)
        _prefetch_next(fetch, p, i, nq, lambda pair, blk: js_ref[pair, blk])
        o_ref[...] = jnp.where(is_a, acc_a[...] / ml[1], acc_b[...] / ml[3])
        lane8 = lax.broadcasted_iota(jnp.int32, (bq, 8), 1)
        st = jnp.where(lane8 == 0, ml[0] + jnp.log(ml[1]), 0.0)
        st_ref[0] = jnp.where(lane8 == 1, ml[2] + jnp.log(ml[3]), st)

    grid_spec = pltpu.PrefetchScalarGridSpec(
        num_scalar_prefetch=1, grid=(N_PAIRS, nq),
        in_specs=[pl.BlockSpec((bq, LANES), lambda p, i, js: (i, col0 + p)),
                  pl.BlockSpec(memory_space=pl.ANY),
                  pl.BlockSpec((1, bq, 8), lambda p, i, js: (p, i, 0)),
                  pl.BlockSpec((1, 8, S), lambda p, i, js: (p, 0, 0))],
        out_specs=[pl.BlockSpec((bq, LANES), lambda p, i, js: (i, p)),
                   pl.BlockSpec((1, bq, 8), lambda p, i, js: (p, i, 0))],
        scratch_shapes=[pltpu.VMEM((bq, LANES), F32), pltpu.VMEM((bq, LANES), F32),
                        pltpu.VMEM((bq, LANES), BF16), pltpu.VMEM((bq, LANES), BF16),
                        pltpu.VMEM((6, bq, 1), F32),
                        pltpu.VMEM((2, bq, LANES), BF16), pltpu.VMEM((2, bq, LANES), BF16),
                        pltpu.SemaphoreType.DMA((2, 2))])
    return pl.pallas_call(
        body, name="fox_fwd", grid_spec=grid_spec,
        out_shape=[jax.ShapeDtypeStruct((S, GROUP_W), F32), jax.ShapeDtypeStruct((N_PAIRS, S, 8), F32)],
    )(jstart, proj, proj, c_col, c_row)


def _fox_bwd(proj, col0, do, o, st, c_col, c_row, jstart, bq):
    S = proj.shape[0]
    bq = min(bq, S)
    nq = S // bq

    def body(js_ref, q_ref, kv_hbm, do_ref, o_ref, st_ref, cc_ref, cr_ref,
             dq_ref, dk_ref, dv_ref, dc_ref, dcq_ref, dq_a, dq_b, qa, qb, doa, dob, dd, kbuf, vbuf, sems):
        p, i = pl.program_id(0), pl.program_id(1)
        j0 = js_ref[p, i]
        fetch = _kv_fetcher(kv_hbm, kbuf, vbuf, sems, col0, bq, p, i, j0)
        is_a = lax.broadcasted_iota(jnp.int32, (bq, LANES), 1) < HEAD_DIM

        @pl.when(i == 0)
        def _():
            dk_ref[...] = jnp.zeros_like(dk_ref)
            dv_ref[...] = jnp.zeros_like(dv_ref)
            dc_ref[...] = jnp.zeros_like(dc_ref)

        dq_a[...] = jnp.zeros_like(dq_a)
        dq_b[...] = jnp.zeros_like(dq_b)
        qa[...], qb[...] = _masked_pair(q_ref[...], is_a, SCALE)
        dov = do_ref[...]
        doa[...], dob[...] = _masked_pair(dov, is_a)
        prod = dov * o_ref[...]
        dd[0] = jnp.sum(jnp.where(is_a, prod, 0.0), axis=1, keepdims=True)
        dd[1] = jnp.sum(jnp.where(is_a, 0.0, prod), axis=1, keepdims=True)
        dd[2] = jnp.zeros((bq, 1), F32)
        dd[3] = jnp.zeros((bq, 1), F32)
        cc, st_v = cc_ref[0], st_ref[0]
        dd[4], dd[5] = _col(cc, 0) - _col(st_v, 0), _col(cc, 1) - _col(st_v, 1)

        def tile(j, slot, masked):
            k, v = kbuf[slot], vbuf[slot]
            if masked:
                tri = lax.broadcasted_iota(jnp.int32, (bq, bq), 0) >= lax.broadcasted_iota(jnp.int32, (bq, bq), 1)
            cols = pl.ds(pl.multiple_of(j * bq, bq), bq)

            def heads(hs):
                qs, dos, dqs = (qa, qb), (doa, dob), (dq_a, dq_b)
                z = {h: _dot(qs[h][...], k, _NT) for h in hs}
                dp = {h: _dot(dos[h][...], v, _NT) for h in hs}
                pr = {h: jnp.exp(z[h] - cr_ref[0, pl.ds(h, 1), cols] + dd[4 + h]) for h in hs}
                if masked:
                    pr = {h: jnp.where(tri, pr[h], 0.0) for h in hs}
                ds = {h: pr[h] * (dp[h] - dd[h]) for h in hs}
                csum = {h: jnp.sum(ds[h], axis=0, keepdims=True) for h in hs}
                rsum = {h: jnp.sum(ds[h], axis=1, keepdims=True) for h in hs}
                dsb = {h: ds[h].astype(BF16) for h in hs}
                prb = {h: pr[h].astype(BF16) for h in hs}
                dqc = {h: _dot(dsb[h], k) for h in hs}
                dkc = [_dot(dsb[h], qs[h][...], _TN) for h in hs]
                dvc = [_dot(prb[h], dos[h][...], _TN) for h in hs]
                for h in hs:
                    dc_ref[0, pl.ds(h, 1), cols] -= csum[h]
                    dd[2 + h] += rsum[h]
                    dqs[h][...] += dqc[h]
                dk_ref[cols, :] += sum(dkc[1:], dkc[0])
                dv_ref[cols, :] += sum(dvc[1:], dvc[0])

            _by_heads(j, js_ref[N_PAIRS + 2 * p, i], js_ref[N_PAIRS + 2 * p + 1, i], heads)

        _walk_up(fetch, j0, i, tile)
        _prefetch_next(fetch, p, i, nq, lambda pair, blk: js_ref[pair, blk])
        dq_ref[...] = jnp.where(is_a, dq_a[...], dq_b[...]) * SCALE
        lane8 = lax.broadcasted_iota(jnp.int32, (bq, 8), 1)
        dcq_ref[0] = jnp.where(lane8 == 0, dd[2], jnp.where(lane8 == 1, dd[3], 0.0))

    grid_spec = pltpu.PrefetchScalarGridSpec(
        num_scalar_prefetch=1, grid=(N_PAIRS, nq),
        in_specs=[pl.BlockSpec((bq, LANES), lambda p, i, js: (i, col0 + p)),
                  pl.BlockSpec(memory_space=pl.ANY),
                  pl.BlockSpec((bq, LANES), lambda p, i, js: (i, p)),
                  pl.BlockSpec((bq, LANES), lambda p, i, js: (i, p)),
                  pl.BlockSpec((1, bq, 8), lambda p, i, js: (p, i, 0)),
                  pl.BlockSpec((1, bq, 8), lambda p, i, js: (p, i, 0)),
                  pl.BlockSpec((1, 8, S), lambda p, i, js: (p, 0, 0))],
        out_specs=[pl.BlockSpec((bq, LANES), lambda p, i, js: (i, p)),
                   pl.BlockSpec((S, LANES), lambda p, i, js: (0, p)),
                   pl.BlockSpec((S, LANES), lambda p, i, js: (0, p)),
                   pl.BlockSpec((1, 8, S), lambda p, i, js: (p, 0, 0)),
                   pl.BlockSpec((1, bq, 8), lambda p, i, js: (p, i, 0))],
        scratch_shapes=[pltpu.VMEM((bq, LANES), F32), pltpu.VMEM((bq, LANES), F32)]
        + [pltpu.VMEM((bq, LANES), BF16)] * 4 + [pltpu.VMEM((6, bq, 1), F32)]
        + [pltpu.VMEM((2, bq, LANES), BF16), pltpu.VMEM((2, bq, LANES), BF16), pltpu.SemaphoreType.DMA((2, 2))])
    return pl.pallas_call(
        body, name="fox_bwd", grid_spec=grid_spec,
        out_shape=[jax.ShapeDtypeStruct((S, GROUP_W), F32)] * 3
        + [jax.ShapeDtypeStruct((N_PAIRS, 8, S), F32), jax.ShapeDtypeStruct((N_PAIRS, S, 8), F32)],
        compiler_params=_params(VMEM_BIG),
    )(jstart, proj, proj, do, o, st, c_col, c_row)


_HBM = pl.BlockSpec(memory_space=pltpu.HBM)


def _coords():
    return lax.axis_index("x"), lax.axis_index("y"), lax.axis_index("c")


def _gather_copies(ins, outs, send_sems, recv_sems, loc_sems):
    n = len(ins)
    x, y, c = _coords()
    mine = 2 * x + y
    chips = [(1 - x, y), (x, 1 - y), (1 - x, 1 - y)]

    def copy(w, r, slab, to):
        return pltpu.make_async_remote_copy(
            src_ref=ins[w], dst_ref=outs[w].at[slab], send_sem=send_sems.at[3 * w + r],
            recv_sem=recv_sems.at[3 * w + r], device_id=to, device_id_type=MESH)

    def own():
        local = [pltpu.make_async_copy(ins[w], outs[w].at[mine], loc_sems.at[w]) for w in range(n)]
        return local, [copy(w, r, mine, (cx, cy, c)) for w in range(n) for r, (cx, cy) in enumerate(chips)]

    def start():
        local, sends = own()
        for cp in local + sends:
            cp.start()

    def wait():
        local, sends = own()
        for w in range(n):
            for r, (cx, cy) in enumerate(chips):
                copy(w, r, 2 * cx + cy, (cx, cy, c)).wait_recv()
        for cp in sends:
            cp.wait_send()
        for cp in local:
            cp.wait()

    return start, wait


def _gather_shapes(shards):
    n = len(shards)
    return ([jax.ShapeDtypeStruct((4,) + s.shape, s.dtype) for s in shards],
            [pltpu.SemaphoreType.DMA((3 * n,)), pltpu.SemaphoreType.DMA((3 * n,)), pltpu.SemaphoreType.DMA((n,))])


def _allgather_chips(shards):
    n = len(shards)

    def body(*refs):
        start, wait = _gather_copies(refs[:n], refs[n:2 * n], *refs[2 * n:])
        start()
        wait()

    out_shape, sems = _gather_shapes(shards)
    return pl.pallas_call(body, name="allgather_weights", in_specs=[_HBM] * n, out_specs=[_HBM] * n,
                          out_shape=out_shape, scratch_shapes=sems)(*shards)


def _proj_gather(x, w, shards, tm, tn):
    (M, K), N, n = x.shape, w.shape[1], len(shards)
    tm = min(tm, M)
    gi, gj = M // tm, N // tn

    def body(a_ref, b_ref, *rest):
        o_ref = rest[n]
        start, wait = _gather_copies(rest[:n], rest[n + 1:2 * n + 1], *rest[2 * n + 1:])
        i, j = pl.program_id(0), pl.program_id(1)
        pl.when(jnp.logical_and(i == 0, j == 0))(start)
        o_ref[...] = _dot(a_ref[...].astype(BF16), b_ref[...]).astype(o_ref.dtype)
        pl.when(jnp.logical_and(i == gi - 1, j == gj - 1))(wait)

    out_shape, sems = _gather_shapes(shards)
    return pl.pallas_call(
        body, name="proj_gather", grid=(gi, gj),
        in_specs=[pl.BlockSpec((tm, K), lambda i, j: (i, 0)), pl.BlockSpec((K, tn), lambda i, j: (0, j))] + [_HBM] * n,
        out_specs=[pl.BlockSpec((tm, tn), lambda i, j: (i, j))] + [_HBM] * n,
        out_shape=[jax.ShapeDtypeStruct((M, N), BF16)] + out_shape, scratch_shapes=sems,
    )(x, w, *shards)


def _exchange(parts, per_chip):
    n = len(parts)
    half = [p.shape[1] // 2 for p in parts] if per_chip else None

    def body(*refs):
        ins, outs = refs[:n], refs[n:2 * n]
        send_sems, recv_sems, loc_sems = refs[2 * n:]
        x, y, c = _coords()
        me = 4 * x + 2 * y + c
        peers = [(x ^ fx, y ^ fy, c ^ fc) for fx in (0, 1) for fy in (0, 1) for fc in (0, 1)][1:]

        def src(w, dev):
            if not per_chip:
                return ins[w]
            return ins[w].at[2 * dev[0] + dev[1], pl.ds(pl.multiple_of(dev[2] * half[w], 16), half[w]), :]

        local = [pltpu.make_async_copy(src(w, (x, y, c)), outs[w].at[me], loc_sems.at[w]) for w in range(n)]
        for cp in local:
            cp.start()

        def copy(w, r, source, slab, to):
            return pltpu.make_async_remote_copy(
                src_ref=source, dst_ref=outs[w].at[slab], send_sem=send_sems.at[7 * w + r],
                recv_sem=recv_sems.at[7 * w + r], device_id=to, device_id_type=MESH)

        sends = [copy(w, r, src(w, dev), me, dev) for w in range(n) for r, dev in enumerate(peers)]
        for cp in sends:
            cp.start()
        for w in range(n):
            for r, dev in enumerate(peers):
                copy(w, r, src(w, dev), 4 * dev[0] + 2 * dev[1] + dev[2], dev).wait_recv()
        for cp in sends:
            cp.wait_send()
        for cp in local:
            cp.wait()

    return pl.pallas_call(
        body, name="exchange_per_chip" if per_chip else "exchange_all",
        in_specs=[_HBM] * n, out_specs=[_HBM] * n,
        out_shape=[jax.ShapeDtypeStruct((8, half[w], p.shape[2]) if per_chip else (8,) + p.shape, p.dtype)
                   for w, p in enumerate(parts)],
        scratch_shapes=[pltpu.SemaphoreType.DMA((7 * n,)), pltpu.SemaphoreType.DMA((7 * n,)),
                        pltpu.SemaphoreType.DMA((n,))],
    )(*parts)


def _sibling_swap(halves):
    n = len(halves)

    def body(*refs):
        ins, outs = refs[:n], refs[n:2 * n]
        send_sems, recv_sems, loc_sems = refs[2 * n:]
        x, y, c = _coords()

        def rows(w, core):
            rh = halves[w].shape[0]
            return outs[w].at[pl.ds(pl.multiple_of(core * rh, 8), rh), :]

        def copy(w, core):
            return pltpu.make_async_remote_copy(
                src_ref=ins[w], dst_ref=rows(w, core), send_sem=send_sems.at[w], recv_sem=recv_sems.at[w],
                device_id=(x, y, 1 - c), device_id_type=MESH)

        local = [pltpu.make_async_copy(ins[w], rows(w, c), loc_sems.at[w]) for w in range(n)]
        sends = [copy(w, c) for w in range(n)]
        for cp in local + sends:
            cp.start()
        for w in range(n):
            copy(w, 1 - c).wait_recv()
        for cp in sends:
            cp.wait_send()
        for cp in local:
            cp.wait()

    vmem = pl.BlockSpec(memory_space=pltpu.VMEM)
    return pl.pallas_call(
        body, name="sibling_swap", in_specs=[vmem] * n, out_specs=[vmem] * n,
        out_shape=[jax.ShapeDtypeStruct((2 * h.shape[0], h.shape[1]), h.dtype) for h in halves],
        scratch_shapes=[pltpu.SemaphoreType.DMA((n,)), pltpu.SemaphoreType.DMA((n,)), pltpu.SemaphoreType.DMA((n,))],
    )(*halves)


def _adamw(w, g, m, v):
    m = ADAM_B1 * m + (1.0 - ADAM_B1) * g
    v = ADAM_B2 * v + (1.0 - ADAM_B2) * (g * g)
    m_hat = m / (1.0 - ADAM_B1 ** ADAM_STEP)
    v_hat = v / (1.0 - ADAM_B2 ** ADAM_STEP)
    delta = -ADAM_LR * (m_hat / (jnp.sqrt(v_hat) + ADAM_EPS) + ADAM_WD * w)
    return delta, m, v


def _sum_parts(parts, name, tr):
    _, R, C = parts.shape
    assert R % tr == 0

    def body(p_ref, g_ref):
        g = p_ref[0].astype(F32)
        for d in range(1, 8):
            g = g + p_ref[d].astype(F32)
        g_ref[...] = g

    return pl.pallas_call(
        body, name=name, grid=(R // tr,),
        in_specs=[pl.BlockSpec((8, tr, C), lambda i: (0, i, 0))],
        out_specs=pl.BlockSpec((tr, C), lambda i: (i, 0)), out_shape=jax.ShapeDtypeStruct((R, C), F32),
    )(parts)


def _adamw_call(g, w, m, v, name, tr):
    R, C = w.shape
    assert R % tr == 0

    def body(g_ref, w_ref, m_ref, v_ref, d_ref, nm_ref, nv_ref):
        d_ref[...], nm_ref[...], nv_ref[...] = _adamw(w_ref[...], g_ref[...], m_ref[...], v_ref[...])

    tile = pl.BlockSpec((tr, C), lambda i: (i, 0))
    return pl.pallas_call(
        body, name=name, grid=(R // tr,), in_specs=[tile] * 4,
        out_specs=[tile] * 3, out_shape=[jax.ShapeDtypeStruct((R, C), F32)] * 3,
    )(g, w, m, v)


def _sum_adamw_small(parts, w, m, v):
    def body(p_ref, w_ref, m_ref, v_ref, g_ref, d_ref, nm_ref, nv_ref, loss_ref):
        g = p_ref[0]
        for d in range(1, 8):
            g = g + p_ref[d]
        g_ref[...] = g
        d_ref[...], nm_ref[...], nv_ref[...] = _adamw(w_ref[...], g, m_ref[...], v_ref[...])
        row = lax.broadcasted_iota(jnp.int32, g.shape, 0)
        per_row = jnp.sum(jnp.where(row == 6, g, 0.0), axis=1, keepdims=True)
        loss_ref[...] = jnp.zeros((8, LANES), F32) + jnp.sum(per_row, axis=0, keepdims=True)

    return pl.pallas_call(
        body, name="sum_adamw_small",
        out_shape=[jax.ShapeDtypeStruct((8, D_MODEL), F32)] * 4 + [jax.ShapeDtypeStruct((8, LANES), F32)],
    )(parts, w, m, v)


def _pack_small(ln1_g, ln1_b, ln2_g, ln2_b, g_sb, g_fox, b_f):
    row5 = jnp.pad(b_f.reshape(1, N_FOX), ((0, 0), (0, D_MODEL - N_FOX)))
    rows = [ln1_g.reshape(1, -1), ln1_b.reshape(1, -1), ln2_g.reshape(1, -1), ln2_b.reshape(1, -1),
            jnp.concatenate([g_sb.reshape(1, -1), g_fox.reshape(1, -1)], axis=1), row5,
            jnp.zeros((2, D_MODEL), F32)]
    return jnp.concatenate(rows, axis=0)


def _unpack_small(p):
    return {"ln1_g": p[0:1], "ln1_b": p[1:2], "ln2_g": p[2:3], "ln2_b": p[3:4], "g_sb": p[4:5, :GROUP_W],
            "g_fox": p[4:5, GROUP_W:], "b_f": p[5:6, :N_FOX]}


def kernel(x, w_in, b_f, g_sb, g_fox, w_out, ln1_g, ln1_b, ln2_g, ln2_b, w_gate_up, w_down, loss_target, m_w_in, m_b_f, m_g_sb, m_g_fox, m_w_out, m_ln1_g, m_ln1_b, m_ln2_g, m_ln2_b, m_w_gate_up, m_w_down, v_w_in, v_b_f, v_g_sb, v_g_fox, v_w_out, v_ln1_g, v_ln1_b, v_ln2_g, v_ln2_b, v_w_gate_up, v_w_down):
    S = x.shape[1]
    x2 = x.reshape(S, D_MODEL)
    tgt = loss_target.reshape(S, D_MODEL)
    TM = 1024
    TR = 512
    BQ = ATTN_BLOCK
    in_w = w_in.shape[2]
    gu_w = w_gate_up.shape[2]

    shards = [w_in[0].astype(BF16), w_out[0].astype(BF16), w_gate_up[0].astype(BF16), w_down[0].astype(BF16)]
    (wi_s,) = _allgather_chips(shards[:1])
    wi = wi_s.transpose(1, 0, 2).reshape(D_MODEL, 4 * in_w)
    w_sb, w_fx = wi[:, :QKV_W // 2], wi[:, QKV_W // 2:QKV_W]
    wqkv = wi[:, :QKV_W]
    wft = wi[:, QKV_W:].T
    proj, wo_s, wgu_s, wd_s = _proj_gather(x2, wqkv, shards[1:], TM, 512)
    wo = wo_s.reshape(D_MODEL, D_MODEL)
    wgu = wgu_s.transpose(1, 0, 2).reshape(D_MODEL, 2 * D_FF)
    wg, wu = wgu[:, :D_FF], wgu[:, D_FF:]
    wd = wd_s.reshape(D_FF, D_MODEL)
    g_row = jnp.concatenate([g_sb, g_fox], axis=1)
    hid = np.arange(D_MODEL) // HEAD_DIM
    he_np = (hid[:, None] == np.arange(LANES)[None, :]).astype(np.float32)
    he, het = jnp.asarray(he_np, BF16), jnp.asarray(he_np.T, BF16)

    lf = _fgate_fwd(x2, wft, b_f.reshape(N_FOX, 1), TM)
    c = _cumsum_fwd(lf)
    c_pair = c.reshape(N_PAIRS, 2, S)
    c_row = jnp.pad(c_pair, ((0, 0), (0, 6), (0, 0)))
    c_col = jnp.pad(c_pair.transpose(0, 2, 1), ((0, 0), (0, 0), (0, 6)))

    o_sb, st_sb, jmin_sb = _sb_fwd(proj, 0, BQ)
    jstart_fx = _fox_start_blocks(proj, 12, c, min(BQ, S))
    o_fx, st_fx = _fox_fwd(proj, 12, c_col, c_row, jstart_fx, BQ)

    def attn_post(i, osb_ref, ofx_ref, g_ref, he_ref, het_ref, on_ref):
        o = jnp.concatenate([osb_ref[...], ofx_ref[...]], axis=1)
        ms = _head_sums(o * o, he_ref[...], het_ref[...]) * (1.0 / HEAD_DIM)
        on_ref[...] = (o * lax.rsqrt(ms + RMS_EPS) * g_ref[...]).astype(BF16)

    (on,) = _rowwise(attn_post, "attn_post", S, TR,
                     [(o_sb, "t"), (o_fx, "t"), (g_row, "f"), (he, "f"), (het, "f")],
                     [((S, D_MODEL), BF16, "t")])

    u1 = _matmul(on, wo, mode="nn", name="mix", tm=TM, tn=D_MODEL, tk=D_MODEL, outs=[F32],
                 extras=[(x2, (TM if S >= TM else S, D_MODEL), _tile_ij)],
                 epilogue=lambda acc, xv: (ALPHA * xv + acc,))

    def ln1_fwd(i, u_ref, g_ref, b_ref, h_ref):
        xh, _ = _ln_stats(u_ref[...])
        h_ref[...] = xh * g_ref[...] + b_ref[...]

    (h1,) = _rowwise(ln1_fwd, "ln1_fwd", S, TR, [(u1, "t"), (ln1_g, "f"), (ln1_b, "f")], [((S, D_MODEL), F32, "t")])

    tm_e = TM if S >= TM else S
    n_ff = D_FF // 256

    def gate_up_body(h_ref, wg_ref, wu_ref, g_ref, u_ref, a_ref):
        h = h_ref[...].astype(BF16)
        g, u = _dot(h, wg_ref[...]), _dot(h, wu_ref[...])
        g_ref[...] = g.astype(BF16)
        u_ref[...] = u.astype(BF16)
        a_ref[...] = (g / (1.0 + jnp.exp(-g)) * u).astype(BF16)

    ff_tile = pl.BlockSpec((tm_e, 256), lambda i, j: (i, j))
    gate, up, act = pl.pallas_call(
        gate_up_body, name="gate_up_act", grid=(S // tm_e, n_ff),
        in_specs=[pl.BlockSpec((tm_e, D_MODEL), lambda i, j: (i, 0)),
                  pl.BlockSpec((D_MODEL, 256), lambda i, j: (0, j)),
                  pl.BlockSpec((D_MODEL, 256), lambda i, j: (0, j + n_ff))],
        out_specs=[ff_tile] * 3, out_shape=[jax.ShapeDtypeStruct((S, D_FF), BF16)] * 3)(h1, wgu, wgu)

    u2 = _matmul(act, wd, mode="nn", name="ffn_down", tm=TM, tn=D_MODEL, tk=D_FF, outs=[F32],
                 extras=[(h1, (TM if S >= TM else S, D_MODEL), _tile_ij)],
                 epilogue=lambda acc, hv: (ALPHA * hv + acc,))

    def ln2_loss(i, u_ref, t_ref, g_ref, b_ref, du_ref, acc_ref):
        xh, r = _ln_stats(u_ref[...])
        g = g_ref[...]
        err = xh * g + b_ref[...] - t_ref[...]
        dy = err * (1.0 / D_MODEL)
        du_ref[...] = _ln_bwd(dy, xh, r, g)
        _acc_rows(i, acc_ref, {2: jnp.sum(dy * xh, axis=0, keepdims=True), 3: jnp.sum(dy, axis=0, keepdims=True),
                               6: jnp.sum(err * err, axis=0, keepdims=True) * (0.5 / D_MODEL)})

    du2, acc_ln2 = _rowwise(ln2_loss, "ln2_loss", S, TR, [(u2, "t"), (tgt, "t"), (ln2_g, "f"), (ln2_b, "f")],
                            [((S, D_MODEL), F32, "t"), ((8, D_MODEL), F32, "f")])

    d_wd = _matmul(act, du2, mode="tn", name="dw_down", tm=1408, tn=D_MODEL, tk=TM, outs=[F32])

    def dgu_epilogue(da, g, u):
        g, u = g.astype(F32), u.astype(F32)
        s = 1.0 / (1.0 + jnp.exp(-g))
        return da * u * (s * (1.0 + g * (1.0 - s))), da * (g * s)

    dgate, dup = _matmul(du2, wd, mode="nt", name="d_act", tm=TM, tn=1408, tk=D_MODEL, outs=[BF16, BF16],
                         extras=[(gate, (tm_e, 1408), _tile_ij), (up, (tm_e, 1408), _tile_ij)],
                         epilogue=dgu_epilogue)
    d_wg = _matmul(h1, dgate, mode="tn", name="dw_gate", tm=D_MODEL, tn=1408, tk=TM, outs=[F32])
    d_wu = _matmul(h1, dup, mode="tn", name="dw_up", tm=D_MODEL, tn=1408, tk=TM, outs=[F32])
    dh1 = _matmul(dgate, wg, mode="nt", name="dh1_gate", tm=TM, tn=D_MODEL, tk=D_FF, outs=[F32],
                  extras=[(du2, (tm_e, D_MODEL), _tile_ij)], epilogue=lambda acc, e: (ALPHA * e + acc,))
    dh1 = _matmul(dup, wu, mode="nt", name="dh1_up", tm=TM, tn=D_MODEL, tk=D_FF, outs=[F32],
                  extras=[(dh1, (tm_e, D_MODEL), _tile_ij)], epilogue=lambda acc, e: (e + acc,))

    def ln1_bwd(i, dh_ref, u_ref, g_ref, du_ref, acc_ref):
        xh, r = _ln_stats(u_ref[...])
        dh = dh_ref[...]
        du_ref[...] = _ln_bwd(dh, xh, r, g_ref[...])
        _acc_rows(i, acc_ref, {0: jnp.sum(dh * xh, axis=0, keepdims=True), 1: jnp.sum(dh, axis=0, keepdims=True)})

    du1, acc_ln1 = _rowwise(ln1_bwd, "ln1_bwd", S, TR, [(dh1, "t"), (u1, "t"), (ln1_g, "f")],
                            [((S, D_MODEL), F32, "t"), ((8, D_MODEL), F32, "f")])
    d_wo = _matmul(on, du1, mode="tn", name="dw_out", tm=D_MODEL, tn=D_MODEL, tk=TM, outs=[F32])
    don = _matmul(du1, wo, mode="nt", name="d_on", tm=TM, tn=D_MODEL, tk=D_MODEL, outs=[F32])

    def rms_bwd(i, don_ref, osb_ref, ofx_ref, g_ref, he_ref, het_ref, dosb_ref, dofx_ref, acc_ref):
        o = jnp.concatenate([osb_ref[...], ofx_ref[...]], axis=1)
        hev, hetv = he_ref[...], het_ref[...]
        r = lax.rsqrt(_head_sums(o * o, hev, hetv) * (1.0 / HEAD_DIM) + RMS_EPS)
        dn = don_ref[...]
        dg = dn * g_ref[...]
        do = r * dg - o * (r * r * r) * (_head_sums(dg * o, hev, hetv) * (1.0 / HEAD_DIM))
        dosb_ref[...] = do[:, :GROUP_W]
        dofx_ref[...] = do[:, GROUP_W:]
        _acc_rows(i, acc_ref, {4: jnp.sum(dn * o * r, axis=0, keepdims=True)})

    do_sb, do_fx, acc_rms = _rowwise(
        rms_bwd, "rms_bwd", S, TR, [(don, "t"), (o_sb, "t"), (o_fx, "t"), (g_row, "f"), (he, "f"), (het, "f")],
        [((S, GROUP_W), F32, "t"), ((S, GROUP_W), F32, "t"), ((8, D_MODEL), F32, "f")])

    dq_sb, dk_sb, dv_sb = _sb_bwd(proj, 0, do_sb, st_sb, jmin_sb, BQ)
    dq_fx, dk_fx, dv_fx, dc, dcq = _fox_bwd(proj, 12, do_fx, o_fx, st_fx, c_col, c_row, jstart_fx, BQ)
    dc = dc[:, :2, :] + dcq[:, :, :2].transpose(0, 2, 1)
    dfl, dbf = _fgate_bwd(dc.reshape(N_FOX, S), lf)
    dp_sb = jnp.concatenate([dq_sb, dk_sb, dv_sb], axis=1).astype(BF16)
    dp_fx = jnp.concatenate([dq_fx, dk_fx, dv_fx], axis=1).astype(BF16)

    d_wsb = _matmul(x2, dp_sb, mode="tn", name="dw_in_sb", tm=D_MODEL, tn=QKV_W // 2, tk=TM, outs=[F32])
    d_wfx = _matmul(x2, dp_fx, mode="tn", name="dw_in_fx", tm=D_MODEL, tn=QKV_W // 2, tk=TM, outs=[F32])
    d_wft = _matmul(dfl, x2, mode="nn", name="dw_in_f", tm=N_FOX, tn=D_MODEL, tk=TM, outs=[F32])
    dx = _matmul(dp_sb, w_sb, mode="nt", name="dx_sb", tm=TM, tn=D_MODEL, tk=QKV_W // 2, outs=[F32],
                 extras=[(du1, (tm_e, D_MODEL), _tile_ij)], epilogue=lambda acc, e: (ALPHA * e + acc,))
    dx = _matmul(dp_fx, w_fx, mode="nt", name="dx_fx", tm=TM, tn=D_MODEL, tk=QKV_W // 2, outs=[F32],
                 extras=[(dx, (tm_e, D_MODEL), _tile_ij)], epilogue=lambda acc, e: (e + acc,))
    dx = _matmul(dfl, wft, mode="tn", name="dx_f", tm=TM, tn=D_MODEL, tk=N_FOX, outs=[F32],
                 extras=[(dx, (tm_e, D_MODEL), _tile_ij)], epilogue=lambda acc, e: (e + acc,))

    d_wi = jnp.concatenate([d_wsb, d_wfx, d_wft.T], axis=1)
    d_wgu = jnp.concatenate([d_wg, d_wu], axis=1)
    parts = [d_wi.reshape(D_MODEL, 4, in_w).transpose(1, 0, 2).astype(BF16),
             d_wo.reshape(4, D_MODEL // 4, D_MODEL).astype(BF16),
             d_wgu.reshape(D_MODEL, 4, gu_w).transpose(1, 0, 2).astype(BF16),
             d_wd.reshape(4, D_FF // 4, D_MODEL).astype(BF16)]
    got = _exchange(parts, True)
    big_names = ("w_in", "w_out", "w_gate_up", "w_down")
    halves = [_sum_parts(p, "sum_" + nm, tr) for nm, p, tr in zip(big_names, got, (256, 128, 128, 176))]
    grads = _sibling_swap(halves)
    big = {}
    for nm, g, w, m, v, tr in zip(big_names, grads, (w_in, w_out, w_gate_up, w_down),
                                  (m_w_in, m_w_out, m_w_gate_up, m_w_down),
                                  (v_w_in, v_w_out, v_w_gate_up, v_w_down), (256, 256, 256, 176)):
        big[nm] = [r[None] for r in [g] + list(_adamw_call(g, w[0], m[0], v[0], "adamw_" + nm, tr))]

    small = acc_ln2 + acc_ln1 + acc_rms
    small = small + jnp.pad(dbf.reshape(1, N_FOX), ((5, 2), (0, D_MODEL - N_FOX)))
    (small_all,) = _exchange([small], False)
    sw = _pack_small(ln1_g, ln1_b, ln2_g, ln2_b, g_sb, g_fox, b_f)
    sm = _pack_small(m_ln1_g, m_ln1_b, m_ln2_g, m_ln2_b, m_g_sb, m_g_fox, m_b_f)
    sv = _pack_small(v_ln1_g, v_ln1_b, v_ln2_g, v_ln2_b, v_g_sb, v_g_fox, v_b_f)
    sg, sd, snm, snv, loss_blk = _sum_adamw_small(small_all, sw, sm, sv)
    sg, sd, snm, snv = _unpack_small(sg), _unpack_small(sd), _unpack_small(snm), _unpack_small(snv)

    names = ["w_in", "b_f", "g_sb", "g_fox", "w_out", "ln1_g", "ln1_b", "ln2_g", "ln2_b", "w_gate_up", "w_down"]
    outs = [loss_blk[0, 0], dx.reshape(1, S, D_MODEL)]
    for k, table in enumerate((sg, sd, snm, snv)):
        outs += [big[n][k] if n in big else table[n] for n in names]
    return tuple(outs)
```

```python
import functools

import numpy as np
import jax
import jax.numpy as jnp
from jax import lax
from jax.experimental import pallas as pl
from jax.experimental.pallas import tpu as pltpu

F32 = jnp.float32
BF16 = jnp.bfloat16

D_MODEL = 1024
HEAD_DIM = 64
LANES = 128
N_PAIRS = 4
GROUP_W = 512
QKV_W = 3072
D_FF = 2816
N_FOX = 8
ALPHA = 2.0 ** 0.25
LN_EPS = 1e-5
RMS_EPS = 1e-6
SCALE = HEAD_DIM ** -0.5
NEG_BIG = -1e30
FOX_SKIP = 30.0
SB_STOP = -105.0
ADAM_LR, ADAM_B1, ADAM_B2, ADAM_EPS, ADAM_WD, ADAM_STEP = 0.001, 0.9, 0.999, 1e-08, 0.01, 10
KV_SLOTS = 4
SCAN_GROUP = 8
ATTN_BLOCK = 256
VMEM_BIG = 56 * 1024 * 1024
MESH = pl.DeviceIdType.MESH

_NN = (((1,), (0,)), ((), ()))
_NT = (((1,), (1,)), ((), ()))
_TN = (((0,), (0,)), ((), ()))


def _dot(a, b, dims=_NN):
    return lax.dot_general(a, b, dims, preferred_element_type=F32)


def _split_dot(x, t):
    hi = x.astype(BF16)
    lo = (x - hi.astype(F32)).astype(BF16)
    return _dot(hi, t) + _dot(lo, t)


def _softplus(z):
    return jnp.maximum(z, 0.0) + jnp.log1p(jnp.exp(-jnp.abs(z)))


def _col(v, h):
    lane = lax.broadcasted_iota(jnp.int32, v.shape, 1)
    return jnp.sum(jnp.where(lane == h, v, 0.0), axis=1, keepdims=True)


def _two_sum(hi, lo, b):
    s = hi + b
    bb = s - hi
    err = (hi - (s - bb)) + (b - bb)
    return s, lo + err


def _params(vmem=None):
    return pltpu.CompilerParams(vmem_limit_bytes=vmem) if vmem else None


def _matmul(a, b, *, mode, name, tm, tn, tk, outs, extras=(), epilogue=None, vmem=None):
    if mode == "nn":
        (M, K), (_, N) = a.shape, b.shape
    elif mode == "nt":
        (M, K), (N, _) = a.shape, b.shape
    else:
        (K, M), (_, N) = a.shape, b.shape
    tm, tn, tk = min(tm, M), min(tn, N), min(tk, K)
    assert M % tm == 0 and N % tn == 0 and K % tk == 0, (name, M, N, K, tm, tn, tk)
    nk = K // tk
    dims = {"nn": _NN, "nt": _NT, "tn": _TN}[mode]
    if mode == "tn":
        a_spec = pl.BlockSpec((tk, tm), lambda i, j, k: (k, i))
    else:
        a_spec = pl.BlockSpec((tm, tk), lambda i, j, k: (i, k))
    if mode == "nt":
        b_spec = pl.BlockSpec((tn, tk), lambda i, j, k: (j, k))
    else:
        b_spec = pl.BlockSpec((tk, tn), lambda i, j, k: (k, j))
    ex_specs = [pl.BlockSpec(bs, (lambda i, j, k, f=f: f(i, j))) for (_, bs, f) in extras]
    ne, no = len(extras), len(outs)
    if epilogue is None:
        epilogue = lambda acc: (acc,)

    def body(a_ref, b_ref, *rest):
        ex_refs, out_refs, acc = rest[:ne], rest[ne:ne + no], rest[-1]
        k = pl.program_id(2)

        @pl.when(k == 0)
        def _():
            acc[...] = jnp.zeros_like(acc)

        acc[...] += _dot(a_ref[...].astype(BF16), b_ref[...].astype(BF16), dims)

        @pl.when(k == nk - 1)
        def _():
            res = epilogue(acc[...], *[e[...] for e in ex_refs])
            for r, o in zip(res, out_refs):
                o[...] = r.astype(o.dtype)

    res = pl.pallas_call(
        body, name=name, grid=(M // tm, N // tn, nk),
        in_specs=[a_spec, b_spec] + ex_specs,
        out_specs=[pl.BlockSpec((tm, tn), lambda i, j, k: (i, j)) for _ in outs],
        out_shape=[jax.ShapeDtypeStruct((M, N), d) for d in outs],
        scratch_shapes=[pltpu.VMEM((tm, tn), F32)],
        compiler_params=_params(vmem),
    )(a, b, *[e[0] for e in extras])
    return res[0] if no == 1 else res


def _tile_ij(i, j):
    return (i, j)


def _rowwise(fn, name, rows, tm, ins, outs, vmem=None):
    tm = min(tm, rows)
    assert rows % tm == 0

    def spec(shape, kind):
        if kind == "t":
            return pl.BlockSpec((tm,) + tuple(shape[1:]), lambda i: (i,) + (0,) * (len(shape) - 1))
        return pl.BlockSpec(tuple(shape), lambda i: (0,) * len(shape))

    def body(*refs):
        fn(pl.program_id(0), *refs)

    return pl.pallas_call(
        body, name=name, grid=(rows // tm,),
        in_specs=[spec(a.shape, k) for a, k in ins],
        out_specs=[spec(s, k) for s, _, k in outs],
        out_shape=[jax.ShapeDtypeStruct(s, d) for s, d, _ in outs],
        compiler_params=_params(vmem),
    )(*[a for a, _ in ins])


def _ln_stats(u):
    mu = jnp.mean(u, axis=-1, keepdims=True)
    d = u - mu
    var = jnp.mean(d * d, axis=-1, keepdims=True)
    r = lax.rsqrt(var + LN_EPS)
    return d * r, r


def _ln_bwd(dh, xh, r, g):
    dxh = dh * g
    m1 = jnp.mean(dxh, axis=-1, keepdims=True)
    m2 = jnp.mean(dxh * xh, axis=-1, keepdims=True)
    return r * (dxh - m1 - xh * m2)


def _acc_rows(i, ref, rows):
    @pl.when(i == 0)
    def _():
        ref[...] = jnp.zeros_like(ref)
    for r, v in rows.items():
        ref[pl.ds(r, 1), :] += v


def _head_sums(v, he, het):
    return _split_dot(_split_dot(v, he), het)


def _fgate_fwd(x, wft, bf_col, tm):
    S = x.shape[0]
    tm = min(tm, S)

    def body(wft_ref, bf_ref, x_ref, lf_ref):
        f = _dot(wft_ref[...], x_ref[...].astype(BF16), _NT) + bf_ref[...]
        lf_ref[...] = -_softplus(-f)

    return pl.pallas_call(
        body, name="fgate_fwd", grid=(S // tm,),
        in_specs=[pl.BlockSpec((N_FOX, D_MODEL), lambda i: (0, 0)), pl.BlockSpec((N_FOX, 1), lambda i: (0, 0)),
                  pl.BlockSpec((tm, D_MODEL), lambda i: (i, 0))],
        out_specs=pl.BlockSpec((N_FOX, tm), lambda i: (0, i)),
        out_shape=jax.ShapeDtypeStruct((N_FOX, S), F32),
    )(wft, bf_col, x)


def _chunk_scan(v, reverse):
    lane = lax.broadcasted_iota(jnp.int32, v.shape, 1)
    sh = 1
    while sh < LANES:
        if reverse:
            v = v + jnp.where(lane < LANES - sh, pltpu.roll(v, LANES - sh, 1), 0.0)
        else:
            v = v + jnp.where(lane >= sh, pltpu.roll(v, sh, 1), 0.0)
        sh *= 2
    return v


def _cumsum_fwd(lf):
    n, S = lf.shape
    nc = S // LANES

    grp = min(SCAN_GROUP, nc)

    def body(lf_ref, c_ref):
        def step(gi, carry):
            sls = [pl.ds(pl.multiple_of((gi * grp + g) * LANES, LANES), LANES) for g in range(grp)]
            vs = [_chunk_scan(lf_ref[:, sl], False) for sl in sls]
            tots = [_col(v, LANES - 1) for v in vs]
            for sl, v, t in zip(sls, vs, tots):
                c_ref[:, sl] = v + carry
                carry = carry + t
            return carry
        lax.fori_loop(0, nc // grp, step, jnp.zeros((n, 1), F32))

    return pl.pallas_call(body, name="cumsum_fwd", out_shape=jax.ShapeDtypeStruct((n, S), F32))(lf)


def _fgate_bwd(dc, lf):
    n, S = dc.shape
    nc = S // LANES

    grp = min(SCAN_GROUP, nc)

    def body(dc_ref, lf_ref, dfl_ref, dbf_ref):
        def step(t, carry):
            car, tot = carry
            gi = nc // grp - 1 - t
            sls = [pl.ds(pl.multiple_of((gi * grp + g) * LANES, LANES), LANES) for g in range(grp)]
            vs = [_chunk_scan(dc_ref[:, sl], True) for sl in sls]
            firsts = [_col(v, 0) for v in vs]
            for sl, v, f in reversed(list(zip(sls, vs, firsts))):
                dfl = (v + car) * (1.0 - jnp.exp(lf_ref[:, sl]))
                dfl_ref[:, sl] = dfl
                tot = tot + jnp.sum(dfl, axis=1, keepdims=True)
                car = car + f
            return car, tot
        _, tot = lax.fori_loop(0, nc // grp, step, (jnp.zeros((n, 1), F32), jnp.zeros((n, 1), F32)))
        dbf_ref[...] = tot

    return pl.pallas_call(body, name="fgate_bwd",
                          out_shape=[jax.ShapeDtypeStruct((n, S), F32), jax.ShapeDtypeStruct((n, 1), F32)])(dc, lf)


def _tri_matrices(b):
    r = np.arange(b)
    tfwd = (r[:, None] <= r[None, :]).astype(np.float32)
    return jnp.asarray(tfwd, BF16), jnp.asarray(tfwd.T, BF16)


def _kv_copies(kv_hbm, kbuf, vbuf, sems, pair_col, bq, j, slot):
    rows = pl.ds(pl.multiple_of(j * bq, bq), bq)

    def cols(c):
        return pl.ds(pl.multiple_of((pair_col + c) * LANES, LANES), LANES)

    return (pltpu.make_async_copy(kv_hbm.at[rows, cols(4)], kbuf.at[slot], sems.at[0, slot]),
            pltpu.make_async_copy(kv_hbm.at[rows, cols(8)], vbuf.at[slot], sems.at[1, slot]))


def _first_two_up(first_block):
    def blocks(pair, blk):
        first = first_block(pair, blk)
        return first, first + 1, first + 1 <= blk
    return blocks


def _first_two_down(pair, blk):
    return blk, blk - 1, blk > 0


def _start_two(fetch, pair, first, second, has_second):
    for cp in fetch(first, 0, pair):
        cp.start()

    @pl.when(has_second)
    def _():
        for cp in fetch(second, 1, pair):
            cp.start()


def _kv_fetcher(kv_hbm, kbuf, vbuf, sems, col0, bq, p, i, blocks):
    def fetch(j, slot, pair=p):
        return _kv_copies(kv_hbm, kbuf, vbuf, sems, col0 + pair, bq, j, slot)

    pl.when(jnp.logical_and(p == 0, i == 0))(lambda: _start_two(fetch, p, *blocks(p, i)))
    return fetch


def _prefetch_next(fetch, p, i, nq, blocks):
    wrap = i == nq - 1

    @pl.when(jnp.logical_not(jnp.logical_and(wrap, p == N_PAIRS - 1)))
    def _():
        pair, blk = jnp.where(wrap, p + 1, p), jnp.where(wrap, 0, i + 1)
        _start_two(fetch, pair, *blocks(pair, blk))


def _masked_pair(v, lane_is_a, scale=1.0):
    v = v.astype(F32) * scale
    return jnp.where(lane_is_a, v, 0.0).astype(BF16), jnp.where(lane_is_a, 0.0, v).astype(BF16)


def _sb_fwd(proj, col0, bq):
    S = proj.shape[0]
    bq = min(bq, S)
    nq = S // bq
    _, trev = _tri_matrices(bq)

    def body(q_ref, kv_hbm, trev_ref, o_ref, st_ref, jmin_ref, acc_a, acc_b, qa, qb, rs, kbuf, vbuf, sems):
        p, i = pl.program_id(0), pl.program_id(1)
        fetch = _kv_fetcher(kv_hbm, kbuf, vbuf, sems, col0, bq, p, i, _first_two_down)
        is_a = lax.broadcasted_iota(jnp.int32, (bq, LANES), 1) < HEAD_DIM
        acc_a[...] = jnp.zeros_like(acc_a)
        acc_b[...] = jnp.zeros_like(acc_b)
        rs[...] = jnp.zeros_like(rs)
        qa[...], qb[...] = _masked_pair(q_ref[...], is_a, SCALE)

        def tile(slot, masked):
            k, v, trev_m = kbuf[slot], vbuf[slot], trev_ref[...]
            if masked:
                tri = lax.broadcasted_iota(jnp.int32, (bq, bq), 0) > lax.broadcasted_iota(jnp.int32, (bq, bq), 1)
            hs, qs, accs = (0, 1), (qa, qb), (acc_a, acc_b)
            z = {h: _dot(qs[h][...], k, _NT) for h in hs}
            lk = {h: -_softplus(z[h]) for h in hs}
            if masked:
                lk = {h: jnp.where(tri, lk[h], 0.0) for h in hs}
            suf = {h: _split_dot(lk[h], trev_m) for h in hs}
            w = {h: jnp.exp(z[h] + suf[h] + (rs[2 * h] + rs[2 * h + 1])) for h in hs}
            if masked:
                w = {h: jnp.where(tri, w[h], 0.0) for h in hs}
            tot = {h: jnp.sum(lk[h], axis=1, keepdims=True) for h in hs}
            pv = {h: _dot(w[h].astype(BF16), v) for h in hs}
            for h in hs:
                accs[h][...] += pv[h]
                rs[2 * h], rs[2 * h + 1] = _two_sum(rs[2 * h], rs[2 * h + 1], tot[h])

        def step(carry):
            j, _ = carry
            slot = lax.rem(i - j, 2)
            for cp in fetch(j, slot):
                cp.wait()

            @pl.when(jnp.logical_and(j > 0, j < i))
            def _():
                for cp in fetch(j - 1, 1 - slot):
                    cp.start()

            pl.when(j == i)(functools.partial(tile, slot, True))
            pl.when(j < i)(functools.partial(tile, slot, False))
            live = jnp.max(jnp.maximum(rs[0], rs[2])) > SB_STOP
            return j - 1, live.astype(jnp.int32)

        j_end, _ = lax.while_loop(lambda c: jnp.logical_and(c[0] >= 0, c[1] > 0), step, (i, jnp.int32(1)))

        @pl.when(j_end >= 0)
        def _():
            for cp in fetch(j_end, lax.rem(i - j_end, 2)):
                cp.wait()

        _prefetch_next(fetch, p, i, nq, _first_two_down)
        jmin_ref[p, i] = j_end + 1
        o_ref[...] = jnp.where(is_a, acc_a[...], acc_b[...])
        lane8 = lax.broadcasted_iota(jnp.int32, (bq, 8), 1)
        st = jnp.zeros((bq, 8), F32)
        for c, src in enumerate((0, 2, 1, 3)):
            st = jnp.where(lane8 == c, rs[src], st)
        st_ref[0] = st

    return pl.pallas_call(
        body, name="sb_fwd", grid=(N_PAIRS, nq),
        in_specs=[pl.BlockSpec((bq, LANES), lambda p, i: (i, col0 + p)),
                  pl.BlockSpec(memory_space=pl.ANY),
                  pl.BlockSpec((bq, bq), lambda p, i: (0, 0))],
        out_specs=[pl.BlockSpec((bq, LANES), lambda p, i: (i, p)),
                   pl.BlockSpec((1, bq, 8), lambda p, i: (p, i, 0)),
                   pl.BlockSpec(memory_space=pltpu.SMEM)],
        out_shape=[jax.ShapeDtypeStruct((S, GROUP_W), F32), jax.ShapeDtypeStruct((N_PAIRS, S, 8), F32),
                   jax.ShapeDtypeStruct((N_PAIRS, nq), jnp.int32)],
        scratch_shapes=[pltpu.VMEM((bq, LANES), F32), pltpu.VMEM((bq, LANES), F32),
                        pltpu.VMEM((bq, LANES), BF16), pltpu.VMEM((bq, LANES), BF16),
                        pltpu.VMEM((4, bq, 1), F32),
                        pltpu.VMEM((2, bq, LANES), BF16), pltpu.VMEM((2, bq, LANES), BF16),
                        pltpu.SemaphoreType.DMA((2, 2))],
    )(proj, proj, trev)


def _sb_bwd(proj, col0, do, st, jmin, bq):
    S = proj.shape[0]
    bq = min(bq, S)
    nq = S // bq
    tfwd, trev = _tri_matrices(bq)

    def body(jmin_ref, q_ref, kv_hbm, do_ref, st_ref, tfwd_ref, trev_ref,
             dq_ref, dk_ref, dv_ref, dq_a, dq_b, qa, qb, doa, dob, rs, kbuf, vbuf, sems):
        p, i = pl.program_id(0), pl.program_id(1)
        j0 = jmin_ref[p, i]
        first_two = _first_two_up(lambda pair, blk: jmin_ref[pair, blk])
        fetch = _kv_fetcher(kv_hbm, kbuf, vbuf, sems, col0, bq, p, i, first_two)
        is_a = lax.broadcasted_iota(jnp.int32, (bq, LANES), 1) < HEAD_DIM

        @pl.when(i == 0)
        def _():
            dk_ref[...] = jnp.zeros_like(dk_ref)
            dv_ref[...] = jnp.zeros_like(dv_ref)

        dq_a[...] = jnp.zeros_like(dq_a)
        dq_b[...] = jnp.zeros_like(dq_b)
        rs[...] = jnp.zeros_like(rs)
        st_v = st_ref[0]
        for h in range(2):
            rs[6 + 2 * h], rs[7 + 2 * h] = _col(st_v, h), _col(st_v, 2 + h)
        qa[...], qb[...] = _masked_pair(q_ref[...], is_a, SCALE)
        doa[...], dob[...] = _masked_pair(do_ref[...], is_a)

        def tile(j, slot, masked):
            k, v = kbuf[slot], vbuf[slot]
            tfwd_m, trev_m = tfwd_ref[...], trev_ref[...]
            if masked:
                tri = lax.broadcasted_iota(jnp.int32, (bq, bq), 0) > lax.broadcasted_iota(jnp.int32, (bq, bq), 1)
            hs, qs, dos, dqs = (0, 1), (qa, qb), (doa, dob), (dq_a, dq_b)
            z = {h: _dot(qs[h][...], k, _NT) for h in hs}
            lk = {h: -_softplus(z[h]) for h in hs}
            if masked:
                lk = {h: jnp.where(tri, lk[h], 0.0) for h in hs}
            suf = {h: _split_dot(lk[h], trev_m) for h in hs}
            dw = {h: _dot(dos[h][...], v, _NT) for h in hs}
            pre = {h: _two_sum(rs[3 * h], rs[3 * h + 1], jnp.sum(lk[h], axis=1, keepdims=True)) for h in hs}
            right = {h: (rs[6 + 2 * h] - pre[h][0]) + (rs[7 + 2 * h] - pre[h][1]) for h in hs}
            w = {h: jnp.exp(z[h] + suf[h] + right[h]) for h in hs}
            if masked:
                w = {h: jnp.where(tri, w[h], 0.0) for h in hs}
            g = {h: dw[h] * w[h] for h in hs}
            gpre = {h: _split_dot(g[h], tfwd_m) for h in hs}
            dz = {h: g[h] - jnp.exp(z[h] + lk[h]) * (gpre[h] + rs[3 * h + 2]) for h in hs}
            if masked:
                dz = {h: jnp.where(tri, dz[h], 0.0) for h in hs}
            gtot = {h: jnp.sum(g[h], axis=1, keepdims=True) for h in hs}
            dzb = {h: dz[h].astype(BF16) for h in hs}
            wb = {h: w[h].astype(BF16) for h in hs}
            dqc = {h: _dot(dzb[h], k) for h in hs}
            dkc = {h: _dot(dzb[h], qs[h][...], _TN) for h in hs}
            dvc = {h: _dot(wb[h], dos[h][...], _TN) for h in hs}
            for h in hs:
                rs[3 * h], rs[3 * h + 1] = pre[h]
                rs[3 * h + 2] += gtot[h]
                dqs[h][...] += dqc[h]
            rows = pl.ds(pl.multiple_of(j * bq, bq), bq)
            dk_ref[rows, :] += dkc[0] + dkc[1]
            dv_ref[rows, :] += dvc[0] + dvc[1]

        _walk_up(fetch, j0, i, tile)
        _prefetch_next(fetch, p, i, nq, first_two)
        dq_ref[...] = jnp.where(is_a, dq_a[...], dq_b[...]) * SCALE

    grid_spec = pltpu.PrefetchScalarGridSpec(
        num_scalar_prefetch=1, grid=(N_PAIRS, nq),
        in_specs=[pl.BlockSpec((bq, LANES), lambda p, i, jm: (i, col0 + p)),
                  pl.BlockSpec(memory_space=pl.ANY),
                  pl.BlockSpec((bq, LANES), lambda p, i, jm: (i, p)),
                  pl.BlockSpec((1, bq, 8), lambda p, i, jm: (p, i, 0)),
                  pl.BlockSpec((bq, bq), lambda p, i, jm: (0, 0)),
                  pl.BlockSpec((bq, bq), lambda p, i, jm: (0, 0))],
        out_specs=[pl.BlockSpec((bq, LANES), lambda p, i, jm: (i, p)),
                   pl.BlockSpec((S, LANES), lambda p, i, jm: (0, p)),
                   pl.BlockSpec((S, LANES), lambda p, i, jm: (0, p))],
        scratch_shapes=[pltpu.VMEM((bq, LANES), F32), pltpu.VMEM((bq, LANES), F32)]
        + [pltpu.VMEM((bq, LANES), BF16)] * 4 + [pltpu.VMEM((10, bq, 1), F32)]
        + [pltpu.VMEM((KV_SLOTS, bq, LANES), BF16)] * 2 + [pltpu.SemaphoreType.DMA((2, KV_SLOTS))])
    return pl.pallas_call(
        body, name="sb_bwd", grid_spec=grid_spec,
        out_shape=[jax.ShapeDtypeStruct((S, GROUP_W), F32)] * 3,
        compiler_params=_params(VMEM_BIG),
    )(jmin, proj, proj, do, st, tfwd, trev)


def _walk_up(fetch, j0, i, tile):
    ahead = KV_SLOTS - 1

    def start(j):
        @pl.when(j <= i)
        def _():
            for cp in fetch(j, lax.rem(j - j0, KV_SLOTS)):
                cp.start()

    for d in range(2, ahead):
        start(j0 + d)

    def step(j, carry):
        slot = lax.rem(j - j0, KV_SLOTS)
        for cp in fetch(j, slot):
            cp.wait()
        start(j + ahead)
        pl.when(j == i)(functools.partial(tile, j, slot, True))
        pl.when(j < i)(functools.partial(tile, j, slot, False))
        return carry

    lax.fori_loop(j0, i + 1, step, 0)


def _by_heads(j, first_a, first_b, heads):
    on_a, on_b = j >= first_a, j >= first_b
    pl.when(jnp.logical_and(on_a, on_b))(functools.partial(heads, (0, 1)))
    pl.when(jnp.logical_and(on_a, jnp.logical_not(on_b)))(functools.partial(heads, (0,)))
    pl.when(jnp.logical_and(on_b, jnp.logical_not(on_a)))(functools.partial(heads, (1,)))


def _fox_start_blocks(proj, col0, c, bq):
    S = proj.shape[0]
    nq = S // bq
    nh = 2 * N_PAIRS

    def heads(first):
        return proj[:, first * LANES:(first + N_PAIRS) * LANES].astype(F32).reshape(S, nh, HEAD_DIM)

    q, k = heads(col0), heads(col0 + 4)
    qn = jnp.sqrt(jnp.sum(q * q, axis=-1))
    kmax = jnp.sqrt(jnp.sum(k * k, axis=-1)).max(axis=0)
    top = SCALE * (qn * kmax[None, :] - jnp.sum(q * k, axis=-1)) + c.T
    top = top.reshape(nq, bq, nh).max(axis=1)
    c_last = c[:, bq - 1::bq].T
    live = top[:, None, :] - c_last[None, :, :] >= -FOX_SKIP

    def first_block(lv):
        first = jnp.where(lv.any(axis=1), jnp.argmax(lv, axis=1), nq)
        return jnp.minimum(first, jnp.arange(nq)[:, None]).T.astype(jnp.int32)

    return jnp.concatenate([first_block(live.reshape(nq, nq, N_PAIRS, 2).any(axis=-1)), first_block(live)], axis=0)


def _fox_fwd(proj, col0, c_col, c_row, jstart, bq):
    S = proj.shape[0]
    bq = min(bq, S)
    nq = S // bq

    def body(js_ref, q_ref, kv_hbm, cc_ref, cr_ref, o_ref, st_ref, acc_a, acc_b, qa, qb, ml, kbuf, vbuf, sems):
        p, i = pl.program_id(0), pl.program_id(1)
        j0 = js_ref[p, i]
        first_two = _first_two_up(lambda pair, blk: js_ref[pair, blk])
        fetch = _kv_fetcher(kv_hbm, kbuf, vbuf, sems, col0, bq, p, i, first_two)
        is_a = lax.broadcasted_iota(jnp.int32, (bq, LANES), 1) < HEAD_DIM
        acc_a[...] = jnp.zeros_like(acc_a)
        acc_b[...] = jnp.zeros_like(acc_b)
        ml[0] = jnp.full((bq, 1), NEG_BIG, F32)
        ml[2] = jnp.full((bq, 1), NEG_BIG, F32)
        ml[1] = jnp.zeros((bq, 1), F32)
        ml[3] = jnp.zeros((bq, 1), F32)
        cc = cc_ref[0]
        ml[4], ml[5] = _col(cc, 0), _col(cc, 1)
        qa[...], qb[...] = _masked_pair(q_ref[...], is_a, SCALE)

        def tile(j, slot, masked):
            k, v = kbuf[slot], vbuf[slot]
            cols = pl.ds(pl.multiple_of(j * bq, bq), bq)
            if masked:
                tri = lax.broadcasted_iota(jnp.int32, (bq, bq), 0) >= lax.broadcasted_iota(jnp.int32, (bq, bq), 1)
            def heads(hs):
                qs, accs = (qa, qb), (acc_a, acc_b)
                s = {h: _dot(qs[h][...], k, _NT) - cr_ref[0, pl.ds(h, 1), cols] for h in hs}
                if masked:
                    s = {h: jnp.where(tri, s[h], NEG_BIG) for h in hs}
                top = {h: jnp.max(s[h], axis=1, keepdims=True) for h in hs}
                m_new = {h: jnp.maximum(ml[2 * h], top[h] + ml[4 + h]) for h in hs}
                a = {h: jnp.exp(ml[2 * h] - m_new[h]) for h in hs}
                pr = {h: jnp.exp(s[h] - (m_new[h] - ml[4 + h])) for h in hs}
                tot = {h: jnp.sum(pr[h], axis=1, keepdims=True) for h in hs}
                pv = {h: _dot(pr[h].astype(BF16), v) for h in hs}
                for h in hs:
                    ml[2 * h] = m_new[h]
                    ml[2 * h + 1] = a[h] * ml[2 * h + 1] + tot[h]
                    accs[h][...] = a[h] * accs[h][...] + pv[h]

            _by_heads(j, js_ref[N_PAIRS + 2 * p, i], js_ref[N_PAIRS + 2 * p + 1, i], heads)

        _walk_up(fetch, j0, i, tile)
        _prefetch_next(fetch, p, i, nq, first_two)
        o_ref[...] = jnp.where(is_a, acc_a[...] / ml[1], acc_b[...] / ml[3])
        lane8 = lax.broadcasted_iota(jnp.int32, (bq, 8), 1)
        st = jnp.where(lane8 == 0, ml[0] + jnp.log(ml[1]), 0.0)
        st_ref[0] = jnp.where(lane8 == 1, ml[2] + jnp.log(ml[3]), st)

    grid_spec = pltpu.PrefetchScalarGridSpec(
        num_scalar_prefetch=1, grid=(N_PAIRS, nq),
        in_specs=[pl.BlockSpec((bq, LANES), lambda p, i, js: (i, col0 + p)),
                  pl.BlockSpec(memory_space=pl.ANY),
                  pl.BlockSpec((1, bq, 8), lambda p, i, js: (p, i, 0)),
                  pl.BlockSpec((1, 8, S), lambda p, i, js: (p, 0, 0))],
        out_specs=[pl.BlockSpec((bq, LANES), lambda p, i, js: (i, p)),
                   pl.BlockSpec((1, bq, 8), lambda p, i, js: (p, i, 0))],
        scratch_shapes=[pltpu.VMEM((bq, LANES), F32), pltpu.VMEM((bq, LANES), F32),
                        pltpu.VMEM((bq, LANES), BF16), pltpu.VMEM((bq, LANES), BF16),
                        pltpu.VMEM((6, bq, 1), F32),
                        pltpu.VMEM((KV_SLOTS, bq, LANES), BF16), pltpu.VMEM((KV_SLOTS, bq, LANES), BF16),
                        pltpu.SemaphoreType.DMA((2, KV_SLOTS))])
    return pl.pallas_call(
        body, name="fox_fwd", grid_spec=grid_spec,
        out_shape=[jax.ShapeDtypeStruct((S, GROUP_W), F32), jax.ShapeDtypeStruct((N_PAIRS, S, 8), F32)],
    )(jstart, proj, proj, c_col, c_row)


def _fox_bwd(proj, col0, do, o, st, c_col, c_row, jstart, bq):
    S = proj.shape[0]
    bq = min(bq, S)
    nq = S // bq

    def body(js_ref, q_ref, kv_hbm, do_ref, o_ref, st_ref, cc_ref, cr_ref,
             dq_ref, dk_ref, dv_ref, dc_ref, dcq_ref, dq_a, dq_b, qa, qb, doa, dob, dd, kbuf, vbuf, sems):
        p, i = pl.program_id(0), pl.program_id(1)
        j0 = js_ref[p, i]
        first_two = _first_two_up(lambda pair, blk: js_ref[pair, blk])
        fetch = _kv_fetcher(kv_hbm, kbuf, vbuf, sems, col0, bq, p, i, first_two)
        is_a = lax.broadcasted_iota(jnp.int32, (bq, LANES), 1) < HEAD_DIM

        @pl.when(i == 0)
        def _():
            dk_ref[...] = jnp.zeros_like(dk_ref)
            dv_ref[...] = jnp.zeros_like(dv_ref)
            dc_ref[...] = jnp.zeros_like(dc_ref)

        dq_a[...] = jnp.zeros_like(dq_a)
        dq_b[...] = jnp.zeros_like(dq_b)
        qa[...], qb[...] = _masked_pair(q_ref[...], is_a, SCALE)
        dov = do_ref[...]
        doa[...], dob[...] = _masked_pair(dov, is_a)
        prod = dov * o_ref[...]
        dd[0] = jnp.sum(jnp.where(is_a, prod, 0.0), axis=1, keepdims=True)
        dd[1] = jnp.sum(jnp.where(is_a, 0.0, prod), axis=1, keepdims=True)
        dd[2] = jnp.zeros((bq, 1), F32)
        dd[3] = jnp.zeros((bq, 1), F32)
        cc, st_v = cc_ref[0], st_ref[0]
        dd[4], dd[5] = _col(cc, 0) - _col(st_v, 0), _col(cc, 1) - _col(st_v, 1)

        def tile(j, slot, masked):
            k, v = kbuf[slot], vbuf[slot]
            if masked:
                tri = lax.broadcasted_iota(jnp.int32, (bq, bq), 0) >= lax.broadcasted_iota(jnp.int32, (bq, bq), 1)
            cols = pl.ds(pl.multiple_of(j * bq, bq), bq)

            def heads(hs):
                qs, dos, dqs = (qa, qb), (doa, dob), (dq_a, dq_b)
                z = {h: _dot(qs[h][...], k, _NT) for h in hs}
                dp = {h: _dot(dos[h][...], v, _NT) for h in hs}
                pr = {h: jnp.exp(z[h] - cr_ref[0, pl.ds(h, 1), cols] + dd[4 + h]) for h in hs}
                if masked:
                    pr = {h: jnp.where(tri, pr[h], 0.0) for h in hs}
                ds = {h: pr[h] * (dp[h] - dd[h]) for h in hs}
                csum = {h: jnp.sum(ds[h], axis=0, keepdims=True) for h in hs}
                rsum = {h: jnp.sum(ds[h], axis=1, keepdims=True) for h in hs}
                dsb = {h: ds[h].astype(BF16) for h in hs}
                prb = {h: pr[h].astype(BF16) for h in hs}
                dqc = {h: _dot(dsb[h], k) for h in hs}
                dkc = [_dot(dsb[h], qs[h][...], _TN) for h in hs]
                dvc = [_dot(prb[h], dos[h][...], _TN) for h in hs]
                for h in hs:
                    dc_ref[0, pl.ds(h, 1), cols] -= csum[h]
                    dd[2 + h] += rsum[h]
                    dqs[h][...] += dqc[h]
                dk_ref[cols, :] += sum(dkc[1:], dkc[0])
                dv_ref[cols, :] += sum(dvc[1:], dvc[0])

            _by_heads(j, js_ref[N_PAIRS + 2 * p, i], js_ref[N_PAIRS + 2 * p + 1, i], heads)

        _walk_up(fetch, j0, i, tile)
        _prefetch_next(fetch, p, i, nq, first_two)
        dq_ref[...] = jnp.where(is_a, dq_a[...], dq_b[...]) * SCALE
        lane8 = lax.broadcasted_iota(jnp.int32, (bq, 8), 1)
        dcq_ref[0] = jnp.where(lane8 == 0, dd[2], jnp.where(lane8 == 1, dd[3], 0.0))

    grid_spec = pltpu.PrefetchScalarGridSpec(
        num_scalar_prefetch=1, grid=(N_PAIRS, nq),
        in_specs=[pl.BlockSpec((bq, LANES), lambda p, i, js: (i, col0 + p)),
                  pl.BlockSpec(memory_space=pl.ANY),
                  pl.BlockSpec((bq, LANES), lambda p, i, js: (i, p)),
                  pl.BlockSpec((bq, LANES), lambda p, i, js: (i, p)),
                  pl.BlockSpec((1, bq, 8), lambda p, i, js: (p, i, 0)),
                  pl.BlockSpec((1, bq, 8), lambda p, i, js: (p, i, 0)),
                  pl.BlockSpec((1, 8, S), lambda p, i, js: (p, 0, 0))],
        out_specs=[pl.BlockSpec((bq, LANES), lambda p, i, js: (i, p)),
                   pl.BlockSpec((S, LANES), lambda p, i, js: (0, p)),
                   pl.BlockSpec((S, LANES), lambda p, i, js: (0, p)),
                   pl.BlockSpec((1, 8, S), lambda p, i, js: (p, 0, 0)),
                   pl.BlockSpec((1, bq, 8), lambda p, i, js: (p, i, 0))],
        scratch_shapes=[pltpu.VMEM((bq, LANES), F32), pltpu.VMEM((bq, LANES), F32)]
        + [pltpu.VMEM((bq, LANES), BF16)] * 4 + [pltpu.VMEM((6, bq, 1), F32)]
        + [pltpu.VMEM((KV_SLOTS, bq, LANES), BF16)] * 2 + [pltpu.SemaphoreType.DMA((2, KV_SLOTS))])
    return pl.pallas_call(
        body, name="fox_bwd", grid_spec=grid_spec,
        out_shape=[jax.ShapeDtypeStruct((S, GROUP_W), F32)] * 3
        + [jax.ShapeDtypeStruct((N_PAIRS, 8, S), F32), jax.ShapeDtypeStruct((N_PAIRS, S, 8), F32)],
        compiler_params=_params(VMEM_BIG),
    )(jstart, proj, proj, do, o, st, c_col, c_row)


_HBM = pl.BlockSpec(memory_space=pltpu.HBM)


def _coords():
    return lax.axis_index("x"), lax.axis_index("y"), lax.axis_index("c")


def _gather_copies(ins, outs, send_sems, recv_sems, loc_sems):
    n = len(ins)
    x, y, c = _coords()
    mine = 2 * x + y
    chips = [(1 - x, y), (x, 1 - y), (1 - x, 1 - y)]

    def copy(w, r, slab, to):
        return pltpu.make_async_remote_copy(
            src_ref=ins[w], dst_ref=outs[w].at[slab], send_sem=send_sems.at[3 * w + r],
            recv_sem=recv_sems.at[3 * w + r], device_id=to, device_id_type=MESH)

    def own():
        local = [pltpu.make_async_copy(ins[w], outs[w].at[mine], loc_sems.at[w]) for w in range(n)]
        return local, [copy(w, r, mine, (cx, cy, c)) for w in range(n) for r, (cx, cy) in enumerate(chips)]

    def start():
        local, sends = own()
        for cp in local + sends:
            cp.start()

    def wait():
        local, sends = own()
        for w in range(n):
            for r, (cx, cy) in enumerate(chips):
                copy(w, r, 2 * cx + cy, (cx, cy, c)).wait_recv()
        for cp in sends:
            cp.wait_send()
        for cp in local:
            cp.wait()

    return start, wait


def _gather_shapes(shards):
    n = len(shards)
    return ([jax.ShapeDtypeStruct((4,) + s.shape, s.dtype) for s in shards],
            [pltpu.SemaphoreType.DMA((3 * n,)), pltpu.SemaphoreType.DMA((3 * n,)), pltpu.SemaphoreType.DMA((n,))])


def _allgather_chips(shards):
    n = len(shards)

    def body(*refs):
        start, wait = _gather_copies(refs[:n], refs[n:2 * n], *refs[2 * n:])
        start()
        wait()

    out_shape, sems = _gather_shapes(shards)
    return pl.pallas_call(body, name="allgather_weights", in_specs=[_HBM] * n, out_specs=[_HBM] * n,
                          out_shape=out_shape, scratch_shapes=sems)(*shards)


def _proj_gather(x, w, shards, tm, tn):
    (M, K), N, n = x.shape, w.shape[1], len(shards)
    tm = min(tm, M)
    gi, gj = M // tm, N // tn

    def body(a_ref, b_ref, *rest):
        o_ref = rest[n]
        start, wait = _gather_copies(rest[:n], rest[n + 1:2 * n + 1], *rest[2 * n + 1:])
        i, j = pl.program_id(0), pl.program_id(1)
        pl.when(jnp.logical_and(i == 0, j == 0))(start)
        o_ref[...] = _dot(a_ref[...].astype(BF16), b_ref[...]).astype(o_ref.dtype)
        pl.when(jnp.logical_and(i == gi - 1, j == gj - 1))(wait)

    out_shape, sems = _gather_shapes(shards)
    return pl.pallas_call(
        body, name="proj_gather", grid=(gi, gj),
        in_specs=[pl.BlockSpec((tm, K), lambda i, j: (i, 0)), pl.BlockSpec((K, tn), lambda i, j: (0, j))] + [_HBM] * n,
        out_specs=[pl.BlockSpec((tm, tn), lambda i, j: (i, j))] + [_HBM] * n,
        out_shape=[jax.ShapeDtypeStruct((M, N), BF16)] + out_shape, scratch_shapes=sems,
    )(x, w, *shards)


def _exchange(parts, per_chip):
    n = len(parts)
    half = [p.shape[1] // 2 for p in parts] if per_chip else None

    def body(*refs):
        ins, outs = refs[:n], refs[n:2 * n]
        send_sems, recv_sems, loc_sems = refs[2 * n:]
        x, y, c = _coords()
        me = 4 * x + 2 * y + c
        peers = [(x ^ fx, y ^ fy, c ^ fc) for fx in (0, 1) for fy in (0, 1) for fc in (0, 1)][1:]

        def src(w, dev):
            if not per_chip:
                return ins[w]
            return ins[w].at[2 * dev[0] + dev[1], pl.ds(pl.multiple_of(dev[2] * half[w], 16), half[w]), :]

        local = [pltpu.make_async_copy(src(w, (x, y, c)), outs[w].at[me], loc_sems.at[w]) for w in range(n)]
        for cp in local:
            cp.start()

        def copy(w, r, source, slab, to):
            return pltpu.make_async_remote_copy(
                src_ref=source, dst_ref=outs[w].at[slab], send_sem=send_sems.at[7 * w + r],
                recv_sem=recv_sems.at[7 * w + r], device_id=to, device_id_type=MESH)

        sends = [copy(w, r, src(w, dev), me, dev) for w in range(n) for r, dev in enumerate(peers)]
        for cp in sends:
            cp.start()
        for w in range(n):
            for r, dev in enumerate(peers):
                copy(w, r, src(w, dev), 4 * dev[0] + 2 * dev[1] + dev[2], dev).wait_recv()
        for cp in sends:
            cp.wait_send()
        for cp in local:
            cp.wait()

    return pl.pallas_call(
        body, name="exchange_per_chip" if per_chip else "exchange_all",
        in_specs=[_HBM] * n, out_specs=[_HBM] * n,
        out_shape=[jax.ShapeDtypeStruct((8, half[w], p.shape[2]) if per_chip else (8,) + p.shape, p.dtype)
                   for w, p in enumerate(parts)],
        scratch_shapes=[pltpu.SemaphoreType.DMA((7 * n,)), pltpu.SemaphoreType.DMA((7 * n,)),
                        pltpu.SemaphoreType.DMA((n,))],
    )(*parts)


def _sibling_swap(halves):
    n = len(halves)

    def body(*refs):
        ins, outs = refs[:n], refs[n:2 * n]
        send_sems, recv_sems, loc_sems = refs[2 * n:]
        x, y, c = _coords()

        def rows(w, core):
            rh = halves[w].shape[0]
            return outs[w].at[pl.ds(pl.multiple_of(core * rh, 8), rh), :]

        def copy(w, core):
            return pltpu.make_async_remote_copy(
                src_ref=ins[w], dst_ref=rows(w, core), send_sem=send_sems.at[w], recv_sem=recv_sems.at[w],
                device_id=(x, y, 1 - c), device_id_type=MESH)

        local = [pltpu.make_async_copy(ins[w], rows(w, c), loc_sems.at[w]) for w in range(n)]
        sends = [copy(w, c) for w in range(n)]
        for cp in local + sends:
            cp.start()
        for w in range(n):
            copy(w, 1 - c).wait_recv()
        for cp in sends:
            cp.wait_send()
        for cp in local:
            cp.wait()

    vmem = pl.BlockSpec(memory_space=pltpu.VMEM)
    return pl.pallas_call(
        body, name="sibling_swap", in_specs=[vmem] * n, out_specs=[vmem] * n,
        out_shape=[jax.ShapeDtypeStruct((2 * h.shape[0], h.shape[1]), h.dtype) for h in halves],
        scratch_shapes=[pltpu.SemaphoreType.DMA((n,)), pltpu.SemaphoreType.DMA((n,)), pltpu.SemaphoreType.DMA((n,))],
    )(*halves)


def _adamw(w, g, m, v):
    m = ADAM_B1 * m + (1.0 - ADAM_B1) * g
    v = ADAM_B2 * v + (1.0 - ADAM_B2) * (g * g)
    m_hat = m / (1.0 - ADAM_B1 ** ADAM_STEP)
    v_hat = v / (1.0 - ADAM_B2 ** ADAM_STEP)
    delta = -ADAM_LR * (m_hat / (jnp.sqrt(v_hat) + ADAM_EPS) + ADAM_WD * w)
    return delta, m, v


def _sum_parts(parts, name, tr):
    _, R, C = parts.shape
    assert R % tr == 0

    def body(p_ref, g_ref):
        g = p_ref[0].astype(F32)
        for d in range(1, 8):
            g = g + p_ref[d].astype(F32)
        g_ref[...] = g

    return pl.pallas_call(
        body, name=name, grid=(R // tr,),
        in_specs=[pl.BlockSpec((8, tr, C), lambda i: (0, i, 0))],
        out_specs=pl.BlockSpec((tr, C), lambda i: (i, 0)), out_shape=jax.ShapeDtypeStruct((R, C), F32),
    )(parts)


def _adamw_call(g, w, m, v, name, tr):
    R, C = w.shape
    assert R % tr == 0

    def body(g_ref, w_ref, m_ref, v_ref, d_ref, nm_ref, nv_ref):
        d_ref[...], nm_ref[...], nv_ref[...] = _adamw(w_ref[...], g_ref[...], m_ref[...], v_ref[...])

    tile = pl.BlockSpec((tr, C), lambda i: (i, 0))
    return pl.pallas_call(
        body, name=name, grid=(R // tr,), in_specs=[tile] * 4,
        out_specs=[tile] * 3, out_shape=[jax.ShapeDtypeStruct((R, C), F32)] * 3,
    )(g, w, m, v)


def _sum_adamw_small(parts, w, m, v):
    def body(p_ref, w_ref, m_ref, v_ref, g_ref, d_ref, nm_ref, nv_ref, loss_ref):
        g = p_ref[0]
        for d in range(1, 8):
            g = g + p_ref[d]
        g_ref[...] = g
        d_ref[...], nm_ref[...], nv_ref[...] = _adamw(w_ref[...], g, m_ref[...], v_ref[...])
        row = lax.broadcasted_iota(jnp.int32, g.shape, 0)
        per_row = jnp.sum(jnp.where(row == 6, g, 0.0), axis=1, keepdims=True)
        loss_ref[...] = jnp.zeros((8, LANES), F32) + jnp.sum(per_row, axis=0, keepdims=True)

    return pl.pallas_call(
        body, name="sum_adamw_small",
        out_shape=[jax.ShapeDtypeStruct((8, D_MODEL), F32)] * 4 + [jax.ShapeDtypeStruct((8, LANES), F32)],
    )(parts, w, m, v)


def _pack_small(ln1_g, ln1_b, ln2_g, ln2_b, g_sb, g_fox, b_f):
    row5 = jnp.pad(b_f.reshape(1, N_FOX), ((0, 0), (0, D_MODEL - N_FOX)))
    rows = [ln1_g.reshape(1, -1), ln1_b.reshape(1, -1), ln2_g.reshape(1, -1), ln2_b.reshape(1, -1),
            jnp.concatenate([g_sb.reshape(1, -1), g_fox.reshape(1, -1)], axis=1), row5,
            jnp.zeros((2, D_MODEL), F32)]
    return jnp.concatenate(rows, axis=0)


def _unpack_small(p):
    return {"ln1_g": p[0:1], "ln1_b": p[1:2], "ln2_g": p[2:3], "ln2_b": p[3:4], "g_sb": p[4:5, :GROUP_W],
            "g_fox": p[4:5, GROUP_W:], "b_f": p[5:6, :N_FOX]}


def kernel(x, w_in, b_f, g_sb, g_fox, w_out, ln1_g, ln1_b, ln2_g, ln2_b, w_gate_up, w_down, loss_target, m_w_in, m_b_f, m_g_sb, m_g_fox, m_w_out, m_ln1_g, m_ln1_b, m_ln2_g, m_ln2_b, m_w_gate_up, m_w_down, v_w_in, v_b_f, v_g_sb, v_g_fox, v_w_out, v_ln1_g, v_ln1_b, v_ln2_g, v_ln2_b, v_w_gate_up, v_w_down):
    S = x.shape[1]
    x2 = x.reshape(S, D_MODEL)
    tgt = loss_target.reshape(S, D_MODEL)
    TM = 1024
    TR = 512
    BQ = ATTN_BLOCK
    in_w = w_in.shape[2]
    gu_w = w_gate_up.shape[2]

    shards = [w_in[0].astype(BF16), w_out[0].astype(BF16), w_gate_up[0].astype(BF16), w_down[0].astype(BF16)]
    (wi_s,) = _allgather_chips(shards[:1])
    wi = wi_s.transpose(1, 0, 2).reshape(D_MODEL, 4 * in_w)
    w_sb, w_fx = wi[:, :QKV_W // 2], wi[:, QKV_W // 2:QKV_W]
    wqkv = wi[:, :QKV_W]
    wft = wi[:, QKV_W:].T
    proj, wo_s, wgu_s, wd_s = _proj_gather(x2, wqkv, shards[1:], TM, 512)
    wo = wo_s.reshape(D_MODEL, D_MODEL)
    wgu = wgu_s.transpose(1, 0, 2).reshape(D_MODEL, 2 * D_FF)
    wg, wu = wgu[:, :D_FF], wgu[:, D_FF:]
    wd = wd_s.reshape(D_FF, D_MODEL)
    g_row = jnp.concatenate([g_sb, g_fox], axis=1)
    hid = np.arange(D_MODEL) // HEAD_DIM
    he_np = (hid[:, None] == np.arange(LANES)[None, :]).astype(np.float32)
    he, het = jnp.asarray(he_np, BF16), jnp.asarray(he_np.T, BF16)

    lf = _fgate_fwd(x2, wft, b_f.reshape(N_FOX, 1), TM)
    c = _cumsum_fwd(lf)
    c_pair = c.reshape(N_PAIRS, 2, S)
    c_row = jnp.pad(c_pair, ((0, 0), (0, 6), (0, 0)))
    c_col = jnp.pad(c_pair.transpose(0, 2, 1), ((0, 0), (0, 0), (0, 6)))

    o_sb, st_sb, jmin_sb = _sb_fwd(proj, 0, BQ)
    jstart_fx = _fox_start_blocks(proj, 12, c, min(BQ, S))
    o_fx, st_fx = _fox_fwd(proj, 12, c_col, c_row, jstart_fx, BQ)

    def attn_post(i, osb_ref, ofx_ref, g_ref, he_ref, het_ref, on_ref):
        o = jnp.concatenate([osb_ref[...], ofx_ref[...]], axis=1)
        ms = _head_sums(o * o, he_ref[...], het_ref[...]) * (1.0 / HEAD_DIM)
        on_ref[...] = (o * lax.rsqrt(ms + RMS_EPS) * g_ref[...]).astype(BF16)

    (on,) = _rowwise(attn_post, "attn_post", S, TR,
                     [(o_sb, "t"), (o_fx, "t"), (g_row, "f"), (he, "f"), (het, "f")],
                     [((S, D_MODEL), BF16, "t")])

    u1 = _matmul(on, wo, mode="nn", name="mix", tm=TM, tn=D_MODEL, tk=D_MODEL, outs=[F32],
                 extras=[(x2, (TM if S >= TM else S, D_MODEL), _tile_ij)],
                 epilogue=lambda acc, xv: (ALPHA * xv + acc,))

    def ln1_fwd(i, u_ref, g_ref, b_ref, h_ref):
        xh, _ = _ln_stats(u_ref[...])
        h_ref[...] = xh * g_ref[...] + b_ref[...]

    (h1,) = _rowwise(ln1_fwd, "ln1_fwd", S, TR, [(u1, "t"), (ln1_g, "f"), (ln1_b, "f")], [((S, D_MODEL), F32, "t")])

    tm_e = TM if S >= TM else S
    n_ff = D_FF // 256

    def gate_up_body(h_ref, wg_ref, wu_ref, g_ref, u_ref, a_ref):
        h = h_ref[...].astype(BF16)
        g, u = _dot(h, wg_ref[...]), _dot(h, wu_ref[...])
        g_ref[...] = g.astype(BF16)
        u_ref[...] = u.astype(BF16)
        a_ref[...] = (g / (1.0 + jnp.exp(-g)) * u).astype(BF16)

    ff_tile = pl.BlockSpec((tm_e, 256), lambda i, j: (i, j))
    gate, up, act = pl.pallas_call(
        gate_up_body, name="gate_up_act", grid=(S // tm_e, n_ff),
        in_specs=[pl.BlockSpec((tm_e, D_MODEL), lambda i, j: (i, 0)),
                  pl.BlockSpec((D_MODEL, 256), lambda i, j: (0, j)),
                  pl.BlockSpec((D_MODEL, 256), lambda i, j: (0, j + n_ff))],
        out_specs=[ff_tile] * 3, out_shape=[jax.ShapeDtypeStruct((S, D_FF), BF16)] * 3)(h1, wgu, wgu)

    u2 = _matmul(act, wd, mode="nn", name="ffn_down", tm=TM, tn=D_MODEL, tk=D_FF, outs=[F32],
                 extras=[(h1, (TM if S >= TM else S, D_MODEL), _tile_ij)],
                 epilogue=lambda acc, hv: (ALPHA * hv + acc,))

    def ln2_loss(i, u_ref, t_ref, g_ref, b_ref, du_ref, acc_ref):
        xh, r = _ln_stats(u_ref[...])
        g = g_ref[...]
        err = xh * g + b_ref[...] - t_ref[...]
        dy = err * (1.0 / D_MODEL)
        du_ref[...] = _ln_bwd(dy, xh, r, g)
        _acc_rows(i, acc_ref, {2: jnp.sum(dy * xh, axis=0, keepdims=True), 3: jnp.sum(dy, axis=0, keepdims=True),
                               6: jnp.sum(err * err, axis=0, keepdims=True) * (0.5 / D_MODEL)})

    du2, acc_ln2 = _rowwise(ln2_loss, "ln2_loss", S, TR, [(u2, "t"), (tgt, "t"), (ln2_g, "f"), (ln2_b, "f")],
                            [((S, D_MODEL), F32, "t"), ((8, D_MODEL), F32, "f")])

    d_wd = _matmul(act, du2, mode="tn", name="dw_down", tm=1408, tn=D_MODEL, tk=TM, outs=[F32])

    def dgu_epilogue(da, g, u):
        g, u = g.astype(F32), u.astype(F32)
        s = 1.0 / (1.0 + jnp.exp(-g))
        return da * u * (s * (1.0 + g * (1.0 - s))), da * (g * s)

    dgate, dup = _matmul(du2, wd, mode="nt", name="d_act", tm=TM, tn=1408, tk=D_MODEL, outs=[BF16, BF16],
                         extras=[(gate, (tm_e, 1408), _tile_ij), (up, (tm_e, 1408), _tile_ij)],
                         epilogue=dgu_epilogue)
    d_wg = _matmul(h1, dgate, mode="tn", name="dw_gate", tm=D_MODEL, tn=1408, tk=TM, outs=[F32])
    d_wu = _matmul(h1, dup, mode="tn", name="dw_up", tm=D_MODEL, tn=1408, tk=TM, outs=[F32])
    dh1 = _matmul(dgate, wg, mode="nt", name="dh1_gate", tm=TM, tn=D_MODEL, tk=D_FF, outs=[F32],
                  extras=[(du2, (tm_e, D_MODEL), _tile_ij)], epilogue=lambda acc, e: (ALPHA * e + acc,))
    dh1 = _matmul(dup, wu, mode="nt", name="dh1_up", tm=TM, tn=D_MODEL, tk=D_FF, outs=[F32],
                  extras=[(dh1, (tm_e, D_MODEL), _tile_ij)], epilogue=lambda acc, e: (e + acc,))

    def ln1_bwd(i, dh_ref, u_ref, g_ref, du_ref, acc_ref):
        xh, r = _ln_stats(u_ref[...])
        dh = dh_ref[...]
        du_ref[...] = _ln_bwd(dh, xh, r, g_ref[...])
        _acc_rows(i, acc_ref, {0: jnp.sum(dh * xh, axis=0, keepdims=True), 1: jnp.sum(dh, axis=0, keepdims=True)})

    du1, acc_ln1 = _rowwise(ln1_bwd, "ln1_bwd", S, TR, [(dh1, "t"), (u1, "t"), (ln1_g, "f")],
                            [((S, D_MODEL), F32, "t"), ((8, D_MODEL), F32, "f")])
    d_wo = _matmul(on, du1, mode="tn", name="dw_out", tm=D_MODEL, tn=D_MODEL, tk=TM, outs=[F32])
    don = _matmul(du1, wo, mode="nt", name="d_on", tm=TM, tn=D_MODEL, tk=D_MODEL, outs=[F32])

    def rms_bwd(i, don_ref, osb_ref, ofx_ref, g_ref, he_ref, het_ref, dosb_ref, dofx_ref, acc_ref):
        o = jnp.concatenate([osb_ref[...], ofx_ref[...]], axis=1)
        hev, hetv = he_ref[...], het_ref[...]
        r = lax.rsqrt(_head_sums(o * o, hev, hetv) * (1.0 / HEAD_DIM) + RMS_EPS)
        dn = don_ref[...]
        dg = dn * g_ref[...]
        do = r * dg - o * (r * r * r) * (_head_sums(dg * o, hev, hetv) * (1.0 / HEAD_DIM))
        dosb_ref[...] = do[:, :GROUP_W]
        dofx_ref[...] = do[:, GROUP_W:]
        _acc_rows(i, acc_ref, {4: jnp.sum(dn * o * r, axis=0, keepdims=True)})

    do_sb, do_fx, acc_rms = _rowwise(
        rms_bwd, "rms_bwd", S, TR, [(don, "t"), (o_sb, "t"), (o_fx, "t"), (g_row, "f"), (he, "f"), (het, "f")],
        [((S, GROUP_W), F32, "t"), ((S, GROUP_W), F32, "t"), ((8, D_MODEL), F32, "f")])

    dq_sb, dk_sb, dv_sb = _sb_bwd(proj, 0, do_sb, st_sb, jmin_sb, BQ)
    dq_fx, dk_fx, dv_fx, dc, dcq = _fox_bwd(proj, 12, do_fx, o_fx, st_fx, c_col, c_row, jstart_fx, BQ)
    dc = dc[:, :2, :] + dcq[:, :, :2].transpose(0, 2, 1)
    dfl, dbf = _fgate_bwd(dc.reshape(N_FOX, S), lf)
    dp_sb = jnp.concatenate([dq_sb, dk_sb, dv_sb], axis=1).astype(BF16)
    dp_fx = jnp.concatenate([dq_fx, dk_fx, dv_fx], axis=1).astype(BF16)

    d_wsb = _matmul(x2, dp_sb, mode="tn", name="dw_in_sb", tm=D_MODEL, tn=QKV_W // 2, tk=TM, outs=[F32])
    d_wfx = _matmul(x2, dp_fx, mode="tn", name="dw_in_fx", tm=D_MODEL, tn=QKV_W // 2, tk=TM, outs=[F32])
    d_wft = _matmul(dfl, x2, mode="nn", name="dw_in_f", tm=N_FOX, tn=D_MODEL, tk=TM, outs=[F32])
    dx = _matmul(dp_sb, w_sb, mode="nt", name="dx_sb", tm=TM, tn=D_MODEL, tk=QKV_W // 2, outs=[F32],
                 extras=[(du1, (tm_e, D_MODEL), _tile_ij)], epilogue=lambda acc, e: (ALPHA * e + acc,))
    dx = _matmul(dp_fx, w_fx, mode="nt", name="dx_fx", tm=TM, tn=D_MODEL, tk=QKV_W // 2, outs=[F32],
                 extras=[(dx, (tm_e, D_MODEL), _tile_ij)], epilogue=lambda acc, e: (e + acc,))
    dx = _matmul(dfl, wft, mode="tn", name="dx_f", tm=TM, tn=D_MODEL, tk=N_FOX, outs=[F32],
                 extras=[(dx, (tm_e, D_MODEL), _tile_ij)], epilogue=lambda acc, e: (e + acc,))

    d_wi = jnp.concatenate([d_wsb, d_wfx, d_wft.T], axis=1)
    d_wgu = jnp.concatenate([d_wg, d_wu], axis=1)
    parts = [d_wi.reshape(D_MODEL, 4, in_w).transpose(1, 0, 2).astype(BF16),
             d_wo.reshape(4, D_MODEL // 4, D_MODEL).astype(BF16),
             d_wgu.reshape(D_MODEL, 4, gu_w).transpose(1, 0, 2).astype(BF16),
             d_wd.reshape(4, D_FF // 4, D_MODEL).astype(BF16)]
    got = _exchange(parts, True)
    big_names = ("w_in", "w_out", "w_gate_up", "w_down")
    halves = [_sum_parts(p, "sum_" + nm, tr) for nm, p, tr in zip(big_names, got, (256, 128, 128, 176))]
    grads = _sibling_swap(halves)
    big = {}
    for nm, g, w, m, v, tr in zip(big_names, grads, (w_in, w_out, w_gate_up, w_down),
                                  (m_w_in, m_w_out, m_w_gate_up, m_w_down),
                                  (v_w_in, v_w_out, v_w_gate_up, v_w_down), (256, 256, 256, 176)):
        big[nm] = [r[None] for r in [g] + list(_adamw_call(g, w[0], m[0], v[0], "adamw_" + nm, tr))]

    small = acc_ln2 + acc_ln1 + acc_rms
    small = small + jnp.pad(dbf.reshape(1, N_FOX), ((5, 2), (0, D_MODEL - N_FOX)))
    (small_all,) = _exchange([small], False)
    sw = _pack_small(ln1_g, ln1_b, ln2_g, ln2_b, g_sb, g_fox, b_f)
    sm = _pack_small(m_ln1_g, m_ln1_b, m_ln2_g, m_ln2_b, m_g_sb, m_g_fox, m_b_f)
    sv = _pack_small(v_ln1_g, v_ln1_b, v_ln2_g, v_ln2_b, v_g_sb, v_g_fox, v_b_f)
    sg, sd, snm, snv, loss_blk = _sum_adamw_small(small_all, sw, sm, sv)
    sg, sd, snm, snv = _unpack_small(sg), _unpack_small(sd), _unpack_small(snm), _unpack_small(snv)

    names = ["w_in", "b_f", "g_sb", "g_fox", "w_out", "ln1_g", "ln1_b", "ln2_g", "ln2_b", "w_gate_up", "w_down"]
    outs = [loss_blk[0, 0], dx.reshape(1, S, D_MODEL)]
    for k, table in enumerate((sg, sd, snm, snv)):
        outs += [big[n][k] if n in big else table[n] for n in names]
    return tuple(outs)
```

```python
import functools

import numpy as np
import jax
import jax.numpy as jnp
from jax import lax
from jax.experimental import pallas as pl
from jax.experimental.pallas import tpu as pltpu

F32 = jnp.float32
BF16 = jnp.bfloat16

D_MODEL = 1024
HEAD_DIM = 64
LANES = 128
N_PAIRS = 4
GROUP_W = 512
QKV_W = 3072
D_FF = 2816
N_FOX = 8
ALPHA = 2.0 ** 0.25
LN_EPS = 1e-5
RMS_EPS = 1e-6
SCALE = HEAD_DIM ** -0.5
NEG_BIG = -1e30
FOX_SKIP = 30.0
SB_STOP = -105.0
ADAM_LR, ADAM_B1, ADAM_B2, ADAM_EPS, ADAM_WD, ADAM_STEP = 0.001, 0.9, 0.999, 1e-08, 0.01, 10
KV_SLOTS = 4
SCAN_GROUP = 8
ATTN_BLOCK = 256
VMEM_BIG = 56 * 1024 * 1024
MESH = pl.DeviceIdType.MESH

_NN = (((1,), (0,)), ((), ()))
_NT = (((1,), (1,)), ((), ()))
_TN = (((0,), (0,)), ((), ()))


def _dot(a, b, dims=_NN):
    return lax.dot_general(a, b, dims, preferred_element_type=F32)


def _split_dot(x, t):
    hi = x.astype(BF16)
    lo = (x - hi.astype(F32)).astype(BF16)
    return _dot(hi, t) + _dot(lo, t)


def _softplus(z):
    return jnp.maximum(z, 0.0) + jnp.log1p(jnp.exp(-jnp.abs(z)))


def _col(v, h):
    lane = lax.broadcasted_iota(jnp.int32, v.shape, 1)
    return jnp.sum(jnp.where(lane == h, v, 0.0), axis=1, keepdims=True)


def _two_sum(hi, lo, b):
    s = hi + b
    bb = s - hi
    err = (hi - (s - bb)) + (b - bb)
    return s, lo + err


def _params(vmem=None):
    return pltpu.CompilerParams(vmem_limit_bytes=vmem) if vmem else None


def _matmul(a, b, *, mode, name, tm, tn, tk, outs, extras=(), epilogue=None, vmem=None):
    if mode == "nn":
        (M, K), (_, N) = a.shape, b.shape
    elif mode == "nt":
        (M, K), (N, _) = a.shape, b.shape
    else:
        (K, M), (_, N) = a.shape, b.shape
    tm, tn, tk = min(tm, M), min(tn, N), min(tk, K)
    assert M % tm == 0 and N % tn == 0 and K % tk == 0, (name, M, N, K, tm, tn, tk)
    nk = K // tk
    dims = {"nn": _NN, "nt": _NT, "tn": _TN}[mode]
    if mode == "tn":
        a_spec = pl.BlockSpec((tk, tm), lambda i, j, k: (k, i))
    else:
        a_spec = pl.BlockSpec((tm, tk), lambda i, j, k: (i, k))
    if mode == "nt":
        b_spec = pl.BlockSpec((tn, tk), lambda i, j, k: (j, k))
    else:
        b_spec = pl.BlockSpec((tk, tn), lambda i, j, k: (k, j))
    ex_specs = [pl.BlockSpec(bs, (lambda i, j, k, f=f: f(i, j))) for (_, bs, f) in extras]
    ne, no = len(extras), len(outs)
    if epilogue is None:
        epilogue = lambda acc: (acc,)

    def body(a_ref, b_ref, *rest):
        ex_refs, out_refs, acc = rest[:ne], rest[ne:ne + no], rest[-1]
        k = pl.program_id(2)

        @pl.when(k == 0)
        def _():
            acc[...] = jnp.zeros_like(acc)

        acc[...] += _dot(a_ref[...].astype(BF16), b_ref[...].astype(BF16), dims)

        @pl.when(k == nk - 1)
        def _():
            res = epilogue(acc[...], *[e[...] for e in ex_refs])
            for r, o in zip(res, out_refs):
                o[...] = r.astype(o.dtype)

    res = pl.pallas_call(
        body, name=name, grid=(M // tm, N // tn, nk),
        in_specs=[a_spec, b_spec] + ex_specs,
        out_specs=[pl.BlockSpec((tm, tn), lambda i, j, k: (i, j)) for _ in outs],
        out_shape=[jax.ShapeDtypeStruct((M, N), d) for d in outs],
        scratch_shapes=[pltpu.VMEM((tm, tn), F32)],
        compiler_params=_params(vmem),
    )(a, b, *[e[0] for e in extras])
    return res[0] if no == 1 else res


def _tile_ij(i, j):
    return (i, j)


def _rowwise(fn, name, rows, tm, ins, outs, vmem=None):
    tm = min(tm, rows)
    assert rows % tm == 0

    def spec(shape, kind):
        if kind == "t":
            return pl.BlockSpec((tm,) + tuple(shape[1:]), lambda i: (i,) + (0,) * (len(shape) - 1))
        return pl.BlockSpec(tuple(shape), lambda i: (0,) * len(shape))

    def body(*refs):
        fn(pl.program_id(0), *refs)

    return pl.pallas_call(
        body, name=name, grid=(rows // tm,),
        in_specs=[spec(a.shape, k) for a, k in ins],
        out_specs=[spec(s, k) for s, _, k in outs],
        out_shape=[jax.ShapeDtypeStruct(s, d) for s, d, _ in outs],
        compiler_params=_params(vmem),
    )(*[a for a, _ in ins])


def _ln_stats(u):
    mu = jnp.mean(u, axis=-1, keepdims=True)
    d = u - mu
    var = jnp.mean(d * d, axis=-1, keepdims=True)
    r = lax.rsqrt(var + LN_EPS)
    return d * r, r


def _ln_bwd(dh, xh, r, g):
    dxh = dh * g
    m1 = jnp.mean(dxh, axis=-1, keepdims=True)
    m2 = jnp.mean(dxh * xh, axis=-1, keepdims=True)
    return r * (dxh - m1 - xh * m2)


def _acc_rows(i, ref, rows):
    @pl.when(i == 0)
    def _():
        ref[...] = jnp.zeros_like(ref)
    for r, v in rows.items():
        ref[pl.ds(r, 1), :] += v


def _head_sums(v, he, het):
    return _split_dot(_split_dot(v, he), het)


def _fgate_fwd(x, wft, bf_col, tm):
    S = x.shape[0]
    tm = min(tm, S)

    def body(wft_ref, bf_ref, x_ref, lf_ref):
        f = _dot(wft_ref[...], x_ref[...].astype(BF16), _NT) + bf_ref[...]
        lf_ref[...] = -_softplus(-f)

    return pl.pallas_call(
        body, name="fgate_fwd", grid=(S // tm,),
        in_specs=[pl.BlockSpec((N_FOX, D_MODEL), lambda i: (0, 0)), pl.BlockSpec((N_FOX, 1), lambda i: (0, 0)),
                  pl.BlockSpec((tm, D_MODEL), lambda i: (i, 0))],
        out_specs=pl.BlockSpec((N_FOX, tm), lambda i: (0, i)),
        out_shape=jax.ShapeDtypeStruct((N_FOX, S), F32),
    )(wft, bf_col, x)


def _chunk_scan(v, reverse):
    lane = lax.broadcasted_iota(jnp.int32, v.shape, 1)
    sh = 1
    while sh < LANES:
        if reverse:
            v = v + jnp.where(lane < LANES - sh, pltpu.roll(v, LANES - sh, 1), 0.0)
        else:
            v = v + jnp.where(lane >= sh, pltpu.roll(v, sh, 1), 0.0)
        sh *= 2
    return v


def _cumsum_fwd(lf):
    n, S = lf.shape
    nc = S // LANES

    grp = min(SCAN_GROUP, nc)

    def body(lf_ref, c_ref):
        def step(gi, carry):
            sls = [pl.ds(pl.multiple_of((gi * grp + g) * LANES, LANES), LANES) for g in range(grp)]
            vs = [_chunk_scan(lf_ref[:, sl], False) for sl in sls]
            tots = [_col(v, LANES - 1) for v in vs]
            for sl, v, t in zip(sls, vs, tots):
                c_ref[:, sl] = v + carry
                carry = carry + t
            return carry
        lax.fori_loop(0, nc // grp, step, jnp.zeros((n, 1), F32))

    return pl.pallas_call(body, name="cumsum_fwd", out_shape=jax.ShapeDtypeStruct((n, S), F32))(lf)


def _fgate_bwd(dc, lf):
    n, S = dc.shape
    nc = S // LANES

    grp = min(SCAN_GROUP, nc)

    def body(dc_ref, lf_ref, dfl_ref, dbf_ref):
        def step(t, carry):
            car, tot = carry
            gi = nc // grp - 1 - t
            sls = [pl.ds(pl.multiple_of((gi * grp + g) * LANES, LANES), LANES) for g in range(grp)]
            vs = [_chunk_scan(dc_ref[:, sl], True) for sl in sls]
            firsts = [_col(v, 0) for v in vs]
            for sl, v, f in reversed(list(zip(sls, vs, firsts))):
                dfl = (v + car) * (1.0 - jnp.exp(lf_ref[:, sl]))
                dfl_ref[:, sl] = dfl
                tot = tot + jnp.sum(dfl, axis=1, keepdims=True)
                car = car + f
            return car, tot
        _, tot = lax.fori_loop(0, nc // grp, step, (jnp.zeros((n, 1), F32), jnp.zeros((n, 1), F32)))
        dbf_ref[...] = tot

    return pl.pallas_call(body, name="fgate_bwd",
                          out_shape=[jax.ShapeDtypeStruct((n, S), F32), jax.ShapeDtypeStruct((n, 1), F32)])(dc, lf)


def _tri_matrices(b):
    r = np.arange(b)
    tfwd = (r[:, None] <= r[None, :]).astype(np.float32)
    return jnp.asarray(tfwd, BF16), jnp.asarray(tfwd.T, BF16)


def _kv_copies(kv_hbm, kbuf, vbuf, sems, pair_col, bq, j, slot):
    rows = pl.ds(pl.multiple_of(j * bq, bq), bq)

    def cols(c):
        return pl.ds(pl.multiple_of((pair_col + c) * LANES, LANES), LANES)

    return (pltpu.make_async_copy(kv_hbm.at[rows, cols(4)], kbuf.at[slot], sems.at[0, slot]),
            pltpu.make_async_copy(kv_hbm.at[rows, cols(8)], vbuf.at[slot], sems.at[1, slot]))


def _first_two_up(first_block, per=1):
    def blocks(pair, blk):
        first = first_block(pair, blk)
        return first, first + 1, first + 1 <= per * blk + per - 1
    return blocks


def _first_two_down(pair, blk):
    return blk, blk - 1, blk > 0


def _start_two(fetch, pair, first, second, has_second):
    for cp in fetch(first, 0, pair):
        cp.start()

    @pl.when(has_second)
    def _():
        for cp in fetch(second, 1, pair):
            cp.start()


def _kv_fetcher(kv_hbm, kbuf, vbuf, sems, col0, bq, p, i, blocks):
    def fetch(j, slot, pair=p):
        return _kv_copies(kv_hbm, kbuf, vbuf, sems, col0 + pair, bq, j, slot)

    pl.when(jnp.logical_and(p == 0, i == 0))(lambda: _start_two(fetch, p, *blocks(p, i)))
    return fetch


def _prefetch_next(fetch, p, i, nq, blocks):
    wrap = i == nq - 1

    @pl.when(jnp.logical_not(jnp.logical_and(wrap, p == N_PAIRS - 1)))
    def _():
        pair, blk = jnp.where(wrap, p + 1, p), jnp.where(wrap, 0, i + 1)
        _start_two(fetch, pair, *blocks(pair, blk))


def _masked_pair(v, lane_is_a, scale=1.0):
    v = v.astype(F32) * scale
    return jnp.where(lane_is_a, v, 0.0).astype(BF16), jnp.where(lane_is_a, 0.0, v).astype(BF16)


def _sb_fwd(proj, col0, bq):
    S = proj.shape[0]
    bq = min(bq, S)
    nq = S // bq
    _, trev = _tri_matrices(bq)

    def body(q_ref, kv_hbm, trev_ref, o_ref, st_ref, jmin_ref, acc_a, acc_b, qa, qb, rs, kbuf, vbuf, sems):
        p, i = pl.program_id(0), pl.program_id(1)
        fetch = _kv_fetcher(kv_hbm, kbuf, vbuf, sems, col0, bq, p, i, _first_two_down)
        is_a = lax.broadcasted_iota(jnp.int32, (bq, LANES), 1) < HEAD_DIM
        acc_a[...] = jnp.zeros_like(acc_a)
        acc_b[...] = jnp.zeros_like(acc_b)
        rs[...] = jnp.zeros_like(rs)
        qa[...], qb[...] = _masked_pair(q_ref[...], is_a, SCALE)

        def tile(slot, masked):
            k, v, trev_m = kbuf[slot], vbuf[slot], trev_ref[...]
            if masked:
                tri = lax.broadcasted_iota(jnp.int32, (bq, bq), 0) > lax.broadcasted_iota(jnp.int32, (bq, bq), 1)
            hs, qs, accs = (0, 1), (qa, qb), (acc_a, acc_b)
            z = {h: _dot(qs[h][...], k, _NT) for h in hs}
            lk = {h: -_softplus(z[h]) for h in hs}
            if masked:
                lk = {h: jnp.where(tri, lk[h], 0.0) for h in hs}
            suf = {h: _split_dot(lk[h], trev_m) for h in hs}
            w = {h: jnp.exp(z[h] + suf[h] + (rs[2 * h] + rs[2 * h + 1])) for h in hs}
            if masked:
                w = {h: jnp.where(tri, w[h], 0.0) for h in hs}
            tot = {h: jnp.sum(lk[h], axis=1, keepdims=True) for h in hs}
            pv = {h: _dot(w[h].astype(BF16), v) for h in hs}
            for h in hs:
                accs[h][...] += pv[h]
                rs[2 * h], rs[2 * h + 1] = _two_sum(rs[2 * h], rs[2 * h + 1], tot[h])

        def step(carry):
            j, _ = carry
            slot = lax.rem(i - j, 2)
            for cp in fetch(j, slot):
                cp.wait()

            @pl.when(jnp.logical_and(j > 0, j < i))
            def _():
                for cp in fetch(j - 1, 1 - slot):
                    cp.start()

            pl.when(j == i)(functools.partial(tile, slot, True))
            pl.when(j < i)(functools.partial(tile, slot, False))
            live = jnp.max(jnp.maximum(rs[0], rs[2])) > SB_STOP
            return j - 1, live.astype(jnp.int32)

        j_end, _ = lax.while_loop(lambda c: jnp.logical_and(c[0] >= 0, c[1] > 0), step, (i, jnp.int32(1)))

        @pl.when(j_end >= 0)
        def _():
            for cp in fetch(j_end, lax.rem(i - j_end, 2)):
                cp.wait()

        _prefetch_next(fetch, p, i, nq, _first_two_down)
        jmin_ref[p, i] = j_end + 1
        o_ref[...] = jnp.where(is_a, acc_a[...], acc_b[...])
        lane8 = lax.broadcasted_iota(jnp.int32, (bq, 8), 1)
        st = jnp.zeros((bq, 8), F32)
        for c, src in enumerate((0, 2, 1, 3)):
            st = jnp.where(lane8 == c, rs[src], st)
        st_ref[0] = st

    return pl.pallas_call(
        body, name="sb_fwd", grid=(N_PAIRS, nq),
        in_specs=[pl.BlockSpec((bq, LANES), lambda p, i: (i, col0 + p)),
                  pl.BlockSpec(memory_space=pl.ANY),
                  pl.BlockSpec((bq, bq), lambda p, i: (0, 0))],
        out_specs=[pl.BlockSpec((bq, LANES), lambda p, i: (i, p)),
                   pl.BlockSpec((1, bq, 8), lambda p, i: (p, i, 0)),
                   pl.BlockSpec(memory_space=pltpu.SMEM)],
        out_shape=[jax.ShapeDtypeStruct((S, GROUP_W), F32), jax.ShapeDtypeStruct((N_PAIRS, S, 8), F32),
                   jax.ShapeDtypeStruct((N_PAIRS, nq), jnp.int32)],
        scratch_shapes=[pltpu.VMEM((bq, LANES), F32), pltpu.VMEM((bq, LANES), F32),
                        pltpu.VMEM((bq, LANES), BF16), pltpu.VMEM((bq, LANES), BF16),
                        pltpu.VMEM((4, bq, 1), F32),
                        pltpu.VMEM((2, bq, LANES), BF16), pltpu.VMEM((2, bq, LANES), BF16),
                        pltpu.SemaphoreType.DMA((2, 2))],
    )(proj, proj, trev)


def _sb_bwd(proj, col0, do, st, jmin, bq):
    S = proj.shape[0]
    bq = min(bq, S)
    nq = S // bq
    tfwd, trev = _tri_matrices(bq)

    def body(jmin_ref, q_ref, kv_hbm, do_ref, st_ref, tfwd_ref, trev_ref,
             dq_ref, dk_ref, dv_ref, dq_a, dq_b, qa, qb, doa, dob, rs, kbuf, vbuf, sems):
        p, i = pl.program_id(0), pl.program_id(1)
        j0 = jmin_ref[p, i]
        first_two = _first_two_up(lambda pair, blk: jmin_ref[pair, blk])
        fetch = _kv_fetcher(kv_hbm, kbuf, vbuf, sems, col0, bq, p, i, first_two)
        is_a = lax.broadcasted_iota(jnp.int32, (bq, LANES), 1) < HEAD_DIM

        @pl.when(i == 0)
        def _():
            dk_ref[...] = jnp.zeros_like(dk_ref)
            dv_ref[...] = jnp.zeros_like(dv_ref)

        dq_a[...] = jnp.zeros_like(dq_a)
        dq_b[...] = jnp.zeros_like(dq_b)
        rs[...] = jnp.zeros_like(rs)
        st_v = st_ref[0]
        for h in range(2):
            rs[6 + 2 * h], rs[7 + 2 * h] = _col(st_v, h), _col(st_v, 2 + h)
        qa[...], qb[...] = _masked_pair(q_ref[...], is_a, SCALE)
        doa[...], dob[...] = _masked_pair(do_ref[...], is_a)

        def tile(j, slot, masked):
            k, v = kbuf[slot], vbuf[slot]
            tfwd_m, trev_m = tfwd_ref[...], trev_ref[...]
            if masked:
                tri = lax.broadcasted_iota(jnp.int32, (bq, bq), 0) > lax.broadcasted_iota(jnp.int32, (bq, bq), 1)
            hs, qs, dos, dqs = (0, 1), (qa, qb), (doa, dob), (dq_a, dq_b)
            z = {h: _dot(qs[h][...], k, _NT) for h in hs}
            lk = {h: -_softplus(z[h]) for h in hs}
            if masked:
                lk = {h: jnp.where(tri, lk[h], 0.0) for h in hs}
            suf = {h: _split_dot(lk[h], trev_m) for h in hs}
            dw = {h: _dot(dos[h][...], v, _NT) for h in hs}
            pre = {h: _two_sum(rs[3 * h], rs[3 * h + 1], jnp.sum(lk[h], axis=1, keepdims=True)) for h in hs}
            right = {h: (rs[6 + 2 * h] - pre[h][0]) + (rs[7 + 2 * h] - pre[h][1]) for h in hs}
            w = {h: jnp.exp(z[h] + suf[h] + right[h]) for h in hs}
            if masked:
                w = {h: jnp.where(tri, w[h], 0.0) for h in hs}
            g = {h: dw[h] * w[h] for h in hs}
            gpre = {h: _split_dot(g[h], tfwd_m) for h in hs}
            dz = {h: g[h] - jnp.exp(z[h] + lk[h]) * (gpre[h] + rs[3 * h + 2]) for h in hs}
            if masked:
                dz = {h: jnp.where(tri, dz[h], 0.0) for h in hs}
            gtot = {h: jnp.sum(g[h], axis=1, keepdims=True) for h in hs}
            dzb = {h: dz[h].astype(BF16) for h in hs}
            wb = {h: w[h].astype(BF16) for h in hs}
            dqc = {h: _dot(dzb[h], k) for h in hs}
            dkc = {h: _dot(dzb[h], qs[h][...], _TN) for h in hs}
            dvc = {h: _dot(wb[h], dos[h][...], _TN) for h in hs}
            for h in hs:
                rs[3 * h], rs[3 * h + 1] = pre[h]
                rs[3 * h + 2] += gtot[h]
                dqs[h][...] += dqc[h]
            rows = pl.ds(pl.multiple_of(j * bq, bq), bq)
            dk_ref[rows, :] += dkc[0] + dkc[1]
            dv_ref[rows, :] += dvc[0] + dvc[1]

        _walk_up(fetch, j0, i, i, tile)
        _prefetch_next(fetch, p, i, nq, first_two)
        dq_ref[...] = jnp.where(is_a, dq_a[...], dq_b[...]) * SCALE

    grid_spec = pltpu.PrefetchScalarGridSpec(
        num_scalar_prefetch=1, grid=(N_PAIRS, nq),
        in_specs=[pl.BlockSpec((bq, LANES), lambda p, i, jm: (i, col0 + p)),
                  pl.BlockSpec(memory_space=pl.ANY),
                  pl.BlockSpec((bq, LANES), lambda p, i, jm: (i, p)),
                  pl.BlockSpec((1, bq, 8), lambda p, i, jm: (p, i, 0)),
                  pl.BlockSpec((bq, bq), lambda p, i, jm: (0, 0)),
                  pl.BlockSpec((bq, bq), lambda p, i, jm: (0, 0))],
        out_specs=[pl.BlockSpec((bq, LANES), lambda p, i, jm: (i, p)),
                   pl.BlockSpec((S, LANES), lambda p, i, jm: (0, p)),
                   pl.BlockSpec((S, LANES), lambda p, i, jm: (0, p))],
        scratch_shapes=[pltpu.VMEM((bq, LANES), F32), pltpu.VMEM((bq, LANES), F32)]
        + [pltpu.VMEM((bq, LANES), BF16)] * 4 + [pltpu.VMEM((10, bq, 1), F32)]
        + [pltpu.VMEM((KV_SLOTS, bq, LANES), BF16)] * 2 + [pltpu.SemaphoreType.DMA((2, KV_SLOTS))])
    return pl.pallas_call(
        body, name="sb_bwd", grid_spec=grid_spec,
        out_shape=[jax.ShapeDtypeStruct((S, GROUP_W), F32)] * 3,
        compiler_params=_params(VMEM_BIG),
    )(jmin, proj, proj, do, st, tfwd, trev)


def _walk_up(fetch, j0, diag, last, tile):
    ahead = KV_SLOTS - 1

    def start(j):
        @pl.when(j <= last)
        def _():
            for cp in fetch(j, lax.rem(j - j0, KV_SLOTS)):
                cp.start()

    for d in range(2, ahead):
        start(j0 + d)

    def step(j, carry):
        slot = lax.rem(j - j0, KV_SLOTS)
        for cp in fetch(j, slot):
            cp.wait()
        start(j + ahead)
        pl.when(j >= diag)(functools.partial(tile, j, slot, True))
        pl.when(j < diag)(functools.partial(tile, j, slot, False))
        return carry

    lax.fori_loop(j0, last + 1, step, 0)


def _causal(bq, bk, i, j):
    row = lax.broadcasted_iota(jnp.int32, (bq, bk), 0)
    col = lax.broadcasted_iota(jnp.int32, (bq, bk), 1)
    return col - row <= i * bq - j * bk


def _by_heads(j, first_a, first_b, heads):
    on_a, on_b = j >= first_a, j >= first_b
    pl.when(jnp.logical_and(on_a, on_b))(functools.partial(heads, (0, 1)))
    pl.when(jnp.logical_and(on_a, jnp.logical_not(on_b)))(functools.partial(heads, (0,)))
    pl.when(jnp.logical_and(on_b, jnp.logical_not(on_a)))(functools.partial(heads, (1,)))


def _fox_start_blocks(proj, col0, c, bq, bk):
    S = proj.shape[0]
    nq, nk, nh = S // bq, S // bk, 2 * N_PAIRS

    def heads(first):
        return proj[:, first * LANES:(first + N_PAIRS) * LANES].astype(F32).reshape(S, nh, HEAD_DIM)

    q, k = heads(col0), heads(col0 + 4)
    qn = jnp.sqrt(jnp.sum(q * q, axis=-1))
    kmax = jnp.sqrt(jnp.sum(k * k, axis=-1)).max(axis=0)
    top = SCALE * (qn * kmax[None, :] - jnp.sum(q * k, axis=-1)) + c.T
    top = top.reshape(nq, bq, nh).max(axis=1)
    c_last = c[:, bk - 1::bk].T
    live = top[:, None, :] - c_last[None, :, :] >= -FOX_SKIP

    def first_block(lv):
        first = jnp.where(lv.any(axis=1), jnp.argmax(lv, axis=1), nk)
        return jnp.minimum(first, (bq // bk) * jnp.arange(nq)[:, None]).T.astype(jnp.int32)

    return jnp.concatenate([first_block(live.reshape(nq, nk, N_PAIRS, 2).any(axis=-1)), first_block(live)], axis=0)


def _fox_fwd(proj, col0, c_col, c_row, jstart, bq, bk):
    S = proj.shape[0]
    nq, per = S // bq, bq // bk

    def body(js_ref, q_ref, kv_hbm, cc_ref, cr_ref, o_ref, st_ref, acc_a, acc_b, qa, qb, ml, kbuf, vbuf, sems):
        p, i = pl.program_id(0), pl.program_id(1)
        j0 = js_ref[p, i]
        first_two = _first_two_up(lambda pair, blk: js_ref[pair, blk], per)
        fetch = _kv_fetcher(kv_hbm, kbuf, vbuf, sems, col0, bk, p, i, first_two)
        is_a = lax.broadcasted_iota(jnp.int32, (bq, LANES), 1) < HEAD_DIM
        acc_a[...] = jnp.zeros_like(acc_a)
        acc_b[...] = jnp.zeros_like(acc_b)
        ml[0] = jnp.full((bq, 1), NEG_BIG, F32)
        ml[2] = jnp.full((bq, 1), NEG_BIG, F32)
        ml[1] = jnp.zeros((bq, 1), F32)
        ml[3] = jnp.zeros((bq, 1), F32)
        cc = cc_ref[0]
        ml[4], ml[5] = _col(cc, 0), _col(cc, 1)
        qa[...], qb[...] = _masked_pair(q_ref[...], is_a, SCALE)

        def tile(j, slot, masked):
            k, v = kbuf[slot], vbuf[slot]
            cols = pl.ds(pl.multiple_of(j * bk, bk), bk)
            if masked:
                tri = _causal(bq, bk, i, j)

            def heads(hs):
                qs, accs = (qa, qb), (acc_a, acc_b)
                s = {h: _dot(qs[h][...], k, _NT) - cr_ref[0, pl.ds(h, 1), cols] for h in hs}
                if masked:
                    s = {h: jnp.where(tri, s[h], NEG_BIG) for h in hs}
                top = {h: jnp.max(s[h], axis=1, keepdims=True) for h in hs}
                m_new = {h: jnp.maximum(ml[2 * h], top[h] + ml[4 + h]) for h in hs}
                a = {h: jnp.exp(ml[2 * h] - m_new[h]) for h in hs}
                pr = {h: jnp.exp(s[h] - (m_new[h] - ml[4 + h])) for h in hs}
                tot = {h: jnp.sum(pr[h], axis=1, keepdims=True) for h in hs}
                pv = {h: _dot(pr[h].astype(BF16), v) for h in hs}
                for h in hs:
                    ml[2 * h] = m_new[h]
                    ml[2 * h + 1] = a[h] * ml[2 * h + 1] + tot[h]
                    accs[h][...] = a[h] * accs[h][...] + pv[h]

            _by_heads(j, js_ref[N_PAIRS + 2 * p, i], js_ref[N_PAIRS + 2 * p + 1, i], heads)

        _walk_up(fetch, j0, per * i, per * i + per - 1, tile)
        _prefetch_next(fetch, p, i, nq, first_two)
        o_ref[...] = jnp.where(is_a, acc_a[...] / ml[1], acc_b[...] / ml[3])
        lane8 = lax.broadcasted_iota(jnp.int32, (bq, 8), 1)
        st = jnp.where(lane8 == 0, ml[0] + jnp.log(ml[1]), 0.0)
        st_ref[0] = jnp.where(lane8 == 1, ml[2] + jnp.log(ml[3]), st)

    grid_spec = pltpu.PrefetchScalarGridSpec(
        num_scalar_prefetch=1, grid=(N_PAIRS, nq),
        in_specs=[pl.BlockSpec((bq, LANES), lambda p, i, js: (i, col0 + p)),
                  pl.BlockSpec(memory_space=pl.ANY),
                  pl.BlockSpec((1, bq, 8), lambda p, i, js: (p, i, 0)),
                  pl.BlockSpec((1, 8, S), lambda p, i, js: (p, 0, 0))],
        out_specs=[pl.BlockSpec((bq, LANES), lambda p, i, js: (i, p)),
                   pl.BlockSpec((1, bq, 8), lambda p, i, js: (p, i, 0))],
        scratch_shapes=[pltpu.VMEM((bq, LANES), F32), pltpu.VMEM((bq, LANES), F32),
                        pltpu.VMEM((bq, LANES), BF16), pltpu.VMEM((bq, LANES), BF16),
                        pltpu.VMEM((6, bq, 1), F32),
                        pltpu.VMEM((KV_SLOTS, bk, LANES), BF16), pltpu.VMEM((KV_SLOTS, bk, LANES), BF16),
                        pltpu.SemaphoreType.DMA((2, KV_SLOTS))])
    return pl.pallas_call(
        body, name="fox_fwd", grid_spec=grid_spec,
        out_shape=[jax.ShapeDtypeStruct((S, GROUP_W), F32), jax.ShapeDtypeStruct((N_PAIRS, S, 8), F32)],
    )(jstart, proj, proj, c_col, c_row)


def _fox_bwd(proj, col0, do, o, st, c_col, c_row, jstart, bq, bk):
    S = proj.shape[0]
    nq, per = S // bq, bq // bk

    def body(js_ref, q_ref, kv_hbm, do_ref, o_ref, st_ref, cc_ref, cr_ref,
             dq_ref, dk_ref, dv_ref, dc_ref, dcq_ref, dq_a, dq_b, qa, qb, doa, dob, dd, kbuf, vbuf, sems):
        p, i = pl.program_id(0), pl.program_id(1)
        j0 = js_ref[p, i]
        first_two = _first_two_up(lambda pair, blk: js_ref[pair, blk], per)
        fetch = _kv_fetcher(kv_hbm, kbuf, vbuf, sems, col0, bk, p, i, first_two)
        is_a = lax.broadcasted_iota(jnp.int32, (bq, LANES), 1) < HEAD_DIM

        @pl.when(i == 0)
        def _():
            dk_ref[...] = jnp.zeros_like(dk_ref)
            dv_ref[...] = jnp.zeros_like(dv_ref)
            dc_ref[...] = jnp.zeros_like(dc_ref)

        dq_a[...] = jnp.zeros_like(dq_a)
        dq_b[...] = jnp.zeros_like(dq_b)
        qa[...], qb[...] = _masked_pair(q_ref[...], is_a, SCALE)
        dov = do_ref[...]
        doa[...], dob[...] = _masked_pair(dov, is_a)
        prod = dov * o_ref[...]
        dd[0] = jnp.sum(jnp.where(is_a, prod, 0.0), axis=1, keepdims=True)
        dd[1] = jnp.sum(jnp.where(is_a, 0.0, prod), axis=1, keepdims=True)
        dd[2] = jnp.zeros((bq, 1), F32)
        dd[3] = jnp.zeros((bq, 1), F32)
        cc, st_v = cc_ref[0], st_ref[0]
        dd[4], dd[5] = _col(cc, 0) - _col(st_v, 0), _col(cc, 1) - _col(st_v, 1)

        def tile(j, slot, masked):
            k, v = kbuf[slot], vbuf[slot]
            if masked:
                tri = _causal(bq, bk, i, j)
            cols = pl.ds(pl.multiple_of(j * bk, bk), bk)

            def heads(hs):
                qs, dos, dqs = (qa, qb), (doa, dob), (dq_a, dq_b)
                z = {h: _dot(qs[h][...], k, _NT) for h in hs}
                dp = {h: _dot(dos[h][...], v, _NT) for h in hs}
                pr = {h: jnp.exp(z[h] - cr_ref[0, pl.ds(h, 1), cols] + dd[4 + h]) for h in hs}
                if masked:
                    pr = {h: jnp.where(tri, pr[h], 0.0) for h in hs}
                ds = {h: pr[h] * (dp[h] - dd[h]) for h in hs}
                csum = {h: jnp.sum(ds[h], axis=0, keepdims=True) for h in hs}
                rsum = {h: jnp.sum(ds[h], axis=1, keepdims=True) for h in hs}
                dsb = {h: ds[h].astype(BF16) for h in hs}
                prb = {h: pr[h].astype(BF16) for h in hs}
                dqc = {h: _dot(dsb[h], k) for h in hs}
                dkc = [_dot(dsb[h], qs[h][...], _TN) for h in hs]
                dvc = [_dot(prb[h], dos[h][...], _TN) for h in hs]
                for h in hs:
                    dc_ref[0, pl.ds(h, 1), cols] -= csum[h]
                    dd[2 + h] += rsum[h]
                    dqs[h][...] += dqc[h]
                dk_ref[cols, :] += sum(dkc[1:], dkc[0])
                dv_ref[cols, :] += sum(dvc[1:], dvc[0])

            _by_heads(j, js_ref[N_PAIRS + 2 * p, i], js_ref[N_PAIRS + 2 * p + 1, i], heads)

        _walk_up(fetch, j0, per * i, per * i + per - 1, tile)
        _prefetch_next(fetch, p, i, nq, first_two)
        dq_ref[...] = jnp.where(is_a, dq_a[...], dq_b[...]) * SCALE
        lane8 = lax.broadcasted_iota(jnp.int32, (bq, 8), 1)
        dcq_ref[0] = jnp.where(lane8 == 0, dd[2], jnp.where(lane8 == 1, dd[3], 0.0))

    grid_spec = pltpu.PrefetchScalarGridSpec(
        num_scalar_prefetch=1, grid=(N_PAIRS, nq),
        in_specs=[pl.BlockSpec((bq, LANES), lambda p, i, js: (i, col0 + p)),
                  pl.BlockSpec(memory_space=pl.ANY),
                  pl.BlockSpec((bq, LANES), lambda p, i, js: (i, p)),
                  pl.BlockSpec((bq, LANES), lambda p, i, js: (i, p)),
                  pl.BlockSpec((1, bq, 8), lambda p, i, js: (p, i, 0)),
                  pl.BlockSpec((1, bq, 8), lambda p, i, js: (p, i, 0)),
                  pl.BlockSpec((1, 8, S), lambda p, i, js: (p, 0, 0))],
        out_specs=[pl.BlockSpec((bq, LANES), lambda p, i, js: (i, p)),
                   pl.BlockSpec((S, LANES), lambda p, i, js: (0, p)),
                   pl.BlockSpec((S, LANES), lambda p, i, js: (0, p)),
                   pl.BlockSpec((1, 8, S), lambda p, i, js: (p, 0, 0)),
                   pl.BlockSpec((1, bq, 8), lambda p, i, js: (p, i, 0))],
        scratch_shapes=[pltpu.VMEM((bq, LANES), F32), pltpu.VMEM((bq, LANES), F32)]
        + [pltpu.VMEM((bq, LANES), BF16)] * 4 + [pltpu.VMEM((6, bq, 1), F32)]
        + [pltpu.VMEM((KV_SLOTS, bk, LANES), BF16)] * 2 + [pltpu.SemaphoreType.DMA((2, KV_SLOTS))])
    return pl.pallas_call(
        body, name="fox_bwd", grid_spec=grid_spec,
        out_shape=[jax.ShapeDtypeStruct((S, GROUP_W), F32)] * 3
        + [jax.ShapeDtypeStruct((N_PAIRS, 8, S), F32), jax.ShapeDtypeStruct((N_PAIRS, S, 8), F32)],
        compiler_params=_params(VMEM_BIG),
    )(jstart, proj, proj, do, o, st, c_col, c_row)


_HBM = pl.BlockSpec(memory_space=pltpu.HBM)


def _coords():
    return lax.axis_index("x"), lax.axis_index("y"), lax.axis_index("c")


def _gather_copies(ins, outs, send_sems, recv_sems, loc_sems):
    n = len(ins)
    x, y, c = _coords()
    mine = 2 * x + y
    chips = [(1 - x, y), (x, 1 - y), (1 - x, 1 - y)]

    def copy(w, r, slab, to):
        return pltpu.make_async_remote_copy(
            src_ref=ins[w], dst_ref=outs[w].at[slab], send_sem=send_sems.at[3 * w + r],
            recv_sem=recv_sems.at[3 * w + r], device_id=to, device_id_type=MESH)

    def own():
        local = [pltpu.make_async_copy(ins[w], outs[w].at[mine], loc_sems.at[w]) for w in range(n)]
        return local, [copy(w, r, mine, (cx, cy, c)) for w in range(n) for r, (cx, cy) in enumerate(chips)]

    def start():
        local, sends = own()
        for cp in local + sends:
            cp.start()

    def wait():
        local, sends = own()
        for w in range(n):
            for r, (cx, cy) in enumerate(chips):
                copy(w, r, 2 * cx + cy, (cx, cy, c)).wait_recv()
        for cp in sends:
            cp.wait_send()
        for cp in local:
            cp.wait()

    return start, wait


def _gather_shapes(shards):
    n = len(shards)
    return ([jax.ShapeDtypeStruct((4,) + s.shape, s.dtype) for s in shards],
            [pltpu.SemaphoreType.DMA((3 * n,)), pltpu.SemaphoreType.DMA((3 * n,)), pltpu.SemaphoreType.DMA((n,))])


def _allgather_chips(shards):
    n = len(shards)

    def body(*refs):
        start, wait = _gather_copies(refs[:n], refs[n:2 * n], *refs[2 * n:])
        start()
        wait()

    out_shape, sems = _gather_shapes(shards)
    return pl.pallas_call(body, name="allgather_weights", in_specs=[_HBM] * n, out_specs=[_HBM] * n,
                          out_shape=out_shape, scratch_shapes=sems)(*shards)


def _proj_gather(x, w, shards, tm, tn):
    (M, K), N, n = x.shape, w.shape[1], len(shards)
    tm = min(tm, M)
    gi, gj = M // tm, N // tn

    def body(a_ref, b_ref, *rest):
        o_ref = rest[n]
        start, wait = _gather_copies(rest[:n], rest[n + 1:2 * n + 1], *rest[2 * n + 1:])
        i, j = pl.program_id(0), pl.program_id(1)
        pl.when(jnp.logical_and(i == 0, j == 0))(start)
        o_ref[...] = _dot(a_ref[...].astype(BF16), b_ref[...]).astype(o_ref.dtype)
        pl.when(jnp.logical_and(i == gi - 1, j == gj - 1))(wait)

    out_shape, sems = _gather_shapes(shards)
    return pl.pallas_call(
        body, name="proj_gather", grid=(gi, gj),
        in_specs=[pl.BlockSpec((tm, K), lambda i, j: (i, 0)), pl.BlockSpec((K, tn), lambda i, j: (0, j))] + [_HBM] * n,
        out_specs=[pl.BlockSpec((tm, tn), lambda i, j: (i, j))] + [_HBM] * n,
        out_shape=[jax.ShapeDtypeStruct((M, N), BF16)] + out_shape, scratch_shapes=sems,
    )(x, w, *shards)


def _exchange(parts, per_chip):
    n = len(parts)
    half = [p.shape[1] // 2 for p in parts] if per_chip else None

    def body(*refs):
        ins, outs = refs[:n], refs[n:2 * n]
        send_sems, recv_sems, loc_sems = refs[2 * n:]
        x, y, c = _coords()
        me = 4 * x + 2 * y + c
        peers = [(x ^ fx, y ^ fy, c ^ fc) for fx in (0, 1) for fy in (0, 1) for fc in (0, 1)][1:]

        def src(w, dev):
            if not per_chip:
                return ins[w]
            return ins[w].at[2 * dev[0] + dev[1], pl.ds(pl.multiple_of(dev[2] * half[w], 16), half[w]), :]

        local = [pltpu.make_async_copy(src(w, (x, y, c)), outs[w].at[me], loc_sems.at[w]) for w in range(n)]
        for cp in local:
            cp.start()

        def copy(w, r, source, slab, to):
            return pltpu.make_async_remote_copy(
                src_ref=source, dst_ref=outs[w].at[slab], send_sem=send_sems.at[7 * w + r],
                recv_sem=recv_sems.at[7 * w + r], device_id=to, device_id_type=MESH)

        sends = [copy(w, r, src(w, dev), me, dev) for w in range(n) for r, dev in enumerate(peers)]
        for cp in sends:
            cp.start()
        for w in range(n):
            for r, dev in enumerate(peers):
                copy(w, r, src(w, dev), 4 * dev[0] + 2 * dev[1] + dev[2], dev).wait_recv()
        for cp in sends:
            cp.wait_send()
        for cp in local:
            cp.wait()

    return pl.pallas_call(
        body, name="exchange_per_chip" if per_chip else "exchange_all",
        in_specs=[_HBM] * n, out_specs=[_HBM] * n,
        out_shape=[jax.ShapeDtypeStruct((8, half[w], p.shape[2]) if per_chip else (8,) + p.shape, p.dtype)
                   for w, p in enumerate(parts)],
        scratch_shapes=[pltpu.SemaphoreType.DMA((7 * n,)), pltpu.SemaphoreType.DMA((7 * n,)),
                        pltpu.SemaphoreType.DMA((n,))],
    )(*parts)


def _sibling_swap(halves):
    n = len(halves)

    def body(*refs):
        ins, outs = refs[:n], refs[n:2 * n]
        send_sems, recv_sems, loc_sems = refs[2 * n:]
        x, y, c = _coords()

        def rows(w, core):
            rh = halves[w].shape[0]
            return outs[w].at[pl.ds(pl.multiple_of(core * rh, 8), rh), :]

        def copy(w, core):
            return pltpu.make_async_remote_copy(
                src_ref=ins[w], dst_ref=rows(w, core), send_sem=send_sems.at[w], recv_sem=recv_sems.at[w],
                device_id=(x, y, 1 - c), device_id_type=MESH)

        local = [pltpu.make_async_copy(ins[w], rows(w, c), loc_sems.at[w]) for w in range(n)]
        sends = [copy(w, c) for w in range(n)]
        for cp in local + sends:
            cp.start()
        for w in range(n):
            copy(w, 1 - c).wait_recv()
        for cp in sends:
            cp.wait_send()
        for cp in local:
            cp.wait()

    vmem = pl.BlockSpec(memory_space=pltpu.VMEM)
    return pl.pallas_call(
        body, name="sibling_swap", in_specs=[vmem] * n, out_specs=[vmem] * n,
        out_shape=[jax.ShapeDtypeStruct((2 * h.shape[0], h.shape[1]), h.dtype) for h in halves],
        scratch_shapes=[pltpu.SemaphoreType.DMA((n,)), pltpu.SemaphoreType.DMA((n,)), pltpu.SemaphoreType.DMA((n,))],
    )(*halves)


def _adamw(w, g, m, v):
    m = ADAM_B1 * m + (1.0 - ADAM_B1) * g
    v = ADAM_B2 * v + (1.0 - ADAM_B2) * (g * g)
    m_hat = m / (1.0 - ADAM_B1 ** ADAM_STEP)
    v_hat = v / (1.0 - ADAM_B2 ** ADAM_STEP)
    delta = -ADAM_LR * (m_hat / (jnp.sqrt(v_hat) + ADAM_EPS) + ADAM_WD * w)
    return delta, m, v


def _sum_parts(parts, name, tr):
    _, R, C = parts.shape
    assert R % tr == 0

    def body(p_ref, g_ref):
        g = p_ref[0].astype(F32)
        for d in range(1, 8):
            g = g + p_ref[d].astype(F32)
        g_ref[...] = g

    return pl.pallas_call(
        body, name=name, grid=(R // tr,),
        in_specs=[pl.BlockSpec((8, tr, C), lambda i: (0, i, 0))],
        out_specs=pl.BlockSpec((tr, C), lambda i: (i, 0)), out_shape=jax.ShapeDtypeStruct((R, C), F32),
    )(parts)


def _adamw_call(g, w, m, v, name, tr):
    R, C = w.shape
    assert R % tr == 0

    def body(g_ref, w_ref, m_ref, v_ref, d_ref, nm_ref, nv_ref):
        d_ref[...], nm_ref[...], nv_ref[...] = _adamw(w_ref[...], g_ref[...], m_ref[...], v_ref[...])

    tile = pl.BlockSpec((tr, C), lambda i: (i, 0))
    return pl.pallas_call(
        body, name=name, grid=(R // tr,), in_specs=[tile] * 4,
        out_specs=[tile] * 3, out_shape=[jax.ShapeDtypeStruct((R, C), F32)] * 3,
    )(g, w, m, v)


def _sum_adamw_small(parts, w, m, v):
    def body(p_ref, w_ref, m_ref, v_ref, g_ref, d_ref, nm_ref, nv_ref, loss_ref):
        g = p_ref[0]
        for d in range(1, 8):
            g = g + p_ref[d]
        g_ref[...] = g
        d_ref[...], nm_ref[...], nv_ref[...] = _adamw(w_ref[...], g, m_ref[...], v_ref[...])
        row = lax.broadcasted_iota(jnp.int32, g.shape, 0)
        per_row = jnp.sum(jnp.where(row == 6, g, 0.0), axis=1, keepdims=True)
        loss_ref[...] = jnp.zeros((8, LANES), F32) + jnp.sum(per_row, axis=0, keepdims=True)

    return pl.pallas_call(
        body, name="sum_adamw_small",
        out_shape=[jax.ShapeDtypeStruct((8, D_MODEL), F32)] * 4 + [jax.ShapeDtypeStruct((8, LANES), F32)],
    )(parts, w, m, v)


def _pack_small(ln1_g, ln1_b, ln2_g, ln2_b, g_sb, g_fox, b_f):
    row5 = jnp.pad(b_f.reshape(1, N_FOX), ((0, 0), (0, D_MODEL - N_FOX)))
    rows = [ln1_g.reshape(1, -1), ln1_b.reshape(1, -1), ln2_g.reshape(1, -1), ln2_b.reshape(1, -1),
            jnp.concatenate([g_sb.reshape(1, -1), g_fox.reshape(1, -1)], axis=1), row5,
            jnp.zeros((2, D_MODEL), F32)]
    return jnp.concatenate(rows, axis=0)


def _unpack_small(p):
    return {"ln1_g": p[0:1], "ln1_b": p[1:2], "ln2_g": p[2:3], "ln2_b": p[3:4], "g_sb": p[4:5, :GROUP_W],
            "g_fox": p[4:5, GROUP_W:], "b_f": p[5:6, :N_FOX]}


def kernel(x, w_in, b_f, g_sb, g_fox, w_out, ln1_g, ln1_b, ln2_g, ln2_b, w_gate_up, w_down, loss_target, m_w_in, m_b_f, m_g_sb, m_g_fox, m_w_out, m_ln1_g, m_ln1_b, m_ln2_g, m_ln2_b, m_w_gate_up, m_w_down, v_w_in, v_b_f, v_g_sb, v_g_fox, v_w_out, v_ln1_g, v_ln1_b, v_ln2_g, v_ln2_b, v_w_gate_up, v_w_down):
    S = x.shape[1]
    x2 = x.reshape(S, D_MODEL)
    tgt = loss_target.reshape(S, D_MODEL)
    TM = 1024
    TR = 512
    BQ = ATTN_BLOCK
    in_w = w_in.shape[2]
    gu_w = w_gate_up.shape[2]

    shards = [w_in[0].astype(BF16), w_out[0].astype(BF16), w_gate_up[0].astype(BF16), w_down[0].astype(BF16)]
    (wi_s,) = _allgather_chips(shards[:1])
    wi = wi_s.transpose(1, 0, 2).reshape(D_MODEL, 4 * in_w)
    w_sb, w_fx = wi[:, :QKV_W // 2], wi[:, QKV_W // 2:QKV_W]
    wqkv = wi[:, :QKV_W]
    wft = wi[:, QKV_W:].T
    proj, wo_s, wgu_s, wd_s = _proj_gather(x2, wqkv, shards[1:], TM, 512)
    wo = wo_s.reshape(D_MODEL, D_MODEL)
    wgu = wgu_s.transpose(1, 0, 2).reshape(D_MODEL, 2 * D_FF)
    wg, wu = wgu[:, :D_FF], wgu[:, D_FF:]
    wd = wd_s.reshape(D_FF, D_MODEL)
    g_row = jnp.concatenate([g_sb, g_fox], axis=1)
    hid = np.arange(D_MODEL) // HEAD_DIM
    he_np = (hid[:, None] == np.arange(LANES)[None, :]).astype(np.float32)
    he, het = jnp.asarray(he_np, BF16), jnp.asarray(he_np.T, BF16)

    lf = _fgate_fwd(x2, wft, b_f.reshape(N_FOX, 1), TM)
    c = _cumsum_fwd(lf)
    c_pair = c.reshape(N_PAIRS, 2, S)
    c_row = jnp.pad(c_pair, ((0, 0), (0, 6), (0, 0)))
    c_col = jnp.pad(c_pair.transpose(0, 2, 1), ((0, 0), (0, 0), (0, 6)))

    o_sb, st_sb, jmin_sb = _sb_fwd(proj, 0, BQ)
    jstart_fx = _fox_start_blocks(proj, 12, c, BQ, BQ)
    o_fx, st_fx = _fox_fwd(proj, 12, c_col, c_row, jstart_fx, BQ, BQ)

    def attn_post(i, osb_ref, ofx_ref, g_ref, he_ref, het_ref, on_ref):
        o = jnp.concatenate([osb_ref[...], ofx_ref[...]], axis=1)
        ms = _head_sums(o * o, he_ref[...], het_ref[...]) * (1.0 / HEAD_DIM)
        on_ref[...] = (o * lax.rsqrt(ms + RMS_EPS) * g_ref[...]).astype(BF16)

    (on,) = _rowwise(attn_post, "attn_post", S, TR,
                     [(o_sb, "t"), (o_fx, "t"), (g_row, "f"), (he, "f"), (het, "f")],
                     [((S, D_MODEL), BF16, "t")])

    u1 = _matmul(on, wo, mode="nn", name="mix", tm=TM, tn=D_MODEL, tk=D_MODEL, outs=[F32],
                 extras=[(x2, (TM if S >= TM else S, D_MODEL), _tile_ij)],
                 epilogue=lambda acc, xv: (ALPHA * xv + acc,))

    def ln1_fwd(i, u_ref, g_ref, b_ref, h_ref):
        xh, _ = _ln_stats(u_ref[...])
        h_ref[...] = xh * g_ref[...] + b_ref[...]

    (h1,) = _rowwise(ln1_fwd, "ln1_fwd", S, TR, [(u1, "t"), (ln1_g, "f"), (ln1_b, "f")], [((S, D_MODEL), F32, "t")])

    tm_e = TM if S >= TM else S
    n_ff = D_FF // 256

    def gate_up_body(h_ref, wg_ref, wu_ref, g_ref, u_ref, a_ref):
        h = h_ref[...].astype(BF16)
        g, u = _dot(h, wg_ref[...]), _dot(h, wu_ref[...])
        g_ref[...] = g.astype(BF16)
        u_ref[...] = u.astype(BF16)
        a_ref[...] = (g / (1.0 + jnp.exp(-g)) * u).astype(BF16)

    ff_tile = pl.BlockSpec((tm_e, 256), lambda i, j: (i, j))
    gate, up, act = pl.pallas_call(
        gate_up_body, name="gate_up_act", grid=(S // tm_e, n_ff),
        in_specs=[pl.BlockSpec((tm_e, D_MODEL), lambda i, j: (i, 0)),
                  pl.BlockSpec((D_MODEL, 256), lambda i, j: (0, j)),
                  pl.BlockSpec((D_MODEL, 256), lambda i, j: (0, j + n_ff))],
        out_specs=[ff_tile] * 3, out_shape=[jax.ShapeDtypeStruct((S, D_FF), BF16)] * 3)(h1, wgu, wgu)

    u2 = _matmul(act, wd, mode="nn", name="ffn_down", tm=TM, tn=D_MODEL, tk=D_FF, outs=[F32],
                 extras=[(h1, (TM if S >= TM else S, D_MODEL), _tile_ij)],
                 epilogue=lambda acc, hv: (ALPHA * hv + acc,))

    def ln2_loss(i, u_ref, t_ref, g_ref, b_ref, du_ref, acc_ref):
        xh, r = _ln_stats(u_ref[...])
        g = g_ref[...]
        err = xh * g + b_ref[...] - t_ref[...]
        dy = err * (1.0 / D_MODEL)
        du_ref[...] = _ln_bwd(dy, xh, r, g)
        _acc_rows(i, acc_ref, {2: jnp.sum(dy * xh, axis=0, keepdims=True), 3: jnp.sum(dy, axis=0, keepdims=True),
                               6: jnp.sum(err * err, axis=0, keepdims=True) * (0.5 / D_MODEL)})

    du2, acc_ln2 = _rowwise(ln2_loss, "ln2_loss", S, TR, [(u2, "t"), (tgt, "t"), (ln2_g, "f"), (ln2_b, "f")],
                            [((S, D_MODEL), F32, "t"), ((8, D_MODEL), F32, "f")])

    d_wd = _matmul(act, du2, mode="tn", name="dw_down", tm=1408, tn=D_MODEL, tk=TM, outs=[BF16])

    def dgu_epilogue(da, g, u):
        g, u = g.astype(F32), u.astype(F32)
        s = 1.0 / (1.0 + jnp.exp(-g))
        return da * u * (s * (1.0 + g * (1.0 - s))), da * (g * s)

    dgate, dup = _matmul(du2, wd, mode="nt", name="d_act", tm=TM, tn=1408, tk=D_MODEL, outs=[BF16, BF16],
                         extras=[(gate, (tm_e, 1408), _tile_ij), (up, (tm_e, 1408), _tile_ij)],
                         epilogue=dgu_epilogue)
    d_wg = _matmul(h1, dgate, mode="tn", name="dw_gate", tm=D_MODEL, tn=1408, tk=TM, outs=[BF16])
    d_wu = _matmul(h1, dup, mode="tn", name="dw_up", tm=D_MODEL, tn=1408, tk=TM, outs=[BF16])
    dh1 = _matmul(dgate, wg, mode="nt", name="dh1_gate", tm=TM, tn=D_MODEL, tk=D_FF, outs=[F32],
                  extras=[(du2, (tm_e, D_MODEL), _tile_ij)], epilogue=lambda acc, e: (ALPHA * e + acc,))
    dh1 = _matmul(dup, wu, mode="nt", name="dh1_up", tm=TM, tn=D_MODEL, tk=D_FF, outs=[F32],
                  extras=[(dh1, (tm_e, D_MODEL), _tile_ij)], epilogue=lambda acc, e: (e + acc,))

    def ln1_bwd(i, dh_ref, u_ref, g_ref, du_ref, acc_ref):
        xh, r = _ln_stats(u_ref[...])
        dh = dh_ref[...]
        du_ref[...] = _ln_bwd(dh, xh, r, g_ref[...])
        _acc_rows(i, acc_ref, {0: jnp.sum(dh * xh, axis=0, keepdims=True), 1: jnp.sum(dh, axis=0, keepdims=True)})

    du1, acc_ln1 = _rowwise(ln1_bwd, "ln1_bwd", S, TR, [(dh1, "t"), (u1, "t"), (ln1_g, "f")],
                            [((S, D_MODEL), F32, "t"), ((8, D_MODEL), F32, "f")])
    d_wo = _matmul(on, du1, mode="tn", name="dw_out", tm=D_MODEL, tn=D_MODEL, tk=TM, outs=[BF16])
    don = _matmul(du1, wo, mode="nt", name="d_on", tm=TM, tn=D_MODEL, tk=D_MODEL, outs=[F32])

    def rms_bwd(i, don_ref, osb_ref, ofx_ref, g_ref, he_ref, het_ref, dosb_ref, dofx_ref, acc_ref):
        o = jnp.concatenate([osb_ref[...], ofx_ref[...]], axis=1)
        hev, hetv = he_ref[...], het_ref[...]
        r = lax.rsqrt(_head_sums(o * o, hev, hetv) * (1.0 / HEAD_DIM) + RMS_EPS)
        dn = don_ref[...]
        dg = dn * g_ref[...]
        do = r * dg - o * (r * r * r) * (_head_sums(dg * o, hev, hetv) * (1.0 / HEAD_DIM))
        dosb_ref[...] = do[:, :GROUP_W]
        dofx_ref[...] = do[:, GROUP_W:]
        _acc_rows(i, acc_ref, {4: jnp.sum(dn * o * r, axis=0, keepdims=True)})

    do_sb, do_fx, acc_rms = _rowwise(
        rms_bwd, "rms_bwd", S, TR, [(don, "t"), (o_sb, "t"), (o_fx, "t"), (g_row, "f"), (he, "f"), (het, "f")],
        [((S, GROUP_W), F32, "t"), ((S, GROUP_W), F32, "t"), ((8, D_MODEL), F32, "f")])

    dq_sb, dk_sb, dv_sb = _sb_bwd(proj, 0, do_sb, st_sb, jmin_sb, BQ)
    jstart_fx2 = jnp.minimum(jstart_fx[:, 0::2], jstart_fx[:, 1::2])
    dq_fx, dk_fx, dv_fx, dc, dcq = _fox_bwd(proj, 12, do_fx, o_fx, st_fx, c_col, c_row, jstart_fx2, 2 * BQ, BQ)
    dc = dc[:, :2, :] + dcq[:, :, :2].transpose(0, 2, 1)
    dfl, dbf = _fgate_bwd(dc.reshape(N_FOX, S), lf)
    dp_sb = jnp.concatenate([dq_sb, dk_sb, dv_sb], axis=1).astype(BF16)
    dp_fx = jnp.concatenate([dq_fx, dk_fx, dv_fx], axis=1).astype(BF16)

    d_wsb = _matmul(x2, dp_sb, mode="tn", name="dw_in_sb", tm=D_MODEL, tn=QKV_W // 2, tk=TM, outs=[BF16])
    d_wfx = _matmul(x2, dp_fx, mode="tn", name="dw_in_fx", tm=D_MODEL, tn=QKV_W // 2, tk=TM, outs=[BF16])
    d_wft = _matmul(dfl, x2, mode="nn", name="dw_in_f", tm=N_FOX, tn=D_MODEL, tk=TM, outs=[BF16])
    dx = _matmul(dp_sb, w_sb, mode="nt", name="dx_sb", tm=TM, tn=D_MODEL, tk=QKV_W // 2, outs=[F32],
                 extras=[(du1, (tm_e, D_MODEL), _tile_ij)], epilogue=lambda acc, e: (ALPHA * e + acc,))
    dx = _matmul(dp_fx, w_fx, mode="nt", name="dx_fx", tm=TM, tn=D_MODEL, tk=QKV_W // 2, outs=[F32],
                 extras=[(dx, (tm_e, D_MODEL), _tile_ij)], epilogue=lambda acc, e: (e + acc,))
    dx = _matmul(dfl, wft, mode="tn", name="dx_f", tm=TM, tn=D_MODEL, tk=N_FOX, outs=[F32],
                 extras=[(dx, (tm_e, D_MODEL), _tile_ij)], epilogue=lambda acc, e: (e + acc,))

    d_wi = jnp.concatenate([d_wsb, d_wfx, d_wft.T], axis=1)
    d_wgu = jnp.concatenate([d_wg, d_wu], axis=1)
    parts = [d_wi.reshape(D_MODEL, 4, in_w).transpose(1, 0, 2).astype(BF16),
             d_wo.reshape(4, D_MODEL // 4, D_MODEL).astype(BF16),
             d_wgu.reshape(D_MODEL, 4, gu_w).transpose(1, 0, 2).astype(BF16),
             d_wd.reshape(4, D_FF // 4, D_MODEL).astype(BF16)]
    got = _exchange(parts, True)
    big_names = ("w_in", "w_out", "w_gate_up", "w_down")
    halves = [_sum_parts(p, "sum_" + nm, tr) for nm, p, tr in zip(big_names, got, (256, 128, 128, 176))]
    grads = _sibling_swap(halves)
    big = {}
    for nm, g, w, m, v, tr in zip(big_names, grads, (w_in, w_out, w_gate_up, w_down),
                                  (m_w_in, m_w_out, m_w_gate_up, m_w_down),
                                  (v_w_in, v_w_out, v_w_gate_up, v_w_down), (256, 256, 256, 176)):
        big[nm] = [r[None] for r in [g] + list(_adamw_call(g, w[0], m[0], v[0], "adamw_" + nm, tr))]

    small = acc_ln2 + acc_ln1 + acc_rms
    small = small + jnp.pad(dbf.reshape(1, N_FOX), ((5, 2), (0, D_MODEL - N_FOX)))
    (small_all,) = _exchange([small], False)
    sw = _pack_small(ln1_g, ln1_b, ln2_g, ln2_b, g_sb, g_fox, b_f)
    sm = _pack_small(m_ln1_g, m_ln1_b, m_ln2_g, m_ln2_b, m_g_sb, m_g_fox, m_b_f)
    sv = _pack_small(v_ln1_g, v_ln1_b, v_ln2_g, v_ln2_b, v_g_sb, v_g_fox, v_b_f)
    sg, sd, snm, snv, loss_blk = _sum_adamw_small(small_all, sw, sm, sv)
    sg, sd, snm, snv = _unpack_small(sg), _unpack_small(sd), _unpack_small(snm), _unpack_small(snv)

    names = ["w_in", "b_f", "g_sb", "g_fox", "w_out", "ln1_g", "ln1_b", "ln2_g", "ln2_b", "w_gate_up", "w_down"]
    outs = [loss_blk[0, 0], dx.reshape(1, S, D_MODEL)]
    for k, table in enumerate((sg, sd, snm, snv)):
        outs += [big[n][k] if n in big else table[n] for n in names]
    return tuple(outs)
```

```python
import functools

import numpy as np
import jax
import jax.numpy as jnp
from jax import lax
from jax.experimental import pallas as pl
from jax.experimental.pallas import tpu as pltpu

F32 = jnp.float32
BF16 = jnp.bfloat16

D_MODEL = 1024
HEAD_DIM = 64
LANES = 128
N_PAIRS = 4
GROUP_W = 512
QKV_W = 3072
D_FF = 2816
N_FOX = 8
ALPHA = 2.0 ** 0.25
LN_EPS = 1e-5
RMS_EPS = 1e-6
SCALE = HEAD_DIM ** -0.5
NEG_BIG = -1e30
FOX_SKIP = 30.0
SB_STOP = -105.0
ADAM_LR, ADAM_B1, ADAM_B2, ADAM_EPS, ADAM_WD, ADAM_STEP = 0.001, 0.9, 0.999, 1e-08, 0.01, 10
STEP_BLOCKS = 2
KV_SLOTS = 4
SCAN_GROUP = 8
ATTN_BLOCK = 256
VMEM_BIG = 56 * 1024 * 1024
MESH = pl.DeviceIdType.MESH

_NN = (((1,), (0,)), ((), ()))
_NT = (((1,), (1,)), ((), ()))
_TN = (((0,), (0,)), ((), ()))


def _dot(a, b, dims=_NN):
    return lax.dot_general(a, b, dims, preferred_element_type=F32)


def _split_dot(x, t):
    hi = x.astype(BF16)
    lo = (x - hi.astype(F32)).astype(BF16)
    return _dot(hi, t) + _dot(lo, t)


def _softplus(z):
    return jnp.maximum(z, 0.0) + jnp.log1p(jnp.exp(-jnp.abs(z)))


def _col(v, h):
    lane = lax.broadcasted_iota(jnp.int32, v.shape, 1)
    return jnp.sum(jnp.where(lane == h, v, 0.0), axis=1, keepdims=True)


def _two_sum(hi, lo, b):
    s = hi + b
    bb = s - hi
    err = (hi - (s - bb)) + (b - bb)
    return s, lo + err


def _params(vmem=None):
    return pltpu.CompilerParams(vmem_limit_bytes=vmem) if vmem else None


def _matmul(a, b, *, mode, name, tm, tn, tk, outs, extras=(), epilogue=None, vmem=None):
    if mode == "nn":
        (M, K), (_, N) = a.shape, b.shape
    elif mode == "nt":
        (M, K), (N, _) = a.shape, b.shape
    else:
        (K, M), (_, N) = a.shape, b.shape
    tm, tn, tk = min(tm, M), min(tn, N), min(tk, K)
    assert M % tm == 0 and N % tn == 0 and K % tk == 0, (name, M, N, K, tm, tn, tk)
    nk = K // tk
    dims = {"nn": _NN, "nt": _NT, "tn": _TN}[mode]
    if mode == "tn":
        a_spec = pl.BlockSpec((tk, tm), lambda i, j, k: (k, i))
    else:
        a_spec = pl.BlockSpec((tm, tk), lambda i, j, k: (i, k))
    if mode == "nt":
        b_spec = pl.BlockSpec((tn, tk), lambda i, j, k: (j, k))
    else:
        b_spec = pl.BlockSpec((tk, tn), lambda i, j, k: (k, j))
    ex_specs = [pl.BlockSpec(bs, (lambda i, j, k, f=f: f(i, j))) for (_, bs, f) in extras]
    ne, no = len(extras), len(outs)
    if epilogue is None:
        epilogue = lambda acc: (acc,)

    def body(a_ref, b_ref, *rest):
        ex_refs, out_refs, acc = rest[:ne], rest[ne:ne + no], rest[-1]
        k = pl.program_id(2)

        @pl.when(k == 0)
        def _():
            acc[...] = jnp.zeros_like(acc)

        acc[...] += _dot(a_ref[...].astype(BF16), b_ref[...].astype(BF16), dims)

        @pl.when(k == nk - 1)
        def _():
            res = epilogue(acc[...], *[e[...] for e in ex_refs])
            for r, o in zip(res, out_refs):
                o[...] = r.astype(o.dtype)

    res = pl.pallas_call(
        body, name=name, grid=(M // tm, N // tn, nk),
        in_specs=[a_spec, b_spec] + ex_specs,
        out_specs=[pl.BlockSpec((tm, tn), lambda i, j, k: (i, j)) for _ in outs],
        out_shape=[jax.ShapeDtypeStruct((M, N), d) for d in outs],
        scratch_shapes=[pltpu.VMEM((tm, tn), F32)],
        compiler_params=_params(vmem),
    )(a, b, *[e[0] for e in extras])
    return res[0] if no == 1 else res


def _tile_ij(i, j):
    return (i, j)


def _rowwise(fn, name, rows, tm, ins, outs, vmem=None):
    tm = min(tm, rows)
    assert rows % tm == 0

    def spec(shape, kind):
        if kind == "t":
            return pl.BlockSpec((tm,) + tuple(shape[1:]), lambda i: (i,) + (0,) * (len(shape) - 1))
        return pl.BlockSpec(tuple(shape), lambda i: (0,) * len(shape))

    def body(*refs):
        fn(pl.program_id(0), *refs)

    return pl.pallas_call(
        body, name=name, grid=(rows // tm,),
        in_specs=[spec(a.shape, k) for a, k in ins],
        out_specs=[spec(s, k) for s, _, k in outs],
        out_shape=[jax.ShapeDtypeStruct(s, d) for s, d, _ in outs],
        compiler_params=_params(vmem),
    )(*[a for a, _ in ins])


def _ln_stats(u):
    mu = jnp.mean(u, axis=-1, keepdims=True)
    d = u - mu
    var = jnp.mean(d * d, axis=-1, keepdims=True)
    r = lax.rsqrt(var + LN_EPS)
    return d * r, r


def _ln_bwd(dh, xh, r, g):
    dxh = dh * g
    m1 = jnp.mean(dxh, axis=-1, keepdims=True)
    m2 = jnp.mean(dxh * xh, axis=-1, keepdims=True)
    return r * (dxh - m1 - xh * m2)


def _acc_rows(i, ref, rows):
    @pl.when(i == 0)
    def _():
        ref[...] = jnp.zeros_like(ref)
    for r, v in rows.items():
        ref[pl.ds(r, 1), :] += v


def _head_sums(v, he, het):
    return _split_dot(_split_dot(v, he), het)


def _fgate_fwd(x, wft, bf_col, tm):
    S = x.shape[0]
    tm = min(tm, S)

    def body(wft_ref, bf_ref, x_ref, lf_ref):
        f = _dot(wft_ref[...], x_ref[...].astype(BF16), _NT) + bf_ref[...]
        lf_ref[...] = -_softplus(-f)

    return pl.pallas_call(
        body, name="fgate_fwd", grid=(S // tm,),
        in_specs=[pl.BlockSpec((N_FOX, D_MODEL), lambda i: (0, 0)), pl.BlockSpec((N_FOX, 1), lambda i: (0, 0)),
                  pl.BlockSpec((tm, D_MODEL), lambda i: (i, 0))],
        out_specs=pl.BlockSpec((N_FOX, tm), lambda i: (0, i)),
        out_shape=jax.ShapeDtypeStruct((N_FOX, S), F32),
    )(wft, bf_col, x)


def _chunk_scan(v, reverse):
    lane = lax.broadcasted_iota(jnp.int32, v.shape, 1)
    sh = 1
    while sh < LANES:
        if reverse:
            v = v + jnp.where(lane < LANES - sh, pltpu.roll(v, LANES - sh, 1), 0.0)
        else:
            v = v + jnp.where(lane >= sh, pltpu.roll(v, sh, 1), 0.0)
        sh *= 2
    return v


def _cumsum_fwd(lf):
    n, S = lf.shape
    nc = S // LANES

    grp = min(SCAN_GROUP, nc)

    def body(lf_ref, c_ref):
        def step(gi, carry):
            sls = [pl.ds(pl.multiple_of((gi * grp + g) * LANES, LANES), LANES) for g in range(grp)]
            vs = [_chunk_scan(lf_ref[:, sl], False) for sl in sls]
            tots = [_col(v, LANES - 1) for v in vs]
            for sl, v, t in zip(sls, vs, tots):
                c_ref[:, sl] = v + carry
                carry = carry + t
            return carry
        lax.fori_loop(0, nc // grp, step, jnp.zeros((n, 1), F32))

    return pl.pallas_call(body, name="cumsum_fwd", out_shape=jax.ShapeDtypeStruct((n, S), F32))(lf)


def _fgate_bwd(dc, lf):
    n, S = dc.shape
    nc = S // LANES

    grp = min(SCAN_GROUP, nc)

    def body(dc_ref, lf_ref, dfl_ref, dbf_ref):
        def step(t, carry):
            car, tot = carry
            gi = nc // grp - 1 - t
            sls = [pl.ds(pl.multiple_of((gi * grp + g) * LANES, LANES), LANES) for g in range(grp)]
            vs = [_chunk_scan(dc_ref[:, sl], True) for sl in sls]
            firsts = [_col(v, 0) for v in vs]
            for sl, v, f in reversed(list(zip(sls, vs, firsts))):
                dfl = (v + car) * (1.0 - jnp.exp(lf_ref[:, sl]))
                dfl_ref[:, sl] = dfl
                tot = tot + jnp.sum(dfl, axis=1, keepdims=True)
                car = car + f
            return car, tot
        _, tot = lax.fori_loop(0, nc // grp, step, (jnp.zeros((n, 1), F32), jnp.zeros((n, 1), F32)))
        dbf_ref[...] = tot

    return pl.pallas_call(body, name="fgate_bwd",
                          out_shape=[jax.ShapeDtypeStruct((n, S), F32), jax.ShapeDtypeStruct((n, 1), F32)])(dc, lf)


def _tri_matrices(b):
    r = np.arange(b)
    tfwd = (r[:, None] <= r[None, :]).astype(np.float32)
    return jnp.asarray(tfwd, BF16), jnp.asarray(tfwd.T, BF16)


def _kv_copies(kv_hbm, kbuf, vbuf, sems, pair_col, bq, j, slot):
    rows = pl.ds(pl.multiple_of(j * bq, bq), bq)

    def cols(c):
        return pl.ds(pl.multiple_of((pair_col + c) * LANES, LANES), LANES)

    return (pltpu.make_async_copy(kv_hbm.at[rows, cols(4)], kbuf.at[slot], sems.at[0, slot]),
            pltpu.make_async_copy(kv_hbm.at[rows, cols(8)], vbuf.at[slot], sems.at[1, slot]))


def _first_two_up(first_block, per=1):
    def blocks(pair, blk):
        first = first_block(pair, blk)
        return first, first + 1, first + 1 <= per * blk + per - 1
    return blocks


def _first_two_down(pair, blk):
    return blk, blk - 1, blk > 0


def _start_two(fetch, pair, first, second, has_second):
    for cp in fetch(first, 0, pair):
        cp.start()

    @pl.when(has_second)
    def _():
        for cp in fetch(second, 1, pair):
            cp.start()


def _kv_fetcher(kv_hbm, kbuf, vbuf, sems, col0, bq, p, i, blocks):
    def fetch(j, slot, pair=p):
        return _kv_copies(kv_hbm, kbuf, vbuf, sems, col0 + pair, bq, j, slot)

    pl.when(jnp.logical_and(p == 0, i == 0))(lambda: _start_two(fetch, p, *blocks(p, i)))
    return fetch


def _prefetch_next(fetch, p, i, nq, blocks):
    wrap = i == nq - 1

    @pl.when(jnp.logical_not(jnp.logical_and(wrap, p == N_PAIRS - 1)))
    def _():
        pair, blk = jnp.where(wrap, p + 1, p), jnp.where(wrap, 0, i + 1)
        _start_two(fetch, pair, *blocks(pair, blk))


def _masked_pair(v, lane_is_a, scale=1.0):
    v = v.astype(F32) * scale
    return jnp.where(lane_is_a, v, 0.0).astype(BF16), jnp.where(lane_is_a, 0.0, v).astype(BF16)


def _sb_fwd(proj, col0, bq):
    S = proj.shape[0]
    bq = min(bq, S)
    nq = S // bq
    _, trev = _tri_matrices(bq)

    def body(q_ref, kv_hbm, trev_ref, o_ref, st_ref, jmin_ref, *scratch):
        for sub in range(STEP_BLOCKS):
            rows = pl.ds(sub * bq, bq)
            block(pl.program_id(0), STEP_BLOCKS * pl.program_id(1) + sub, q_ref.at[rows, :], kv_hbm, trev_ref,
                  o_ref.at[rows, :], st_ref.at[:, rows, :], jmin_ref, *scratch)

    def block(p, i, q_ref, kv_hbm, trev_ref, o_ref, st_ref, jmin_ref, acc_a, acc_b, qa, qb, rs, kbuf, vbuf, sems):
        fetch = _kv_fetcher(kv_hbm, kbuf, vbuf, sems, col0, bq, p, i, _first_two_down)
        is_a = lax.broadcasted_iota(jnp.int32, (bq, LANES), 1) < HEAD_DIM
        acc_a[...] = jnp.zeros_like(acc_a)
        acc_b[...] = jnp.zeros_like(acc_b)
        rs[...] = jnp.zeros_like(rs)
        qa[...], qb[...] = _masked_pair(q_ref[...], is_a, SCALE)

        def tile(slot, masked):
            k, v, trev_m = kbuf[slot], vbuf[slot], trev_ref[...]
            if masked:
                tri = lax.broadcasted_iota(jnp.int32, (bq, bq), 0) > lax.broadcasted_iota(jnp.int32, (bq, bq), 1)
            hs, qs, accs = (0, 1), (qa, qb), (acc_a, acc_b)
            z = {h: _dot(qs[h][...], k, _NT) for h in hs}
            lk = {h: -_softplus(z[h]) for h in hs}
            if masked:
                lk = {h: jnp.where(tri, lk[h], 0.0) for h in hs}
            suf = {h: _split_dot(lk[h], trev_m) for h in hs}
            w = {h: jnp.exp(z[h] + suf[h] + (rs[2 * h] + rs[2 * h + 1])) for h in hs}
            if masked:
                w = {h: jnp.where(tri, w[h], 0.0) for h in hs}
            tot = {h: jnp.sum(lk[h], axis=1, keepdims=True) for h in hs}
            pv = {h: _dot(w[h].astype(BF16), v) for h in hs}
            for h in hs:
                accs[h][...] += pv[h]
                rs[2 * h], rs[2 * h + 1] = _two_sum(rs[2 * h], rs[2 * h + 1], tot[h])

        def step(carry):
            j, _ = carry
            slot = lax.rem(i - j, 2)
            for cp in fetch(j, slot):
                cp.wait()

            @pl.when(jnp.logical_and(j > 0, j < i))
            def _():
                for cp in fetch(j - 1, 1 - slot):
                    cp.start()

            pl.when(j == i)(functools.partial(tile, slot, True))
            pl.when(j < i)(functools.partial(tile, slot, False))
            live = jnp.max(jnp.maximum(rs[0], rs[2])) > SB_STOP
            return j - 1, live.astype(jnp.int32)

        j_end, _ = lax.while_loop(lambda c: jnp.logical_and(c[0] >= 0, c[1] > 0), step, (i, jnp.int32(1)))

        @pl.when(j_end >= 0)
        def _():
            for cp in fetch(j_end, lax.rem(i - j_end, 2)):
                cp.wait()

        _prefetch_next(fetch, p, i, nq, _first_two_down)
        jmin_ref[p, i] = j_end + 1
        o_ref[...] = jnp.where(is_a, acc_a[...], acc_b[...])
        lane8 = lax.broadcasted_iota(jnp.int32, (bq, 8), 1)
        st = jnp.zeros((bq, 8), F32)
        for c, src in enumerate((0, 2, 1, 3)):
            st = jnp.where(lane8 == c, rs[src], st)
        st_ref[0] = st

    return pl.pallas_call(
        body, name="sb_fwd", grid=(N_PAIRS, nq // STEP_BLOCKS),
        in_specs=[pl.BlockSpec((STEP_BLOCKS * bq, LANES), lambda p, i: (i, col0 + p)),
                  pl.BlockSpec(memory_space=pl.ANY),
                  pl.BlockSpec((bq, bq), lambda p, i: (0, 0))],
        out_specs=[pl.BlockSpec((STEP_BLOCKS * bq, LANES), lambda p, i: (i, p)),
                   pl.BlockSpec((1, STEP_BLOCKS * bq, 8), lambda p, i: (p, i, 0)),
                   pl.BlockSpec(memory_space=pltpu.SMEM)],
        out_shape=[jax.ShapeDtypeStruct((S, GROUP_W), F32), jax.ShapeDtypeStruct((N_PAIRS, S, 8), F32),
                   jax.ShapeDtypeStruct((N_PAIRS, nq), jnp.int32)],
        scratch_shapes=[pltpu.VMEM((bq, LANES), F32), pltpu.VMEM((bq, LANES), F32),
                        pltpu.VMEM((bq, LANES), BF16), pltpu.VMEM((bq, LANES), BF16),
                        pltpu.VMEM((4, bq, 1), F32),
                        pltpu.VMEM((2, bq, LANES), BF16), pltpu.VMEM((2, bq, LANES), BF16),
                        pltpu.SemaphoreType.DMA((2, 2))],
    )(proj, proj, trev)


def _sb_bwd(proj, col0, do, st, jmin, bq):
    S = proj.shape[0]
    bq = min(bq, S)
    nq = S // bq
    tfwd, trev = _tri_matrices(bq)

    def body(jmin_ref, q_ref, kv_hbm, do_ref, st_ref, tfwd_ref, trev_ref,
             dq_ref, dk_ref, dv_ref, dq_a, dq_b, qa, qb, doa, dob, rs, kbuf, vbuf, sems):
        p, i = pl.program_id(0), pl.program_id(1)
        j0 = jmin_ref[p, i]
        first_two = _first_two_up(lambda pair, blk: jmin_ref[pair, blk])
        fetch = _kv_fetcher(kv_hbm, kbuf, vbuf, sems, col0, bq, p, i, first_two)
        is_a = lax.broadcasted_iota(jnp.int32, (bq, LANES), 1) < HEAD_DIM

        @pl.when(i == 0)
        def _():
            dk_ref[...] = jnp.zeros_like(dk_ref)
            dv_ref[...] = jnp.zeros_like(dv_ref)

        dq_a[...] = jnp.zeros_like(dq_a)
        dq_b[...] = jnp.zeros_like(dq_b)
        rs[...] = jnp.zeros_like(rs)
        st_v = st_ref[0]
        for h in range(2):
            rs[6 + 2 * h], rs[7 + 2 * h] = _col(st_v, h), _col(st_v, 2 + h)
        qa[...], qb[...] = _masked_pair(q_ref[...], is_a, SCALE)
        doa[...], dob[...] = _masked_pair(do_ref[...], is_a)

        def tile(j, slot, masked):
            k, v = kbuf[slot], vbuf[slot]
            tfwd_m, trev_m = tfwd_ref[...], trev_ref[...]
            if masked:
                tri = lax.broadcasted_iota(jnp.int32, (bq, bq), 0) > lax.broadcasted_iota(jnp.int32, (bq, bq), 1)
            hs, qs, dos, dqs = (0, 1), (qa, qb), (doa, dob), (dq_a, dq_b)
            z = {h: _dot(qs[h][...], k, _NT) for h in hs}
            lk = {h: -_softplus(z[h]) for h in hs}
            if masked:
                lk = {h: jnp.where(tri, lk[h], 0.0) for h in hs}
            suf = {h: _split_dot(lk[h], trev_m) for h in hs}
            dw = {h: _dot(dos[h][...], v, _NT) for h in hs}
            pre = {h: _two_sum(rs[3 * h], rs[3 * h + 1], jnp.sum(lk[h], axis=1, keepdims=True)) for h in hs}
            right = {h: (rs[6 + 2 * h] - pre[h][0]) + (rs[7 + 2 * h] - pre[h][1]) for h in hs}
            w = {h: jnp.exp(z[h] + suf[h] + right[h]) for h in hs}
            if masked:
                w = {h: jnp.where(tri, w[h], 0.0) for h in hs}
            g = {h: dw[h] * w[h] for h in hs}
            gpre = {h: _split_dot(g[h], tfwd_m) for h in hs}
            dz = {h: g[h] - jnp.exp(z[h] + lk[h]) * (gpre[h] + rs[3 * h + 2]) for h in hs}
            if masked:
                dz = {h: jnp.where(tri, dz[h], 0.0) for h in hs}
            gtot = {h: jnp.sum(g[h], axis=1, keepdims=True) for h in hs}
            dzb = {h: dz[h].astype(BF16) for h in hs}
            wb = {h: w[h].astype(BF16) for h in hs}
            dqc = {h: _dot(dzb[h], k) for h in hs}
            dkc = {h: _dot(dzb[h], qs[h][...], _TN) for h in hs}
            dvc = {h: _dot(wb[h], dos[h][...], _TN) for h in hs}
            for h in hs:
                rs[3 * h], rs[3 * h + 1] = pre[h]
                rs[3 * h + 2] += gtot[h]
                dqs[h][...] += dqc[h]
            rows = pl.ds(pl.multiple_of(j * bq, bq), bq)
            dk_ref[rows, :] += dkc[0] + dkc[1]
            dv_ref[rows, :] += dvc[0] + dvc[1]

        _walk_up(fetch, j0, i, i, tile)
        _prefetch_next(fetch, p, i, nq, first_two)
        dq_ref[...] = jnp.where(is_a, dq_a[...], dq_b[...]) * SCALE

    grid_spec = pltpu.PrefetchScalarGridSpec(
        num_scalar_prefetch=1, grid=(N_PAIRS, nq),
        in_specs=[pl.BlockSpec((bq, LANES), lambda p, i, jm: (i, col0 + p)),
                  pl.BlockSpec(memory_space=pl.ANY),
                  pl.BlockSpec((bq, LANES), lambda p, i, jm: (i, p)),
                  pl.BlockSpec((1, bq, 8), lambda p, i, jm: (p, i, 0)),
                  pl.BlockSpec((bq, bq), lambda p, i, jm: (0, 0)),
                  pl.BlockSpec((bq, bq), lambda p, i, jm: (0, 0))],
        out_specs=[pl.BlockSpec((bq, LANES), lambda p, i, jm: (i, p)),
                   pl.BlockSpec((S, LANES), lambda p, i, jm: (0, p)),
                   pl.BlockSpec((S, LANES), lambda p, i, jm: (0, p))],
        scratch_shapes=[pltpu.VMEM((bq, LANES), F32), pltpu.VMEM((bq, LANES), F32)]
        + [pltpu.VMEM((bq, LANES), BF16)] * 4 + [pltpu.VMEM((10, bq, 1), F32)]
        + [pltpu.VMEM((KV_SLOTS, bq, LANES), BF16)] * 2 + [pltpu.SemaphoreType.DMA((2, KV_SLOTS))])
    return pl.pallas_call(
        body, name="sb_bwd", grid_spec=grid_spec,
        out_shape=[jax.ShapeDtypeStruct((S, GROUP_W), F32)] * 3,
        compiler_params=_params(VMEM_BIG),
    )(jmin, proj, proj, do, st, tfwd, trev)


def _walk_up(fetch, j0, diag, last, tile):
    ahead = KV_SLOTS - 1

    def start(j):
        @pl.when(j <= last)
        def _():
            for cp in fetch(j, lax.rem(j - j0, KV_SLOTS)):
                cp.start()

    for d in range(2, ahead):
        start(j0 + d)

    def step(j, carry):
        slot = lax.rem(j - j0, KV_SLOTS)
        for cp in fetch(j, slot):
            cp.wait()
        start(j + ahead)
        pl.when(j >= diag)(functools.partial(tile, j, slot, True))
        pl.when(j < diag)(functools.partial(tile, j, slot, False))
        return carry

    lax.fori_loop(j0, last + 1, step, 0)


def _causal(bq, bk, i, j):
    row = lax.broadcasted_iota(jnp.int32, (bq, bk), 0)
    col = lax.broadcasted_iota(jnp.int32, (bq, bk), 1)
    return col - row <= i * bq - j * bk


def _by_heads(j, first_a, first_b, heads):
    on_a, on_b = j >= first_a, j >= first_b
    pl.when(jnp.logical_and(on_a, on_b))(functools.partial(heads, (0, 1)))
    pl.when(jnp.logical_and(on_a, jnp.logical_not(on_b)))(functools.partial(heads, (0,)))
    pl.when(jnp.logical_and(on_b, jnp.logical_not(on_a)))(functools.partial(heads, (1,)))


def _fox_start_blocks(proj, col0, c, bq, bk):
    S = proj.shape[0]
    nq, nk, nh = S // bq, S // bk, 2 * N_PAIRS

    def heads(first):
        return proj[:, first * LANES:(first + N_PAIRS) * LANES].astype(F32).reshape(S, nh, HEAD_DIM)

    q, k = heads(col0), heads(col0 + 4)
    qn = jnp.sqrt(jnp.sum(q * q, axis=-1))
    kmax = jnp.sqrt(jnp.sum(k * k, axis=-1)).max(axis=0)
    top = SCALE * (qn * kmax[None, :] - jnp.sum(q * k, axis=-1)) + c.T
    top = top.reshape(nq, bq, nh).max(axis=1)
    c_last = c[:, bk - 1::bk].T
    live = top[:, None, :] - c_last[None, :, :] >= -FOX_SKIP

    def first_block(lv):
        first = jnp.where(lv.any(axis=1), jnp.argmax(lv, axis=1), nk)
        return jnp.minimum(first, (bq // bk) * jnp.arange(nq)[:, None]).T.astype(jnp.int32)

    return jnp.concatenate([first_block(live.reshape(nq, nk, N_PAIRS, 2).any(axis=-1)), first_block(live)], axis=0)


def _fox_fwd(proj, col0, c_col, c_row, jstart, bq, bk):
    S = proj.shape[0]
    nq, per = S // bq, bq // bk

    def body(js_ref, q_ref, kv_hbm, cc_ref, cr_ref, o_ref, st_ref, *scratch):
        for sub in range(STEP_BLOCKS):
            rows = pl.ds(sub * bq, bq)
            block(pl.program_id(0), STEP_BLOCKS * pl.program_id(1) + sub, js_ref, q_ref.at[rows, :], kv_hbm,
                  cc_ref.at[:, rows, :], cr_ref, o_ref.at[rows, :], st_ref.at[:, rows, :], *scratch)

    def block(p, i, js_ref, q_ref, kv_hbm, cc_ref, cr_ref, o_ref, st_ref, acc_a, acc_b, qa, qb, ml, kbuf, vbuf, sems):
        j0 = js_ref[p, i]
        first_two = _first_two_up(lambda pair, blk: js_ref[pair, blk], per)
        fetch = _kv_fetcher(kv_hbm, kbuf, vbuf, sems, col0, bk, p, i, first_two)
        is_a = lax.broadcasted_iota(jnp.int32, (bq, LANES), 1) < HEAD_DIM
        acc_a[...] = jnp.zeros_like(acc_a)
        acc_b[...] = jnp.zeros_like(acc_b)
        ml[0] = jnp.full((bq, 1), NEG_BIG, F32)
        ml[2] = jnp.full((bq, 1), NEG_BIG, F32)
        ml[1] = jnp.zeros((bq, 1), F32)
        ml[3] = jnp.zeros((bq, 1), F32)
        cc = cc_ref[0]
        ml[4], ml[5] = _col(cc, 0), _col(cc, 1)
        qa[...], qb[...] = _masked_pair(q_ref[...], is_a, SCALE)

        def tile(j, slot, masked):
            k, v = kbuf[slot], vbuf[slot]
            cols = pl.ds(pl.multiple_of(j * bk, bk), bk)
            if masked:
                tri = _causal(bq, bk, i, j)

            def heads(hs):
                qs, accs = (qa, qb), (acc_a, acc_b)
                s = {h: _dot(qs[h][...], k, _NT) - cr_ref[0, pl.ds(h, 1), cols] for h in hs}
                if masked:
                    s = {h: jnp.where(tri, s[h], NEG_BIG) for h in hs}
                top = {h: jnp.max(s[h], axis=1, keepdims=True) for h in hs}
                m_new = {h: jnp.maximum(ml[2 * h], top[h] + ml[4 + h]) for h in hs}
                a = {h: jnp.exp(ml[2 * h] - m_new[h]) for h in hs}
                pr = {h: jnp.exp(s[h] - (m_new[h] - ml[4 + h])) for h in hs}
                tot = {h: jnp.sum(pr[h], axis=1, keepdims=True) for h in hs}
                pv = {h: _dot(pr[h].astype(BF16), v) for h in hs}
                for h in hs:
                    ml[2 * h] = m_new[h]
                    ml[2 * h + 1] = a[h] * ml[2 * h + 1] + tot[h]
                    accs[h][...] = a[h] * accs[h][...] + pv[h]

            _by_heads(j, js_ref[N_PAIRS + 2 * p, i], js_ref[N_PAIRS + 2 * p + 1, i], heads)

        _walk_up(fetch, j0, per * i, per * i + per - 1, tile)
        _prefetch_next(fetch, p, i, nq, first_two)
        o_ref[...] = jnp.where(is_a, acc_a[...] / ml[1], acc_b[...] / ml[3])
        lane8 = lax.broadcasted_iota(jnp.int32, (bq, 8), 1)
        st = jnp.where(lane8 == 0, ml[0] + jnp.log(ml[1]), 0.0)
        st_ref[0] = jnp.where(lane8 == 1, ml[2] + jnp.log(ml[3]), st)

    grid_spec = pltpu.PrefetchScalarGridSpec(
        num_scalar_prefetch=1, grid=(N_PAIRS, nq // STEP_BLOCKS),
        in_specs=[pl.BlockSpec((STEP_BLOCKS * bq, LANES), lambda p, i, js: (i, col0 + p)),
                  pl.BlockSpec(memory_space=pl.ANY),
                  pl.BlockSpec((1, STEP_BLOCKS * bq, 8), lambda p, i, js: (p, i, 0)),
                  pl.BlockSpec((1, 8, S), lambda p, i, js: (p, 0, 0))],
        out_specs=[pl.BlockSpec((STEP_BLOCKS * bq, LANES), lambda p, i, js: (i, p)),
                   pl.BlockSpec((1, STEP_BLOCKS * bq, 8), lambda p, i, js: (p, i, 0))],
        scratch_shapes=[pltpu.VMEM((bq, LANES), F32), pltpu.VMEM((bq, LANES), F32),
                        pltpu.VMEM((bq, LANES), BF16), pltpu.VMEM((bq, LANES), BF16),
                        pltpu.VMEM((6, bq, 1), F32),
                        pltpu.VMEM((KV_SLOTS, bk, LANES), BF16), pltpu.VMEM((KV_SLOTS, bk, LANES), BF16),
                        pltpu.SemaphoreType.DMA((2, KV_SLOTS))])
    return pl.pallas_call(
        body, name="fox_fwd", grid_spec=grid_spec,
        out_shape=[jax.ShapeDtypeStruct((S, GROUP_W), F32), jax.ShapeDtypeStruct((N_PAIRS, S, 8), F32)],
    )(jstart, proj, proj, c_col, c_row)


def _fox_bwd(proj, col0, do, o, st, c_col, c_row, jstart, bq, bk):
    S = proj.shape[0]
    nq, per = S // bq, bq // bk

    def body(js_ref, q_ref, kv_hbm, do_ref, o_ref, st_ref, cc_ref, cr_ref,
             dq_ref, dk_ref, dv_ref, dc_ref, dcq_ref, dq_a, dq_b, qa, qb, doa, dob, dd, kbuf, vbuf, sems):
        p, i = pl.program_id(0), pl.program_id(1)
        j0 = js_ref[p, i]
        first_two = _first_two_up(lambda pair, blk: js_ref[pair, blk], per)
        fetch = _kv_fetcher(kv_hbm, kbuf, vbuf, sems, col0, bk, p, i, first_two)
        is_a = lax.broadcasted_iota(jnp.int32, (bq, LANES), 1) < HEAD_DIM

        @pl.when(i == 0)
        def _():
            dk_ref[...] = jnp.zeros_like(dk_ref)
            dv_ref[...] = jnp.zeros_like(dv_ref)
            dc_ref[...] = jnp.zeros_like(dc_ref)

        dq_a[...] = jnp.zeros_like(dq_a)
        dq_b[...] = jnp.zeros_like(dq_b)
        qa[...], qb[...] = _masked_pair(q_ref[...], is_a, SCALE)
        dov = do_ref[...]
        doa[...], dob[...] = _masked_pair(dov, is_a)
        prod = dov * o_ref[...]
        dd[0] = jnp.sum(jnp.where(is_a, prod, 0.0), axis=1, keepdims=True)
        dd[1] = jnp.sum(jnp.where(is_a, 0.0, prod), axis=1, keepdims=True)
        dd[2] = jnp.zeros((bq, 1), F32)
        dd[3] = jnp.zeros((bq, 1), F32)
        cc, st_v = cc_ref[0], st_ref[0]
        dd[4], dd[5] = _col(cc, 0) - _col(st_v, 0), _col(cc, 1) - _col(st_v, 1)

        def tile(j, slot, masked):
            k, v = kbuf[slot], vbuf[slot]
            if masked:
                tri = _causal(bq, bk, i, j)
            cols = pl.ds(pl.multiple_of(j * bk, bk), bk)

            def heads(hs):
                qs, dos, dqs = (qa, qb), (doa, dob), (dq_a, dq_b)
                z = {h: _dot(qs[h][...], k, _NT) for h in hs}
                dp = {h: _dot(dos[h][...], v, _NT) for h in hs}
                pr = {h: jnp.exp(z[h] - cr_ref[0, pl.ds(h, 1), cols] + dd[4 + h]) for h in hs}
                if masked:
                    pr = {h: jnp.where(tri, pr[h], 0.0) for h in hs}
                ds = {h: pr[h] * (dp[h] - dd[h]) for h in hs}
                csum = {h: jnp.sum(ds[h], axis=0, keepdims=True) for h in hs}
                rsum = {h: jnp.sum(ds[h], axis=1, keepdims=True) for h in hs}
                dsb = {h: ds[h].astype(BF16) for h in hs}
                prb = {h: pr[h].astype(BF16) for h in hs}
                dqc = {h: _dot(dsb[h], k) for h in hs}
                dkc = [_dot(dsb[h], qs[h][...], _TN) for h in hs]
                dvc = [_dot(prb[h], dos[h][...], _TN) for h in hs]
                for h in hs:
                    dc_ref[0, pl.ds(h, 1), cols] -= csum[h]
                    dd[2 + h] += rsum[h]
                    dqs[h][...] += dqc[h]
                dk_ref[cols, :] += sum(dkc[1:], dkc[0])
                dv_ref[cols, :] += sum(dvc[1:], dvc[0])

            _by_heads(j, js_ref[N_PAIRS + 2 * p, i], js_ref[N_PAIRS + 2 * p + 1, i], heads)

        _walk_up(fetch, j0, per * i, per * i + per - 1, tile)
        _prefetch_next(fetch, p, i, nq, first_two)
        dq_ref[...] = jnp.where(is_a, dq_a[...], dq_b[...]) * SCALE
        lane8 = lax.broadcasted_iota(jnp.int32, (bq, 8), 1)
        dcq_ref[0] = jnp.where(lane8 == 0, dd[2], jnp.where(lane8 == 1, dd[3], 0.0))

    grid_spec = pltpu.PrefetchScalarGridSpec(
        num_scalar_prefetch=1, grid=(N_PAIRS, nq),
        in_specs=[pl.BlockSpec((bq, LANES), lambda p, i, js: (i, col0 + p)),
                  pl.BlockSpec(memory_space=pl.ANY),
                  pl.BlockSpec((bq, LANES), lambda p, i, js: (i, p)),
                  pl.BlockSpec((bq, LANES), lambda p, i, js: (i, p)),
                  pl.BlockSpec((1, bq, 8), lambda p, i, js: (p, i, 0)),
                  pl.BlockSpec((1, bq, 8), lambda p, i, js: (p, i, 0)),
                  pl.BlockSpec((1, 8, S), lambda p, i, js: (p, 0, 0))],
        out_specs=[pl.BlockSpec((bq, LANES), lambda p, i, js: (i, p)),
                   pl.BlockSpec((S, LANES), lambda p, i, js: (0, p)),
                   pl.BlockSpec((S, LANES), lambda p, i, js: (0, p)),
                   pl.BlockSpec((1, 8, S), lambda p, i, js: (p, 0, 0)),
                   pl.BlockSpec((1, bq, 8), lambda p, i, js: (p, i, 0))],
        scratch_shapes=[pltpu.VMEM((bq, LANES), F32), pltpu.VMEM((bq, LANES), F32)]
        + [pltpu.VMEM((bq, LANES), BF16)] * 4 + [pltpu.VMEM((6, bq, 1), F32)]
        + [pltpu.VMEM((KV_SLOTS, bk, LANES), BF16)] * 2 + [pltpu.SemaphoreType.DMA((2, KV_SLOTS))])
    return pl.pallas_call(
        body, name="fox_bwd", grid_spec=grid_spec,
        out_shape=[jax.ShapeDtypeStruct((S, GROUP_W), F32)] * 3
        + [jax.ShapeDtypeStruct((N_PAIRS, 8, S), F32), jax.ShapeDtypeStruct((N_PAIRS, S, 8), F32)],
        compiler_params=_params(VMEM_BIG),
    )(jstart, proj, proj, do, o, st, c_col, c_row)


_HBM = pl.BlockSpec(memory_space=pltpu.HBM)


def _coords():
    return lax.axis_index("x"), lax.axis_index("y"), lax.axis_index("c")


def _gather_copies(ins, outs, send_sems, recv_sems, loc_sems):
    n = len(ins)
    x, y, c = _coords()
    mine = 2 * x + y
    chips = [(1 - x, y), (x, 1 - y), (1 - x, 1 - y)]

    def copy(w, r, slab, to):
        return pltpu.make_async_remote_copy(
            src_ref=ins[w], dst_ref=outs[w].at[slab], send_sem=send_sems.at[3 * w + r],
            recv_sem=recv_sems.at[3 * w + r], device_id=to, device_id_type=MESH)

    def own():
        local = [pltpu.make_async_copy(ins[w], outs[w].at[mine], loc_sems.at[w]) for w in range(n)]
        return local, [copy(w, r, mine, (cx, cy, c)) for w in range(n) for r, (cx, cy) in enumerate(chips)]

    def start():
        local, sends = own()
        for cp in local + sends:
            cp.start()

    def wait():
        local, sends = own()
        for w in range(n):
            for r, (cx, cy) in enumerate(chips):
                copy(w, r, 2 * cx + cy, (cx, cy, c)).wait_recv()
        for cp in sends:
            cp.wait_send()
        for cp in local:
            cp.wait()

    return start, wait


def _gather_shapes(shards):
    n = len(shards)
    return ([jax.ShapeDtypeStruct((4,) + s.shape, s.dtype) for s in shards],
            [pltpu.SemaphoreType.DMA((3 * n,)), pltpu.SemaphoreType.DMA((3 * n,)), pltpu.SemaphoreType.DMA((n,))])


def _allgather_chips(shards):
    n = len(shards)

    def body(*refs):
        start, wait = _gather_copies(refs[:n], refs[n:2 * n], *refs[2 * n:])
        start()
        wait()

    out_shape, sems = _gather_shapes(shards)
    return pl.pallas_call(body, name="allgather_weights", in_specs=[_HBM] * n, out_specs=[_HBM] * n,
                          out_shape=out_shape, scratch_shapes=sems)(*shards)


def _proj_gather(x, w, shards, tm, tn):
    (M, K), N, n = x.shape, w.shape[1], len(shards)
    tm = min(tm, M)
    gi, gj = M // tm, N // tn

    def body(a_ref, b_ref, *rest):
        o_ref = rest[n]
        start, wait = _gather_copies(rest[:n], rest[n + 1:2 * n + 1], *rest[2 * n + 1:])
        i, j = pl.program_id(0), pl.program_id(1)
        pl.when(jnp.logical_and(i == 0, j == 0))(start)
        o_ref[...] = _dot(a_ref[...].astype(BF16), b_ref[...]).astype(o_ref.dtype)
        pl.when(jnp.logical_and(i == gi - 1, j == gj - 1))(wait)

    out_shape, sems = _gather_shapes(shards)
    return pl.pallas_call(
        body, name="proj_gather", grid=(gi, gj),
        in_specs=[pl.BlockSpec((tm, K), lambda i, j: (i, 0)), pl.BlockSpec((K, tn), lambda i, j: (0, j))] + [_HBM] * n,
        out_specs=[pl.BlockSpec((tm, tn), lambda i, j: (i, j))] + [_HBM] * n,
        out_shape=[jax.ShapeDtypeStruct((M, N), BF16)] + out_shape, scratch_shapes=sems,
    )(x, w, *shards)


def _exchange(parts, per_chip):
    n = len(parts)
    half = [p.shape[1] // 2 for p in parts] if per_chip else None

    def body(*refs):
        ins, outs = refs[:n], refs[n:2 * n]
        send_sems, recv_sems, loc_sems = refs[2 * n:]
        x, y, c = _coords()
        me = 4 * x + 2 * y + c
        peers = [(x ^ fx, y ^ fy, c ^ fc) for fx in (0, 1) for fy in (0, 1) for fc in (0, 1)][1:]

        def src(w, dev):
            if not per_chip:
                return ins[w]
            return ins[w].at[2 * dev[0] + dev[1], pl.ds(pl.multiple_of(dev[2] * half[w], 16), half[w]), :]

        local = [pltpu.make_async_copy(src(w, (x, y, c)), outs[w].at[me], loc_sems.at[w]) for w in range(n)]
        for cp in local:
            cp.start()

        def copy(w, r, source, slab, to):
            return pltpu.make_async_remote_copy(
                src_ref=source, dst_ref=outs[w].at[slab], send_sem=send_sems.at[7 * w + r],
                recv_sem=recv_sems.at[7 * w + r], device_id=to, device_id_type=MESH)

        sends = [copy(w, r, src(w, dev), me, dev) for w in range(n) for r, dev in enumerate(peers)]
        for cp in sends:
            cp.start()
        for w in range(n):
            for r, dev in enumerate(peers):
                copy(w, r, src(w, dev), 4 * dev[0] + 2 * dev[1] + dev[2], dev).wait_recv()
        for cp in sends:
            cp.wait_send()
        for cp in local:
            cp.wait()

    return pl.pallas_call(
        body, name="exchange_per_chip" if per_chip else "exchange_all",
        in_specs=[_HBM] * n, out_specs=[_HBM] * n,
        out_shape=[jax.ShapeDtypeStruct((8, half[w], p.shape[2]) if per_chip else (8,) + p.shape, p.dtype)
                   for w, p in enumerate(parts)],
        scratch_shapes=[pltpu.SemaphoreType.DMA((7 * n,)), pltpu.SemaphoreType.DMA((7 * n,)),
                        pltpu.SemaphoreType.DMA((n,))],
    )(*parts)


def _sibling_swap(halves):
    n = len(halves)

    def body(*refs):
        ins, outs = refs[:n], refs[n:2 * n]
        send_sems, recv_sems, loc_sems = refs[2 * n:]
        x, y, c = _coords()

        def rows(w, core):
            rh = halves[w].shape[0]
            return outs[w].at[pl.ds(pl.multiple_of(core * rh, 8), rh), :]

        def copy(w, core):
            return pltpu.make_async_remote_copy(
                src_ref=ins[w], dst_ref=rows(w, core), send_sem=send_sems.at[w], recv_sem=recv_sems.at[w],
                device_id=(x, y, 1 - c), device_id_type=MESH)

        local = [pltpu.make_async_copy(ins[w], rows(w, c), loc_sems.at[w]) for w in range(n)]
        sends = [copy(w, c) for w in range(n)]
        for cp in local + sends:
            cp.start()
        for w in range(n):
            copy(w, 1 - c).wait_recv()
        for cp in sends:
            cp.wait_send()
        for cp in local:
            cp.wait()

    vmem = pl.BlockSpec(memory_space=pltpu.VMEM)
    return pl.pallas_call(
        body, name="sibling_swap", in_specs=[vmem] * n, out_specs=[vmem] * n,
        out_shape=[jax.ShapeDtypeStruct((2 * h.shape[0], h.shape[1]), h.dtype) for h in halves],
        scratch_shapes=[pltpu.SemaphoreType.DMA((n,)), pltpu.SemaphoreType.DMA((n,)), pltpu.SemaphoreType.DMA((n,))],
    )(*halves)


def _adamw(w, g, m, v):
    m = ADAM_B1 * m + (1.0 - ADAM_B1) * g
    v = ADAM_B2 * v + (1.0 - ADAM_B2) * (g * g)
    m_hat = m / (1.0 - ADAM_B1 ** ADAM_STEP)
    v_hat = v / (1.0 - ADAM_B2 ** ADAM_STEP)
    delta = -ADAM_LR * (m_hat / (jnp.sqrt(v_hat) + ADAM_EPS) + ADAM_WD * w)
    return delta, m, v


def _sum_parts(parts, name, tr):
    _, R, C = parts.shape
    assert R % tr == 0

    def body(p_ref, g_ref):
        g = p_ref[0].astype(F32)
        for d in range(1, 8):
            g = g + p_ref[d].astype(F32)
        g_ref[...] = g

    return pl.pallas_call(
        body, name=name, grid=(R // tr,),
        in_specs=[pl.BlockSpec((8, tr, C), lambda i: (0, i, 0))],
        out_specs=pl.BlockSpec((tr, C), lambda i: (i, 0)), out_shape=jax.ShapeDtypeStruct((R, C), F32),
    )(parts)


def _adamw_call(g, w, m, v, name, tr):
    R, C = w.shape
    assert R % tr == 0

    def body(g_ref, w_ref, m_ref, v_ref, d_ref, nm_ref, nv_ref):
        d_ref[...], nm_ref[...], nv_ref[...] = _adamw(w_ref[...], g_ref[...], m_ref[...], v_ref[...])

    tile = pl.BlockSpec((tr, C), lambda i: (i, 0))
    return pl.pallas_call(
        body, name=name, grid=(R // tr,), in_specs=[tile] * 4,
        out_specs=[tile] * 3, out_shape=[jax.ShapeDtypeStruct((R, C), F32)] * 3,
    )(g, w, m, v)


def _sum_adamw_small(parts, w, m, v):
    def body(p_ref, w_ref, m_ref, v_ref, g_ref, d_ref, nm_ref, nv_ref, loss_ref):
        g = p_ref[0]
        for d in range(1, 8):
            g = g + p_ref[d]
        g_ref[...] = g
        d_ref[...], nm_ref[...], nv_ref[...] = _adamw(w_ref[...], g, m_ref[...], v_ref[...])
        row = lax.broadcasted_iota(jnp.int32, g.shape, 0)
        per_row = jnp.sum(jnp.where(row == 6, g, 0.0), axis=1, keepdims=True)
        loss_ref[...] = jnp.zeros((8, LANES), F32) + jnp.sum(per_row, axis=0, keepdims=True)

    return pl.pallas_call(
        body, name="sum_adamw_small",
        out_shape=[jax.ShapeDtypeStruct((8, D_MODEL), F32)] * 4 + [jax.ShapeDtypeStruct((8, LANES), F32)],
    )(parts, w, m, v)


def _pack_small(ln1_g, ln1_b, ln2_g, ln2_b, g_sb, g_fox, b_f):
    row5 = jnp.pad(b_f.reshape(1, N_FOX), ((0, 0), (0, D_MODEL - N_FOX)))
    rows = [ln1_g.reshape(1, -1), ln1_b.reshape(1, -1), ln2_g.reshape(1, -1), ln2_b.reshape(1, -1),
            jnp.concatenate([g_sb.reshape(1, -1), g_fox.reshape(1, -1)], axis=1), row5,
            jnp.zeros((2, D_MODEL), F32)]
    return jnp.concatenate(rows, axis=0)


def _unpack_small(p):
    return {"ln1_g": p[0:1], "ln1_b": p[1:2], "ln2_g": p[2:3], "ln2_b": p[3:4], "g_sb": p[4:5, :GROUP_W],
            "g_fox": p[4:5, GROUP_W:], "b_f": p[5:6, :N_FOX]}


def kernel(x, w_in, b_f, g_sb, g_fox, w_out, ln1_g, ln1_b, ln2_g, ln2_b, w_gate_up, w_down, loss_target, m_w_in, m_b_f, m_g_sb, m_g_fox, m_w_out, m_ln1_g, m_ln1_b, m_ln2_g, m_ln2_b, m_w_gate_up, m_w_down, v_w_in, v_b_f, v_g_sb, v_g_fox, v_w_out, v_ln1_g, v_ln1_b, v_ln2_g, v_ln2_b, v_w_gate_up, v_w_down):
    S = x.shape[1]
    x2 = x.reshape(S, D_MODEL)
    tgt = loss_target.reshape(S, D_MODEL)
    TM = 1024
    TR = 512
    BQ = ATTN_BLOCK
    in_w = w_in.shape[2]
    gu_w = w_gate_up.shape[2]

    shards = [w_in[0].astype(BF16), w_out[0].astype(BF16), w_gate_up[0].astype(BF16), w_down[0].astype(BF16)]
    (wi_s,) = _allgather_chips(shards[:1])
    wi = wi_s.transpose(1, 0, 2).reshape(D_MODEL, 4 * in_w)
    w_sb, w_fx = wi[:, :QKV_W // 2], wi[:, QKV_W // 2:QKV_W]
    wqkv = wi[:, :QKV_W]
    wft = wi[:, QKV_W:].T
    proj, wo_s, wgu_s, wd_s = _proj_gather(x2, wqkv, shards[1:], TM, 512)
    wo = wo_s.reshape(D_MODEL, D_MODEL)
    wgu = wgu_s.transpose(1, 0, 2).reshape(D_MODEL, 2 * D_FF)
    wg, wu = wgu[:, :D_FF], wgu[:, D_FF:]
    wd = wd_s.reshape(D_FF, D_MODEL)
    g_row = jnp.concatenate([g_sb, g_fox], axis=1)
    hid = np.arange(D_MODEL) // HEAD_DIM
    he_np = (hid[:, None] == np.arange(LANES)[None, :]).astype(np.float32)
    he, het = jnp.asarray(he_np, BF16), jnp.asarray(he_np.T, BF16)

    lf = _fgate_fwd(x2, wft, b_f.reshape(N_FOX, 1), TM)
    c = _cumsum_fwd(lf)
    c_pair = c.reshape(N_PAIRS, 2, S)
    c_row = jnp.pad(c_pair, ((0, 0), (0, 6), (0, 0)))
    c_col = jnp.pad(c_pair.transpose(0, 2, 1), ((0, 0), (0, 0), (0, 6)))

    o_sb, st_sb, jmin_sb = _sb_fwd(proj, 0, BQ)
    jstart_fx = _fox_start_blocks(proj, 12, c, BQ, BQ)
    o_fx, st_fx = _fox_fwd(proj, 12, c_col, c_row, jstart_fx, BQ, BQ)

    def attn_post(i, osb_ref, ofx_ref, g_ref, he_ref, het_ref, on_ref):
        o = jnp.concatenate([osb_ref[...], ofx_ref[...]], axis=1)
        ms = _head_sums(o * o, he_ref[...], het_ref[...]) * (1.0 / HEAD_DIM)
        on_ref[...] = (o * lax.rsqrt(ms + RMS_EPS) * g_ref[...]).astype(BF16)

    (on,) = _rowwise(attn_post, "attn_post", S, TR,
                     [(o_sb, "t"), (o_fx, "t"), (g_row, "f"), (he, "f"), (het, "f")],
                     [((S, D_MODEL), BF16, "t")])

    u1 = _matmul(on, wo, mode="nn", name="mix", tm=TM, tn=D_MODEL, tk=D_MODEL, outs=[F32],
                 extras=[(x2, (TM if S >= TM else S, D_MODEL), _tile_ij)],
                 epilogue=lambda acc, xv: (ALPHA * xv + acc,))

    def ln1_fwd(i, u_ref, g_ref, b_ref, h_ref):
        xh, _ = _ln_stats(u_ref[...])
        h_ref[...] = xh * g_ref[...] + b_ref[...]

    (h1,) = _rowwise(ln1_fwd, "ln1_fwd", S, TR, [(u1, "t"), (ln1_g, "f"), (ln1_b, "f")], [((S, D_MODEL), F32, "t")])

    tm_e = TM if S >= TM else S
    n_ff = D_FF // 256

    def gate_up_body(h_ref, wg_ref, wu_ref, g_ref, u_ref, a_ref):
        h = h_ref[...].astype(BF16)
        g, u = _dot(h, wg_ref[...]), _dot(h, wu_ref[...])
        g_ref[...] = g.astype(BF16)
        u_ref[...] = u.astype(BF16)
        a_ref[...] = (g / (1.0 + jnp.exp(-g)) * u).astype(BF16)

    ff_tile = pl.BlockSpec((tm_e, 256), lambda i, j: (i, j))
    gate, up, act = pl.pallas_call(
        gate_up_body, name="gate_up_act", grid=(S // tm_e, n_ff),
        in_specs=[pl.BlockSpec((tm_e, D_MODEL), lambda i, j: (i, 0)),
                  pl.BlockSpec((D_MODEL, 256), lambda i, j: (0, j)),
                  pl.BlockSpec((D_MODEL, 256), lambda i, j: (0, j + n_ff))],
        out_specs=[ff_tile] * 3, out_shape=[jax.ShapeDtypeStruct((S, D_FF), BF16)] * 3)(h1, wgu, wgu)

    u2 = _matmul(act, wd, mode="nn", name="ffn_down", tm=TM, tn=D_MODEL, tk=D_FF, outs=[F32],
                 extras=[(h1, (TM if S >= TM else S, D_MODEL), _tile_ij)],
                 epilogue=lambda acc, hv: (ALPHA * hv + acc,))

    def ln2_loss(i, u_ref, t_ref, g_ref, b_ref, du_ref, acc_ref):
        xh, r = _ln_stats(u_ref[...])
        g = g_ref[...]
        err = xh * g + b_ref[...] - t_ref[...]
        dy = err * (1.0 / D_MODEL)
        du_ref[...] = _ln_bwd(dy, xh, r, g)
        _acc_rows(i, acc_ref, {2: jnp.sum(dy * xh, axis=0, keepdims=True), 3: jnp.sum(dy, axis=0, keepdims=True),
                               6: jnp.sum(err * err, axis=0, keepdims=True) * (0.5 / D_MODEL)})

    du2, acc_ln2 = _rowwise(ln2_loss, "ln2_loss", S, TR, [(u2, "t"), (tgt, "t"), (ln2_g, "f"), (ln2_b, "f")],
                            [((S, D_MODEL), F32, "t"), ((8, D_MODEL), F32, "f")])

    d_wd = _matmul(act, du2, mode="tn", name="dw_down", tm=1408, tn=D_MODEL, tk=TM, outs=[BF16])

    def dgu_epilogue(da, g, u):
        g, u = g.astype(F32), u.astype(F32)
        s = 1.0 / (1.0 + jnp.exp(-g))
        return da * u * (s * (1.0 + g * (1.0 - s))), da * (g * s)

    dgate, dup = _matmul(du2, wd, mode="nt", name="d_act", tm=TM, tn=1408, tk=D_MODEL, outs=[BF16, BF16],
                         extras=[(gate, (tm_e, 1408), _tile_ij), (up, (tm_e, 1408), _tile_ij)],
                         epilogue=dgu_epilogue)
    d_wg = _matmul(h1, dgate, mode="tn", name="dw_gate", tm=D_MODEL, tn=1408, tk=TM, outs=[BF16])
    d_wu = _matmul(h1, dup, mode="tn", name="dw_up", tm=D_MODEL, tn=1408, tk=TM, outs=[BF16])
    dh1 = _matmul(dgate, wg, mode="nt", name="dh1_gate", tm=TM, tn=D_MODEL, tk=D_FF, outs=[F32],
                  extras=[(du2, (tm_e, D_MODEL), _tile_ij)], epilogue=lambda acc, e: (ALPHA * e + acc,))
    dh1 = _matmul(dup, wu, mode="nt", name="dh1_up", tm=TM, tn=D_MODEL, tk=D_FF, outs=[F32],
                  extras=[(dh1, (tm_e, D_MODEL), _tile_ij)], epilogue=lambda acc, e: (e + acc,))

    def ln1_bwd(i, dh_ref, u_ref, g_ref, du_ref, acc_ref):
        xh, r = _ln_stats(u_ref[...])
        dh = dh_ref[...]
        du_ref[...] = _ln_bwd(dh, xh, r, g_ref[...])
        _acc_rows(i, acc_ref, {0: jnp.sum(dh * xh, axis=0, keepdims=True), 1: jnp.sum(dh, axis=0, keepdims=True)})

    du1, acc_ln1 = _rowwise(ln1_bwd, "ln1_bwd", S, TR, [(dh1, "t"), (u1, "t"), (ln1_g, "f")],
                            [((S, D_MODEL), F32, "t"), ((8, D_MODEL), F32, "f")])
    d_wo = _matmul(on, du1, mode="tn", name="dw_out", tm=D_MODEL, tn=D_MODEL, tk=TM, outs=[BF16])
    don = _matmul(du1, wo, mode="nt", name="d_on", tm=TM, tn=D_MODEL, tk=D_MODEL, outs=[F32])

    def rms_bwd(i, don_ref, osb_ref, ofx_ref, g_ref, he_ref, het_ref, dosb_ref, dofx_ref, acc_ref):
        o = jnp.concatenate([osb_ref[...], ofx_ref[...]], axis=1)
        hev, hetv = he_ref[...], het_ref[...]
        r = lax.rsqrt(_head_sums(o * o, hev, hetv) * (1.0 / HEAD_DIM) + RMS_EPS)
        dn = don_ref[...]
        dg = dn * g_ref[...]
        do = r * dg - o * (r * r * r) * (_head_sums(dg * o, hev, hetv) * (1.0 / HEAD_DIM))
        dosb_ref[...] = do[:, :GROUP_W]
        dofx_ref[...] = do[:, GROUP_W:]
        _acc_rows(i, acc_ref, {4: jnp.sum(dn * o * r, axis=0, keepdims=True)})

    do_sb, do_fx, acc_rms = _rowwise(
        rms_bwd, "rms_bwd", S, TR, [(don, "t"), (o_sb, "t"), (o_fx, "t"), (g_row, "f"), (he, "f"), (het, "f")],
        [((S, GROUP_W), F32, "t"), ((S, GROUP_W), F32, "t"), ((8, D_MODEL), F32, "f")])

    dq_sb, dk_sb, dv_sb = _sb_bwd(proj, 0, do_sb, st_sb, jmin_sb, BQ)
    jstart_fx2 = jnp.minimum(jstart_fx[:, 0::2], jstart_fx[:, 1::2])
    dq_fx, dk_fx, dv_fx, dc, dcq = _fox_bwd(proj, 12, do_fx, o_fx, st_fx, c_col, c_row, jstart_fx2, 2 * BQ, BQ)
    dc = dc[:, :2, :] + dcq[:, :, :2].transpose(0, 2, 1)
    dfl, dbf = _fgate_bwd(dc.reshape(N_FOX, S), lf)
    dp_sb = jnp.concatenate([dq_sb, dk_sb, dv_sb], axis=1).astype(BF16)
    dp_fx = jnp.concatenate([dq_fx, dk_fx, dv_fx], axis=1).astype(BF16)

    d_wsb = _matmul(x2, dp_sb, mode="tn", name="dw_in_sb", tm=D_MODEL, tn=QKV_W // 2, tk=TM, outs=[BF16])
    d_wfx = _matmul(x2, dp_fx, mode="tn", name="dw_in_fx", tm=D_MODEL, tn=QKV_W // 2, tk=TM, outs=[BF16])
    d_wft = _matmul(dfl, x2, mode="nn", name="dw_in_f", tm=N_FOX, tn=D_MODEL, tk=TM, outs=[BF16])
    dx = _matmul(dp_sb, w_sb, mode="nt", name="dx_sb", tm=TM, tn=D_MODEL, tk=QKV_W // 2, outs=[F32],
                 extras=[(du1, (tm_e, D_MODEL), _tile_ij)], epilogue=lambda acc, e: (ALPHA * e + acc,))
    dx = _matmul(dp_fx, w_fx, mode="nt", name="dx_fx", tm=TM, tn=D_MODEL, tk=QKV_W // 2, outs=[F32],
                 extras=[(dx, (tm_e, D_MODEL), _tile_ij)], epilogue=lambda acc, e: (e + acc,))
    dx = _matmul(dfl, wft, mode="tn", name="dx_f", tm=TM, tn=D_MODEL, tk=N_FOX, outs=[F32],
                 extras=[(dx, (tm_e, D_MODEL), _tile_ij)], epilogue=lambda acc, e: (e + acc,))

    d_wi = jnp.concatenate([d_wsb, d_wfx, d_wft.T], axis=1)
    d_wgu = jnp.concatenate([d_wg, d_wu], axis=1)
    parts = [d_wi.reshape(D_MODEL, 4, in_w).transpose(1, 0, 2).astype(BF16),
             d_wo.reshape(4, D_MODEL // 4, D_MODEL).astype(BF16),
             d_wgu.reshape(D_MODEL, 4, gu_w).transpose(1, 0, 2).astype(BF16),
             d_wd.reshape(4, D_FF // 4, D_MODEL).astype(BF16)]
    got = _exchange(parts, True)
    big_names = ("w_in", "w_out", "w_gate_up", "w_down")
    halves = [_sum_parts(p, "sum_" + nm, tr) for nm, p, tr in zip(big_names, got, (256, 128, 128, 176))]
    grads = _sibling_swap(halves)
    big = {}
    for nm, g, w, m, v, tr in zip(big_names, grads, (w_in, w_out, w_gate_up, w_down),
                                  (m_w_in, m_w_out, m_w_gate_up, m_w_down),
                                  (v_w_in, v_w_out, v_w_gate_up, v_w_down), (256, 256, 256, 176)):
        big[nm] = [r[None] for r in [g] + list(_adamw_call(g, w[0], m[0], v[0], "adamw_" + nm, tr))]

    small = acc_ln2 + acc_ln1 + acc_rms
    small = small + jnp.pad(dbf.reshape(1, N_FOX), ((5, 2), (0, D_MODEL - N_FOX)))
    (small_all,) = _exchange([small], False)
    sw = _pack_small(ln1_g, ln1_b, ln2_g, ln2_b, g_sb, g_fox, b_f)
    sm = _pack_small(m_ln1_g, m_ln1_b, m_ln2_g, m_ln2_b, m_g_sb, m_g_fox, m_b_f)
    sv = _pack_small(v_ln1_g, v_ln1_b, v_ln2_g, v_ln2_b, v_g_sb, v_g_fox, v_b_f)
    sg, sd, snm, snv, loss_blk = _sum_adamw_small(small_all, sw, sm, sv)
    sg, sd, snm, snv = _unpack_small(sg), _unpack_small(sd), _unpack_small(snm), _unpack_small(snv)

    names = ["w_in", "b_f", "g_sb", "g_fox", "w_out", "ln1_g", "ln1_b", "ln2_g", "ln2_b", "w_gate_up", "w_down"]
    outs = [loss_blk[0, 0], dx.reshape(1, S, D_MODEL)]
    for k, table in enumerate((sg, sd, snm, snv)):
        outs += [big[n][k] if n in big else table[n] for n in names]
    return tuple(outs)
```

```python
import functools

import numpy as np
import jax
import jax.numpy as jnp
from jax import lax
from jax.experimental import pallas as pl
from jax.experimental.pallas import tpu as pltpu

F32 = jnp.float32
BF16 = jnp.bfloat16

D_MODEL = 1024
HEAD_DIM = 64
LANES = 128
N_PAIRS = 4
GROUP_W = 512
QKV_W = 3072
D_FF = 2816
N_FOX = 8
ALPHA = 2.0 ** 0.25
LN_EPS = 1e-5
RMS_EPS = 1e-6
SCALE = HEAD_DIM ** -0.5
NEG_BIG = -1e30
FOX_SKIP = 30.0
SB_STOP = -105.0
ADAM_LR, ADAM_B1, ADAM_B2, ADAM_EPS, ADAM_WD, ADAM_STEP = 0.001, 0.9, 0.999, 1e-08, 0.01, 10
KV_SLOTS = 4
SCAN_GROUP = 8
ATTN_BLOCK = 256
VMEM_BIG = 56 * 1024 * 1024
MESH = pl.DeviceIdType.MESH

_NN = (((1,), (0,)), ((), ()))
_NT = (((1,), (1,)), ((), ()))
_TN = (((0,), (0,)), ((), ()))


def _dot(a, b, dims=_NN):
    return lax.dot_general(a, b, dims, preferred_element_type=F32)


def _split_dot(x, t):
    hi = x.astype(BF16)
    lo = (x - hi.astype(F32)).astype(BF16)
    return _dot(hi, t) + _dot(lo, t)


def _softplus(z):
    return jnp.maximum(z, 0.0) + jnp.log1p(jnp.exp(-jnp.abs(z)))


def _col(v, h):
    lane = lax.broadcasted_iota(jnp.int32, v.shape, 1)
    return jnp.sum(jnp.where(lane == h, v, 0.0), axis=1, keepdims=True)


def _two_sum(hi, lo, b):
    s = hi + b
    bb = s - hi
    err = (hi - (s - bb)) + (b - bb)
    return s, lo + err


def _params(vmem=None):
    return pltpu.CompilerParams(vmem_limit_bytes=vmem) if vmem else None


def _matmul(a, b, *, mode, name, tm, tn, tk, outs, extras=(), epilogue=None, vmem=None):
    if mode == "nn":
        (M, K), (_, N) = a.shape, b.shape
    elif mode == "nt":
        (M, K), (N, _) = a.shape, b.shape
    else:
        (K, M), (_, N) = a.shape, b.shape
    tm, tn, tk = min(tm, M), min(tn, N), min(tk, K)
    assert M % tm == 0 and N % tn == 0 and K % tk == 0, (name, M, N, K, tm, tn, tk)
    nk = K // tk
    dims = {"nn": _NN, "nt": _NT, "tn": _TN}[mode]
    if mode == "tn":
        a_spec = pl.BlockSpec((tk, tm), lambda i, j, k: (k, i))
    else:
        a_spec = pl.BlockSpec((tm, tk), lambda i, j, k: (i, k))
    if mode == "nt":
        b_spec = pl.BlockSpec((tn, tk), lambda i, j, k: (j, k))
    else:
        b_spec = pl.BlockSpec((tk, tn), lambda i, j, k: (k, j))
    ex_specs = [pl.BlockSpec(bs, (lambda i, j, k, f=f: f(i, j))) for (_, bs, f) in extras]
    ne, no = len(extras), len(outs)
    if epilogue is None:
        epilogue = lambda acc: (acc,)

    def body(a_ref, b_ref, *rest):
        ex_refs, out_refs, acc = rest[:ne], rest[ne:ne + no], rest[-1]
        k = pl.program_id(2)

        @pl.when(k == 0)
        def _():
            acc[...] = jnp.zeros_like(acc)

        acc[...] += _dot(a_ref[...].astype(BF16), b_ref[...].astype(BF16), dims)

        @pl.when(k == nk - 1)
        def _():
            res = epilogue(acc[...], *[e[...] for e in ex_refs])
            for r, o in zip(res, out_refs):
                o[...] = r.astype(o.dtype)

    res = pl.pallas_call(
        body, name=name, grid=(M // tm, N // tn, nk),
        in_specs=[a_spec, b_spec] + ex_specs,
        out_specs=[pl.BlockSpec((tm, tn), lambda i, j, k: (i, j)) for _ in outs],
        out_shape=[jax.ShapeDtypeStruct((M, N), d) for d in outs],
        scratch_shapes=[pltpu.VMEM((tm, tn), F32)],
        compiler_params=_params(vmem),
    )(a, b, *[e[0] for e in extras])
    return res[0] if no == 1 else res


def _tile_ij(i, j):
    return (i, j)


def _rowwise(fn, name, rows, tm, ins, outs, vmem=None):
    tm = min(tm, rows)
    assert rows % tm == 0

    def spec(shape, kind):
        if kind == "t":
            return pl.BlockSpec((tm,) + tuple(shape[1:]), lambda i: (i,) + (0,) * (len(shape) - 1))
        return pl.BlockSpec(tuple(shape), lambda i: (0,) * len(shape))

    def body(*refs):
        fn(pl.program_id(0), *refs)

    return pl.pallas_call(
        body, name=name, grid=(rows // tm,),
        in_specs=[spec(a.shape, k) for a, k in ins],
        out_specs=[spec(s, k) for s, _, k in outs],
        out_shape=[jax.ShapeDtypeStruct(s, d) for s, d, _ in outs],
        compiler_params=_params(vmem),
    )(*[a for a, _ in ins])


def _ln_stats(u):
    mu = jnp.mean(u, axis=-1, keepdims=True)
    d = u - mu
    var = jnp.mean(d * d, axis=-1, keepdims=True)
    r = lax.rsqrt(var + LN_EPS)
    return d * r, r


def _ln_bwd(dh, xh, r, g):
    dxh = dh * g
    m1 = jnp.mean(dxh, axis=-1, keepdims=True)
    m2 = jnp.mean(dxh * xh, axis=-1, keepdims=True)
    return r * (dxh - m1 - xh * m2)


def _acc_rows(i, ref, rows):
    @pl.when(i == 0)
    def _():
        ref[...] = jnp.zeros_like(ref)
    for r, v in rows.items():
        ref[pl.ds(r, 1), :] += v


def _head_sums(v, he, het):
    return _split_dot(_split_dot(v, he), het)


def _fgate_fwd(x, wft, bf_col, tm):
    S = x.shape[0]
    tm = min(tm, S)

    def body(wft_ref, bf_ref, x_ref, lf_ref):
        f = _dot(wft_ref[...], x_ref[...].astype(BF16), _NT) + bf_ref[...]
        lf_ref[...] = -_softplus(-f)

    return pl.pallas_call(
        body, name="fgate_fwd", grid=(S // tm,),
        in_specs=[pl.BlockSpec((N_FOX, D_MODEL), lambda i: (0, 0)), pl.BlockSpec((N_FOX, 1), lambda i: (0, 0)),
                  pl.BlockSpec((tm, D_MODEL), lambda i: (i, 0))],
        out_specs=pl.BlockSpec((N_FOX, tm), lambda i: (0, i)),
        out_shape=jax.ShapeDtypeStruct((N_FOX, S), F32),
    )(wft, bf_col, x)


def _chunk_scan(v, reverse):
    lane = lax.broadcasted_iota(jnp.int32, v.shape, 1)
    sh = 1
    while sh < LANES:
        if reverse:
            v = v + jnp.where(lane < LANES - sh, pltpu.roll(v, LANES - sh, 1), 0.0)
        else:
            v = v + jnp.where(lane >= sh, pltpu.roll(v, sh, 1), 0.0)
        sh *= 2
    return v


def _cumsum_fwd(lf):
    n, S = lf.shape
    nc = S // LANES

    grp = min(SCAN_GROUP, nc)

    def body(lf_ref, c_ref):
        def step(gi, carry):
            sls = [pl.ds(pl.multiple_of((gi * grp + g) * LANES, LANES), LANES) for g in range(grp)]
            vs = [_chunk_scan(lf_ref[:, sl], False) for sl in sls]
            tots = [_col(v, LANES - 1) for v in vs]
            for sl, v, t in zip(sls, vs, tots):
                c_ref[:, sl] = v + carry
                carry = carry + t
            return carry
        lax.fori_loop(0, nc // grp, step, jnp.zeros((n, 1), F32))

    return pl.pallas_call(body, name="cumsum_fwd", out_shape=jax.ShapeDtypeStruct((n, S), F32))(lf)


def _fgate_bwd(dc, lf):
    n, S = dc.shape
    nc = S // LANES

    grp = min(SCAN_GROUP, nc)

    def body(dc_ref, lf_ref, dfl_ref, dbf_ref):
        def step(t, carry):
            car, tot = carry
            gi = nc // grp - 1 - t
            sls = [pl.ds(pl.multiple_of((gi * grp + g) * LANES, LANES), LANES) for g in range(grp)]
            vs = [_chunk_scan(dc_ref[:, sl], True) for sl in sls]
            firsts = [_col(v, 0) for v in vs]
            for sl, v, f in reversed(list(zip(sls, vs, firsts))):
                dfl = (v + car) * (1.0 - jnp.exp(lf_ref[:, sl]))
                dfl_ref[:, sl] = dfl
                tot = tot + jnp.sum(dfl, axis=1, keepdims=True)
                car = car + f
            return car, tot
        _, tot = lax.fori_loop(0, nc // grp, step, (jnp.zeros((n, 1), F32), jnp.zeros((n, 1), F32)))
        dbf_ref[...] = tot

    return pl.pallas_call(body, name="fgate_bwd",
                          out_shape=[jax.ShapeDtypeStruct((n, S), F32), jax.ShapeDtypeStruct((n, 1), F32)])(dc, lf)


def _tri_matrices(b):
    r = np.arange(b)
    tfwd = (r[:, None] <= r[None, :]).astype(np.float32)
    return jnp.asarray(tfwd, BF16), jnp.asarray(tfwd.T, BF16)


def _kv_copies(kv_hbm, kbuf, vbuf, sems, pair_col, bq, j, slot):
    rows = pl.ds(pl.multiple_of(j * bq, bq), bq)

    def cols(c):
        return pl.ds(pl.multiple_of((pair_col + c) * LANES, LANES), LANES)

    return (pltpu.make_async_copy(kv_hbm.at[rows, cols(4)], kbuf.at[slot], sems.at[0, slot]),
            pltpu.make_async_copy(kv_hbm.at[rows, cols(8)], vbuf.at[slot], sems.at[1, slot]))


def _first_two_up(first_block, per=1):
    def blocks(pair, blk):
        first = first_block(pair, blk)
        return first, first + 1, first + 1 <= per * blk + per - 1
    return blocks


def _first_two_down(pair, blk):
    return blk, blk - 1, blk > 0


def _start_two(fetch, pair, first, second, has_second):
    for cp in fetch(first, 0, pair):
        cp.start()

    @pl.when(has_second)
    def _():
        for cp in fetch(second, 1, pair):
            cp.start()


def _kv_fetcher(kv_hbm, kbuf, vbuf, sems, col0, bq, p, i, blocks):
    def fetch(j, slot, pair=p):
        return _kv_copies(kv_hbm, kbuf, vbuf, sems, col0 + pair, bq, j, slot)

    pl.when(jnp.logical_and(p == 0, i == 0))(lambda: _start_two(fetch, p, *blocks(p, i)))
    return fetch


def _prefetch_next(fetch, p, i, nq, blocks):
    wrap = i == nq - 1

    @pl.when(jnp.logical_not(jnp.logical_and(wrap, p == N_PAIRS - 1)))
    def _():
        pair, blk = jnp.where(wrap, p + 1, p), jnp.where(wrap, 0, i + 1)
        _start_two(fetch, pair, *blocks(pair, blk))


def _masked_pair(v, lane_is_a, scale=1.0):
    v = v.astype(F32) * scale
    return jnp.where(lane_is_a, v, 0.0).astype(BF16), jnp.where(lane_is_a, 0.0, v).astype(BF16)


def _sb_fwd(proj, col0, bq):
    S = proj.shape[0]
    bq = min(bq, S)
    nq = S // bq
    _, trev = _tri_matrices(bq)

    def body(q_ref, kv_hbm, trev_ref, o_ref, st_ref, jmin_ref, acc_a, acc_b, qa, qb, rs, kbuf, vbuf, sems):
        p, i = pl.program_id(0), pl.program_id(1)
        fetch = _kv_fetcher(kv_hbm, kbuf, vbuf, sems, col0, bq, p, i, _first_two_down)
        is_a = lax.broadcasted_iota(jnp.int32, (bq, LANES), 1) < HEAD_DIM
        acc_a[...] = jnp.zeros_like(acc_a)
        acc_b[...] = jnp.zeros_like(acc_b)
        rs[...] = jnp.zeros_like(rs)
        qa[...], qb[...] = _masked_pair(q_ref[...], is_a, SCALE)

        def tiles(blocks):
            hs, qs, accs, trev_m = (0, 1), (qa, qb), (acc_a, acc_b), trev_ref[...]
            kv = [(kbuf[s], vbuf[s]) for s, _ in blocks]
            bh = [(b, h) for b in range(len(blocks)) for h in hs]
            tri = lax.broadcasted_iota(jnp.int32, (bq, bq), 0) > lax.broadcasted_iota(jnp.int32, (bq, bq), 1)
            z = {(b, h): _dot(qs[h][...], kv[b][0], _NT) for b, h in bh}
            lk = {(b, h): -_softplus(z[b, h]) for b, h in bh}
            lk = {(b, h): jnp.where(tri, lk[b, h], 0.0) if blocks[b][1] else lk[b, h] for b, h in bh}
            suf = {(b, h): _split_dot(lk[b, h], trev_m) for b, h in bh}
            tot = {(b, h): jnp.sum(lk[b, h], axis=1, keepdims=True) for b, h in bh}
            right = {}
            for h in hs:
                r = rs[2 * h] + rs[2 * h + 1]
                for b in range(len(blocks)):
                    right[b, h] = r
                    r = r + tot[b, h]
            w = {(b, h): jnp.exp(z[b, h] + suf[b, h] + right[b, h]) for b, h in bh}
            w = {(b, h): jnp.where(tri, w[b, h], 0.0) if blocks[b][1] else w[b, h] for b, h in bh}
            pv = {(b, h): _dot(w[b, h].astype(BF16), kv[b][1]) for b, h in bh}
            for h in hs:
                accs[h][...] += sum([pv[b, h] for b in range(1, len(blocks))], pv[0, h])
                hi, lo = rs[2 * h], rs[2 * h + 1]
                for b in range(len(blocks)):
                    hi, lo = _two_sum(hi, lo, tot[b, h])
                rs[2 * h], rs[2 * h + 1] = hi, lo

        def live():
            return (jnp.max(jnp.maximum(rs[0], rs[2])) > SB_STOP).astype(jnp.int32)

        for cp in fetch(i, 0):
            cp.wait()
        pl.when(i == 0)(functools.partial(tiles, [(0, True)]))

        @pl.when(i > 0)
        def _():
            for cp in fetch(i - 1, 1):
                cp.wait()
            tiles([(0, True), (1, False)])

        def step(carry):
            j, _ = carry
            slot = lax.rem(i - j, 2)
            for cp in fetch(j, slot):
                cp.start()
            for cp in fetch(j, slot):
                cp.wait()
            tiles([(slot, False)])
            return j - 1, live()

        j_end, _ = lax.while_loop(lambda c: jnp.logical_and(c[0] >= 0, c[1] > 0), step, (i - 2, live()))
        _prefetch_next(fetch, p, i, nq, _first_two_down)
        jmin_ref[p, i] = jnp.maximum(j_end + 1, 0)
        o_ref[...] = jnp.where(is_a, acc_a[...], acc_b[...])
        lane8 = lax.broadcasted_iota(jnp.int32, (bq, 8), 1)
        st = jnp.zeros((bq, 8), F32)
        for c, src in enumerate((0, 2, 1, 3)):
            st = jnp.where(lane8 == c, rs[src], st)
        st_ref[0] = st

    return pl.pallas_call(
        body, name="sb_fwd", grid=(N_PAIRS, nq),
        in_specs=[pl.BlockSpec((bq, LANES), lambda p, i: (i, col0 + p)),
                  pl.BlockSpec(memory_space=pl.ANY),
                  pl.BlockSpec((bq, bq), lambda p, i: (0, 0))],
        out_specs=[pl.BlockSpec((bq, LANES), lambda p, i: (i, p)),
                   pl.BlockSpec((1, bq, 8), lambda p, i: (p, i, 0)),
                   pl.BlockSpec(memory_space=pltpu.SMEM)],
        out_shape=[jax.ShapeDtypeStruct((S, GROUP_W), F32), jax.ShapeDtypeStruct((N_PAIRS, S, 8), F32),
                   jax.ShapeDtypeStruct((N_PAIRS, nq), jnp.int32)],
        scratch_shapes=[pltpu.VMEM((bq, LANES), F32), pltpu.VMEM((bq, LANES), F32),
                        pltpu.VMEM((bq, LANES), BF16), pltpu.VMEM((bq, LANES), BF16),
                        pltpu.VMEM((4, bq, 1), F32),
                        pltpu.VMEM((2, bq, LANES), BF16), pltpu.VMEM((2, bq, LANES), BF16),
                        pltpu.SemaphoreType.DMA((2, 2))],
    )(proj, proj, trev)


def _sb_bwd(proj, col0, do, st, jmin, bq):
    S = proj.shape[0]
    bq = min(bq, S)
    nq = S // bq
    tfwd, trev = _tri_matrices(bq)

    def body(jmin_ref, q_ref, kv_hbm, do_ref, st_ref, tfwd_ref, trev_ref,
             dq_ref, dk_ref, dv_ref, dq_a, dq_b, qa, qb, doa, dob, rs, kbuf, vbuf, sems):
        p, i = pl.program_id(0), pl.program_id(1)
        j0 = jmin_ref[p, i]
        first_two = _first_two_up(lambda pair, blk: jmin_ref[pair, blk])
        fetch = _kv_fetcher(kv_hbm, kbuf, vbuf, sems, col0, bq, p, i, first_two)
        is_a = lax.broadcasted_iota(jnp.int32, (bq, LANES), 1) < HEAD_DIM

        @pl.when(i == 0)
        def _():
            dk_ref[...] = jnp.zeros_like(dk_ref)
            dv_ref[...] = jnp.zeros_like(dv_ref)

        dq_a[...] = jnp.zeros_like(dq_a)
        dq_b[...] = jnp.zeros_like(dq_b)
        rs[...] = jnp.zeros_like(rs)
        st_v = st_ref[0]
        for h in range(2):
            rs[6 + 2 * h], rs[7 + 2 * h] = _col(st_v, h), _col(st_v, 2 + h)
        qa[...], qb[...] = _masked_pair(q_ref[...], is_a, SCALE)
        doa[...], dob[...] = _masked_pair(do_ref[...], is_a)

        def tiles(blocks):
            hs, qs, dos, dqs = (0, 1), (qa, qb), (doa, dob), (dq_a, dq_b)
            tfwd_m, trev_m = tfwd_ref[...], trev_ref[...]
            kv = [(kbuf[s], vbuf[s]) for _, s, _ in blocks]
            nb = len(blocks)
            bh = [(b, h) for b in range(nb) for h in hs]
            tri = lax.broadcasted_iota(jnp.int32, (bq, bq), 0) > lax.broadcasted_iota(jnp.int32, (bq, bq), 1)

            def mask(x, b):
                return jnp.where(tri, x, 0.0) if blocks[b][2] else x

            z = {(b, h): _dot(qs[h][...], kv[b][0], _NT) for b, h in bh}
            dw = {(b, h): _dot(dos[h][...], kv[b][1], _NT) for b, h in bh}
            lk = {(b, h): mask(-_softplus(z[b, h]), b) for b, h in bh}
            suf = {(b, h): _split_dot(lk[b, h], trev_m) for b, h in bh}
            tot = {(b, h): jnp.sum(lk[b, h], axis=1, keepdims=True) for b, h in bh}
            pre = {}
            for h in hs:
                run = (rs[3 * h], rs[3 * h + 1])
                for b in range(nb):
                    run = _two_sum(run[0], run[1], tot[b, h])
                    pre[b, h] = run
            right = {(b, h): (rs[6 + 2 * h] - pre[b, h][0]) + (rs[7 + 2 * h] - pre[b, h][1]) for b, h in bh}
            w = {(b, h): mask(jnp.exp(z[b, h] + suf[b, h] + right[b, h]), b) for b, h in bh}
            g = {(b, h): dw[b, h] * w[b, h] for b, h in bh}
            gpre = {(b, h): _split_dot(g[b, h], tfwd_m) for b, h in bh}
            gtot = {(b, h): jnp.sum(g[b, h], axis=1, keepdims=True) for b, h in bh}
            gleft = {}
            for h in hs:
                run = rs[3 * h + 2]
                for b in range(nb):
                    gleft[b, h] = run
                    run = run + gtot[b, h]
                gleft[nb, h] = run
            dz = {(b, h): mask(g[b, h] - jnp.exp(z[b, h] + lk[b, h]) * (gpre[b, h] + gleft[b, h]), b) for b, h in bh}
            dzb = {(b, h): dz[b, h].astype(BF16) for b, h in bh}
            wb = {(b, h): w[b, h].astype(BF16) for b, h in bh}
            dqc = {(b, h): _dot(dzb[b, h], kv[b][0]) for b, h in bh}
            dkc = {(b, h): _dot(dzb[b, h], qs[h][...], _TN) for b, h in bh}
            dvc = {(b, h): _dot(wb[b, h], dos[h][...], _TN) for b, h in bh}
            for h in hs:
                rs[3 * h], rs[3 * h + 1] = pre[nb - 1, h]
                rs[3 * h + 2] = gleft[nb, h]
                dqs[h][...] += sum([dqc[b, h] for b in range(1, nb)], dqc[0, h])
            for b, (j, _, _) in enumerate(blocks):
                rows = pl.ds(pl.multiple_of(j * bq, bq), bq)
                dk_ref[rows, :] += dkc[b, 0] + dkc[b, 1]
                dv_ref[rows, :] += dvc[b, 0] + dvc[b, 1]

        def single(j, slot, masked):
            tiles([(j, slot, masked)])

        def wait(j):
            slot = lax.rem(j - j0, KV_SLOTS)
            for cp in fetch(j, slot):
                cp.wait()
            return slot

        _walk_up(fetch, j0, i, i, single, stop=jnp.maximum(i - 1, j0))

        @pl.when(j0 < i)
        def _():
            tiles([(i - 1, wait(i - 1), False), (i, wait(i), True)])

        @pl.when(j0 == i)
        def _():
            tiles([(i, wait(i), True)])

        _prefetch_next(fetch, p, i, nq, first_two)
        dq_ref[...] = jnp.where(is_a, dq_a[...], dq_b[...]) * SCALE

    grid_spec = pltpu.PrefetchScalarGridSpec(
        num_scalar_prefetch=1, grid=(N_PAIRS, nq),
        in_specs=[pl.BlockSpec((bq, LANES), lambda p, i, jm: (i, col0 + p)),
                  pl.BlockSpec(memory_space=pl.ANY),
                  pl.BlockSpec((bq, LANES), lambda p, i, jm: (i, p)),
                  pl.BlockSpec((1, bq, 8), lambda p, i, jm: (p, i, 0)),
                  pl.BlockSpec((bq, bq), lambda p, i, jm: (0, 0)),
                  pl.BlockSpec((bq, bq), lambda p, i, jm: (0, 0))],
        out_specs=[pl.BlockSpec((bq, LANES), lambda p, i, jm: (i, p)),
                   pl.BlockSpec((S, LANES), lambda p, i, jm: (0, p)),
                   pl.BlockSpec((S, LANES), lambda p, i, jm: (0, p))],
        scratch_shapes=[pltpu.VMEM((bq, LANES), F32), pltpu.VMEM((bq, LANES), F32)]
        + [pltpu.VMEM((bq, LANES), BF16)] * 4 + [pltpu.VMEM((10, bq, 1), F32)]
        + [pltpu.VMEM((KV_SLOTS, bq, LANES), BF16)] * 2 + [pltpu.SemaphoreType.DMA((2, KV_SLOTS))])
    return pl.pallas_call(
        body, name="sb_bwd", grid_spec=grid_spec,
        out_shape=[jax.ShapeDtypeStruct((S, GROUP_W), F32)] * 3,
        compiler_params=_params(VMEM_BIG),
    )(jmin, proj, proj, do, st, tfwd, trev)


def _walk_up(fetch, j0, diag, last, tile, stop=None):
    ahead = KV_SLOTS - 1
    stop = last + 1 if stop is None else stop

    def start(j):
        @pl.when(j <= last)
        def _():
            for cp in fetch(j, lax.rem(j - j0, KV_SLOTS)):
                cp.start()

    for d in range(2, ahead):
        start(j0 + d)

    def step(j, carry):
        slot = lax.rem(j - j0, KV_SLOTS)
        for cp in fetch(j, slot):
            cp.wait()
        start(j + ahead)
        pl.when(j >= diag)(functools.partial(tile, j, slot, True))
        pl.when(j < diag)(functools.partial(tile, j, slot, False))
        return carry

    lax.fori_loop(j0, stop, step, 0)


def _causal(bq, bk, i, j):
    row = lax.broadcasted_iota(jnp.int32, (bq, bk), 0)
    col = lax.broadcasted_iota(jnp.int32, (bq, bk), 1)
    return col - row <= i * bq - j * bk


def _by_heads(j, first_a, first_b, heads):
    on_a, on_b = j >= first_a, j >= first_b
    pl.when(jnp.logical_and(on_a, on_b))(functools.partial(heads, (0, 1)))
    pl.when(jnp.logical_and(on_a, jnp.logical_not(on_b)))(functools.partial(heads, (0,)))
    pl.when(jnp.logical_and(on_b, jnp.logical_not(on_a)))(functools.partial(heads, (1,)))


def _fox_start_blocks(proj, col0, c, bq, bk):
    S = proj.shape[0]
    nq, nk, nh = S // bq, S // bk, 2 * N_PAIRS

    def heads(first):
        return proj[:, first * LANES:(first + N_PAIRS) * LANES].astype(F32).reshape(S, nh, HEAD_DIM)

    q, k = heads(col0), heads(col0 + 4)
    qn = jnp.sqrt(jnp.sum(q * q, axis=-1))
    kmax = jnp.sqrt(jnp.sum(k * k, axis=-1)).max(axis=0)
    top = SCALE * (qn * kmax[None, :] - jnp.sum(q * k, axis=-1)) + c.T
    top = top.reshape(nq, bq, nh).max(axis=1)
    c_last = c[:, bk - 1::bk].T
    live = top[:, None, :] - c_last[None, :, :] >= -FOX_SKIP

    def first_block(lv):
        first = jnp.where(lv.any(axis=1), jnp.argmax(lv, axis=1), nk)
        return jnp.minimum(first, (bq // bk) * jnp.arange(nq)[:, None]).T.astype(jnp.int32)

    return jnp.concatenate([first_block(live.reshape(nq, nk, N_PAIRS, 2).any(axis=-1)), first_block(live)], axis=0)


def _fox_fwd(proj, col0, c_col, c_row, jstart, bq, bk):
    S = proj.shape[0]
    nq, per = S // bq, bq // bk

    def body(js_ref, q_ref, kv_hbm, cc_ref, cr_ref, o_ref, st_ref, acc_a, acc_b, qa, qb, ml, kbuf, vbuf, sems):
        p, i = pl.program_id(0), pl.program_id(1)
        j0 = js_ref[p, i]
        first_two = _first_two_up(lambda pair, blk: js_ref[pair, blk], per)
        fetch = _kv_fetcher(kv_hbm, kbuf, vbuf, sems, col0, bk, p, i, first_two)
        is_a = lax.broadcasted_iota(jnp.int32, (bq, LANES), 1) < HEAD_DIM
        acc_a[...] = jnp.zeros_like(acc_a)
        acc_b[...] = jnp.zeros_like(acc_b)
        ml[0] = jnp.full((bq, 1), NEG_BIG, F32)
        ml[2] = jnp.full((bq, 1), NEG_BIG, F32)
        ml[1] = jnp.zeros((bq, 1), F32)
        ml[3] = jnp.zeros((bq, 1), F32)
        cc = cc_ref[0]
        ml[4], ml[5] = _col(cc, 0), _col(cc, 1)
        qa[...], qb[...] = _masked_pair(q_ref[...], is_a, SCALE)

        def tile(j, slot, masked):
            k, v = kbuf[slot], vbuf[slot]
            cols = pl.ds(pl.multiple_of(j * bk, bk), bk)
            if masked:
                tri = _causal(bq, bk, i, j)

            def heads(hs):
                qs, accs = (qa, qb), (acc_a, acc_b)
                s = {h: _dot(qs[h][...], k, _NT) - cr_ref[0, pl.ds(h, 1), cols] for h in hs}
                if masked:
                    s = {h: jnp.where(tri, s[h], NEG_BIG) for h in hs}
                top = {h: jnp.max(s[h], axis=1, keepdims=True) for h in hs}
                m_new = {h: jnp.maximum(ml[2 * h], top[h] + ml[4 + h]) for h in hs}
                a = {h: jnp.exp(ml[2 * h] - m_new[h]) for h in hs}
                pr = {h: jnp.exp(s[h] - (m_new[h] - ml[4 + h])) for h in hs}
                tot = {h: jnp.sum(pr[h], axis=1, keepdims=True) for h in hs}
                pv = {h: _dot(pr[h].astype(BF16), v) for h in hs}
                for h in hs:
                    ml[2 * h] = m_new[h]
                    ml[2 * h + 1] = a[h] * ml[2 * h + 1] + tot[h]
                    accs[h][...] = a[h] * accs[h][...] + pv[h]

            _by_heads(j, js_ref[N_PAIRS + 2 * p, i], js_ref[N_PAIRS + 2 * p + 1, i], heads)

        _walk_up(fetch, j0, per * i, per * i + per - 1, tile)
        _prefetch_next(fetch, p, i, nq, first_two)
        o_ref[...] = jnp.where(is_a, acc_a[...] / ml[1], acc_b[...] / ml[3])
        lane8 = lax.broadcasted_iota(jnp.int32, (bq, 8), 1)
        st = jnp.where(lane8 == 0, ml[0] + jnp.log(ml[1]), 0.0)
        st_ref[0] = jnp.where(lane8 == 1, ml[2] + jnp.log(ml[3]), st)

    grid_spec = pltpu.PrefetchScalarGridSpec(
        num_scalar_prefetch=1, grid=(N_PAIRS, nq),
        in_specs=[pl.BlockSpec((bq, LANES), lambda p, i, js: (i, col0 + p)),
                  pl.BlockSpec(memory_space=pl.ANY),
                  pl.BlockSpec((1, bq, 8), lambda p, i, js: (p, i, 0)),
                  pl.BlockSpec((1, 8, S), lambda p, i, js: (p, 0, 0))],
        out_specs=[pl.BlockSpec((bq, LANES), lambda p, i, js: (i, p)),
                   pl.BlockSpec((1, bq, 8), lambda p, i, js: (p, i, 0))],
        scratch_shapes=[pltpu.VMEM((bq, LANES), F32), pltpu.VMEM((bq, LANES), F32),
                        pltpu.VMEM((bq, LANES), BF16), pltpu.VMEM((bq, LANES), BF16),
                        pltpu.VMEM((6, bq, 1), F32),
                        pltpu.VMEM((KV_SLOTS, bk, LANES), BF16), pltpu.VMEM((KV_SLOTS, bk, LANES), BF16),
                        pltpu.SemaphoreType.DMA((2, KV_SLOTS))])
    return pl.pallas_call(
        body, name="fox_fwd", grid_spec=grid_spec,
        out_shape=[jax.ShapeDtypeStruct((S, GROUP_W), F32), jax.ShapeDtypeStruct((N_PAIRS, S, 8), F32)],
    )(jstart, proj, proj, c_col, c_row)


def _fox_bwd(proj, col0, do, o, st, c_col, c_row, jstart, bq, bk):
    S = proj.shape[0]
    nq, per = S // bq, bq // bk

    def body(js_ref, q_ref, kv_hbm, do_ref, o_ref, st_ref, cc_ref, cr_ref,
             dq_ref, dk_ref, dv_ref, dc_ref, dcq_ref, dq_a, dq_b, qa, qb, doa, dob, dd, kbuf, vbuf, sems):
        p, i = pl.program_id(0), pl.program_id(1)
        j0 = js_ref[p, i]
        first_two = _first_two_up(lambda pair, blk: js_ref[pair, blk], per)
        fetch = _kv_fetcher(kv_hbm, kbuf, vbuf, sems, col0, bk, p, i, first_two)
        is_a = lax.broadcasted_iota(jnp.int32, (bq, LANES), 1) < HEAD_DIM

        @pl.when(i == 0)
        def _():
            dk_ref[...] = jnp.zeros_like(dk_ref)
            dv_ref[...] = jnp.zeros_like(dv_ref)
            dc_ref[...] = jnp.zeros_like(dc_ref)

        dq_a[...] = jnp.zeros_like(dq_a)
        dq_b[...] = jnp.zeros_like(dq_b)
        qa[...], qb[...] = _masked_pair(q_ref[...], is_a, SCALE)
        dov = do_ref[...]
        doa[...], dob[...] = _masked_pair(dov, is_a)
        prod = dov * o_ref[...]
        dd[0] = jnp.sum(jnp.where(is_a, prod, 0.0), axis=1, keepdims=True)
        dd[1] = jnp.sum(jnp.where(is_a, 0.0, prod), axis=1, keepdims=True)
        dd[2] = jnp.zeros((bq, 1), F32)
        dd[3] = jnp.zeros((bq, 1), F32)
        cc, st_v = cc_ref[0], st_ref[0]
        dd[4], dd[5] = _col(cc, 0) - _col(st_v, 0), _col(cc, 1) - _col(st_v, 1)

        def tile(j, slot, masked):
            k, v = kbuf[slot], vbuf[slot]
            if masked:
                tri = _causal(bq, bk, i, j)
            cols = pl.ds(pl.multiple_of(j * bk, bk), bk)

            def heads(hs):
                qs, dos, dqs = (qa, qb), (doa, dob), (dq_a, dq_b)
                z = {h: _dot(qs[h][...], k, _NT) for h in hs}
                dp = {h: _dot(dos[h][...], v, _NT) for h in hs}
                pr = {h: jnp.exp(z[h] - cr_ref[0, pl.ds(h, 1), cols] + dd[4 + h]) for h in hs}
                if masked:
                    pr = {h: jnp.where(tri, pr[h], 0.0) for h in hs}
                ds = {h: pr[h] * (dp[h] - dd[h]) for h in hs}
                csum = {h: jnp.sum(ds[h], axis=0, keepdims=True) for h in hs}
                rsum = {h: jnp.sum(ds[h], axis=1, keepdims=True) for h in hs}
                dsb = {h: ds[h].astype(BF16) for h in hs}
                prb = {h: pr[h].astype(BF16) for h in hs}
                dqc = {h: _dot(dsb[h], k) for h in hs}
                dkc = [_dot(dsb[h], qs[h][...], _TN) for h in hs]
                dvc = [_dot(prb[h], dos[h][...], _TN) for h in hs]
                for h in hs:
                    dc_ref[0, pl.ds(h, 1), cols] -= csum[h]
                    dd[2 + h] += rsum[h]
                    dqs[h][...] += dqc[h]
                dk_ref[cols, :] += sum(dkc[1:], dkc[0])
                dv_ref[cols, :] += sum(dvc[1:], dvc[0])

            _by_heads(j, js_ref[N_PAIRS + 2 * p, i], js_ref[N_PAIRS + 2 * p + 1, i], heads)

        _walk_up(fetch, j0, per * i, per * i + per - 1, tile)
        _prefetch_next(fetch, p, i, nq, first_two)
        dq_ref[...] = jnp.where(is_a, dq_a[...], dq_b[...]) * SCALE
        lane8 = lax.broadcasted_iota(jnp.int32, (bq, 8), 1)
        dcq_ref[0] = jnp.where(lane8 == 0, dd[2], jnp.where(lane8 == 1, dd[3], 0.0))

    grid_spec = pltpu.PrefetchScalarGridSpec(
        num_scalar_prefetch=1, grid=(N_PAIRS, nq),
        in_specs=[pl.BlockSpec((bq, LANES), lambda p, i, js: (i, col0 + p)),
                  pl.BlockSpec(memory_space=pl.ANY),
                  pl.BlockSpec((bq, LANES), lambda p, i, js: (i, p)),
                  pl.BlockSpec((bq, LANES), lambda p, i, js: (i, p)),
                  pl.BlockSpec((1, bq, 8), lambda p, i, js: (p, i, 0)),
                  pl.BlockSpec((1, bq, 8), lambda p, i, js: (p, i, 0)),
                  pl.BlockSpec((1, 8, S), lambda p, i, js: (p, 0, 0))],
        out_specs=[pl.BlockSpec((bq, LANES), lambda p, i, js: (i, p)),
                   pl.BlockSpec((S, LANES), lambda p, i, js: (0, p)),
                   pl.BlockSpec((S, LANES), lambda p, i, js: (0, p)),
                   pl.BlockSpec((1, 8, S), lambda p, i, js: (p, 0, 0)),
                   pl.BlockSpec((1, bq, 8), lambda p, i, js: (p, i, 0))],
        scratch_shapes=[pltpu.VMEM((bq, LANES), F32), pltpu.VMEM((bq, LANES), F32)]
        + [pltpu.VMEM((bq, LANES), BF16)] * 4 + [pltpu.VMEM((6, bq, 1), F32)]
        + [pltpu.VMEM((KV_SLOTS, bk, LANES), BF16)] * 2 + [pltpu.SemaphoreType.DMA((2, KV_SLOTS))])
    return pl.pallas_call(
        body, name="fox_bwd", grid_spec=grid_spec,
        out_shape=[jax.ShapeDtypeStruct((S, GROUP_W), F32)] * 3
        + [jax.ShapeDtypeStruct((N_PAIRS, 8, S), F32), jax.ShapeDtypeStruct((N_PAIRS, S, 8), F32)],
        compiler_params=_params(VMEM_BIG),
    )(jstart, proj, proj, do, o, st, c_col, c_row)


_HBM = pl.BlockSpec(memory_space=pltpu.HBM)


def _coords():
    return lax.axis_index("x"), lax.axis_index("y"), lax.axis_index("c")


def _gather_copies(ins, outs, send_sems, recv_sems, loc_sems):
    n = len(ins)
    x, y, c = _coords()
    mine = 2 * x + y
    chips = [(1 - x, y), (x, 1 - y), (1 - x, 1 - y)]

    def copy(w, r, slab, to):
        return pltpu.make_async_remote_copy(
            src_ref=ins[w], dst_ref=outs[w].at[slab], send_sem=send_sems.at[3 * w + r],
            recv_sem=recv_sems.at[3 * w + r], device_id=to, device_id_type=MESH)

    def own():
        local = [pltpu.make_async_copy(ins[w], outs[w].at[mine], loc_sems.at[w]) for w in range(n)]
        return local, [copy(w, r, mine, (cx, cy, c)) for w in range(n) for r, (cx, cy) in enumerate(chips)]

    def start():
        local, sends = own()
        for cp in local + sends:
            cp.start()

    def wait():
        local, sends = own()
        for w in range(n):
            for r, (cx, cy) in enumerate(chips):
                copy(w, r, 2 * cx + cy, (cx, cy, c)).wait_recv()
        for cp in sends:
            cp.wait_send()
        for cp in local:
            cp.wait()

    return start, wait


def _gather_shapes(shards):
    n = len(shards)
    return ([jax.ShapeDtypeStruct((4,) + s.shape, s.dtype) for s in shards],
            [pltpu.SemaphoreType.DMA((3 * n,)), pltpu.SemaphoreType.DMA((3 * n,)), pltpu.SemaphoreType.DMA((n,))])


def _allgather_chips(shards):
    n = len(shards)

    def body(*refs):
        start, wait = _gather_copies(refs[:n], refs[n:2 * n], *refs[2 * n:])
        start()
        wait()

    out_shape, sems = _gather_shapes(shards)
    return pl.pallas_call(body, name="allgather_weights", in_specs=[_HBM] * n, out_specs=[_HBM] * n,
                          out_shape=out_shape, scratch_shapes=sems)(*shards)


def _proj_gather(x, w, shards, tm, tn):
    (M, K), N, n = x.shape, w.shape[1], len(shards)
    tm = min(tm, M)
    gi, gj = M // tm, N // tn

    def body(a_ref, b_ref, *rest):
        o_ref = rest[n]
        start, wait = _gather_copies(rest[:n], rest[n + 1:2 * n + 1], *rest[2 * n + 1:])
        i, j = pl.program_id(0), pl.program_id(1)
        pl.when(jnp.logical_and(i == 0, j == 0))(start)
        o_ref[...] = _dot(a_ref[...].astype(BF16), b_ref[...]).astype(o_ref.dtype)
        pl.when(jnp.logical_and(i == gi - 1, j == gj - 1))(wait)

    out_shape, sems = _gather_shapes(shards)
    return pl.pallas_call(
        body, name="proj_gather", grid=(gi, gj),
        in_specs=[pl.BlockSpec((tm, K), lambda i, j: (i, 0)), pl.BlockSpec((K, tn), lambda i, j: (0, j))] + [_HBM] * n,
        out_specs=[pl.BlockSpec((tm, tn), lambda i, j: (i, j))] + [_HBM] * n,
        out_shape=[jax.ShapeDtypeStruct((M, N), BF16)] + out_shape, scratch_shapes=sems,
    )(x, w, *shards)


def _exchange(parts, per_chip):
    n = len(parts)
    half = [p.shape[1] // 2 for p in parts] if per_chip else None

    def body(*refs):
        ins, outs = refs[:n], refs[n:2 * n]
        send_sems, recv_sems, loc_sems = refs[2 * n:]
        x, y, c = _coords()
        me = 4 * x + 2 * y + c
        peers = [(x ^ fx, y ^ fy, c ^ fc) for fx in (0, 1) for fy in (0, 1) for fc in (0, 1)][1:]

        def src(w, dev):
            if not per_chip:
                return ins[w]
            return ins[w].at[2 * dev[0] + dev[1], pl.ds(pl.multiple_of(dev[2] * half[w], 16), half[w]), :]

        local = [pltpu.make_async_copy(src(w, (x, y, c)), outs[w].at[me], loc_sems.at[w]) for w in range(n)]
        for cp in local:
            cp.start()

        def copy(w, r, source, slab, to):
            return pltpu.make_async_remote_copy(
                src_ref=source, dst_ref=outs[w].at[slab], send_sem=send_sems.at[7 * w + r],
                recv_sem=recv_sems.at[7 * w + r], device_id=to, device_id_type=MESH)

        sends = [copy(w, r, src(w, dev), me, dev) for w in range(n) for r, dev in enumerate(peers)]
        for cp in sends:
            cp.start()
        for w in range(n):
            for r, dev in enumerate(peers):
                copy(w, r, src(w, dev), 4 * dev[0] + 2 * dev[1] + dev[2], dev).wait_recv()
        for cp in sends:
            cp.wait_send()
        for cp in local:
            cp.wait()

    return pl.pallas_call(
        body, name="exchange_per_chip" if per_chip else "exchange_all",
        in_specs=[_HBM] * n, out_specs=[_HBM] * n,
        out_shape=[jax.ShapeDtypeStruct((8, half[w], p.shape[2]) if per_chip else (8,) + p.shape, p.dtype)
                   for w, p in enumerate(parts)],
        scratch_shapes=[pltpu.SemaphoreType.DMA((7 * n,)), pltpu.SemaphoreType.DMA((7 * n,)),
                        pltpu.SemaphoreType.DMA((n,))],
    )(*parts)


def _sibling_swap(halves):
    n = len(halves)

    def body(*refs):
        ins, outs = refs[:n], refs[n:2 * n]
        send_sems, recv_sems, loc_sems = refs[2 * n:]
        x, y, c = _coords()

        def rows(w, core):
            rh = halves[w].shape[0]
            return outs[w].at[pl.ds(pl.multiple_of(core * rh, 8), rh), :]

        def copy(w, core):
            return pltpu.make_async_remote_copy(
                src_ref=ins[w], dst_ref=rows(w, core), send_sem=send_sems.at[w], recv_sem=recv_sems.at[w],
                device_id=(x, y, 1 - c), device_id_type=MESH)

        local = [pltpu.make_async_copy(ins[w], rows(w, c), loc_sems.at[w]) for w in range(n)]
        sends = [copy(w, c) for w in range(n)]
        for cp in local + sends:
            cp.start()
        for w in range(n):
            copy(w, 1 - c).wait_recv()
        for cp in sends:
            cp.wait_send()
        for cp in local:
            cp.wait()

    vmem = pl.BlockSpec(memory_space=pltpu.VMEM)
    return pl.pallas_call(
        body, name="sibling_swap", in_specs=[vmem] * n, out_specs=[vmem] * n,
        out_shape=[jax.ShapeDtypeStruct((2 * h.shape[0], h.shape[1]), h.dtype) for h in halves],
        scratch_shapes=[pltpu.SemaphoreType.DMA((n,)), pltpu.SemaphoreType.DMA((n,)), pltpu.SemaphoreType.DMA((n,))],
    )(*halves)


def _adamw(w, g, m, v):
    m = ADAM_B1 * m + (1.0 - ADAM_B1) * g
    v = ADAM_B2 * v + (1.0 - ADAM_B2) * (g * g)
    m_hat = m / (1.0 - ADAM_B1 ** ADAM_STEP)
    v_hat = v / (1.0 - ADAM_B2 ** ADAM_STEP)
    delta = -ADAM_LR * (m_hat / (jnp.sqrt(v_hat) + ADAM_EPS) + ADAM_WD * w)
    return delta, m, v


def _sum_parts(parts, name, tr):
    _, R, C = parts.shape
    assert R % tr == 0

    def body(p_ref, g_ref):
        g = p_ref[0].astype(F32)
        for d in range(1, 8):
            g = g + p_ref[d].astype(F32)
        g_ref[...] = g

    return pl.pallas_call(
        body, name=name, grid=(R // tr,),
        in_specs=[pl.BlockSpec((8, tr, C), lambda i: (0, i, 0))],
        out_specs=pl.BlockSpec((tr, C), lambda i: (i, 0)), out_shape=jax.ShapeDtypeStruct((R, C), F32),
    )(parts)


def _adamw_call(g, w, m, v, name, tr):
    R, C = w.shape
    assert R % tr == 0

    def body(g_ref, w_ref, m_ref, v_ref, d_ref, nm_ref, nv_ref):
        d_ref[...], nm_ref[...], nv_ref[...] = _adamw(w_ref[...], g_ref[...], m_ref[...], v_ref[...])

    tile = pl.BlockSpec((tr, C), lambda i: (i, 0))
    return pl.pallas_call(
        body, name=name, grid=(R // tr,), in_specs=[tile] * 4,
        out_specs=[tile] * 3, out_shape=[jax.ShapeDtypeStruct((R, C), F32)] * 3,
    )(g, w, m, v)


def _sum_adamw_small(parts, w, m, v):
    def body(p_ref, w_ref, m_ref, v_ref, g_ref, d_ref, nm_ref, nv_ref, loss_ref):
        g = p_ref[0]
        for d in range(1, 8):
            g = g + p_ref[d]
        g_ref[...] = g
        d_ref[...], nm_ref[...], nv_ref[...] = _adamw(w_ref[...], g, m_ref[...], v_ref[...])
        row = lax.broadcasted_iota(jnp.int32, g.shape, 0)
        per_row = jnp.sum(jnp.where(row == 6, g, 0.0), axis=1, keepdims=True)
        loss_ref[...] = jnp.zeros((8, LANES), F32) + jnp.sum(per_row, axis=0, keepdims=True)

    return pl.pallas_call(
        body, name="sum_adamw_small",
        out_shape=[jax.ShapeDtypeStruct((8, D_MODEL), F32)] * 4 + [jax.ShapeDtypeStruct((8, LANES), F32)],
    )(parts, w, m, v)


def _pack_small(ln1_g, ln1_b, ln2_g, ln2_b, g_sb, g_fox, b_f):
    row5 = jnp.pad(b_f.reshape(1, N_FOX), ((0, 0), (0, D_MODEL - N_FOX)))
    rows = [ln1_g.reshape(1, -1), ln1_b.reshape(1, -1), ln2_g.reshape(1, -1), ln2_b.reshape(1, -1),
            jnp.concatenate([g_sb.reshape(1, -1), g_fox.reshape(1, -1)], axis=1), row5,
            jnp.zeros((2, D_MODEL), F32)]
    return jnp.concatenate(rows, axis=0)


def _unpack_small(p):
    return {"ln1_g": p[0:1], "ln1_b": p[1:2], "ln2_g": p[2:3], "ln2_b": p[3:4], "g_sb": p[4:5, :GROUP_W],
            "g_fox": p[4:5, GROUP_W:], "b_f": p[5:6, :N_FOX]}


def kernel(x, w_in, b_f, g_sb, g_fox, w_out, ln1_g, ln1_b, ln2_g, ln2_b, w_gate_up, w_down, loss_target, m_w_in, m_b_f, m_g_sb, m_g_fox, m_w_out, m_ln1_g, m_ln1_b, m_ln2_g, m_ln2_b, m_w_gate_up, m_w_down, v_w_in, v_b_f, v_g_sb, v_g_fox, v_w_out, v_ln1_g, v_ln1_b, v_ln2_g, v_ln2_b, v_w_gate_up, v_w_down):
    S = x.shape[1]
    x2 = x.reshape(S, D_MODEL)
    tgt = loss_target.reshape(S, D_MODEL)
    TM = 1024
    TR = 512
    BQ = ATTN_BLOCK
    in_w = w_in.shape[2]
    gu_w = w_gate_up.shape[2]

    shards = [w_in[0].astype(BF16), w_out[0].astype(BF16), w_gate_up[0].astype(BF16), w_down[0].astype(BF16)]
    (wi_s,) = _allgather_chips(shards[:1])
    wi = wi_s.transpose(1, 0, 2).reshape(D_MODEL, 4 * in_w)
    w_sb, w_fx = wi[:, :QKV_W // 2], wi[:, QKV_W // 2:QKV_W]
    wqkv = wi[:, :QKV_W]
    wft = wi[:, QKV_W:].T
    proj, wo_s, wgu_s, wd_s = _proj_gather(x2, wqkv, shards[1:], TM, 512)
    wo = wo_s.reshape(D_MODEL, D_MODEL)
    wgu = wgu_s.transpose(1, 0, 2).reshape(D_MODEL, 2 * D_FF)
    wg, wu = wgu[:, :D_FF], wgu[:, D_FF:]
    wd = wd_s.reshape(D_FF, D_MODEL)
    g_row = jnp.concatenate([g_sb, g_fox], axis=1)
    hid = np.arange(D_MODEL) // HEAD_DIM
    he_np = (hid[:, None] == np.arange(LANES)[None, :]).astype(np.float32)
    he, het = jnp.asarray(he_np, BF16), jnp.asarray(he_np.T, BF16)

    lf = _fgate_fwd(x2, wft, b_f.reshape(N_FOX, 1), TM)
    c = _cumsum_fwd(lf)
    c_pair = c.reshape(N_PAIRS, 2, S)
    c_row = jnp.pad(c_pair, ((0, 0), (0, 6), (0, 0)))
    c_col = jnp.pad(c_pair.transpose(0, 2, 1), ((0, 0), (0, 0), (0, 6)))

    o_sb, st_sb, jmin_sb = _sb_fwd(proj, 0, BQ)
    jstart_fx = _fox_start_blocks(proj, 12, c, BQ, BQ)
    o_fx, st_fx = _fox_fwd(proj, 12, c_col, c_row, jstart_fx, BQ, BQ)

    def attn_post(i, osb_ref, ofx_ref, g_ref, he_ref, het_ref, on_ref):
        o = jnp.concatenate([osb_ref[...], ofx_ref[...]], axis=1)
        ms = _head_sums(o * o, he_ref[...], het_ref[...]) * (1.0 / HEAD_DIM)
        on_ref[...] = (o * lax.rsqrt(ms + RMS_EPS) * g_ref[...]).astype(BF16)

    (on,) = _rowwise(attn_post, "attn_post", S, TR,
                     [(o_sb, "t"), (o_fx, "t"), (g_row, "f"), (he, "f"), (het, "f")],
                     [((S, D_MODEL), BF16, "t")])

    u1 = _matmul(on, wo, mode="nn", name="mix", tm=TM, tn=D_MODEL, tk=D_MODEL, outs=[F32],
                 extras=[(x2, (TM if S >= TM else S, D_MODEL), _tile_ij)],
                 epilogue=lambda acc, xv: (ALPHA * xv + acc,))

    def ln1_fwd(i, u_ref, g_ref, b_ref, h_ref):
        xh, _ = _ln_stats(u_ref[...])
        h_ref[...] = xh * g_ref[...] + b_ref[...]

    (h1,) = _rowwise(ln1_fwd, "ln1_fwd", S, TR, [(u1, "t"), (ln1_g, "f"), (ln1_b, "f")], [((S, D_MODEL), F32, "t")])

    tm_e = TM if S >= TM else S
    n_ff = D_FF // 256

    def gate_up_body(h_ref, wg_ref, wu_ref, g_ref, u_ref, a_ref):
        h = h_ref[...].astype(BF16)
        g, u = _dot(h, wg_ref[...]), _dot(h, wu_ref[...])
        g_ref[...] = g.astype(BF16)
        u_ref[...] = u.astype(BF16)
        a_ref[...] = (g / (1.0 + jnp.exp(-g)) * u).astype(BF16)

    ff_tile = pl.BlockSpec((tm_e, 256), lambda i, j: (i, j))
    gate, up, act = pl.pallas_call(
        gate_up_body, name="gate_up_act", grid=(S // tm_e, n_ff),
        in_specs=[pl.BlockSpec((tm_e, D_MODEL), lambda i, j: (i, 0)),
                  pl.BlockSpec((D_MODEL, 256), lambda i, j: (0, j)),
                  pl.BlockSpec((D_MODEL, 256), lambda i, j: (0, j + n_ff))],
        out_specs=[ff_tile] * 3, out_shape=[jax.ShapeDtypeStruct((S, D_FF), BF16)] * 3)(h1, wgu, wgu)

    u2 = _matmul(act, wd, mode="nn", name="ffn_down", tm=TM, tn=D_MODEL, tk=D_FF, outs=[F32],
                 extras=[(h1, (TM if S >= TM else S, D_MODEL), _tile_ij)],
                 epilogue=lambda acc, hv: (ALPHA * hv + acc,))

    def ln2_loss(i, u_ref, t_ref, g_ref, b_ref, du_ref, acc_ref):
        xh, r = _ln_stats(u_ref[...])
        g = g_ref[...]
        err = xh * g + b_ref[...] - t_ref[...]
        dy = err * (1.0 / D_MODEL)
        du_ref[...] = _ln_bwd(dy, xh, r, g)
        _acc_rows(i, acc_ref, {2: jnp.sum(dy * xh, axis=0, keepdims=True), 3: jnp.sum(dy, axis=0, keepdims=True),
                               6: jnp.sum(err * err, axis=0, keepdims=True) * (0.5 / D_MODEL)})

    du2, acc_ln2 = _rowwise(ln2_loss, "ln2_loss", S, TR, [(u2, "t"), (tgt, "t"), (ln2_g, "f"), (ln2_b, "f")],
                            [((S, D_MODEL), F32, "t"), ((8, D_MODEL), F32, "f")])

    d_wd = _matmul(act, du2, mode="tn", name="dw_down", tm=1408, tn=D_MODEL, tk=TM, outs=[BF16])

    def dgu_epilogue(da, g, u):
        g, u = g.astype(F32), u.astype(F32)
        s = 1.0 / (1.0 + jnp.exp(-g))
        return da * u * (s * (1.0 + g * (1.0 - s))), da * (g * s)

    dgate, dup = _matmul(du2, wd, mode="nt", name="d_act", tm=TM, tn=1408, tk=D_MODEL, outs=[BF16, BF16],
                         extras=[(gate, (tm_e, 1408), _tile_ij), (up, (tm_e, 1408), _tile_ij)],
                         epilogue=dgu_epilogue)
    d_wg = _matmul(h1, dgate, mode="tn", name="dw_gate", tm=D_MODEL, tn=1408, tk=TM, outs=[BF16])
    d_wu = _matmul(h1, dup, mode="tn", name="dw_up", tm=D_MODEL, tn=1408, tk=TM, outs=[BF16])
    dh1 = _matmul(dgate, wg, mode="nt", name="dh1_gate", tm=TM, tn=D_MODEL, tk=D_FF, outs=[F32],
                  extras=[(du2, (tm_e, D_MODEL), _tile_ij)], epilogue=lambda acc, e: (ALPHA * e + acc,))
    dh1 = _matmul(dup, wu, mode="nt", name="dh1_up", tm=TM, tn=D_MODEL, tk=D_FF, outs=[F32],
                  extras=[(dh1, (tm_e, D_MODEL), _tile_ij)], epilogue=lambda acc, e: (e + acc,))

    def ln1_bwd(i, dh_ref, u_ref, g_ref, du_ref, acc_ref):
        xh, r = _ln_stats(u_ref[...])
        dh = dh_ref[...]
        du_ref[...] = _ln_bwd(dh, xh, r, g_ref[...])
        _acc_rows(i, acc_ref, {0: jnp.sum(dh * xh, axis=0, keepdims=True), 1: jnp.sum(dh, axis=0, keepdims=True)})

    du1, acc_ln1 = _rowwise(ln1_bwd, "ln1_bwd", S, TR, [(dh1, "t"), (u1, "t"), (ln1_g, "f")],
                            [((S, D_MODEL), F32, "t"), ((8, D_MODEL), F32, "f")])
    d_wo = _matmul(on, du1, mode="tn", name="dw_out", tm=D_MODEL, tn=D_MODEL, tk=TM, outs=[BF16])
    don = _matmul(du1, wo, mode="nt", name="d_on", tm=TM, tn=D_MODEL, tk=D_MODEL, outs=[F32])

    def rms_bwd(i, don_ref, osb_ref, ofx_ref, g_ref, he_ref, het_ref, dosb_ref, dofx_ref, acc_ref):
        o = jnp.concatenate([osb_ref[...], ofx_ref[...]], axis=1)
        hev, hetv = he_ref[...], het_ref[...]
        r = lax.rsqrt(_head_sums(o * o, hev, hetv) * (1.0 / HEAD_DIM) + RMS_EPS)
        dn = don_ref[...]
        dg = dn * g_ref[...]
        do = r * dg - o * (r * r * r) * (_head_sums(dg * o, hev, hetv) * (1.0 / HEAD_DIM))
        dosb_ref[...] = do[:, :GROUP_W]
        dofx_ref[...] = do[:, GROUP_W:]
        _acc_rows(i, acc_ref, {4: jnp.sum(dn * o * r, axis=0, keepdims=True)})

    do_sb, do_fx, acc_rms = _rowwise(
        rms_bwd, "rms_bwd", S, TR, [(don, "t"), (o_sb, "t"), (o_fx, "t"), (g_row, "f"), (he, "f"), (het, "f")],
        [((S, GROUP_W), F32, "t"), ((S, GROUP_W), F32, "t"), ((8, D_MODEL), F32, "f")])

    dq_sb, dk_sb, dv_sb = _sb_bwd(proj, 0, do_sb, st_sb, jmin_sb, BQ)
    jstart_fx2 = jnp.minimum(jstart_fx[:, 0::2], jstart_fx[:, 1::2])
    dq_fx, dk_fx, dv_fx, dc, dcq = _fox_bwd(proj, 12, do_fx, o_fx, st_fx, c_col, c_row, jstart_fx2, 2 * BQ, BQ)
    dc = dc[:, :2, :] + dcq[:, :, :2].transpose(0, 2, 1)
    dfl, dbf = _fgate_bwd(dc.reshape(N_FOX, S), lf)
    dp_sb = jnp.concatenate([dq_sb, dk_sb, dv_sb], axis=1).astype(BF16)
    dp_fx = jnp.concatenate([dq_fx, dk_fx, dv_fx], axis=1).astype(BF16)

    d_wsb = _matmul(x2, dp_sb, mode="tn", name="dw_in_sb", tm=D_MODEL, tn=QKV_W // 2, tk=TM, outs=[BF16])
    d_wfx = _matmul(x2, dp_fx, mode="tn", name="dw_in_fx", tm=D_MODEL, tn=QKV_W // 2, tk=TM, outs=[BF16])
    d_wft = _matmul(dfl, x2, mode="nn", name="dw_in_f", tm=N_FOX, tn=D_MODEL, tk=TM, outs=[BF16])
    dx = _matmul(dp_sb, w_sb, mode="nt", name="dx_sb", tm=TM, tn=D_MODEL, tk=QKV_W // 2, outs=[F32],
                 extras=[(du1, (tm_e, D_MODEL), _tile_ij)], epilogue=lambda acc, e: (ALPHA * e + acc,))
    dx = _matmul(dp_fx, w_fx, mode="nt", name="dx_fx", tm=TM, tn=D_MODEL, tk=QKV_W // 2, outs=[F32],
                 extras=[(dx, (tm_e, D_MODEL), _tile_ij)], epilogue=lambda acc, e: (e + acc,))
    dx = _matmul(dfl, wft, mode="tn", name="dx_f", tm=TM, tn=D_MODEL, tk=N_FOX, outs=[F32],
                 extras=[(dx, (tm_e, D_MODEL), _tile_ij)], epilogue=lambda acc, e: (e + acc,))

    d_wi = jnp.concatenate([d_wsb, d_wfx, d_wft.T], axis=1)
    d_wgu = jnp.concatenate([d_wg, d_wu], axis=1)
    parts = [d_wi.reshape(D_MODEL, 4, in_w).transpose(1, 0, 2).astype(BF16),
             d_wo.reshape(4, D_MODEL // 4, D_MODEL).astype(BF16),
             d_wgu.reshape(D_MODEL, 4, gu_w).transpose(1, 0, 2).astype(BF16),
             d_wd.reshape(4, D_FF // 4, D_MODEL).astype(BF16)]
    got = _exchange(parts, True)
    big_names = ("w_in", "w_out", "w_gate_up", "w_down")
    halves = [_sum_parts(p, "sum_" + nm, tr) for nm, p, tr in zip(big_names, got, (256, 128, 128, 176))]
    grads = _sibling_swap(halves)
    big = {}
    for nm, g, w, m, v, tr in zip(big_names, grads, (w_in, w_out, w_gate_up, w_down),
                                  (m_w_in, m_w_out, m_w_gate_up, m_w_down),
                                  (v_w_in, v_w_out, v_w_gate_up, v_w_down), (256, 256, 256, 176)):
        big[nm] = [r[None] for r in [g] + list(_adamw_call(g, w[0], m[0], v[0], "adamw_" + nm, tr))]

    small = acc_ln2 + acc_ln1 + acc_rms
    small = small + jnp.pad(dbf.reshape(1, N_FOX), ((5, 2), (0, D_MODEL - N_FOX)))
    (small_all,) = _exchange([small], False)
    sw = _pack_small(ln1_g, ln1_b, ln2_g, ln2_b, g_sb, g_fox, b_f)
    sm = _pack_small(m_ln1_g, m_ln1_b, m_ln2_g, m_ln2_b, m_g_sb, m_g_fox, m_b_f)
    sv = _pack_small(v_ln1_g, v_ln1_b, v_ln2_g, v_ln2_b, v_g_sb, v_g_fox, v_b_f)
    sg, sd, snm, snv, loss_blk = _sum_adamw_small(small_all, sw, sm, sv)
    sg, sd, snm, snv = _unpack_small(sg), _unpack_small(sd), _unpack_small(snm), _unpack_small(snv)

    names = ["w_in", "b_f", "g_sb", "g_fox", "w_out", "ln1_g", "ln1_b", "ln2_g", "ln2_b", "w_gate_up", "w_down"]
    outs = [loss_blk[0, 0], dx.reshape(1, S, D_MODEL)]
    for k, table in enumerate((sg, sd, snm, snv)):
        outs += [big[n][k] if n in big else table[n] for n in names]
    return tuple(outs)
```

```python
import functools

import numpy as np
import jax
import jax.numpy as jnp
from jax import lax
from jax.experimental import pallas as pl
from jax.experimental.pallas import tpu as pltpu

F32 = jnp.float32
BF16 = jnp.bfloat16

D_MODEL = 1024
HEAD_DIM = 64
LANES = 128
N_PAIRS = 4
GROUP_W = 512
QKV_W = 3072
D_FF = 2816
N_FOX = 8
ALPHA = 2.0 ** 0.25
LN_EPS = 1e-5
RMS_EPS = 1e-6
SCALE = HEAD_DIM ** -0.5
NEG_BIG = -1e30
FOX_SKIP = 30.0
SB_STOP = -105.0
ADAM_LR, ADAM_B1, ADAM_B2, ADAM_EPS, ADAM_WD, ADAM_STEP = 0.001, 0.9, 0.999, 1e-08, 0.01, 10
KV_SLOTS = 4
SCAN_GROUP = 8
ATTN_BLOCK = 256
VMEM_BIG = 56 * 1024 * 1024
MESH = pl.DeviceIdType.MESH

_NN = (((1,), (0,)), ((), ()))
_NT = (((1,), (1,)), ((), ()))
_TN = (((0,), (0,)), ((), ()))


def _dot(a, b, dims=_NN):
    return lax.dot_general(a, b, dims, preferred_element_type=F32)


def _split_dot(x, t):
    hi = x.astype(BF16)
    lo = (x - hi.astype(F32)).astype(BF16)
    return _dot(hi, t) + _dot(lo, t)


def _softplus(z):
    return jnp.maximum(z, 0.0) + jnp.log1p(jnp.exp(-jnp.abs(z)))


def _col(v, h):
    lane = lax.broadcasted_iota(jnp.int32, v.shape, 1)
    return jnp.sum(jnp.where(lane == h, v, 0.0), axis=1, keepdims=True)


def _two_sum(hi, lo, b):
    s = hi + b
    bb = s - hi
    err = (hi - (s - bb)) + (b - bb)
    return s, lo + err


def _params(vmem=None):
    return pltpu.CompilerParams(vmem_limit_bytes=vmem) if vmem else None


def _matmul(a, b, *, mode, name, tm, tn, tk, outs, extras=(), epilogue=None, vmem=None):
    if mode == "nn":
        (M, K), (_, N) = a.shape, b.shape
    elif mode == "nt":
        (M, K), (N, _) = a.shape, b.shape
    else:
        (K, M), (_, N) = a.shape, b.shape
    tm, tn, tk = min(tm, M), min(tn, N), min(tk, K)
    assert M % tm == 0 and N % tn == 0 and K % tk == 0, (name, M, N, K, tm, tn, tk)
    nk = K // tk
    dims = {"nn": _NN, "nt": _NT, "tn": _TN}[mode]
    if mode == "tn":
        a_spec = pl.BlockSpec((tk, tm), lambda i, j, k: (k, i))
    else:
        a_spec = pl.BlockSpec((tm, tk), lambda i, j, k: (i, k))
    if mode == "nt":
        b_spec = pl.BlockSpec((tn, tk), lambda i, j, k: (j, k))
    else:
        b_spec = pl.BlockSpec((tk, tn), lambda i, j, k: (k, j))
    ex_specs = [pl.BlockSpec(bs, (lambda i, j, k, f=f: f(i, j))) for (_, bs, f) in extras]
    ne, no = len(extras), len(outs)
    if epilogue is None:
        epilogue = lambda acc: (acc,)

    def body(a_ref, b_ref, *rest):
        ex_refs, out_refs, acc = rest[:ne], rest[ne:ne + no], rest[-1]
        k = pl.program_id(2)

        @pl.when(k == 0)
        def _():
            acc[...] = jnp.zeros_like(acc)

        acc[...] += _dot(a_ref[...].astype(BF16), b_ref[...].astype(BF16), dims)

        @pl.when(k == nk - 1)
        def _():
            res = epilogue(acc[...], *[e[...] for e in ex_refs])
            for r, o in zip(res, out_refs):
                o[...] = r.astype(o.dtype)

    res = pl.pallas_call(
        body, name=name, grid=(M // tm, N // tn, nk),
        in_specs=[a_spec, b_spec] + ex_specs,
        out_specs=[pl.BlockSpec((tm, tn), lambda i, j, k: (i, j)) for _ in outs],
        out_shape=[jax.ShapeDtypeStruct((M, N), d) for d in outs],
        scratch_shapes=[pltpu.VMEM((tm, tn), F32)],
        compiler_params=_params(vmem),
    )(a, b, *[e[0] for e in extras])
    return res[0] if no == 1 else res


def _tile_ij(i, j):
    return (i, j)


def _rowwise(fn, name, rows, tm, ins, outs, vmem=None):
    tm = min(tm, rows)
    assert rows % tm == 0

    def spec(shape, kind):
        if kind == "t":
            return pl.BlockSpec((tm,) + tuple(shape[1:]), lambda i: (i,) + (0,) * (len(shape) - 1))
        return pl.BlockSpec(tuple(shape), lambda i: (0,) * len(shape))

    def body(*refs):
        fn(pl.program_id(0), *refs)

    return pl.pallas_call(
        body, name=name, grid=(rows // tm,),
        in_specs=[spec(a.shape, k) for a, k in ins],
        out_specs=[spec(s, k) for s, _, k in outs],
        out_shape=[jax.ShapeDtypeStruct(s, d) for s, d, _ in outs],
        compiler_params=_params(vmem),
    )(*[a for a, _ in ins])


def _ln_stats(u):
    mu = jnp.mean(u, axis=-1, keepdims=True)
    d = u - mu
    var = jnp.mean(d * d, axis=-1, keepdims=True)
    r = lax.rsqrt(var + LN_EPS)
    return d * r, r


def _ln_bwd(dh, xh, r, g):
    dxh = dh * g
    m1 = jnp.mean(dxh, axis=-1, keepdims=True)
    m2 = jnp.mean(dxh * xh, axis=-1, keepdims=True)
    return r * (dxh - m1 - xh * m2)


def _acc_rows(i, ref, rows):
    @pl.when(i == 0)
    def _():
        ref[...] = jnp.zeros_like(ref)
    for r, v in rows.items():
        ref[pl.ds(r, 1), :] += v


def _head_sums(v, he, het):
    return _split_dot(_split_dot(v, he), het)


def _fgate_fwd(x, wft, bf_col, tm):
    S = x.shape[0]
    tm = min(tm, S)

    def body(wft_ref, bf_ref, x_ref, lf_ref):
        f = _dot(wft_ref[...], x_ref[...].astype(BF16), _NT) + bf_ref[...]
        lf_ref[...] = -_softplus(-f)

    return pl.pallas_call(
        body, name="fgate_fwd", grid=(S // tm,),
        in_specs=[pl.BlockSpec((N_FOX, D_MODEL), lambda i: (0, 0)), pl.BlockSpec((N_FOX, 1), lambda i: (0, 0)),
                  pl.BlockSpec((tm, D_MODEL), lambda i: (i, 0))],
        out_specs=pl.BlockSpec((N_FOX, tm), lambda i: (0, i)),
        out_shape=jax.ShapeDtypeStruct((N_FOX, S), F32),
    )(wft, bf_col, x)


def _chunk_scan(v, reverse):
    lane = lax.broadcasted_iota(jnp.int32, v.shape, 1)
    sh = 1
    while sh < LANES:
        if reverse:
            v = v + jnp.where(lane < LANES - sh, pltpu.roll(v, LANES - sh, 1), 0.0)
        else:
            v = v + jnp.where(lane >= sh, pltpu.roll(v, sh, 1), 0.0)
        sh *= 2
    return v


def _cumsum_fwd(lf):
    n, S = lf.shape
    nc = S // LANES

    grp = min(SCAN_GROUP, nc)

    def body(lf_ref, c_ref):
        def step(gi, carry):
            sls = [pl.ds(pl.multiple_of((gi * grp + g) * LANES, LANES), LANES) for g in range(grp)]
            vs = [_chunk_scan(lf_ref[:, sl], False) for sl in sls]
            tots = [_col(v, LANES - 1) for v in vs]
            for sl, v, t in zip(sls, vs, tots):
                c_ref[:, sl] = v + carry
                carry = carry + t
            return carry
        lax.fori_loop(0, nc // grp, step, jnp.zeros((n, 1), F32))

    return pl.pallas_call(body, name="cumsum_fwd", out_shape=jax.ShapeDtypeStruct((n, S), F32))(lf)


def _fgate_bwd(dc, lf):
    n, S = dc.shape
    nc = S // LANES

    grp = min(SCAN_GROUP, nc)

    def body(dc_ref, lf_ref, dfl_ref, dbf_ref):
        def step(t, carry):
            car, tot = carry
            gi = nc // grp - 1 - t
            sls = [pl.ds(pl.multiple_of((gi * grp + g) * LANES, LANES), LANES) for g in range(grp)]
            vs = [_chunk_scan(dc_ref[:, sl], True) for sl in sls]
            firsts = [_col(v, 0) for v in vs]
            for sl, v, f in reversed(list(zip(sls, vs, firsts))):
                dfl = (v + car) * (1.0 - jnp.exp(lf_ref[:, sl]))
                dfl_ref[:, sl] = dfl
                tot = tot + jnp.sum(dfl, axis=1, keepdims=True)
                car = car + f
            return car, tot
        _, tot = lax.fori_loop(0, nc // grp, step, (jnp.zeros((n, 1), F32), jnp.zeros((n, 1), F32)))
        dbf_ref[...] = tot

    return pl.pallas_call(body, name="fgate_bwd",
                          out_shape=[jax.ShapeDtypeStruct((n, S), F32), jax.ShapeDtypeStruct((n, 1), F32)])(dc, lf)


def _tri_matrices(b):
    r = np.arange(b)
    tfwd = (r[:, None] <= r[None, :]).astype(np.float32)
    return jnp.asarray(tfwd, BF16), jnp.asarray(tfwd.T, BF16)


def _kv_copies(kv_hbm, kbuf, vbuf, sems, sem0, pair_col, bq, j, slot):
    rows = pl.ds(pl.multiple_of(j * bq, bq), bq)

    def cols(c):
        return pl.ds(pl.multiple_of((pair_col + c) * LANES, LANES), LANES)

    return (pltpu.make_async_copy(kv_hbm.at[rows, cols(4)], kbuf.at[slot], sems.at[0, sem0 + slot]),
            pltpu.make_async_copy(kv_hbm.at[rows, cols(8)], vbuf.at[slot], sems.at[1, sem0 + slot]))


def _first_two_up(first_block, per=1):
    def blocks(pair, blk):
        first = first_block(pair, blk)
        return first, first + 1, first + 1 <= per * blk + per - 1
    return blocks


def _first_two_down(pair, blk):
    return blk, blk - 1, blk > 0


def _start_two(fetch, pair, first, second, has_second, ahead):
    for cp in fetch(first, 0, pair, ahead):
        cp.start()

    @pl.when(has_second)
    def _():
        for cp in fetch(second, 1, pair, ahead):
            cp.start()


def _kv_fetcher(kv_hbm, kbuf, vbuf, sems, ns, col0, bq, p, i, nq, blocks):
    base = lax.rem(p * nq + i, 2) * ns
    own = (kbuf.at[pl.ds(base, ns)], vbuf.at[pl.ds(base, ns)])
    other = (kbuf.at[pl.ds(ns - base, ns)], vbuf.at[pl.ds(ns - base, ns)])

    def fetch(j, slot, pair=p, ahead=False):
        kb, vb = other if ahead else own
        return _kv_copies(kv_hbm, kb, vb, sems, ns - base if ahead else base, col0 + pair, bq, j, slot)

    pl.when(jnp.logical_and(p == 0, i == 0))(lambda: _start_two(fetch, p, *blocks(p, i), False))
    wrap = i == nq - 1

    @pl.when(jnp.logical_not(jnp.logical_and(wrap, p == N_PAIRS - 1)))
    def _():
        pair, blk = jnp.where(wrap, p + 1, p), jnp.where(wrap, 0, i + 1)
        _start_two(fetch, pair, *blocks(pair, blk), True)

    return fetch, own[0], own[1]


def _masked_pair(v, lane_is_a, scale=1.0):
    v = v.astype(F32) * scale
    return jnp.where(lane_is_a, v, 0.0).astype(BF16), jnp.where(lane_is_a, 0.0, v).astype(BF16)


def _sb_fwd(proj, col0, bq):
    S = proj.shape[0]
    bq = min(bq, S)
    nq = S // bq
    _, trev = _tri_matrices(bq)

    def body(q_ref, kv_hbm, trev_ref, o_ref, st_ref, jmin_ref, acc_a, acc_b, qa, qb, rs, kbuf, vbuf, sems):
        p, i = pl.program_id(0), pl.program_id(1)
        fetch, kbuf, vbuf = _kv_fetcher(kv_hbm, kbuf, vbuf, sems, 2, col0, bq, p, i, nq, _first_two_down)
        is_a = lax.broadcasted_iota(jnp.int32, (bq, LANES), 1) < HEAD_DIM
        acc_a[...] = jnp.zeros_like(acc_a)
        acc_b[...] = jnp.zeros_like(acc_b)
        rs[...] = jnp.zeros_like(rs)
        qa[...], qb[...] = _masked_pair(q_ref[...], is_a, SCALE)

        def tiles(blocks):
            hs, qs, accs, trev_m = (0, 1), (qa, qb), (acc_a, acc_b), trev_ref[...]
            kv = [(kbuf[s], vbuf[s]) for s, _ in blocks]
            bh = [(b, h) for b in range(len(blocks)) for h in hs]
            tri = lax.broadcasted_iota(jnp.int32, (bq, bq), 0) > lax.broadcasted_iota(jnp.int32, (bq, bq), 1)
            z = {(b, h): _dot(qs[h][...], kv[b][0], _NT) for b, h in bh}
            lk = {(b, h): -_softplus(z[b, h]) for b, h in bh}
            lk = {(b, h): jnp.where(tri, lk[b, h], 0.0) if blocks[b][1] else lk[b, h] for b, h in bh}
            suf = {(b, h): _split_dot(lk[b, h], trev_m) for b, h in bh}
            tot = {(b, h): jnp.sum(lk[b, h], axis=1, keepdims=True) for b, h in bh}
            right = {}
            for h in hs:
                r = rs[2 * h] + rs[2 * h + 1]
                for b in range(len(blocks)):
                    right[b, h] = r
                    r = r + tot[b, h]
            w = {(b, h): jnp.exp(z[b, h] + suf[b, h] + right[b, h]) for b, h in bh}
            w = {(b, h): jnp.where(tri, w[b, h], 0.0) if blocks[b][1] else w[b, h] for b, h in bh}
            pv = {(b, h): _dot(w[b, h].astype(BF16), kv[b][1]) for b, h in bh}
            for h in hs:
                accs[h][...] += sum([pv[b, h] for b in range(1, len(blocks))], pv[0, h])
                hi, lo = rs[2 * h], rs[2 * h + 1]
                for b in range(len(blocks)):
                    hi, lo = _two_sum(hi, lo, tot[b, h])
                rs[2 * h], rs[2 * h + 1] = hi, lo

        def live():
            return (jnp.max(jnp.maximum(rs[0], rs[2])) > SB_STOP).astype(jnp.int32)

        for cp in fetch(i, 0):
            cp.wait()
        pl.when(i == 0)(functools.partial(tiles, [(0, True)]))

        @pl.when(i > 0)
        def _():
            for cp in fetch(i - 1, 1):
                cp.wait()
            tiles([(0, True), (1, False)])

        def step(carry):
            j, _ = carry
            slot = lax.rem(i - j, 2)
            for cp in fetch(j, slot):
                cp.start()
            for cp in fetch(j, slot):
                cp.wait()
            tiles([(slot, False)])
            return j - 1, live()

        j_end, _ = lax.while_loop(lambda c: jnp.logical_and(c[0] >= 0, c[1] > 0), step, (i - 2, live()))
        jmin_ref[p, i] = jnp.maximum(j_end + 1, 0)
        o_ref[...] = jnp.where(is_a, acc_a[...], acc_b[...])
        lane8 = lax.broadcasted_iota(jnp.int32, (bq, 8), 1)
        st = jnp.zeros((bq, 8), F32)
        for c, src in enumerate((0, 2, 1, 3)):
            st = jnp.where(lane8 == c, rs[src], st)
        st_ref[0] = st

    return pl.pallas_call(
        body, name="sb_fwd", grid=(N_PAIRS, nq),
        in_specs=[pl.BlockSpec((bq, LANES), lambda p, i: (i, col0 + p)),
                  pl.BlockSpec(memory_space=pl.ANY),
                  pl.BlockSpec((bq, bq), lambda p, i: (0, 0))],
        out_specs=[pl.BlockSpec((bq, LANES), lambda p, i: (i, p)),
                   pl.BlockSpec((1, bq, 8), lambda p, i: (p, i, 0)),
                   pl.BlockSpec(memory_space=pltpu.SMEM)],
        out_shape=[jax.ShapeDtypeStruct((S, GROUP_W), F32), jax.ShapeDtypeStruct((N_PAIRS, S, 8), F32),
                   jax.ShapeDtypeStruct((N_PAIRS, nq), jnp.int32)],
        scratch_shapes=[pltpu.VMEM((bq, LANES), F32), pltpu.VMEM((bq, LANES), F32),
                        pltpu.VMEM((bq, LANES), BF16), pltpu.VMEM((bq, LANES), BF16),
                        pltpu.VMEM((4, bq, 1), F32),
                        pltpu.VMEM((4, bq, LANES), BF16), pltpu.VMEM((4, bq, LANES), BF16),
                        pltpu.SemaphoreType.DMA((2, 4))],
    )(proj, proj, trev)


def _sb_bwd(proj, col0, do, st, jmin, bq):
    S = proj.shape[0]
    bq = min(bq, S)
    nq = S // bq
    tfwd, trev = _tri_matrices(bq)

    def body(jmin_ref, q_ref, kv_hbm, do_ref, st_ref, tfwd_ref, trev_ref,
             dq_ref, dk_ref, dv_ref, dq_a, dq_b, qa, qb, doa, dob, rs, kbuf, vbuf, sems):
        p, i = pl.program_id(0), pl.program_id(1)
        j0 = jmin_ref[p, i]
        first_two = _first_two_up(lambda pair, blk: jmin_ref[pair, blk])
        fetch, kbuf, vbuf = _kv_fetcher(kv_hbm, kbuf, vbuf, sems, KV_SLOTS, col0, bq, p, i, nq, first_two)
        is_a = lax.broadcasted_iota(jnp.int32, (bq, LANES), 1) < HEAD_DIM

        @pl.when(i == 0)
        def _():
            dk_ref[...] = jnp.zeros_like(dk_ref)
            dv_ref[...] = jnp.zeros_like(dv_ref)

        dq_a[...] = jnp.zeros_like(dq_a)
        dq_b[...] = jnp.zeros_like(dq_b)
        rs[...] = jnp.zeros_like(rs)
        st_v = st_ref[0]
        for h in range(2):
            rs[6 + 2 * h], rs[7 + 2 * h] = _col(st_v, h), _col(st_v, 2 + h)
        qa[...], qb[...] = _masked_pair(q_ref[...], is_a, SCALE)
        doa[...], dob[...] = _masked_pair(do_ref[...], is_a)

        def tiles(blocks):
            hs, qs, dos, dqs = (0, 1), (qa, qb), (doa, dob), (dq_a, dq_b)
            tfwd_m, trev_m = tfwd_ref[...], trev_ref[...]
            kv = [(kbuf[s], vbuf[s]) for _, s, _ in blocks]
            nb = len(blocks)
            bh = [(b, h) for b in range(nb) for h in hs]
            tri = lax.broadcasted_iota(jnp.int32, (bq, bq), 0) > lax.broadcasted_iota(jnp.int32, (bq, bq), 1)

            def mask(x, b):
                return jnp.where(tri, x, 0.0) if blocks[b][2] else x

            z = {(b, h): _dot(qs[h][...], kv[b][0], _NT) for b, h in bh}
            dw = {(b, h): _dot(dos[h][...], kv[b][1], _NT) for b, h in bh}
            lk = {(b, h): mask(-_softplus(z[b, h]), b) for b, h in bh}
            suf = {(b, h): _split_dot(lk[b, h], trev_m) for b, h in bh}
            tot = {(b, h): jnp.sum(lk[b, h], axis=1, keepdims=True) for b, h in bh}
            pre = {}
            for h in hs:
                run = (rs[3 * h], rs[3 * h + 1])
                for b in range(nb):
                    run = _two_sum(run[0], run[1], tot[b, h])
                    pre[b, h] = run
            right = {(b, h): (rs[6 + 2 * h] - pre[b, h][0]) + (rs[7 + 2 * h] - pre[b, h][1]) for b, h in bh}
            w = {(b, h): mask(jnp.exp(z[b, h] + suf[b, h] + right[b, h]), b) for b, h in bh}
            g = {(b, h): dw[b, h] * w[b, h] for b, h in bh}
            gpre = {(b, h): _split_dot(g[b, h], tfwd_m) for b, h in bh}
            gtot = {(b, h): jnp.sum(g[b, h], axis=1, keepdims=True) for b, h in bh}
            gleft = {}
            for h in hs:
                run = rs[3 * h + 2]
                for b in range(nb):
                    gleft[b, h] = run
                    run = run + gtot[b, h]
                gleft[nb, h] = run
            dz = {(b, h): mask(g[b, h] - jnp.exp(z[b, h] + lk[b, h]) * (gpre[b, h] + gleft[b, h]), b) for b, h in bh}
            dzb = {(b, h): dz[b, h].astype(BF16) for b, h in bh}
            wb = {(b, h): w[b, h].astype(BF16) for b, h in bh}
            dqc = {(b, h): _dot(dzb[b, h], kv[b][0]) for b, h in bh}
            dkc = {(b, h): _dot(dzb[b, h], qs[h][...], _TN) for b, h in bh}
            dvc = {(b, h): _dot(wb[b, h], dos[h][...], _TN) for b, h in bh}
            for h in hs:
                rs[3 * h], rs[3 * h + 1] = pre[nb - 1, h]
                rs[3 * h + 2] = gleft[nb, h]
                dqs[h][...] += sum([dqc[b, h] for b in range(1, nb)], dqc[0, h])
            for b, (j, _, _) in enumerate(blocks):
                rows = pl.ds(pl.multiple_of(j * bq, bq), bq)
                dk_ref[rows, :] += dkc[b, 0] + dkc[b, 1]
                dv_ref[rows, :] += dvc[b, 0] + dvc[b, 1]

        def single(j, slot, masked):
            tiles([(j, slot, masked)])

        def wait(j):
            slot = lax.rem(j - j0, KV_SLOTS)
            for cp in fetch(j, slot):
                cp.wait()
            return slot

        _walk_up(fetch, j0, i, i, single, stop=jnp.maximum(i - 1, j0))

        @pl.when(j0 < i)
        def _():
            tiles([(i - 1, wait(i - 1), False), (i, wait(i), True)])

        @pl.when(j0 == i)
        def _():
            tiles([(i, wait(i), True)])

        dq_ref[...] = jnp.where(is_a, dq_a[...], dq_b[...]) * SCALE

    grid_spec = pltpu.PrefetchScalarGridSpec(
        num_scalar_prefetch=1, grid=(N_PAIRS, nq),
        in_specs=[pl.BlockSpec((bq, LANES), lambda p, i, jm: (i, col0 + p)),
                  pl.BlockSpec(memory_space=pl.ANY),
                  pl.BlockSpec((bq, LANES), lambda p, i, jm: (i, p)),
                  pl.BlockSpec((1, bq, 8), lambda p, i, jm: (p, i, 0)),
                  pl.BlockSpec((bq, bq), lambda p, i, jm: (0, 0)),
                  pl.BlockSpec((bq, bq), lambda p, i, jm: (0, 0))],
        out_specs=[pl.BlockSpec((bq, LANES), lambda p, i, jm: (i, p)),
                   pl.BlockSpec((S, LANES), lambda p, i, jm: (0, p)),
                   pl.BlockSpec((S, LANES), lambda p, i, jm: (0, p))],
        scratch_shapes=[pltpu.VMEM((bq, LANES), F32), pltpu.VMEM((bq, LANES), F32)]
        + [pltpu.VMEM((bq, LANES), BF16)] * 4 + [pltpu.VMEM((10, bq, 1), F32)]
        + [pltpu.VMEM((2 * KV_SLOTS, bq, LANES), BF16)] * 2 + [pltpu.SemaphoreType.DMA((2, 2 * KV_SLOTS))])
    return pl.pallas_call(
        body, name="sb_bwd", grid_spec=grid_spec,
        out_shape=[jax.ShapeDtypeStruct((S, GROUP_W), F32)] * 3,
        compiler_params=_params(VMEM_BIG),
    )(jmin, proj, proj, do, st, tfwd, trev)


def _walk_up(fetch, j0, diag, last, tile, stop=None):
    ahead = KV_SLOTS - 1
    stop = last + 1 if stop is None else stop

    def start(j):
        @pl.when(j <= last)
        def _():
            for cp in fetch(j, lax.rem(j - j0, KV_SLOTS)):
                cp.start()

    for d in range(2, ahead):
        start(j0 + d)

    def step(j, carry):
        slot = lax.rem(j - j0, KV_SLOTS)
        for cp in fetch(j, slot):
            cp.wait()
        start(j + ahead)
        pl.when(j >= diag)(functools.partial(tile, j, slot, True))
        pl.when(j < diag)(functools.partial(tile, j, slot, False))
        return carry

    lax.fori_loop(j0, stop, step, 0)


def _causal(bq, bk, i, j):
    row = lax.broadcasted_iota(jnp.int32, (bq, bk), 0)
    col = lax.broadcasted_iota(jnp.int32, (bq, bk), 1)
    return col - row <= i * bq - j * bk


def _by_heads(j, first_a, first_b, heads):
    on_a, on_b = j >= first_a, j >= first_b
    pl.when(jnp.logical_and(on_a, on_b))(functools.partial(heads, (0, 1)))
    pl.when(jnp.logical_and(on_a, jnp.logical_not(on_b)))(functools.partial(heads, (0,)))
    pl.when(jnp.logical_and(on_b, jnp.logical_not(on_a)))(functools.partial(heads, (1,)))


def _fox_start_blocks(proj, col0, c, bq, bk):
    S = proj.shape[0]
    nq, nk, nh = S // bq, S // bk, 2 * N_PAIRS

    def heads(first):
        return proj[:, first * LANES:(first + N_PAIRS) * LANES].astype(F32).reshape(S, nh, HEAD_DIM)

    q, k = heads(col0), heads(col0 + 4)
    qn = jnp.sqrt(jnp.sum(q * q, axis=-1))
    kmax = jnp.sqrt(jnp.sum(k * k, axis=-1)).max(axis=0)
    top = SCALE * (qn * kmax[None, :] - jnp.sum(q * k, axis=-1)) + c.T
    top = top.reshape(nq, bq, nh).max(axis=1)
    c_last = c[:, bk - 1::bk].T
    live = top[:, None, :] - c_last[None, :, :] >= -FOX_SKIP

    def first_block(lv):
        first = jnp.where(lv.any(axis=1), jnp.argmax(lv, axis=1), nk)
        return jnp.minimum(first, (bq // bk) * jnp.arange(nq)[:, None]).T.astype(jnp.int32)

    return jnp.concatenate([first_block(live.reshape(nq, nk, N_PAIRS, 2).any(axis=-1)), first_block(live)], axis=0)


def _fox_fwd(proj, col0, c_col, c_row, jstart, bq, bk):
    S = proj.shape[0]
    nq, per = S // bq, bq // bk

    def body(js_ref, q_ref, kv_hbm, cc_ref, cr_ref, o_ref, st_ref, acc_a, acc_b, qa, qb, ml, kbuf, vbuf, sems):
        p, i = pl.program_id(0), pl.program_id(1)
        j0 = js_ref[p, i]
        first_two = _first_two_up(lambda pair, blk: js_ref[pair, blk], per)
        fetch, kbuf, vbuf = _kv_fetcher(kv_hbm, kbuf, vbuf, sems, KV_SLOTS, col0, bk, p, i, nq, first_two)
        is_a = lax.broadcasted_iota(jnp.int32, (bq, LANES), 1) < HEAD_DIM
        acc_a[...] = jnp.zeros_like(acc_a)
        acc_b[...] = jnp.zeros_like(acc_b)
        ml[0] = jnp.full((bq, 1), NEG_BIG, F32)
        ml[2] = jnp.full((bq, 1), NEG_BIG, F32)
        ml[1] = jnp.zeros((bq, 1), F32)
        ml[3] = jnp.zeros((bq, 1), F32)
        cc = cc_ref[0]
        ml[4], ml[5] = _col(cc, 0), _col(cc, 1)
        qa[...], qb[...] = _masked_pair(q_ref[...], is_a, SCALE)

        def tile(j, slot, masked):
            k, v = kbuf[slot], vbuf[slot]
            cols = pl.ds(pl.multiple_of(j * bk, bk), bk)
            if masked:
                tri = _causal(bq, bk, i, j)

            def heads(hs):
                qs, accs = (qa, qb), (acc_a, acc_b)
                s = {h: _dot(qs[h][...], k, _NT) - cr_ref[0, pl.ds(h, 1), cols] for h in hs}
                if masked:
                    s = {h: jnp.where(tri, s[h], NEG_BIG) for h in hs}
                top = {h: jnp.max(s[h], axis=1, keepdims=True) for h in hs}
                m_new = {h: jnp.maximum(ml[2 * h], top[h] + ml[4 + h]) for h in hs}
                a = {h: jnp.exp(ml[2 * h] - m_new[h]) for h in hs}
                pr = {h: jnp.exp(s[h] - (m_new[h] - ml[4 + h])) for h in hs}
                tot = {h: jnp.sum(pr[h], axis=1, keepdims=True) for h in hs}
                pv = {h: _dot(pr[h].astype(BF16), v) for h in hs}
                for h in hs:
                    ml[2 * h] = m_new[h]
                    ml[2 * h + 1] = a[h] * ml[2 * h + 1] + tot[h]
                    accs[h][...] = a[h] * accs[h][...] + pv[h]

            _by_heads(j, js_ref[N_PAIRS + 2 * p, i], js_ref[N_PAIRS + 2 * p + 1, i], heads)

        _walk_up(fetch, j0, per * i, per * i + per - 1, tile)
        o_ref[...] = jnp.where(is_a, acc_a[...] / ml[1], acc_b[...] / ml[3])
        lane8 = lax.broadcasted_iota(jnp.int32, (bq, 8), 1)
        st = jnp.where(lane8 == 0, ml[0] + jnp.log(ml[1]), 0.0)
        st_ref[0] = jnp.where(lane8 == 1, ml[2] + jnp.log(ml[3]), st)

    grid_spec = pltpu.PrefetchScalarGridSpec(
        num_scalar_prefetch=1, grid=(N_PAIRS, nq),
        in_specs=[pl.BlockSpec((bq, LANES), lambda p, i, js: (i, col0 + p)),
                  pl.BlockSpec(memory_space=pl.ANY),
                  pl.BlockSpec((1, bq, 8), lambda p, i, js: (p, i, 0)),
                  pl.BlockSpec((1, 8, S), lambda p, i, js: (p, 0, 0))],
        out_specs=[pl.BlockSpec((bq, LANES), lambda p, i, js: (i, p)),
                   pl.BlockSpec((1, bq, 8), lambda p, i, js: (p, i, 0))],
        scratch_shapes=[pltpu.VMEM((bq, LANES), F32), pltpu.VMEM((bq, LANES), F32),
                        pltpu.VMEM((bq, LANES), BF16), pltpu.VMEM((bq, LANES), BF16),
                        pltpu.VMEM((6, bq, 1), F32),
                        pltpu.VMEM((2 * KV_SLOTS, bk, LANES), BF16), pltpu.VMEM((2 * KV_SLOTS, bk, LANES), BF16),
                        pltpu.SemaphoreType.DMA((2, 2 * KV_SLOTS))])
    return pl.pallas_call(
        body, name="fox_fwd", grid_spec=grid_spec,
        out_shape=[jax.ShapeDtypeStruct((S, GROUP_W), F32), jax.ShapeDtypeStruct((N_PAIRS, S, 8), F32)],
    )(jstart, proj, proj, c_col, c_row)


def _fox_bwd(proj, col0, do, o, st, c_col, c_row, jstart, bq, bk):
    S = proj.shape[0]
    nq, per = S // bq, bq // bk

    def body(js_ref, q_ref, kv_hbm, do_ref, o_ref, st_ref, cc_ref, cr_ref,
             dq_ref, dk_ref, dv_ref, dc_ref, dcq_ref, dq_a, dq_b, qa, qb, doa, dob, dd, kbuf, vbuf, sems):
        p, i = pl.program_id(0), pl.program_id(1)
        j0 = js_ref[p, i]
        first_two = _first_two_up(lambda pair, blk: js_ref[pair, blk], per)
        fetch, kbuf, vbuf = _kv_fetcher(kv_hbm, kbuf, vbuf, sems, KV_SLOTS, col0, bk, p, i, nq, first_two)
        is_a = lax.broadcasted_iota(jnp.int32, (bq, LANES), 1) < HEAD_DIM

        @pl.when(i == 0)
        def _():
            dk_ref[...] = jnp.zeros_like(dk_ref)
            dv_ref[...] = jnp.zeros_like(dv_ref)
            dc_ref[...] = jnp.zeros_like(dc_ref)

        dq_a[...] = jnp.zeros_like(dq_a)
        dq_b[...] = jnp.zeros_like(dq_b)
        qa[...], qb[...] = _masked_pair(q_ref[...], is_a, SCALE)
        dov = do_ref[...]
        doa[...], dob[...] = _masked_pair(dov, is_a)
        prod = dov * o_ref[...]
        dd[0] = jnp.sum(jnp.where(is_a, prod, 0.0), axis=1, keepdims=True)
        dd[1] = jnp.sum(jnp.where(is_a, 0.0, prod), axis=1, keepdims=True)
        dd[2] = jnp.zeros((bq, 1), F32)
        dd[3] = jnp.zeros((bq, 1), F32)
        cc, st_v = cc_ref[0], st_ref[0]
        dd[4], dd[5] = _col(cc, 0) - _col(st_v, 0), _col(cc, 1) - _col(st_v, 1)

        def tile(j, slot, masked):
            k, v = kbuf[slot], vbuf[slot]
            if masked:
                tri = _causal(bq, bk, i, j)
            cols = pl.ds(pl.multiple_of(j * bk, bk), bk)

            def heads(hs):
                qs, dos, dqs = (qa, qb), (doa, dob), (dq_a, dq_b)
                z = {h: _dot(qs[h][...], k, _NT) for h in hs}
                dp = {h: _dot(dos[h][...], v, _NT) for h in hs}
                pr = {h: jnp.exp(z[h] - cr_ref[0, pl.ds(h, 1), cols] + dd[4 + h]) for h in hs}
                if masked:
                    pr = {h: jnp.where(tri, pr[h], 0.0) for h in hs}
                ds = {h: pr[h] * (dp[h] - dd[h]) for h in hs}
                csum = {h: jnp.sum(ds[h], axis=0, keepdims=True) for h in hs}
                rsum = {h: jnp.sum(ds[h], axis=1, keepdims=True) for h in hs}
                dsb = {h: ds[h].astype(BF16) for h in hs}
                prb = {h: pr[h].astype(BF16) for h in hs}
                dqc = {h: _dot(dsb[h], k) for h in hs}
                dkc = [_dot(dsb[h], qs[h][...], _TN) for h in hs]
                dvc = [_dot(prb[h], dos[h][...], _TN) for h in hs]
                for h in hs:
                    dc_ref[0, pl.ds(h, 1), cols] -= csum[h]
                    dd[2 + h] += rsum[h]
                    dqs[h][...] += dqc[h]
                dk_ref[cols, :] += sum(dkc[1:], dkc[0])
                dv_ref[cols, :] += sum(dvc[1:], dvc[0])

            _by_heads(j, js_ref[N_PAIRS + 2 * p, i], js_ref[N_PAIRS + 2 * p + 1, i], heads)

        _walk_up(fetch, j0, per * i, per * i + per - 1, tile)
        dq_ref[...] = jnp.where(is_a, dq_a[...], dq_b[...]) * SCALE
        lane8 = lax.broadcasted_iota(jnp.int32, (bq, 8), 1)
        dcq_ref[0] = jnp.where(lane8 == 0, dd[2], jnp.where(lane8 == 1, dd[3], 0.0))

    grid_spec = pltpu.PrefetchScalarGridSpec(
        num_scalar_prefetch=1, grid=(N_PAIRS, nq),
        in_specs=[pl.BlockSpec((bq, LANES), lambda p, i, js: (i, col0 + p)),
                  pl.BlockSpec(memory_space=pl.ANY),
                  pl.BlockSpec((bq, LANES), lambda p, i, js: (i, p)),
                  pl.BlockSpec((bq, LANES), lambda p, i, js: (i, p)),
                  pl.BlockSpec((1, bq, 8), lambda p, i, js: (p, i, 0)),
                  pl.BlockSpec((1, bq, 8), lambda p, i, js: (p, i, 0)),
                  pl.BlockSpec((1, 8, S), lambda p, i, js: (p, 0, 0))],
        out_specs=[pl.BlockSpec((bq, LANES), lambda p, i, js: (i, p)),
                   pl.BlockSpec((S, LANES), lambda p, i, js: (0, p)),
                   pl.BlockSpec((S, LANES), lambda p, i, js: (0, p)),
                   pl.BlockSpec((1, 8, S), lambda p, i, js: (p, 0, 0)),
                   pl.BlockSpec((1, bq, 8), lambda p, i, js: (p, i, 0))],
        scratch_shapes=[pltpu.VMEM((bq, LANES), F32), pltpu.VMEM((bq, LANES), F32)]
        + [pltpu.VMEM((bq, LANES), BF16)] * 4 + [pltpu.VMEM((6, bq, 1), F32)]
        + [pltpu.VMEM((2 * KV_SLOTS, bk, LANES), BF16)] * 2 + [pltpu.SemaphoreType.DMA((2, 2 * KV_SLOTS))])
    return pl.pallas_call(
        body, name="fox_bwd", grid_spec=grid_spec,
        out_shape=[jax.ShapeDtypeStruct((S, GROUP_W), F32)] * 3
        + [jax.ShapeDtypeStruct((N_PAIRS, 8, S), F32), jax.ShapeDtypeStruct((N_PAIRS, S, 8), F32)],
        compiler_params=_params(VMEM_BIG),
    )(jstart, proj, proj, do, o, st, c_col, c_row)


_HBM = pl.BlockSpec(memory_space=pltpu.HBM)


def _coords():
    return lax.axis_index("x"), lax.axis_index("y"), lax.axis_index("c")


def _gather_copies(ins, outs, send_sems, recv_sems, loc_sems):
    n = len(ins)
    x, y, c = _coords()
    mine = 2 * x + y
    chips = [(1 - x, y), (x, 1 - y), (1 - x, 1 - y)]

    def copy(w, r, slab, to):
        return pltpu.make_async_remote_copy(
            src_ref=ins[w], dst_ref=outs[w].at[slab], send_sem=send_sems.at[3 * w + r],
            recv_sem=recv_sems.at[3 * w + r], device_id=to, device_id_type=MESH)

    def own():
        local = [pltpu.make_async_copy(ins[w], outs[w].at[mine], loc_sems.at[w]) for w in range(n)]
        return local, [copy(w, r, mine, (cx, cy, c)) for w in range(n) for r, (cx, cy) in enumerate(chips)]

    def start():
        local, sends = own()
        for cp in local + sends:
            cp.start()

    def wait():
        local, sends = own()
        for w in range(n):
            for r, (cx, cy) in enumerate(chips):
                copy(w, r, 2 * cx + cy, (cx, cy, c)).wait_recv()
        for cp in sends:
            cp.wait_send()
        for cp in local:
            cp.wait()

    return start, wait


def _gather_shapes(shards):
    n = len(shards)
    return ([jax.ShapeDtypeStruct((4,) + s.shape, s.dtype) for s in shards],
            [pltpu.SemaphoreType.DMA((3 * n,)), pltpu.SemaphoreType.DMA((3 * n,)), pltpu.SemaphoreType.DMA((n,))])


def _allgather_chips(shards):
    n = len(shards)

    def body(*refs):
        start, wait = _gather_copies(refs[:n], refs[n:2 * n], *refs[2 * n:])
        start()
        wait()

    out_shape, sems = _gather_shapes(shards)
    return pl.pallas_call(body, name="allgather_weights", in_specs=[_HBM] * n, out_specs=[_HBM] * n,
                          out_shape=out_shape, scratch_shapes=sems)(*shards)


def _proj_gather(x, w, shards, tm, tn):
    (M, K), N, n = x.shape, w.shape[1], len(shards)
    tm = min(tm, M)
    gi, gj = M // tm, N // tn

    def body(a_ref, b_ref, *rest):
        o_ref = rest[n]
        start, wait = _gather_copies(rest[:n], rest[n + 1:2 * n + 1], *rest[2 * n + 1:])
        i, j = pl.program_id(0), pl.program_id(1)
        pl.when(jnp.logical_and(i == 0, j == 0))(start)
        o_ref[...] = _dot(a_ref[...].astype(BF16), b_ref[...]).astype(o_ref.dtype)
        pl.when(jnp.logical_and(i == gi - 1, j == gj - 1))(wait)

    out_shape, sems = _gather_shapes(shards)
    return pl.pallas_call(
        body, name="proj_gather", grid=(gi, gj),
        in_specs=[pl.BlockSpec((tm, K), lambda i, j: (i, 0)), pl.BlockSpec((K, tn), lambda i, j: (0, j))] + [_HBM] * n,
        out_specs=[pl.BlockSpec((tm, tn), lambda i, j: (i, j))] + [_HBM] * n,
        out_shape=[jax.ShapeDtypeStruct((M, N), BF16)] + out_shape, scratch_shapes=sems,
    )(x, w, *shards)


def _exchange(parts, per_chip):
    n = len(parts)
    half = [p.shape[1] // 2 for p in parts] if per_chip else None

    def body(*refs):
        ins, outs = refs[:n], refs[n:2 * n]
        send_sems, recv_sems, loc_sems = refs[2 * n:]
        x, y, c = _coords()
        me = 4 * x + 2 * y + c
        peers = [(x ^ fx, y ^ fy, c ^ fc) for fx in (0, 1) for fy in (0, 1) for fc in (0, 1)][1:]

        def src(w, dev):
            if not per_chip:
                return ins[w]
            return ins[w].at[2 * dev[0] + dev[1], pl.ds(pl.multiple_of(dev[2] * half[w], 16), half[w]), :]

        local = [pltpu.make_async_copy(src(w, (x, y, c)), outs[w].at[me], loc_sems.at[w]) for w in range(n)]
        for cp in local:
            cp.start()

        def copy(w, r, source, slab, to):
            return pltpu.make_async_remote_copy(
                src_ref=source, dst_ref=outs[w].at[slab], send_sem=send_sems.at[7 * w + r],
                recv_sem=recv_sems.at[7 * w + r], device_id=to, device_id_type=MESH)

        sends = [copy(w, r, src(w, dev), me, dev) for w in range(n) for r, dev in enumerate(peers)]
        for cp in sends:
            cp.start()
        for w in range(n):
            for r, dev in enumerate(peers):
                copy(w, r, src(w, dev), 4 * dev[0] + 2 * dev[1] + dev[2], dev).wait_recv()
        for cp in sends:
            cp.wait_send()
        for cp in local:
            cp.wait()

    return pl.pallas_call(
        body, name="exchange_per_chip" if per_chip else "exchange_all",
        in_specs=[_HBM] * n, out_specs=[_HBM] * n,
        out_shape=[jax.ShapeDtypeStruct((8, half[w], p.shape[2]) if per_chip else (8,) + p.shape, p.dtype)
                   for w, p in enumerate(parts)],
        scratch_shapes=[pltpu.SemaphoreType.DMA((7 * n,)), pltpu.SemaphoreType.DMA((7 * n,)),
                        pltpu.SemaphoreType.DMA((n,))],
    )(*parts)


def _sibling_swap(halves):
    n = len(halves)

    def body(*refs):
        ins, outs = refs[:n], refs[n:2 * n]
        send_sems, recv_sems, loc_sems = refs[2 * n:]
        x, y, c = _coords()

        def rows(w, core):
            rh = halves[w].shape[0]
            return outs[w].at[pl.ds(pl.multiple_of(core * rh, 8), rh), :]

        def copy(w, core):
            return pltpu.make_async_remote_copy(
                src_ref=ins[w], dst_ref=rows(w, core), send_sem=send_sems.at[w], recv_sem=recv_sems.at[w],
                device_id=(x, y, 1 - c), device_id_type=MESH)

        local = [pltpu.make_async_copy(ins[w], rows(w, c), loc_sems.at[w]) for w in range(n)]
        sends = [copy(w, c) for w in range(n)]
        for cp in local + sends:
            cp.start()
        for w in range(n):
            copy(w, 1 - c).wait_recv()
        for cp in sends:
            cp.wait_send()
        for cp in local:
            cp.wait()

    vmem = pl.BlockSpec(memory_space=pltpu.VMEM)
    return pl.pallas_call(
        body, name="sibling_swap", in_specs=[vmem] * n, out_specs=[vmem] * n,
        out_shape=[jax.ShapeDtypeStruct((2 * h.shape[0], h.shape[1]), h.dtype) for h in halves],
        scratch_shapes=[pltpu.SemaphoreType.DMA((n,)), pltpu.SemaphoreType.DMA((n,)), pltpu.SemaphoreType.DMA((n,))],
    )(*halves)


def _adamw(w, g, m, v):
    m = ADAM_B1 * m + (1.0 - ADAM_B1) * g
    v = ADAM_B2 * v + (1.0 - ADAM_B2) * (g * g)
    m_hat = m / (1.0 - ADAM_B1 ** ADAM_STEP)
    v_hat = v / (1.0 - ADAM_B2 ** ADAM_STEP)
    delta = -ADAM_LR * (m_hat / (jnp.sqrt(v_hat) + ADAM_EPS) + ADAM_WD * w)
    return delta, m, v


def _sum_parts(parts, name, tr):
    _, R, C = parts.shape
    assert R % tr == 0

    def body(p_ref, g_ref):
        g = p_ref[0].astype(F32)
        for d in range(1, 8):
            g = g + p_ref[d].astype(F32)
        g_ref[...] = g

    return pl.pallas_call(
        body, name=name, grid=(R // tr,),
        in_specs=[pl.BlockSpec((8, tr, C), lambda i: (0, i, 0))],
        out_specs=pl.BlockSpec((tr, C), lambda i: (i, 0)), out_shape=jax.ShapeDtypeStruct((R, C), F32),
    )(parts)


def _adamw_call(g, w, m, v, name, tr):
    R, C = w.shape
    assert R % tr == 0

    def body(g_ref, w_ref, m_ref, v_ref, d_ref, nm_ref, nv_ref):
        d_ref[...], nm_ref[...], nv_ref[...] = _adamw(w_ref[...], g_ref[...], m_ref[...], v_ref[...])

    tile = pl.BlockSpec((tr, C), lambda i: (i, 0))
    return pl.pallas_call(
        body, name=name, grid=(R // tr,), in_specs=[tile] * 4,
        out_specs=[tile] * 3, out_shape=[jax.ShapeDtypeStruct((R, C), F32)] * 3,
    )(g, w, m, v)


def _sum_adamw_small(parts, w, m, v):
    def body(p_ref, w_ref, m_ref, v_ref, g_ref, d_ref, nm_ref, nv_ref, loss_ref):
        g = p_ref[0]
        for d in range(1, 8):
            g = g + p_ref[d]
        g_ref[...] = g
        d_ref[...], nm_ref[...], nv_ref[...] = _adamw(w_ref[...], g, m_ref[...], v_ref[...])
        row = lax.broadcasted_iota(jnp.int32, g.shape, 0)
        per_row = jnp.sum(jnp.where(row == 6, g, 0.0), axis=1, keepdims=True)
        loss_ref[...] = jnp.zeros((8, LANES), F32) + jnp.sum(per_row, axis=0, keepdims=True)

    return pl.pallas_call(
        body, name="sum_adamw_small",
        out_shape=[jax.ShapeDtypeStruct((8, D_MODEL), F32)] * 4 + [jax.ShapeDtypeStruct((8, LANES), F32)],
    )(parts, w, m, v)


def _pack_small(ln1_g, ln1_b, ln2_g, ln2_b, g_sb, g_fox, b_f):
    row5 = jnp.pad(b_f.reshape(1, N_FOX), ((0, 0), (0, D_MODEL - N_FOX)))
    rows = [ln1_g.reshape(1, -1), ln1_b.reshape(1, -1), ln2_g.reshape(1, -1), ln2_b.reshape(1, -1),
            jnp.concatenate([g_sb.reshape(1, -1), g_fox.reshape(1, -1)], axis=1), row5,
            jnp.zeros((2, D_MODEL), F32)]
    return jnp.concatenate(rows, axis=0)


def _unpack_small(p):
    return {"ln1_g": p[0:1], "ln1_b": p[1:2], "ln2_g": p[2:3], "ln2_b": p[3:4], "g_sb": p[4:5, :GROUP_W],
            "g_fox": p[4:5, GROUP_W:], "b_f": p[5:6, :N_FOX]}


def kernel(x, w_in, b_f, g_sb, g_fox, w_out, ln1_g, ln1_b, ln2_g, ln2_b, w_gate_up, w_down, loss_target, m_w_in, m_b_f, m_g_sb, m_g_fox, m_w_out, m_ln1_g, m_ln1_b, m_ln2_g, m_ln2_b, m_w_gate_up, m_w_down, v_w_in, v_b_f, v_g_sb, v_g_fox, v_w_out, v_ln1_g, v_ln1_b, v_ln2_g, v_ln2_b, v_w_gate_up, v_w_down):
    S = x.shape[1]
    x2 = x.reshape(S, D_MODEL)
    tgt = loss_target.reshape(S, D_MODEL)
    TM = 1024
    TR = 512
    BQ = ATTN_BLOCK
    in_w = w_in.shape[2]
    gu_w = w_gate_up.shape[2]

    shards = [w_in[0].astype(BF16), w_out[0].astype(BF16), w_gate_up[0].astype(BF16), w_down[0].astype(BF16)]
    (wi_s,) = _allgather_chips(shards[:1])
    wi = wi_s.transpose(1, 0, 2).reshape(D_MODEL, 4 * in_w)
    w_sb, w_fx = wi[:, :QKV_W // 2], wi[:, QKV_W // 2:QKV_W]
    wqkv = wi[:, :QKV_W]
    wft = wi[:, QKV_W:].T
    proj, wo_s, wgu_s, wd_s = _proj_gather(x2, wqkv, shards[1:], TM, 512)
    wo = wo_s.reshape(D_MODEL, D_MODEL)
    wgu = wgu_s.transpose(1, 0, 2).reshape(D_MODEL, 2 * D_FF)
    wg, wu = wgu[:, :D_FF], wgu[:, D_FF:]
    wd = wd_s.reshape(D_FF, D_MODEL)
    g_row = jnp.concatenate([g_sb, g_fox], axis=1)
    hid = np.arange(D_MODEL) // HEAD_DIM
    he_np = (hid[:, None] == np.arange(LANES)[None, :]).astype(np.float32)
    he, het = jnp.asarray(he_np, BF16), jnp.asarray(he_np.T, BF16)

    lf = _fgate_fwd(x2, wft, b_f.reshape(N_FOX, 1), TM)
    c = _cumsum_fwd(lf)
    c_pair = c.reshape(N_PAIRS, 2, S)
    c_row = jnp.pad(c_pair, ((0, 0), (0, 6), (0, 0)))
    c_col = jnp.pad(c_pair.transpose(0, 2, 1), ((0, 0), (0, 0), (0, 6)))

    o_sb, st_sb, jmin_sb = _sb_fwd(proj, 0, BQ)
    jstart_fx = _fox_start_blocks(proj, 12, c, BQ, BQ)
    o_fx, st_fx = _fox_fwd(proj, 12, c_col, c_row, jstart_fx, BQ, BQ)

    def attn_post(i, osb_ref, ofx_ref, g_ref, he_ref, het_ref, on_ref):
        o = jnp.concatenate([osb_ref[...], ofx_ref[...]], axis=1)
        ms = _head_sums(o * o, he_ref[...], het_ref[...]) * (1.0 / HEAD_DIM)
        on_ref[...] = (o * lax.rsqrt(ms + RMS_EPS) * g_ref[...]).astype(BF16)

    (on,) = _rowwise(attn_post, "attn_post", S, TR,
                     [(o_sb, "t"), (o_fx, "t"), (g_row, "f"), (he, "f"), (het, "f")],
                     [((S, D_MODEL), BF16, "t")])

    u1 = _matmul(on, wo, mode="nn", name="mix", tm=TM, tn=D_MODEL, tk=D_MODEL, outs=[F32],
                 extras=[(x2, (TM if S >= TM else S, D_MODEL), _tile_ij)],
                 epilogue=lambda acc, xv: (ALPHA * xv + acc,))

    def ln1_fwd(i, u_ref, g_ref, b_ref, h_ref):
        xh, _ = _ln_stats(u_ref[...])
        h_ref[...] = xh * g_ref[...] + b_ref[...]

    (h1,) = _rowwise(ln1_fwd, "ln1_fwd", S, TR, [(u1, "t"), (ln1_g, "f"), (ln1_b, "f")], [((S, D_MODEL), F32, "t")])

    tm_e = TM if S >= TM else S
    n_ff = D_FF // 256

    def gate_up_body(h_ref, wg_ref, wu_ref, g_ref, u_ref, a_ref):
        h = h_ref[...].astype(BF16)
        g, u = _dot(h, wg_ref[...]), _dot(h, wu_ref[...])
        g_ref[...] = g.astype(BF16)
        u_ref[...] = u.astype(BF16)
        a_ref[...] = (g / (1.0 + jnp.exp(-g)) * u).astype(BF16)

    ff_tile = pl.BlockSpec((tm_e, 256), lambda i, j: (i, j))
    gate, up, act = pl.pallas_call(
        gate_up_body, name="gate_up_act", grid=(S // tm_e, n_ff),
        in_specs=[pl.BlockSpec((tm_e, D_MODEL), lambda i, j: (i, 0)),
                  pl.BlockSpec((D_MODEL, 256), lambda i, j: (0, j)),
                  pl.BlockSpec((D_MODEL, 256), lambda i, j: (0, j + n_ff))],
        out_specs=[ff_tile] * 3, out_shape=[jax.ShapeDtypeStruct((S, D_FF), BF16)] * 3)(h1, wgu, wgu)

    u2 = _matmul(act, wd, mode="nn", name="ffn_down", tm=TM, tn=D_MODEL, tk=D_FF, outs=[F32],
                 extras=[(h1, (TM if S >= TM else S, D_MODEL), _tile_ij)],
                 epilogue=lambda acc, hv: (ALPHA * hv + acc,))

    def ln2_loss(i, u_ref, t_ref, g_ref, b_ref, du_ref, acc_ref):
        xh, r = _ln_stats(u_ref[...])
        g = g_ref[...]
        err = xh * g + b_ref[...] - t_ref[...]
        dy = err * (1.0 / D_MODEL)
        du_ref[...] = _ln_bwd(dy, xh, r, g)
        _acc_rows(i, acc_ref, {2: jnp.sum(dy * xh, axis=0, keepdims=True), 3: jnp.sum(dy, axis=0, keepdims=True),
                               6: jnp.sum(err * err, axis=0, keepdims=True) * (0.5 / D_MODEL)})

    du2, acc_ln2 = _rowwise(ln2_loss, "ln2_loss", S, TR, [(u2, "t"), (tgt, "t"), (ln2_g, "f"), (ln2_b, "f")],
                            [((S, D_MODEL), F32, "t"), ((8, D_MODEL), F32, "f")])

    d_wd = _matmul(act, du2, mode="tn", name="dw_down", tm=1408, tn=D_MODEL, tk=TM, outs=[BF16])

    def dgu_epilogue(da, g, u):
        g, u = g.astype(F32), u.astype(F32)
        s = 1.0 / (1.0 + jnp.exp(-g))
        return da * u * (s * (1.0 + g * (1.0 - s))), da * (g * s)

    dgate, dup = _matmul(du2, wd, mode="nt", name="d_act", tm=TM, tn=1408, tk=D_MODEL, outs=[BF16, BF16],
                         extras=[(gate, (tm_e, 1408), _tile_ij), (up, (tm_e, 1408), _tile_ij)],
                         epilogue=dgu_epilogue)
    d_wg = _matmul(h1, dgate, mode="tn", name="dw_gate", tm=D_MODEL, tn=1408, tk=TM, outs=[BF16])
    d_wu = _matmul(h1, dup, mode="tn", name="dw_up", tm=D_MODEL, tn=1408, tk=TM, outs=[BF16])
    dh1 = _matmul(dgate, wg, mode="nt", name="dh1_gate", tm=TM, tn=D_MODEL, tk=D_FF, outs=[F32],
                  extras=[(du2, (tm_e, D_MODEL), _tile_ij)], epilogue=lambda acc, e: (ALPHA * e + acc,))
    dh1 = _matmul(dup, wu, mode="nt", name="dh1_up", tm=TM, tn=D_MODEL, tk=D_FF, outs=[F32],
                  extras=[(dh1, (tm_e, D_MODEL), _tile_ij)], epilogue=lambda acc, e: (e + acc,))

    def ln1_bwd(i, dh_ref, u_ref, g_ref, du_ref, acc_ref):
        xh, r = _ln_stats(u_ref[...])
        dh = dh_ref[...]
        du_ref[...] = _ln_bwd(dh, xh, r, g_ref[...])
        _acc_rows(i, acc_ref, {0: jnp.sum(dh * xh, axis=0, keepdims=True), 1: jnp.sum(dh, axis=0, keepdims=True)})

    du1, acc_ln1 = _rowwise(ln1_bwd, "ln1_bwd", S, TR, [(dh1, "t"), (u1, "t"), (ln1_g, "f")],
                            [((S, D_MODEL), F32, "t"), ((8, D_MODEL), F32, "f")])
    d_wo = _matmul(on, du1, mode="tn", name="dw_out", tm=D_MODEL, tn=D_MODEL, tk=TM, outs=[BF16])
    don = _matmul(du1, wo, mode="nt", name="d_on", tm=TM, tn=D_MODEL, tk=D_MODEL, outs=[F32])

    def rms_bwd(i, don_ref, osb_ref, ofx_ref, g_ref, he_ref, het_ref, dosb_ref, dofx_ref, acc_ref):
        o = jnp.concatenate([osb_ref[...], ofx_ref[...]], axis=1)
        hev, hetv = he_ref[...], het_ref[...]
        r = lax.rsqrt(_head_sums(o * o, hev, hetv) * (1.0 / HEAD_DIM) + RMS_EPS)
        dn = don_ref[...]
        dg = dn * g_ref[...]
        do = r * dg - o * (r * r * r) * (_head_sums(dg * o, hev, hetv) * (1.0 / HEAD_DIM))
        dosb_ref[...] = do[:, :GROUP_W]
        dofx_ref[...] = do[:, GROUP_W:]
        _acc_rows(i, acc_ref, {4: jnp.sum(dn * o * r, axis=0, keepdims=True)})

    do_sb, do_fx, acc_rms = _rowwise(
        rms_bwd, "rms_bwd", S, TR, [(don, "t"), (o_sb, "t"), (o_fx, "t"), (g_row, "f"), (he, "f"), (het, "f")],
        [((S, GROUP_W), F32, "t"), ((S, GROUP_W), F32, "t"), ((8, D_MODEL), F32, "f")])

    dq_sb, dk_sb, dv_sb = _sb_bwd(proj, 0, do_sb, st_sb, jmin_sb, BQ)
    jstart_fx2 = jnp.minimum(jstart_fx[:, 0::2], jstart_fx[:, 1::2])
    dq_fx, dk_fx, dv_fx, dc, dcq = _fox_bwd(proj, 12, do_fx, o_fx, st_fx, c_col, c_row, jstart_fx2, 2 * BQ, BQ)
    dc = dc[:, :2, :] + dcq[:, :, :2].transpose(0, 2, 1)
    dfl, dbf = _fgate_bwd(dc.reshape(N_FOX, S), lf)
    dp_sb = jnp.concatenate([dq_sb, dk_sb, dv_sb], axis=1).astype(BF16)
    dp_fx = jnp.concatenate([dq_fx, dk_fx, dv_fx], axis=1).astype(BF16)

    d_wsb = _matmul(x2, dp_sb, mode="tn", name="dw_in_sb", tm=D_MODEL, tn=QKV_W // 2, tk=TM, outs=[BF16])
    d_wfx = _matmul(x2, dp_fx, mode="tn", name="dw_in_fx", tm=D_MODEL, tn=QKV_W // 2, tk=TM, outs=[BF16])
    d_wft = _matmul(dfl, x2, mode="nn", name="dw_in_f", tm=N_FOX, tn=D_MODEL, tk=TM, outs=[BF16])
    dx = _matmul(dp_sb, w_sb, mode="nt", name="dx_sb", tm=TM, tn=D_MODEL, tk=QKV_W // 2, outs=[F32],
                 extras=[(du1, (tm_e, D_MODEL), _tile_ij)], epilogue=lambda acc, e: (ALPHA * e + acc,))
    dx = _matmul(dp_fx, w_fx, mode="nt", name="dx_fx", tm=TM, tn=D_MODEL, tk=QKV_W // 2, outs=[F32],
                 extras=[(dx, (tm_e, D_MODEL), _tile_ij)], epilogue=lambda acc, e: (e + acc,))
    dx = _matmul(dfl, wft, mode="tn", name="dx_f", tm=TM, tn=D_MODEL, tk=N_FOX, outs=[F32],
                 extras=[(dx, (tm_e, D_MODEL), _tile_ij)], epilogue=lambda acc, e: (e + acc,))

    d_wi = jnp.concatenate([d_wsb, d_wfx, d_wft.T], axis=1)
    d_wgu = jnp.concatenate([d_wg, d_wu], axis=1)
    parts = [d_wi.reshape(D_MODEL, 4, in_w).transpose(1, 0, 2).astype(BF16),
             d_wo.reshape(4, D_MODEL // 4, D_MODEL).astype(BF16),
             d_wgu.reshape(D_MODEL, 4, gu_w).transpose(1, 0, 2).astype(BF16),
             d_wd.reshape(4, D_FF // 4, D_MODEL).astype(BF16)]
    got = _exchange(parts, True)
    big_names = ("w_in", "w_out", "w_gate_up", "w_down")
    halves = [_sum_parts(p, "sum_" + nm, tr) for nm, p, tr in zip(big_names, got, (256, 128, 128, 176))]
    grads = _sibling_swap(halves)
    big = {}
    for nm, g, w, m, v, tr in zip(big_names, grads, (w_in, w_out, w_gate_up, w_down),
                                  (m_w_in, m_w_out, m_w_gate_up, m_w_down),
                                  (v_w_in, v_w_out, v_w_gate_up, v_w_down), (256, 256, 256, 176)):
        big[nm] = [r[None] for r in [g] + list(_adamw_call(g, w[0], m[0], v[0], "adamw_" + nm, tr))]

    small = acc_ln2 + acc_ln1 + acc_rms
    small = small + jnp.pad(dbf.reshape(1, N_FOX), ((5, 2), (0, D_MODEL - N_FOX)))
    (small_all,) = _exchange([small], False)
    sw = _pack_small(ln1_g, ln1_b, ln2_g, ln2_b, g_sb, g_fox, b_f)
    sm = _pack_small(m_ln1_g, m_ln1_b, m_ln2_g, m_ln2_b, m_g_sb, m_g_fox, m_b_f)
    sv = _pack_small(v_ln1_g, v_ln1_b, v_ln2_g, v_ln2_b, v_g_sb, v_g_fox, v_b_f)
    sg, sd, snm, snv, loss_blk = _sum_adamw_small(small_all, sw, sm, sv)
    sg, sd, snm, snv = _unpack_small(sg), _unpack_small(sd), _unpack_small(snm), _unpack_small(snv)

    names = ["w_in", "b_f", "g_sb", "g_fox", "w_out", "ln1_g", "ln1_b", "ln2_g", "ln2_b", "w_gate_up", "w_down"]
    outs = [loss_blk[0, 0], dx.reshape(1, S, D_MODEL)]
    for k, table in enumerate((sg, sd, snm, snv)):
        outs += [big[n][k] if n in big else table[n] for n in names]
    return tuple(outs)
```

```python
import functools

import numpy as np
import jax
import jax.numpy as jnp
from jax import lax
from jax.experimental import pallas as pl
from jax.experimental.pallas import tpu as pltpu

F32 = jnp.float32
BF16 = jnp.bfloat16

D_MODEL = 1024
HEAD_DIM = 64
LANES = 128
N_PAIRS = 4
GROUP_W = 512
QKV_W = 3072
D_FF = 2816
N_FOX = 8
ALPHA = 2.0 ** 0.25
LN_EPS = 1e-5
RMS_EPS = 1e-6
SCALE = HEAD_DIM ** -0.5
NEG_BIG = -1e30
FOX_SKIP = 30.0
SB_STOP = -105.0
ADAM_LR, ADAM_B1, ADAM_B2, ADAM_EPS, ADAM_WD, ADAM_STEP = 0.001, 0.9, 0.999, 1e-08, 0.01, 10
KV_SLOTS = 4
SCAN_GROUP = 8
ATTN_BLOCK = 256
VMEM_BIG = 56 * 1024 * 1024
MESH = pl.DeviceIdType.MESH

_NN = (((1,), (0,)), ((), ()))
_NT = (((1,), (1,)), ((), ()))
_TN = (((0,), (0,)), ((), ()))


def _dot(a, b, dims=_NN):
    return lax.dot_general(a, b, dims, preferred_element_type=F32)


def _split_dot(x, t):
    hi = x.astype(BF16)
    lo = (x - hi.astype(F32)).astype(BF16)
    return _dot(hi, t) + _dot(lo, t)


def _softplus(z):
    return jnp.maximum(z, 0.0) + jnp.log1p(jnp.exp(-jnp.abs(z)))


def _col(v, h):
    lane = lax.broadcasted_iota(jnp.int32, v.shape, 1)
    return jnp.sum(jnp.where(lane == h, v, 0.0), axis=1, keepdims=True)


def _two_sum(hi, lo, b):
    s = hi + b
    bb = s - hi
    err = (hi - (s - bb)) + (b - bb)
    return s, lo + err


def _params(vmem=None):
    return pltpu.CompilerParams(vmem_limit_bytes=vmem) if vmem else None


def _matmul(a, b, *, mode, name, tm, tn, tk, outs, extras=(), epilogue=None, vmem=None):
    if mode == "nn":
        (M, K), (_, N) = a.shape, b.shape
    elif mode == "nt":
        (M, K), (N, _) = a.shape, b.shape
    else:
        (K, M), (_, N) = a.shape, b.shape
    tm, tn, tk = min(tm, M), min(tn, N), min(tk, K)
    assert M % tm == 0 and N % tn == 0 and K % tk == 0, (name, M, N, K, tm, tn, tk)
    nk = K // tk
    dims = {"nn": _NN, "nt": _NT, "tn": _TN}[mode]
    if mode == "tn":
        a_spec = pl.BlockSpec((tk, tm), lambda i, j, k: (k, i))
    else:
        a_spec = pl.BlockSpec((tm, tk), lambda i, j, k: (i, k))
    if mode == "nt":
        b_spec = pl.BlockSpec((tn, tk), lambda i, j, k: (j, k))
    else:
        b_spec = pl.BlockSpec((tk, tn), lambda i, j, k: (k, j))
    ex_specs = [pl.BlockSpec(bs, (lambda i, j, k, f=f: f(i, j))) for (_, bs, f) in extras]
    ne, no = len(extras), len(outs)
    if epilogue is None:
        epilogue = lambda acc: (acc,)

    def body(a_ref, b_ref, *rest):
        ex_refs, out_refs, acc = rest[:ne], rest[ne:ne + no], rest[-1]
        k = pl.program_id(2)

        @pl.when(k == 0)
        def _():
            acc[...] = jnp.zeros_like(acc)

        acc[...] += _dot(a_ref[...].astype(BF16), b_ref[...].astype(BF16), dims)

        @pl.when(k == nk - 1)
        def _():
            res = epilogue(acc[...], *[e[...] for e in ex_refs])
            for r, o in zip(res, out_refs):
                o[...] = r.astype(o.dtype)

    res = pl.pallas_call(
        body, name=name, grid=(M // tm, N // tn, nk),
        in_specs=[a_spec, b_spec] + ex_specs,
        out_specs=[pl.BlockSpec((tm, tn), lambda i, j, k: (i, j)) for _ in outs],
        out_shape=[jax.ShapeDtypeStruct((M, N), d) for d in outs],
        scratch_shapes=[pltpu.VMEM((tm, tn), F32)],
        compiler_params=_params(vmem),
    )(a, b, *[e[0] for e in extras])
    return res[0] if no == 1 else res


def _tile_ij(i, j):
    return (i, j)


def _rowwise(fn, name, rows, tm, ins, outs, vmem=None):
    tm = min(tm, rows)
    assert rows % tm == 0

    def spec(shape, kind):
        if kind == "t":
            return pl.BlockSpec((tm,) + tuple(shape[1:]), lambda i: (i,) + (0,) * (len(shape) - 1))
        return pl.BlockSpec(tuple(shape), lambda i: (0,) * len(shape))

    def body(*refs):
        fn(pl.program_id(0), *refs)

    return pl.pallas_call(
        body, name=name, grid=(rows // tm,),
        in_specs=[spec(a.shape, k) for a, k in ins],
        out_specs=[spec(s, k) for s, _, k in outs],
        out_shape=[jax.ShapeDtypeStruct(s, d) for s, d, _ in outs],
        compiler_params=_params(vmem),
    )(*[a for a, _ in ins])


def _ln_stats(u):
    mu = jnp.mean(u, axis=-1, keepdims=True)
    d = u - mu
    var = jnp.mean(d * d, axis=-1, keepdims=True)
    r = lax.rsqrt(var + LN_EPS)
    return d * r, r


def _ln_bwd(dh, xh, r, g):
    dxh = dh * g
    m1 = jnp.mean(dxh, axis=-1, keepdims=True)
    m2 = jnp.mean(dxh * xh, axis=-1, keepdims=True)
    return r * (dxh - m1 - xh * m2)


def _acc_rows(i, ref, rows):
    @pl.when(i == 0)
    def _():
        ref[...] = jnp.zeros_like(ref)
    for r, v in rows.items():
        ref[pl.ds(r, 1), :] += v


def _head_sums(v, he, het):
    return _split_dot(_split_dot(v, he), het)


def _fgate_fwd(x, wft, bf_col, tm):
    S = x.shape[0]
    tm = min(tm, S)

    def body(wft_ref, bf_ref, x_ref, lf_ref):
        f = _dot(wft_ref[...], x_ref[...].astype(BF16), _NT) + bf_ref[...]
        lf_ref[...] = -_softplus(-f)

    return pl.pallas_call(
        body, name="fgate_fwd", grid=(S // tm,),
        in_specs=[pl.BlockSpec((N_FOX, D_MODEL), lambda i: (0, 0)), pl.BlockSpec((N_FOX, 1), lambda i: (0, 0)),
                  pl.BlockSpec((tm, D_MODEL), lambda i: (i, 0))],
        out_specs=pl.BlockSpec((N_FOX, tm), lambda i: (0, i)),
        out_shape=jax.ShapeDtypeStruct((N_FOX, S), F32),
    )(wft, bf_col, x)


def _chunk_scan(v, reverse):
    lane = lax.broadcasted_iota(jnp.int32, v.shape, 1)
    sh = 1
    while sh < LANES:
        if reverse:
            v = v + jnp.where(lane < LANES - sh, pltpu.roll(v, LANES - sh, 1), 0.0)
        else:
            v = v + jnp.where(lane >= sh, pltpu.roll(v, sh, 1), 0.0)
        sh *= 2
    return v


def _cumsum_fwd(lf):
    n, S = lf.shape
    nc = S // LANES

    grp = min(SCAN_GROUP, nc)

    def body(lf_ref, c_ref):
        def step(gi, carry):
            sls = [pl.ds(pl.multiple_of((gi * grp + g) * LANES, LANES), LANES) for g in range(grp)]
            vs = [_chunk_scan(lf_ref[:, sl], False) for sl in sls]
            tots = [_col(v, LANES - 1) for v in vs]
            for sl, v, t in zip(sls, vs, tots):
                c_ref[:, sl] = v + carry
                carry = carry + t
            return carry
        lax.fori_loop(0, nc // grp, step, jnp.zeros((n, 1), F32))

    return pl.pallas_call(body, name="cumsum_fwd", out_shape=jax.ShapeDtypeStruct((n, S), F32))(lf)


def _fgate_bwd(dc, lf):
    n, S = dc.shape
    nc = S // LANES

    grp = min(SCAN_GROUP, nc)

    def body(dc_ref, lf_ref, dfl_ref, dbf_ref):
        def step(t, carry):
            car, tot = carry
            gi = nc // grp - 1 - t
            sls = [pl.ds(pl.multiple_of((gi * grp + g) * LANES, LANES), LANES) for g in range(grp)]
            vs = [_chunk_scan(dc_ref[:, sl], True) for sl in sls]
            firsts = [_col(v, 0) for v in vs]
            for sl, v, f in reversed(list(zip(sls, vs, firsts))):
                dfl = (v + car) * (1.0 - jnp.exp(lf_ref[:, sl]))
                dfl_ref[:, sl] = dfl
                tot = tot + jnp.sum(dfl, axis=1, keepdims=True)
                car = car + f
            return car, tot
        _, tot = lax.fori_loop(0, nc // grp, step, (jnp.zeros((n, 1), F32), jnp.zeros((n, 1), F32)))
        dbf_ref[...] = tot

    return pl.pallas_call(body, name="fgate_bwd",
                          out_shape=[jax.ShapeDtypeStruct((n, S), F32), jax.ShapeDtypeStruct((n, 1), F32)])(dc, lf)


def _tri_matrices(b):
    r = np.arange(b)
    tfwd = (r[:, None] <= r[None, :]).astype(np.float32)
    return jnp.asarray(tfwd, BF16), jnp.asarray(tfwd.T, BF16)


def _kv_copies(kv_hbm, kbuf, vbuf, sems, sem0, pair_col, bq, j, slot):
    rows = pl.ds(pl.multiple_of(j * bq, bq), bq)

    def cols(c):
        return pl.ds(pl.multiple_of((pair_col + c) * LANES, LANES), LANES)

    return (pltpu.make_async_copy(kv_hbm.at[rows, cols(4)], kbuf.at[slot], sems.at[0, sem0 + slot]),
            pltpu.make_async_copy(kv_hbm.at[rows, cols(8)], vbuf.at[slot], sems.at[1, sem0 + slot]))


def _first_two_up(first_block, per=1):
    def blocks(pair, blk):
        first = first_block(pair, blk)
        return first, first + 1, first + 1 <= per * blk + per - 1
    return blocks


def _first_two_down(pair, blk):
    return blk, blk - 1, blk > 0


def _start_two(fetch, pair, first, second, has_second, ahead):
    for cp in fetch(first, 0, pair, ahead):
        cp.start()

    @pl.when(has_second)
    def _():
        for cp in fetch(second, 1, pair, ahead):
            cp.start()


def _kv_fetcher(kv_hbm, kbuf, vbuf, sems, ns, col0, bq, p, i, nq, blocks):
    base = lax.rem(p * nq + i, 2) * ns
    own = (kbuf.at[pl.ds(base, ns)], vbuf.at[pl.ds(base, ns)])
    other = (kbuf.at[pl.ds(ns - base, ns)], vbuf.at[pl.ds(ns - base, ns)])

    def fetch(j, slot, pair=p, ahead=False):
        kb, vb = other if ahead else own
        return _kv_copies(kv_hbm, kb, vb, sems, ns - base if ahead else base, col0 + pair, bq, j, slot)

    pl.when(jnp.logical_and(p == 0, i == 0))(lambda: _start_two(fetch, p, *blocks(p, i), False))
    wrap = i == nq - 1

    @pl.when(jnp.logical_not(jnp.logical_and(wrap, p == N_PAIRS - 1)))
    def _():
        pair, blk = jnp.where(wrap, p + 1, p), jnp.where(wrap, 0, i + 1)
        _start_two(fetch, pair, *blocks(pair, blk), True)

    return fetch, own[0], own[1]


def _masked_pair(v, lane_is_a, scale=1.0):
    v = v.astype(F32) * scale
    return jnp.where(lane_is_a, v, 0.0).astype(BF16), jnp.where(lane_is_a, 0.0, v).astype(BF16)


def _sb_fwd(proj, col0, bq):
    S = proj.shape[0]
    bq = min(bq, S)
    nq = S // bq
    _, trev = _tri_matrices(bq)

    def body(q_ref, kv_hbm, trev_ref, o_ref, st_ref, jmin_ref, acc_a, acc_b, qa, qb, rs, kbuf, vbuf, sems):
        p, i = pl.program_id(0), pl.program_id(1)
        fetch, kbuf, vbuf = _kv_fetcher(kv_hbm, kbuf, vbuf, sems, 2, col0, bq, p, i, nq, _first_two_down)
        is_a = lax.broadcasted_iota(jnp.int32, (bq, LANES), 1) < HEAD_DIM
        acc_a[...] = jnp.zeros_like(acc_a)
        acc_b[...] = jnp.zeros_like(acc_b)
        rs[...] = jnp.zeros_like(rs)
        qa[...], qb[...] = _masked_pair(q_ref[...], is_a, SCALE)

        def tiles(blocks):
            hs, qs, accs, trev_m = (0, 1), (qa, qb), (acc_a, acc_b), trev_ref[...]
            kv = [(kbuf[s], vbuf[s]) for s, _ in blocks]
            bh = [(b, h) for b in range(len(blocks)) for h in hs]
            tri = lax.broadcasted_iota(jnp.int32, (bq, bq), 0) > lax.broadcasted_iota(jnp.int32, (bq, bq), 1)
            z = {(b, h): _dot(qs[h][...], kv[b][0], _NT) for b, h in bh}
            lk = {(b, h): -_softplus(z[b, h]) for b, h in bh}
            lk = {(b, h): jnp.where(tri, lk[b, h], 0.0) if blocks[b][1] else lk[b, h] for b, h in bh}
            suf = {(b, h): _split_dot(lk[b, h], trev_m) for b, h in bh}
            tot = {(b, h): jnp.sum(lk[b, h], axis=1, keepdims=True) for b, h in bh}
            right = {}
            for h in hs:
                r = rs[2 * h] + rs[2 * h + 1]
                for b in range(len(blocks)):
                    right[b, h] = r
                    r = r + tot[b, h]
            w = {(b, h): jnp.exp(z[b, h] + suf[b, h] + right[b, h]) for b, h in bh}
            w = {(b, h): jnp.where(tri, w[b, h], 0.0) if blocks[b][1] else w[b, h] for b, h in bh}
            pv = {(b, h): _dot(w[b, h].astype(BF16), kv[b][1]) for b, h in bh}
            for h in hs:
                accs[h][...] += sum([pv[b, h] for b in range(1, len(blocks))], pv[0, h])
                hi, lo = rs[2 * h], rs[2 * h + 1]
                for b in range(len(blocks)):
                    hi, lo = _two_sum(hi, lo, tot[b, h])
                rs[2 * h], rs[2 * h + 1] = hi, lo

        def live():
            return (jnp.max(jnp.maximum(rs[0], rs[2])) > SB_STOP).astype(jnp.int32)

        for cp in fetch(i, 0):
            cp.wait()
        pl.when(i == 0)(functools.partial(tiles, [(0, True)]))

        @pl.when(i > 0)
        def _():
            for cp in fetch(i - 1, 1):
                cp.wait()
            tiles([(0, True), (1, False)])

        def step(carry):
            j, _ = carry
            slot = lax.rem(i - j, 2)
            for cp in fetch(j, slot):
                cp.start()
            for cp in fetch(j, slot):
                cp.wait()
            tiles([(slot, False)])
            return j - 1, live()

        j_end, _ = lax.while_loop(lambda c: jnp.logical_and(c[0] >= 0, c[1] > 0), step, (i - 2, live()))
        jmin_ref[p, i] = jnp.maximum(j_end + 1, 0)
        o_ref[...] = jnp.where(is_a, acc_a[...], acc_b[...])
        lane8 = lax.broadcasted_iota(jnp.int32, (bq, 8), 1)
        st = jnp.zeros((bq, 8), F32)
        for c, src in enumerate((0, 2, 1, 3)):
            st = jnp.where(lane8 == c, rs[src], st)
        st_ref[0] = st

    return pl.pallas_call(
        body, name="sb_fwd", grid=(N_PAIRS, nq),
        in_specs=[pl.BlockSpec((bq, LANES), lambda p, i: (i, col0 + p)),
                  pl.BlockSpec(memory_space=pl.ANY),
                  pl.BlockSpec((bq, bq), lambda p, i: (0, 0))],
        out_specs=[pl.BlockSpec((bq, LANES), lambda p, i: (i, p)),
                   pl.BlockSpec((1, bq, 8), lambda p, i: (p, i, 0)),
                   pl.BlockSpec(memory_space=pltpu.SMEM)],
        out_shape=[jax.ShapeDtypeStruct((S, GROUP_W), F32), jax.ShapeDtypeStruct((N_PAIRS, S, 8), F32),
                   jax.ShapeDtypeStruct((N_PAIRS, nq), jnp.int32)],
        scratch_shapes=[pltpu.VMEM((bq, LANES), F32), pltpu.VMEM((bq, LANES), F32),
                        pltpu.VMEM((bq, LANES), BF16), pltpu.VMEM((bq, LANES), BF16),
                        pltpu.VMEM((4, bq, 1), F32),
                        pltpu.VMEM((4, bq, LANES), BF16), pltpu.VMEM((4, bq, LANES), BF16),
                        pltpu.SemaphoreType.DMA((2, 4))],
    )(proj, proj, trev)


def _sb_bwd(proj, col0, do, st, jmin, bq):
    S = proj.shape[0]
    bq = min(bq, S)
    nq = S // bq
    tfwd, trev = _tri_matrices(bq)

    def body(jmin_ref, q_ref, kv_hbm, do_ref, st_ref, tfwd_ref, trev_ref,
             dq_ref, dk_out, dv_out, dq_a, dq_b, qa, qb, doa, dob, rs, kbuf, vbuf, sems, dk_ref, dv_ref):
        p, i = pl.program_id(0), pl.program_id(1)
        j0 = jmin_ref[p, i]
        first_two = _first_two_up(lambda pair, blk: jmin_ref[pair, blk])
        fetch, kbuf, vbuf = _kv_fetcher(kv_hbm, kbuf, vbuf, sems, KV_SLOTS, col0, bq, p, i, nq, first_two)
        is_a = lax.broadcasted_iota(jnp.int32, (bq, LANES), 1) < HEAD_DIM

        @pl.when(i == 0)
        def _():
            dk_ref[...] = jnp.zeros_like(dk_ref)
            dv_ref[...] = jnp.zeros_like(dv_ref)

        dq_a[...] = jnp.zeros_like(dq_a)
        dq_b[...] = jnp.zeros_like(dq_b)
        rs[...] = jnp.zeros_like(rs)
        st_v = st_ref[0]
        for h in range(2):
            rs[6 + 2 * h], rs[7 + 2 * h] = _col(st_v, h), _col(st_v, 2 + h)
        qa[...], qb[...] = _masked_pair(q_ref[...], is_a, SCALE)
        doa[...], dob[...] = _masked_pair(do_ref[...], is_a)

        def tiles(blocks):
            hs, qs, dos, dqs = (0, 1), (qa, qb), (doa, dob), (dq_a, dq_b)
            tfwd_m, trev_m = tfwd_ref[...], trev_ref[...]
            kv = [(kbuf[s], vbuf[s]) for _, s, _ in blocks]
            nb = len(blocks)
            bh = [(b, h) for b in range(nb) for h in hs]
            tri = lax.broadcasted_iota(jnp.int32, (bq, bq), 0) > lax.broadcasted_iota(jnp.int32, (bq, bq), 1)

            def mask(x, b):
                return jnp.where(tri, x, 0.0) if blocks[b][2] else x

            z = {(b, h): _dot(qs[h][...], kv[b][0], _NT) for b, h in bh}
            dw = {(b, h): _dot(dos[h][...], kv[b][1], _NT) for b, h in bh}
            lk = {(b, h): mask(-_softplus(z[b, h]), b) for b, h in bh}
            suf = {(b, h): _split_dot(lk[b, h], trev_m) for b, h in bh}
            tot = {(b, h): jnp.sum(lk[b, h], axis=1, keepdims=True) for b, h in bh}
            pre = {}
            for h in hs:
                run = (rs[3 * h], rs[3 * h + 1])
                for b in range(nb):
                    run = _two_sum(run[0], run[1], tot[b, h])
                    pre[b, h] = run
            right = {(b, h): (rs[6 + 2 * h] - pre[b, h][0]) + (rs[7 + 2 * h] - pre[b, h][1]) for b, h in bh}
            w = {(b, h): mask(jnp.exp(z[b, h] + suf[b, h] + right[b, h]), b) for b, h in bh}
            g = {(b, h): dw[b, h] * w[b, h] for b, h in bh}
            gpre = {(b, h): _split_dot(g[b, h], tfwd_m) for b, h in bh}
            gtot = {(b, h): jnp.sum(g[b, h], axis=1, keepdims=True) for b, h in bh}
            gleft = {}
            for h in hs:
                run = rs[3 * h + 2]
                for b in range(nb):
                    gleft[b, h] = run
                    run = run + gtot[b, h]
                gleft[nb, h] = run
            dz = {(b, h): mask(g[b, h] - jnp.exp(z[b, h] + lk[b, h]) * (gpre[b, h] + gleft[b, h]), b) for b, h in bh}
            dzb = {(b, h): dz[b, h].astype(BF16) for b, h in bh}
            wb = {(b, h): w[b, h].astype(BF16) for b, h in bh}
            dqc = {(b, h): _dot(dzb[b, h], kv[b][0]) for b, h in bh}
            dkc = {(b, h): _dot(dzb[b, h], qs[h][...], _TN) for b, h in bh}
            dvc = {(b, h): _dot(wb[b, h], dos[h][...], _TN) for b, h in bh}
            for h in hs:
                rs[3 * h], rs[3 * h + 1] = pre[nb - 1, h]
                rs[3 * h + 2] = gleft[nb, h]
                dqs[h][...] += sum([dqc[b, h] for b in range(1, nb)], dqc[0, h])
            for b, (j, _, _) in enumerate(blocks):
                rows = pl.ds(pl.multiple_of(j * bq, bq), bq)
                dk_ref[rows, :] += dkc[b, 0] + dkc[b, 1]
                dv_ref[rows, :] += dvc[b, 0] + dvc[b, 1]

        def single(j, slot, masked):
            tiles([(j, slot, masked)])

        def wait(j):
            slot = lax.rem(j - j0, KV_SLOTS)
            for cp in fetch(j, slot):
                cp.wait()
            return slot

        _walk_up(fetch, j0, i, i, single, stop=jnp.maximum(i - 1, j0))

        @pl.when(j0 < i)
        def _():
            tiles([(i - 1, wait(i - 1), False), (i, wait(i), True)])

        @pl.when(j0 == i)
        def _():
            tiles([(i, wait(i), True)])

        dq_ref[...] = (jnp.where(is_a, dq_a[...], dq_b[...]) * SCALE).astype(BF16)

        @pl.when(i == nq - 1)
        def _():
            dk_out[...] = dk_ref[...].astype(BF16)
            dv_out[...] = dv_ref[...].astype(BF16)

    grid_spec = pltpu.PrefetchScalarGridSpec(
        num_scalar_prefetch=1, grid=(N_PAIRS, nq),
        in_specs=[pl.BlockSpec((bq, LANES), lambda p, i, jm: (i, col0 + p)),
                  pl.BlockSpec(memory_space=pl.ANY),
                  pl.BlockSpec((bq, LANES), lambda p, i, jm: (i, p)),
                  pl.BlockSpec((1, bq, 8), lambda p, i, jm: (p, i, 0)),
                  pl.BlockSpec((bq, bq), lambda p, i, jm: (0, 0)),
                  pl.BlockSpec((bq, bq), lambda p, i, jm: (0, 0))],
        out_specs=[pl.BlockSpec((bq, LANES), lambda p, i, jm: (i, p)),
                   pl.BlockSpec((S, LANES), lambda p, i, jm: (0, p)),
                   pl.BlockSpec((S, LANES), lambda p, i, jm: (0, p))],
        scratch_shapes=[pltpu.VMEM((bq, LANES), F32), pltpu.VMEM((bq, LANES), F32)]
        + [pltpu.VMEM((bq, LANES), BF16)] * 4 + [pltpu.VMEM((10, bq, 1), F32)]
        + [pltpu.VMEM((2 * KV_SLOTS, bq, LANES), BF16)] * 2 + [pltpu.SemaphoreType.DMA((2, 2 * KV_SLOTS))]
        + [pltpu.VMEM((S, LANES), F32)] * 2)
    return pl.pallas_call(
        body, name="sb_bwd", grid_spec=grid_spec,
        out_shape=[jax.ShapeDtypeStruct((S, GROUP_W), BF16)] * 3,
        compiler_params=_params(VMEM_BIG),
    )(jmin, proj, proj, do, st, tfwd, trev)


def _walk_up(fetch, j0, diag, last, tile, stop=None):
    ahead = KV_SLOTS - 1
    stop = last + 1 if stop is None else stop

    def start(j):
        @pl.when(j <= last)
        def _():
            for cp in fetch(j, lax.rem(j - j0, KV_SLOTS)):
                cp.start()

    for d in range(2, ahead):
        start(j0 + d)

    def step(j, carry):
        slot = lax.rem(j - j0, KV_SLOTS)
        for cp in fetch(j, slot):
            cp.wait()
        start(j + ahead)
        pl.when(j >= diag)(functools.partial(tile, j, slot, True))
        pl.when(j < diag)(functools.partial(tile, j, slot, False))
        return carry

    lax.fori_loop(j0, stop, step, 0)


def _causal(bq, bk, i, j):
    row = lax.broadcasted_iota(jnp.int32, (bq, bk), 0)
    col = lax.broadcasted_iota(jnp.int32, (bq, bk), 1)
    return col - row <= i * bq - j * bk


def _by_heads(j, first_a, first_b, heads):
    on_a, on_b = j >= first_a, j >= first_b
    pl.when(jnp.logical_and(on_a, on_b))(functools.partial(heads, (0, 1)))
    pl.when(jnp.logical_and(on_a, jnp.logical_not(on_b)))(functools.partial(heads, (0,)))
    pl.when(jnp.logical_and(on_b, jnp.logical_not(on_a)))(functools.partial(heads, (1,)))


def _fox_row_norms(proj, col0, tm):
    S = proj.shape[0]
    tm = min(tm, S)
    head_of = np.arange(GROUP_W) // HEAD_DIM
    he_t = jnp.asarray((np.arange(2 * N_PAIRS)[:, None] == head_of[None, :]).astype(np.float32), BF16)

    def body(q_ref, k_ref, he_ref, qn_ref, kn_ref, d_ref):
        q, k, he = q_ref[...].astype(F32), k_ref[...].astype(F32), he_ref[...]

        def head_sums_t(x):
            hi = x.astype(BF16)
            lo = (x - hi.astype(F32)).astype(BF16)
            return _dot(he, hi, _NT) + _dot(he, lo, _NT)

        qn_ref[...] = jnp.sqrt(head_sums_t(q * q))
        kn_ref[...] = jnp.sqrt(head_sums_t(k * k))
        d_ref[...] = SCALE * head_sums_t(q * k)

    wide = GROUP_W // LANES
    return pl.pallas_call(
        body, name="fox_row_norms", grid=(S // tm,),
        in_specs=[pl.BlockSpec((tm, GROUP_W), lambda i: (i, col0 // wide)),
                  pl.BlockSpec((tm, GROUP_W), lambda i: (i, (col0 + 4) // wide)),
                  pl.BlockSpec((2 * N_PAIRS, GROUP_W), lambda i: (0, 0))],
        out_specs=[pl.BlockSpec((2 * N_PAIRS, tm), lambda i: (0, i))] * 3,
        out_shape=[jax.ShapeDtypeStruct((2 * N_PAIRS, S), F32)] * 3)(proj, proj, he_t)


def _fox_start_blocks(qn, kn, d, c, bq, bk):
    nh, S = c.shape
    nq, nk = S // bq, S // bk
    top = SCALE * qn * kn.max(axis=1, keepdims=True) - d + c
    top = top.reshape(nh, nq, bq).max(axis=2)
    c_last = c[:, bk - 1::bk]
    live = top[:, :, None] - c_last[:, None, :] >= -FOX_SKIP

    def first_block(lv):
        first = jnp.where(lv.any(axis=2), jnp.argmax(lv, axis=2), nk)
        return jnp.minimum(first, (bq // bk) * jnp.arange(nq)[None, :]).astype(jnp.int32)

    return jnp.concatenate([first_block(live.reshape(N_PAIRS, 2, nq, nk).any(axis=1)), first_block(live)], axis=0)


def _fox_fwd(proj, col0, c_col, c_row, jstart, bq, bk):
    S = proj.shape[0]
    nq, per = S // bq, bq // bk

    def body(js_ref, q_ref, kv_hbm, cc_ref, cr_ref, o_ref, st_ref, acc_a, acc_b, qa, qb, ml, kbuf, vbuf, sems):
        p, i = pl.program_id(0), pl.program_id(1)
        j0 = js_ref[p, i]
        first_two = _first_two_up(lambda pair, blk: js_ref[pair, blk], per)
        fetch, kbuf, vbuf = _kv_fetcher(kv_hbm, kbuf, vbuf, sems, KV_SLOTS, col0, bk, p, i, nq, first_two)
        is_a = lax.broadcasted_iota(jnp.int32, (bq, LANES), 1) < HEAD_DIM
        acc_a[...] = jnp.zeros_like(acc_a)
        acc_b[...] = jnp.zeros_like(acc_b)
        ml[0] = jnp.full((bq, 1), NEG_BIG, F32)
        ml[2] = jnp.full((bq, 1), NEG_BIG, F32)
        ml[1] = jnp.zeros((bq, 1), F32)
        ml[3] = jnp.zeros((bq, 1), F32)
        cc = cc_ref[0]
        ml[4], ml[5] = _col(cc, 0), _col(cc, 1)
        qa[...], qb[...] = _masked_pair(q_ref[...], is_a, SCALE)

        def tile(j, slot, masked):
            k, v = kbuf[slot], vbuf[slot]
            cols = pl.ds(pl.multiple_of(j * bk, bk), bk)
            if masked:
                tri = _causal(bq, bk, i, j)

            def heads(hs):
                qs, accs = (qa, qb), (acc_a, acc_b)
                s = {h: _dot(qs[h][...], k, _NT) - cr_ref[0, pl.ds(h, 1), cols] for h in hs}
                if masked:
                    s = {h: jnp.where(tri, s[h], NEG_BIG) for h in hs}
                top = {h: jnp.max(s[h], axis=1, keepdims=True) for h in hs}
                m_new = {h: jnp.maximum(ml[2 * h], top[h] + ml[4 + h]) for h in hs}
                a = {h: jnp.exp(ml[2 * h] - m_new[h]) for h in hs}
                pr = {h: jnp.exp(s[h] - (m_new[h] - ml[4 + h])) for h in hs}
                tot = {h: jnp.sum(pr[h], axis=1, keepdims=True) for h in hs}
                pv = {h: _dot(pr[h].astype(BF16), v) for h in hs}
                for h in hs:
                    ml[2 * h] = m_new[h]
                    ml[2 * h + 1] = a[h] * ml[2 * h + 1] + tot[h]
                    accs[h][...] = a[h] * accs[h][...] + pv[h]

            _by_heads(j, js_ref[N_PAIRS + 2 * p, i], js_ref[N_PAIRS + 2 * p + 1, i], heads)

        _walk_up(fetch, j0, per * i, per * i + per - 1, tile)
        o_ref[...] = jnp.where(is_a, acc_a[...] / ml[1], acc_b[...] / ml[3])
        lane8 = lax.broadcasted_iota(jnp.int32, (bq, 8), 1)
        st = jnp.where(lane8 == 0, ml[0] + jnp.log(ml[1]), 0.0)
        st_ref[0] = jnp.where(lane8 == 1, ml[2] + jnp.log(ml[3]), st)

    grid_spec = pltpu.PrefetchScalarGridSpec(
        num_scalar_prefetch=1, grid=(N_PAIRS, nq),
        in_specs=[pl.BlockSpec((bq, LANES), lambda p, i, js: (i, col0 + p)),
                  pl.BlockSpec(memory_space=pl.ANY),
                  pl.BlockSpec((1, bq, 8), lambda p, i, js: (p, i, 0)),
                  pl.BlockSpec((1, 8, S), lambda p, i, js: (p, 0, 0))],
        out_specs=[pl.BlockSpec((bq, LANES), lambda p, i, js: (i, p)),
                   pl.BlockSpec((1, bq, 8), lambda p, i, js: (p, i, 0))],
        scratch_shapes=[pltpu.VMEM((bq, LANES), F32), pltpu.VMEM((bq, LANES), F32),
                        pltpu.VMEM((bq, LANES), BF16), pltpu.VMEM((bq, LANES), BF16),
                        pltpu.VMEM((6, bq, 1), F32),
                        pltpu.VMEM((2 * KV_SLOTS, bk, LANES), BF16), pltpu.VMEM((2 * KV_SLOTS, bk, LANES), BF16),
                        pltpu.SemaphoreType.DMA((2, 2 * KV_SLOTS))])
    return pl.pallas_call(
        body, name="fox_fwd", grid_spec=grid_spec,
        out_shape=[jax.ShapeDtypeStruct((S, GROUP_W), F32), jax.ShapeDtypeStruct((N_PAIRS, S, 8), F32)],
    )(jstart, proj, proj, c_col, c_row)


def _fox_bwd(proj, col0, do, o, st, c_col, c_row, jstart, bq, bk):
    S = proj.shape[0]
    nq, per = S // bq, bq // bk

    def body(js_ref, q_ref, kv_hbm, do_ref, o_ref, st_ref, cc_ref, cr_ref,
             dq_ref, dk_out, dv_out, dc_ref, dq_a, dq_b, qa, qb, doa, dob, dd, kbuf, vbuf, sems, dk_ref, dv_ref):
        p, i = pl.program_id(0), pl.program_id(1)
        j0 = js_ref[p, i]
        first_two = _first_two_up(lambda pair, blk: js_ref[pair, blk], per)
        fetch, kbuf, vbuf = _kv_fetcher(kv_hbm, kbuf, vbuf, sems, KV_SLOTS, col0, bk, p, i, nq, first_two)
        is_a = lax.broadcasted_iota(jnp.int32, (bq, LANES), 1) < HEAD_DIM

        @pl.when(i == 0)
        def _():
            dk_ref[...] = jnp.zeros_like(dk_ref)
            dv_ref[...] = jnp.zeros_like(dv_ref)
            dc_ref[...] = jnp.zeros_like(dc_ref)

        dq_a[...] = jnp.zeros_like(dq_a)
        dq_b[...] = jnp.zeros_like(dq_b)
        qa[...], qb[...] = _masked_pair(q_ref[...], is_a, SCALE)
        dov = do_ref[...]
        doa[...], dob[...] = _masked_pair(dov, is_a)
        prod = dov * o_ref[...]
        dd[0] = jnp.sum(jnp.where(is_a, prod, 0.0), axis=1, keepdims=True)
        dd[1] = jnp.sum(jnp.where(is_a, 0.0, prod), axis=1, keepdims=True)
        dd[2] = jnp.zeros((bq, 1), F32)
        dd[3] = jnp.zeros((bq, 1), F32)
        cc, st_v = cc_ref[0], st_ref[0]
        dd[4], dd[5] = _col(cc, 0) - _col(st_v, 0), _col(cc, 1) - _col(st_v, 1)

        def tile(j, slot, masked):
            k, v = kbuf[slot], vbuf[slot]
            if masked:
                tri = _causal(bq, bk, i, j)
            cols = pl.ds(pl.multiple_of(j * bk, bk), bk)

            def heads(hs):
                qs, dos, dqs = (qa, qb), (doa, dob), (dq_a, dq_b)
                z = {h: _dot(qs[h][...], k, _NT) for h in hs}
                dp = {h: _dot(dos[h][...], v, _NT) for h in hs}
                pr = {h: jnp.exp(z[h] - cr_ref[0, pl.ds(h, 1), cols] + dd[4 + h]) for h in hs}
                if masked:
                    pr = {h: jnp.where(tri, pr[h], 0.0) for h in hs}
                ds = {h: pr[h] * (dp[h] - dd[h]) for h in hs}
                csum = {h: jnp.sum(ds[h], axis=0, keepdims=True) for h in hs}
                rsum = {h: jnp.sum(ds[h], axis=1, keepdims=True) for h in hs}
                dsb = {h: ds[h].astype(BF16) for h in hs}
                prb = {h: pr[h].astype(BF16) for h in hs}
                dqc = {h: _dot(dsb[h], k) for h in hs}
                dkc = [_dot(dsb[h], qs[h][...], _TN) for h in hs]
                dvc = [_dot(prb[h], dos[h][...], _TN) for h in hs]
                for h in hs:
                    dc_ref[0, pl.ds(h, 1), cols] -= csum[h]
                    dd[2 + h] += rsum[h]
                    dqs[h][...] += dqc[h]
                dk_ref[cols, :] += sum(dkc[1:], dkc[0])
                dv_ref[cols, :] += sum(dvc[1:], dvc[0])

            _by_heads(j, js_ref[N_PAIRS + 2 * p, i], js_ref[N_PAIRS + 2 * p + 1, i], heads)

        _walk_up(fetch, j0, per * i, per * i + per - 1, tile)
        dq_ref[...] = (jnp.where(is_a, dq_a[...], dq_b[...]) * SCALE).astype(BF16)
        eye = lax.broadcasted_iota(jnp.int32, (bq, bq), 0) == lax.broadcasted_iota(jnp.int32, (bq, bq), 1)
        own = pl.ds(pl.multiple_of(i * bq, bq), bq)
        for h in range(2):
            dc_ref[0, pl.ds(h, 1), own] += jnp.sum(jnp.where(eye, dd[2 + h], 0.0), axis=0, keepdims=True)

        @pl.when(i == nq - 1)
        def _():
            dk_out[...] = dk_ref[...].astype(BF16)
            dv_out[...] = dv_ref[...].astype(BF16)

    grid_spec = pltpu.PrefetchScalarGridSpec(
        num_scalar_prefetch=1, grid=(N_PAIRS, nq),
        in_specs=[pl.BlockSpec((bq, LANES), lambda p, i, js: (i, col0 + p)),
                  pl.BlockSpec(memory_space=pl.ANY),
                  pl.BlockSpec((bq, LANES), lambda p, i, js: (i, p)),
                  pl.BlockSpec((bq, LANES), lambda p, i, js: (i, p)),
                  pl.BlockSpec((1, bq, 8), lambda p, i, js: (p, i, 0)),
                  pl.BlockSpec((1, bq, 8), lambda p, i, js: (p, i, 0)),
                  pl.BlockSpec((1, 8, S), lambda p, i, js: (p, 0, 0))],
        out_specs=[pl.BlockSpec((bq, LANES), lambda p, i, js: (i, p)),
                   pl.BlockSpec((S, LANES), lambda p, i, js: (0, p)),
                   pl.BlockSpec((S, LANES), lambda p, i, js: (0, p)),
                   pl.BlockSpec((1, 8, S), lambda p, i, js: (p, 0, 0))],
        scratch_shapes=[pltpu.VMEM((bq, LANES), F32), pltpu.VMEM((bq, LANES), F32)]
        + [pltpu.VMEM((bq, LANES), BF16)] * 4 + [pltpu.VMEM((6, bq, 1), F32)]
        + [pltpu.VMEM((2 * KV_SLOTS, bk, LANES), BF16)] * 2 + [pltpu.SemaphoreType.DMA((2, 2 * KV_SLOTS))]
        + [pltpu.VMEM((S, LANES), F32)] * 2)
    return pl.pallas_call(
        body, name="fox_bwd", grid_spec=grid_spec,
        out_shape=[jax.ShapeDtypeStruct((S, GROUP_W), BF16)] * 3 + [jax.ShapeDtypeStruct((N_PAIRS, 8, S), F32)],
        compiler_params=_params(VMEM_BIG),
    )(jstart, proj, proj, do, o, st, c_col, c_row)


_HBM = pl.BlockSpec(memory_space=pltpu.HBM)


def _coords():
    return lax.axis_index("x"), lax.axis_index("y"), lax.axis_index("c")


def _gather_copies(ins, outs, send_sems, recv_sems, loc_sems):
    n = len(ins)
    x, y, c = _coords()
    mine = 2 * x + y
    chips = [(1 - x, y), (x, 1 - y), (1 - x, 1 - y)]

    def copy(w, r, slab, to):
        return pltpu.make_async_remote_copy(
            src_ref=ins[w], dst_ref=outs[w].at[slab], send_sem=send_sems.at[3 * w + r],
            recv_sem=recv_sems.at[3 * w + r], device_id=to, device_id_type=MESH)

    def own():
        local = [pltpu.make_async_copy(ins[w], outs[w].at[mine], loc_sems.at[w]) for w in range(n)]
        return local, [copy(w, r, mine, (cx, cy, c)) for w in range(n) for r, (cx, cy) in enumerate(chips)]

    def start():
        local, sends = own()
        for cp in local + sends:
            cp.start()

    def wait():
        local, sends = own()
        for w in range(n):
            for r, (cx, cy) in enumerate(chips):
                copy(w, r, 2 * cx + cy, (cx, cy, c)).wait_recv()
        for cp in sends:
            cp.wait_send()
        for cp in local:
            cp.wait()

    return start, wait


def _gather_shapes(shards):
    n = len(shards)
    return ([jax.ShapeDtypeStruct((4,) + s.shape, s.dtype) for s in shards],
            [pltpu.SemaphoreType.DMA((3 * n,)), pltpu.SemaphoreType.DMA((3 * n,)), pltpu.SemaphoreType.DMA((n,))])


def _allgather_chips(shards):
    n = len(shards)

    def body(*refs):
        start, wait = _gather_copies(refs[:n], refs[n:2 * n], *refs[2 * n:])
        start()
        wait()

    out_shape, sems = _gather_shapes(shards)
    return pl.pallas_call(body, name="allgather_weights", in_specs=[_HBM] * n, out_specs=[_HBM] * n,
                          out_shape=out_shape, scratch_shapes=sems)(*shards)


def _proj_gather(x, w, shards, tm, tn):
    (M, K), N, n = x.shape, w.shape[1], len(shards)
    tm = min(tm, M)
    gi, gj = M // tm, N // tn

    def body(a_ref, b_ref, *rest):
        o_ref = rest[n]
        start, wait = _gather_copies(rest[:n], rest[n + 1:2 * n + 1], *rest[2 * n + 1:])
        i, j = pl.program_id(0), pl.program_id(1)
        pl.when(jnp.logical_and(i == 0, j == 0))(start)
        o_ref[...] = _dot(a_ref[...].astype(BF16), b_ref[...]).astype(o_ref.dtype)
        pl.when(jnp.logical_and(i == gi - 1, j == gj - 1))(wait)

    out_shape, sems = _gather_shapes(shards)
    return pl.pallas_call(
        body, name="proj_gather", grid=(gi, gj),
        in_specs=[pl.BlockSpec((tm, K), lambda i, j: (i, 0)), pl.BlockSpec((K, tn), lambda i, j: (0, j))] + [_HBM] * n,
        out_specs=[pl.BlockSpec((tm, tn), lambda i, j: (i, j))] + [_HBM] * n,
        out_shape=[jax.ShapeDtypeStruct((M, N), BF16)] + out_shape, scratch_shapes=sems,
    )(x, w, *shards)


def _exchange(parts, per_chip):
    n = len(parts)
    half = [p.shape[1] // 2 for p in parts] if per_chip else None

    def body(*refs):
        ins, outs = refs[:n], refs[n:2 * n]
        send_sems, recv_sems, loc_sems = refs[2 * n:]
        x, y, c = _coords()
        me = 4 * x + 2 * y + c
        peers = [(x ^ fx, y ^ fy, c ^ fc) for fx in (0, 1) for fy in (0, 1) for fc in (0, 1)][1:]

        def src(w, dev):
            if not per_chip:
                return ins[w]
            return ins[w].at[2 * dev[0] + dev[1], pl.ds(pl.multiple_of(dev[2] * half[w], 16), half[w]), :]

        local = [pltpu.make_async_copy(src(w, (x, y, c)), outs[w].at[me], loc_sems.at[w]) for w in range(n)]
        for cp in local:
            cp.start()

        def copy(w, r, source, slab, to):
            return pltpu.make_async_remote_copy(
                src_ref=source, dst_ref=outs[w].at[slab], send_sem=send_sems.at[7 * w + r],
                recv_sem=recv_sems.at[7 * w + r], device_id=to, device_id_type=MESH)

        sends = [copy(w, r, src(w, dev), me, dev) for w in range(n) for r, dev in enumerate(peers)]
        for cp in sends:
            cp.start()
        for w in range(n):
            for r, dev in enumerate(peers):
                copy(w, r, src(w, dev), 4 * dev[0] + 2 * dev[1] + dev[2], dev).wait_recv()
        for cp in sends:
            cp.wait_send()
        for cp in local:
            cp.wait()

    return pl.pallas_call(
        body, name="exchange_per_chip" if per_chip else "exchange_all",
        in_specs=[_HBM] * n, out_specs=[_HBM] * n,
        out_shape=[jax.ShapeDtypeStruct((8, half[w], p.shape[2]) if per_chip else (8,) + p.shape, p.dtype)
                   for w, p in enumerate(parts)],
        scratch_shapes=[pltpu.SemaphoreType.DMA((7 * n,)), pltpu.SemaphoreType.DMA((7 * n,)),
                        pltpu.SemaphoreType.DMA((n,))],
    )(*parts)


def _sibling_swap(halves):
    n = len(halves)

    def body(*refs):
        ins, outs = refs[:n], refs[n:2 * n]
        send_sems, recv_sems, loc_sems = refs[2 * n:]
        x, y, c = _coords()

        def rows(w, core):
            rh = halves[w].shape[0]
            return outs[w].at[pl.ds(pl.multiple_of(core * rh, 8), rh), :]

        def copy(w, core):
            return pltpu.make_async_remote_copy(
                src_ref=ins[w], dst_ref=rows(w, core), send_sem=send_sems.at[w], recv_sem=recv_sems.at[w],
                device_id=(x, y, 1 - c), device_id_type=MESH)

        local = [pltpu.make_async_copy(ins[w], rows(w, c), loc_sems.at[w]) for w in range(n)]
        sends = [copy(w, c) for w in range(n)]
        for cp in local + sends:
            cp.start()
        for w in range(n):
            copy(w, 1 - c).wait_recv()
        for cp in sends:
            cp.wait_send()
        for cp in local:
            cp.wait()

    vmem = pl.BlockSpec(memory_space=pltpu.VMEM)
    return pl.pallas_call(
        body, name="sibling_swap", in_specs=[vmem] * n, out_specs=[vmem] * n,
        out_shape=[jax.ShapeDtypeStruct((2 * h.shape[0], h.shape[1]), h.dtype) for h in halves],
        scratch_shapes=[pltpu.SemaphoreType.DMA((n,)), pltpu.SemaphoreType.DMA((n,)), pltpu.SemaphoreType.DMA((n,))],
    )(*halves)


def _adamw(w, g, m, v):
    m = ADAM_B1 * m + (1.0 - ADAM_B1) * g
    v = ADAM_B2 * v + (1.0 - ADAM_B2) * (g * g)
    m_hat = m / (1.0 - ADAM_B1 ** ADAM_STEP)
    v_hat = v / (1.0 - ADAM_B2 ** ADAM_STEP)
    delta = -ADAM_LR * (m_hat / (jnp.sqrt(v_hat) + ADAM_EPS) + ADAM_WD * w)
    return delta, m, v


def _sum_parts(parts, name, tr):
    _, R, C = parts.shape
    assert R % tr == 0

    def body(p_ref, g_ref):
        g = p_ref[0].astype(F32)
        for d in range(1, 8):
            g = g + p_ref[d].astype(F32)
        g_ref[...] = g

    return pl.pallas_call(
        body, name=name, grid=(R // tr,),
        in_specs=[pl.BlockSpec((8, tr, C), lambda i: (0, i, 0))],
        out_specs=pl.BlockSpec((tr, C), lambda i: (i, 0)), out_shape=jax.ShapeDtypeStruct((R, C), F32),
    )(parts)


def _adamw_call(g, w, m, v, name, tr):
    R, C = w.shape
    assert R % tr == 0

    def body(g_ref, w_ref, m_ref, v_ref, d_ref, nm_ref, nv_ref):
        d_ref[...], nm_ref[...], nv_ref[...] = _adamw(w_ref[...], g_ref[...], m_ref[...], v_ref[...])

    tile = pl.BlockSpec((tr, C), lambda i: (i, 0))
    return pl.pallas_call(
        body, name=name, grid=(R // tr,), in_specs=[tile] * 4,
        out_specs=[tile] * 3, out_shape=[jax.ShapeDtypeStruct((R, C), F32)] * 3,
    )(g, w, m, v)


def _sum_adamw_small(parts, w, m, v):
    def body(p_ref, w_ref, m_ref, v_ref, g_ref, d_ref, nm_ref, nv_ref, loss_ref):
        g = p_ref[0]
        for d in range(1, 8):
            g = g + p_ref[d]
        g_ref[...] = g
        d_ref[...], nm_ref[...], nv_ref[...] = _adamw(w_ref[...], g, m_ref[...], v_ref[...])
        row = lax.broadcasted_iota(jnp.int32, g.shape, 0)
        per_row = jnp.sum(jnp.where(row == 6, g, 0.0), axis=1, keepdims=True)
        loss_ref[...] = jnp.zeros((8, LANES), F32) + jnp.sum(per_row, axis=0, keepdims=True)

    return pl.pallas_call(
        body, name="sum_adamw_small",
        out_shape=[jax.ShapeDtypeStruct((8, D_MODEL), F32)] * 4 + [jax.ShapeDtypeStruct((8, LANES), F32)],
    )(parts, w, m, v)


def _pack_small(ln1_g, ln1_b, ln2_g, ln2_b, g_sb, g_fox, b_f):
    row5 = jnp.pad(b_f.reshape(1, N_FOX), ((0, 0), (0, D_MODEL - N_FOX)))
    rows = [ln1_g.reshape(1, -1), ln1_b.reshape(1, -1), ln2_g.reshape(1, -1), ln2_b.reshape(1, -1),
            jnp.concatenate([g_sb.reshape(1, -1), g_fox.reshape(1, -1)], axis=1), row5,
            jnp.zeros((2, D_MODEL), F32)]
    return jnp.concatenate(rows, axis=0)


def _unpack_small(p):
    return {"ln1_g": p[0:1], "ln1_b": p[1:2], "ln2_g": p[2:3], "ln2_b": p[3:4], "g_sb": p[4:5, :GROUP_W],
            "g_fox": p[4:5, GROUP_W:], "b_f": p[5:6, :N_FOX]}


def kernel(x, w_in, b_f, g_sb, g_fox, w_out, ln1_g, ln1_b, ln2_g, ln2_b, w_gate_up, w_down, loss_target, m_w_in, m_b_f, m_g_sb, m_g_fox, m_w_out, m_ln1_g, m_ln1_b, m_ln2_g, m_ln2_b, m_w_gate_up, m_w_down, v_w_in, v_b_f, v_g_sb, v_g_fox, v_w_out, v_ln1_g, v_ln1_b, v_ln2_g, v_ln2_b, v_w_gate_up, v_w_down):
    S = x.shape[1]
    x2 = x.reshape(S, D_MODEL)
    tgt = loss_target.reshape(S, D_MODEL)
    TM = 1024
    TR = 512
    BQ = ATTN_BLOCK
    in_w = w_in.shape[2]
    gu_w = w_gate_up.shape[2]

    shards = [w_in[0].astype(BF16), w_out[0].astype(BF16), w_gate_up[0].astype(BF16), w_down[0].astype(BF16)]
    (wi_s,) = _allgather_chips(shards[:1])
    wi = wi_s.transpose(1, 0, 2).reshape(D_MODEL, 4 * in_w)
    w_sb, w_fx = wi[:, :QKV_W // 2], wi[:, QKV_W // 2:QKV_W]
    wqkv = wi[:, :QKV_W]
    wft = wi[:, QKV_W:].T
    proj, wo_s, wgu_s, wd_s = _proj_gather(x2, wqkv, shards[1:], TM, 512)
    wo = wo_s.reshape(D_MODEL, D_MODEL)
    wgu = wgu_s.transpose(1, 0, 2).reshape(D_MODEL, 2 * D_FF)
    wg, wu = wgu[:, :D_FF], wgu[:, D_FF:]
    wd = wd_s.reshape(D_FF, D_MODEL)
    g_row = jnp.concatenate([g_sb, g_fox], axis=1)
    hid = np.arange(D_MODEL) // HEAD_DIM
    he_np = (hid[:, None] == np.arange(LANES)[None, :]).astype(np.float32)
    he, het = jnp.asarray(he_np, BF16), jnp.asarray(he_np.T, BF16)

    lf = _fgate_fwd(x2, wft, b_f.reshape(N_FOX, 1), TM)
    c = _cumsum_fwd(lf)
    c_pair = c.reshape(N_PAIRS, 2, S)
    c_row = jnp.pad(c_pair, ((0, 0), (0, 6), (0, 0)))
    c_col = jnp.pad(c_pair.transpose(0, 2, 1), ((0, 0), (0, 0), (0, 6)))

    o_sb, st_sb, jmin_sb = _sb_fwd(proj, 0, BQ)
    jstart_fx = _fox_start_blocks(*_fox_row_norms(proj, 12, TR), c, BQ, BQ)
    o_fx, st_fx = _fox_fwd(proj, 12, c_col, c_row, jstart_fx, BQ, BQ)

    def attn_post(i, osb_ref, ofx_ref, g_ref, he_ref, het_ref, on_ref):
        o = jnp.concatenate([osb_ref[...], ofx_ref[...]], axis=1)
        ms = _head_sums(o * o, he_ref[...], het_ref[...]) * (1.0 / HEAD_DIM)
        on_ref[...] = (o * lax.rsqrt(ms + RMS_EPS) * g_ref[...]).astype(BF16)

    (on,) = _rowwise(attn_post, "attn_post", S, TR,
                     [(o_sb, "t"), (o_fx, "t"), (g_row, "f"), (he, "f"), (het, "f")],
                     [((S, D_MODEL), BF16, "t")])

    u1 = _matmul(on, wo, mode="nn", name="mix", tm=TM, tn=D_MODEL, tk=D_MODEL, outs=[F32],
                 extras=[(x2, (TM if S >= TM else S, D_MODEL), _tile_ij)],
                 epilogue=lambda acc, xv: (ALPHA * xv + acc,))

    def ln1_fwd(i, u_ref, g_ref, b_ref, h_ref):
        xh, _ = _ln_stats(u_ref[...])
        h_ref[...] = xh * g_ref[...] + b_ref[...]

    (h1,) = _rowwise(ln1_fwd, "ln1_fwd", S, TR, [(u1, "t"), (ln1_g, "f"), (ln1_b, "f")], [((S, D_MODEL), F32, "t")])

    tm_e = TM if S >= TM else S
    n_ff = D_FF // 256

    def gate_up_body(h_ref, wg_ref, wu_ref, g_ref, u_ref, a_ref):
        h = h_ref[...].astype(BF16)
        g, u = _dot(h, wg_ref[...]), _dot(h, wu_ref[...])
        g_ref[...] = g.astype(BF16)
        u_ref[...] = u.astype(BF16)
        a_ref[...] = (g / (1.0 + jnp.exp(-g)) * u).astype(BF16)

    ff_tile = pl.BlockSpec((tm_e, 256), lambda i, j: (i, j))
    gate, up, act = pl.pallas_call(
        gate_up_body, name="gate_up_act", grid=(S // tm_e, n_ff),
        in_specs=[pl.BlockSpec((tm_e, D_MODEL), lambda i, j: (i, 0)),
                  pl.BlockSpec((D_MODEL, 256), lambda i, j: (0, j)),
                  pl.BlockSpec((D_MODEL, 256), lambda i, j: (0, j + n_ff))],
        out_specs=[ff_tile] * 3, out_shape=[jax.ShapeDtypeStruct((S, D_FF), BF16)] * 3)(h1, wgu, wgu)

    u2 = _matmul(act, wd, mode="nn", name="ffn_down", tm=TM, tn=D_MODEL, tk=D_FF, outs=[F32],
                 extras=[(h1, (TM if S >= TM else S, D_MODEL), _tile_ij)],
                 epilogue=lambda acc, hv: (ALPHA * hv + acc,))

    def ln2_loss(i, u_ref, t_ref, g_ref, b_ref, du_ref, acc_ref):
        xh, r = _ln_stats(u_ref[...])
        g = g_ref[...]
        err = xh * g + b_ref[...] - t_ref[...]
        dy = err * (1.0 / D_MODEL)
        du_ref[...] = _ln_bwd(dy, xh, r, g)
        _acc_rows(i, acc_ref, {2: jnp.sum(dy * xh, axis=0, keepdims=True), 3: jnp.sum(dy, axis=0, keepdims=True),
                               6: jnp.sum(err * err, axis=0, keepdims=True) * (0.5 / D_MODEL)})

    du2, acc_ln2 = _rowwise(ln2_loss, "ln2_loss", S, TR, [(u2, "t"), (tgt, "t"), (ln2_g, "f"), (ln2_b, "f")],
                            [((S, D_MODEL), F32, "t"), ((8, D_MODEL), F32, "f")])

    d_wd = _matmul(act, du2, mode="tn", name="dw_down", tm=1408, tn=D_MODEL, tk=TM, outs=[BF16])

    def dgu_epilogue(da, g, u):
        g, u = g.astype(F32), u.astype(F32)
        s = 1.0 / (1.0 + jnp.exp(-g))
        return da * u * (s * (1.0 + g * (1.0 - s))), da * (g * s)

    dgate, dup = _matmul(du2, wd, mode="nt", name="d_act", tm=TM, tn=1408, tk=D_MODEL, outs=[BF16, BF16],
                         extras=[(gate, (tm_e, 1408), _tile_ij), (up, (tm_e, 1408), _tile_ij)],
                         epilogue=dgu_epilogue)
    d_wg = _matmul(h1, dgate, mode="tn", name="dw_gate", tm=D_MODEL, tn=1408, tk=TM, outs=[BF16])
    d_wu = _matmul(h1, dup, mode="tn", name="dw_up", tm=D_MODEL, tn=1408, tk=TM, outs=[BF16])
    dh1 = _matmul(dgate, wg, mode="nt", name="dh1_gate", tm=TM, tn=D_MODEL, tk=D_FF, outs=[F32],
                  extras=[(du2, (tm_e, D_MODEL), _tile_ij)], epilogue=lambda acc, e: (ALPHA * e + acc,))
    dh1 = _matmul(dup, wu, mode="nt", name="dh1_up", tm=TM, tn=D_MODEL, tk=D_FF, outs=[F32],
                  extras=[(dh1, (tm_e, D_MODEL), _tile_ij)], epilogue=lambda acc, e: (e + acc,))

    def ln1_bwd(i, dh_ref, u_ref, g_ref, du_ref, acc_ref):
        xh, r = _ln_stats(u_ref[...])
        dh = dh_ref[...]
        du_ref[...] = _ln_bwd(dh, xh, r, g_ref[...])
        _acc_rows(i, acc_ref, {0: jnp.sum(dh * xh, axis=0, keepdims=True), 1: jnp.sum(dh, axis=0, keepdims=True)})

    du1, acc_ln1 = _rowwise(ln1_bwd, "ln1_bwd", S, TR, [(dh1, "t"), (u1, "t"), (ln1_g, "f")],
                            [((S, D_MODEL), F32, "t"), ((8, D_MODEL), F32, "f")])
    d_wo = _matmul(on, du1, mode="tn", name="dw_out", tm=D_MODEL, tn=D_MODEL, tk=TM, outs=[BF16])
    don = _matmul(du1, wo, mode="nt", name="d_on", tm=TM, tn=D_MODEL, tk=D_MODEL, outs=[F32])

    def rms_bwd(i, don_ref, osb_ref, ofx_ref, g_ref, he_ref, het_ref, dosb_ref, dofx_ref, acc_ref):
        o = jnp.concatenate([osb_ref[...], ofx_ref[...]], axis=1)
        hev, hetv = he_ref[...], het_ref[...]
        r = lax.rsqrt(_head_sums(o * o, hev, hetv) * (1.0 / HEAD_DIM) + RMS_EPS)
        dn = don_ref[...]
        dg = dn * g_ref[...]
        do = r * dg - o * (r * r * r) * (_head_sums(dg * o, hev, hetv) * (1.0 / HEAD_DIM))
        dosb_ref[...] = do[:, :GROUP_W]
        dofx_ref[...] = do[:, GROUP_W:]
        _acc_rows(i, acc_ref, {4: jnp.sum(dn * o * r, axis=0, keepdims=True)})

    do_sb, do_fx, acc_rms = _rowwise(
        rms_bwd, "rms_bwd", S, TR, [(don, "t"), (o_sb, "t"), (o_fx, "t"), (g_row, "f"), (he, "f"), (het, "f")],
        [((S, GROUP_W), F32, "t"), ((S, GROUP_W), F32, "t"), ((8, D_MODEL), F32, "f")])

    dq_sb, dk_sb, dv_sb = _sb_bwd(proj, 0, do_sb, st_sb, jmin_sb, BQ)
    jstart_fx2 = jnp.minimum(jstart_fx[:, 0::2], jstart_fx[:, 1::2])
    dq_fx, dk_fx, dv_fx, dc = _fox_bwd(proj, 12, do_fx, o_fx, st_fx, c_col, c_row, jstart_fx2, 2 * BQ, BQ)
    dfl, dbf = _fgate_bwd(dc[:, :2, :].reshape(N_FOX, S), lf)
    dp_sb = jnp.concatenate([dq_sb, dk_sb, dv_sb], axis=1)
    dp_fx = jnp.concatenate([dq_fx, dk_fx, dv_fx], axis=1)

    d_wsb = _matmul(x2, dp_sb, mode="tn", name="dw_in_sb", tm=D_MODEL, tn=QKV_W // 2, tk=TM, outs=[BF16])
    d_wfx = _matmul(x2, dp_fx, mode="tn", name="dw_in_fx", tm=D_MODEL, tn=QKV_W // 2, tk=TM, outs=[BF16])
    d_wft = _matmul(dfl, x2, mode="nn", name="dw_in_f", tm=N_FOX, tn=D_MODEL, tk=TM, outs=[BF16])
    dx = _matmul(dp_sb, w_sb, mode="nt", name="dx_sb", tm=TM, tn=D_MODEL, tk=QKV_W // 2, outs=[F32],
                 extras=[(du1, (tm_e, D_MODEL), _tile_ij)], epilogue=lambda acc, e: (ALPHA * e + acc,))
    dx = _matmul(dp_fx, w_fx, mode="nt", name="dx_fx", tm=TM, tn=D_MODEL, tk=QKV_W // 2, outs=[F32],
                 extras=[(dx, (tm_e, D_MODEL), _tile_ij)], epilogue=lambda acc, e: (e + acc,))
    dx = _matmul(dfl, wft, mode="tn", name="dx_f", tm=TM, tn=D_MODEL, tk=N_FOX, outs=[F32],
                 extras=[(dx, (tm_e, D_MODEL), _tile_ij)], epilogue=lambda acc, e: (e + acc,))

    d_wi = jnp.concatenate([d_wsb, d_wfx, d_wft.T], axis=1)
    d_wgu = jnp.concatenate([d_wg, d_wu], axis=1)
    parts = [d_wi.reshape(D_MODEL, 4, in_w).transpose(1, 0, 2).astype(BF16),
             d_wo.reshape(4, D_MODEL // 4, D_MODEL).astype(BF16),
             d_wgu.reshape(D_MODEL, 4, gu_w).transpose(1, 0, 2).astype(BF16),
             d_wd.reshape(4, D_FF // 4, D_MODEL).astype(BF16)]
    got = _exchange(parts, True)
    big_names = ("w_in", "w_out", "w_gate_up", "w_down")
    halves = [_sum_parts(p, "sum_" + nm, tr) for nm, p, tr in zip(big_names, got, (256, 128, 128, 176))]
    grads = _sibling_swap(halves)
    big = {}
    for nm, g, w, m, v, tr in zip(big_names, grads, (w_in, w_out, w_gate_up, w_down),
                                  (m_w_in, m_w_out, m_w_gate_up, m_w_down),
                                  (v_w_in, v_w_out, v_w_gate_up, v_w_down), (256, 256, 256, 176)):
        big[nm] = [r[None] for r in [g] + list(_adamw_call(g, w[0], m[0], v[0], "adamw_" + nm, tr))]

    small = acc_ln2 + acc_ln1 + acc_rms
    small = small + jnp.pad(dbf.reshape(1, N_FOX), ((5, 2), (0, D_MODEL - N_FOX)))
    (small_all,) = _exchange([small], False)
    sw = _pack_small(ln1_g, ln1_b, ln2_g, ln2_b, g_sb, g_fox, b_f)
    sm = _pack_small(m_ln1_g, m_ln1_b, m_ln2_g, m_ln2_b, m_g_sb, m_g_fox, m_b_f)
    sv = _pack_small(v_ln1_g, v_ln1_b, v_ln2_g, v_ln2_b, v_g_sb, v_g_fox, v_b_f)
    sg, sd, snm, snv, loss_blk = _sum_adamw_small(small_all, sw, sm, sv)
    sg, sd, snm, snv = _unpack_small(sg), _unpack_small(sd), _unpack_small(snm), _unpack_small(snv)

    names = ["w_in", "b_f", "g_sb", "g_fox", "w_out", "ln1_g", "ln1_b", "ln2_g", "ln2_b", "w_gate_up", "w_down"]
    outs = [loss_blk[0, 0], dx.reshape(1, S, D_MODEL)]
    for k, table in enumerate((sg, sd, snm, snv)):
        outs += [big[n][k] if n in big else table[n] for n in names]
    return tuple(outs)
```

```python
import functools

import numpy as np
import jax
import jax.numpy as jnp
from jax import lax
from jax.experimental import pallas as pl
from jax.experimental.pallas import tpu as pltpu

F32 = jnp.float32
BF16 = jnp.bfloat16

D_MODEL = 1024
HEAD_DIM = 64
LANES = 128
N_PAIRS = 4
GROUP_W = 512
QKV_W = 3072
D_FF = 2816
N_FOX = 8
ALPHA = 2.0 ** 0.25
LN_EPS = 1e-5
RMS_EPS = 1e-6
SCALE = HEAD_DIM ** -0.5
NEG_BIG = -1e30
FOX_SKIP = 30.0
SB_STOP = -105.0
ADAM_LR, ADAM_B1, ADAM_B2, ADAM_EPS, ADAM_WD, ADAM_STEP = 0.001, 0.9, 0.999, 1e-08, 0.01, 10
KV_SLOTS = 4
SCAN_GROUP = 8
ATTN_BLOCK = 256
VMEM_BIG = 56 * 1024 * 1024
MESH = pl.DeviceIdType.MESH

_NN = (((1,), (0,)), ((), ()))
_NT = (((1,), (1,)), ((), ()))
_TN = (((0,), (0,)), ((), ()))


def _dot(a, b, dims=_NN):
    return lax.dot_general(a, b, dims, preferred_element_type=F32)


def _split_dot(x, t):
    hi = x.astype(BF16)
    lo = (x - hi.astype(F32)).astype(BF16)
    return _dot(hi, t) + _dot(lo, t)


def _softplus(z):
    return jnp.maximum(z, 0.0) + jnp.log1p(jnp.exp(-jnp.abs(z)))


def _sigmoid(x):
    return 0.5 * jnp.tanh(0.5 * x) + 0.5


def _col(v, h):
    lane = lax.broadcasted_iota(jnp.int32, v.shape, 1)
    return jnp.sum(jnp.where(lane == h, v, 0.0), axis=1, keepdims=True)


def _two_sum(hi, lo, b):
    s = hi + b
    bb = s - hi
    err = (hi - (s - bb)) + (b - bb)
    return s, lo + err


def _params(vmem=None):
    return pltpu.CompilerParams(vmem_limit_bytes=vmem) if vmem else None


def _matmul(a, b, *, mode, name, tm, tn, tk, outs, extras=(), epilogue=None, vmem=None, hosted=()):
    if mode == "nn":
        (M, K), (_, N) = a.shape, b.shape
    elif mode == "nt":
        (M, K), (N, _) = a.shape, b.shape
    else:
        (K, M), (_, N) = a.shape, b.shape
    tm, tn, tk = min(tm, M), min(tn, N), min(tk, K)
    assert M % tm == 0 and N % tn == 0 and K % tk == 0, (name, M, N, K, tm, tn, tk)
    nk = K // tk
    dims = {"nn": _NN, "nt": _NT, "tn": _TN}[mode]
    if mode == "tn":
        a_spec = pl.BlockSpec((tk, tm), lambda i, j, k: (k, i))
    else:
        a_spec = pl.BlockSpec((tm, tk), lambda i, j, k: (i, k))
    if mode == "nt":
        b_spec = pl.BlockSpec((tn, tk), lambda i, j, k: (j, k))
    else:
        b_spec = pl.BlockSpec((tk, tn), lambda i, j, k: (k, j))
    ex_specs = [pl.BlockSpec(bs, (lambda i, j, k, f=f: f(i, j))) for (_, bs, f) in extras]
    ne, no, nh = len(extras), len(outs), len(hosted)
    if epilogue is None:
        epilogue = lambda acc: (acc,)
    gi, gj = M // tm, N // tn
    host_shapes, host_sems = _exchange_shapes(hosted, True) if nh else ([], [])

    def body(a_ref, b_ref, *rest):
        ex_refs, host_ins = rest[:ne], rest[ne:ne + nh]
        out_refs, host_outs = rest[ne + nh:ne + nh + no], rest[ne + nh + no:ne + 2 * nh + no]
        acc = rest[ne + 2 * nh + no]
        i, j, k = pl.program_id(0), pl.program_id(1), pl.program_id(2)
        if nh:
            start, wait = _exchange_copies(host_ins, host_outs, *rest[ne + 2 * nh + no + 1:], True, hosted)
            pl.when(jnp.logical_and(jnp.logical_and(i == 0, j == 0), k == 0))(start)

        @pl.when(k == 0)
        def _():
            acc[...] = jnp.zeros_like(acc)

        acc[...] += _dot(a_ref[...].astype(BF16), b_ref[...].astype(BF16), dims)

        @pl.when(k == nk - 1)
        def _():
            res = epilogue(acc[...], *[e[...] for e in ex_refs])
            for r, o in zip(res, out_refs):
                o[...] = r.astype(o.dtype)

        if nh:
            pl.when(jnp.logical_and(jnp.logical_and(i == gi - 1, j == gj - 1), k == nk - 1))(wait)

    res = pl.pallas_call(
        body, name=name, grid=(gi, gj, nk),
        in_specs=[a_spec, b_spec] + ex_specs + [_HBM] * nh,
        out_specs=[pl.BlockSpec((tm, tn), lambda i, j, k: (i, j)) for _ in outs] + [_HBM] * nh,
        out_shape=[jax.ShapeDtypeStruct((M, N), d) for d in outs] + host_shapes,
        scratch_shapes=[pltpu.VMEM((tm, tn), F32)] + host_sems,
        compiler_params=_params(vmem),
    )(a, b, *[e[0] for e in extras], *hosted)
    return res[0] if no + nh == 1 else res


def _tile_ij(i, j):
    return (i, j)


def _rowwise(fn, name, rows, tm, ins, outs, vmem=None):
    tm = min(tm, rows)
    assert rows % tm == 0

    def spec(shape, kind):
        if kind == "t":
            return pl.BlockSpec((tm,) + tuple(shape[1:]), lambda i: (i,) + (0,) * (len(shape) - 1))
        return pl.BlockSpec(tuple(shape), lambda i: (0,) * len(shape))

    def body(*refs):
        fn(pl.program_id(0), *refs)

    return pl.pallas_call(
        body, name=name, grid=(rows // tm,),
        in_specs=[spec(a.shape, k) for a, k in ins],
        out_specs=[spec(s, k) for s, _, k in outs],
        out_shape=[jax.ShapeDtypeStruct(s, d) for s, d, _ in outs],
        compiler_params=_params(vmem),
    )(*[a for a, _ in ins])


def _ln_stats(u):
    mu = jnp.mean(u, axis=-1, keepdims=True)
    d = u - mu
    var = jnp.mean(d * d, axis=-1, keepdims=True)
    r = lax.rsqrt(var + LN_EPS)
    return d * r, r


def _ln_bwd(dh, xh, r, g):
    dxh = dh * g
    m1 = jnp.mean(dxh, axis=-1, keepdims=True)
    m2 = jnp.mean(dxh * xh, axis=-1, keepdims=True)
    return r * (dxh - m1 - xh * m2)


def _acc_rows(i, ref, rows):
    @pl.when(i == 0)
    def _():
        ref[...] = jnp.zeros_like(ref)
    for r, v in rows.items():
        ref[pl.ds(r, 1), :] += v


def _head_sums(v, he, het):
    return _split_dot(_split_dot(v, he), het)


def _fgate_fwd(x, wft, bf_col, tm):
    S = x.shape[0]
    tm = min(tm, S)

    def body(wft_ref, bf_ref, x_ref, lf_ref):
        f = _dot(wft_ref[...], x_ref[...].astype(BF16), _NT) + bf_ref[...]
        lf_ref[...] = -_softplus(-f)

    return pl.pallas_call(
        body, name="fgate_fwd", grid=(S // tm,),
        in_specs=[pl.BlockSpec((N_FOX, D_MODEL), lambda i: (0, 0)), pl.BlockSpec((N_FOX, 1), lambda i: (0, 0)),
                  pl.BlockSpec((tm, D_MODEL), lambda i: (i, 0))],
        out_specs=pl.BlockSpec((N_FOX, tm), lambda i: (0, i)),
        out_shape=jax.ShapeDtypeStruct((N_FOX, S), F32),
    )(wft, bf_col, x)


def _chunk_scan(v, reverse):
    lane = lax.broadcasted_iota(jnp.int32, v.shape, 1)
    sh = 1
    while sh < LANES:
        if reverse:
            v = v + jnp.where(lane < LANES - sh, pltpu.roll(v, LANES - sh, 1), 0.0)
        else:
            v = v + jnp.where(lane >= sh, pltpu.roll(v, sh, 1), 0.0)
        sh *= 2
    return v


def _cumsum_fwd(lf):
    n, S = lf.shape
    nc = S // LANES

    grp = min(SCAN_GROUP, nc)

    def body(lf_ref, c_ref):
        def step(gi, carry):
            sls = [pl.ds(pl.multiple_of((gi * grp + g) * LANES, LANES), LANES) for g in range(grp)]
            vs = [_chunk_scan(lf_ref[:, sl], False) for sl in sls]
            tots = [_col(v, LANES - 1) for v in vs]
            for sl, v, t in zip(sls, vs, tots):
                c_ref[:, sl] = v + carry
                carry = carry + t
            return carry
        lax.fori_loop(0, nc // grp, step, jnp.zeros((n, 1), F32))

    return pl.pallas_call(body, name="cumsum_fwd", out_shape=jax.ShapeDtypeStruct((n, S), F32))(lf)


def _fgate_bwd(dc, lf):
    n, S = dc.shape
    nc = S // LANES

    grp = min(SCAN_GROUP, nc)

    def body(dc_ref, lf_ref, dfl_ref, dbf_ref):
        def step(t, carry):
            car, tot = carry
            gi = nc // grp - 1 - t
            sls = [pl.ds(pl.multiple_of((gi * grp + g) * LANES, LANES), LANES) for g in range(grp)]
            vs = [_chunk_scan(dc_ref[:, sl], True) for sl in sls]
            firsts = [_col(v, 0) for v in vs]
            for sl, v, f in reversed(list(zip(sls, vs, firsts))):
                dfl = (v + car) * (1.0 - jnp.exp(lf_ref[:, sl]))
                dfl_ref[:, sl] = dfl
                tot = tot + jnp.sum(dfl, axis=1, keepdims=True)
                car = car + f
            return car, tot
        _, tot = lax.fori_loop(0, nc // grp, step, (jnp.zeros((n, 1), F32), jnp.zeros((n, 1), F32)))
        dbf_ref[...] = tot

    return pl.pallas_call(body, name="fgate_bwd",
                          out_shape=[jax.ShapeDtypeStruct((n, S), F32), jax.ShapeDtypeStruct((n, 1), F32)])(dc, lf)


def _tri_matrices(b):
    r = np.arange(b)
    tfwd = (r[:, None] <= r[None, :]).astype(np.float32)
    return jnp.asarray(tfwd, BF16), jnp.asarray(tfwd.T, BF16)


def _kv_copies(kv_hbm, kbuf, vbuf, sems, sem0, pair_col, bq, j, slot):
    rows = pl.ds(pl.multiple_of(j * bq, bq), bq)

    def cols(c):
        return pl.ds(pl.multiple_of((pair_col + c) * LANES, LANES), LANES)

    return (pltpu.make_async_copy(kv_hbm.at[rows, cols(4)], kbuf.at[slot], sems.at[0, sem0 + slot]),
            pltpu.make_async_copy(kv_hbm.at[rows, cols(8)], vbuf.at[slot], sems.at[1, sem0 + slot]))


def _first_two_up(first_block, per=1):
    def blocks(pair, blk):
        first = first_block(pair, blk)
        return first, first + 1, first + 1 <= per * blk + per - 1
    return blocks


def _first_two_down(pair, blk):
    return blk, blk - 1, blk > 0


def _start_two(fetch, pair, first, second, has_second, ahead):
    for cp in fetch(first, 0, pair, ahead):
        cp.start()

    @pl.when(has_second)
    def _():
        for cp in fetch(second, 1, pair, ahead):
            cp.start()


def _kv_fetcher(kv_hbm, kbuf, vbuf, sems, ns, col0, bq, p, i, nq, blocks):
    base = lax.rem(p * nq + i, 2) * ns
    own = (kbuf.at[pl.ds(base, ns)], vbuf.at[pl.ds(base, ns)])
    other = (kbuf.at[pl.ds(ns - base, ns)], vbuf.at[pl.ds(ns - base, ns)])

    def fetch(j, slot, pair=p, ahead=False):
        kb, vb = other if ahead else own
        return _kv_copies(kv_hbm, kb, vb, sems, ns - base if ahead else base, col0 + pair, bq, j, slot)

    pl.when(jnp.logical_and(p == 0, i == 0))(lambda: _start_two(fetch, p, *blocks(p, i), False))
    wrap = i == nq - 1

    @pl.when(jnp.logical_not(jnp.logical_and(wrap, p == N_PAIRS - 1)))
    def _():
        pair, blk = jnp.where(wrap, p + 1, p), jnp.where(wrap, 0, i + 1)
        _start_two(fetch, pair, *blocks(pair, blk), True)

    return fetch, own[0], own[1]


def _masked_pair(v, lane_is_a, scale=1.0):
    v = v.astype(F32) * scale
    return jnp.where(lane_is_a, v, 0.0).astype(BF16), jnp.where(lane_is_a, 0.0, v).astype(BF16)


def _sb_fwd(proj, col0, bq):
    S = proj.shape[0]
    bq = min(bq, S)
    nq = S // bq
    _, trev = _tri_matrices(bq)

    def body(q_ref, kv_hbm, trev_ref, o_ref, st_ref, jmin_ref, acc_a, acc_b, qa, qb, rs, kbuf, vbuf, sems):
        p, i = pl.program_id(0), pl.program_id(1)
        fetch, kbuf, vbuf = _kv_fetcher(kv_hbm, kbuf, vbuf, sems, 2, col0, bq, p, i, nq, _first_two_down)
        is_a = lax.broadcasted_iota(jnp.int32, (bq, LANES), 1) < HEAD_DIM
        acc_a[...] = jnp.zeros_like(acc_a)
        acc_b[...] = jnp.zeros_like(acc_b)
        rs[...] = jnp.zeros_like(rs)
        qa[...], qb[...] = _masked_pair(q_ref[...], is_a, SCALE)

        def tiles(blocks):
            hs, qs, accs, trev_m = (0, 1), (qa, qb), (acc_a, acc_b), trev_ref[...]
            kv = [(kbuf[s], vbuf[s]) for s, _ in blocks]
            bh = [(b, h) for b in range(len(blocks)) for h in hs]
            tri = lax.broadcasted_iota(jnp.int32, (bq, bq), 0) > lax.broadcasted_iota(jnp.int32, (bq, bq), 1)
            z = {(b, h): _dot(qs[h][...], kv[b][0], _NT) for b, h in bh}
            lk = {(b, h): -_softplus(z[b, h]) for b, h in bh}
            lk = {(b, h): jnp.where(tri, lk[b, h], 0.0) if blocks[b][1] else lk[b, h] for b, h in bh}
            suf = {(b, h): _split_dot(lk[b, h], trev_m) for b, h in bh}
            tot = {(b, h): jnp.sum(lk[b, h], axis=1, keepdims=True) for b, h in bh}
            right = {}
            for h in hs:
                r = rs[2 * h] + rs[2 * h + 1]
                for b in range(len(blocks)):
                    right[b, h] = r
                    r = r + tot[b, h]
            w = {(b, h): jnp.exp(z[b, h] + suf[b, h] + right[b, h]) for b, h in bh}
            w = {(b, h): jnp.where(tri, w[b, h], 0.0) if blocks[b][1] else w[b, h] for b, h in bh}
            pv = {(b, h): _dot(w[b, h].astype(BF16), kv[b][1]) for b, h in bh}
            for h in hs:
                accs[h][...] += sum([pv[b, h] for b in range(1, len(blocks))], pv[0, h])
                hi, lo = rs[2 * h], rs[2 * h + 1]
                for b in range(len(blocks)):
                    hi, lo = _two_sum(hi, lo, tot[b, h])
                rs[2 * h], rs[2 * h + 1] = hi, lo

        def live():
            return (jnp.max(jnp.maximum(rs[0], rs[2])) > SB_STOP).astype(jnp.int32)

        for cp in fetch(i, 0):
            cp.wait()
        pl.when(i == 0)(functools.partial(tiles, [(0, True)]))

        @pl.when(i > 0)
        def _():
            for cp in fetch(i - 1, 1):
                cp.wait()
            tiles([(0, True), (1, False)])

        def step(carry):
            j, _ = carry
            slot = lax.rem(i - j, 2)
            for cp in fetch(j, slot):
                cp.start()
            for cp in fetch(j, slot):
                cp.wait()
            tiles([(slot, False)])
            return j - 1, live()

        j_end, _ = lax.while_loop(lambda c: jnp.logical_and(c[0] >= 0, c[1] > 0), step, (i - 2, live()))
        jmin_ref[p, i] = jnp.maximum(j_end + 1, 0)
        o_ref[...] = jnp.where(is_a, acc_a[...], acc_b[...])
        lane8 = lax.broadcasted_iota(jnp.int32, (bq, 8), 1)
        st = jnp.zeros((bq, 8), F32)
        for c, src in enumerate((0, 2, 1, 3)):
            st = jnp.where(lane8 == c, rs[src], st)
        st_ref[0] = st

    return pl.pallas_call(
        body, name="sb_fwd", grid=(N_PAIRS, nq),
        in_specs=[pl.BlockSpec((bq, LANES), lambda p, i: (i, col0 + p)),
                  pl.BlockSpec(memory_space=pl.ANY),
                  pl.BlockSpec((bq, bq), lambda p, i: (0, 0))],
        out_specs=[pl.BlockSpec((bq, LANES), lambda p, i: (i, p)),
                   pl.BlockSpec((1, bq, 8), lambda p, i: (p, i, 0)),
                   pl.BlockSpec(memory_space=pltpu.SMEM)],
        out_shape=[jax.ShapeDtypeStruct((S, GROUP_W), F32), jax.ShapeDtypeStruct((N_PAIRS, S, 8), F32),
                   jax.ShapeDtypeStruct((N_PAIRS, nq), jnp.int32)],
        scratch_shapes=[pltpu.VMEM((bq, LANES), F32), pltpu.VMEM((bq, LANES), F32),
                        pltpu.VMEM((bq, LANES), BF16), pltpu.VMEM((bq, LANES), BF16),
                        pltpu.VMEM((4, bq, 1), F32),
                        pltpu.VMEM((4, bq, LANES), BF16), pltpu.VMEM((4, bq, LANES), BF16),
                        pltpu.SemaphoreType.DMA((2, 4))],
    )(proj, proj, trev)


def _sb_bwd(proj, col0, do, st, jmin, bq):
    S = proj.shape[0]
    bq = min(bq, S)
    nq = S // bq
    tfwd, trev = _tri_matrices(bq)

    def body(jmin_ref, q_ref, kv_hbm, do_ref, st_ref, tfwd_ref, trev_ref,
             dq_ref, dk_out, dv_out, dq_a, dq_b, qa, qb, doa, dob, rs, kbuf, vbuf, sems, dk_ref, dv_ref):
        p, i = pl.program_id(0), pl.program_id(1)
        j0 = jmin_ref[p, i]
        first_two = _first_two_up(lambda pair, blk: jmin_ref[pair, blk])
        fetch, kbuf, vbuf = _kv_fetcher(kv_hbm, kbuf, vbuf, sems, KV_SLOTS, col0, bq, p, i, nq, first_two)
        is_a = lax.broadcasted_iota(jnp.int32, (bq, LANES), 1) < HEAD_DIM

        @pl.when(i == 0)
        def _():
            dk_ref[...] = jnp.zeros_like(dk_ref)
            dv_ref[...] = jnp.zeros_like(dv_ref)

        dq_a[...] = jnp.zeros_like(dq_a)
        dq_b[...] = jnp.zeros_like(dq_b)
        rs[...] = jnp.zeros_like(rs)
        st_v = st_ref[0]
        for h in range(2):
            rs[6 + 2 * h], rs[7 + 2 * h] = _col(st_v, h), _col(st_v, 2 + h)
        qa[...], qb[...] = _masked_pair(q_ref[...], is_a, SCALE)
        doa[...], dob[...] = _masked_pair(do_ref[...], is_a)

        def tiles(blocks):
            hs, qs, dos, dqs = (0, 1), (qa, qb), (doa, dob), (dq_a, dq_b)
            tfwd_m, trev_m = tfwd_ref[...], trev_ref[...]
            kv = [(kbuf[s], vbuf[s]) for _, s, _ in blocks]
            nb = len(blocks)
            bh = [(b, h) for b in range(nb) for h in hs]
            tri = lax.broadcasted_iota(jnp.int32, (bq, bq), 0) > lax.broadcasted_iota(jnp.int32, (bq, bq), 1)

            def mask(x, b):
                return jnp.where(tri, x, 0.0) if blocks[b][2] else x

            z = {(b, h): _dot(qs[h][...], kv[b][0], _NT) for b, h in bh}
            dw = {(b, h): _dot(dos[h][...], kv[b][1], _NT) for b, h in bh}
            lk = {(b, h): mask(-_softplus(z[b, h]), b) for b, h in bh}
            suf = {(b, h): _split_dot(lk[b, h], trev_m) for b, h in bh}
            tot = {(b, h): jnp.sum(lk[b, h], axis=1, keepdims=True) for b, h in bh}
            pre = {}
            for h in hs:
                run = (rs[3 * h], rs[3 * h + 1])
                for b in range(nb):
                    run = _two_sum(run[0], run[1], tot[b, h])
                    pre[b, h] = run
            right = {(b, h): (rs[6 + 2 * h] - pre[b, h][0]) + (rs[7 + 2 * h] - pre[b, h][1]) for b, h in bh}
            w = {(b, h): mask(jnp.exp(z[b, h] + suf[b, h] + right[b, h]), b) for b, h in bh}
            g = {(b, h): dw[b, h] * w[b, h] for b, h in bh}
            gpre = {(b, h): _split_dot(g[b, h], tfwd_m) for b, h in bh}
            gtot = {(b, h): jnp.sum(g[b, h], axis=1, keepdims=True) for b, h in bh}
            gleft = {}
            for h in hs:
                run = rs[3 * h + 2]
                for b in range(nb):
                    gleft[b, h] = run
                    run = run + gtot[b, h]
                gleft[nb, h] = run
            dz = {(b, h): mask(g[b, h] - jnp.exp(z[b, h] + lk[b, h]) * (gpre[b, h] + gleft[b, h]), b) for b, h in bh}
            dzb = {(b, h): dz[b, h].astype(BF16) for b, h in bh}
            wb = {(b, h): w[b, h].astype(BF16) for b, h in bh}
            dqc = {(b, h): _dot(dzb[b, h], kv[b][0]) for b, h in bh}
            dkc = {(b, h): _dot(dzb[b, h], qs[h][...], _TN) for b, h in bh}
            dvc = {(b, h): _dot(wb[b, h], dos[h][...], _TN) for b, h in bh}
            for h in hs:
                rs[3 * h], rs[3 * h + 1] = pre[nb - 1, h]
                rs[3 * h + 2] = gleft[nb, h]
                dqs[h][...] += sum([dqc[b, h] for b in range(1, nb)], dqc[0, h])
            for b, (j, _, _) in enumerate(blocks):
                rows = pl.ds(pl.multiple_of(j * bq, bq), bq)
                dk_ref[rows, :] += dkc[b, 0] + dkc[b, 1]
                dv_ref[rows, :] += dvc[b, 0] + dvc[b, 1]

        def single(j, slot, masked):
            tiles([(j, slot, masked)])

        def wait(j):
            slot = lax.rem(j - j0, KV_SLOTS)
            for cp in fetch(j, slot):
                cp.wait()
            return slot

        _walk_up(fetch, j0, i, i, single, stop=jnp.maximum(i - 1, j0))

        @pl.when(j0 < i)
        def _():
            tiles([(i - 1, wait(i - 1), False), (i, wait(i), True)])

        @pl.when(j0 == i)
        def _():
            tiles([(i, wait(i), True)])

        dq_ref[...] = (jnp.where(is_a, dq_a[...], dq_b[...]) * SCALE).astype(BF16)

        @pl.when(i == nq - 1)
        def _():
            dk_out[...] = dk_ref[...].astype(BF16)
            dv_out[...] = dv_ref[...].astype(BF16)

    grid_spec = pltpu.PrefetchScalarGridSpec(
        num_scalar_prefetch=1, grid=(N_PAIRS, nq),
        in_specs=[pl.BlockSpec((bq, LANES), lambda p, i, jm: (i, col0 + p)),
                  pl.BlockSpec(memory_space=pl.ANY),
                  pl.BlockSpec((bq, LANES), lambda p, i, jm: (i, p)),
                  pl.BlockSpec((1, bq, 8), lambda p, i, jm: (p, i, 0)),
                  pl.BlockSpec((bq, bq), lambda p, i, jm: (0, 0)),
                  pl.BlockSpec((bq, bq), lambda p, i, jm: (0, 0))],
        out_specs=[pl.BlockSpec((bq, LANES), lambda p, i, jm: (i, p)),
                   pl.BlockSpec((S, LANES), lambda p, i, jm: (0, p)),
                   pl.BlockSpec((S, LANES), lambda p, i, jm: (0, p))],
        scratch_shapes=[pltpu.VMEM((bq, LANES), F32), pltpu.VMEM((bq, LANES), F32)]
        + [pltpu.VMEM((bq, LANES), BF16)] * 4 + [pltpu.VMEM((10, bq, 1), F32)]
        + [pltpu.VMEM((2 * KV_SLOTS, bq, LANES), BF16)] * 2 + [pltpu.SemaphoreType.DMA((2, 2 * KV_SLOTS))]
        + [pltpu.VMEM((S, LANES), F32)] * 2)
    return pl.pallas_call(
        body, name="sb_bwd", grid_spec=grid_spec,
        out_shape=[jax.ShapeDtypeStruct((S, GROUP_W), BF16)] * 3,
        compiler_params=_params(VMEM_BIG),
    )(jmin, proj, proj, do, st, tfwd, trev)


def _walk_up(fetch, j0, diag, last, tile, stop=None):
    ahead = KV_SLOTS - 1
    stop = last + 1 if stop is None else stop

    def start(j):
        @pl.when(j <= last)
        def _():
            for cp in fetch(j, lax.rem(j - j0, KV_SLOTS)):
                cp.start()

    for d in range(2, ahead):
        start(j0 + d)

    def step(j, carry):
        slot = lax.rem(j - j0, KV_SLOTS)
        for cp in fetch(j, slot):
            cp.wait()
        start(j + ahead)
        pl.when(j >= diag)(functools.partial(tile, j, slot, True))
        pl.when(j < diag)(functools.partial(tile, j, slot, False))
        return carry

    lax.fori_loop(j0, stop, step, 0)


def _causal(bq, bk, i, j):
    row = lax.broadcasted_iota(jnp.int32, (bq, bk), 0)
    col = lax.broadcasted_iota(jnp.int32, (bq, bk), 1)
    return col - row <= i * bq - j * bk


def _by_heads(j, first_a, first_b, heads):
    on_a, on_b = j >= first_a, j >= first_b
    pl.when(jnp.logical_and(on_a, on_b))(functools.partial(heads, (0, 1)))
    pl.when(jnp.logical_and(on_a, jnp.logical_not(on_b)))(functools.partial(heads, (0,)))
    pl.when(jnp.logical_and(on_b, jnp.logical_not(on_a)))(functools.partial(heads, (1,)))


def _fox_row_norms(proj, col0, tm):
    S = proj.shape[0]
    tm = min(tm, S)
    head_of = np.arange(GROUP_W) // HEAD_DIM
    he_t = jnp.asarray((np.arange(2 * N_PAIRS)[:, None] == head_of[None, :]).astype(np.float32), BF16)

    def body(q_ref, k_ref, he_ref, qn_ref, kn_ref, d_ref):
        q, k, he = q_ref[...].astype(F32), k_ref[...].astype(F32), he_ref[...]

        def head_sums_t(x):
            hi = x.astype(BF16)
            lo = (x - hi.astype(F32)).astype(BF16)
            return _dot(he, hi, _NT) + _dot(he, lo, _NT)

        qn_ref[...] = jnp.sqrt(head_sums_t(q * q))
        kn_ref[...] = jnp.sqrt(head_sums_t(k * k))
        d_ref[...] = SCALE * head_sums_t(q * k)

    wide = GROUP_W // LANES
    return pl.pallas_call(
        body, name="fox_row_norms", grid=(S // tm,),
        in_specs=[pl.BlockSpec((tm, GROUP_W), lambda i: (i, col0 // wide)),
                  pl.BlockSpec((tm, GROUP_W), lambda i: (i, (col0 + 4) // wide)),
                  pl.BlockSpec((2 * N_PAIRS, GROUP_W), lambda i: (0, 0))],
        out_specs=[pl.BlockSpec((2 * N_PAIRS, tm), lambda i: (0, i))] * 3,
        out_shape=[jax.ShapeDtypeStruct((2 * N_PAIRS, S), F32)] * 3)(proj, proj, he_t)


def _fox_start_blocks(qn, kn, d, c, bq, bk):
    nh, S = c.shape
    nq, nk = S // bq, S // bk
    top = SCALE * qn * kn.max(axis=1, keepdims=True) - d + c
    top = top.reshape(nh, nq, bq).max(axis=2)
    c_last = c[:, bk - 1::bk]
    live = top[:, :, None] - c_last[:, None, :] >= -FOX_SKIP

    def first_block(lv):
        first = jnp.where(lv.any(axis=2), jnp.argmax(lv, axis=2), nk)
        return jnp.minimum(first, (bq // bk) * jnp.arange(nq)[None, :]).astype(jnp.int32)

    return jnp.concatenate([first_block(live.reshape(N_PAIRS, 2, nq, nk).any(axis=1)), first_block(live)], axis=0)


def _fox_fwd(proj, col0, c_col, c_row, jstart, bq, bk):
    S = proj.shape[0]
    nq, per = S // bq, bq // bk

    def body(js_ref, q_ref, kv_hbm, cc_ref, cr_ref, o_ref, st_ref, acc_a, acc_b, qa, qb, ml, kbuf, vbuf, sems):
        p, i = pl.program_id(0), pl.program_id(1)
        j0 = js_ref[p, i]
        first_two = _first_two_up(lambda pair, blk: js_ref[pair, blk], per)
        fetch, kbuf, vbuf = _kv_fetcher(kv_hbm, kbuf, vbuf, sems, KV_SLOTS, col0, bk, p, i, nq, first_two)
        is_a = lax.broadcasted_iota(jnp.int32, (bq, LANES), 1) < HEAD_DIM
        acc_a[...] = jnp.zeros_like(acc_a)
        acc_b[...] = jnp.zeros_like(acc_b)
        ml[0] = jnp.full((bq, 1), NEG_BIG, F32)
        ml[2] = jnp.full((bq, 1), NEG_BIG, F32)
        ml[1] = jnp.zeros((bq, 1), F32)
        ml[3] = jnp.zeros((bq, 1), F32)
        cc = cc_ref[0]
        ml[4], ml[5] = _col(cc, 0), _col(cc, 1)
        qa[...], qb[...] = _masked_pair(q_ref[...], is_a, SCALE)

        def tile(j, slot, masked):
            k, v = kbuf[slot], vbuf[slot]
            cols = pl.ds(pl.multiple_of(j * bk, bk), bk)
            if masked:
                tri = _causal(bq, bk, i, j)

            def heads(hs):
                qs, accs = (qa, qb), (acc_a, acc_b)
                s = {h: _dot(qs[h][...], k, _NT) - cr_ref[0, pl.ds(h, 1), cols] for h in hs}
                if masked:
                    s = {h: jnp.where(tri, s[h], NEG_BIG) for h in hs}
                top = {h: jnp.max(s[h], axis=1, keepdims=True) for h in hs}
                m_new = {h: jnp.maximum(ml[2 * h], top[h] + ml[4 + h]) for h in hs}
                a = {h: jnp.exp(ml[2 * h] - m_new[h]) for h in hs}
                pr = {h: jnp.exp(s[h] - (m_new[h] - ml[4 + h])) for h in hs}
                tot = {h: jnp.sum(pr[h], axis=1, keepdims=True) for h in hs}
                pv = {h: _dot(pr[h].astype(BF16), v) for h in hs}
                for h in hs:
                    ml[2 * h] = m_new[h]
                    ml[2 * h + 1] = a[h] * ml[2 * h + 1] + tot[h]
                    accs[h][...] = a[h] * accs[h][...] + pv[h]

            _by_heads(j, js_ref[N_PAIRS + 2 * p, i], js_ref[N_PAIRS + 2 * p + 1, i], heads)

        _walk_up(fetch, j0, per * i, per * i + per - 1, tile)
        o_ref[...] = jnp.where(is_a, acc_a[...] / ml[1], acc_b[...] / ml[3])
        lane8 = lax.broadcasted_iota(jnp.int32, (bq, 8), 1)
        st = jnp.where(lane8 == 0, ml[0] + jnp.log(ml[1]), 0.0)
        st_ref[0] = jnp.where(lane8 == 1, ml[2] + jnp.log(ml[3]), st)

    grid_spec = pltpu.PrefetchScalarGridSpec(
        num_scalar_prefetch=1, grid=(N_PAIRS, nq),
        in_specs=[pl.BlockSpec((bq, LANES), lambda p, i, js: (i, col0 + p)),
                  pl.BlockSpec(memory_space=pl.ANY),
                  pl.BlockSpec((1, bq, 8), lambda p, i, js: (p, i, 0)),
                  pl.BlockSpec((1, 8, S), lambda p, i, js: (p, 0, 0))],
        out_specs=[pl.BlockSpec((bq, LANES), lambda p, i, js: (i, p)),
                   pl.BlockSpec((1, bq, 8), lambda p, i, js: (p, i, 0))],
        scratch_shapes=[pltpu.VMEM((bq, LANES), F32), pltpu.VMEM((bq, LANES), F32),
                        pltpu.VMEM((bq, LANES), BF16), pltpu.VMEM((bq, LANES), BF16),
                        pltpu.VMEM((6, bq, 1), F32),
                        pltpu.VMEM((2 * KV_SLOTS, bk, LANES), BF16), pltpu.VMEM((2 * KV_SLOTS, bk, LANES), BF16),
                        pltpu.SemaphoreType.DMA((2, 2 * KV_SLOTS))])
    return pl.pallas_call(
        body, name="fox_fwd", grid_spec=grid_spec,
        out_shape=[jax.ShapeDtypeStruct((S, GROUP_W), F32), jax.ShapeDtypeStruct((N_PAIRS, S, 8), F32)],
    )(jstart, proj, proj, c_col, c_row)


def _fox_bwd(proj, col0, do, o, st, c_col, c_row, jstart, bq, bk):
    S = proj.shape[0]
    nq, per = S // bq, bq // bk

    def body(js_ref, q_ref, kv_hbm, do_ref, o_ref, st_ref, cc_ref, cr_ref,
             dq_ref, dk_out, dv_out, dc_ref, dq_a, dq_b, qa, qb, doa, dob, dd, kbuf, vbuf, sems, dk_ref, dv_ref):
        p, i = pl.program_id(0), pl.program_id(1)
        j0 = js_ref[p, i]
        first_two = _first_two_up(lambda pair, blk: js_ref[pair, blk], per)
        fetch, kbuf, vbuf = _kv_fetcher(kv_hbm, kbuf, vbuf, sems, KV_SLOTS, col0, bk, p, i, nq, first_two)
        is_a = lax.broadcasted_iota(jnp.int32, (bq, LANES), 1) < HEAD_DIM

        @pl.when(i == 0)
        def _():
            dk_ref[...] = jnp.zeros_like(dk_ref)
            dv_ref[...] = jnp.zeros_like(dv_ref)
            dc_ref[...] = jnp.zeros_like(dc_ref)

        dq_a[...] = jnp.zeros_like(dq_a)
        dq_b[...] = jnp.zeros_like(dq_b)
        qa[...], qb[...] = _masked_pair(q_ref[...], is_a, SCALE)
        dov = do_ref[...]
        doa[...], dob[...] = _masked_pair(dov, is_a)
        prod = dov * o_ref[...]
        dd[0] = jnp.sum(jnp.where(is_a, prod, 0.0), axis=1, keepdims=True)
        dd[1] = jnp.sum(jnp.where(is_a, 0.0, prod), axis=1, keepdims=True)
        dd[2] = jnp.zeros((bq, 1), F32)
        dd[3] = jnp.zeros((bq, 1), F32)
        cc, st_v = cc_ref[0], st_ref[0]
        dd[4], dd[5] = _col(cc, 0) - _col(st_v, 0), _col(cc, 1) - _col(st_v, 1)

        def tile(j, slot, masked):
            k, v = kbuf[slot], vbuf[slot]
            if masked:
                tri = _causal(bq, bk, i, j)
            cols = pl.ds(pl.multiple_of(j * bk, bk), bk)

            def heads(hs):
                qs, dos, dqs = (qa, qb), (doa, dob), (dq_a, dq_b)
                z = {h: _dot(qs[h][...], k, _NT) for h in hs}
                dp = {h: _dot(dos[h][...], v, _NT) for h in hs}
                pr = {h: jnp.exp(z[h] - cr_ref[0, pl.ds(h, 1), cols] + dd[4 + h]) for h in hs}
                if masked:
                    pr = {h: jnp.where(tri, pr[h], 0.0) for h in hs}
                ds = {h: pr[h] * (dp[h] - dd[h]) for h in hs}
                csum = {h: jnp.sum(ds[h], axis=0, keepdims=True) for h in hs}
                rsum = {h: jnp.sum(ds[h], axis=1, keepdims=True) for h in hs}
                dsb = {h: ds[h].astype(BF16) for h in hs}
                prb = {h: pr[h].astype(BF16) for h in hs}
                dqc = {h: _dot(dsb[h], k) for h in hs}
                dkc = [_dot(dsb[h], qs[h][...], _TN) for h in hs]
                dvc = [_dot(prb[h], dos[h][...], _TN) for h in hs]
                for h in hs:
                    dc_ref[0, pl.ds(h, 1), cols] -= csum[h]
                    dd[2 + h] += rsum[h]
                    dqs[h][...] += dqc[h]
                dk_ref[cols, :] += sum(dkc[1:], dkc[0])
                dv_ref[cols, :] += sum(dvc[1:], dvc[0])

            _by_heads(j, js_ref[N_PAIRS + 2 * p, i], js_ref[N_PAIRS + 2 * p + 1, i], heads)

        _walk_up(fetch, j0, per * i, per * i + per - 1, tile)
        dq_ref[...] = (jnp.where(is_a, dq_a[...], dq_b[...]) * SCALE).astype(BF16)
        eye = lax.broadcasted_iota(jnp.int32, (bq, bq), 0) == lax.broadcasted_iota(jnp.int32, (bq, bq), 1)
        own = pl.ds(pl.multiple_of(i * bq, bq), bq)
        for h in range(2):
            dc_ref[0, pl.ds(h, 1), own] += jnp.sum(jnp.where(eye, dd[2 + h], 0.0), axis=0, keepdims=True)

        @pl.when(i == nq - 1)
        def _():
            dk_out[...] = dk_ref[...].astype(BF16)
            dv_out[...] = dv_ref[...].astype(BF16)

    grid_spec = pltpu.PrefetchScalarGridSpec(
        num_scalar_prefetch=1, grid=(N_PAIRS, nq),
        in_specs=[pl.BlockSpec((bq, LANES), lambda p, i, js: (i, col0 + p)),
                  pl.BlockSpec(memory_space=pl.ANY),
                  pl.BlockSpec((bq, LANES), lambda p, i, js: (i, p)),
                  pl.BlockSpec((bq, LANES), lambda p, i, js: (i, p)),
                  pl.BlockSpec((1, bq, 8), lambda p, i, js: (p, i, 0)),
                  pl.BlockSpec((1, bq, 8), lambda p, i, js: (p, i, 0)),
                  pl.BlockSpec((1, 8, S), lambda p, i, js: (p, 0, 0))],
        out_specs=[pl.BlockSpec((bq, LANES), lambda p, i, js: (i, p)),
                   pl.BlockSpec((S, LANES), lambda p, i, js: (0, p)),
                   pl.BlockSpec((S, LANES), lambda p, i, js: (0, p)),
                   pl.BlockSpec((1, 8, S), lambda p, i, js: (p, 0, 0))],
        scratch_shapes=[pltpu.VMEM((bq, LANES), F32), pltpu.VMEM((bq, LANES), F32)]
        + [pltpu.VMEM((bq, LANES), BF16)] * 4 + [pltpu.VMEM((6, bq, 1), F32)]
        + [pltpu.VMEM((2 * KV_SLOTS, bk, LANES), BF16)] * 2 + [pltpu.SemaphoreType.DMA((2, 2 * KV_SLOTS))]
        + [pltpu.VMEM((S, LANES), F32)] * 2)
    return pl.pallas_call(
        body, name="fox_bwd", grid_spec=grid_spec,
        out_shape=[jax.ShapeDtypeStruct((S, GROUP_W), BF16)] * 3 + [jax.ShapeDtypeStruct((N_PAIRS, 8, S), F32)],
        compiler_params=_params(VMEM_BIG),
    )(jstart, proj, proj, do, o, st, c_col, c_row)


_HBM = pl.BlockSpec(memory_space=pltpu.HBM)


def _coords():
    return lax.axis_index("x"), lax.axis_index("y"), lax.axis_index("c")


def _gather_copies(ins, outs, send_sems, recv_sems, loc_sems):
    n = len(ins)
    x, y, c = _coords()
    mine = 2 * x + y
    chips = [(1 - x, y), (x, 1 - y), (1 - x, 1 - y)]

    def copy(w, r, slab, to):
        return pltpu.make_async_remote_copy(
            src_ref=ins[w], dst_ref=outs[w].at[slab], send_sem=send_sems.at[3 * w + r],
            recv_sem=recv_sems.at[3 * w + r], device_id=to, device_id_type=MESH)

    def own():
        local = [pltpu.make_async_copy(ins[w], outs[w].at[mine], loc_sems.at[w]) for w in range(n)]
        return local, [copy(w, r, mine, (cx, cy, c)) for w in range(n) for r, (cx, cy) in enumerate(chips)]

    def start():
        local, sends = own()
        for cp in local + sends:
            cp.start()

    def wait():
        local, sends = own()
        for w in range(n):
            for r, (cx, cy) in enumerate(chips):
                copy(w, r, 2 * cx + cy, (cx, cy, c)).wait_recv()
        for cp in sends:
            cp.wait_send()
        for cp in local:
            cp.wait()

    return start, wait


def _gather_shapes(shards):
    n = len(shards)
    return ([jax.ShapeDtypeStruct((4,) + s.shape, s.dtype) for s in shards],
            [pltpu.SemaphoreType.DMA((3 * n,)), pltpu.SemaphoreType.DMA((3 * n,)), pltpu.SemaphoreType.DMA((n,))])


def _allgather_chips(shards):
    n = len(shards)

    def body(*refs):
        start, wait = _gather_copies(refs[:n], refs[n:2 * n], *refs[2 * n:])
        start()
        wait()

    out_shape, sems = _gather_shapes(shards)
    return pl.pallas_call(body, name="allgather_weights", in_specs=[_HBM] * n, out_specs=[_HBM] * n,
                          out_shape=out_shape, scratch_shapes=sems)(*shards)


def _proj_gather(x, w, shards, tm, tn):
    (M, K), N, n = x.shape, w.shape[1], len(shards)
    tm = min(tm, M)
    gi, gj = M // tm, N // tn

    def body(a_ref, b_ref, *rest):
        o_ref = rest[n]
        start, wait = _gather_copies(rest[:n], rest[n + 1:2 * n + 1], *rest[2 * n + 1:])
        i, j = pl.program_id(0), pl.program_id(1)
        pl.when(jnp.logical_and(i == 0, j == 0))(start)
        o_ref[...] = _dot(a_ref[...].astype(BF16), b_ref[...]).astype(o_ref.dtype)
        pl.when(jnp.logical_and(i == gi - 1, j == gj - 1))(wait)

    out_shape, sems = _gather_shapes(shards)
    return pl.pallas_call(
        body, name="proj_gather", grid=(gi, gj),
        in_specs=[pl.BlockSpec((tm, K), lambda i, j: (i, 0)), pl.BlockSpec((K, tn), lambda i, j: (0, j))] + [_HBM] * n,
        out_specs=[pl.BlockSpec((tm, tn), lambda i, j: (i, j))] + [_HBM] * n,
        out_shape=[jax.ShapeDtypeStruct((M, N), BF16)] + out_shape, scratch_shapes=sems,
    )(x, w, *shards)


def _exchange_copies(ins, outs, send_sems, recv_sems, loc_sems, per_chip, parts):
    n = len(parts)
    half = [p.shape[1] // 2 for p in parts] if per_chip else None
    x, y, c = _coords()
    me = 4 * x + 2 * y + c
    peers = [(x ^ fx, y ^ fy, c ^ fc) for fx in (0, 1) for fy in (0, 1) for fc in (0, 1)][1:]

    def src(w, dev):
        if not per_chip:
            return ins[w]
        return ins[w].at[2 * dev[0] + dev[1], pl.ds(pl.multiple_of(dev[2] * half[w], 16), half[w]), :]

    def copy(w, r, source, slab, to):
        return pltpu.make_async_remote_copy(
            src_ref=source, dst_ref=outs[w].at[slab], send_sem=send_sems.at[7 * w + r],
            recv_sem=recv_sems.at[7 * w + r], device_id=to, device_id_type=MESH)

    def own():
        local = [pltpu.make_async_copy(src(w, (x, y, c)), outs[w].at[me], loc_sems.at[w]) for w in range(n)]
        return local, [copy(w, r, src(w, dev), me, dev) for w in range(n) for r, dev in enumerate(peers)]

    def start():
        local, sends = own()
        for cp in local + sends:
            cp.start()

    def wait():
        local, sends = own()
        for w in range(n):
            for r, dev in enumerate(peers):
                copy(w, r, src(w, dev), 4 * dev[0] + 2 * dev[1] + dev[2], dev).wait_recv()
        for cp in sends:
            cp.wait_send()
        for cp in local:
            cp.wait()

    return start, wait


def _exchange_shapes(parts, per_chip):
    n = len(parts)
    return ([jax.ShapeDtypeStruct((8, p.shape[1] // 2, p.shape[2]) if per_chip else (8,) + p.shape, p.dtype)
             for p in parts],
            [pltpu.SemaphoreType.DMA((7 * n,)), pltpu.SemaphoreType.DMA((7 * n,)), pltpu.SemaphoreType.DMA((n,))])


def _exchange(parts, per_chip):
    n = len(parts)

    def body(*refs):
        start, wait = _exchange_copies(refs[:n], refs[n:2 * n], *refs[2 * n:], per_chip, parts)
        start()
        wait()

    out_shape, sems = _exchange_shapes(parts, per_chip)
    return pl.pallas_call(body, name="exchange_per_chip" if per_chip else "exchange_all",
                          in_specs=[_HBM] * n, out_specs=[_HBM] * n, out_shape=out_shape, scratch_shapes=sems)(*parts)


def _sibling_swap(halves):
    n = len(halves)

    def body(*refs):
        ins, outs = refs[:n], refs[n:2 * n]
        send_sems, recv_sems, loc_sems = refs[2 * n:]
        x, y, c = _coords()

        def rows(w, core):
            rh = halves[w].shape[0]
            return outs[w].at[pl.ds(pl.multiple_of(core * rh, 8), rh), :]

        def copy(w, core):
            return pltpu.make_async_remote_copy(
                src_ref=ins[w], dst_ref=rows(w, core), send_sem=send_sems.at[w], recv_sem=recv_sems.at[w],
                device_id=(x, y, 1 - c), device_id_type=MESH)

        local = [pltpu.make_async_copy(ins[w], rows(w, c), loc_sems.at[w]) for w in range(n)]
        sends = [copy(w, c) for w in range(n)]
        for cp in local + sends:
            cp.start()
        for w in range(n):
            copy(w, 1 - c).wait_recv()
        for cp in sends:
            cp.wait_send()
        for cp in local:
            cp.wait()

    vmem = pl.BlockSpec(memory_space=pltpu.VMEM)
    return pl.pallas_call(
        body, name="sibling_swap", in_specs=[vmem] * n, out_specs=[vmem] * n,
        out_shape=[jax.ShapeDtypeStruct((2 * h.shape[0], h.shape[1]), h.dtype) for h in halves],
        scratch_shapes=[pltpu.SemaphoreType.DMA((n,)), pltpu.SemaphoreType.DMA((n,)), pltpu.SemaphoreType.DMA((n,))],
    )(*halves)


def _adamw(w, g, m, v):
    m = ADAM_B1 * m + (1.0 - ADAM_B1) * g
    v = ADAM_B2 * v + (1.0 - ADAM_B2) * (g * g)
    m_hat = m / (1.0 - ADAM_B1 ** ADAM_STEP)
    v_hat = v / (1.0 - ADAM_B2 ** ADAM_STEP)
    delta = -ADAM_LR * (m_hat / (jnp.sqrt(v_hat) + ADAM_EPS) + ADAM_WD * w)
    return delta, m, v


def _sum_parts(parts, name, tr):
    _, R, C = parts.shape
    assert R % tr == 0

    def body(p_ref, g_ref):
        g = p_ref[0].astype(F32)
        for d in range(1, 8):
            g = g + p_ref[d].astype(F32)
        g_ref[...] = g

    return pl.pallas_call(
        body, name=name, grid=(R // tr,),
        in_specs=[pl.BlockSpec((8, tr, C), lambda i: (0, i, 0))],
        out_specs=pl.BlockSpec((tr, C), lambda i: (i, 0)), out_shape=jax.ShapeDtypeStruct((R, C), F32),
    )(parts)


def _adamw_call(g, w, m, v, name, tr):
    R, C = w.shape
    assert R % tr == 0

    def body(g_ref, w_ref, m_ref, v_ref, d_ref, nm_ref, nv_ref):
        d_ref[...], nm_ref[...], nv_ref[...] = _adamw(w_ref[...], g_ref[...], m_ref[...], v_ref[...])

    tile = pl.BlockSpec((tr, C), lambda i: (i, 0))
    return pl.pallas_call(
        body, name=name, grid=(R // tr,), in_specs=[tile] * 4,
        out_specs=[tile] * 3, out_shape=[jax.ShapeDtypeStruct((R, C), F32)] * 3,
    )(g, w, m, v)


def _sum_adamw_small(parts, w, m, v):
    def body(p_ref, w_ref, m_ref, v_ref, g_ref, d_ref, nm_ref, nv_ref, loss_ref):
        g = p_ref[0]
        for d in range(1, 8):
            g = g + p_ref[d]
        g_ref[...] = g
        d_ref[...], nm_ref[...], nv_ref[...] = _adamw(w_ref[...], g, m_ref[...], v_ref[...])
        row = lax.broadcasted_iota(jnp.int32, g.shape, 0)
        per_row = jnp.sum(jnp.where(row == 6, g, 0.0), axis=1, keepdims=True)
        loss_ref[...] = jnp.zeros((8, LANES), F32) + jnp.sum(per_row, axis=0, keepdims=True)

    return pl.pallas_call(
        body, name="sum_adamw_small",
        out_shape=[jax.ShapeDtypeStruct((8, D_MODEL), F32)] * 4 + [jax.ShapeDtypeStruct((8, LANES), F32)],
    )(parts, w, m, v)


def _pack_small(ln1_g, ln1_b, ln2_g, ln2_b, g_sb, g_fox, b_f):
    row5 = jnp.pad(b_f.reshape(1, N_FOX), ((0, 0), (0, D_MODEL - N_FOX)))
    rows = [ln1_g.reshape(1, -1), ln1_b.reshape(1, -1), ln2_g.reshape(1, -1), ln2_b.reshape(1, -1),
            jnp.concatenate([g_sb.reshape(1, -1), g_fox.reshape(1, -1)], axis=1), row5,
            jnp.zeros((2, D_MODEL), F32)]
    return jnp.concatenate(rows, axis=0)


def _unpack_small(p):
    return {"ln1_g": p[0:1], "ln1_b": p[1:2], "ln2_g": p[2:3], "ln2_b": p[3:4], "g_sb": p[4:5, :GROUP_W],
            "g_fox": p[4:5, GROUP_W:], "b_f": p[5:6, :N_FOX]}


def kernel(x, w_in, b_f, g_sb, g_fox, w_out, ln1_g, ln1_b, ln2_g, ln2_b, w_gate_up, w_down, loss_target, m_w_in, m_b_f, m_g_sb, m_g_fox, m_w_out, m_ln1_g, m_ln1_b, m_ln2_g, m_ln2_b, m_w_gate_up, m_w_down, v_w_in, v_b_f, v_g_sb, v_g_fox, v_w_out, v_ln1_g, v_ln1_b, v_ln2_g, v_ln2_b, v_w_gate_up, v_w_down):
    S = x.shape[1]
    x2 = x.reshape(S, D_MODEL)
    tgt = loss_target.reshape(S, D_MODEL)
    TM = 1024
    TR = 512
    BQ = ATTN_BLOCK
    in_w = w_in.shape[2]
    gu_w = w_gate_up.shape[2]

    shards = [w_in[0].astype(BF16), w_out[0].astype(BF16), w_gate_up[0].astype(BF16), w_down[0].astype(BF16)]
    (wi_s,) = _allgather_chips(shards[:1])
    wi = wi_s.transpose(1, 0, 2).reshape(D_MODEL, 4 * in_w)
    w_sb, w_fx = wi[:, :QKV_W // 2], wi[:, QKV_W // 2:QKV_W]
    wqkv = wi[:, :QKV_W]
    wft = wi[:, QKV_W:].T
    proj, wo_s, wgu_s, wd_s = _proj_gather(x2, wqkv, shards[1:], TM, 512)
    wo = wo_s.reshape(D_MODEL, D_MODEL)
    wgu = wgu_s.transpose(1, 0, 2).reshape(D_MODEL, 2 * D_FF)
    wg, wu = wgu[:, :D_FF], wgu[:, D_FF:]
    wd = wd_s.reshape(D_FF, D_MODEL)
    g_row = jnp.concatenate([g_sb, g_fox], axis=1)
    hid = np.arange(D_MODEL) // HEAD_DIM
    he_np = (hid[:, None] == np.arange(LANES)[None, :]).astype(np.float32)
    he, het = jnp.asarray(he_np, BF16), jnp.asarray(he_np.T, BF16)

    lf = _fgate_fwd(x2, wft, b_f.reshape(N_FOX, 1), TM)
    c = _cumsum_fwd(lf)
    c_pair = c.reshape(N_PAIRS, 2, S)
    c_row = jnp.pad(c_pair, ((0, 0), (0, 6), (0, 0)))
    c_col = jnp.pad(c_pair.transpose(0, 2, 1), ((0, 0), (0, 0), (0, 6)))

    o_sb, st_sb, jmin_sb = _sb_fwd(proj, 0, BQ)
    jstart_fx = _fox_start_blocks(*_fox_row_norms(proj, 12, TR), c, BQ, BQ)
    o_fx, st_fx = _fox_fwd(proj, 12, c_col, c_row, jstart_fx, BQ, BQ)

    def attn_post(i, osb_ref, ofx_ref, g_ref, he_ref, het_ref, on_ref):
        o = jnp.concatenate([osb_ref[...], ofx_ref[...]], axis=1)
        ms = _head_sums(o * o, he_ref[...], het_ref[...]) * (1.0 / HEAD_DIM)
        on_ref[...] = (o * lax.rsqrt(ms + RMS_EPS) * g_ref[...]).astype(BF16)

    (on,) = _rowwise(attn_post, "attn_post", S, TR,
                     [(o_sb, "t"), (o_fx, "t"), (g_row, "f"), (he, "f"), (het, "f")],
                     [((S, D_MODEL), BF16, "t")])

    u1 = _matmul(on, wo, mode="nn", name="mix", tm=TM, tn=D_MODEL, tk=D_MODEL, outs=[F32],
                 extras=[(x2, (TM if S >= TM else S, D_MODEL), _tile_ij)],
                 epilogue=lambda acc, xv: (ALPHA * xv + acc,))

    def ln1_fwd(i, u_ref, g_ref, b_ref, h_ref):
        xh, _ = _ln_stats(u_ref[...])
        h_ref[...] = xh * g_ref[...] + b_ref[...]

    (h1,) = _rowwise(ln1_fwd, "ln1_fwd", S, TR, [(u1, "t"), (ln1_g, "f"), (ln1_b, "f")], [((S, D_MODEL), F32, "t")])

    tm_e = TM if S >= TM else S
    n_ff = D_FF // 256

    def gate_up_body(h_ref, wg_ref, wu_ref, g_ref, u_ref, a_ref):
        h = h_ref[...].astype(BF16)
        g, u = _dot(h, wg_ref[...]), _dot(h, wu_ref[...])
        g_ref[...] = g.astype(BF16)
        u_ref[...] = u.astype(BF16)
        a_ref[...] = (g * _sigmoid(g) * u).astype(BF16)

    ff_tile = pl.BlockSpec((tm_e, 256), lambda i, j: (i, j))
    gate, up, act = pl.pallas_call(
        gate_up_body, name="gate_up_act", grid=(S // tm_e, n_ff),
        in_specs=[pl.BlockSpec((tm_e, D_MODEL), lambda i, j: (i, 0)),
                  pl.BlockSpec((D_MODEL, 256), lambda i, j: (0, j)),
                  pl.BlockSpec((D_MODEL, 256), lambda i, j: (0, j + n_ff))],
        out_specs=[ff_tile] * 3, out_shape=[jax.ShapeDtypeStruct((S, D_FF), BF16)] * 3)(h1, wgu, wgu)

    u2 = _matmul(act, wd, mode="nn", name="ffn_down", tm=TM, tn=D_MODEL, tk=D_FF, outs=[F32],
                 extras=[(h1, (TM if S >= TM else S, D_MODEL), _tile_ij)],
                 epilogue=lambda acc, hv: (ALPHA * hv + acc,))

    def ln2_loss(i, u_ref, t_ref, g_ref, b_ref, du_ref, acc_ref):
        xh, r = _ln_stats(u_ref[...])
        g = g_ref[...]
        err = xh * g + b_ref[...] - t_ref[...]
        dy = err * (1.0 / D_MODEL)
        du_ref[...] = _ln_bwd(dy, xh, r, g)
        _acc_rows(i, acc_ref, {2: jnp.sum(dy * xh, axis=0, keepdims=True), 3: jnp.sum(dy, axis=0, keepdims=True),
                               6: jnp.sum(err * err, axis=0, keepdims=True) * (0.5 / D_MODEL)})

    du2, acc_ln2 = _rowwise(ln2_loss, "ln2_loss", S, TR, [(u2, "t"), (tgt, "t"), (ln2_g, "f"), (ln2_b, "f")],
                            [((S, D_MODEL), F32, "t"), ((8, D_MODEL), F32, "f")])

    d_wd = _matmul(act, du2, mode="tn", name="dw_down", tm=1408, tn=D_MODEL, tk=TM, outs=[BF16])

    def dgu_epilogue(da, g, u):
        g, u = g.astype(F32), u.astype(F32)
        s = _sigmoid(g)
        return da * u * (s * (1.0 + g * (1.0 - s))), da * (g * s)

    dgate, dup = _matmul(du2, wd, mode="nt", name="d_act", tm=TM, tn=1408, tk=D_MODEL, outs=[BF16, BF16],
                         extras=[(gate, (tm_e, 1408), _tile_ij), (up, (tm_e, 1408), _tile_ij)],
                         epilogue=dgu_epilogue)
    d_wg = _matmul(h1, dgate, mode="tn", name="dw_gate", tm=D_MODEL, tn=1408, tk=TM, outs=[BF16])
    d_wu = _matmul(h1, dup, mode="tn", name="dw_up", tm=D_MODEL, tn=1408, tk=TM, outs=[BF16])
    d_wgu = jnp.concatenate([d_wg, d_wu], axis=1)
    dh1, got_down = _matmul(dgate, wg, mode="nt", name="dh1_gate", tm=TM, tn=D_MODEL, tk=D_FF, outs=[F32],
                            extras=[(du2, (tm_e, D_MODEL), _tile_ij)], epilogue=lambda acc, e: (ALPHA * e + acc,),
                            hosted=[d_wd.reshape(4, D_FF // 4, D_MODEL)])
    dh1, got_gu = _matmul(dup, wu, mode="nt", name="dh1_up", tm=TM, tn=D_MODEL, tk=D_FF, outs=[F32],
                          extras=[(dh1, (tm_e, D_MODEL), _tile_ij)], epilogue=lambda acc, e: (e + acc,),
                          hosted=[d_wgu.reshape(D_MODEL, 4, gu_w).transpose(1, 0, 2)])

    def ln1_bwd(i, dh_ref, u_ref, g_ref, du_ref, acc_ref):
        xh, r = _ln_stats(u_ref[...])
        dh = dh_ref[...]
        du_ref[...] = _ln_bwd(dh, xh, r, g_ref[...])
        _acc_rows(i, acc_ref, {0: jnp.sum(dh * xh, axis=0, keepdims=True), 1: jnp.sum(dh, axis=0, keepdims=True)})

    du1, acc_ln1 = _rowwise(ln1_bwd, "ln1_bwd", S, TR, [(dh1, "t"), (u1, "t"), (ln1_g, "f")],
                            [((S, D_MODEL), F32, "t"), ((8, D_MODEL), F32, "f")])
    d_wo = _matmul(on, du1, mode="tn", name="dw_out", tm=D_MODEL, tn=D_MODEL, tk=TM, outs=[BF16])
    don, got_out = _matmul(du1, wo, mode="nt", name="d_on", tm=TM, tn=D_MODEL, tk=D_MODEL, outs=[F32],
                           hosted=[d_wo.reshape(4, D_MODEL // 4, D_MODEL)])

    def rms_bwd(i, don_ref, osb_ref, ofx_ref, g_ref, he_ref, het_ref, dosb_ref, dofx_ref, acc_ref):
        o = jnp.concatenate([osb_ref[...], ofx_ref[...]], axis=1)
        hev, hetv = he_ref[...], het_ref[...]
        r = lax.rsqrt(_head_sums(o * o, hev, hetv) * (1.0 / HEAD_DIM) + RMS_EPS)
        dn = don_ref[...]
        dg = dn * g_ref[...]
        do = r * dg - o * (r * r * r) * (_head_sums(dg * o, hev, hetv) * (1.0 / HEAD_DIM))
        dosb_ref[...] = do[:, :GROUP_W]
        dofx_ref[...] = do[:, GROUP_W:]
        _acc_rows(i, acc_ref, {4: jnp.sum(dn * o * r, axis=0, keepdims=True)})

    do_sb, do_fx, acc_rms = _rowwise(
        rms_bwd, "rms_bwd", S, TR, [(don, "t"), (o_sb, "t"), (o_fx, "t"), (g_row, "f"), (he, "f"), (het, "f")],
        [((S, GROUP_W), F32, "t"), ((S, GROUP_W), F32, "t"), ((8, D_MODEL), F32, "f")])

    dq_sb, dk_sb, dv_sb = _sb_bwd(proj, 0, do_sb, st_sb, jmin_sb, BQ)
    jstart_fx2 = jnp.minimum(jstart_fx[:, 0::2], jstart_fx[:, 1::2])
    dq_fx, dk_fx, dv_fx, dc = _fox_bwd(proj, 12, do_fx, o_fx, st_fx, c_col, c_row, jstart_fx2, 2 * BQ, BQ)
    dfl, dbf = _fgate_bwd(dc[:, :2, :].reshape(N_FOX, S), lf)
    dp_sb = jnp.concatenate([dq_sb, dk_sb, dv_sb], axis=1)
    dp_fx = jnp.concatenate([dq_fx, dk_fx, dv_fx], axis=1)

    d_wsb = _matmul(x2, dp_sb, mode="tn", name="dw_in_sb", tm=D_MODEL, tn=QKV_W // 2, tk=TM, outs=[BF16])
    d_wfx = _matmul(x2, dp_fx, mode="tn", name="dw_in_fx", tm=D_MODEL, tn=QKV_W // 2, tk=TM, outs=[BF16])
    d_wft = _matmul(dfl, x2, mode="nn", name="dw_in_f", tm=N_FOX, tn=D_MODEL, tk=TM, outs=[BF16])
    dx = _matmul(dp_sb, w_sb, mode="nt", name="dx_sb", tm=TM, tn=D_MODEL, tk=QKV_W // 2, outs=[F32],
                 extras=[(du1, (tm_e, D_MODEL), _tile_ij)], epilogue=lambda acc, e: (ALPHA * e + acc,))
    dx = _matmul(dp_fx, w_fx, mode="nt", name="dx_fx", tm=TM, tn=D_MODEL, tk=QKV_W // 2, outs=[F32],
                 extras=[(dx, (tm_e, D_MODEL), _tile_ij)], epilogue=lambda acc, e: (e + acc,))
    dx = _matmul(dfl, wft, mode="tn", name="dx_f", tm=TM, tn=D_MODEL, tk=N_FOX, outs=[F32],
                 extras=[(dx, (tm_e, D_MODEL), _tile_ij)], epilogue=lambda acc, e: (e + acc,))

    d_wi = jnp.concatenate([d_wsb, d_wfx, d_wft.T], axis=1)
    (got_in,) = _exchange([d_wi.reshape(D_MODEL, 4, in_w).transpose(1, 0, 2)], True)
    got = [got_in, got_out, got_gu, got_down]
    big_names = ("w_in", "w_out", "w_gate_up", "w_down")
    halves = [_sum_parts(p, "sum_" + nm, tr) for nm, p, tr in zip(big_names, got, (256, 128, 128, 176))]
    grads = _sibling_swap(halves)
    big = {}
    for nm, g, w, m, v, tr in zip(big_names, grads, (w_in, w_out, w_gate_up, w_down),
                                  (m_w_in, m_w_out, m_w_gate_up, m_w_down),
                                  (v_w_in, v_w_out, v_w_gate_up, v_w_down), (256, 256, 256, 176)):
        big[nm] = [r[None] for r in [g] + list(_adamw_call(g, w[0], m[0], v[0], "adamw_" + nm, tr))]

    small = acc_ln2 + acc_ln1 + acc_rms
    small = small + jnp.pad(dbf.reshape(1, N_FOX), ((5, 2), (0, D_MODEL - N_FOX)))
    (small_all,) = _exchange([small], False)
    sw = _pack_small(ln1_g, ln1_b, ln2_g, ln2_b, g_sb, g_fox, b_f)
    sm = _pack_small(m_ln1_g, m_ln1_b, m_ln2_g, m_ln2_b, m_g_sb, m_g_fox, m_b_f)
    sv = _pack_small(v_ln1_g, v_ln1_b, v_ln2_g, v_ln2_b, v_g_sb, v_g_fox, v_b_f)
    sg, sd, snm, snv, loss_blk = _sum_adamw_small(small_all, sw, sm, sv)
    sg, sd, snm, snv = _unpack_small(sg), _unpack_small(sd), _unpack_small(snm), _unpack_small(snv)

    names = ["w_in", "b_f", "g_sb", "g_fox", "w_out", "ln1_g", "ln1_b", "ln2_g", "ln2_b", "w_gate_up", "w_down"]
    outs = [loss_blk[0, 0], dx.reshape(1, S, D_MODEL)]
    for k, table in enumerate((sg, sd, snm, snv)):
        outs += [big[n][k] if n in big else table[n] for n in names]
    return tuple(outs)
```

```python
import functools

import numpy as np
import jax
import jax.numpy as jnp
from jax import lax
from jax.experimental import pallas as pl
from jax.experimental.pallas import tpu as pltpu

F32 = jnp.float32
BF16 = jnp.bfloat16

D_MODEL = 1024
HEAD_DIM = 64
LANES = 128
N_PAIRS = 4
GROUP_W = 512
QKV_W = 3072
D_FF = 2816
N_FOX = 8
ALPHA = 2.0 ** 0.25
LN_EPS = 1e-5
RMS_EPS = 1e-6
SCALE = HEAD_DIM ** -0.5
NEG_BIG = -1e30
FOX_SKIP = 30.0
SB_STOP = -105.0
ADAM_LR, ADAM_B1, ADAM_B2, ADAM_EPS, ADAM_WD, ADAM_STEP = 0.001, 0.9, 0.999, 1e-08, 0.01, 10
KV_SLOTS = 4
SCAN_GROUP = 8
ATTN_BLOCK = 256
VMEM_BIG = 56 * 1024 * 1024
MESH = pl.DeviceIdType.MESH

_NN = (((1,), (0,)), ((), ()))
_NT = (((1,), (1,)), ((), ()))
_TN = (((0,), (0,)), ((), ()))


def _dot(a, b, dims=_NN):
    return lax.dot_general(a, b, dims, preferred_element_type=F32)


def _split_dot(x, t):
    hi = x.astype(BF16)
    lo = (x - hi.astype(F32)).astype(BF16)
    return _dot(hi, t) + _dot(lo, t)


def _softplus(z):
    return jnp.maximum(z, 0.0) + jnp.log1p(jnp.exp(-jnp.abs(z)))


def _sigmoid(x):
    return 0.5 * jnp.tanh(0.5 * x) + 0.5


def _col(v, h):
    lane = lax.broadcasted_iota(jnp.int32, v.shape, 1)
    return jnp.sum(jnp.where(lane == h, v, 0.0), axis=1, keepdims=True)


def _two_sum(hi, lo, b):
    s = hi + b
    bb = s - hi
    err = (hi - (s - bb)) + (b - bb)
    return s, lo + err


def _params(vmem=None):
    return pltpu.CompilerParams(vmem_limit_bytes=vmem) if vmem else None


def _matmul(a, b, *, mode, name, tm, tn, tk, outs, extras=(), epilogue=None, vmem=None, hosted=()):
    if mode == "nn":
        (M, K), (_, N) = a.shape, b.shape
    elif mode == "nt":
        (M, K), (N, _) = a.shape, b.shape
    else:
        (K, M), (_, N) = a.shape, b.shape
    tm, tn, tk = min(tm, M), min(tn, N), min(tk, K)
    assert M % tm == 0 and N % tn == 0 and K % tk == 0, (name, M, N, K, tm, tn, tk)
    nk = K // tk
    dims = {"nn": _NN, "nt": _NT, "tn": _TN}[mode]
    if mode == "tn":
        a_spec = pl.BlockSpec((tk, tm), lambda i, j, k: (k, i))
    else:
        a_spec = pl.BlockSpec((tm, tk), lambda i, j, k: (i, k))
    if mode == "nt":
        b_spec = pl.BlockSpec((tn, tk), lambda i, j, k: (j, k))
    else:
        b_spec = pl.BlockSpec((tk, tn), lambda i, j, k: (k, j))
    ex_specs = [pl.BlockSpec(bs, (lambda i, j, k, f=f: f(i, j))) for (_, bs, f) in extras]
    ne, no, nh = len(extras), len(outs), len(hosted)
    if epilogue is None:
        epilogue = lambda acc: (acc,)
    gi, gj = M // tm, N // tn
    host_shapes, host_sems = _exchange_shapes(hosted, True) if nh else ([], [])

    def body(a_ref, b_ref, *rest):
        ex_refs, host_ins = rest[:ne], rest[ne:ne + nh]
        out_refs, host_outs = rest[ne + nh:ne + nh + no], rest[ne + nh + no:ne + 2 * nh + no]
        acc = rest[ne + 2 * nh + no]
        i, j, k = pl.program_id(0), pl.program_id(1), pl.program_id(2)
        if nh:
            start, wait = _exchange_copies(host_ins, host_outs, *rest[ne + 2 * nh + no + 1:], True, hosted)
            pl.when(jnp.logical_and(jnp.logical_and(i == 0, j == 0), k == 0))(start)

        @pl.when(k == 0)
        def _():
            acc[...] = jnp.zeros_like(acc)

        acc[...] += _dot(a_ref[...].astype(BF16), b_ref[...].astype(BF16), dims)

        @pl.when(k == nk - 1)
        def _():
            res = epilogue(acc[...], *[e[...] for e in ex_refs])
            for r, o in zip(res, out_refs):
                o[...] = r.astype(o.dtype)

        if nh:
            pl.when(jnp.logical_and(jnp.logical_and(i == gi - 1, j == gj - 1), k == nk - 1))(wait)

    res = pl.pallas_call(
        body, name=name, grid=(gi, gj, nk),
        in_specs=[a_spec, b_spec] + ex_specs + [_HBM] * nh,
        out_specs=[pl.BlockSpec((tm, tn), lambda i, j, k: (i, j)) for _ in outs] + [_HBM] * nh,
        out_shape=[jax.ShapeDtypeStruct((M, N), d) for d in outs] + host_shapes,
        scratch_shapes=[pltpu.VMEM((tm, tn), F32)] + host_sems,
        compiler_params=_params(vmem),
    )(a, b, *[e[0] for e in extras], *hosted)
    return res[0] if no + nh == 1 else res


def _tile_ij(i, j):
    return (i, j)


def _rowwise(fn, name, rows, tm, ins, outs, vmem=None):
    tm = min(tm, rows)
    assert rows % tm == 0

    def spec(shape, kind):
        if kind == "t":
            return pl.BlockSpec((tm,) + tuple(shape[1:]), lambda i: (i,) + (0,) * (len(shape) - 1))
        return pl.BlockSpec(tuple(shape), lambda i: (0,) * len(shape))

    def body(*refs):
        fn(pl.program_id(0), *refs)

    return pl.pallas_call(
        body, name=name, grid=(rows // tm,),
        in_specs=[spec(a.shape, k) for a, k in ins],
        out_specs=[spec(s, k) for s, _, k in outs],
        out_shape=[jax.ShapeDtypeStruct(s, d) for s, d, _ in outs],
        compiler_params=_params(vmem),
    )(*[a for a, _ in ins])


def _ln_stats(u):
    mu = jnp.mean(u, axis=-1, keepdims=True)
    d = u - mu
    var = jnp.mean(d * d, axis=-1, keepdims=True)
    r = lax.rsqrt(var + LN_EPS)
    return d * r, r


def _ln_bwd(dh, xh, r, g):
    dxh = dh * g
    m1 = jnp.mean(dxh, axis=-1, keepdims=True)
    m2 = jnp.mean(dxh * xh, axis=-1, keepdims=True)
    return r * (dxh - m1 - xh * m2)


def _acc_rows(i, ref, rows):
    @pl.when(i == 0)
    def _():
        ref[...] = jnp.zeros_like(ref)
    for r, v in rows.items():
        ref[pl.ds(r, 1), :] += v


def _head_sums(v, he, het):
    return _split_dot(_split_dot(v, he), het)


def _fgate_fwd(x, wft, bf_col, tm):
    S = x.shape[0]
    tm = min(tm, S)

    def body(wft_ref, bf_ref, x_ref, lf_ref):
        f = _dot(wft_ref[...], x_ref[...].astype(BF16), _NT) + bf_ref[...]
        lf_ref[...] = -_softplus(-f)

    return pl.pallas_call(
        body, name="fgate_fwd", grid=(S // tm,),
        in_specs=[pl.BlockSpec((N_FOX, D_MODEL), lambda i: (0, 0)), pl.BlockSpec((N_FOX, 1), lambda i: (0, 0)),
                  pl.BlockSpec((tm, D_MODEL), lambda i: (i, 0))],
        out_specs=pl.BlockSpec((N_FOX, tm), lambda i: (0, i)),
        out_shape=jax.ShapeDtypeStruct((N_FOX, S), F32),
    )(wft, bf_col, x)


def _chunk_scan(v, reverse):
    lane = lax.broadcasted_iota(jnp.int32, v.shape, 1)
    sh = 1
    while sh < LANES:
        if reverse:
            v = v + jnp.where(lane < LANES - sh, pltpu.roll(v, LANES - sh, 1), 0.0)
        else:
            v = v + jnp.where(lane >= sh, pltpu.roll(v, sh, 1), 0.0)
        sh *= 2
    return v


def _cumsum_fwd(lf):
    n, S = lf.shape
    nc = S // LANES

    grp = min(SCAN_GROUP, nc)

    def body(lf_ref, c_ref):
        def step(gi, carry):
            sls = [pl.ds(pl.multiple_of((gi * grp + g) * LANES, LANES), LANES) for g in range(grp)]
            vs = [_chunk_scan(lf_ref[:, sl], False) for sl in sls]
            tots = [_col(v, LANES - 1) for v in vs]
            for sl, v, t in zip(sls, vs, tots):
                c_ref[:, sl] = v + carry
                carry = carry + t
            return carry
        lax.fori_loop(0, nc // grp, step, jnp.zeros((n, 1), F32))

    return pl.pallas_call(body, name="cumsum_fwd", out_shape=jax.ShapeDtypeStruct((n, S), F32))(lf)


def _fgate_bwd(dc, lf):
    n, S = dc.shape
    nc = S // LANES

    grp = min(SCAN_GROUP, nc)

    def body(dc_ref, lf_ref, dfl_ref, dbf_ref):
        def step(t, carry):
            car, tot = carry
            gi = nc // grp - 1 - t
            sls = [pl.ds(pl.multiple_of((gi * grp + g) * LANES, LANES), LANES) for g in range(grp)]
            vs = [_chunk_scan(dc_ref[:, sl], True) for sl in sls]
            firsts = [_col(v, 0) for v in vs]
            for sl, v, f in reversed(list(zip(sls, vs, firsts))):
                dfl = (v + car) * (1.0 - jnp.exp(lf_ref[:, sl]))
                dfl_ref[:, sl] = dfl
                tot = tot + jnp.sum(dfl, axis=1, keepdims=True)
                car = car + f
            return car, tot
        _, tot = lax.fori_loop(0, nc // grp, step, (jnp.zeros((n, 1), F32), jnp.zeros((n, 1), F32)))
        dbf_ref[...] = tot

    return pl.pallas_call(body, name="fgate_bwd",
                          out_shape=[jax.ShapeDtypeStruct((n, S), F32), jax.ShapeDtypeStruct((n, 1), F32)])(dc, lf)


def _tri_matrices(b):
    r = np.arange(b)
    tfwd = (r[:, None] <= r[None, :]).astype(np.float32)
    return jnp.asarray(tfwd, BF16), jnp.asarray(tfwd.T, BF16)


def _kv_copies(kv_hbm, kbuf, vbuf, sems, sem0, pair_col, bq, j, slot):
    rows = pl.ds(pl.multiple_of(j * bq, bq), bq)

    def cols(c):
        return pl.ds(pl.multiple_of((pair_col + c) * LANES, LANES), LANES)

    return (pltpu.make_async_copy(kv_hbm.at[rows, cols(4)], kbuf.at[slot], sems.at[0, sem0 + slot]),
            pltpu.make_async_copy(kv_hbm.at[rows, cols(8)], vbuf.at[slot], sems.at[1, sem0 + slot]))


def _first_two_up(first_block, per=1):
    def blocks(pair, blk):
        first = first_block(pair, blk)
        return first, first + 1, first + 1 <= per * blk + per - 1
    return blocks


def _first_two_down(pair, blk):
    return blk, blk - 1, blk > 0


def _start_two(fetch, pair, first, second, has_second, ahead):
    for cp in fetch(first, 0, pair, ahead):
        cp.start()

    @pl.when(has_second)
    def _():
        for cp in fetch(second, 1, pair, ahead):
            cp.start()


def _kv_fetcher(kv_hbm, kbuf, vbuf, sems, ns, col0, bq, p, i, nq, blocks):
    base = lax.rem(p * nq + i, 2) * ns
    own = (kbuf.at[pl.ds(base, ns)], vbuf.at[pl.ds(base, ns)])
    other = (kbuf.at[pl.ds(ns - base, ns)], vbuf.at[pl.ds(ns - base, ns)])

    def fetch(j, slot, pair=p, ahead=False):
        kb, vb = other if ahead else own
        return _kv_copies(kv_hbm, kb, vb, sems, ns - base if ahead else base, col0 + pair, bq, j, slot)

    pl.when(jnp.logical_and(p == 0, i == 0))(lambda: _start_two(fetch, p, *blocks(p, i), False))
    wrap = i == nq - 1

    @pl.when(jnp.logical_not(jnp.logical_and(wrap, p == N_PAIRS - 1)))
    def _():
        pair, blk = jnp.where(wrap, p + 1, p), jnp.where(wrap, 0, i + 1)
        _start_two(fetch, pair, *blocks(pair, blk), True)

    return fetch, own[0], own[1]


def _masked_pair(v, lane_is_a, scale=1.0):
    v = v.astype(F32) * scale
    return jnp.where(lane_is_a, v, 0.0).astype(BF16), jnp.where(lane_is_a, 0.0, v).astype(BF16)


def _sb_fwd(proj, col0, bq):
    S = proj.shape[0]
    bq = min(bq, S)
    nq = S // bq
    _, trev = _tri_matrices(bq)

    def body(q_ref, kv_hbm, trev_ref, o_ref, st_ref, jmin_ref, acc_a, acc_b, qa, qb, rs, kbuf, vbuf, sems):
        p, i = pl.program_id(0), pl.program_id(1)
        fetch, kbuf, vbuf = _kv_fetcher(kv_hbm, kbuf, vbuf, sems, 2, col0, bq, p, i, nq, _first_two_down)
        is_a = lax.broadcasted_iota(jnp.int32, (bq, LANES), 1) < HEAD_DIM
        acc_a[...] = jnp.zeros_like(acc_a)
        acc_b[...] = jnp.zeros_like(acc_b)
        rs[...] = jnp.zeros_like(rs)
        qa[...], qb[...] = _masked_pair(q_ref[...], is_a, SCALE)

        def tiles(blocks):
            hs, qs, accs, trev_m = (0, 1), (qa, qb), (acc_a, acc_b), trev_ref[...]
            kv = [(kbuf[s], vbuf[s]) for s, _ in blocks]
            bh = [(b, h) for b in range(len(blocks)) for h in hs]
            tri = lax.broadcasted_iota(jnp.int32, (bq, bq), 0) > lax.broadcasted_iota(jnp.int32, (bq, bq), 1)
            z = {(b, h): _dot(qs[h][...], kv[b][0], _NT) for b, h in bh}
            lk = {(b, h): -_softplus(z[b, h]) for b, h in bh}
            lk = {(b, h): jnp.where(tri, lk[b, h], 0.0) if blocks[b][1] else lk[b, h] for b, h in bh}
            suf = {(b, h): _split_dot(lk[b, h], trev_m) for b, h in bh}
            tot = {(b, h): jnp.sum(lk[b, h], axis=1, keepdims=True) for b, h in bh}
            right = {}
            for h in hs:
                r = rs[2 * h] + rs[2 * h + 1]
                for b in range(len(blocks)):
                    right[b, h] = r
                    r = r + tot[b, h]
            w = {(b, h): jnp.exp(z[b, h] + suf[b, h] + right[b, h]) for b, h in bh}
            w = {(b, h): jnp.where(tri, w[b, h], 0.0) if blocks[b][1] else w[b, h] for b, h in bh}
            pv = {(b, h): _dot(w[b, h].astype(BF16), kv[b][1]) for b, h in bh}
            for h in hs:
                accs[h][...] += sum([pv[b, h] for b in range(1, len(blocks))], pv[0, h])
                hi, lo = rs[2 * h], rs[2 * h + 1]
                for b in range(len(blocks)):
                    hi, lo = _two_sum(hi, lo, tot[b, h])
                rs[2 * h], rs[2 * h + 1] = hi, lo

        def live():
            return (jnp.max(jnp.maximum(rs[0], rs[2])) > SB_STOP).astype(jnp.int32)

        for cp in fetch(i, 0):
            cp.wait()
        pl.when(i == 0)(functools.partial(tiles, [(0, True)]))

        @pl.when(i > 0)
        def _():
            for cp in fetch(i - 1, 1):
                cp.wait()
            tiles([(0, True), (1, False)])

        def step(carry):
            j, _ = carry
            slot = lax.rem(i - j, 2)
            for cp in fetch(j, slot):
                cp.start()
            for cp in fetch(j, slot):
                cp.wait()
            tiles([(slot, False)])
            return j - 1, live()

        j_end, _ = lax.while_loop(lambda c: jnp.logical_and(c[0] >= 0, c[1] > 0), step, (i - 2, live()))
        jmin_ref[p, i] = jnp.maximum(j_end + 1, 0)
        o_ref[...] = jnp.where(is_a, acc_a[...], acc_b[...])
        lane8 = lax.broadcasted_iota(jnp.int32, (bq, 8), 1)
        st = jnp.zeros((bq, 8), F32)
        for c, src in enumerate((0, 2, 1, 3)):
            st = jnp.where(lane8 == c, rs[src], st)
        st_ref[0] = st

    return pl.pallas_call(
        body, name="sb_fwd", grid=(N_PAIRS, nq),
        in_specs=[pl.BlockSpec((bq, LANES), lambda p, i: (i, col0 + p)),
                  pl.BlockSpec(memory_space=pl.ANY),
                  pl.BlockSpec((bq, bq), lambda p, i: (0, 0))],
        out_specs=[pl.BlockSpec((bq, LANES), lambda p, i: (i, p)),
                   pl.BlockSpec((1, bq, 8), lambda p, i: (p, i, 0)),
                   pl.BlockSpec(memory_space=pltpu.SMEM)],
        out_shape=[jax.ShapeDtypeStruct((S, GROUP_W), F32), jax.ShapeDtypeStruct((N_PAIRS, S, 8), F32),
                   jax.ShapeDtypeStruct((N_PAIRS, nq), jnp.int32)],
        scratch_shapes=[pltpu.VMEM((bq, LANES), F32), pltpu.VMEM((bq, LANES), F32),
                        pltpu.VMEM((bq, LANES), BF16), pltpu.VMEM((bq, LANES), BF16),
                        pltpu.VMEM((4, bq, 1), F32),
                        pltpu.VMEM((4, bq, LANES), BF16), pltpu.VMEM((4, bq, LANES), BF16),
                        pltpu.SemaphoreType.DMA((2, 4))],
    )(proj, proj, trev)


def _sb_bwd(proj, col0, do, st, jmin, bq):
    S = proj.shape[0]
    bq = min(bq, S)
    nq = S // bq
    tfwd, trev = _tri_matrices(bq)

    def body(jmin_ref, q_ref, kv_hbm, do_ref, st_ref, tfwd_ref, trev_ref,
             dq_ref, dk_out, dv_out, dq_a, dq_b, qa, qb, doa, dob, rs, kbuf, vbuf, sems, dk_ref, dv_ref):
        p, i = pl.program_id(0), pl.program_id(1)
        j0 = jmin_ref[p, i]
        first_two = _first_two_up(lambda pair, blk: jmin_ref[pair, blk])
        fetch, kbuf, vbuf = _kv_fetcher(kv_hbm, kbuf, vbuf, sems, KV_SLOTS, col0, bq, p, i, nq, first_two)
        is_a = lax.broadcasted_iota(jnp.int32, (bq, LANES), 1) < HEAD_DIM

        @pl.when(i == 0)
        def _():
            dk_ref[...] = jnp.zeros_like(dk_ref)
            dv_ref[...] = jnp.zeros_like(dv_ref)

        dq_a[...] = jnp.zeros_like(dq_a)
        dq_b[...] = jnp.zeros_like(dq_b)
        rs[...] = jnp.zeros_like(rs)
        st_v = st_ref[0]
        for h in range(2):
            rs[6 + 2 * h], rs[7 + 2 * h] = _col(st_v, h), _col(st_v, 2 + h)
        qa[...], qb[...] = _masked_pair(q_ref[...], is_a, SCALE)
        doa[...], dob[...] = _masked_pair(do_ref[...], is_a)

        def tiles(blocks):
            hs, qs, dos, dqs = (0, 1), (qa, qb), (doa, dob), (dq_a, dq_b)
            tfwd_m, trev_m = tfwd_ref[...], trev_ref[...]
            kv = [(kbuf[s], vbuf[s]) for _, s, _ in blocks]
            nb = len(blocks)
            bh = [(b, h) for b in range(nb) for h in hs]
            tri = lax.broadcasted_iota(jnp.int32, (bq, bq), 0) > lax.broadcasted_iota(jnp.int32, (bq, bq), 1)

            def mask(x, b):
                return jnp.where(tri, x, 0.0) if blocks[b][2] else x

            z = {(b, h): _dot(qs[h][...], kv[b][0], _NT) for b, h in bh}
            dw = {(b, h): _dot(dos[h][...], kv[b][1], _NT) for b, h in bh}
            lk = {(b, h): mask(-_softplus(z[b, h]), b) for b, h in bh}
            suf = {(b, h): _split_dot(lk[b, h], trev_m) for b, h in bh}
            tot = {(b, h): jnp.sum(lk[b, h], axis=1, keepdims=True) for b, h in bh}
            pre = {}
            for h in hs:
                run = (rs[3 * h], rs[3 * h + 1])
                for b in range(nb):
                    run = _two_sum(run[0], run[1], tot[b, h])
                    pre[b, h] = run
            right = {(b, h): (rs[6 + 2 * h] - pre[b, h][0]) + (rs[7 + 2 * h] - pre[b, h][1]) for b, h in bh}
            w = {(b, h): mask(jnp.exp(z[b, h] + suf[b, h] + right[b, h]), b) for b, h in bh}
            g = {(b, h): dw[b, h] * w[b, h] for b, h in bh}
            gpre = {(b, h): _split_dot(g[b, h], tfwd_m) for b, h in bh}
            gtot = {(b, h): jnp.sum(g[b, h], axis=1, keepdims=True) for b, h in bh}
            gleft = {}
            for h in hs:
                run = rs[3 * h + 2]
                for b in range(nb):
                    gleft[b, h] = run
                    run = run + gtot[b, h]
                gleft[nb, h] = run
            dz = {(b, h): mask(g[b, h] - jnp.exp(z[b, h] + lk[b, h]) * (gpre[b, h] + gleft[b, h]), b) for b, h in bh}
            dzb = {(b, h): dz[b, h].astype(BF16) for b, h in bh}
            wb = {(b, h): w[b, h].astype(BF16) for b, h in bh}
            dqc = {(b, h): _dot(dzb[b, h], kv[b][0]) for b, h in bh}
            dkc = {(b, h): _dot(dzb[b, h], qs[h][...], _TN) for b, h in bh}
            dvc = {(b, h): _dot(wb[b, h], dos[h][...], _TN) for b, h in bh}
            for h in hs:
                rs[3 * h], rs[3 * h + 1] = pre[nb - 1, h]
                rs[3 * h + 2] = gleft[nb, h]
                dqs[h][...] += sum([dqc[b, h] for b in range(1, nb)], dqc[0, h])
            for b, (j, _, _) in enumerate(blocks):
                rows = pl.ds(pl.multiple_of(j * bq, bq), bq)
                dk_ref[rows, :] += dkc[b, 0] + dkc[b, 1]
                dv_ref[rows, :] += dvc[b, 0] + dvc[b, 1]

        def single(j, slot, masked):
            tiles([(j, slot, masked)])

        def wait(j):
            slot = lax.rem(j - j0, KV_SLOTS)
            for cp in fetch(j, slot):
                cp.wait()
            return slot

        _walk_up(fetch, j0, i, i, single, stop=jnp.maximum(i - 1, j0))

        @pl.when(j0 < i)
        def _():
            tiles([(i - 1, wait(i - 1), False), (i, wait(i), True)])

        @pl.when(j0 == i)
        def _():
            tiles([(i, wait(i), True)])

        dq_ref[...] = (jnp.where(is_a, dq_a[...], dq_b[...]) * SCALE).astype(BF16)

        @pl.when(i == nq - 1)
        def _():
            dk_out[...] = dk_ref[...].astype(BF16)
            dv_out[...] = dv_ref[...].astype(BF16)

    grid_spec = pltpu.PrefetchScalarGridSpec(
        num_scalar_prefetch=1, grid=(N_PAIRS, nq),
        in_specs=[pl.BlockSpec((bq, LANES), lambda p, i, jm: (i, col0 + p)),
                  pl.BlockSpec(memory_space=pl.ANY),
                  pl.BlockSpec((bq, LANES), lambda p, i, jm: (i, p)),
                  pl.BlockSpec((1, bq, 8), lambda p, i, jm: (p, i, 0)),
                  pl.BlockSpec((bq, bq), lambda p, i, jm: (0, 0)),
                  pl.BlockSpec((bq, bq), lambda p, i, jm: (0, 0))],
        out_specs=[pl.BlockSpec((bq, LANES), lambda p, i, jm: (i, p)),
                   pl.BlockSpec((S, LANES), lambda p, i, jm: (0, p)),
                   pl.BlockSpec((S, LANES), lambda p, i, jm: (0, p))],
        scratch_shapes=[pltpu.VMEM((bq, LANES), F32), pltpu.VMEM((bq, LANES), F32)]
        + [pltpu.VMEM((bq, LANES), BF16)] * 4 + [pltpu.VMEM((10, bq, 1), F32)]
        + [pltpu.VMEM((2 * KV_SLOTS, bq, LANES), BF16)] * 2 + [pltpu.SemaphoreType.DMA((2, 2 * KV_SLOTS))]
        + [pltpu.VMEM((S, LANES), F32)] * 2)
    return pl.pallas_call(
        body, name="sb_bwd", grid_spec=grid_spec,
        out_shape=[jax.ShapeDtypeStruct((S, GROUP_W), BF16)] * 3,
        compiler_params=_params(VMEM_BIG),
    )(jmin, proj, proj, do, st, tfwd, trev)


def _walk_up(fetch, j0, diag, last, tile, stop=None):
    ahead = KV_SLOTS - 1
    stop = last + 1 if stop is None else stop

    def start(j):
        @pl.when(j <= last)
        def _():
            for cp in fetch(j, lax.rem(j - j0, KV_SLOTS)):
                cp.start()

    for d in range(2, ahead):
        start(j0 + d)

    def step(j, carry):
        slot = lax.rem(j - j0, KV_SLOTS)
        for cp in fetch(j, slot):
            cp.wait()
        start(j + ahead)
        pl.when(j >= diag)(functools.partial(tile, j, slot, True))
        pl.when(j < diag)(functools.partial(tile, j, slot, False))
        return carry

    lax.fori_loop(j0, stop, step, 0)


def _causal(bq, bk, i, j):
    row = lax.broadcasted_iota(jnp.int32, (bq, bk), 0)
    col = lax.broadcasted_iota(jnp.int32, (bq, bk), 1)
    return col - row <= i * bq - j * bk


def _by_heads(j, first_a, first_b, heads):
    on_a, on_b = j >= first_a, j >= first_b
    pl.when(jnp.logical_and(on_a, on_b))(functools.partial(heads, (0, 1)))
    pl.when(jnp.logical_and(on_a, jnp.logical_not(on_b)))(functools.partial(heads, (0,)))
    pl.when(jnp.logical_and(on_b, jnp.logical_not(on_a)))(functools.partial(heads, (1,)))


def _fox_row_norms(proj, col0, tm):
    S = proj.shape[0]
    tm = min(tm, S)
    head_of = np.arange(GROUP_W) // HEAD_DIM
    he_t = jnp.asarray((np.arange(2 * N_PAIRS)[:, None] == head_of[None, :]).astype(np.float32), BF16)

    def body(q_ref, k_ref, he_ref, qn_ref, kn_ref, d_ref):
        q, k, he = q_ref[...].astype(F32), k_ref[...].astype(F32), he_ref[...]

        def head_sums_t(x):
            hi = x.astype(BF16)
            lo = (x - hi.astype(F32)).astype(BF16)
            return _dot(he, hi, _NT) + _dot(he, lo, _NT)

        qn_ref[...] = jnp.sqrt(head_sums_t(q * q))
        kn_ref[...] = jnp.sqrt(head_sums_t(k * k))
        d_ref[...] = SCALE * head_sums_t(q * k)

    wide = GROUP_W // LANES
    return pl.pallas_call(
        body, name="fox_row_norms", grid=(S // tm,),
        in_specs=[pl.BlockSpec((tm, GROUP_W), lambda i: (i, col0 // wide)),
                  pl.BlockSpec((tm, GROUP_W), lambda i: (i, (col0 + 4) // wide)),
                  pl.BlockSpec((2 * N_PAIRS, GROUP_W), lambda i: (0, 0))],
        out_specs=[pl.BlockSpec((2 * N_PAIRS, tm), lambda i: (0, i))] * 3,
        out_shape=[jax.ShapeDtypeStruct((2 * N_PAIRS, S), F32)] * 3)(proj, proj, he_t)


def _fox_start_blocks(qn, kn, d, c, bq, bk):
    nh, S = c.shape
    nq, nk = S // bq, S // bk
    top = SCALE * qn * kn.max(axis=1, keepdims=True) - d + c
    top = top.reshape(nh, nq, bq).max(axis=2)
    c_last = c[:, bk - 1::bk]
    live = top[:, :, None] - c_last[:, None, :] >= -FOX_SKIP

    def first_block(lv):
        first = jnp.where(lv.any(axis=2), jnp.argmax(lv, axis=2), nk)
        return jnp.minimum(first, (bq // bk) * jnp.arange(nq)[None, :]).astype(jnp.int32)

    return jnp.concatenate([first_block(live.reshape(N_PAIRS, 2, nq, nk).any(axis=1)), first_block(live)], axis=0)


def _fox_fwd(proj, col0, c_col, c_row, jstart, bq, bk):
    S = proj.shape[0]
    nq, per = S // bq, bq // bk

    def body(js_ref, q_ref, kv_hbm, cc_ref, cr_ref, o_ref, st_ref, acc_a, acc_b, qa, qb, ml, kbuf, vbuf, sems):
        p, i = pl.program_id(0), pl.program_id(1)
        j0 = js_ref[p, i]
        first_two = _first_two_up(lambda pair, blk: js_ref[pair, blk], per)
        fetch, kbuf, vbuf = _kv_fetcher(kv_hbm, kbuf, vbuf, sems, KV_SLOTS, col0, bk, p, i, nq, first_two)
        is_a = lax.broadcasted_iota(jnp.int32, (bq, LANES), 1) < HEAD_DIM
        acc_a[...] = jnp.zeros_like(acc_a)
        acc_b[...] = jnp.zeros_like(acc_b)
        ml[0] = jnp.full((bq, 1), NEG_BIG, F32)
        ml[2] = jnp.full((bq, 1), NEG_BIG, F32)
        ml[1] = jnp.zeros((bq, 1), F32)
        ml[3] = jnp.zeros((bq, 1), F32)
        cc = cc_ref[0]
        ml[4], ml[5] = _col(cc, 0), _col(cc, 1)
        qa[...], qb[...] = _masked_pair(q_ref[...], is_a, SCALE)

        def tile(j, slot, masked):
            k, v = kbuf[slot], vbuf[slot]
            cols = pl.ds(pl.multiple_of(j * bk, bk), bk)
            if masked:
                tri = _causal(bq, bk, i, j)

            def heads(hs):
                qs, accs = (qa, qb), (acc_a, acc_b)
                s = {h: _dot(qs[h][...], k, _NT) - cr_ref[0, pl.ds(h, 1), cols] for h in hs}
                if masked:
                    s = {h: jnp.where(tri, s[h], NEG_BIG) for h in hs}
                top = {h: jnp.max(s[h], axis=1, keepdims=True) for h in hs}
                m_new = {h: jnp.maximum(ml[2 * h], top[h] + ml[4 + h]) for h in hs}
                a = {h: jnp.exp(ml[2 * h] - m_new[h]) for h in hs}
                pr = {h: jnp.exp(s[h] - (m_new[h] - ml[4 + h])) for h in hs}
                tot = {h: jnp.sum(pr[h], axis=1, keepdims=True) for h in hs}
                pv = {h: _dot(pr[h].astype(BF16), v) for h in hs}
                for h in hs:
                    ml[2 * h] = m_new[h]
                    ml[2 * h + 1] = a[h] * ml[2 * h + 1] + tot[h]
                    accs[h][...] = a[h] * accs[h][...] + pv[h]

            _by_heads(j, js_ref[N_PAIRS + 2 * p, i], js_ref[N_PAIRS + 2 * p + 1, i], heads)

        _walk_up(fetch, j0, per * i, per * i + per - 1, tile)
        o_ref[...] = jnp.where(is_a, acc_a[...] / ml[1], acc_b[...] / ml[3])
        lane8 = lax.broadcasted_iota(jnp.int32, (bq, 8), 1)
        st = jnp.where(lane8 == 0, ml[0] + jnp.log(ml[1]), 0.0)
        st_ref[0] = jnp.where(lane8 == 1, ml[2] + jnp.log(ml[3]), st)

    grid_spec = pltpu.PrefetchScalarGridSpec(
        num_scalar_prefetch=1, grid=(N_PAIRS, nq),
        in_specs=[pl.BlockSpec((bq, LANES), lambda p, i, js: (i, col0 + p)),
                  pl.BlockSpec(memory_space=pl.ANY),
                  pl.BlockSpec((1, bq, 8), lambda p, i, js: (p, i, 0)),
                  pl.BlockSpec((1, 8, S), lambda p, i, js: (p, 0, 0))],
        out_specs=[pl.BlockSpec((bq, LANES), lambda p, i, js: (i, p)),
                   pl.BlockSpec((1, bq, 8), lambda p, i, js: (p, i, 0))],
        scratch_shapes=[pltpu.VMEM((bq, LANES), F32), pltpu.VMEM((bq, LANES), F32),
                        pltpu.VMEM((bq, LANES), BF16), pltpu.VMEM((bq, LANES), BF16),
                        pltpu.VMEM((6, bq, 1), F32),
                        pltpu.VMEM((2 * KV_SLOTS, bk, LANES), BF16), pltpu.VMEM((2 * KV_SLOTS, bk, LANES), BF16),
                        pltpu.SemaphoreType.DMA((2, 2 * KV_SLOTS))])
    return pl.pallas_call(
        body, name="fox_fwd", grid_spec=grid_spec,
        out_shape=[jax.ShapeDtypeStruct((S, GROUP_W), F32), jax.ShapeDtypeStruct((N_PAIRS, S, 8), F32)],
    )(jstart, proj, proj, c_col, c_row)


def _fox_bwd(proj, col0, do, o, st, c_col, c_row, jstart, bq, bk):
    S = proj.shape[0]
    nq, per = S // bq, bq // bk

    def body(js_ref, q_ref, kv_hbm, do_ref, o_ref, st_ref, cc_ref, cr_ref,
             dq_ref, dk_out, dv_out, dc_ref, dq_a, dq_b, qa, qb, doa, dob, dd, kbuf, vbuf, sems, dk_ref, dv_ref):
        p, i = pl.program_id(0), pl.program_id(1)
        j0 = js_ref[p, i]
        first_two = _first_two_up(lambda pair, blk: js_ref[pair, blk], per)
        fetch, kbuf, vbuf = _kv_fetcher(kv_hbm, kbuf, vbuf, sems, KV_SLOTS, col0, bk, p, i, nq, first_two)
        is_a = lax.broadcasted_iota(jnp.int32, (bq, LANES), 1) < HEAD_DIM

        @pl.when(i == 0)
        def _():
            dk_ref[...] = jnp.zeros_like(dk_ref)
            dv_ref[...] = jnp.zeros_like(dv_ref)
            dc_ref[...] = jnp.zeros_like(dc_ref)

        dq_a[...] = jnp.zeros_like(dq_a)
        dq_b[...] = jnp.zeros_like(dq_b)
        qa[...], qb[...] = _masked_pair(q_ref[...], is_a, SCALE)
        dov = do_ref[...]
        doa[...], dob[...] = _masked_pair(dov, is_a)
        prod = dov * o_ref[...]
        dd[0] = jnp.sum(jnp.where(is_a, prod, 0.0), axis=1, keepdims=True)
        dd[1] = jnp.sum(jnp.where(is_a, 0.0, prod), axis=1, keepdims=True)
        dd[2] = jnp.zeros((bq, 1), F32)
        dd[3] = jnp.zeros((bq, 1), F32)
        cc, st_v = cc_ref[0], st_ref[0]
        dd[4], dd[5] = _col(cc, 0) - _col(st_v, 0), _col(cc, 1) - _col(st_v, 1)

        def tile(j, slot, masked):
            k, v = kbuf[slot], vbuf[slot]
            if masked:
                tri = _causal(bq, bk, i, j)
            cols = pl.ds(pl.multiple_of(j * bk, bk), bk)

            def heads(hs):
                qs, dos, dqs = (qa, qb), (doa, dob), (dq_a, dq_b)
                z = {h: _dot(qs[h][...], k, _NT) for h in hs}
                dp = {h: _dot(dos[h][...], v, _NT) for h in hs}
                pr = {h: jnp.exp(z[h] - cr_ref[0, pl.ds(h, 1), cols] + dd[4 + h]) for h in hs}
                if masked:
                    pr = {h: jnp.where(tri, pr[h], 0.0) for h in hs}
                ds = {h: pr[h] * (dp[h] - dd[h]) for h in hs}
                csum = {h: jnp.sum(ds[h], axis=0, keepdims=True) for h in hs}
                rsum = {h: jnp.sum(ds[h], axis=1, keepdims=True) for h in hs}
                dsb = {h: ds[h].astype(BF16) for h in hs}
                prb = {h: pr[h].astype(BF16) for h in hs}
                dqc = {h: _dot(dsb[h], k) for h in hs}
                dkc = [_dot(dsb[h], qs[h][...], _TN) for h in hs]
                dvc = [_dot(prb[h], dos[h][...], _TN) for h in hs]
                for h in hs:
                    dc_ref[0, pl.ds(h, 1), cols] -= csum[h]
                    dd[2 + h] += rsum[h]
                    dqs[h][...] += dqc[h]
                dk_ref[cols, :] += sum(dkc[1:], dkc[0])
                dv_ref[cols, :] += sum(dvc[1:], dvc[0])

            _by_heads(j, js_ref[N_PAIRS + 2 * p, i], js_ref[N_PAIRS + 2 * p + 1, i], heads)

        _walk_up(fetch, j0, per * i, per * i + per - 1, tile)
        dq_ref[...] = (jnp.where(is_a, dq_a[...], dq_b[...]) * SCALE).astype(BF16)
        eye = lax.broadcasted_iota(jnp.int32, (bq, bq), 0) == lax.broadcasted_iota(jnp.int32, (bq, bq), 1)
        own = pl.ds(pl.multiple_of(i * bq, bq), bq)
        for h in range(2):
            dc_ref[0, pl.ds(h, 1), own] += jnp.sum(jnp.where(eye, dd[2 + h], 0.0), axis=0, keepdims=True)

        @pl.when(i == nq - 1)
        def _():
            dk_out[...] = dk_ref[...].astype(BF16)
            dv_out[...] = dv_ref[...].astype(BF16)

    grid_spec = pltpu.PrefetchScalarGridSpec(
        num_scalar_prefetch=1, grid=(N_PAIRS, nq),
        in_specs=[pl.BlockSpec((bq, LANES), lambda p, i, js: (i, col0 + p)),
                  pl.BlockSpec(memory_space=pl.ANY),
                  pl.BlockSpec((bq, LANES), lambda p, i, js: (i, p)),
                  pl.BlockSpec((bq, LANES), lambda p, i, js: (i, p)),
                  pl.BlockSpec((1, bq, 8), lambda p, i, js: (p, i, 0)),
                  pl.BlockSpec((1, bq, 8), lambda p, i, js: (p, i, 0)),
                  pl.BlockSpec((1, 8, S), lambda p, i, js: (p, 0, 0))],
        out_specs=[pl.BlockSpec((bq, LANES), lambda p, i, js: (i, p)),
                   pl.BlockSpec((S, LANES), lambda p, i, js: (0, p)),
                   pl.BlockSpec((S, LANES), lambda p, i, js: (0, p)),
                   pl.BlockSpec((1, 8, S), lambda p, i, js: (p, 0, 0))],
        scratch_shapes=[pltpu.VMEM((bq, LANES), F32), pltpu.VMEM((bq, LANES), F32)]
        + [pltpu.VMEM((bq, LANES), BF16)] * 4 + [pltpu.VMEM((6, bq, 1), F32)]
        + [pltpu.VMEM((2 * KV_SLOTS, bk, LANES), BF16)] * 2 + [pltpu.SemaphoreType.DMA((2, 2 * KV_SLOTS))]
        + [pltpu.VMEM((S, LANES), F32)] * 2)
    return pl.pallas_call(
        body, name="fox_bwd", grid_spec=grid_spec,
        out_shape=[jax.ShapeDtypeStruct((S, GROUP_W), BF16)] * 3 + [jax.ShapeDtypeStruct((N_PAIRS, 8, S), F32)],
        compiler_params=_params(VMEM_BIG),
    )(jstart, proj, proj, do, o, st, c_col, c_row)


_HBM = pl.BlockSpec(memory_space=pltpu.HBM)


def _coords():
    return lax.axis_index("x"), lax.axis_index("y"), lax.axis_index("c")


def _gather_copies(ins, outs, send_sems, recv_sems, loc_sems):
    n = len(ins)
    x, y, c = _coords()
    mine = 2 * x + y
    chips = [(1 - x, y), (x, 1 - y), (1 - x, 1 - y)]

    def copy(w, r, slab, to):
        return pltpu.make_async_remote_copy(
            src_ref=ins[w], dst_ref=outs[w].at[slab], send_sem=send_sems.at[3 * w + r],
            recv_sem=recv_sems.at[3 * w + r], device_id=to, device_id_type=MESH)

    def own():
        local = [pltpu.make_async_copy(ins[w], outs[w].at[mine], loc_sems.at[w]) for w in range(n)]
        return local, [copy(w, r, mine, (cx, cy, c)) for w in range(n) for r, (cx, cy) in enumerate(chips)]

    def start():
        local, sends = own()
        for cp in local + sends:
            cp.start()

    def wait():
        local, sends = own()
        for w in range(n):
            for r, (cx, cy) in enumerate(chips):
                copy(w, r, 2 * cx + cy, (cx, cy, c)).wait_recv()
        for cp in sends:
            cp.wait_send()
        for cp in local:
            cp.wait()

    return start, wait


def _gather_shapes(shards):
    n = len(shards)
    return ([jax.ShapeDtypeStruct((4,) + s.shape, s.dtype) for s in shards],
            [pltpu.SemaphoreType.DMA((3 * n,)), pltpu.SemaphoreType.DMA((3 * n,)), pltpu.SemaphoreType.DMA((n,))])


def _allgather_chips(shards):
    n = len(shards)

    def body(*refs):
        start, wait = _gather_copies(refs[:n], refs[n:2 * n], *refs[2 * n:])
        start()
        wait()

    out_shape, sems = _gather_shapes(shards)
    return pl.pallas_call(body, name="allgather_weights", in_specs=[_HBM] * n, out_specs=[_HBM] * n,
                          out_shape=out_shape, scratch_shapes=sems)(*shards)


def _proj_gather(x, w, shards, tm, tn):
    (M, K), N, n = x.shape, w.shape[1], len(shards)
    tm = min(tm, M)
    gi, gj = M // tm, N // tn

    def body(a_ref, b_ref, *rest):
        o_ref = rest[n]
        start, wait = _gather_copies(rest[:n], rest[n + 1:2 * n + 1], *rest[2 * n + 1:])
        i, j = pl.program_id(0), pl.program_id(1)
        pl.when(jnp.logical_and(i == 0, j == 0))(start)
        o_ref[...] = _dot(a_ref[...].astype(BF16), b_ref[...]).astype(o_ref.dtype)
        pl.when(jnp.logical_and(i == gi - 1, j == gj - 1))(wait)

    out_shape, sems = _gather_shapes(shards)
    return pl.pallas_call(
        body, name="proj_gather", grid=(gi, gj),
        in_specs=[pl.BlockSpec((tm, K), lambda i, j: (i, 0)), pl.BlockSpec((K, tn), lambda i, j: (0, j))] + [_HBM] * n,
        out_specs=[pl.BlockSpec((tm, tn), lambda i, j: (i, j))] + [_HBM] * n,
        out_shape=[jax.ShapeDtypeStruct((M, N), BF16)] + out_shape, scratch_shapes=sems,
    )(x, w, *shards)


def _exchange_copies(ins, outs, send_sems, recv_sems, loc_sems, per_chip, parts):
    n = len(parts)
    half = [p.shape[1] // 2 for p in parts] if per_chip else None
    x, y, c = _coords()
    me = 4 * x + 2 * y + c
    peers = [(x ^ fx, y ^ fy, c ^ fc) for fx in (0, 1) for fy in (0, 1) for fc in (0, 1)][1:]

    def src(w, dev):
        if not per_chip:
            return ins[w]
        return ins[w].at[2 * dev[0] + dev[1], pl.ds(pl.multiple_of(dev[2] * half[w], 16), half[w]), :]

    def copy(w, r, source, slab, to):
        return pltpu.make_async_remote_copy(
            src_ref=source, dst_ref=outs[w].at[slab], send_sem=send_sems.at[7 * w + r],
            recv_sem=recv_sems.at[7 * w + r], device_id=to, device_id_type=MESH)

    def own():
        local = [pltpu.make_async_copy(src(w, (x, y, c)), outs[w].at[me], loc_sems.at[w]) for w in range(n)]
        return local, [copy(w, r, src(w, dev), me, dev) for w in range(n) for r, dev in enumerate(peers)]

    def start():
        local, sends = own()
        for cp in local + sends:
            cp.start()

    def wait():
        local, sends = own()
        for w in range(n):
            for r, dev in enumerate(peers):
                copy(w, r, src(w, dev), 4 * dev[0] + 2 * dev[1] + dev[2], dev).wait_recv()
        for cp in sends:
            cp.wait_send()
        for cp in local:
            cp.wait()

    return start, wait


def _exchange_shapes(parts, per_chip):
    n = len(parts)
    return ([jax.ShapeDtypeStruct((8, p.shape[1] // 2, p.shape[2]) if per_chip else (8,) + p.shape, p.dtype)
             for p in parts],
            [pltpu.SemaphoreType.DMA((7 * n,)), pltpu.SemaphoreType.DMA((7 * n,)), pltpu.SemaphoreType.DMA((n,))])


def _exchange(parts, per_chip):
    n = len(parts)

    def body(*refs):
        start, wait = _exchange_copies(refs[:n], refs[n:2 * n], *refs[2 * n:], per_chip, parts)
        start()
        wait()

    out_shape, sems = _exchange_shapes(parts, per_chip)
    return pl.pallas_call(body, name="exchange_per_chip" if per_chip else "exchange_all",
                          in_specs=[_HBM] * n, out_specs=[_HBM] * n, out_shape=out_shape, scratch_shapes=sems)(*parts)


def _sibling_swap(halves):
    n = len(halves)

    def body(*refs):
        ins, outs = refs[:n], refs[n:2 * n]
        send_sems, recv_sems, loc_sems = refs[2 * n:]
        x, y, c = _coords()

        def rows(w, core):
            rh = halves[w].shape[0]
            return outs[w].at[pl.ds(pl.multiple_of(core * rh, 8), rh), :]

        def copy(w, core):
            return pltpu.make_async_remote_copy(
                src_ref=ins[w], dst_ref=rows(w, core), send_sem=send_sems.at[w], recv_sem=recv_sems.at[w],
                device_id=(x, y, 1 - c), device_id_type=MESH)

        local = [pltpu.make_async_copy(ins[w], rows(w, c), loc_sems.at[w]) for w in range(n)]
        sends = [copy(w, c) for w in range(n)]
        for cp in local + sends:
            cp.start()
        for w in range(n):
            copy(w, 1 - c).wait_recv()
        for cp in sends:
            cp.wait_send()
        for cp in local:
            cp.wait()

    vmem = pl.BlockSpec(memory_space=pltpu.VMEM)
    return pl.pallas_call(
        body, name="sibling_swap", in_specs=[vmem] * n, out_specs=[vmem] * n,
        out_shape=[jax.ShapeDtypeStruct((2 * h.shape[0], h.shape[1]), h.dtype) for h in halves],
        scratch_shapes=[pltpu.SemaphoreType.DMA((n,)), pltpu.SemaphoreType.DMA((n,)), pltpu.SemaphoreType.DMA((n,))],
    )(*halves)


def _adamw(w, g, m, v):
    m = ADAM_B1 * m + (1.0 - ADAM_B1) * g
    v = ADAM_B2 * v + (1.0 - ADAM_B2) * (g * g)
    m_hat = m / (1.0 - ADAM_B1 ** ADAM_STEP)
    v_hat = v / (1.0 - ADAM_B2 ** ADAM_STEP)
    delta = -ADAM_LR * (m_hat / (jnp.sqrt(v_hat) + ADAM_EPS) + ADAM_WD * w)
    return delta, m, v


def _sum_parts(parts, name, tr):
    _, R, C = parts.shape
    assert R % tr == 0

    def body(p_ref, g_ref):
        g = p_ref[0].astype(F32)
        for d in range(1, 8):
            g = g + p_ref[d].astype(F32)
        g_ref[...] = g

    return pl.pallas_call(
        body, name=name, grid=(R // tr,),
        in_specs=[pl.BlockSpec((8, tr, C), lambda i: (0, i, 0))],
        out_specs=pl.BlockSpec((tr, C), lambda i: (i, 0)), out_shape=jax.ShapeDtypeStruct((R, C), F32),
    )(parts)


def _adamw_call(g, w, m, v, name, tr):
    R, C = w.shape
    assert R % tr == 0

    def body(g_ref, w_ref, m_ref, v_ref, d_ref, nm_ref, nv_ref):
        d_ref[...], nm_ref[...], nv_ref[...] = _adamw(w_ref[...], g_ref[...], m_ref[...], v_ref[...])

    tile = pl.BlockSpec((tr, C), lambda i: (i, 0))
    return pl.pallas_call(
        body, name=name, grid=(R // tr,), in_specs=[tile] * 4,
        out_specs=[tile] * 3, out_shape=[jax.ShapeDtypeStruct((R, C), F32)] * 3,
    )(g, w, m, v)


def _sum_adamw_small(parts, w, m, v):
    def body(p_ref, w_ref, m_ref, v_ref, g_ref, d_ref, nm_ref, nv_ref, loss_ref):
        g = p_ref[0]
        for d in range(1, 8):
            g = g + p_ref[d]
        g_ref[...] = g
        d_ref[...], nm_ref[...], nv_ref[...] = _adamw(w_ref[...], g, m_ref[...], v_ref[...])
        row = lax.broadcasted_iota(jnp.int32, g.shape, 0)
        per_row = jnp.sum(jnp.where(row == 6, g, 0.0), axis=1, keepdims=True)
        loss_ref[...] = jnp.zeros((8, LANES), F32) + jnp.sum(per_row, axis=0, keepdims=True)

    return pl.pallas_call(
        body, name="sum_adamw_small",
        out_shape=[jax.ShapeDtypeStruct((8, D_MODEL), F32)] * 4 + [jax.ShapeDtypeStruct((8, LANES), F32)],
    )(parts, w, m, v)


def _pack_small(ln1_g, ln1_b, ln2_g, ln2_b, g_sb, g_fox, b_f):
    row5 = jnp.pad(b_f.reshape(1, N_FOX), ((0, 0), (0, D_MODEL - N_FOX)))
    rows = [ln1_g.reshape(1, -1), ln1_b.reshape(1, -1), ln2_g.reshape(1, -1), ln2_b.reshape(1, -1),
            jnp.concatenate([g_sb.reshape(1, -1), g_fox.reshape(1, -1)], axis=1), row5,
            jnp.zeros((2, D_MODEL), F32)]
    return jnp.concatenate(rows, axis=0)


def _unpack_small(p):
    return {"ln1_g": p[0:1], "ln1_b": p[1:2], "ln2_g": p[2:3], "ln2_b": p[3:4], "g_sb": p[4:5, :GROUP_W],
            "g_fox": p[4:5, GROUP_W:], "b_f": p[5:6, :N_FOX]}


def kernel(x, w_in, b_f, g_sb, g_fox, w_out, ln1_g, ln1_b, ln2_g, ln2_b, w_gate_up, w_down, loss_target, m_w_in, m_b_f, m_g_sb, m_g_fox, m_w_out, m_ln1_g, m_ln1_b, m_ln2_g, m_ln2_b, m_w_gate_up, m_w_down, v_w_in, v_b_f, v_g_sb, v_g_fox, v_w_out, v_ln1_g, v_ln1_b, v_ln2_g, v_ln2_b, v_w_gate_up, v_w_down):
    S = x.shape[1]
    x2 = x.reshape(S, D_MODEL)
    tgt = loss_target.reshape(S, D_MODEL)
    TM = 1024
    TR = 512
    BQ = ATTN_BLOCK
    in_w = w_in.shape[2]
    gu_w = w_gate_up.shape[2]

    shards = [w_in[0].astype(BF16), w_out[0].astype(BF16), w_gate_up[0].astype(BF16), w_down[0].astype(BF16)]
    (wi_s,) = _allgather_chips(shards[:1])
    wi = wi_s.transpose(1, 0, 2).reshape(D_MODEL, 4 * in_w)
    w_sb, w_fx = wi[:, :QKV_W // 2], wi[:, QKV_W // 2:QKV_W]
    wqkv = wi[:, :QKV_W]
    wft = wi[:, QKV_W:].T
    proj, wo_s, wgu_s, wd_s = _proj_gather(x2, wqkv, shards[1:], TM, 512)
    wo = wo_s.reshape(D_MODEL, D_MODEL)
    wgu = wgu_s.transpose(1, 0, 2).reshape(D_MODEL, 2 * D_FF)
    wg, wu = wgu[:, :D_FF], wgu[:, D_FF:]
    wd = wd_s.reshape(D_FF, D_MODEL)
    g_row = jnp.concatenate([g_sb, g_fox], axis=1)
    hid = np.arange(D_MODEL) // HEAD_DIM
    he_np = (hid[:, None] == np.arange(LANES)[None, :]).astype(np.float32)
    he, het = jnp.asarray(he_np, BF16), jnp.asarray(he_np.T, BF16)

    lf = _fgate_fwd(x2, wft, b_f.reshape(N_FOX, 1), TM)
    c = _cumsum_fwd(lf)
    c_pair = c.reshape(N_PAIRS, 2, S)
    c_row = jnp.pad(c_pair, ((0, 0), (0, 6), (0, 0)))
    c_col = jnp.pad(c_pair.transpose(0, 2, 1), ((0, 0), (0, 0), (0, 6)))

    o_sb, st_sb, jmin_sb = _sb_fwd(proj, 0, BQ)
    jstart_fx = _fox_start_blocks(*_fox_row_norms(proj, 12, TR), c, BQ, BQ)
    o_fx, st_fx = _fox_fwd(proj, 12, c_col, c_row, jstart_fx, BQ, BQ)

    def attn_post(i, osb_ref, ofx_ref, g_ref, he_ref, het_ref, on_ref):
        o = jnp.concatenate([osb_ref[...], ofx_ref[...]], axis=1)
        ms = _head_sums(o * o, he_ref[...], het_ref[...]) * (1.0 / HEAD_DIM)
        on_ref[...] = (o * lax.rsqrt(ms + RMS_EPS) * g_ref[...]).astype(BF16)

    (on,) = _rowwise(attn_post, "attn_post", S, TR,
                     [(o_sb, "t"), (o_fx, "t"), (g_row, "f"), (he, "f"), (het, "f")],
                     [((S, D_MODEL), BF16, "t")])

    u1 = _matmul(on, wo, mode="nn", name="mix", tm=TM, tn=D_MODEL, tk=D_MODEL, outs=[F32],
                 extras=[(x2, (TM if S >= TM else S, D_MODEL), _tile_ij)],
                 epilogue=lambda acc, xv: (ALPHA * xv + acc,))

    def ln1_fwd(i, u_ref, g_ref, b_ref, h_ref):
        xh, _ = _ln_stats(u_ref[...])
        h_ref[...] = xh * g_ref[...] + b_ref[...]

    (h1,) = _rowwise(ln1_fwd, "ln1_fwd", S, TR, [(u1, "t"), (ln1_g, "f"), (ln1_b, "f")], [((S, D_MODEL), F32, "t")])

    tm_e = TM if S >= TM else S
    n_ff = D_FF // 256

    def gate_up_body(h_ref, wg_ref, wu_ref, g_ref, u_ref, a_ref):
        h = h_ref[...].astype(BF16)
        g, u = _dot(h, wg_ref[...]), _dot(h, wu_ref[...])
        g_ref[...] = g.astype(BF16)
        u_ref[...] = u.astype(BF16)
        a_ref[...] = (g * _sigmoid(g) * u).astype(BF16)

    ff_tile = pl.BlockSpec((tm_e, 256), lambda i, j: (i, j))
    gate, up, act = pl.pallas_call(
        gate_up_body, name="gate_up_act", grid=(S // tm_e, n_ff),
        in_specs=[pl.BlockSpec((tm_e, D_MODEL), lambda i, j: (i, 0)),
                  pl.BlockSpec((D_MODEL, 256), lambda i, j: (0, j)),
                  pl.BlockSpec((D_MODEL, 256), lambda i, j: (0, j + n_ff))],
        out_specs=[ff_tile] * 3, out_shape=[jax.ShapeDtypeStruct((S, D_FF), BF16)] * 3)(h1, wgu, wgu)

    u2 = _matmul(act, wd, mode="nn", name="ffn_down", tm=TM, tn=D_MODEL, tk=D_FF, outs=[F32],
                 extras=[(h1, (TM if S >= TM else S, D_MODEL), _tile_ij)],
                 epilogue=lambda acc, hv: (ALPHA * hv + acc,))

    def ln2_loss(i, u_ref, t_ref, g_ref, b_ref, du_ref, acc_ref):
        xh, r = _ln_stats(u_ref[...])
        g = g_ref[...]
        err = xh * g + b_ref[...] - t_ref[...]
        dy = err * (1.0 / D_MODEL)
        du_ref[...] = _ln_bwd(dy, xh, r, g)
        _acc_rows(i, acc_ref, {2: jnp.sum(dy * xh, axis=0, keepdims=True), 3: jnp.sum(dy, axis=0, keepdims=True),
                               6: jnp.sum(err * err, axis=0, keepdims=True) * (0.5 / D_MODEL)})

    du2, acc_ln2 = _rowwise(ln2_loss, "ln2_loss", S, TR, [(u2, "t"), (tgt, "t"), (ln2_g, "f"), (ln2_b, "f")],
                            [((S, D_MODEL), F32, "t"), ((8, D_MODEL), F32, "f")])

    d_wd = _matmul(act, du2, mode="tn", name="dw_down", tm=1408, tn=D_MODEL, tk=TM, outs=[BF16])

    def dgu_epilogue(da, g, u):
        g, u = g.astype(F32), u.astype(F32)
        s = _sigmoid(g)
        return da * u * (s * (1.0 + g * (1.0 - s))), da * (g * s)

    dgate, dup = _matmul(du2, wd, mode="nt", name="d_act", tm=TM, tn=1408, tk=D_MODEL, outs=[BF16, BF16],
                         extras=[(gate, (tm_e, 1408), _tile_ij), (up, (tm_e, 1408), _tile_ij)],
                         epilogue=dgu_epilogue)
    d_wg = _matmul(h1, dgate, mode="tn", name="dw_gate", tm=D_MODEL, tn=1408, tk=TM, outs=[BF16])
    d_wu = _matmul(h1, dup, mode="tn", name="dw_up", tm=D_MODEL, tn=1408, tk=TM, outs=[BF16])
    d_wgu = jnp.concatenate([d_wg, d_wu], axis=1)
    dh1, got_down = _matmul(dgate, wg, mode="nt", name="dh1_gate", tm=TM, tn=D_MODEL, tk=D_FF, outs=[F32],
                            extras=[(du2, (tm_e, D_MODEL), _tile_ij)], epilogue=lambda acc, e: (ALPHA * e + acc,),
                            hosted=[d_wd.reshape(4, D_FF // 4, D_MODEL)])
    dh1, got_gu = _matmul(dup, wu, mode="nt", name="dh1_up", tm=TM, tn=D_MODEL, tk=D_FF, outs=[F32],
                          extras=[(dh1, (tm_e, D_MODEL), _tile_ij)], epilogue=lambda acc, e: (e + acc,),
                          hosted=[d_wgu.reshape(D_MODEL, 4, gu_w).transpose(1, 0, 2)])

    def ln1_bwd(i, dh_ref, u_ref, g_ref, du_ref, acc_ref):
        xh, r = _ln_stats(u_ref[...])
        dh = dh_ref[...]
        du_ref[...] = _ln_bwd(dh, xh, r, g_ref[...])
        _acc_rows(i, acc_ref, {0: jnp.sum(dh * xh, axis=0, keepdims=True), 1: jnp.sum(dh, axis=0, keepdims=True)})

    du1, acc_ln1 = _rowwise(ln1_bwd, "ln1_bwd", S, TR, [(dh1, "t"), (u1, "t"), (ln1_g, "f")],
                            [((S, D_MODEL), F32, "t"), ((8, D_MODEL), F32, "f")])
    d_wo = _matmul(on, du1, mode="tn", name="dw_out", tm=D_MODEL, tn=D_MODEL, tk=TM, outs=[BF16])
    don, got_out = _matmul(du1, wo, mode="nt", name="d_on", tm=TM, tn=D_MODEL, tk=D_MODEL, outs=[F32],
                           hosted=[d_wo.reshape(4, D_MODEL // 4, D_MODEL)])

    def rms_bwd(i, don_ref, osb_ref, ofx_ref, g_ref, he_ref, het_ref, dosb_ref, dofx_ref, acc_ref):
        o = jnp.concatenate([osb_ref[...], ofx_ref[...]], axis=1)
        hev, hetv = he_ref[...], het_ref[...]
        r = lax.rsqrt(_head_sums(o * o, hev, hetv) * (1.0 / HEAD_DIM) + RMS_EPS)
        dn = don_ref[...]
        dg = dn * g_ref[...]
        do = r * dg - o * (r * r * r) * (_head_sums(dg * o, hev, hetv) * (1.0 / HEAD_DIM))
        dosb_ref[...] = do[:, :GROUP_W]
        dofx_ref[...] = do[:, GROUP_W:]
        _acc_rows(i, acc_ref, {4: jnp.sum(dn * o * r, axis=0, keepdims=True)})

    do_sb, do_fx, acc_rms = _rowwise(
        rms_bwd, "rms_bwd", S, TR, [(don, "t"), (o_sb, "t"), (o_fx, "t"), (g_row, "f"), (he, "f"), (het, "f")],
        [((S, GROUP_W), F32, "t"), ((S, GROUP_W), F32, "t"), ((8, D_MODEL), F32, "f")])

    dq_sb, dk_sb, dv_sb = _sb_bwd(proj, 0, do_sb, st_sb, jmin_sb, BQ)
    jstart_fx2 = jnp.minimum(jstart_fx[:, 0::2], jstart_fx[:, 1::2])
    dq_fx, dk_fx, dv_fx, dc = _fox_bwd(proj, 12, do_fx, o_fx, st_fx, c_col, c_row, jstart_fx2, 2 * BQ, BQ)
    dfl, dbf = _fgate_bwd(dc[:, :2, :].reshape(N_FOX, S), lf)
    dp_sb = jnp.concatenate([dq_sb, dk_sb, dv_sb], axis=1)
    dp_fx = jnp.concatenate([dq_fx, dk_fx, dv_fx], axis=1)

    d_wsb = _matmul(x2, dp_sb, mode="tn", name="dw_in_sb", tm=D_MODEL, tn=QKV_W // 2, tk=TM, outs=[BF16])
    d_wfx = _matmul(x2, dp_fx, mode="tn", name="dw_in_fx", tm=D_MODEL, tn=QKV_W // 2, tk=TM, outs=[BF16])
    d_wft = _matmul(dfl, x2, mode="nn", name="dw_in_f", tm=N_FOX, tn=D_MODEL, tk=TM, outs=[BF16])
    d_wi = jnp.concatenate([d_wsb, d_wfx, d_wft.T], axis=1)
    dx, got_in = _matmul(dp_sb, w_sb, mode="nt", name="dx_sb", tm=TM, tn=D_MODEL, tk=QKV_W // 2, outs=[F32],
                         extras=[(du1, (tm_e, D_MODEL), _tile_ij)], epilogue=lambda acc, e: (ALPHA * e + acc,),
                         hosted=[d_wi.reshape(D_MODEL, 4, in_w).transpose(1, 0, 2)])
    dx = _matmul(dp_fx, w_fx, mode="nt", name="dx_fx", tm=TM, tn=D_MODEL, tk=QKV_W // 2, outs=[F32],
                 extras=[(dx, (tm_e, D_MODEL), _tile_ij)], epilogue=lambda acc, e: (e + acc,))
    dx = _matmul(dfl, wft, mode="tn", name="dx_f", tm=TM, tn=D_MODEL, tk=N_FOX, outs=[F32],
                 extras=[(dx, (tm_e, D_MODEL), _tile_ij)], epilogue=lambda acc, e: (e + acc,))

    got = [got_in, got_out, got_gu, got_down]
    big_names = ("w_in", "w_out", "w_gate_up", "w_down")
    halves = [_sum_parts(p, "sum_" + nm, tr) for nm, p, tr in zip(big_names, got, (256, 128, 128, 176))]
    grads = _sibling_swap(halves)
    big = {}
    for nm, g, w, m, v, tr in zip(big_names, grads, (w_in, w_out, w_gate_up, w_down),
                                  (m_w_in, m_w_out, m_w_gate_up, m_w_down),
                                  (v_w_in, v_w_out, v_w_gate_up, v_w_down), (256, 256, 256, 176)):
        big[nm] = [r[None] for r in [g] + list(_adamw_call(g, w[0], m[0], v[0], "adamw_" + nm, tr))]

    small = acc_ln2 + acc_ln1 + acc_rms
    small = small + jnp.pad(dbf.reshape(1, N_FOX), ((5, 2), (0, D_MODEL - N_FOX)))
    (small_all,) = _exchange([small], False)
    sw = _pack_small(ln1_g, ln1_b, ln2_g, ln2_b, g_sb, g_fox, b_f)
    sm = _pack_small(m_ln1_g, m_ln1_b, m_ln2_g, m_ln2_b, m_g_sb, m_g_fox, m_b_f)
    sv = _pack_small(v_ln1_g, v_ln1_b, v_ln2_g, v_ln2_b, v_g_sb, v_g_fox, v_b_f)
    sg, sd, snm, snv, loss_blk = _sum_adamw_small(small_all, sw, sm, sv)
    sg, sd, snm, snv = _unpack_small(sg), _unpack_small(sd), _unpack_small(snm), _unpack_small(snv)

    names = ["w_in", "b_f", "g_sb", "g_fox", "w_out", "ln1_g", "ln1_b", "ln2_g", "ln2_b", "w_gate_up", "w_down"]
    outs = [loss_blk[0, 0], dx.reshape(1, S, D_MODEL)]
    for k, table in enumerate((sg, sd, snm, snv)):
        outs += [big[n][k] if n in big else table[n] for n in names]
    return tuple(outs)
```

```python
import functools

import numpy as np
import jax
import jax.numpy as jnp
from jax import lax
from jax.experimental import pallas as pl
from jax.experimental.pallas import tpu as pltpu

F32 = jnp.float32
BF16 = jnp.bfloat16

D_MODEL = 1024
HEAD_DIM = 64
LANES = 128
N_PAIRS = 4
GROUP_W = 512
QKV_W = 3072
D_FF = 2816
N_FOX = 8
ALPHA = 2.0 ** 0.25
LN_EPS = 1e-5
RMS_EPS = 1e-6
SCALE = HEAD_DIM ** -0.5
NEG_BIG = -1e30
FOX_SKIP = 30.0
SB_STOP = -105.0
ADAM_LR, ADAM_B1, ADAM_B2, ADAM_EPS, ADAM_WD, ADAM_STEP = 0.001, 0.9, 0.999, 1e-08, 0.01, 10
KV_SLOTS = 4
SCAN_GROUP = 8
ATTN_BLOCK = 256
VMEM_BIG = 56 * 1024 * 1024
MESH = pl.DeviceIdType.MESH

_NN = (((1,), (0,)), ((), ()))
_NT = (((1,), (1,)), ((), ()))
_TN = (((0,), (0,)), ((), ()))


def _dot(a, b, dims=_NN):
    return lax.dot_general(a, b, dims, preferred_element_type=F32)


def _split_dot(x, t):
    hi = x.astype(BF16)
    lo = (x - hi.astype(F32)).astype(BF16)
    return _dot(hi, t) + _dot(lo, t)


def _softplus(z):
    return jnp.maximum(z, 0.0) + jnp.log1p(jnp.exp(-jnp.abs(z)))


def _sigmoid(x):
    return 0.5 * jnp.tanh(0.5 * x) + 0.5


def _col(v, h):
    lane = lax.broadcasted_iota(jnp.int32, v.shape, 1)
    return jnp.sum(jnp.where(lane == h, v, 0.0), axis=1, keepdims=True)


def _two_sum(hi, lo, b):
    s = hi + b
    bb = s - hi
    err = (hi - (s - bb)) + (b - bb)
    return s, lo + err


def _params(vmem=None):
    return pltpu.CompilerParams(vmem_limit_bytes=vmem) if vmem else None


def _matmul(a, b, *, mode, name, tm, tn, tk, outs, extras=(), epilogue=None, vmem=None, hosted=()):
    if mode == "nn":
        (M, K), (_, N) = a.shape, b.shape
    elif mode == "nt":
        (M, K), (N, _) = a.shape, b.shape
    else:
        (K, M), (_, N) = a.shape, b.shape
    tm, tn, tk = min(tm, M), min(tn, N), min(tk, K)
    assert M % tm == 0 and N % tn == 0 and K % tk == 0, (name, M, N, K, tm, tn, tk)
    nk = K // tk
    dims = {"nn": _NN, "nt": _NT, "tn": _TN}[mode]
    if mode == "tn":
        a_spec = pl.BlockSpec((tk, tm), lambda i, j, k: (k, i))
    else:
        a_spec = pl.BlockSpec((tm, tk), lambda i, j, k: (i, k))
    if mode == "nt":
        b_spec = pl.BlockSpec((tn, tk), lambda i, j, k: (j, k))
    else:
        b_spec = pl.BlockSpec((tk, tn), lambda i, j, k: (k, j))
    ex_specs = [pl.BlockSpec(bs, (lambda i, j, k, f=f: f(i, j))) for (_, bs, f) in extras]
    ne, no, nh = len(extras), len(outs), len(hosted)
    if epilogue is None:
        epilogue = lambda acc: (acc,)
    gi, gj = M // tm, N // tn
    host_shapes, host_sems = _exchange_shapes(hosted, True) if nh else ([], [])

    def body(a_ref, b_ref, *rest):
        ex_refs, host_ins = rest[:ne], rest[ne:ne + nh]
        out_refs, host_outs = rest[ne + nh:ne + nh + no], rest[ne + nh + no:ne + 2 * nh + no]
        acc = rest[ne + 2 * nh + no]
        i, j, k = pl.program_id(0), pl.program_id(1), pl.program_id(2)
        if nh:
            start, wait = _exchange_copies(host_ins, host_outs, *rest[ne + 2 * nh + no + 1:], True, hosted)
            pl.when(jnp.logical_and(jnp.logical_and(i == 0, j == 0), k == 0))(start)

        @pl.when(k == 0)
        def _():
            acc[...] = jnp.zeros_like(acc)

        acc[...] += _dot(a_ref[...].astype(BF16), b_ref[...].astype(BF16), dims)

        @pl.when(k == nk - 1)
        def _():
            res = epilogue(acc[...], *[e[...] for e in ex_refs])
            for r, o in zip(res, out_refs):
                o[...] = r.astype(o.dtype)

        if nh:
            pl.when(jnp.logical_and(jnp.logical_and(i == gi - 1, j == gj - 1), k == nk - 1))(wait)

    res = pl.pallas_call(
        body, name=name, grid=(gi, gj, nk),
        in_specs=[a_spec, b_spec] + ex_specs + [_HBM] * nh,
        out_specs=[pl.BlockSpec((tm, tn), lambda i, j, k: (i, j)) for _ in outs] + [_HBM] * nh,
        out_shape=[jax.ShapeDtypeStruct((M, N), d) for d in outs] + host_shapes,
        scratch_shapes=[pltpu.VMEM((tm, tn), F32)] + host_sems,
        compiler_params=_params(vmem),
    )(a, b, *[e[0] for e in extras], *hosted)
    return res[0] if no + nh == 1 else res


def _tile_ij(i, j):
    return (i, j)


def _rowwise(fn, name, rows, tm, ins, outs, vmem=None):
    tm = min(tm, rows)
    assert rows % tm == 0

    def spec(shape, kind):
        if kind == "t":
            return pl.BlockSpec((tm,) + tuple(shape[1:]), lambda i: (i,) + (0,) * (len(shape) - 1))
        return pl.BlockSpec(tuple(shape), lambda i: (0,) * len(shape))

    def body(*refs):
        fn(pl.program_id(0), *refs)

    return pl.pallas_call(
        body, name=name, grid=(rows // tm,),
        in_specs=[spec(a.shape, k) for a, k in ins],
        out_specs=[spec(s, k) for s, _, k in outs],
        out_shape=[jax.ShapeDtypeStruct(s, d) for s, d, _ in outs],
        compiler_params=_params(vmem),
    )(*[a for a, _ in ins])


def _ln_stats(u):
    mu = jnp.mean(u, axis=-1, keepdims=True)
    d = u - mu
    var = jnp.mean(d * d, axis=-1, keepdims=True)
    r = lax.rsqrt(var + LN_EPS)
    return d * r, r


def _ln_bwd(dh, xh, r, g):
    dxh = dh * g
    m1 = jnp.mean(dxh, axis=-1, keepdims=True)
    m2 = jnp.mean(dxh * xh, axis=-1, keepdims=True)
    return r * (dxh - m1 - xh * m2)


def _acc_rows(i, ref, rows):
    @pl.when(i == 0)
    def _():
        ref[...] = jnp.zeros_like(ref)
    for r, v in rows.items():
        ref[pl.ds(r, 1), :] += v


def _head_sums(v, he, het):
    return _split_dot(_split_dot(v, he), het)


def _fgate_fwd(x, wft, bf_col, tm):
    S = x.shape[0]
    tm = min(tm, S)

    def body(wft_ref, bf_ref, x_ref, lf_ref):
        f = _dot(wft_ref[...], x_ref[...].astype(BF16), _NT) + bf_ref[...]
        lf_ref[...] = -_softplus(-f)

    return pl.pallas_call(
        body, name="fgate_fwd", grid=(S // tm,),
        in_specs=[pl.BlockSpec((N_FOX, D_MODEL), lambda i: (0, 0)), pl.BlockSpec((N_FOX, 1), lambda i: (0, 0)),
                  pl.BlockSpec((tm, D_MODEL), lambda i: (i, 0))],
        out_specs=pl.BlockSpec((N_FOX, tm), lambda i: (0, i)),
        out_shape=jax.ShapeDtypeStruct((N_FOX, S), F32),
    )(wft, bf_col, x)


def _chunk_scan(v, reverse):
    lane = lax.broadcasted_iota(jnp.int32, v.shape, 1)
    sh = 1
    while sh < LANES:
        if reverse:
            v = v + jnp.where(lane < LANES - sh, pltpu.roll(v, LANES - sh, 1), 0.0)
        else:
            v = v + jnp.where(lane >= sh, pltpu.roll(v, sh, 1), 0.0)
        sh *= 2
    return v


def _cumsum_fwd(lf):
    n, S = lf.shape
    nc = S // LANES

    grp = min(SCAN_GROUP, nc)

    def body(lf_ref, c_ref):
        def step(gi, carry):
            sls = [pl.ds(pl.multiple_of((gi * grp + g) * LANES, LANES), LANES) for g in range(grp)]
            vs = [_chunk_scan(lf_ref[:, sl], False) for sl in sls]
            tots = [_col(v, LANES - 1) for v in vs]
            for sl, v, t in zip(sls, vs, tots):
                c_ref[:, sl] = v + carry
                carry = carry + t
            return carry
        lax.fori_loop(0, nc // grp, step, jnp.zeros((n, 1), F32))

    return pl.pallas_call(body, name="cumsum_fwd", out_shape=jax.ShapeDtypeStruct((n, S), F32))(lf)


def _fgate_bwd(dc, lf):
    n, S = dc.shape
    nc = S // LANES

    grp = min(SCAN_GROUP, nc)

    def body(dc_ref, lf_ref, dfl_ref, dbf_ref):
        def step(t, carry):
            car, tot = carry
            gi = nc // grp - 1 - t
            sls = [pl.ds(pl.multiple_of((gi * grp + g) * LANES, LANES), LANES) for g in range(grp)]
            vs = [_chunk_scan(dc_ref[:, sl], True) for sl in sls]
            firsts = [_col(v, 0) for v in vs]
            for sl, v, f in reversed(list(zip(sls, vs, firsts))):
                dfl = (v + car) * (1.0 - jnp.exp(lf_ref[:, sl]))
                dfl_ref[:, sl] = dfl
                tot = tot + jnp.sum(dfl, axis=1, keepdims=True)
                car = car + f
            return car, tot
        _, tot = lax.fori_loop(0, nc // grp, step, (jnp.zeros((n, 1), F32), jnp.zeros((n, 1), F32)))
        dbf_ref[...] = tot

    return pl.pallas_call(body, name="fgate_bwd",
                          out_shape=[jax.ShapeDtypeStruct((n, S), F32), jax.ShapeDtypeStruct((n, 1), F32)])(dc, lf)


def _tri_matrices(b):
    r = np.arange(b)
    tfwd = (r[:, None] <= r[None, :]).astype(np.float32)
    return jnp.asarray(tfwd, BF16), jnp.asarray(tfwd.T, BF16)


def _kv_copies(kv_hbm, kbuf, vbuf, sems, sem0, pair_col, bq, j, slot):
    rows = pl.ds(pl.multiple_of(j * bq, bq), bq)

    def cols(c):
        return pl.ds(pl.multiple_of((pair_col + c) * LANES, LANES), LANES)

    return (pltpu.make_async_copy(kv_hbm.at[rows, cols(4)], kbuf.at[slot], sems.at[0, sem0 + slot]),
            pltpu.make_async_copy(kv_hbm.at[rows, cols(8)], vbuf.at[slot], sems.at[1, sem0 + slot]))


def _first_two_up(first_block, per=1):
    def blocks(pair, blk):
        first = first_block(pair, blk)
        return first, first + 1, first + 1 <= per * blk + per - 1
    return blocks


def _first_two_down(pair, blk):
    return blk, blk - 1, blk > 0


def _start_two(fetch, pair, first, second, has_second, ahead):
    for cp in fetch(first, 0, pair, ahead):
        cp.start()

    @pl.when(has_second)
    def _():
        for cp in fetch(second, 1, pair, ahead):
            cp.start()


def _kv_fetcher(kv_hbm, kbuf, vbuf, sems, ns, col0, bq, p, i, nq, blocks):
    base = lax.rem(p * nq + i, 2) * ns
    own = (kbuf.at[pl.ds(base, ns)], vbuf.at[pl.ds(base, ns)])
    other = (kbuf.at[pl.ds(ns - base, ns)], vbuf.at[pl.ds(ns - base, ns)])

    def fetch(j, slot, pair=p, ahead=False):
        kb, vb = other if ahead else own
        return _kv_copies(kv_hbm, kb, vb, sems, ns - base if ahead else base, col0 + pair, bq, j, slot)

    pl.when(jnp.logical_and(p == 0, i == 0))(lambda: _start_two(fetch, p, *blocks(p, i), False))
    wrap = i == nq - 1

    @pl.when(jnp.logical_not(jnp.logical_and(wrap, p == N_PAIRS - 1)))
    def _():
        pair, blk = jnp.where(wrap, p + 1, p), jnp.where(wrap, 0, i + 1)
        _start_two(fetch, pair, *blocks(pair, blk), True)

    return fetch, own[0], own[1]


def _masked_pair(v, lane_is_a, scale=1.0):
    v = v.astype(F32) * scale
    return jnp.where(lane_is_a, v, 0.0).astype(BF16), jnp.where(lane_is_a, 0.0, v).astype(BF16)


def _sb_fwd(proj, col0, bq, shards=()):
    S = proj.shape[0]
    bq = min(bq, S)
    nq = S // bq
    _, trev = _tri_matrices(bq)
    nh = len(shards)
    gather_shapes, gather_sems = _gather_shapes(shards) if nh else ([], [])

    def body(q_ref, kv_hbm, trev_ref, *rest):
        o_ref, st_ref, jmin_ref = rest[nh:nh + 3]
        acc_a, acc_b, qa, qb, rs, kbuf, vbuf, sems = rest[2 * nh + 3:2 * nh + 11]
        p, i = pl.program_id(0), pl.program_id(1)
        if nh:
            gather_start, gather_wait = _gather_copies(rest[:nh], rest[nh + 3:2 * nh + 3], *rest[2 * nh + 11:])
            pl.when(jnp.logical_and(p == 0, i == 0))(gather_start)
        fetch, kbuf, vbuf = _kv_fetcher(kv_hbm, kbuf, vbuf, sems, 2, col0, bq, p, i, nq, _first_two_down)
        is_a = lax.broadcasted_iota(jnp.int32, (bq, LANES), 1) < HEAD_DIM
        acc_a[...] = jnp.zeros_like(acc_a)
        acc_b[...] = jnp.zeros_like(acc_b)
        rs[...] = jnp.zeros_like(rs)
        qa[...], qb[...] = _masked_pair(q_ref[...], is_a, SCALE)

        def tiles(blocks):
            hs, qs, accs, trev_m = (0, 1), (qa, qb), (acc_a, acc_b), trev_ref[...]
            kv = [(kbuf[s], vbuf[s]) for s, _ in blocks]
            bh = [(b, h) for b in range(len(blocks)) for h in hs]
            tri = lax.broadcasted_iota(jnp.int32, (bq, bq), 0) > lax.broadcasted_iota(jnp.int32, (bq, bq), 1)
            z = {(b, h): _dot(qs[h][...], kv[b][0], _NT) for b, h in bh}
            lk = {(b, h): -_softplus(z[b, h]) for b, h in bh}
            lk = {(b, h): jnp.where(tri, lk[b, h], 0.0) if blocks[b][1] else lk[b, h] for b, h in bh}
            suf = {(b, h): _split_dot(lk[b, h], trev_m) for b, h in bh}
            tot = {(b, h): jnp.sum(lk[b, h], axis=1, keepdims=True) for b, h in bh}
            right = {}
            for h in hs:
                r = rs[2 * h] + rs[2 * h + 1]
                for b in range(len(blocks)):
                    right[b, h] = r
                    r = r + tot[b, h]
            w = {(b, h): jnp.exp(z[b, h] + suf[b, h] + right[b, h]) for b, h in bh}
            w = {(b, h): jnp.where(tri, w[b, h], 0.0) if blocks[b][1] else w[b, h] for b, h in bh}
            pv = {(b, h): _dot(w[b, h].astype(BF16), kv[b][1]) for b, h in bh}
            for h in hs:
                accs[h][...] += sum([pv[b, h] for b in range(1, len(blocks))], pv[0, h])
                hi, lo = rs[2 * h], rs[2 * h + 1]
                for b in range(len(blocks)):
                    hi, lo = _two_sum(hi, lo, tot[b, h])
                rs[2 * h], rs[2 * h + 1] = hi, lo

        def live():
            return (jnp.max(jnp.maximum(rs[0], rs[2])) > SB_STOP).astype(jnp.int32)

        for cp in fetch(i, 0):
            cp.wait()
        pl.when(i == 0)(functools.partial(tiles, [(0, True)]))

        @pl.when(i > 0)
        def _():
            for cp in fetch(i - 1, 1):
                cp.wait()
            tiles([(0, True), (1, False)])

        def step(carry):
            j, _ = carry
            slot = lax.rem(i - j, 2)
            for cp in fetch(j, slot):
                cp.start()
            for cp in fetch(j, slot):
                cp.wait()
            tiles([(slot, False)])
            return j - 1, live()

        j_end, _ = lax.while_loop(lambda c: jnp.logical_and(c[0] >= 0, c[1] > 0), step, (i - 2, live()))
        jmin_ref[p, i] = jnp.maximum(j_end + 1, 0)
        o_ref[...] = jnp.where(is_a, acc_a[...], acc_b[...])
        lane8 = lax.broadcasted_iota(jnp.int32, (bq, 8), 1)
        st = jnp.zeros((bq, 8), F32)
        for c, src in enumerate((0, 2, 1, 3)):
            st = jnp.where(lane8 == c, rs[src], st)
        st_ref[0] = st
        if nh:
            pl.when(jnp.logical_and(p == N_PAIRS - 1, i == nq - 1))(gather_wait)

    return pl.pallas_call(
        body, name="sb_fwd", grid=(N_PAIRS, nq),
        in_specs=[pl.BlockSpec((bq, LANES), lambda p, i: (i, col0 + p)),
                  pl.BlockSpec(memory_space=pl.ANY),
                  pl.BlockSpec((bq, bq), lambda p, i: (0, 0))] + [_HBM] * nh,
        out_specs=[pl.BlockSpec((bq, LANES), lambda p, i: (i, p)),
                   pl.BlockSpec((1, bq, 8), lambda p, i: (p, i, 0)),
                   pl.BlockSpec(memory_space=pltpu.SMEM)] + [_HBM] * nh,
        out_shape=[jax.ShapeDtypeStruct((S, GROUP_W), F32), jax.ShapeDtypeStruct((N_PAIRS, S, 8), F32),
                   jax.ShapeDtypeStruct((N_PAIRS, nq), jnp.int32)] + gather_shapes,
        scratch_shapes=[pltpu.VMEM((bq, LANES), F32), pltpu.VMEM((bq, LANES), F32),
                        pltpu.VMEM((bq, LANES), BF16), pltpu.VMEM((bq, LANES), BF16),
                        pltpu.VMEM((4, bq, 1), F32),
                        pltpu.VMEM((4, bq, LANES), BF16), pltpu.VMEM((4, bq, LANES), BF16),
                        pltpu.SemaphoreType.DMA((2, 4))] + gather_sems,
    )(proj, proj, trev, *shards)


def _sb_bwd(proj, col0, do, st, jmin, bq):
    S = proj.shape[0]
    bq = min(bq, S)
    nq = S // bq
    tfwd, trev = _tri_matrices(bq)

    def body(jmin_ref, q_ref, kv_hbm, do_ref, st_ref, tfwd_ref, trev_ref,
             dq_ref, dk_out, dv_out, dq_a, dq_b, qa, qb, doa, dob, rs, kbuf, vbuf, sems, dk_ref, dv_ref):
        p, i = pl.program_id(0), pl.program_id(1)
        j0 = jmin_ref[p, i]
        first_two = _first_two_up(lambda pair, blk: jmin_ref[pair, blk])
        fetch, kbuf, vbuf = _kv_fetcher(kv_hbm, kbuf, vbuf, sems, KV_SLOTS, col0, bq, p, i, nq, first_two)
        is_a = lax.broadcasted_iota(jnp.int32, (bq, LANES), 1) < HEAD_DIM

        @pl.when(i == 0)
        def _():
            dk_ref[...] = jnp.zeros_like(dk_ref)
            dv_ref[...] = jnp.zeros_like(dv_ref)

        dq_a[...] = jnp.zeros_like(dq_a)
        dq_b[...] = jnp.zeros_like(dq_b)
        rs[...] = jnp.zeros_like(rs)
        st_v = st_ref[0]
        for h in range(2):
            rs[6 + 2 * h], rs[7 + 2 * h] = _col(st_v, h), _col(st_v, 2 + h)
        qa[...], qb[...] = _masked_pair(q_ref[...], is_a, SCALE)
        doa[...], dob[...] = _masked_pair(do_ref[...], is_a)

        def tiles(blocks):
            hs, qs, dos, dqs = (0, 1), (qa, qb), (doa, dob), (dq_a, dq_b)
            tfwd_m, trev_m = tfwd_ref[...], trev_ref[...]
            kv = [(kbuf[s], vbuf[s]) for _, s, _ in blocks]
            nb = len(blocks)
            bh = [(b, h) for b in range(nb) for h in hs]
            tri = lax.broadcasted_iota(jnp.int32, (bq, bq), 0) > lax.broadcasted_iota(jnp.int32, (bq, bq), 1)

            def mask(x, b):
                return jnp.where(tri, x, 0.0) if blocks[b][2] else x

            z = {(b, h): _dot(qs[h][...], kv[b][0], _NT) for b, h in bh}
            dw = {(b, h): _dot(dos[h][...], kv[b][1], _NT) for b, h in bh}
            lk = {(b, h): mask(-_softplus(z[b, h]), b) for b, h in bh}
            suf = {(b, h): _split_dot(lk[b, h], trev_m) for b, h in bh}
            tot = {(b, h): jnp.sum(lk[b, h], axis=1, keepdims=True) for b, h in bh}
            pre = {}
            for h in hs:
                run = (rs[3 * h], rs[3 * h + 1])
                for b in range(nb):
                    run = _two_sum(run[0], run[1], tot[b, h])
                    pre[b, h] = run
            right = {(b, h): (rs[6 + 2 * h] - pre[b, h][0]) + (rs[7 + 2 * h] - pre[b, h][1]) for b, h in bh}
            w = {(b, h): mask(jnp.exp(z[b, h] + suf[b, h] + right[b, h]), b) for b, h in bh}
            g = {(b, h): dw[b, h] * w[b, h] for b, h in bh}
            gpre = {(b, h): _split_dot(g[b, h], tfwd_m) for b, h in bh}
            gtot = {(b, h): jnp.sum(g[b, h], axis=1, keepdims=True) for b, h in bh}
            gleft = {}
            for h in hs:
                run = rs[3 * h + 2]
                for b in range(nb):
                    gleft[b, h] = run
                    run = run + gtot[b, h]
                gleft[nb, h] = run
            dz = {(b, h): mask(g[b, h] - jnp.exp(z[b, h] + lk[b, h]) * (gpre[b, h] + gleft[b, h]), b) for b, h in bh}
            dzb = {(b, h): dz[b, h].astype(BF16) for b, h in bh}
            wb = {(b, h): w[b, h].astype(BF16) for b, h in bh}
            dqc = {(b, h): _dot(dzb[b, h], kv[b][0]) for b, h in bh}
            dkc = {(b, h): _dot(dzb[b, h], qs[h][...], _TN) for b, h in bh}
            dvc = {(b, h): _dot(wb[b, h], dos[h][...], _TN) for b, h in bh}
            for h in hs:
                rs[3 * h], rs[3 * h + 1] = pre[nb - 1, h]
                rs[3 * h + 2] = gleft[nb, h]
                dqs[h][...] += sum([dqc[b, h] for b in range(1, nb)], dqc[0, h])
            for b, (j, _, _) in enumerate(blocks):
                rows = pl.ds(pl.multiple_of(j * bq, bq), bq)
                dk_ref[rows, :] += dkc[b, 0] + dkc[b, 1]
                dv_ref[rows, :] += dvc[b, 0] + dvc[b, 1]

        def single(j, slot, masked):
            tiles([(j, slot, masked)])

        def wait(j):
            slot = lax.rem(j - j0, KV_SLOTS)
            for cp in fetch(j, slot):
                cp.wait()
            return slot

        _walk_up(fetch, j0, i, i, single, stop=jnp.maximum(i - 1, j0))

        @pl.when(j0 < i)
        def _():
            tiles([(i - 1, wait(i - 1), False), (i, wait(i), True)])

        @pl.when(j0 == i)
        def _():
            tiles([(i, wait(i), True)])

        dq_ref[...] = (jnp.where(is_a, dq_a[...], dq_b[...]) * SCALE).astype(BF16)

        @pl.when(i == nq - 1)
        def _():
            dk_out[...] = dk_ref[...].astype(BF16)
            dv_out[...] = dv_ref[...].astype(BF16)

    grid_spec = pltpu.PrefetchScalarGridSpec(
        num_scalar_prefetch=1, grid=(N_PAIRS, nq),
        in_specs=[pl.BlockSpec((bq, LANES), lambda p, i, jm: (i, col0 + p)),
                  pl.BlockSpec(memory_space=pl.ANY),
                  pl.BlockSpec((bq, LANES), lambda p, i, jm: (i, p)),
                  pl.BlockSpec((1, bq, 8), lambda p, i, jm: (p, i, 0)),
                  pl.BlockSpec((bq, bq), lambda p, i, jm: (0, 0)),
                  pl.BlockSpec((bq, bq), lambda p, i, jm: (0, 0))],
        out_specs=[pl.BlockSpec((bq, LANES), lambda p, i, jm: (i, p)),
                   pl.BlockSpec((S, LANES), lambda p, i, jm: (0, p)),
                   pl.BlockSpec((S, LANES), lambda p, i, jm: (0, p))],
        scratch_shapes=[pltpu.VMEM((bq, LANES), F32), pltpu.VMEM((bq, LANES), F32)]
        + [pltpu.VMEM((bq, LANES), BF16)] * 4 + [pltpu.VMEM((10, bq, 1), F32)]
        + [pltpu.VMEM((2 * KV_SLOTS, bq, LANES), BF16)] * 2 + [pltpu.SemaphoreType.DMA((2, 2 * KV_SLOTS))]
        + [pltpu.VMEM((S, LANES), F32)] * 2)
    return pl.pallas_call(
        body, name="sb_bwd", grid_spec=grid_spec,
        out_shape=[jax.ShapeDtypeStruct((S, GROUP_W), BF16)] * 3,
        compiler_params=_params(VMEM_BIG),
    )(jmin, proj, proj, do, st, tfwd, trev)


def _walk_up(fetch, j0, diag, last, tile, stop=None):
    ahead = KV_SLOTS - 1
    stop = last + 1 if stop is None else stop

    def start(j):
        @pl.when(j <= last)
        def _():
            for cp in fetch(j, lax.rem(j - j0, KV_SLOTS)):
                cp.start()

    for d in range(2, ahead):
        start(j0 + d)

    def step(j, carry):
        slot = lax.rem(j - j0, KV_SLOTS)
        for cp in fetch(j, slot):
            cp.wait()
        start(j + ahead)
        pl.when(j >= diag)(functools.partial(tile, j, slot, True))
        pl.when(j < diag)(functools.partial(tile, j, slot, False))
        return carry

    lax.fori_loop(j0, stop, step, 0)


def _causal(bq, bk, i, j):
    row = lax.broadcasted_iota(jnp.int32, (bq, bk), 0)
    col = lax.broadcasted_iota(jnp.int32, (bq, bk), 1)
    return col - row <= i * bq - j * bk


def _by_heads(j, first_a, first_b, heads):
    on_a, on_b = j >= first_a, j >= first_b
    pl.when(jnp.logical_and(on_a, on_b))(functools.partial(heads, (0, 1)))
    pl.when(jnp.logical_and(on_a, jnp.logical_not(on_b)))(functools.partial(heads, (0,)))
    pl.when(jnp.logical_and(on_b, jnp.logical_not(on_a)))(functools.partial(heads, (1,)))


def _fox_row_norms(proj, col0, tm):
    S = proj.shape[0]
    tm = min(tm, S)
    head_of = np.arange(GROUP_W) // HEAD_DIM
    he_t = jnp.asarray((np.arange(2 * N_PAIRS)[:, None] == head_of[None, :]).astype(np.float32), BF16)

    def body(q_ref, k_ref, he_ref, qn_ref, kn_ref, d_ref):
        q, k, he = q_ref[...].astype(F32), k_ref[...].astype(F32), he_ref[...]

        def head_sums_t(x):
            hi = x.astype(BF16)
            lo = (x - hi.astype(F32)).astype(BF16)
            return _dot(he, hi, _NT) + _dot(he, lo, _NT)

        qn_ref[...] = jnp.sqrt(head_sums_t(q * q))
        kn_ref[...] = jnp.sqrt(head_sums_t(k * k))
        d_ref[...] = SCALE * head_sums_t(q * k)

    wide = GROUP_W // LANES
    return pl.pallas_call(
        body, name="fox_row_norms", grid=(S // tm,),
        in_specs=[pl.BlockSpec((tm, GROUP_W), lambda i: (i, col0 // wide)),
                  pl.BlockSpec((tm, GROUP_W), lambda i: (i, (col0 + 4) // wide)),
                  pl.BlockSpec((2 * N_PAIRS, GROUP_W), lambda i: (0, 0))],
        out_specs=[pl.BlockSpec((2 * N_PAIRS, tm), lambda i: (0, i))] * 3,
        out_shape=[jax.ShapeDtypeStruct((2 * N_PAIRS, S), F32)] * 3)(proj, proj, he_t)


def _fox_start_blocks(qn, kn, d, c, bq, bk):
    nh, S = c.shape
    nq, nk = S // bq, S // bk
    top = SCALE * qn * kn.max(axis=1, keepdims=True) - d + c
    top = top.reshape(nh, nq, bq).max(axis=2)
    c_last = c[:, bk - 1::bk]
    live = top[:, :, None] - c_last[:, None, :] >= -FOX_SKIP

    def first_block(lv):
        first = jnp.where(lv.any(axis=2), jnp.argmax(lv, axis=2), nk)
        return jnp.minimum(first, (bq // bk) * jnp.arange(nq)[None, :]).astype(jnp.int32)

    return jnp.concatenate([first_block(live.reshape(N_PAIRS, 2, nq, nk).any(axis=1)), first_block(live)], axis=0)


def _fox_fwd(proj, col0, c_col, c_row, jstart, bq, bk):
    S = proj.shape[0]
    nq, per = S // bq, bq // bk

    def body(js_ref, q_ref, kv_hbm, cc_ref, cr_ref, o_ref, st_ref, acc_a, acc_b, qa, qb, ml, kbuf, vbuf, sems):
        p, i = pl.program_id(0), pl.program_id(1)
        j0 = js_ref[p, i]
        first_two = _first_two_up(lambda pair, blk: js_ref[pair, blk], per)
        fetch, kbuf, vbuf = _kv_fetcher(kv_hbm, kbuf, vbuf, sems, KV_SLOTS, col0, bk, p, i, nq, first_two)
        is_a = lax.broadcasted_iota(jnp.int32, (bq, LANES), 1) < HEAD_DIM
        acc_a[...] = jnp.zeros_like(acc_a)
        acc_b[...] = jnp.zeros_like(acc_b)
        ml[0] = jnp.full((bq, 1), NEG_BIG, F32)
        ml[2] = jnp.full((bq, 1), NEG_BIG, F32)
        ml[1] = jnp.zeros((bq, 1), F32)
        ml[3] = jnp.zeros((bq, 1), F32)
        cc = cc_ref[0]
        ml[4], ml[5] = _col(cc, 0), _col(cc, 1)
        qa[...], qb[...] = _masked_pair(q_ref[...], is_a, SCALE)

        def tile(j, slot, masked):
            k, v = kbuf[slot], vbuf[slot]
            cols = pl.ds(pl.multiple_of(j * bk, bk), bk)
            if masked:
                tri = _causal(bq, bk, i, j)

            def heads(hs):
                qs, accs = (qa, qb), (acc_a, acc_b)
                s = {h: _dot(qs[h][...], k, _NT) - cr_ref[0, pl.ds(h, 1), cols] for h in hs}
                if masked:
                    s = {h: jnp.where(tri, s[h], NEG_BIG) for h in hs}
                top = {h: jnp.max(s[h], axis=1, keepdims=True) for h in hs}
                m_new = {h: jnp.maximum(ml[2 * h], top[h] + ml[4 + h]) for h in hs}
                a = {h: jnp.exp(ml[2 * h] - m_new[h]) for h in hs}
                pr = {h: jnp.exp(s[h] - (m_new[h] - ml[4 + h])) for h in hs}
                tot = {h: jnp.sum(pr[h], axis=1, keepdims=True) for h in hs}
                pv = {h: _dot(pr[h].astype(BF16), v) for h in hs}
                for h in hs:
                    ml[2 * h] = m_new[h]
                    ml[2 * h + 1] = a[h] * ml[2 * h + 1] + tot[h]
                    accs[h][...] = a[h] * accs[h][...] + pv[h]

            _by_heads(j, js_ref[N_PAIRS + 2 * p, i], js_ref[N_PAIRS + 2 * p + 1, i], heads)

        _walk_up(fetch, j0, per * i, per * i + per - 1, tile)
        o_ref[...] = jnp.where(is_a, acc_a[...] / ml[1], acc_b[...] / ml[3])
        lane8 = lax.broadcasted_iota(jnp.int32, (bq, 8), 1)
        st = jnp.where(lane8 == 0, ml[0] + jnp.log(ml[1]), 0.0)
        st_ref[0] = jnp.where(lane8 == 1, ml[2] + jnp.log(ml[3]), st)

    grid_spec = pltpu.PrefetchScalarGridSpec(
        num_scalar_prefetch=1, grid=(N_PAIRS, nq),
        in_specs=[pl.BlockSpec((bq, LANES), lambda p, i, js: (i, col0 + p)),
                  pl.BlockSpec(memory_space=pl.ANY),
                  pl.BlockSpec((1, bq, 8), lambda p, i, js: (p, i, 0)),
                  pl.BlockSpec((1, 8, S), lambda p, i, js: (p, 0, 0))],
        out_specs=[pl.BlockSpec((bq, LANES), lambda p, i, js: (i, p)),
                   pl.BlockSpec((1, bq, 8), lambda p, i, js: (p, i, 0))],
        scratch_shapes=[pltpu.VMEM((bq, LANES), F32), pltpu.VMEM((bq, LANES), F32),
                        pltpu.VMEM((bq, LANES), BF16), pltpu.VMEM((bq, LANES), BF16),
                        pltpu.VMEM((6, bq, 1), F32),
                        pltpu.VMEM((2 * KV_SLOTS, bk, LANES), BF16), pltpu.VMEM((2 * KV_SLOTS, bk, LANES), BF16),
                        pltpu.SemaphoreType.DMA((2, 2 * KV_SLOTS))])
    return pl.pallas_call(
        body, name="fox_fwd", grid_spec=grid_spec,
        out_shape=[jax.ShapeDtypeStruct((S, GROUP_W), F32), jax.ShapeDtypeStruct((N_PAIRS, S, 8), F32)],
    )(jstart, proj, proj, c_col, c_row)


def _fox_bwd(proj, col0, do, o, st, c_col, c_row, jstart, bq, bk):
    S = proj.shape[0]
    nq, per = S // bq, bq // bk

    def body(js_ref, q_ref, kv_hbm, do_ref, o_ref, st_ref, cc_ref, cr_ref,
             dq_ref, dk_out, dv_out, dc_ref, dq_a, dq_b, qa, qb, doa, dob, dd, kbuf, vbuf, sems, dk_ref, dv_ref):
        p, i = pl.program_id(0), pl.program_id(1)
        j0 = js_ref[p, i]
        first_two = _first_two_up(lambda pair, blk: js_ref[pair, blk], per)
        fetch, kbuf, vbuf = _kv_fetcher(kv_hbm, kbuf, vbuf, sems, KV_SLOTS, col0, bk, p, i, nq, first_two)
        is_a = lax.broadcasted_iota(jnp.int32, (bq, LANES), 1) < HEAD_DIM

        @pl.when(i == 0)
        def _():
            dk_ref[...] = jnp.zeros_like(dk_ref)
            dv_ref[...] = jnp.zeros_like(dv_ref)
            dc_ref[...] = jnp.zeros_like(dc_ref)

        dq_a[...] = jnp.zeros_like(dq_a)
        dq_b[...] = jnp.zeros_like(dq_b)
        qa[...], qb[...] = _masked_pair(q_ref[...], is_a, SCALE)
        dov = do_ref[...]
        doa[...], dob[...] = _masked_pair(dov, is_a)
        prod = dov * o_ref[...]
        dd[0] = jnp.sum(jnp.where(is_a, prod, 0.0), axis=1, keepdims=True)
        dd[1] = jnp.sum(jnp.where(is_a, 0.0, prod), axis=1, keepdims=True)
        dd[2] = jnp.zeros((bq, 1), F32)
        dd[3] = jnp.zeros((bq, 1), F32)
        cc, st_v = cc_ref[0], st_ref[0]
        dd[4], dd[5] = _col(cc, 0) - _col(st_v, 0), _col(cc, 1) - _col(st_v, 1)

        def tile(j, slot, masked):
            k, v = kbuf[slot], vbuf[slot]
            if masked:
                tri = _causal(bq, bk, i, j)
            cols = pl.ds(pl.multiple_of(j * bk, bk), bk)

            def heads(hs):
                qs, dos, dqs = (qa, qb), (doa, dob), (dq_a, dq_b)
                z = {h: _dot(qs[h][...], k, _NT) for h in hs}
                dp = {h: _dot(dos[h][...], v, _NT) for h in hs}
                pr = {h: jnp.exp(z[h] - cr_ref[0, pl.ds(h, 1), cols] + dd[4 + h]) for h in hs}
                if masked:
                    pr = {h: jnp.where(tri, pr[h], 0.0) for h in hs}
                ds = {h: pr[h] * (dp[h] - dd[h]) for h in hs}
                csum = {h: jnp.sum(ds[h], axis=0, keepdims=True) for h in hs}
                rsum = {h: jnp.sum(ds[h], axis=1, keepdims=True) for h in hs}
                dsb = {h: ds[h].astype(BF16) for h in hs}
                prb = {h: pr[h].astype(BF16) for h in hs}
                dqc = {h: _dot(dsb[h], k) for h in hs}
                dkc = [_dot(dsb[h], qs[h][...], _TN) for h in hs]
                dvc = [_dot(prb[h], dos[h][...], _TN) for h in hs]
                for h in hs:
                    dc_ref[0, pl.ds(h, 1), cols] -= csum[h]
                    dd[2 + h] += rsum[h]
                    dqs[h][...] += dqc[h]
                dk_ref[cols, :] += sum(dkc[1:], dkc[0])
                dv_ref[cols, :] += sum(dvc[1:], dvc[0])

            _by_heads(j, js_ref[N_PAIRS + 2 * p, i], js_ref[N_PAIRS + 2 * p + 1, i], heads)

        _walk_up(fetch, j0, per * i, per * i + per - 1, tile)
        dq_ref[...] = (jnp.where(is_a, dq_a[...], dq_b[...]) * SCALE).astype(BF16)
        eye = lax.broadcasted_iota(jnp.int32, (bq, bq), 0) == lax.broadcasted_iota(jnp.int32, (bq, bq), 1)
        own = pl.ds(pl.multiple_of(i * bq, bq), bq)
        for h in range(2):
            dc_ref[0, pl.ds(h, 1), own] += jnp.sum(jnp.where(eye, dd[2 + h], 0.0), axis=0, keepdims=True)

        @pl.when(i == nq - 1)
        def _():
            dk_out[...] = dk_ref[...].astype(BF16)
            dv_out[...] = dv_ref[...].astype(BF16)

    grid_spec = pltpu.PrefetchScalarGridSpec(
        num_scalar_prefetch=1, grid=(N_PAIRS, nq),
        in_specs=[pl.BlockSpec((bq, LANES), lambda p, i, js: (i, col0 + p)),
                  pl.BlockSpec(memory_space=pl.ANY),
                  pl.BlockSpec((bq, LANES), lambda p, i, js: (i, p)),
                  pl.BlockSpec((bq, LANES), lambda p, i, js: (i, p)),
                  pl.BlockSpec((1, bq, 8), lambda p, i, js: (p, i, 0)),
                  pl.BlockSpec((1, bq, 8), lambda p, i, js: (p, i, 0)),
                  pl.BlockSpec((1, 8, S), lambda p, i, js: (p, 0, 0))],
        out_specs=[pl.BlockSpec((bq, LANES), lambda p, i, js: (i, p)),
                   pl.BlockSpec((S, LANES), lambda p, i, js: (0, p)),
                   pl.BlockSpec((S, LANES), lambda p, i, js: (0, p)),
                   pl.BlockSpec((1, 8, S), lambda p, i, js: (p, 0, 0))],
        scratch_shapes=[pltpu.VMEM((bq, LANES), F32), pltpu.VMEM((bq, LANES), F32)]
        + [pltpu.VMEM((bq, LANES), BF16)] * 4 + [pltpu.VMEM((6, bq, 1), F32)]
        + [pltpu.VMEM((2 * KV_SLOTS, bk, LANES), BF16)] * 2 + [pltpu.SemaphoreType.DMA((2, 2 * KV_SLOTS))]
        + [pltpu.VMEM((S, LANES), F32)] * 2)
    return pl.pallas_call(
        body, name="fox_bwd", grid_spec=grid_spec,
        out_shape=[jax.ShapeDtypeStruct((S, GROUP_W), BF16)] * 3 + [jax.ShapeDtypeStruct((N_PAIRS, 8, S), F32)],
        compiler_params=_params(VMEM_BIG),
    )(jstart, proj, proj, do, o, st, c_col, c_row)


_HBM = pl.BlockSpec(memory_space=pltpu.HBM)


def _coords():
    return lax.axis_index("x"), lax.axis_index("y"), lax.axis_index("c")


def _gather_copies(ins, outs, send_sems, recv_sems, loc_sems):
    n = len(ins)
    x, y, c = _coords()
    mine = 2 * x + y
    chips = [(1 - x, y), (x, 1 - y), (1 - x, 1 - y)]

    def copy(w, r, slab, to):
        return pltpu.make_async_remote_copy(
            src_ref=ins[w], dst_ref=outs[w].at[slab], send_sem=send_sems.at[3 * w + r],
            recv_sem=recv_sems.at[3 * w + r], device_id=to, device_id_type=MESH)

    def own():
        local = [pltpu.make_async_copy(ins[w], outs[w].at[mine], loc_sems.at[w]) for w in range(n)]
        return local, [copy(w, r, mine, (cx, cy, c)) for w in range(n) for r, (cx, cy) in enumerate(chips)]

    def start():
        local, sends = own()
        for cp in local + sends:
            cp.start()

    def wait():
        local, sends = own()
        for w in range(n):
            for r, (cx, cy) in enumerate(chips):
                copy(w, r, 2 * cx + cy, (cx, cy, c)).wait_recv()
        for cp in sends:
            cp.wait_send()
        for cp in local:
            cp.wait()

    return start, wait


def _gather_shapes(shards):
    n = len(shards)
    return ([jax.ShapeDtypeStruct((4,) + s.shape, s.dtype) for s in shards],
            [pltpu.SemaphoreType.DMA((3 * n,)), pltpu.SemaphoreType.DMA((3 * n,)), pltpu.SemaphoreType.DMA((n,))])


def _allgather_chips(shards):
    n = len(shards)

    def body(*refs):
        start, wait = _gather_copies(refs[:n], refs[n:2 * n], *refs[2 * n:])
        start()
        wait()

    out_shape, sems = _gather_shapes(shards)
    return pl.pallas_call(body, name="allgather_weights", in_specs=[_HBM] * n, out_specs=[_HBM] * n,
                          out_shape=out_shape, scratch_shapes=sems)(*shards)


def _proj_gather(x, w, shards, tm, tn):
    (M, K), N, n = x.shape, w.shape[1], len(shards)
    tm = min(tm, M)
    gi, gj = M // tm, N // tn

    def body(a_ref, b_ref, *rest):
        o_ref = rest[n]
        start, wait = _gather_copies(rest[:n], rest[n + 1:2 * n + 1], *rest[2 * n + 1:])
        i, j = pl.program_id(0), pl.program_id(1)
        pl.when(jnp.logical_and(i == 0, j == 0))(start)
        o_ref[...] = _dot(a_ref[...].astype(BF16), b_ref[...]).astype(o_ref.dtype)
        pl.when(jnp.logical_and(i == gi - 1, j == gj - 1))(wait)

    out_shape, sems = _gather_shapes(shards)
    return pl.pallas_call(
        body, name="proj_gather", grid=(gi, gj),
        in_specs=[pl.BlockSpec((tm, K), lambda i, j: (i, 0)), pl.BlockSpec((K, tn), lambda i, j: (0, j))] + [_HBM] * n,
        out_specs=[pl.BlockSpec((tm, tn), lambda i, j: (i, j))] + [_HBM] * n,
        out_shape=[jax.ShapeDtypeStruct((M, N), BF16)] + out_shape, scratch_shapes=sems,
    )(x, w, *shards)


def _exchange_copies(ins, outs, send_sems, recv_sems, loc_sems, per_chip, parts):
    n = len(parts)
    half = [p.shape[1] // 2 for p in parts] if per_chip else None
    x, y, c = _coords()
    me = 4 * x + 2 * y + c
    peers = [(x ^ fx, y ^ fy, c ^ fc) for fx in (0, 1) for fy in (0, 1) for fc in (0, 1)][1:]

    def src(w, dev):
        if not per_chip:
            return ins[w]
        return ins[w].at[2 * dev[0] + dev[1], pl.ds(pl.multiple_of(dev[2] * half[w], 16), half[w]), :]

    def copy(w, r, source, slab, to):
        return pltpu.make_async_remote_copy(
            src_ref=source, dst_ref=outs[w].at[slab], send_sem=send_sems.at[7 * w + r],
            recv_sem=recv_sems.at[7 * w + r], device_id=to, device_id_type=MESH)

    def own():
        local = [pltpu.make_async_copy(src(w, (x, y, c)), outs[w].at[me], loc_sems.at[w]) for w in range(n)]
        return local, [copy(w, r, src(w, dev), me, dev) for w in range(n) for r, dev in enumerate(peers)]

    def start():
        local, sends = own()
        for cp in local + sends:
            cp.start()

    def wait():
        local, sends = own()
        for w in range(n):
            for r, dev in enumerate(peers):
                copy(w, r, src(w, dev), 4 * dev[0] + 2 * dev[1] + dev[2], dev).wait_recv()
        for cp in sends:
            cp.wait_send()
        for cp in local:
            cp.wait()

    return start, wait


def _exchange_shapes(parts, per_chip):
    n = len(parts)
    return ([jax.ShapeDtypeStruct((8, p.shape[1] // 2, p.shape[2]) if per_chip else (8,) + p.shape, p.dtype)
             for p in parts],
            [pltpu.SemaphoreType.DMA((7 * n,)), pltpu.SemaphoreType.DMA((7 * n,)), pltpu.SemaphoreType.DMA((n,))])


def _exchange(parts, per_chip):
    n = len(parts)

    def body(*refs):
        start, wait = _exchange_copies(refs[:n], refs[n:2 * n], *refs[2 * n:], per_chip, parts)
        start()
        wait()

    out_shape, sems = _exchange_shapes(parts, per_chip)
    return pl.pallas_call(body, name="exchange_per_chip" if per_chip else "exchange_all",
                          in_specs=[_HBM] * n, out_specs=[_HBM] * n, out_shape=out_shape, scratch_shapes=sems)(*parts)


def _sibling_swap(halves):
    n = len(halves)

    def body(*refs):
        ins, outs = refs[:n], refs[n:2 * n]
        send_sems, recv_sems, loc_sems = refs[2 * n:]
        x, y, c = _coords()

        def rows(w, core):
            rh = halves[w].shape[0]
            return outs[w].at[pl.ds(pl.multiple_of(core * rh, 8), rh), :]

        def copy(w, core):
            return pltpu.make_async_remote_copy(
                src_ref=ins[w], dst_ref=rows(w, core), send_sem=send_sems.at[w], recv_sem=recv_sems.at[w],
                device_id=(x, y, 1 - c), device_id_type=MESH)

        local = [pltpu.make_async_copy(ins[w], rows(w, c), loc_sems.at[w]) for w in range(n)]
        sends = [copy(w, c) for w in range(n)]
        for cp in local + sends:
            cp.start()
        for w in range(n):
            copy(w, 1 - c).wait_recv()
        for cp in sends:
            cp.wait_send()
        for cp in local:
            cp.wait()

    vmem = pl.BlockSpec(memory_space=pltpu.VMEM)
    return pl.pallas_call(
        body, name="sibling_swap", in_specs=[vmem] * n, out_specs=[vmem] * n,
        out_shape=[jax.ShapeDtypeStruct((2 * h.shape[0], h.shape[1]), h.dtype) for h in halves],
        scratch_shapes=[pltpu.SemaphoreType.DMA((n,)), pltpu.SemaphoreType.DMA((n,)), pltpu.SemaphoreType.DMA((n,))],
    )(*halves)


def _adamw(w, g, m, v):
    m = ADAM_B1 * m + (1.0 - ADAM_B1) * g
    v = ADAM_B2 * v + (1.0 - ADAM_B2) * (g * g)
    m_hat = m / (1.0 - ADAM_B1 ** ADAM_STEP)
    v_hat = v / (1.0 - ADAM_B2 ** ADAM_STEP)
    delta = -ADAM_LR * (m_hat / (jnp.sqrt(v_hat) + ADAM_EPS) + ADAM_WD * w)
    return delta, m, v


def _sum_parts(parts, name, tr):
    _, R, C = parts.shape
    assert R % tr == 0

    def body(p_ref, g_ref):
        g = p_ref[0].astype(F32)
        for d in range(1, 8):
            g = g + p_ref[d].astype(F32)
        g_ref[...] = g

    return pl.pallas_call(
        body, name=name, grid=(R // tr,),
        in_specs=[pl.BlockSpec((8, tr, C), lambda i: (0, i, 0))],
        out_specs=pl.BlockSpec((tr, C), lambda i: (i, 0)), out_shape=jax.ShapeDtypeStruct((R, C), F32),
    )(parts)


def _adamw_call(g, w, m, v, name, tr):
    R, C = w.shape
    assert R % tr == 0

    def body(g_ref, w_ref, m_ref, v_ref, d_ref, nm_ref, nv_ref):
        d_ref[...], nm_ref[...], nv_ref[...] = _adamw(w_ref[...], g_ref[...], m_ref[...], v_ref[...])

    tile = pl.BlockSpec((tr, C), lambda i: (i, 0))
    return pl.pallas_call(
        body, name=name, grid=(R // tr,), in_specs=[tile] * 4,
        out_specs=[tile] * 3, out_shape=[jax.ShapeDtypeStruct((R, C), F32)] * 3,
    )(g, w, m, v)


def _sum_adamw_small(parts, w, m, v):
    def body(p_ref, w_ref, m_ref, v_ref, g_ref, d_ref, nm_ref, nv_ref, loss_ref):
        g = p_ref[0]
        for d in range(1, 8):
            g = g + p_ref[d]
        g_ref[...] = g
        d_ref[...], nm_ref[...], nv_ref[...] = _adamw(w_ref[...], g, m_ref[...], v_ref[...])
        row = lax.broadcasted_iota(jnp.int32, g.shape, 0)
        per_row = jnp.sum(jnp.where(row == 6, g, 0.0), axis=1, keepdims=True)
        loss_ref[...] = jnp.zeros((8, LANES), F32) + jnp.sum(per_row, axis=0, keepdims=True)

    return pl.pallas_call(
        body, name="sum_adamw_small",
        out_shape=[jax.ShapeDtypeStruct((8, D_MODEL), F32)] * 4 + [jax.ShapeDtypeStruct((8, LANES), F32)],
    )(parts, w, m, v)


def _pack_small(ln1_g, ln1_b, ln2_g, ln2_b, g_sb, g_fox, b_f):
    row5 = jnp.pad(b_f.reshape(1, N_FOX), ((0, 0), (0, D_MODEL - N_FOX)))
    rows = [ln1_g.reshape(1, -1), ln1_b.reshape(1, -1), ln2_g.reshape(1, -1), ln2_b.reshape(1, -1),
            jnp.concatenate([g_sb.reshape(1, -1), g_fox.reshape(1, -1)], axis=1), row5,
            jnp.zeros((2, D_MODEL), F32)]
    return jnp.concatenate(rows, axis=0)


def _unpack_small(p):
    return {"ln1_g": p[0:1], "ln1_b": p[1:2], "ln2_g": p[2:3], "ln2_b": p[3:4], "g_sb": p[4:5, :GROUP_W],
            "g_fox": p[4:5, GROUP_W:], "b_f": p[5:6, :N_FOX]}


def kernel(x, w_in, b_f, g_sb, g_fox, w_out, ln1_g, ln1_b, ln2_g, ln2_b, w_gate_up, w_down, loss_target, m_w_in, m_b_f, m_g_sb, m_g_fox, m_w_out, m_ln1_g, m_ln1_b, m_ln2_g, m_ln2_b, m_w_gate_up, m_w_down, v_w_in, v_b_f, v_g_sb, v_g_fox, v_w_out, v_ln1_g, v_ln1_b, v_ln2_g, v_ln2_b, v_w_gate_up, v_w_down):
    S = x.shape[1]
    x2 = x.reshape(S, D_MODEL)
    tgt = loss_target.reshape(S, D_MODEL)
    TM = 1024
    TR = 512
    BQ = ATTN_BLOCK
    in_w = w_in.shape[2]
    gu_w = w_gate_up.shape[2]

    shards = [w_in[0].astype(BF16), w_out[0].astype(BF16), w_gate_up[0].astype(BF16), w_down[0].astype(BF16)]
    (wi_s,) = _allgather_chips(shards[:1])
    wi = wi_s.transpose(1, 0, 2).reshape(D_MODEL, 4 * in_w)
    w_sb, w_fx = wi[:, :QKV_W // 2], wi[:, QKV_W // 2:QKV_W]
    wqkv = wi[:, :QKV_W]
    wft = wi[:, QKV_W:].T
    proj, wgu_s = _proj_gather(x2, wqkv, shards[2:3], TM, 512)
    g_row = jnp.concatenate([g_sb, g_fox], axis=1)
    hid = np.arange(D_MODEL) // HEAD_DIM
    he_np = (hid[:, None] == np.arange(LANES)[None, :]).astype(np.float32)
    he, het = jnp.asarray(he_np, BF16), jnp.asarray(he_np.T, BF16)

    lf = _fgate_fwd(x2, wft, b_f.reshape(N_FOX, 1), TM)
    c = _cumsum_fwd(lf)
    c_pair = c.reshape(N_PAIRS, 2, S)
    c_row = jnp.pad(c_pair, ((0, 0), (0, 6), (0, 0)))
    c_col = jnp.pad(c_pair.transpose(0, 2, 1), ((0, 0), (0, 0), (0, 6)))

    o_sb, st_sb, jmin_sb, wo_s, wd_s = _sb_fwd(proj, 0, BQ, [shards[1], shards[3]])
    wo = wo_s.reshape(D_MODEL, D_MODEL)
    wgu = wgu_s.transpose(1, 0, 2).reshape(D_MODEL, 2 * D_FF)
    wg, wu = wgu[:, :D_FF], wgu[:, D_FF:]
    wd = wd_s.reshape(D_FF, D_MODEL)
    jstart_fx = _fox_start_blocks(*_fox_row_norms(proj, 12, TR), c, BQ, BQ)
    o_fx, st_fx = _fox_fwd(proj, 12, c_col, c_row, jstart_fx, BQ, BQ)

    def attn_post(i, osb_ref, ofx_ref, g_ref, he_ref, het_ref, on_ref):
        o = jnp.concatenate([osb_ref[...], ofx_ref[...]], axis=1)
        ms = _head_sums(o * o, he_ref[...], het_ref[...]) * (1.0 / HEAD_DIM)
        on_ref[...] = (o * lax.rsqrt(ms + RMS_EPS) * g_ref[...]).astype(BF16)

    (on,) = _rowwise(attn_post, "attn_post", S, TR,
                     [(o_sb, "t"), (o_fx, "t"), (g_row, "f"), (he, "f"), (het, "f")],
                     [((S, D_MODEL), BF16, "t")])

    u1 = _matmul(on, wo, mode="nn", name="mix", tm=TM, tn=D_MODEL, tk=D_MODEL, outs=[F32],
                 extras=[(x2, (TM if S >= TM else S, D_MODEL), _tile_ij)],
                 epilogue=lambda acc, xv: (ALPHA * xv + acc,))

    def ln1_fwd(i, u_ref, g_ref, b_ref, h_ref):
        xh, _ = _ln_stats(u_ref[...])
        h_ref[...] = xh * g_ref[...] + b_ref[...]

    (h1,) = _rowwise(ln1_fwd, "ln1_fwd", S, TR, [(u1, "t"), (ln1_g, "f"), (ln1_b, "f")], [((S, D_MODEL), F32, "t")])

    tm_e = TM if S >= TM else S
    n_ff = D_FF // 256

    def gate_up_body(h_ref, wg_ref, wu_ref, g_ref, u_ref, a_ref):
        h = h_ref[...].astype(BF16)
        g, u = _dot(h, wg_ref[...]), _dot(h, wu_ref[...])
        g_ref[...] = g.astype(BF16)
        u_ref[...] = u.astype(BF16)
        a_ref[...] = (g * _sigmoid(g) * u).astype(BF16)

    tm_g = min(2 * TM, S)
    ff_tile = pl.BlockSpec((tm_g, 256), lambda i, j: (i, j))
    gate, up, act = pl.pallas_call(
        gate_up_body, name="gate_up_act", grid=(S // tm_g, n_ff),
        in_specs=[pl.BlockSpec((tm_g, D_MODEL), lambda i, j: (i, 0)),
                  pl.BlockSpec((D_MODEL, 256), lambda i, j: (0, j)),
                  pl.BlockSpec((D_MODEL, 256), lambda i, j: (0, j + n_ff))],
        out_specs=[ff_tile] * 3, out_shape=[jax.ShapeDtypeStruct((S, D_FF), BF16)] * 3)(h1, wgu, wgu)

    u2 = _matmul(act, wd, mode="nn", name="ffn_down", tm=TM, tn=D_MODEL, tk=D_FF, outs=[F32],
                 extras=[(h1, (TM if S >= TM else S, D_MODEL), _tile_ij)],
                 epilogue=lambda acc, hv: (ALPHA * hv + acc,))

    def ln2_loss(i, u_ref, t_ref, g_ref, b_ref, du_ref, acc_ref):
        xh, r = _ln_stats(u_ref[...])
        g = g_ref[...]
        err = xh * g + b_ref[...] - t_ref[...]
        dy = err * (1.0 / D_MODEL)
        du_ref[...] = _ln_bwd(dy, xh, r, g)
        _acc_rows(i, acc_ref, {2: jnp.sum(dy * xh, axis=0, keepdims=True), 3: jnp.sum(dy, axis=0, keepdims=True),
                               6: jnp.sum(err * err, axis=0, keepdims=True) * (0.5 / D_MODEL)})

    du2, acc_ln2 = _rowwise(ln2_loss, "ln2_loss", S, TR, [(u2, "t"), (tgt, "t"), (ln2_g, "f"), (ln2_b, "f")],
                            [((S, D_MODEL), F32, "t"), ((8, D_MODEL), F32, "f")])

    d_wd = _matmul(act, du2, mode="tn", name="dw_down", tm=1408, tn=D_MODEL, tk=TM, outs=[BF16])

    def dgu_epilogue(da, g, u):
        g, u = g.astype(F32), u.astype(F32)
        s = _sigmoid(g)
        return da * u * (s * (1.0 + g * (1.0 - s))), da * (g * s)

    dgate, dup = _matmul(du2, wd, mode="nt", name="d_act", tm=TM, tn=1408, tk=D_MODEL, outs=[BF16, BF16],
                         extras=[(gate, (tm_e, 1408), _tile_ij), (up, (tm_e, 1408), _tile_ij)],
                         epilogue=dgu_epilogue)
    d_wg = _matmul(h1, dgate, mode="tn", name="dw_gate", tm=D_MODEL, tn=1408, tk=TM, outs=[BF16])
    d_wu = _matmul(h1, dup, mode="tn", name="dw_up", tm=D_MODEL, tn=1408, tk=TM, outs=[BF16])
    d_wgu = jnp.concatenate([d_wg, d_wu], axis=1)
    dh1, got_down = _matmul(dgate, wg, mode="nt", name="dh1_gate", tm=TM, tn=D_MODEL, tk=D_FF, outs=[F32],
                            extras=[(du2, (tm_e, D_MODEL), _tile_ij)], epilogue=lambda acc, e: (ALPHA * e + acc,),
                            hosted=[d_wd.reshape(4, D_FF // 4, D_MODEL)])
    dh1, got_gu = _matmul(dup, wu, mode="nt", name="dh1_up", tm=TM, tn=D_MODEL, tk=D_FF, outs=[F32],
                          extras=[(dh1, (tm_e, D_MODEL), _tile_ij)], epilogue=lambda acc, e: (e + acc,),
                          hosted=[d_wgu.reshape(D_MODEL, 4, gu_w).transpose(1, 0, 2)])

    def ln1_bwd(i, dh_ref, u_ref, g_ref, du_ref, acc_ref):
        xh, r = _ln_stats(u_ref[...])
        dh = dh_ref[...]
        du_ref[...] = _ln_bwd(dh, xh, r, g_ref[...])
        _acc_rows(i, acc_ref, {0: jnp.sum(dh * xh, axis=0, keepdims=True), 1: jnp.sum(dh, axis=0, keepdims=True)})

    du1, acc_ln1 = _rowwise(ln1_bwd, "ln1_bwd", S, TR, [(dh1, "t"), (u1, "t"), (ln1_g, "f")],
                            [((S, D_MODEL), F32, "t"), ((8, D_MODEL), F32, "f")])
    d_wo = _matmul(on, du1, mode="tn", name="dw_out", tm=D_MODEL, tn=D_MODEL, tk=TM, outs=[BF16])
    don, got_out = _matmul(du1, wo, mode="nt", name="d_on", tm=TM, tn=D_MODEL, tk=D_MODEL, outs=[F32],
                           hosted=[d_wo.reshape(4, D_MODEL // 4, D_MODEL)])

    def rms_bwd(i, don_ref, osb_ref, ofx_ref, g_ref, he_ref, het_ref, dosb_ref, dofx_ref, acc_ref):
        o = jnp.concatenate([osb_ref[...], ofx_ref[...]], axis=1)
        hev, hetv = he_ref[...], het_ref[...]
        r = lax.rsqrt(_head_sums(o * o, hev, hetv) * (1.0 / HEAD_DIM) + RMS_EPS)
        dn = don_ref[...]
        dg = dn * g_ref[...]
        do = r * dg - o * (r * r * r) * (_head_sums(dg * o, hev, hetv) * (1.0 / HEAD_DIM))
        dosb_ref[...] = do[:, :GROUP_W]
        dofx_ref[...] = do[:, GROUP_W:]
        _acc_rows(i, acc_ref, {4: jnp.sum(dn * o * r, axis=0, keepdims=True)})

    do_sb, do_fx, acc_rms = _rowwise(
        rms_bwd, "rms_bwd", S, TR, [(don, "t"), (o_sb, "t"), (o_fx, "t"), (g_row, "f"), (he, "f"), (het, "f")],
        [((S, GROUP_W), F32, "t"), ((S, GROUP_W), F32, "t"), ((8, D_MODEL), F32, "f")])

    dq_sb, dk_sb, dv_sb = _sb_bwd(proj, 0, do_sb, st_sb, jmin_sb, BQ)
    jstart_fx2 = jnp.minimum(jstart_fx[:, 0::2], jstart_fx[:, 1::2])
    dq_fx, dk_fx, dv_fx, dc = _fox_bwd(proj, 12, do_fx, o_fx, st_fx, c_col, c_row, jstart_fx2, 2 * BQ, BQ)
    dfl, dbf = _fgate_bwd(dc[:, :2, :].reshape(N_FOX, S), lf)
    dp_sb = jnp.concatenate([dq_sb, dk_sb, dv_sb], axis=1)
    dp_fx = jnp.concatenate([dq_fx, dk_fx, dv_fx], axis=1)

    d_wsb = _matmul(x2, dp_sb, mode="tn", name="dw_in_sb", tm=D_MODEL, tn=QKV_W // 2, tk=TM, outs=[BF16])
    d_wfx = _matmul(x2, dp_fx, mode="tn", name="dw_in_fx", tm=D_MODEL, tn=QKV_W // 2, tk=TM, outs=[BF16])
    d_wft = _matmul(dfl, x2, mode="nn", name="dw_in_f", tm=N_FOX, tn=D_MODEL, tk=TM, outs=[BF16])
    d_wi = jnp.concatenate([d_wsb, d_wfx, d_wft.T], axis=1)
    dx, got_in = _matmul(dp_sb, w_sb, mode="nt", name="dx_sb", tm=TM, tn=D_MODEL, tk=QKV_W // 2, outs=[F32],
                         extras=[(du1, (tm_e, D_MODEL), _tile_ij)], epilogue=lambda acc, e: (ALPHA * e + acc,),
                         hosted=[d_wi.reshape(D_MODEL, 4, in_w).transpose(1, 0, 2)])
    dx = _matmul(dp_fx, w_fx, mode="nt", name="dx_fx", tm=TM, tn=D_MODEL, tk=QKV_W // 2, outs=[F32],
                 extras=[(dx, (tm_e, D_MODEL), _tile_ij)], epilogue=lambda acc, e: (e + acc,))
    dx = _matmul(dfl, wft, mode="tn", name="dx_f", tm=TM, tn=D_MODEL, tk=N_FOX, outs=[F32],
                 extras=[(dx, (tm_e, D_MODEL), _tile_ij)], epilogue=lambda acc, e: (e + acc,))

    got = [got_in, got_out, got_gu, got_down]
    big_names = ("w_in", "w_out", "w_gate_up", "w_down")
    halves = [_sum_parts(p, "sum_" + nm, tr) for nm, p, tr in zip(big_names, got, (256, 128, 128, 176))]
    grads = _sibling_swap(halves)
    big = {}
    for nm, g, w, m, v, tr in zip(big_names, grads, (w_in, w_out, w_gate_up, w_down),
                                  (m_w_in, m_w_out, m_w_gate_up, m_w_down),
                                  (v_w_in, v_w_out, v_w_gate_up, v_w_down), (256, 256, 256, 176)):
        big[nm] = [r[None] for r in [g] + list(_adamw_call(g, w[0], m[0], v[0], "adamw_" + nm, tr))]

    small = acc_ln2 + acc_ln1 + acc_rms
    small = small + jnp.pad(dbf.reshape(1, N_FOX), ((5, 2), (0, D_MODEL - N_FOX)))
    (small_all,) = _exchange([small], False)
    sw = _pack_small(ln1_g, ln1_b, ln2_g, ln2_b, g_sb, g_fox, b_f)
    sm = _pack_small(m_ln1_g, m_ln1_b, m_ln2_g, m_ln2_b, m_g_sb, m_g_fox, m_b_f)
    sv = _pack_small(v_ln1_g, v_ln1_b, v_ln2_g, v_ln2_b, v_g_sb, v_g_fox, v_b_f)
    sg, sd, snm, snv, loss_blk = _sum_adamw_small(small_all, sw, sm, sv)
    sg, sd, snm, snv = _unpack_small(sg), _unpack_small(sd), _unpack_small(snm), _unpack_small(snv)

    names = ["w_in", "b_f", "g_sb", "g_fox", "w_out", "ln1_g", "ln1_b", "ln2_g", "ln2_b", "w_gate_up", "w_down"]
    outs = [loss_blk[0, 0], dx.reshape(1, S, D_MODEL)]
    for k, table in enumerate((sg, sd, snm, snv)):
        outs += [big[n][k] if n in big else table[n] for n in names]
    return tuple(outs)
```

```python
import functools

import numpy as np
import jax
import jax.numpy as jnp
from jax import lax
from jax.experimental import pallas as pl
from jax.experimental.pallas import tpu as pltpu

F32 = jnp.float32
BF16 = jnp.bfloat16

D_MODEL = 1024
HEAD_DIM = 64
LANES = 128
N_PAIRS = 4
GROUP_W = 512
QKV_W = 3072
D_FF = 2816
N_FOX = 8
ALPHA = 2.0 ** 0.25
LN_EPS = 1e-5
RMS_EPS = 1e-6
SCALE = HEAD_DIM ** -0.5
NEG_BIG = -1e30
FOX_SKIP = 30.0
SB_STOP = -105.0
ADAM_LR, ADAM_B1, ADAM_B2, ADAM_EPS, ADAM_WD, ADAM_STEP = 0.001, 0.9, 0.999, 1e-08, 0.01, 10
KV_SLOTS = 4
SCAN_GROUP = 8
ATTN_BLOCK = 256
VMEM_BIG = 56 * 1024 * 1024
MESH = pl.DeviceIdType.MESH

_NN = (((1,), (0,)), ((), ()))
_NT = (((1,), (1,)), ((), ()))
_TN = (((0,), (0,)), ((), ()))


def _dot(a, b, dims=_NN):
    return lax.dot_general(a, b, dims, preferred_element_type=F32)


def _split_dot(x, t):
    hi = x.astype(BF16)
    lo = (x - hi.astype(F32)).astype(BF16)
    return _dot(hi, t) + _dot(lo, t)


def _softplus(z):
    return jnp.maximum(z, 0.0) + jnp.log1p(jnp.exp(-jnp.abs(z)))


def _sigmoid(x):
    return 0.5 * jnp.tanh(0.5 * x) + 0.5


def _col(v, h):
    lane = lax.broadcasted_iota(jnp.int32, v.shape, 1)
    return jnp.sum(jnp.where(lane == h, v, 0.0), axis=1, keepdims=True)


def _two_sum(hi, lo, b):
    s = hi + b
    bb = s - hi
    err = (hi - (s - bb)) + (b - bb)
    return s, lo + err


def _params(vmem=None):
    return pltpu.CompilerParams(vmem_limit_bytes=vmem) if vmem else None


def _matmul(a, b, *, mode, name, tm, tn, tk, outs, extras=(), epilogue=None, vmem=None, hosted=()):
    if mode == "nn":
        (M, K), (_, N) = a.shape, b.shape
    elif mode == "nt":
        (M, K), (N, _) = a.shape, b.shape
    else:
        (K, M), (_, N) = a.shape, b.shape
    tm, tn, tk = min(tm, M), min(tn, N), min(tk, K)
    assert M % tm == 0 and N % tn == 0 and K % tk == 0, (name, M, N, K, tm, tn, tk)
    nk = K // tk
    dims = {"nn": _NN, "nt": _NT, "tn": _TN}[mode]
    if mode == "tn":
        a_spec = pl.BlockSpec((tk, tm), lambda i, j, k: (k, i))
    else:
        a_spec = pl.BlockSpec((tm, tk), lambda i, j, k: (i, k))
    if mode == "nt":
        b_spec = pl.BlockSpec((tn, tk), lambda i, j, k: (j, k))
    else:
        b_spec = pl.BlockSpec((tk, tn), lambda i, j, k: (k, j))
    ex_specs = [pl.BlockSpec(bs, (lambda i, j, k, f=f: f(i, j))) for (_, bs, f) in extras]
    ne, no, nh = len(extras), len(outs), len(hosted)
    if epilogue is None:
        epilogue = lambda acc: (acc,)
    gi, gj = M // tm, N // tn
    host_shapes, host_sems = _exchange_shapes(hosted, True) if nh else ([], [])

    def body(a_ref, b_ref, *rest):
        ex_refs, host_ins = rest[:ne], rest[ne:ne + nh]
        out_refs, host_outs = rest[ne + nh:ne + nh + no], rest[ne + nh + no:ne + 2 * nh + no]
        acc = rest[ne + 2 * nh + no]
        i, j, k = pl.program_id(0), pl.program_id(1), pl.program_id(2)
        if nh:
            start, wait = _exchange_copies(host_ins, host_outs, *rest[ne + 2 * nh + no + 1:], True, hosted)
            pl.when(jnp.logical_and(jnp.logical_and(i == 0, j == 0), k == 0))(start)

        @pl.when(k == 0)
        def _():
            acc[...] = jnp.zeros_like(acc)

        acc[...] += _dot(a_ref[...].astype(BF16), b_ref[...].astype(BF16), dims)

        @pl.when(k == nk - 1)
        def _():
            res = epilogue(acc[...], *[e[...] for e in ex_refs])
            for r, o in zip(res, out_refs):
                o[...] = r.astype(o.dtype)

        if nh:
            pl.when(jnp.logical_and(jnp.logical_and(i == gi - 1, j == gj - 1), k == nk - 1))(wait)

    res = pl.pallas_call(
        body, name=name, grid=(gi, gj, nk),
        in_specs=[a_spec, b_spec] + ex_specs + [_HBM] * nh,
        out_specs=[pl.BlockSpec((tm, tn), lambda i, j, k: (i, j)) for _ in outs] + [_HBM] * nh,
        out_shape=[jax.ShapeDtypeStruct((M, N), d) for d in outs] + host_shapes,
        scratch_shapes=[pltpu.VMEM((tm, tn), F32)] + host_sems,
        compiler_params=_params(vmem),
    )(a, b, *[e[0] for e in extras], *hosted)
    return res[0] if no + nh == 1 else res


def _tile_ij(i, j):
    return (i, j)


def _rowwise(fn, name, rows, tm, ins, outs, vmem=None):
    tm = min(tm, rows)
    assert rows % tm == 0

    def spec(shape, kind):
        if kind == "t":
            return pl.BlockSpec((tm,) + tuple(shape[1:]), lambda i: (i,) + (0,) * (len(shape) - 1))
        return pl.BlockSpec(tuple(shape), lambda i: (0,) * len(shape))

    def body(*refs):
        fn(pl.program_id(0), *refs)

    return pl.pallas_call(
        body, name=name, grid=(rows // tm,),
        in_specs=[spec(a.shape, k) for a, k in ins],
        out_specs=[spec(s, k) for s, _, k in outs],
        out_shape=[jax.ShapeDtypeStruct(s, d) for s, d, _ in outs],
        compiler_params=_params(vmem),
    )(*[a for a, _ in ins])


def _ln_stats(u):
    mu = jnp.mean(u, axis=-1, keepdims=True)
    d = u - mu
    var = jnp.mean(d * d, axis=-1, keepdims=True)
    r = lax.rsqrt(var + LN_EPS)
    return d * r, r


def _ln_bwd(dh, xh, r, g):
    dxh = dh * g
    m1 = jnp.mean(dxh, axis=-1, keepdims=True)
    m2 = jnp.mean(dxh * xh, axis=-1, keepdims=True)
    return r * (dxh - m1 - xh * m2)


def _acc_rows(i, ref, rows):
    @pl.when(i == 0)
    def _():
        ref[...] = jnp.zeros_like(ref)
    for r, v in rows.items():
        ref[pl.ds(r, 1), :] += v


def _head_sums(v, he, het):
    return _split_dot(_split_dot(v, he), het)


def _fgate_fwd(x, wft, bf_col, tm):
    S = x.shape[0]
    tm = min(tm, S)

    def body(wft_ref, bf_ref, x_ref, lf_ref):
        f = _dot(wft_ref[...], x_ref[...].astype(BF16), _NT) + bf_ref[...]
        lf_ref[...] = -_softplus(-f)

    return pl.pallas_call(
        body, name="fgate_fwd", grid=(S // tm,),
        in_specs=[pl.BlockSpec((N_FOX, D_MODEL), lambda i: (0, 0)), pl.BlockSpec((N_FOX, 1), lambda i: (0, 0)),
                  pl.BlockSpec((tm, D_MODEL), lambda i: (i, 0))],
        out_specs=pl.BlockSpec((N_FOX, tm), lambda i: (0, i)),
        out_shape=jax.ShapeDtypeStruct((N_FOX, S), F32),
    )(wft, bf_col, x)


def _chunk_scan(v, reverse):
    lane = lax.broadcasted_iota(jnp.int32, v.shape, 1)
    sh = 1
    while sh < LANES:
        if reverse:
            v = v + jnp.where(lane < LANES - sh, pltpu.roll(v, LANES - sh, 1), 0.0)
        else:
            v = v + jnp.where(lane >= sh, pltpu.roll(v, sh, 1), 0.0)
        sh *= 2
    return v


def _cumsum_fwd(lf):
    n, S = lf.shape
    nc = S // LANES

    grp = min(SCAN_GROUP, nc)

    def body(lf_ref, c_ref):
        def step(gi, carry):
            sls = [pl.ds(pl.multiple_of((gi * grp + g) * LANES, LANES), LANES) for g in range(grp)]
            vs = [_chunk_scan(lf_ref[:, sl], False) for sl in sls]
            tots = [_col(v, LANES - 1) for v in vs]
            for sl, v, t in zip(sls, vs, tots):
                c_ref[:, sl] = v + carry
                carry = carry + t
            return carry
        lax.fori_loop(0, nc // grp, step, jnp.zeros((n, 1), F32))

    return pl.pallas_call(body, name="cumsum_fwd", out_shape=jax.ShapeDtypeStruct((n, S), F32))(lf)


def _fgate_bwd(dc, lf):
    n, S = dc.shape
    nc = S // LANES

    grp = min(SCAN_GROUP, nc)

    def body(dc_ref, lf_ref, dfl_ref, dbf_ref):
        def step(t, carry):
            car, tot = carry
            gi = nc // grp - 1 - t
            sls = [pl.ds(pl.multiple_of((gi * grp + g) * LANES, LANES), LANES) for g in range(grp)]
            vs = [_chunk_scan(dc_ref[:, sl], True) for sl in sls]
            firsts = [_col(v, 0) for v in vs]
            for sl, v, f in reversed(list(zip(sls, vs, firsts))):
                dfl = (v + car) * (1.0 - jnp.exp(lf_ref[:, sl]))
                dfl_ref[:, sl] = dfl
                tot = tot + jnp.sum(dfl, axis=1, keepdims=True)
                car = car + f
            return car, tot
        _, tot = lax.fori_loop(0, nc // grp, step, (jnp.zeros((n, 1), F32), jnp.zeros((n, 1), F32)))
        dbf_ref[...] = tot

    return pl.pallas_call(body, name="fgate_bwd",
                          out_shape=[jax.ShapeDtypeStruct((n, S), F32), jax.ShapeDtypeStruct((n, 1), F32)])(dc, lf)


def _tri_matrices(b):
    r = np.arange(b)
    tfwd = (r[:, None] <= r[None, :]).astype(np.float32)
    return jnp.asarray(tfwd, BF16), jnp.asarray(tfwd.T, BF16)


def _kv_copies(kv_hbm, kbuf, vbuf, sems, sem0, pair_col, bq, j, slot):
    rows = pl.ds(pl.multiple_of(j * bq, bq), bq)

    def cols(c):
        return pl.ds(pl.multiple_of((pair_col + c) * LANES, LANES), LANES)

    return (pltpu.make_async_copy(kv_hbm.at[rows, cols(4)], kbuf.at[slot], sems.at[0, sem0 + slot]),
            pltpu.make_async_copy(kv_hbm.at[rows, cols(8)], vbuf.at[slot], sems.at[1, sem0 + slot]))


def _first_two_up(first_block, per=1):
    def blocks(pair, blk):
        first = first_block(pair, blk)
        return first, first + 1, first + 1 <= per * blk + per - 1
    return blocks


def _first_two_down(pair, blk):
    return blk, blk - 1, blk > 0


def _start_two(fetch, pair, first, second, has_second, ahead):
    for cp in fetch(first, 0, pair, ahead):
        cp.start()

    @pl.when(has_second)
    def _():
        for cp in fetch(second, 1, pair, ahead):
            cp.start()


def _kv_fetcher(kv_hbm, kbuf, vbuf, sems, ns, col0, bq, p, i, nq, blocks):
    base = lax.rem(p * nq + i, 2) * ns
    own = (kbuf.at[pl.ds(base, ns)], vbuf.at[pl.ds(base, ns)])
    other = (kbuf.at[pl.ds(ns - base, ns)], vbuf.at[pl.ds(ns - base, ns)])

    def fetch(j, slot, pair=p, ahead=False):
        kb, vb = other if ahead else own
        return _kv_copies(kv_hbm, kb, vb, sems, ns - base if ahead else base, col0 + pair, bq, j, slot)

    pl.when(jnp.logical_and(p == 0, i == 0))(lambda: _start_two(fetch, p, *blocks(p, i), False))
    wrap = i == nq - 1

    @pl.when(jnp.logical_not(jnp.logical_and(wrap, p == N_PAIRS - 1)))
    def _():
        pair, blk = jnp.where(wrap, p + 1, p), jnp.where(wrap, 0, i + 1)
        _start_two(fetch, pair, *blocks(pair, blk), True)

    return fetch, own[0], own[1]


def _masked_pair(v, lane_is_a, scale=1.0):
    v = v.astype(F32) * scale
    return jnp.where(lane_is_a, v, 0.0).astype(BF16), jnp.where(lane_is_a, 0.0, v).astype(BF16)


def _sb_fwd(proj, col0, bq, shards=()):
    S = proj.shape[0]
    bq = min(bq, S)
    nq = S // bq
    _, trev = _tri_matrices(bq)
    nh = len(shards)
    gather_shapes, gather_sems = _gather_shapes(shards) if nh else ([], [])

    def body(q_ref, kv_hbm, trev_ref, *rest):
        o_ref, st_ref, jmin_ref = rest[nh:nh + 3]
        acc_a, acc_b, qa, qb, rs, kbuf, vbuf, sems = rest[2 * nh + 3:2 * nh + 11]
        p, i = pl.program_id(0), pl.program_id(1)
        if nh:
            gather_start, gather_wait = _gather_copies(rest[:nh], rest[nh + 3:2 * nh + 3], *rest[2 * nh + 11:])
            pl.when(jnp.logical_and(p == 0, i == 0))(gather_start)
        fetch, kbuf, vbuf = _kv_fetcher(kv_hbm, kbuf, vbuf, sems, 2, col0, bq, p, i, nq, _first_two_down)
        is_a = lax.broadcasted_iota(jnp.int32, (bq, LANES), 1) < HEAD_DIM
        acc_a[...] = jnp.zeros_like(acc_a)
        acc_b[...] = jnp.zeros_like(acc_b)
        rs[...] = jnp.zeros_like(rs)
        qa[...], qb[...] = _masked_pair(q_ref[...], is_a, SCALE)

        def tiles(blocks):
            hs, qs, accs, trev_m = (0, 1), (qa, qb), (acc_a, acc_b), trev_ref[...]
            kv = [(kbuf[s], vbuf[s]) for s, _ in blocks]
            bh = [(b, h) for b in range(len(blocks)) for h in hs]
            tri = lax.broadcasted_iota(jnp.int32, (bq, bq), 0) > lax.broadcasted_iota(jnp.int32, (bq, bq), 1)
            z = {(b, h): _dot(qs[h][...], kv[b][0], _NT) for b, h in bh}
            lk = {(b, h): -_softplus(z[b, h]) for b, h in bh}
            lk = {(b, h): jnp.where(tri, lk[b, h], 0.0) if blocks[b][1] else lk[b, h] for b, h in bh}
            suf = {(b, h): _split_dot(lk[b, h], trev_m) for b, h in bh}
            tot = {(b, h): jnp.sum(lk[b, h], axis=1, keepdims=True) for b, h in bh}
            right = {}
            for h in hs:
                r = rs[2 * h] + rs[2 * h + 1]
                for b in range(len(blocks)):
                    right[b, h] = r
                    r = r + tot[b, h]
            w = {(b, h): jnp.exp(z[b, h] + suf[b, h] + right[b, h]) for b, h in bh}
            w = {(b, h): jnp.where(tri, w[b, h], 0.0) if blocks[b][1] else w[b, h] for b, h in bh}
            pv = {(b, h): _dot(w[b, h].astype(BF16), kv[b][1]) for b, h in bh}
            for h in hs:
                accs[h][...] += sum([pv[b, h] for b in range(1, len(blocks))], pv[0, h])
                hi, lo = rs[2 * h], rs[2 * h + 1]
                for b in range(len(blocks)):
                    hi, lo = _two_sum(hi, lo, tot[b, h])
                rs[2 * h], rs[2 * h + 1] = hi, lo

        def live():
            return (jnp.max(jnp.maximum(rs[0], rs[2])) > SB_STOP).astype(jnp.int32)

        for cp in fetch(i, 0):
            cp.wait()
        pl.when(i == 0)(functools.partial(tiles, [(0, True)]))

        @pl.when(i > 0)
        def _():
            for cp in fetch(i - 1, 1):
                cp.wait()
            tiles([(0, True), (1, False)])

        def step(carry):
            j, _ = carry
            slot = lax.rem(i - j, 2)
            for cp in fetch(j, slot):
                cp.start()
            for cp in fetch(j, slot):
                cp.wait()
            tiles([(slot, False)])
            return j - 1, live()

        j_end, _ = lax.while_loop(lambda c: jnp.logical_and(c[0] >= 0, c[1] > 0), step, (i - 2, live()))
        jmin_ref[p, i] = jnp.maximum(j_end + 1, 0)
        o_ref[...] = jnp.where(is_a, acc_a[...], acc_b[...])
        lane8 = lax.broadcasted_iota(jnp.int32, (bq, 8), 1)
        st = jnp.zeros((bq, 8), F32)
        for c, src in enumerate((0, 2, 1, 3)):
            st = jnp.where(lane8 == c, rs[src], st)
        st_ref[0] = st
        if nh:
            pl.when(jnp.logical_and(p == N_PAIRS - 1, i == nq - 1))(gather_wait)

    return pl.pallas_call(
        body, name="sb_fwd", grid=(N_PAIRS, nq),
        in_specs=[pl.BlockSpec((bq, LANES), lambda p, i: (i, col0 + p)),
                  pl.BlockSpec(memory_space=pl.ANY),
                  pl.BlockSpec((bq, bq), lambda p, i: (0, 0))] + [_HBM] * nh,
        out_specs=[pl.BlockSpec((bq, LANES), lambda p, i: (i, p)),
                   pl.BlockSpec((1, bq, 8), lambda p, i: (p, i, 0)),
                   pl.BlockSpec(memory_space=pltpu.SMEM)] + [_HBM] * nh,
        out_shape=[jax.ShapeDtypeStruct((S, GROUP_W), F32), jax.ShapeDtypeStruct((N_PAIRS, S, 8), F32),
                   jax.ShapeDtypeStruct((N_PAIRS, nq), jnp.int32)] + gather_shapes,
        scratch_shapes=[pltpu.VMEM((bq, LANES), F32), pltpu.VMEM((bq, LANES), F32),
                        pltpu.VMEM((bq, LANES), BF16), pltpu.VMEM((bq, LANES), BF16),
                        pltpu.VMEM((4, bq, 1), F32),
                        pltpu.VMEM((4, bq, LANES), BF16), pltpu.VMEM((4, bq, LANES), BF16),
                        pltpu.SemaphoreType.DMA((2, 4))] + gather_sems,
    )(proj, proj, trev, *shards)


def _sb_bwd(proj, col0, do, st, jmin, bq):
    S = proj.shape[0]
    bq = min(bq, S)
    nq = S // bq
    tfwd, trev = _tri_matrices(bq)

    def body(jmin_ref, q_ref, kv_hbm, do_ref, st_ref, tfwd_ref, trev_ref,
             dq_ref, dk_out, dv_out, dq_a, dq_b, qa, qb, doa, dob, rs, kbuf, vbuf, sems, dk_ref, dv_ref):
        p, i = pl.program_id(0), pl.program_id(1)
        j0 = jmin_ref[p, i]
        first_two = _first_two_up(lambda pair, blk: jmin_ref[pair, blk])
        fetch, kbuf, vbuf = _kv_fetcher(kv_hbm, kbuf, vbuf, sems, KV_SLOTS, col0, bq, p, i, nq, first_two)
        is_a = lax.broadcasted_iota(jnp.int32, (bq, LANES), 1) < HEAD_DIM

        @pl.when(i == 0)
        def _():
            dk_ref[...] = jnp.zeros_like(dk_ref)
            dv_ref[...] = jnp.zeros_like(dv_ref)

        dq_a[...] = jnp.zeros_like(dq_a)
        dq_b[...] = jnp.zeros_like(dq_b)
        rs[...] = jnp.zeros_like(rs)
        st_v = st_ref[0]
        for h in range(2):
            rs[6 + 2 * h], rs[7 + 2 * h] = _col(st_v, h), _col(st_v, 2 + h)
        qa[...], qb[...] = _masked_pair(q_ref[...], is_a, SCALE)
        doa[...], dob[...] = _masked_pair(do_ref[...], is_a)

        def tiles(blocks):
            hs, qs, dos, dqs = (0, 1), (qa, qb), (doa, dob), (dq_a, dq_b)
            tfwd_m, trev_m = tfwd_ref[...], trev_ref[...]
            kv = [(kbuf[s], vbuf[s]) for _, s, _ in blocks]
            nb = len(blocks)
            bh = [(b, h) for b in range(nb) for h in hs]
            tri = lax.broadcasted_iota(jnp.int32, (bq, bq), 0) > lax.broadcasted_iota(jnp.int32, (bq, bq), 1)

            def mask(x, b):
                return jnp.where(tri, x, 0.0) if blocks[b][2] else x

            z = {(b, h): _dot(qs[h][...], kv[b][0], _NT) for b, h in bh}
            dw = {(b, h): _dot(dos[h][...], kv[b][1], _NT) for b, h in bh}
            lk = {(b, h): mask(-_softplus(z[b, h]), b) for b, h in bh}
            suf = {(b, h): _split_dot(lk[b, h], trev_m) for b, h in bh}
            tot = {(b, h): jnp.sum(lk[b, h], axis=1, keepdims=True) for b, h in bh}
            pre = {}
            for h in hs:
                run = (rs[3 * h], rs[3 * h + 1])
                for b in range(nb):
                    run = _two_sum(run[0], run[1], tot[b, h])
                    pre[b, h] = run
            right = {(b, h): (rs[6 + 2 * h] - pre[b, h][0]) + (rs[7 + 2 * h] - pre[b, h][1]) for b, h in bh}
            w = {(b, h): mask(jnp.exp(z[b, h] + suf[b, h] + right[b, h]), b) for b, h in bh}
            g = {(b, h): dw[b, h] * w[b, h] for b, h in bh}
            gpre = {(b, h): _split_dot(g[b, h], tfwd_m) for b, h in bh}
            gtot = {(b, h): jnp.sum(g[b, h], axis=1, keepdims=True) for b, h in bh}
            gleft = {}
            for h in hs:
                run = rs[3 * h + 2]
                for b in range(nb):
                    gleft[b, h] = run
                    run = run + gtot[b, h]
                gleft[nb, h] = run
            dz = {(b, h): mask(g[b, h] - jnp.exp(z[b, h] + lk[b, h]) * (gpre[b, h] + gleft[b, h]), b) for b, h in bh}
            dzb = {(b, h): dz[b, h].astype(BF16) for b, h in bh}
            wb = {(b, h): w[b, h].astype(BF16) for b, h in bh}
            dqc = {(b, h): _dot(dzb[b, h], kv[b][0]) for b, h in bh}
            dkc = {(b, h): _dot(dzb[b, h], qs[h][...], _TN) for b, h in bh}
            dvc = {(b, h): _dot(wb[b, h], dos[h][...], _TN) for b, h in bh}
            for h in hs:
                rs[3 * h], rs[3 * h + 1] = pre[nb - 1, h]
                rs[3 * h + 2] = gleft[nb, h]
                dqs[h][...] += sum([dqc[b, h] for b in range(1, nb)], dqc[0, h])
            for b, (j, _, _) in enumerate(blocks):
                rows = pl.ds(pl.multiple_of(j * bq, bq), bq)
                dk_ref[rows, :] += dkc[b, 0] + dkc[b, 1]
                dv_ref[rows, :] += dvc[b, 0] + dvc[b, 1]

        def single(j, slot, masked):
            tiles([(j, slot, masked)])

        def wait(j):
            slot = lax.rem(j - j0, KV_SLOTS)
            for cp in fetch(j, slot):
                cp.wait()
            return slot

        _walk_up(fetch, j0, i, i, single, stop=jnp.maximum(i - 1, j0))

        @pl.when(j0 < i)
        def _():
            tiles([(i - 1, wait(i - 1), False), (i, wait(i), True)])

        @pl.when(j0 == i)
        def _():
            tiles([(i, wait(i), True)])

        dq_ref[...] = (jnp.where(is_a, dq_a[...], dq_b[...]) * SCALE).astype(BF16)

        @pl.when(i == nq - 1)
        def _():
            dk_out[...] = dk_ref[...].astype(BF16)
            dv_out[...] = dv_ref[...].astype(BF16)

    grid_spec = pltpu.PrefetchScalarGridSpec(
        num_scalar_prefetch=1, grid=(N_PAIRS, nq),
        in_specs=[pl.BlockSpec((bq, LANES), lambda p, i, jm: (i, col0 + p)),
                  pl.BlockSpec(memory_space=pl.ANY),
                  pl.BlockSpec((bq, LANES), lambda p, i, jm: (i, p)),
                  pl.BlockSpec((1, bq, 8), lambda p, i, jm: (p, i, 0)),
                  pl.BlockSpec((bq, bq), lambda p, i, jm: (0, 0)),
                  pl.BlockSpec((bq, bq), lambda p, i, jm: (0, 0))],
        out_specs=[pl.BlockSpec((bq, LANES), lambda p, i, jm: (i, p)),
                   pl.BlockSpec((S, LANES), lambda p, i, jm: (0, p)),
                   pl.BlockSpec((S, LANES), lambda p, i, jm: (0, p))],
        scratch_shapes=[pltpu.VMEM((bq, LANES), F32), pltpu.VMEM((bq, LANES), F32)]
        + [pltpu.VMEM((bq, LANES), BF16)] * 4 + [pltpu.VMEM((10, bq, 1), F32)]
        + [pltpu.VMEM((2 * KV_SLOTS, bq, LANES), BF16)] * 2 + [pltpu.SemaphoreType.DMA((2, 2 * KV_SLOTS))]
        + [pltpu.VMEM((S, LANES), F32)] * 2)
    return pl.pallas_call(
        body, name="sb_bwd", grid_spec=grid_spec,
        out_shape=[jax.ShapeDtypeStruct((S, GROUP_W), BF16)] * 3,
        compiler_params=_params(VMEM_BIG),
    )(jmin, proj, proj, do, st, tfwd, trev)


def _walk_up(fetch, j0, diag, last, tile, stop=None):
    ahead = KV_SLOTS - 1
    stop = last + 1 if stop is None else stop

    def start(j):
        @pl.when(j <= last)
        def _():
            for cp in fetch(j, lax.rem(j - j0, KV_SLOTS)):
                cp.start()

    for d in range(2, ahead):
        start(j0 + d)

    def step(j, carry):
        slot = lax.rem(j - j0, KV_SLOTS)
        for cp in fetch(j, slot):
            cp.wait()
        start(j + ahead)
        pl.when(j >= diag)(functools.partial(tile, j, slot, True))
        pl.when(j < diag)(functools.partial(tile, j, slot, False))
        return carry

    lax.fori_loop(j0, stop, step, 0)


def _causal(bq, bk, i, j):
    row = lax.broadcasted_iota(jnp.int32, (bq, bk), 0)
    col = lax.broadcasted_iota(jnp.int32, (bq, bk), 1)
    return col - row <= i * bq - j * bk


def _by_heads(j, first_a, first_b, heads):
    on_a, on_b = j >= first_a, j >= first_b
    pl.when(jnp.logical_and(on_a, on_b))(functools.partial(heads, (0, 1)))
    pl.when(jnp.logical_and(on_a, jnp.logical_not(on_b)))(functools.partial(heads, (0,)))
    pl.when(jnp.logical_and(on_b, jnp.logical_not(on_a)))(functools.partial(heads, (1,)))


def _fox_row_norms(proj, col0, tm):
    S = proj.shape[0]
    tm = min(tm, S)
    head_of = np.arange(GROUP_W) // HEAD_DIM
    he_t = jnp.asarray((np.arange(2 * N_PAIRS)[:, None] == head_of[None, :]).astype(np.float32), BF16)

    def body(q_ref, k_ref, he_ref, qn_ref, kn_ref, d_ref):
        q, k, he = q_ref[...].astype(F32), k_ref[...].astype(F32), he_ref[...]

        def head_sums_t(x):
            hi = x.astype(BF16)
            lo = (x - hi.astype(F32)).astype(BF16)
            return _dot(he, hi, _NT) + _dot(he, lo, _NT)

        qn_ref[...] = jnp.sqrt(head_sums_t(q * q))
        kn_ref[...] = jnp.sqrt(head_sums_t(k * k))
        d_ref[...] = SCALE * head_sums_t(q * k)

    wide = GROUP_W // LANES
    return pl.pallas_call(
        body, name="fox_row_norms", grid=(S // tm,),
        in_specs=[pl.BlockSpec((tm, GROUP_W), lambda i: (i, col0 // wide)),
                  pl.BlockSpec((tm, GROUP_W), lambda i: (i, (col0 + 4) // wide)),
                  pl.BlockSpec((2 * N_PAIRS, GROUP_W), lambda i: (0, 0))],
        out_specs=[pl.BlockSpec((2 * N_PAIRS, tm), lambda i: (0, i))] * 3,
        out_shape=[jax.ShapeDtypeStruct((2 * N_PAIRS, S), F32)] * 3)(proj, proj, he_t)


def _fox_start_blocks(qn, kn, d, c, bq, bk):
    nh, S = c.shape
    nq, nk = S // bq, S // bk
    top = SCALE * qn * kn.max(axis=1, keepdims=True) - d + c
    top = top.reshape(nh, nq, bq).max(axis=2)
    c_last = c[:, bk - 1::bk]
    live = top[:, :, None] - c_last[:, None, :] >= -FOX_SKIP

    def first_block(lv):
        first = jnp.where(lv.any(axis=2), jnp.argmax(lv, axis=2), nk)
        return jnp.minimum(first, (bq // bk) * jnp.arange(nq)[None, :]).astype(jnp.int32)

    return jnp.concatenate([first_block(live.reshape(N_PAIRS, 2, nq, nk).any(axis=1)), first_block(live)], axis=0)


def _fox_fwd(proj, col0, c_col, c_row, jstart, bq, bk):
    S = proj.shape[0]
    nq, per = S // bq, bq // bk

    def body(js_ref, q_ref, kv_hbm, cc_ref, cr_ref, o_ref, st_ref, acc_a, acc_b, qa, qb, ml, kbuf, vbuf, sems):
        p, i = pl.program_id(0), pl.program_id(1)
        j0 = js_ref[p, i]
        first_two = _first_two_up(lambda pair, blk: js_ref[pair, blk], per)
        fetch, kbuf, vbuf = _kv_fetcher(kv_hbm, kbuf, vbuf, sems, KV_SLOTS, col0, bk, p, i, nq, first_two)
        is_a = lax.broadcasted_iota(jnp.int32, (bq, LANES), 1) < HEAD_DIM
        acc_a[...] = jnp.zeros_like(acc_a)
        acc_b[...] = jnp.zeros_like(acc_b)
        ml[0] = jnp.full((bq, 1), NEG_BIG, F32)
        ml[2] = jnp.full((bq, 1), NEG_BIG, F32)
        ml[1] = jnp.zeros((bq, 1), F32)
        ml[3] = jnp.zeros((bq, 1), F32)
        cc = cc_ref[0]
        ml[4], ml[5] = _col(cc, 0), _col(cc, 1)
        qa[...], qb[...] = _masked_pair(q_ref[...], is_a, SCALE)

        def tile(j, slot, masked):
            k, v = kbuf[slot], vbuf[slot]
            cols = pl.ds(pl.multiple_of(j * bk, bk), bk)
            if masked:
                tri = _causal(bq, bk, i, j)

            def heads(hs):
                qs, accs = (qa, qb), (acc_a, acc_b)
                s = {h: _dot(qs[h][...], k, _NT) - cr_ref[0, pl.ds(h, 1), cols] for h in hs}
                if masked:
                    s = {h: jnp.where(tri, s[h], NEG_BIG) for h in hs}
                top = {h: jnp.max(s[h], axis=1, keepdims=True) for h in hs}
                m_new = {h: jnp.maximum(ml[2 * h], top[h] + ml[4 + h]) for h in hs}
                a = {h: jnp.exp(ml[2 * h] - m_new[h]) for h in hs}
                pr = {h: jnp.exp(s[h] - (m_new[h] - ml[4 + h])) for h in hs}
                tot = {h: jnp.sum(pr[h], axis=1, keepdims=True) for h in hs}
                pv = {h: _dot(pr[h].astype(BF16), v) for h in hs}
                for h in hs:
                    ml[2 * h] = m_new[h]
                    ml[2 * h + 1] = a[h] * ml[2 * h + 1] + tot[h]
                    accs[h][...] = a[h] * accs[h][...] + pv[h]

            _by_heads(j, js_ref[N_PAIRS + 2 * p, i], js_ref[N_PAIRS + 2 * p + 1, i], heads)

        _walk_up(fetch, j0, per * i, per * i + per - 1, tile)
        o_ref[...] = jnp.where(is_a, acc_a[...] / ml[1], acc_b[...] / ml[3])
        lane8 = lax.broadcasted_iota(jnp.int32, (bq, 8), 1)
        st = jnp.where(lane8 == 0, ml[0] + jnp.log(ml[1]), 0.0)
        st_ref[0] = jnp.where(lane8 == 1, ml[2] + jnp.log(ml[3]), st)

    grid_spec = pltpu.PrefetchScalarGridSpec(
        num_scalar_prefetch=1, grid=(N_PAIRS, nq),
        in_specs=[pl.BlockSpec((bq, LANES), lambda p, i, js: (i, col0 + p)),
                  pl.BlockSpec(memory_space=pl.ANY),
                  pl.BlockSpec((1, bq, 8), lambda p, i, js: (p, i, 0)),
                  pl.BlockSpec((1, 8, S), lambda p, i, js: (p, 0, 0))],
        out_specs=[pl.BlockSpec((bq, LANES), lambda p, i, js: (i, p)),
                   pl.BlockSpec((1, bq, 8), lambda p, i, js: (p, i, 0))],
        scratch_shapes=[pltpu.VMEM((bq, LANES), F32), pltpu.VMEM((bq, LANES), F32),
                        pltpu.VMEM((bq, LANES), BF16), pltpu.VMEM((bq, LANES), BF16),
                        pltpu.VMEM((6, bq, 1), F32),
                        pltpu.VMEM((2 * KV_SLOTS, bk, LANES), BF16), pltpu.VMEM((2 * KV_SLOTS, bk, LANES), BF16),
                        pltpu.SemaphoreType.DMA((2, 2 * KV_SLOTS))])
    return pl.pallas_call(
        body, name="fox_fwd", grid_spec=grid_spec,
        out_shape=[jax.ShapeDtypeStruct((S, GROUP_W), F32), jax.ShapeDtypeStruct((N_PAIRS, S, 8), F32)],
    )(jstart, proj, proj, c_col, c_row)


def _fox_bwd(proj, col0, do, o, st, c_col, c_row, jstart, bq, bk):
    S = proj.shape[0]
    nq, per = S // bq, bq // bk

    def body(js_ref, q_ref, kv_hbm, do_ref, o_ref, st_ref, cc_ref, cr_ref,
             dq_ref, dk_out, dv_out, dc_ref, dq_a, dq_b, qa, qb, doa, dob, dd, kbuf, vbuf, sems, dk_ref, dv_ref):
        p, i = pl.program_id(0), pl.program_id(1)
        j0 = js_ref[p, i]
        first_two = _first_two_up(lambda pair, blk: js_ref[pair, blk], per)
        fetch, kbuf, vbuf = _kv_fetcher(kv_hbm, kbuf, vbuf, sems, KV_SLOTS, col0, bk, p, i, nq, first_two)
        is_a = lax.broadcasted_iota(jnp.int32, (bq, LANES), 1) < HEAD_DIM

        @pl.when(i == 0)
        def _():
            dk_ref[...] = jnp.zeros_like(dk_ref)
            dv_ref[...] = jnp.zeros_like(dv_ref)
            dc_ref[...] = jnp.zeros_like(dc_ref)

        dq_a[...] = jnp.zeros_like(dq_a)
        dq_b[...] = jnp.zeros_like(dq_b)
        qa[...], qb[...] = _masked_pair(q_ref[...], is_a, SCALE)
        dov = do_ref[...]
        doa[...], dob[...] = _masked_pair(dov, is_a)
        prod = dov * o_ref[...]
        dd[0] = jnp.sum(jnp.where(is_a, prod, 0.0), axis=1, keepdims=True)
        dd[1] = jnp.sum(jnp.where(is_a, 0.0, prod), axis=1, keepdims=True)
        dd[2] = jnp.zeros((bq, 1), F32)
        dd[3] = jnp.zeros((bq, 1), F32)
        cc, st_v = cc_ref[0], st_ref[0]
        dd[4], dd[5] = _col(cc, 0) - _col(st_v, 0), _col(cc, 1) - _col(st_v, 1)

        def tile(j, slot, masked):
            k, v = kbuf[slot], vbuf[slot]
            if masked:
                tri = _causal(bq, bk, i, j)
            cols = pl.ds(pl.multiple_of(j * bk, bk), bk)

            def heads(hs):
                qs, dos, dqs = (qa, qb), (doa, dob), (dq_a, dq_b)
                z = {h: _dot(qs[h][...], k, _NT) for h in hs}
                dp = {h: _dot(dos[h][...], v, _NT) for h in hs}
                pr = {h: jnp.exp(z[h] - cr_ref[0, pl.ds(h, 1), cols] + dd[4 + h]) for h in hs}
                if masked:
                    pr = {h: jnp.where(tri, pr[h], 0.0) for h in hs}
                ds = {h: pr[h] * (dp[h] - dd[h]) for h in hs}
                csum = {h: jnp.sum(ds[h], axis=0, keepdims=True) for h in hs}
                rsum = {h: jnp.sum(ds[h], axis=1, keepdims=True) for h in hs}
                dsb = {h: ds[h].astype(BF16) for h in hs}
                prb = {h: pr[h].astype(BF16) for h in hs}
                dqc = {h: _dot(dsb[h], k) for h in hs}
                dkc = [_dot(dsb[h], qs[h][...], _TN) for h in hs]
                dvc = [_dot(prb[h], dos[h][...], _TN) for h in hs]
                for h in hs:
                    dc_ref[0, pl.ds(h, 1), cols] -= csum[h]
                    dd[2 + h] += rsum[h]
                    dqs[h][...] += dqc[h]
                dk_ref[cols, :] += sum(dkc[1:], dkc[0])
                dv_ref[cols, :] += sum(dvc[1:], dvc[0])

            _by_heads(j, js_ref[N_PAIRS + 2 * p, i], js_ref[N_PAIRS + 2 * p + 1, i], heads)

        _walk_up(fetch, j0, per * i, per * i + per - 1, tile)
        dq_ref[...] = (jnp.where(is_a, dq_a[...], dq_b[...]) * SCALE).astype(BF16)
        eye = lax.broadcasted_iota(jnp.int32, (bq, bq), 0) == lax.broadcasted_iota(jnp.int32, (bq, bq), 1)
        own = pl.ds(pl.multiple_of(i * bq, bq), bq)
        for h in range(2):
            dc_ref[0, pl.ds(h, 1), own] += jnp.sum(jnp.where(eye, dd[2 + h], 0.0), axis=0, keepdims=True)

        @pl.when(i == nq - 1)
        def _():
            dk_out[...] = dk_ref[...].astype(BF16)
            dv_out[...] = dv_ref[...].astype(BF16)

    grid_spec = pltpu.PrefetchScalarGridSpec(
        num_scalar_prefetch=1, grid=(N_PAIRS, nq),
        in_specs=[pl.BlockSpec((bq, LANES), lambda p, i, js: (i, col0 + p)),
                  pl.BlockSpec(memory_space=pl.ANY),
                  pl.BlockSpec((bq, LANES), lambda p, i, js: (i, p)),
                  pl.BlockSpec((bq, LANES), lambda p, i, js: (i, p)),
                  pl.BlockSpec((1, bq, 8), lambda p, i, js: (p, i, 0)),
                  pl.BlockSpec((1, bq, 8), lambda p, i, js: (p, i, 0)),
                  pl.BlockSpec((1, 8, S), lambda p, i, js: (p, 0, 0))],
        out_specs=[pl.BlockSpec((bq, LANES), lambda p, i, js: (i, p)),
                   pl.BlockSpec((S, LANES), lambda p, i, js: (0, p)),
                   pl.BlockSpec((S, LANES), lambda p, i, js: (0, p)),
                   pl.BlockSpec((1, 8, S), lambda p, i, js: (p, 0, 0))],
        scratch_shapes=[pltpu.VMEM((bq, LANES), F32), pltpu.VMEM((bq, LANES), F32)]
        + [pltpu.VMEM((bq, LANES), BF16)] * 4 + [pltpu.VMEM((6, bq, 1), F32)]
        + [pltpu.VMEM((2 * KV_SLOTS, bk, LANES), BF16)] * 2 + [pltpu.SemaphoreType.DMA((2, 2 * KV_SLOTS))]
        + [pltpu.VMEM((S, LANES), F32)] * 2)
    return pl.pallas_call(
        body, name="fox_bwd", grid_spec=grid_spec,
        out_shape=[jax.ShapeDtypeStruct((S, GROUP_W), BF16)] * 3 + [jax.ShapeDtypeStruct((N_PAIRS, 8, S), F32)],
        compiler_params=_params(VMEM_BIG),
    )(jstart, proj, proj, do, o, st, c_col, c_row)


_HBM = pl.BlockSpec(memory_space=pltpu.HBM)


def _coords():
    return lax.axis_index("x"), lax.axis_index("y"), lax.axis_index("c")


def _gather_copies(ins, outs, send_sems, recv_sems, loc_sems):
    n = len(ins)
    x, y, c = _coords()
    mine = 2 * x + y
    chips = [(1 - x, y), (x, 1 - y), (1 - x, 1 - y)]

    def copy(w, r, slab, to):
        return pltpu.make_async_remote_copy(
            src_ref=ins[w], dst_ref=outs[w].at[slab], send_sem=send_sems.at[3 * w + r],
            recv_sem=recv_sems.at[3 * w + r], device_id=to, device_id_type=MESH)

    def own():
        local = [pltpu.make_async_copy(ins[w], outs[w].at[mine], loc_sems.at[w]) for w in range(n)]
        return local, [copy(w, r, mine, (cx, cy, c)) for w in range(n) for r, (cx, cy) in enumerate(chips)]

    def start():
        local, sends = own()
        for cp in local + sends:
            cp.start()

    def wait():
        local, sends = own()
        for w in range(n):
            for r, (cx, cy) in enumerate(chips):
                copy(w, r, 2 * cx + cy, (cx, cy, c)).wait_recv()
        for cp in sends:
            cp.wait_send()
        for cp in local:
            cp.wait()

    return start, wait


def _gather_shapes(shards):
    n = len(shards)
    return ([jax.ShapeDtypeStruct((4,) + s.shape, s.dtype) for s in shards],
            [pltpu.SemaphoreType.DMA((3 * n,)), pltpu.SemaphoreType.DMA((3 * n,)), pltpu.SemaphoreType.DMA((n,))])


def _allgather_chips(shards):
    n = len(shards)

    def body(*refs):
        start, wait = _gather_copies(refs[:n], refs[n:2 * n], *refs[2 * n:])
        start()
        wait()

    out_shape, sems = _gather_shapes(shards)
    return pl.pallas_call(body, name="allgather_weights", in_specs=[_HBM] * n, out_specs=[_HBM] * n,
                          out_shape=out_shape, scratch_shapes=sems)(*shards)


def _exchange_copies(ins, outs, send_sems, recv_sems, loc_sems, per_chip, parts):
    n = len(parts)
    half = [p.shape[1] // 2 for p in parts] if per_chip else None
    x, y, c = _coords()
    me = 4 * x + 2 * y + c
    peers = [(x ^ fx, y ^ fy, c ^ fc) for fx in (0, 1) for fy in (0, 1) for fc in (0, 1)][1:]

    def src(w, dev):
        if not per_chip:
            return ins[w]
        return ins[w].at[2 * dev[0] + dev[1], pl.ds(pl.multiple_of(dev[2] * half[w], 16), half[w]), :]

    def copy(w, r, source, slab, to):
        return pltpu.make_async_remote_copy(
            src_ref=source, dst_ref=outs[w].at[slab], send_sem=send_sems.at[7 * w + r],
            recv_sem=recv_sems.at[7 * w + r], device_id=to, device_id_type=MESH)

    def own():
        local = [pltpu.make_async_copy(src(w, (x, y, c)), outs[w].at[me], loc_sems.at[w]) for w in range(n)]
        return local, [copy(w, r, src(w, dev), me, dev) for w in range(n) for r, dev in enumerate(peers)]

    def start():
        local, sends = own()
        for cp in local + sends:
            cp.start()

    def wait():
        local, sends = own()
        for w in range(n):
            for r, dev in enumerate(peers):
                copy(w, r, src(w, dev), 4 * dev[0] + 2 * dev[1] + dev[2], dev).wait_recv()
        for cp in sends:
            cp.wait_send()
        for cp in local:
            cp.wait()

    return start, wait


def _exchange_shapes(parts, per_chip):
    n = len(parts)
    return ([jax.ShapeDtypeStruct((8, p.shape[1] // 2, p.shape[2]) if per_chip else (8,) + p.shape, p.dtype)
             for p in parts],
            [pltpu.SemaphoreType.DMA((7 * n,)), pltpu.SemaphoreType.DMA((7 * n,)), pltpu.SemaphoreType.DMA((n,))])


def _exchange(parts, per_chip):
    n = len(parts)

    def body(*refs):
        start, wait = _exchange_copies(refs[:n], refs[n:2 * n], *refs[2 * n:], per_chip, parts)
        start()
        wait()

    out_shape, sems = _exchange_shapes(parts, per_chip)
    return pl.pallas_call(body, name="exchange_per_chip" if per_chip else "exchange_all",
                          in_specs=[_HBM] * n, out_specs=[_HBM] * n, out_shape=out_shape, scratch_shapes=sems)(*parts)


def _sibling_swap(halves):
    n = len(halves)

    def body(*refs):
        ins, outs = refs[:n], refs[n:2 * n]
        send_sems, recv_sems, loc_sems = refs[2 * n:]
        x, y, c = _coords()

        def rows(w, core):
            rh = halves[w].shape[0]
            return outs[w].at[pl.ds(pl.multiple_of(core * rh, 8), rh), :]

        def copy(w, core):
            return pltpu.make_async_remote_copy(
                src_ref=ins[w], dst_ref=rows(w, core), send_sem=send_sems.at[w], recv_sem=recv_sems.at[w],
                device_id=(x, y, 1 - c), device_id_type=MESH)

        local = [pltpu.make_async_copy(ins[w], rows(w, c), loc_sems.at[w]) for w in range(n)]
        sends = [copy(w, c) for w in range(n)]
        for cp in local + sends:
            cp.start()
        for w in range(n):
            copy(w, 1 - c).wait_recv()
        for cp in sends:
            cp.wait_send()
        for cp in local:
            cp.wait()

    vmem = pl.BlockSpec(memory_space=pltpu.VMEM)
    return pl.pallas_call(
        body, name="sibling_swap", in_specs=[vmem] * n, out_specs=[vmem] * n,
        out_shape=[jax.ShapeDtypeStruct((2 * h.shape[0], h.shape[1]), h.dtype) for h in halves],
        scratch_shapes=[pltpu.SemaphoreType.DMA((n,)), pltpu.SemaphoreType.DMA((n,)), pltpu.SemaphoreType.DMA((n,))],
    )(*halves)


def _adamw(w, g, m, v):
    m = ADAM_B1 * m + (1.0 - ADAM_B1) * g
    v = ADAM_B2 * v + (1.0 - ADAM_B2) * (g * g)
    m_hat = m / (1.0 - ADAM_B1 ** ADAM_STEP)
    v_hat = v / (1.0 - ADAM_B2 ** ADAM_STEP)
    delta = -ADAM_LR * (m_hat / (jnp.sqrt(v_hat) + ADAM_EPS) + ADAM_WD * w)
    return delta, m, v


def _sum_parts(parts, name, tr):
    _, R, C = parts.shape
    assert R % tr == 0

    def body(p_ref, g_ref):
        g = p_ref[0].astype(F32)
        for d in range(1, 8):
            g = g + p_ref[d].astype(F32)
        g_ref[...] = g

    return pl.pallas_call(
        body, name=name, grid=(R // tr,),
        in_specs=[pl.BlockSpec((8, tr, C), lambda i: (0, i, 0))],
        out_specs=pl.BlockSpec((tr, C), lambda i: (i, 0)), out_shape=jax.ShapeDtypeStruct((R, C), F32),
    )(parts)


def _adamw_call(g, w, m, v, name, tr):
    R, C = w.shape
    assert R % tr == 0

    def body(g_ref, w_ref, m_ref, v_ref, d_ref, nm_ref, nv_ref):
        d_ref[...], nm_ref[...], nv_ref[...] = _adamw(w_ref[...], g_ref[...], m_ref[...], v_ref[...])

    tile = pl.BlockSpec((tr, C), lambda i: (i, 0))
    return pl.pallas_call(
        body, name=name, grid=(R // tr,), in_specs=[tile] * 4,
        out_specs=[tile] * 3, out_shape=[jax.ShapeDtypeStruct((R, C), F32)] * 3,
    )(g, w, m, v)


def _sum_adamw_small(parts, w, m, v):
    def body(p_ref, w_ref, m_ref, v_ref, g_ref, d_ref, nm_ref, nv_ref, loss_ref):
        g = p_ref[0]
        for d in range(1, 8):
            g = g + p_ref[d]
        g_ref[...] = g
        d_ref[...], nm_ref[...], nv_ref[...] = _adamw(w_ref[...], g, m_ref[...], v_ref[...])
        row = lax.broadcasted_iota(jnp.int32, g.shape, 0)
        per_row = jnp.sum(jnp.where(row == 6, g, 0.0), axis=1, keepdims=True)
        loss_ref[...] = jnp.zeros((8, LANES), F32) + jnp.sum(per_row, axis=0, keepdims=True)

    return pl.pallas_call(
        body, name="sum_adamw_small",
        out_shape=[jax.ShapeDtypeStruct((8, D_MODEL), F32)] * 4 + [jax.ShapeDtypeStruct((8, LANES), F32)],
    )(parts, w, m, v)


def _pack_small(ln1_g, ln1_b, ln2_g, ln2_b, g_sb, g_fox, b_f):
    row5 = jnp.pad(b_f.reshape(1, N_FOX), ((0, 0), (0, D_MODEL - N_FOX)))
    rows = [ln1_g.reshape(1, -1), ln1_b.reshape(1, -1), ln2_g.reshape(1, -1), ln2_b.reshape(1, -1),
            jnp.concatenate([g_sb.reshape(1, -1), g_fox.reshape(1, -1)], axis=1), row5,
            jnp.zeros((2, D_MODEL), F32)]
    return jnp.concatenate(rows, axis=0)


def _unpack_small(p):
    return {"ln1_g": p[0:1], "ln1_b": p[1:2], "ln2_g": p[2:3], "ln2_b": p[3:4], "g_sb": p[4:5, :GROUP_W],
            "g_fox": p[4:5, GROUP_W:], "b_f": p[5:6, :N_FOX]}


def kernel(x, w_in, b_f, g_sb, g_fox, w_out, ln1_g, ln1_b, ln2_g, ln2_b, w_gate_up, w_down, loss_target, m_w_in, m_b_f, m_g_sb, m_g_fox, m_w_out, m_ln1_g, m_ln1_b, m_ln2_g, m_ln2_b, m_w_gate_up, m_w_down, v_w_in, v_b_f, v_g_sb, v_g_fox, v_w_out, v_ln1_g, v_ln1_b, v_ln2_g, v_ln2_b, v_w_gate_up, v_w_down):
    S = x.shape[1]
    x2 = x.reshape(S, D_MODEL)
    tgt = loss_target.reshape(S, D_MODEL)
    TM = 1024
    TR = 512
    BQ = ATTN_BLOCK
    in_w = w_in.shape[2]
    gu_w = w_gate_up.shape[2]

    shards = [w_in[0].astype(BF16), w_out[0].astype(BF16), w_gate_up[0].astype(BF16), w_down[0].astype(BF16)]
    (wi_s,) = _allgather_chips(shards[:1])
    wi = wi_s.transpose(1, 0, 2).reshape(D_MODEL, 4 * in_w)
    w_sb, w_fx = wi[:, :QKV_W // 2], wi[:, QKV_W // 2:QKV_W]
    wqkv = wi[:, :QKV_W]
    wft = wi[:, QKV_W:].T
    proj = _matmul(x2, wqkv, mode="nn", name="proj", tm=TM, tn=512, tk=D_MODEL, outs=[BF16])
    g_row = jnp.concatenate([g_sb, g_fox], axis=1)
    hid = np.arange(D_MODEL) // HEAD_DIM
    he_np = (hid[:, None] == np.arange(LANES)[None, :]).astype(np.float32)
    he, het = jnp.asarray(he_np, BF16), jnp.asarray(he_np.T, BF16)

    lf = _fgate_fwd(x2, wft, b_f.reshape(N_FOX, 1), TM)
    c = _cumsum_fwd(lf)
    c_pair = c.reshape(N_PAIRS, 2, S)
    c_row = jnp.pad(c_pair, ((0, 0), (0, 6), (0, 0)))
    c_col = jnp.pad(c_pair.transpose(0, 2, 1), ((0, 0), (0, 0), (0, 6)))

    o_sb, st_sb, jmin_sb, wo_s, wgu_s, wd_s = _sb_fwd(proj, 0, BQ, shards[1:])
    wo = wo_s.reshape(D_MODEL, D_MODEL)
    wgu = wgu_s.transpose(1, 0, 2).reshape(D_MODEL, 2 * D_FF)
    wg, wu = wgu[:, :D_FF], wgu[:, D_FF:]
    wd = wd_s.reshape(D_FF, D_MODEL)
    jstart_fx = _fox_start_blocks(*_fox_row_norms(proj, 12, TR), c, BQ, BQ)
    o_fx, st_fx = _fox_fwd(proj, 12, c_col, c_row, jstart_fx, BQ, BQ)

    def attn_post(i, osb_ref, ofx_ref, g_ref, he_ref, het_ref, on_ref):
        o = jnp.concatenate([osb_ref[...], ofx_ref[...]], axis=1)
        ms = _head_sums(o * o, he_ref[...], het_ref[...]) * (1.0 / HEAD_DIM)
        on_ref[...] = (o * lax.rsqrt(ms + RMS_EPS) * g_ref[...]).astype(BF16)

    (on,) = _rowwise(attn_post, "attn_post", S, TR,
                     [(o_sb, "t"), (o_fx, "t"), (g_row, "f"), (he, "f"), (het, "f")],
                     [((S, D_MODEL), BF16, "t")])

    u1 = _matmul(on, wo, mode="nn", name="mix", tm=TM, tn=D_MODEL, tk=D_MODEL, outs=[F32],
                 extras=[(x2, (TM if S >= TM else S, D_MODEL), _tile_ij)],
                 epilogue=lambda acc, xv: (ALPHA * xv + acc,))

    def ln1_fwd(i, u_ref, g_ref, b_ref, h_ref):
        xh, _ = _ln_stats(u_ref[...])
        h_ref[...] = xh * g_ref[...] + b_ref[...]

    (h1,) = _rowwise(ln1_fwd, "ln1_fwd", S, TR, [(u1, "t"), (ln1_g, "f"), (ln1_b, "f")], [((S, D_MODEL), F32, "t")])

    tm_e = TM if S >= TM else S
    n_ff = D_FF // 256

    def gate_up_body(h_ref, wg_ref, wu_ref, g_ref, u_ref, a_ref):
        h = h_ref[...].astype(BF16)
        g, u = _dot(h, wg_ref[...]), _dot(h, wu_ref[...])
        g_ref[...] = g.astype(BF16)
        u_ref[...] = u.astype(BF16)
        a_ref[...] = (g * _sigmoid(g) * u).astype(BF16)

    tm_g = min(2 * TM, S)
    ff_tile = pl.BlockSpec((tm_g, 256), lambda i, j: (i, j))
    gate, up, act = pl.pallas_call(
        gate_up_body, name="gate_up_act", grid=(S // tm_g, n_ff),
        in_specs=[pl.BlockSpec((tm_g, D_MODEL), lambda i, j: (i, 0)),
                  pl.BlockSpec((D_MODEL, 256), lambda i, j: (0, j)),
                  pl.BlockSpec((D_MODEL, 256), lambda i, j: (0, j + n_ff))],
        out_specs=[ff_tile] * 3, out_shape=[jax.ShapeDtypeStruct((S, D_FF), BF16)] * 3)(h1, wgu, wgu)

    u2 = _matmul(act, wd, mode="nn", name="ffn_down", tm=TM, tn=D_MODEL, tk=D_FF, outs=[F32],
                 extras=[(h1, (TM if S >= TM else S, D_MODEL), _tile_ij)],
                 epilogue=lambda acc, hv: (ALPHA * hv + acc,))

    def ln2_loss(i, u_ref, t_ref, g_ref, b_ref, du_ref, acc_ref):
        xh, r = _ln_stats(u_ref[...])
        g = g_ref[...]
        err = xh * g + b_ref[...] - t_ref[...]
        dy = err * (1.0 / D_MODEL)
        du_ref[...] = _ln_bwd(dy, xh, r, g)
        _acc_rows(i, acc_ref, {2: jnp.sum(dy * xh, axis=0, keepdims=True), 3: jnp.sum(dy, axis=0, keepdims=True),
                               6: jnp.sum(err * err, axis=0, keepdims=True) * (0.5 / D_MODEL)})

    du2, acc_ln2 = _rowwise(ln2_loss, "ln2_loss", S, TR, [(u2, "t"), (tgt, "t"), (ln2_g, "f"), (ln2_b, "f")],
                            [((S, D_MODEL), F32, "t"), ((8, D_MODEL), F32, "f")])

    d_wd = _matmul(act, du2, mode="tn", name="dw_down", tm=1408, tn=D_MODEL, tk=TM, outs=[BF16])

    def dgu_epilogue(da, g, u):
        g, u = g.astype(F32), u.astype(F32)
        s = _sigmoid(g)
        return da * u * (s * (1.0 + g * (1.0 - s))), da * (g * s)

    dgate, dup = _matmul(du2, wd, mode="nt", name="d_act", tm=TM, tn=1408, tk=D_MODEL, outs=[BF16, BF16],
                         extras=[(gate, (tm_e, 1408), _tile_ij), (up, (tm_e, 1408), _tile_ij)],
                         epilogue=dgu_epilogue)
    d_wg = _matmul(h1, dgate, mode="tn", name="dw_gate", tm=D_MODEL, tn=1408, tk=TM, outs=[BF16])
    d_wu = _matmul(h1, dup, mode="tn", name="dw_up", tm=D_MODEL, tn=1408, tk=TM, outs=[BF16])
    d_wgu = jnp.concatenate([d_wg, d_wu], axis=1)
    dh1, got_down = _matmul(dgate, wg, mode="nt", name="dh1_gate", tm=TM, tn=D_MODEL, tk=D_FF, outs=[F32],
                            extras=[(du2, (tm_e, D_MODEL), _tile_ij)], epilogue=lambda acc, e: (ALPHA * e + acc,),
                            hosted=[d_wd.reshape(4, D_FF // 4, D_MODEL)])
    dh1, got_gu = _matmul(dup, wu, mode="nt", name="dh1_up", tm=TM, tn=D_MODEL, tk=D_FF, outs=[F32],
                          extras=[(dh1, (tm_e, D_MODEL), _tile_ij)], epilogue=lambda acc, e: (e + acc,),
                          hosted=[d_wgu.reshape(D_MODEL, 4, gu_w).transpose(1, 0, 2)])

    def ln1_bwd(i, dh_ref, u_ref, g_ref, du_ref, acc_ref):
        xh, r = _ln_stats(u_ref[...])
        dh = dh_ref[...]
        du_ref[...] = _ln_bwd(dh, xh, r, g_ref[...])
        _acc_rows(i, acc_ref, {0: jnp.sum(dh * xh, axis=0, keepdims=True), 1: jnp.sum(dh, axis=0, keepdims=True)})

    du1, acc_ln1 = _rowwise(ln1_bwd, "ln1_bwd", S, TR, [(dh1, "t"), (u1, "t"), (ln1_g, "f")],
                            [((S, D_MODEL), F32, "t"), ((8, D_MODEL), F32, "f")])
    d_wo = _matmul(on, du1, mode="tn", name="dw_out", tm=D_MODEL, tn=D_MODEL, tk=TM, outs=[BF16])
    don, got_out = _matmul(du1, wo, mode="nt", name="d_on", tm=TM, tn=D_MODEL, tk=D_MODEL, outs=[F32],
                           hosted=[d_wo.reshape(4, D_MODEL // 4, D_MODEL)])

    def rms_bwd(i, don_ref, osb_ref, ofx_ref, g_ref, he_ref, het_ref, dosb_ref, dofx_ref, acc_ref):
        o = jnp.concatenate([osb_ref[...], ofx_ref[...]], axis=1)
        hev, hetv = he_ref[...], het_ref[...]
        r = lax.rsqrt(_head_sums(o * o, hev, hetv) * (1.0 / HEAD_DIM) + RMS_EPS)
        dn = don_ref[...]
        dg = dn * g_ref[...]
        do = r * dg - o * (r * r * r) * (_head_sums(dg * o, hev, hetv) * (1.0 / HEAD_DIM))
        dosb_ref[...] = do[:, :GROUP_W]
        dofx_ref[...] = do[:, GROUP_W:]
        _acc_rows(i, acc_ref, {4: jnp.sum(dn * o * r, axis=0, keepdims=True)})

    do_sb, do_fx, acc_rms = _rowwise(
        rms_bwd, "rms_bwd", S, TR, [(don, "t"), (o_sb, "t"), (o_fx, "t"), (g_row, "f"), (he, "f"), (het, "f")],
        [((S, GROUP_W), F32, "t"), ((S, GROUP_W), F32, "t"), ((8, D_MODEL), F32, "f")])

    dq_sb, dk_sb, dv_sb = _sb_bwd(proj, 0, do_sb, st_sb, jmin_sb, BQ)
    jstart_fx2 = jnp.minimum(jstart_fx[:, 0::2], jstart_fx[:, 1::2])
    dq_fx, dk_fx, dv_fx, dc = _fox_bwd(proj, 12, do_fx, o_fx, st_fx, c_col, c_row, jstart_fx2, 2 * BQ, BQ)
    dfl, dbf = _fgate_bwd(dc[:, :2, :].reshape(N_FOX, S), lf)
    dp_sb = jnp.concatenate([dq_sb, dk_sb, dv_sb], axis=1)
    dp_fx = jnp.concatenate([dq_fx, dk_fx, dv_fx], axis=1)

    d_wsb = _matmul(x2, dp_sb, mode="tn", name="dw_in_sb", tm=D_MODEL, tn=QKV_W // 2, tk=TM, outs=[BF16])
    d_wfx = _matmul(x2, dp_fx, mode="tn", name="dw_in_fx", tm=D_MODEL, tn=QKV_W // 2, tk=TM, outs=[BF16])
    d_wft = _matmul(dfl, x2, mode="nn", name="dw_in_f", tm=N_FOX, tn=D_MODEL, tk=TM, outs=[BF16])
    d_wi = jnp.concatenate([d_wsb, d_wfx, d_wft.T], axis=1)
    dx, got_in = _matmul(dp_sb, w_sb, mode="nt", name="dx_sb", tm=TM, tn=D_MODEL, tk=QKV_W // 2, outs=[F32],
                         extras=[(du1, (tm_e, D_MODEL), _tile_ij)], epilogue=lambda acc, e: (ALPHA * e + acc,),
                         hosted=[d_wi.reshape(D_MODEL, 4, in_w).transpose(1, 0, 2)])
    dx = _matmul(dp_fx, w_fx, mode="nt", name="dx_fx", tm=TM, tn=D_MODEL, tk=QKV_W // 2, outs=[F32],
                 extras=[(dx, (tm_e, D_MODEL), _tile_ij)], epilogue=lambda acc, e: (e + acc,))
    dx = _matmul(dfl, wft, mode="tn", name="dx_f", tm=TM, tn=D_MODEL, tk=N_FOX, outs=[F32],
                 extras=[(dx, (tm_e, D_MODEL), _tile_ij)], epilogue=lambda acc, e: (e + acc,))

    got = [got_in, got_out, got_gu, got_down]
    big_names = ("w_in", "w_out", "w_gate_up", "w_down")
    halves = [_sum_parts(p, "sum_" + nm, tr) for nm, p, tr in zip(big_names, got, (256, 128, 128, 176))]
    grads = _sibling_swap(halves)
    big = {}
    for nm, g, w, m, v, tr in zip(big_names, grads, (w_in, w_out, w_gate_up, w_down),
                                  (m_w_in, m_w_out, m_w_gate_up, m_w_down),
                                  (v_w_in, v_w_out, v_w_gate_up, v_w_down), (256, 256, 256, 176)):
        big[nm] = [r[None] for r in [g] + list(_adamw_call(g, w[0], m[0], v[0], "adamw_" + nm, tr))]

    small = acc_ln2 + acc_ln1 + acc_rms
    small = small + jnp.pad(dbf.reshape(1, N_FOX), ((5, 2), (0, D_MODEL - N_FOX)))
    (small_all,) = _exchange([small], False)
    sw = _pack_small(ln1_g, ln1_b, ln2_g, ln2_b, g_sb, g_fox, b_f)
    sm = _pack_small(m_ln1_g, m_ln1_b, m_ln2_g, m_ln2_b, m_g_sb, m_g_fox, m_b_f)
    sv = _pack_small(v_ln1_g, v_ln1_b, v_ln2_g, v_ln2_b, v_g_sb, v_g_fox, v_b_f)
    sg, sd, snm, snv, loss_blk = _sum_adamw_small(small_all, sw, sm, sv)
    sg, sd, snm, snv = _unpack_small(sg), _unpack_small(sd), _unpack_small(snm), _unpack_small(snv)

    names = ["w_in", "b_f", "g_sb", "g_fox", "w_out", "ln1_g", "ln1_b", "ln2_g", "ln2_b", "w_gate_up", "w_down"]
    outs = [loss_blk[0, 0], dx.reshape(1, S, D_MODEL)]
    for k, table in enumerate((sg, sd, snm, snv)):
        outs += [big[n][k] if n in big else table[n] for n in names]
    return tuple(outs)
```

```python
import functools

import numpy as np
import jax
import jax.numpy as jnp
from jax import lax
from jax.experimental import pallas as pl
from jax.experimental.pallas import tpu as pltpu

F32 = jnp.float32
BF16 = jnp.bfloat16

D_MODEL = 1024
HEAD_DIM = 64
LANES = 128
N_PAIRS = 4
GROUP_W = 512
QKV_W = 3072
D_FF = 2816
N_FOX = 8
ALPHA = 2.0 ** 0.25
LN_EPS = 1e-5
RMS_EPS = 1e-6
SCALE = HEAD_DIM ** -0.5
NEG_BIG = -1e30
FOX_SKIP = 30.0
SB_STOP = -105.0
ADAM_LR, ADAM_B1, ADAM_B2, ADAM_EPS, ADAM_WD, ADAM_STEP = 0.001, 0.9, 0.999, 1e-08, 0.01, 10
KV_SLOTS = 6
KV_EARLY = 4
SCAN_GROUP = 8
ATTN_BLOCK = 256
VMEM_BIG = 56 * 1024 * 1024
MESH = pl.DeviceIdType.MESH

_NN = (((1,), (0,)), ((), ()))
_NT = (((1,), (1,)), ((), ()))
_TN = (((0,), (0,)), ((), ()))


def _dot(a, b, dims=_NN):
    return lax.dot_general(a, b, dims, preferred_element_type=F32)


def _split_dot(x, t):
    hi = x.astype(BF16)
    lo = (x - hi.astype(F32)).astype(BF16)
    return _dot(hi, t) + _dot(lo, t)


def _softplus(z):
    return jnp.maximum(z, 0.0) + jnp.log1p(jnp.exp(-jnp.abs(z)))


def _sigmoid(x):
    return 0.5 * jnp.tanh(0.5 * x) + 0.5


def _col(v, h):
    lane = lax.broadcasted_iota(jnp.int32, v.shape, 1)
    return jnp.sum(jnp.where(lane == h, v, 0.0), axis=1, keepdims=True)


def _two_sum(hi, lo, b):
    s = hi + b
    bb = s - hi
    err = (hi - (s - bb)) + (b - bb)
    return s, lo + err


def _params(vmem=None):
    return pltpu.CompilerParams(vmem_limit_bytes=vmem) if vmem else None


def _matmul(a, b, *, mode, name, tm, tn, tk, outs, extras=(), epilogue=None, vmem=None, hosted=()):
    if mode == "nn":
        (M, K), (_, N) = a.shape, b.shape
    elif mode == "nt":
        (M, K), (N, _) = a.shape, b.shape
    else:
        (K, M), (_, N) = a.shape, b.shape
    tm, tn, tk = min(tm, M), min(tn, N), min(tk, K)
    assert M % tm == 0 and N % tn == 0 and K % tk == 0, (name, M, N, K, tm, tn, tk)
    nk = K // tk
    dims = {"nn": _NN, "nt": _NT, "tn": _TN}[mode]
    if mode == "tn":
        a_spec = pl.BlockSpec((tk, tm), lambda i, j, k: (k, i))
    else:
        a_spec = pl.BlockSpec((tm, tk), lambda i, j, k: (i, k))
    if mode == "nt":
        b_spec = pl.BlockSpec((tn, tk), lambda i, j, k: (j, k))
    else:
        b_spec = pl.BlockSpec((tk, tn), lambda i, j, k: (k, j))
    ex_specs = [pl.BlockSpec(bs, (lambda i, j, k, f=f: f(i, j))) for (_, bs, f) in extras]
    ne, no, nh = len(extras), len(outs), len(hosted)
    if epilogue is None:
        epilogue = lambda acc: (acc,)
    gi, gj = M // tm, N // tn
    host_shapes, host_sems = _exchange_shapes(hosted, True) if nh else ([], [])

    def body(a_ref, b_ref, *rest):
        ex_refs, host_ins = rest[:ne], rest[ne:ne + nh]
        out_refs, host_outs = rest[ne + nh:ne + nh + no], rest[ne + nh + no:ne + 2 * nh + no]
        acc = rest[ne + 2 * nh + no]
        i, j, k = pl.program_id(0), pl.program_id(1), pl.program_id(2)
        if nh:
            start, wait = _exchange_copies(host_ins, host_outs, *rest[ne + 2 * nh + no + 1:], True, hosted)
            pl.when(jnp.logical_and(jnp.logical_and(i == 0, j == 0), k == 0))(start)

        @pl.when(k == 0)
        def _():
            acc[...] = jnp.zeros_like(acc)

        acc[...] += _dot(a_ref[...].astype(BF16), b_ref[...].astype(BF16), dims)

        @pl.when(k == nk - 1)
        def _():
            res = epilogue(acc[...], *[e[...] for e in ex_refs])
            for r, o in zip(res, out_refs):
                o[...] = r.astype(o.dtype)

        if nh:
            pl.when(jnp.logical_and(jnp.logical_and(i == gi - 1, j == gj - 1), k == nk - 1))(wait)

    res = pl.pallas_call(
        body, name=name, grid=(gi, gj, nk),
        in_specs=[a_spec, b_spec] + ex_specs + [_HBM] * nh,
        out_specs=[pl.BlockSpec((tm, tn), lambda i, j, k: (i, j)) for _ in outs] + [_HBM] * nh,
        out_shape=[jax.ShapeDtypeStruct((M, N), d) for d in outs] + host_shapes,
        scratch_shapes=[pltpu.VMEM((tm, tn), F32)] + host_sems,
        compiler_params=_params(vmem),
    )(a, b, *[e[0] for e in extras], *hosted)
    return res[0] if no + nh == 1 else res


def _tile_ij(i, j):
    return (i, j)


def _rowwise(fn, name, rows, tm, ins, outs, vmem=None):
    tm = min(tm, rows)
    assert rows % tm == 0

    def spec(shape, kind):
        if kind == "t":
            return pl.BlockSpec((tm,) + tuple(shape[1:]), lambda i: (i,) + (0,) * (len(shape) - 1))
        return pl.BlockSpec(tuple(shape), lambda i: (0,) * len(shape))

    def body(*refs):
        fn(pl.program_id(0), *refs)

    return pl.pallas_call(
        body, name=name, grid=(rows // tm,),
        in_specs=[spec(a.shape, k) for a, k in ins],
        out_specs=[spec(s, k) for s, _, k in outs],
        out_shape=[jax.ShapeDtypeStruct(s, d) for s, d, _ in outs],
        compiler_params=_params(vmem),
    )(*[a for a, _ in ins])


def _ln_stats(u):
    mu = jnp.mean(u, axis=-1, keepdims=True)
    d = u - mu
    var = jnp.mean(d * d, axis=-1, keepdims=True)
    r = lax.rsqrt(var + LN_EPS)
    return d * r, r


def _ln_bwd(dh, xh, r, g):
    dxh = dh * g
    m1 = jnp.mean(dxh, axis=-1, keepdims=True)
    m2 = jnp.mean(dxh * xh, axis=-1, keepdims=True)
    return r * (dxh - m1 - xh * m2)


def _acc_rows(i, ref, rows):
    @pl.when(i == 0)
    def _():
        ref[...] = jnp.zeros_like(ref)
    for r, v in rows.items():
        ref[pl.ds(r, 1), :] += v


def _head_sums(v, he, het):
    return _split_dot(_split_dot(v, he), het)


def _fgate_fwd(x, wft, bf_col, tm):
    S = x.shape[0]
    tm = min(tm, S)

    def body(wft_ref, bf_ref, x_ref, lf_ref):
        f = _dot(wft_ref[...], x_ref[...].astype(BF16), _NT) + bf_ref[...]
        lf_ref[...] = -_softplus(-f)

    return pl.pallas_call(
        body, name="fgate_fwd", grid=(S // tm,),
        in_specs=[pl.BlockSpec((N_FOX, D_MODEL), lambda i: (0, 0)), pl.BlockSpec((N_FOX, 1), lambda i: (0, 0)),
                  pl.BlockSpec((tm, D_MODEL), lambda i: (i, 0))],
        out_specs=pl.BlockSpec((N_FOX, tm), lambda i: (0, i)),
        out_shape=jax.ShapeDtypeStruct((N_FOX, S), F32),
    )(wft, bf_col, x)


def _chunk_scan(v, reverse):
    lane = lax.broadcasted_iota(jnp.int32, v.shape, 1)
    sh = 1
    while sh < LANES:
        if reverse:
            v = v + jnp.where(lane < LANES - sh, pltpu.roll(v, LANES - sh, 1), 0.0)
        else:
            v = v + jnp.where(lane >= sh, pltpu.roll(v, sh, 1), 0.0)
        sh *= 2
    return v


def _cumsum_fwd(lf):
    n, S = lf.shape
    nc = S // LANES

    grp = min(SCAN_GROUP, nc)

    def body(lf_ref, c_ref):
        def step(gi, carry):
            sls = [pl.ds(pl.multiple_of((gi * grp + g) * LANES, LANES), LANES) for g in range(grp)]
            vs = [_chunk_scan(lf_ref[:, sl], False) for sl in sls]
            tots = [_col(v, LANES - 1) for v in vs]
            for sl, v, t in zip(sls, vs, tots):
                c_ref[:, sl] = v + carry
                carry = carry + t
            return carry
        lax.fori_loop(0, nc // grp, step, jnp.zeros((n, 1), F32))

    return pl.pallas_call(body, name="cumsum_fwd", out_shape=jax.ShapeDtypeStruct((n, S), F32))(lf)


def _fgate_bwd(dc, lf):
    n, S = dc.shape
    nc = S // LANES

    grp = min(SCAN_GROUP, nc)

    def body(dc_ref, lf_ref, dfl_ref, dbf_ref):
        def step(t, carry):
            car, tot = carry
            gi = nc // grp - 1 - t
            sls = [pl.ds(pl.multiple_of((gi * grp + g) * LANES, LANES), LANES) for g in range(grp)]
            vs = [_chunk_scan(dc_ref[:, sl], True) for sl in sls]
            firsts = [_col(v, 0) for v in vs]
            for sl, v, f in reversed(list(zip(sls, vs, firsts))):
                dfl = (v + car) * (1.0 - jnp.exp(lf_ref[:, sl]))
                dfl_ref[:, sl] = dfl
                tot = tot + jnp.sum(dfl, axis=1, keepdims=True)
                car = car + f
            return car, tot
        _, tot = lax.fori_loop(0, nc // grp, step, (jnp.zeros((n, 1), F32), jnp.zeros((n, 1), F32)))
        dbf_ref[...] = tot

    return pl.pallas_call(body, name="fgate_bwd",
                          out_shape=[jax.ShapeDtypeStruct((n, S), F32), jax.ShapeDtypeStruct((n, 1), F32)])(dc, lf)


def _tri_matrices(b):
    r = np.arange(b)
    tfwd = (r[:, None] <= r[None, :]).astype(np.float32)
    return jnp.asarray(tfwd, BF16), jnp.asarray(tfwd.T, BF16)


def _kv_copies(kv_hbm, kbuf, vbuf, sems, sem0, pair_col, bq, j, slot):
    rows = pl.ds(pl.multiple_of(j * bq, bq), bq)

    def cols(c):
        return pl.ds(pl.multiple_of((pair_col + c) * LANES, LANES), LANES)

    return (pltpu.make_async_copy(kv_hbm.at[rows, cols(4)], kbuf.at[slot], sems.at[0, sem0 + slot]),
            pltpu.make_async_copy(kv_hbm.at[rows, cols(8)], vbuf.at[slot], sems.at[1, sem0 + slot]))


def _first_two_up(first_block, per=1):
    def blocks(pair, blk):
        first = first_block(pair, blk)
        return [(first + d, first + d <= per * blk + per - 1) for d in range(KV_EARLY)]
    return blocks


def _first_two_down(pair, blk):
    return [(blk, blk >= 0), (blk - 1, blk > 0)]


def _start_first(fetch, pair, blocks, ahead):
    for d, (j, exists) in enumerate(blocks):
        @pl.when(exists)
        def _(d=d, j=j):
            for cp in fetch(j, d, pair, ahead):
                cp.start()


def _kv_fetcher(kv_hbm, kbuf, vbuf, sems, ns, col0, bq, p, i, nq, blocks):
    base = lax.rem(p * nq + i, 2) * ns
    own = (kbuf.at[pl.ds(base, ns)], vbuf.at[pl.ds(base, ns)])
    other = (kbuf.at[pl.ds(ns - base, ns)], vbuf.at[pl.ds(ns - base, ns)])

    def fetch(j, slot, pair=p, ahead=False):
        kb, vb = other if ahead else own
        return _kv_copies(kv_hbm, kb, vb, sems, ns - base if ahead else base, col0 + pair, bq, j, slot)

    pl.when(jnp.logical_and(p == 0, i == 0))(lambda: _start_first(fetch, p, blocks(p, i), False))
    wrap = i == nq - 1

    @pl.when(jnp.logical_not(jnp.logical_and(wrap, p == N_PAIRS - 1)))
    def _():
        pair, blk = jnp.where(wrap, p + 1, p), jnp.where(wrap, 0, i + 1)
        _start_first(fetch, pair, blocks(pair, blk), True)

    return fetch, own[0], own[1]


def _masked_pair(v, lane_is_a, scale=1.0):
    v = v.astype(F32) * scale
    return jnp.where(lane_is_a, v, 0.0).astype(BF16), jnp.where(lane_is_a, 0.0, v).astype(BF16)


def _sb_fwd(proj, col0, bq, shards=()):
    S = proj.shape[0]
    bq = min(bq, S)
    nq = S // bq
    _, trev = _tri_matrices(bq)
    nh = len(shards)
    gather_shapes, gather_sems = _gather_shapes(shards) if nh else ([], [])

    def body(q_ref, kv_hbm, trev_ref, *rest):
        o_ref, st_ref, jmin_ref = rest[nh:nh + 3]
        acc_a, acc_b, qa, qb, rs, kbuf, vbuf, sems = rest[2 * nh + 3:2 * nh + 11]
        p, i = pl.program_id(0), pl.program_id(1)
        if nh:
            gather_start, gather_wait = _gather_copies(rest[:nh], rest[nh + 3:2 * nh + 3], *rest[2 * nh + 11:])
            pl.when(jnp.logical_and(p == 0, i == 0))(gather_start)
        fetch, kbuf, vbuf = _kv_fetcher(kv_hbm, kbuf, vbuf, sems, 2, col0, bq, p, i, nq, _first_two_down)
        is_a = lax.broadcasted_iota(jnp.int32, (bq, LANES), 1) < HEAD_DIM
        acc_a[...] = jnp.zeros_like(acc_a)
        acc_b[...] = jnp.zeros_like(acc_b)
        rs[...] = jnp.zeros_like(rs)
        qa[...], qb[...] = _masked_pair(q_ref[...], is_a, SCALE)

        def tiles(blocks):
            hs, qs, accs, trev_m = (0, 1), (qa, qb), (acc_a, acc_b), trev_ref[...]
            kv = [(kbuf[s], vbuf[s]) for s, _ in blocks]
            bh = [(b, h) for b in range(len(blocks)) for h in hs]
            tri = lax.broadcasted_iota(jnp.int32, (bq, bq), 0) > lax.broadcasted_iota(jnp.int32, (bq, bq), 1)
            z = {(b, h): _dot(qs[h][...], kv[b][0], _NT) for b, h in bh}
            lk = {(b, h): -_softplus(z[b, h]) for b, h in bh}
            lk = {(b, h): jnp.where(tri, lk[b, h], 0.0) if blocks[b][1] else lk[b, h] for b, h in bh}
            suf = {(b, h): _split_dot(lk[b, h], trev_m) for b, h in bh}
            tot = {(b, h): jnp.sum(lk[b, h], axis=1, keepdims=True) for b, h in bh}
            right = {}
            for h in hs:
                r = rs[2 * h] + rs[2 * h + 1]
                for b in range(len(blocks)):
                    right[b, h] = r
                    r = r + tot[b, h]
            w = {(b, h): jnp.exp(z[b, h] + suf[b, h] + right[b, h]) for b, h in bh}
            w = {(b, h): jnp.where(tri, w[b, h], 0.0) if blocks[b][1] else w[b, h] for b, h in bh}
            pv = {(b, h): _dot(w[b, h].astype(BF16), kv[b][1]) for b, h in bh}
            for h in hs:
                accs[h][...] += sum([pv[b, h] for b in range(1, len(blocks))], pv[0, h])
                hi, lo = rs[2 * h], rs[2 * h + 1]
                for b in range(len(blocks)):
                    hi, lo = _two_sum(hi, lo, tot[b, h])
                rs[2 * h], rs[2 * h + 1] = hi, lo

        def live():
            return (jnp.max(jnp.maximum(rs[0], rs[2])) > SB_STOP).astype(jnp.int32)

        for cp in fetch(i, 0):
            cp.wait()
        pl.when(i == 0)(functools.partial(tiles, [(0, True)]))

        @pl.when(i > 0)
        def _():
            for cp in fetch(i - 1, 1):
                cp.wait()
            tiles([(0, True), (1, False)])

        def step(carry):
            j, _ = carry
            slot = lax.rem(i - j, 2)
            for cp in fetch(j, slot):
                cp.start()
            for cp in fetch(j, slot):
                cp.wait()
            tiles([(slot, False)])
            return j - 1, live()

        j_end, _ = lax.while_loop(lambda c: jnp.logical_and(c[0] >= 0, c[1] > 0), step, (i - 2, live()))
        jmin_ref[p, i] = jnp.maximum(j_end + 1, 0)
        o_ref[...] = jnp.where(is_a, acc_a[...], acc_b[...])
        lane8 = lax.broadcasted_iota(jnp.int32, (bq, 8), 1)
        st = jnp.zeros((bq, 8), F32)
        for c, src in enumerate((0, 2, 1, 3)):
            st = jnp.where(lane8 == c, rs[src], st)
        st_ref[0] = st
        if nh:
            pl.when(jnp.logical_and(p == N_PAIRS - 1, i == nq - 1))(gather_wait)

    return pl.pallas_call(
        body, name="sb_fwd", grid=(N_PAIRS, nq),
        in_specs=[pl.BlockSpec((bq, LANES), lambda p, i: (i, col0 + p)),
                  pl.BlockSpec(memory_space=pl.ANY),
                  pl.BlockSpec((bq, bq), lambda p, i: (0, 0))] + [_HBM] * nh,
        out_specs=[pl.BlockSpec((bq, LANES), lambda p, i: (i, p)),
                   pl.BlockSpec((1, bq, 8), lambda p, i: (p, i, 0)),
                   pl.BlockSpec(memory_space=pltpu.SMEM)] + [_HBM] * nh,
        out_shape=[jax.ShapeDtypeStruct((S, GROUP_W), F32), jax.ShapeDtypeStruct((N_PAIRS, S, 8), F32),
                   jax.ShapeDtypeStruct((N_PAIRS, nq), jnp.int32)] + gather_shapes,
        scratch_shapes=[pltpu.VMEM((bq, LANES), F32), pltpu.VMEM((bq, LANES), F32),
                        pltpu.VMEM((bq, LANES), BF16), pltpu.VMEM((bq, LANES), BF16),
                        pltpu.VMEM((4, bq, 1), F32),
                        pltpu.VMEM((4, bq, LANES), BF16), pltpu.VMEM((4, bq, LANES), BF16),
                        pltpu.SemaphoreType.DMA((2, 4))] + gather_sems,
    )(proj, proj, trev, *shards)


def _sb_bwd(proj, col0, do, st, jmin, bq):
    S = proj.shape[0]
    bq = min(bq, S)
    nq = S // bq
    tfwd, trev = _tri_matrices(bq)

    def body(jmin_ref, q_ref, kv_hbm, do_ref, st_ref, tfwd_ref, trev_ref,
             dq_ref, dk_out, dv_out, dq_a, dq_b, qa, qb, doa, dob, rs, kbuf, vbuf, sems, dk_ref, dv_ref):
        p, i = pl.program_id(0), pl.program_id(1)
        j0 = jmin_ref[p, i]
        first_two = _first_two_up(lambda pair, blk: jmin_ref[pair, blk])
        fetch, kbuf, vbuf = _kv_fetcher(kv_hbm, kbuf, vbuf, sems, KV_SLOTS, col0, bq, p, i, nq, first_two)
        is_a = lax.broadcasted_iota(jnp.int32, (bq, LANES), 1) < HEAD_DIM

        @pl.when(i == 0)
        def _():
            dk_ref[...] = jnp.zeros_like(dk_ref)
            dv_ref[...] = jnp.zeros_like(dv_ref)

        dq_a[...] = jnp.zeros_like(dq_a)
        dq_b[...] = jnp.zeros_like(dq_b)
        rs[...] = jnp.zeros_like(rs)
        st_v = st_ref[0]
        for h in range(2):
            rs[6 + 2 * h], rs[7 + 2 * h] = _col(st_v, h), _col(st_v, 2 + h)
        qa[...], qb[...] = _masked_pair(q_ref[...], is_a, SCALE)
        doa[...], dob[...] = _masked_pair(do_ref[...], is_a)

        def tiles(blocks):
            hs, qs, dos, dqs = (0, 1), (qa, qb), (doa, dob), (dq_a, dq_b)
            tfwd_m, trev_m = tfwd_ref[...], trev_ref[...]
            kv = [(kbuf[s], vbuf[s]) for _, s, _ in blocks]
            nb = len(blocks)
            bh = [(b, h) for b in range(nb) for h in hs]
            tri = lax.broadcasted_iota(jnp.int32, (bq, bq), 0) > lax.broadcasted_iota(jnp.int32, (bq, bq), 1)

            def mask(x, b):
                return jnp.where(tri, x, 0.0) if blocks[b][2] else x

            z = {(b, h): _dot(qs[h][...], kv[b][0], _NT) for b, h in bh}
            dw = {(b, h): _dot(dos[h][...], kv[b][1], _NT) for b, h in bh}
            lk = {(b, h): mask(-_softplus(z[b, h]), b) for b, h in bh}
            suf = {(b, h): _split_dot(lk[b, h], trev_m) for b, h in bh}
            tot = {(b, h): jnp.sum(lk[b, h], axis=1, keepdims=True) for b, h in bh}
            pre = {}
            for h in hs:
                run = (rs[3 * h], rs[3 * h + 1])
                for b in range(nb):
                    run = _two_sum(run[0], run[1], tot[b, h])
                    pre[b, h] = run
            right = {(b, h): (rs[6 + 2 * h] - pre[b, h][0]) + (rs[7 + 2 * h] - pre[b, h][1]) for b, h in bh}
            w = {(b, h): mask(jnp.exp(z[b, h] + suf[b, h] + right[b, h]), b) for b, h in bh}
            g = {(b, h): dw[b, h] * w[b, h] for b, h in bh}
            gpre = {(b, h): _split_dot(g[b, h], tfwd_m) for b, h in bh}
            gtot = {(b, h): jnp.sum(g[b, h], axis=1, keepdims=True) for b, h in bh}
            gleft = {}
            for h in hs:
                run = rs[3 * h + 2]
                for b in range(nb):
                    gleft[b, h] = run
                    run = run + gtot[b, h]
                gleft[nb, h] = run
            dz = {(b, h): mask(g[b, h] - jnp.exp(z[b, h] + lk[b, h]) * (gpre[b, h] + gleft[b, h]), b) for b, h in bh}
            dzb = {(b, h): dz[b, h].astype(BF16) for b, h in bh}
            wb = {(b, h): w[b, h].astype(BF16) for b, h in bh}
            dqc = {(b, h): _dot(dzb[b, h], kv[b][0]) for b, h in bh}
            dkc = {(b, h): _dot(dzb[b, h], qs[h][...], _TN) for b, h in bh}
            dvc = {(b, h): _dot(wb[b, h], dos[h][...], _TN) for b, h in bh}
            for h in hs:
                rs[3 * h], rs[3 * h + 1] = pre[nb - 1, h]
                rs[3 * h + 2] = gleft[nb, h]
                dqs[h][...] += sum([dqc[b, h] for b in range(1, nb)], dqc[0, h])
            for b, (j, _, _) in enumerate(blocks):
                rows = pl.ds(pl.multiple_of(j * bq, bq), bq)
                dk_ref[rows, :] += dkc[b, 0] + dkc[b, 1]
                dv_ref[rows, :] += dvc[b, 0] + dvc[b, 1]

        def single(j, slot, masked):
            tiles([(j, slot, masked)])

        def wait(j):
            slot = lax.rem(j - j0, KV_SLOTS)
            for cp in fetch(j, slot):
                cp.wait()
            return slot

        _walk_up(fetch, j0, i, i, single, stop=jnp.maximum(i - 1, j0))

        @pl.when(j0 < i)
        def _():
            tiles([(i - 1, wait(i - 1), False), (i, wait(i), True)])

        @pl.when(j0 == i)
        def _():
            tiles([(i, wait(i), True)])

        dq_ref[...] = (jnp.where(is_a, dq_a[...], dq_b[...]) * SCALE).astype(BF16)

        @pl.when(i == nq - 1)
        def _():
            dk_out[...] = dk_ref[...].astype(BF16)
            dv_out[...] = dv_ref[...].astype(BF16)

    grid_spec = pltpu.PrefetchScalarGridSpec(
        num_scalar_prefetch=1, grid=(N_PAIRS, nq),
        in_specs=[pl.BlockSpec((bq, LANES), lambda p, i, jm: (i, col0 + p)),
                  pl.BlockSpec(memory_space=pl.ANY),
                  pl.BlockSpec((bq, LANES), lambda p, i, jm: (i, p)),
                  pl.BlockSpec((1, bq, 8), lambda p, i, jm: (p, i, 0)),
                  pl.BlockSpec((bq, bq), lambda p, i, jm: (0, 0)),
                  pl.BlockSpec((bq, bq), lambda p, i, jm: (0, 0))],
        out_specs=[pl.BlockSpec((bq, LANES), lambda p, i, jm: (i, p)),
                   pl.BlockSpec((S, LANES), lambda p, i, jm: (0, p)),
                   pl.BlockSpec((S, LANES), lambda p, i, jm: (0, p))],
        scratch_shapes=[pltpu.VMEM((bq, LANES), F32), pltpu.VMEM((bq, LANES), F32)]
        + [pltpu.VMEM((bq, LANES), BF16)] * 4 + [pltpu.VMEM((10, bq, 1), F32)]
        + [pltpu.VMEM((2 * KV_SLOTS, bq, LANES), BF16)] * 2 + [pltpu.SemaphoreType.DMA((2, 2 * KV_SLOTS))]
        + [pltpu.VMEM((S, LANES), F32)] * 2)
    return pl.pallas_call(
        body, name="sb_bwd", grid_spec=grid_spec,
        out_shape=[jax.ShapeDtypeStruct((S, GROUP_W), BF16)] * 3,
        compiler_params=_params(VMEM_BIG),
    )(jmin, proj, proj, do, st, tfwd, trev)


def _walk_up(fetch, j0, diag, last, tile, stop=None):
    ahead = KV_SLOTS - 1
    stop = last + 1 if stop is None else stop

    def start(j):
        @pl.when(j <= last)
        def _():
            for cp in fetch(j, lax.rem(j - j0, KV_SLOTS)):
                cp.start()

    for d in range(KV_EARLY, ahead):
        start(j0 + d)

    def step(j, carry):
        slot = lax.rem(j - j0, KV_SLOTS)
        for cp in fetch(j, slot):
            cp.wait()
        start(j + ahead)
        pl.when(j >= diag)(functools.partial(tile, j, slot, True))
        pl.when(j < diag)(functools.partial(tile, j, slot, False))
        return carry

    lax.fori_loop(j0, stop, step, 0)


def _causal(bq, bk, i, j):
    row = lax.broadcasted_iota(jnp.int32, (bq, bk), 0)
    col = lax.broadcasted_iota(jnp.int32, (bq, bk), 1)
    return col - row <= i * bq - j * bk


def _by_heads(j, first_a, first_b, heads):
    on_a, on_b = j >= first_a, j >= first_b
    pl.when(jnp.logical_and(on_a, on_b))(functools.partial(heads, (0, 1)))
    pl.when(jnp.logical_and(on_a, jnp.logical_not(on_b)))(functools.partial(heads, (0,)))
    pl.when(jnp.logical_and(on_b, jnp.logical_not(on_a)))(functools.partial(heads, (1,)))


def _fox_row_norms(proj, col0, tm):
    S = proj.shape[0]
    tm = min(tm, S)
    head_of = np.arange(GROUP_W) // HEAD_DIM
    he_t = jnp.asarray((np.arange(2 * N_PAIRS)[:, None] == head_of[None, :]).astype(np.float32), BF16)

    def body(q_ref, k_ref, he_ref, qn_ref, kn_ref, d_ref):
        q, k, he = q_ref[...].astype(F32), k_ref[...].astype(F32), he_ref[...]

        def head_sums_t(x):
            hi = x.astype(BF16)
            lo = (x - hi.astype(F32)).astype(BF16)
            return _dot(he, hi, _NT) + _dot(he, lo, _NT)

        qn_ref[...] = jnp.sqrt(head_sums_t(q * q))
        kn_ref[...] = jnp.sqrt(head_sums_t(k * k))
        d_ref[...] = SCALE * head_sums_t(q * k)

    wide = GROUP_W // LANES
    return pl.pallas_call(
        body, name="fox_row_norms", grid=(S // tm,),
        in_specs=[pl.BlockSpec((tm, GROUP_W), lambda i: (i, col0 // wide)),
                  pl.BlockSpec((tm, GROUP_W), lambda i: (i, (col0 + 4) // wide)),
                  pl.BlockSpec((2 * N_PAIRS, GROUP_W), lambda i: (0, 0))],
        out_specs=[pl.BlockSpec((2 * N_PAIRS, tm), lambda i: (0, i))] * 3,
        out_shape=[jax.ShapeDtypeStruct((2 * N_PAIRS, S), F32)] * 3)(proj, proj, he_t)


def _fox_start_blocks(qn, kn, d, c, bq, bk):
    nh, S = c.shape
    nq, nk = S // bq, S // bk
    top = SCALE * qn * kn.max(axis=1, keepdims=True) - d + c
    top = top.reshape(nh, nq, bq).max(axis=2)
    c_last = c[:, bk - 1::bk]
    live = top[:, :, None] - c_last[:, None, :] >= -FOX_SKIP

    def first_block(lv):
        first = jnp.where(lv.any(axis=2), jnp.argmax(lv, axis=2), nk)
        return jnp.minimum(first, (bq // bk) * jnp.arange(nq)[None, :]).astype(jnp.int32)

    return jnp.concatenate([first_block(live.reshape(N_PAIRS, 2, nq, nk).any(axis=1)), first_block(live)], axis=0)


def _fox_fwd(proj, col0, c_col, c_row, jstart, bq, bk):
    S = proj.shape[0]
    nq, per = S // bq, bq // bk

    def body(js_ref, q_ref, kv_hbm, cc_ref, cr_ref, o_ref, st_ref, acc_a, acc_b, qa, qb, ml, kbuf, vbuf, sems):
        p, i = pl.program_id(0), pl.program_id(1)
        j0 = js_ref[p, i]
        first_two = _first_two_up(lambda pair, blk: js_ref[pair, blk], per)
        fetch, kbuf, vbuf = _kv_fetcher(kv_hbm, kbuf, vbuf, sems, KV_SLOTS, col0, bk, p, i, nq, first_two)
        is_a = lax.broadcasted_iota(jnp.int32, (bq, LANES), 1) < HEAD_DIM
        acc_a[...] = jnp.zeros_like(acc_a)
        acc_b[...] = jnp.zeros_like(acc_b)
        ml[0] = jnp.full((bq, 1), NEG_BIG, F32)
        ml[2] = jnp.full((bq, 1), NEG_BIG, F32)
        ml[1] = jnp.zeros((bq, 1), F32)
        ml[3] = jnp.zeros((bq, 1), F32)
        cc = cc_ref[0]
        ml[4], ml[5] = _col(cc, 0), _col(cc, 1)
        qa[...], qb[...] = _masked_pair(q_ref[...], is_a, SCALE)

        def tile(j, slot, masked):
            k, v = kbuf[slot], vbuf[slot]
            cols = pl.ds(pl.multiple_of(j * bk, bk), bk)
            if masked:
                tri = _causal(bq, bk, i, j)

            def heads(hs):
                qs, accs = (qa, qb), (acc_a, acc_b)
                s = {h: _dot(qs[h][...], k, _NT) - cr_ref[0, pl.ds(h, 1), cols] for h in hs}
                if masked:
                    s = {h: jnp.where(tri, s[h], NEG_BIG) for h in hs}
                top = {h: jnp.max(s[h], axis=1, keepdims=True) for h in hs}
                m_new = {h: jnp.maximum(ml[2 * h], top[h] + ml[4 + h]) for h in hs}
                a = {h: jnp.exp(ml[2 * h] - m_new[h]) for h in hs}
                pr = {h: jnp.exp(s[h] - (m_new[h] - ml[4 + h])) for h in hs}
                tot = {h: jnp.sum(pr[h], axis=1, keepdims=True) for h in hs}
                pv = {h: _dot(pr[h].astype(BF16), v) for h in hs}
                for h in hs:
                    ml[2 * h] = m_new[h]
                    ml[2 * h + 1] = a[h] * ml[2 * h + 1] + tot[h]
                    accs[h][...] = a[h] * accs[h][...] + pv[h]

            _by_heads(j, js_ref[N_PAIRS + 2 * p, i], js_ref[N_PAIRS + 2 * p + 1, i], heads)

        _walk_up(fetch, j0, per * i, per * i + per - 1, tile)
        o_ref[...] = jnp.where(is_a, acc_a[...] / ml[1], acc_b[...] / ml[3])
        lane8 = lax.broadcasted_iota(jnp.int32, (bq, 8), 1)
        st = jnp.where(lane8 == 0, ml[0] + jnp.log(ml[1]), 0.0)
        st_ref[0] = jnp.where(lane8 == 1, ml[2] + jnp.log(ml[3]), st)

    grid_spec = pltpu.PrefetchScalarGridSpec(
        num_scalar_prefetch=1, grid=(N_PAIRS, nq),
        in_specs=[pl.BlockSpec((bq, LANES), lambda p, i, js: (i, col0 + p)),
                  pl.BlockSpec(memory_space=pl.ANY),
                  pl.BlockSpec((1, bq, 8), lambda p, i, js: (p, i, 0)),
                  pl.BlockSpec((1, 8, S), lambda p, i, js: (p, 0, 0))],
        out_specs=[pl.BlockSpec((bq, LANES), lambda p, i, js: (i, p)),
                   pl.BlockSpec((1, bq, 8), lambda p, i, js: (p, i, 0))],
        scratch_shapes=[pltpu.VMEM((bq, LANES), F32), pltpu.VMEM((bq, LANES), F32),
                        pltpu.VMEM((bq, LANES), BF16), pltpu.VMEM((bq, LANES), BF16),
                        pltpu.VMEM((6, bq, 1), F32),
                        pltpu.VMEM((2 * KV_SLOTS, bk, LANES), BF16), pltpu.VMEM((2 * KV_SLOTS, bk, LANES), BF16),
                        pltpu.SemaphoreType.DMA((2, 2 * KV_SLOTS))])
    return pl.pallas_call(
        body, name="fox_fwd", grid_spec=grid_spec,
        out_shape=[jax.ShapeDtypeStruct((S, GROUP_W), F32), jax.ShapeDtypeStruct((N_PAIRS, S, 8), F32)],
    )(jstart, proj, proj, c_col, c_row)


def _fox_bwd(proj, col0, do, o, st, c_col, c_row, jstart, bq, bk):
    S = proj.shape[0]
    nq, per = S // bq, bq // bk

    def body(js_ref, q_ref, kv_hbm, do_ref, o_ref, st_ref, cc_ref, cr_ref,
             dq_ref, dk_out, dv_out, dc_ref, dq_a, dq_b, qa, qb, doa, dob, dd, kbuf, vbuf, sems, dk_ref, dv_ref):
        p, i = pl.program_id(0), pl.program_id(1)
        j0 = js_ref[p, i]
        first_two = _first_two_up(lambda pair, blk: js_ref[pair, blk], per)
        fetch, kbuf, vbuf = _kv_fetcher(kv_hbm, kbuf, vbuf, sems, KV_SLOTS, col0, bk, p, i, nq, first_two)
        is_a = lax.broadcasted_iota(jnp.int32, (bq, LANES), 1) < HEAD_DIM

        @pl.when(i == 0)
        def _():
            dk_ref[...] = jnp.zeros_like(dk_ref)
            dv_ref[...] = jnp.zeros_like(dv_ref)
            dc_ref[...] = jnp.zeros_like(dc_ref)

        dq_a[...] = jnp.zeros_like(dq_a)
        dq_b[...] = jnp.zeros_like(dq_b)
        qa[...], qb[...] = _masked_pair(q_ref[...], is_a, SCALE)
        dov = do_ref[...]
        doa[...], dob[...] = _masked_pair(dov, is_a)
        prod = dov * o_ref[...]
        dd[0] = jnp.sum(jnp.where(is_a, prod, 0.0), axis=1, keepdims=True)
        dd[1] = jnp.sum(jnp.where(is_a, 0.0, prod), axis=1, keepdims=True)
        dd[2] = jnp.zeros((bq, 1), F32)
        dd[3] = jnp.zeros((bq, 1), F32)
        cc, st_v = cc_ref[0], st_ref[0]
        dd[4], dd[5] = _col(cc, 0) - _col(st_v, 0), _col(cc, 1) - _col(st_v, 1)

        def tile(j, slot, masked):
            k, v = kbuf[slot], vbuf[slot]
            if masked:
                tri = _causal(bq, bk, i, j)
            cols = pl.ds(pl.multiple_of(j * bk, bk), bk)

            def heads(hs):
                qs, dos, dqs = (qa, qb), (doa, dob), (dq_a, dq_b)
                z = {h: _dot(qs[h][...], k, _NT) for h in hs}
                dp = {h: _dot(dos[h][...], v, _NT) for h in hs}
                pr = {h: jnp.exp(z[h] - cr_ref[0, pl.ds(h, 1), cols] + dd[4 + h]) for h in hs}
                if masked:
                    pr = {h: jnp.where(tri, pr[h], 0.0) for h in hs}
                ds = {h: pr[h] * (dp[h] - dd[h]) for h in hs}
                csum = {h: jnp.sum(ds[h], axis=0, keepdims=True) for h in hs}
                rsum = {h: jnp.sum(ds[h], axis=1, keepdims=True) for h in hs}
                dsb = {h: ds[h].astype(BF16) for h in hs}
                prb = {h: pr[h].astype(BF16) for h in hs}
                dqc = {h: _dot(dsb[h], k) for h in hs}
                dkc = [_dot(dsb[h], qs[h][...], _TN) for h in hs]
                dvc = [_dot(prb[h], dos[h][...], _TN) for h in hs]
                for h in hs:
                    dc_ref[0, pl.ds(h, 1), cols] -= csum[h]
                    dd[2 + h] += rsum[h]
                    dqs[h][...] += dqc[h]
                dk_ref[cols, :] += sum(dkc[1:], dkc[0])
                dv_ref[cols, :] += sum(dvc[1:], dvc[0])

            _by_heads(j, js_ref[N_PAIRS + 2 * p, i], js_ref[N_PAIRS + 2 * p + 1, i], heads)

        _walk_up(fetch, j0, per * i, per * i + per - 1, tile)
        dq_ref[...] = (jnp.where(is_a, dq_a[...], dq_b[...]) * SCALE).astype(BF16)
        eye = lax.broadcasted_iota(jnp.int32, (bq, bq), 0) == lax.broadcasted_iota(jnp.int32, (bq, bq), 1)
        own = pl.ds(pl.multiple_of(i * bq, bq), bq)
        for h in range(2):
            dc_ref[0, pl.ds(h, 1), own] += jnp.sum(jnp.where(eye, dd[2 + h], 0.0), axis=0, keepdims=True)

        @pl.when(i == nq - 1)
        def _():
            dk_out[...] = dk_ref[...].astype(BF16)
            dv_out[...] = dv_ref[...].astype(BF16)

    grid_spec = pltpu.PrefetchScalarGridSpec(
        num_scalar_prefetch=1, grid=(N_PAIRS, nq),
        in_specs=[pl.BlockSpec((bq, LANES), lambda p, i, js: (i, col0 + p)),
                  pl.BlockSpec(memory_space=pl.ANY),
                  pl.BlockSpec((bq, LANES), lambda p, i, js: (i, p)),
                  pl.BlockSpec((bq, LANES), lambda p, i, js: (i, p)),
                  pl.BlockSpec((1, bq, 8), lambda p, i, js: (p, i, 0)),
                  pl.BlockSpec((1, bq, 8), lambda p, i, js: (p, i, 0)),
                  pl.BlockSpec((1, 8, S), lambda p, i, js: (p, 0, 0))],
        out_specs=[pl.BlockSpec((bq, LANES), lambda p, i, js: (i, p)),
                   pl.BlockSpec((S, LANES), lambda p, i, js: (0, p)),
                   pl.BlockSpec((S, LANES), lambda p, i, js: (0, p)),
                   pl.BlockSpec((1, 8, S), lambda p, i, js: (p, 0, 0))],
        scratch_shapes=[pltpu.VMEM((bq, LANES), F32), pltpu.VMEM((bq, LANES), F32)]
        + [pltpu.VMEM((bq, LANES), BF16)] * 4 + [pltpu.VMEM((6, bq, 1), F32)]
        + [pltpu.VMEM((2 * KV_SLOTS, bk, LANES), BF16)] * 2 + [pltpu.SemaphoreType.DMA((2, 2 * KV_SLOTS))]
        + [pltpu.VMEM((S, LANES), F32)] * 2)
    return pl.pallas_call(
        body, name="fox_bwd", grid_spec=grid_spec,
        out_shape=[jax.ShapeDtypeStruct((S, GROUP_W), BF16)] * 3 + [jax.ShapeDtypeStruct((N_PAIRS, 8, S), F32)],
        compiler_params=_params(VMEM_BIG),
    )(jstart, proj, proj, do, o, st, c_col, c_row)


_HBM = pl.BlockSpec(memory_space=pltpu.HBM)


def _coords():
    return lax.axis_index("x"), lax.axis_index("y"), lax.axis_index("c")


def _gather_copies(ins, outs, send_sems, recv_sems, loc_sems):
    n = len(ins)
    x, y, c = _coords()
    mine = 2 * x + y
    chips = [(1 - x, y), (x, 1 - y), (1 - x, 1 - y)]

    def copy(w, r, slab, to):
        return pltpu.make_async_remote_copy(
            src_ref=ins[w], dst_ref=outs[w].at[slab], send_sem=send_sems.at[3 * w + r],
            recv_sem=recv_sems.at[3 * w + r], device_id=to, device_id_type=MESH)

    def own():
        local = [pltpu.make_async_copy(ins[w], outs[w].at[mine], loc_sems.at[w]) for w in range(n)]
        return local, [copy(w, r, mine, (cx, cy, c)) for w in range(n) for r, (cx, cy) in enumerate(chips)]

    def start():
        local, sends = own()
        for cp in local + sends:
            cp.start()

    def wait():
        local, sends = own()
        for w in range(n):
            for r, (cx, cy) in enumerate(chips):
                copy(w, r, 2 * cx + cy, (cx, cy, c)).wait_recv()
        for cp in sends:
            cp.wait_send()
        for cp in local:
            cp.wait()

    return start, wait


def _gather_shapes(shards):
    n = len(shards)
    return ([jax.ShapeDtypeStruct((4,) + s.shape, s.dtype) for s in shards],
            [pltpu.SemaphoreType.DMA((3 * n,)), pltpu.SemaphoreType.DMA((3 * n,)), pltpu.SemaphoreType.DMA((n,))])


def _allgather_chips(shards):
    n = len(shards)

    def body(*refs):
        start, wait = _gather_copies(refs[:n], refs[n:2 * n], *refs[2 * n:])
        start()
        wait()

    out_shape, sems = _gather_shapes(shards)
    return pl.pallas_call(body, name="allgather_weights", in_specs=[_HBM] * n, out_specs=[_HBM] * n,
                          out_shape=out_shape, scratch_shapes=sems)(*shards)


def _exchange_copies(ins, outs, send_sems, recv_sems, loc_sems, per_chip, parts):
    n = len(parts)
    half = [p.shape[1] // 2 for p in parts] if per_chip else None
    x, y, c = _coords()
    me = 4 * x + 2 * y + c
    peers = [(x ^ fx, y ^ fy, c ^ fc) for fx in (0, 1) for fy in (0, 1) for fc in (0, 1)][1:]

    def src(w, dev):
        if not per_chip:
            return ins[w]
        return ins[w].at[2 * dev[0] + dev[1], pl.ds(pl.multiple_of(dev[2] * half[w], 16), half[w]), :]

    def copy(w, r, source, slab, to):
        return pltpu.make_async_remote_copy(
            src_ref=source, dst_ref=outs[w].at[slab], send_sem=send_sems.at[7 * w + r],
            recv_sem=recv_sems.at[7 * w + r], device_id=to, device_id_type=MESH)

    def own():
        local = [pltpu.make_async_copy(src(w, (x, y, c)), outs[w].at[me], loc_sems.at[w]) for w in range(n)]
        return local, [copy(w, r, src(w, dev), me, dev) for w in range(n) for r, dev in enumerate(peers)]

    def start():
        local, sends = own()
        for cp in local + sends:
            cp.start()

    def wait():
        local, sends = own()
        for w in range(n):
            for r, dev in enumerate(peers):
                copy(w, r, src(w, dev), 4 * dev[0] + 2 * dev[1] + dev[2], dev).wait_recv()
        for cp in sends:
            cp.wait_send()
        for cp in local:
            cp.wait()

    return start, wait


def _exchange_shapes(parts, per_chip):
    n = len(parts)
    return ([jax.ShapeDtypeStruct((8, p.shape[1] // 2, p.shape[2]) if per_chip else (8,) + p.shape, p.dtype)
             for p in parts],
            [pltpu.SemaphoreType.DMA((7 * n,)), pltpu.SemaphoreType.DMA((7 * n,)), pltpu.SemaphoreType.DMA((n,))])


def _exchange(parts, per_chip):
    n = len(parts)

    def body(*refs):
        start, wait = _exchange_copies(refs[:n], refs[n:2 * n], *refs[2 * n:], per_chip, parts)
        start()
        wait()

    out_shape, sems = _exchange_shapes(parts, per_chip)
    return pl.pallas_call(body, name="exchange_per_chip" if per_chip else "exchange_all",
                          in_specs=[_HBM] * n, out_specs=[_HBM] * n, out_shape=out_shape, scratch_shapes=sems)(*parts)


def _sibling_swap(halves):
    n = len(halves)

    def body(*refs):
        ins, outs = refs[:n], refs[n:2 * n]
        send_sems, recv_sems, loc_sems = refs[2 * n:]
        x, y, c = _coords()

        def rows(w, core):
            rh = halves[w].shape[0]
            return outs[w].at[pl.ds(pl.multiple_of(core * rh, 8), rh), :]

        def copy(w, core):
            return pltpu.make_async_remote_copy(
                src_ref=ins[w], dst_ref=rows(w, core), send_sem=send_sems.at[w], recv_sem=recv_sems.at[w],
                device_id=(x, y, 1 - c), device_id_type=MESH)

        local = [pltpu.make_async_copy(ins[w], rows(w, c), loc_sems.at[w]) for w in range(n)]
        sends = [copy(w, c) for w in range(n)]
        for cp in local + sends:
            cp.start()
        for w in range(n):
            copy(w, 1 - c).wait_recv()
        for cp in sends:
            cp.wait_send()
        for cp in local:
            cp.wait()

    vmem = pl.BlockSpec(memory_space=pltpu.VMEM)
    return pl.pallas_call(
        body, name="sibling_swap", in_specs=[vmem] * n, out_specs=[vmem] * n,
        out_shape=[jax.ShapeDtypeStruct((2 * h.shape[0], h.shape[1]), h.dtype) for h in halves],
        scratch_shapes=[pltpu.SemaphoreType.DMA((n,)), pltpu.SemaphoreType.DMA((n,)), pltpu.SemaphoreType.DMA((n,))],
    )(*halves)


def _adamw(w, g, m, v):
    m = ADAM_B1 * m + (1.0 - ADAM_B1) * g
    v = ADAM_B2 * v + (1.0 - ADAM_B2) * (g * g)
    m_hat = m / (1.0 - ADAM_B1 ** ADAM_STEP)
    v_hat = v / (1.0 - ADAM_B2 ** ADAM_STEP)
    delta = -ADAM_LR * (m_hat / (jnp.sqrt(v_hat) + ADAM_EPS) + ADAM_WD * w)
    return delta, m, v


def _sum_parts(parts, name, tr):
    _, R, C = parts.shape
    assert R % tr == 0

    def body(p_ref, g_ref):
        g = p_ref[0].astype(F32)
        for d in range(1, 8):
            g = g + p_ref[d].astype(F32)
        g_ref[...] = g

    return pl.pallas_call(
        body, name=name, grid=(R // tr,),
        in_specs=[pl.BlockSpec((8, tr, C), lambda i: (0, i, 0))],
        out_specs=pl.BlockSpec((tr, C), lambda i: (i, 0)), out_shape=jax.ShapeDtypeStruct((R, C), F32),
    )(parts)


def _adamw_call(g, w, m, v, name, tr):
    R, C = w.shape
    assert R % tr == 0

    def body(g_ref, w_ref, m_ref, v_ref, d_ref, nm_ref, nv_ref):
        d_ref[...], nm_ref[...], nv_ref[...] = _adamw(w_ref[...], g_ref[...], m_ref[...], v_ref[...])

    tile = pl.BlockSpec((tr, C), lambda i: (i, 0))
    return pl.pallas_call(
        body, name=name, grid=(R // tr,), in_specs=[tile] * 4,
        out_specs=[tile] * 3, out_shape=[jax.ShapeDtypeStruct((R, C), F32)] * 3,
    )(g, w, m, v)


def _sum_adamw_small(parts, w, m, v):
    def body(p_ref, w_ref, m_ref, v_ref, g_ref, d_ref, nm_ref, nv_ref, loss_ref):
        g = p_ref[0]
        for d in range(1, 8):
            g = g + p_ref[d]
        g_ref[...] = g
        d_ref[...], nm_ref[...], nv_ref[...] = _adamw(w_ref[...], g, m_ref[...], v_ref[...])
        row = lax.broadcasted_iota(jnp.int32, g.shape, 0)
        per_row = jnp.sum(jnp.where(row == 6, g, 0.0), axis=1, keepdims=True)
        loss_ref[...] = jnp.zeros((8, LANES), F32) + jnp.sum(per_row, axis=0, keepdims=True)

    return pl.pallas_call(
        body, name="sum_adamw_small",
        out_shape=[jax.ShapeDtypeStruct((8, D_MODEL), F32)] * 4 + [jax.ShapeDtypeStruct((8, LANES), F32)],
    )(parts, w, m, v)


def _pack_small(ln1_g, ln1_b, ln2_g, ln2_b, g_sb, g_fox, b_f):
    row5 = jnp.pad(b_f.reshape(1, N_FOX), ((0, 0), (0, D_MODEL - N_FOX)))
    rows = [ln1_g.reshape(1, -1), ln1_b.reshape(1, -1), ln2_g.reshape(1, -1), ln2_b.reshape(1, -1),
            jnp.concatenate([g_sb.reshape(1, -1), g_fox.reshape(1, -1)], axis=1), row5,
            jnp.zeros((2, D_MODEL), F32)]
    return jnp.concatenate(rows, axis=0)


def _unpack_small(p):
    return {"ln1_g": p[0:1], "ln1_b": p[1:2], "ln2_g": p[2:3], "ln2_b": p[3:4], "g_sb": p[4:5, :GROUP_W],
            "g_fox": p[4:5, GROUP_W:], "b_f": p[5:6, :N_FOX]}


def kernel(x, w_in, b_f, g_sb, g_fox, w_out, ln1_g, ln1_b, ln2_g, ln2_b, w_gate_up, w_down, loss_target, m_w_in, m_b_f, m_g_sb, m_g_fox, m_w_out, m_ln1_g, m_ln1_b, m_ln2_g, m_ln2_b, m_w_gate_up, m_w_down, v_w_in, v_b_f, v_g_sb, v_g_fox, v_w_out, v_ln1_g, v_ln1_b, v_ln2_g, v_ln2_b, v_w_gate_up, v_w_down):
    S = x.shape[1]
    x2 = x.reshape(S, D_MODEL)
    tgt = loss_target.reshape(S, D_MODEL)
    TM = 1024
    TR = 512
    BQ = ATTN_BLOCK
    in_w = w_in.shape[2]
    gu_w = w_gate_up.shape[2]

    shards = [w_in[0].astype(BF16), w_out[0].astype(BF16), w_gate_up[0].astype(BF16), w_down[0].astype(BF16)]
    (wi_s,) = _allgather_chips(shards[:1])
    wi = wi_s.transpose(1, 0, 2).reshape(D_MODEL, 4 * in_w)
    w_sb, w_fx = wi[:, :QKV_W // 2], wi[:, QKV_W // 2:QKV_W]
    wqkv = wi[:, :QKV_W]
    wft = wi[:, QKV_W:].T
    proj = _matmul(x2, wqkv, mode="nn", name="proj", tm=TM, tn=512, tk=D_MODEL, outs=[BF16])
    g_row = jnp.concatenate([g_sb, g_fox], axis=1)
    hid = np.arange(D_MODEL) // HEAD_DIM
    he_np = (hid[:, None] == np.arange(LANES)[None, :]).astype(np.float32)
    he, het = jnp.asarray(he_np, BF16), jnp.asarray(he_np.T, BF16)

    lf = _fgate_fwd(x2, wft, b_f.reshape(N_FOX, 1), TM)
    c = _cumsum_fwd(lf)
    c_pair = c.reshape(N_PAIRS, 2, S)
    c_row = jnp.pad(c_pair, ((0, 0), (0, 6), (0, 0)))
    c_col = jnp.pad(c_pair.transpose(0, 2, 1), ((0, 0), (0, 0), (0, 6)))

    o_sb, st_sb, jmin_sb, wo_s, wgu_s, wd_s = _sb_fwd(proj, 0, BQ, shards[1:])
    wo = wo_s.reshape(D_MODEL, D_MODEL)
    wgu = wgu_s.transpose(1, 0, 2).reshape(D_MODEL, 2 * D_FF)
    wg, wu = wgu[:, :D_FF], wgu[:, D_FF:]
    wd = wd_s.reshape(D_FF, D_MODEL)
    jstart_fx = _fox_start_blocks(*_fox_row_norms(proj, 12, TR), c, BQ, BQ)
    o_fx, st_fx = _fox_fwd(proj, 12, c_col, c_row, jstart_fx, BQ, BQ)

    def attn_post(i, osb_ref, ofx_ref, g_ref, he_ref, het_ref, on_ref):
        o = jnp.concatenate([osb_ref[...], ofx_ref[...]], axis=1)
        ms = _head_sums(o * o, he_ref[...], het_ref[...]) * (1.0 / HEAD_DIM)
        on_ref[...] = (o * lax.rsqrt(ms + RMS_EPS) * g_ref[...]).astype(BF16)

    (on,) = _rowwise(attn_post, "attn_post", S, TR,
                     [(o_sb, "t"), (o_fx, "t"), (g_row, "f"), (he, "f"), (het, "f")],
                     [((S, D_MODEL), BF16, "t")])

    u1 = _matmul(on, wo, mode="nn", name="mix", tm=TM, tn=D_MODEL, tk=D_MODEL, outs=[F32],
                 extras=[(x2, (TM if S >= TM else S, D_MODEL), _tile_ij)],
                 epilogue=lambda acc, xv: (ALPHA * xv + acc,))

    def ln1_fwd(i, u_ref, g_ref, b_ref, h_ref):
        xh, _ = _ln_stats(u_ref[...])
        h_ref[...] = xh * g_ref[...] + b_ref[...]

    (h1,) = _rowwise(ln1_fwd, "ln1_fwd", S, TR, [(u1, "t"), (ln1_g, "f"), (ln1_b, "f")], [((S, D_MODEL), F32, "t")])

    tm_e = TM if S >= TM else S
    n_ff = D_FF // 256

    def gate_up_body(h_ref, wg_ref, wu_ref, g_ref, u_ref, a_ref):
        h = h_ref[...].astype(BF16)
        g, u = _dot(h, wg_ref[...]), _dot(h, wu_ref[...])
        g_ref[...] = g.astype(BF16)
        u_ref[...] = u.astype(BF16)
        a_ref[...] = (g * _sigmoid(g) * u).astype(BF16)

    tm_g = min(2 * TM, S)
    ff_tile = pl.BlockSpec((tm_g, 256), lambda i, j: (i, j))
    gate, up, act = pl.pallas_call(
        gate_up_body, name="gate_up_act", grid=(S // tm_g, n_ff),
        in_specs=[pl.BlockSpec((tm_g, D_MODEL), lambda i, j: (i, 0)),
                  pl.BlockSpec((D_MODEL, 256), lambda i, j: (0, j)),
                  pl.BlockSpec((D_MODEL, 256), lambda i, j: (0, j + n_ff))],
        out_specs=[ff_tile] * 3, out_shape=[jax.ShapeDtypeStruct((S, D_FF), BF16)] * 3)(h1, wgu, wgu)

    u2 = _matmul(act, wd, mode="nn", name="ffn_down", tm=TM, tn=D_MODEL, tk=D_FF, outs=[F32],
                 extras=[(h1, (TM if S >= TM else S, D_MODEL), _tile_ij)],
                 epilogue=lambda acc, hv: (ALPHA * hv + acc,))

    def ln2_loss(i, u_ref, t_ref, g_ref, b_ref, du_ref, acc_ref):
        xh, r = _ln_stats(u_ref[...])
        g = g_ref[...]
        err = xh * g + b_ref[...] - t_ref[...]
        dy = err * (1.0 / D_MODEL)
        du_ref[...] = _ln_bwd(dy, xh, r, g)
        _acc_rows(i, acc_ref, {2: jnp.sum(dy * xh, axis=0, keepdims=True), 3: jnp.sum(dy, axis=0, keepdims=True),
                               6: jnp.sum(err * err, axis=0, keepdims=True) * (0.5 / D_MODEL)})

    du2, acc_ln2 = _rowwise(ln2_loss, "ln2_loss", S, TR, [(u2, "t"), (tgt, "t"), (ln2_g, "f"), (ln2_b, "f")],
                            [((S, D_MODEL), F32, "t"), ((8, D_MODEL), F32, "f")])

    d_wd = _matmul(act, du2, mode="tn", name="dw_down", tm=1408, tn=D_MODEL, tk=TM, outs=[BF16])

    def dgu_epilogue(da, g, u):
        g, u = g.astype(F32), u.astype(F32)
        s = _sigmoid(g)
        return da * u * (s * (1.0 + g * (1.0 - s))), da * (g * s)

    dgate, dup = _matmul(du2, wd, mode="nt", name="d_act", tm=TM, tn=1408, tk=D_MODEL, outs=[BF16, BF16],
                         extras=[(gate, (tm_e, 1408), _tile_ij), (up, (tm_e, 1408), _tile_ij)],
                         epilogue=dgu_epilogue)
    d_wg = _matmul(h1, dgate, mode="tn", name="dw_gate", tm=D_MODEL, tn=1408, tk=TM, outs=[BF16])
    d_wu = _matmul(h1, dup, mode="tn", name="dw_up", tm=D_MODEL, tn=1408, tk=TM, outs=[BF16])
    d_wgu = jnp.concatenate([d_wg, d_wu], axis=1)
    dh1, got_down = _matmul(dgate, wg, mode="nt", name="dh1_gate", tm=TM, tn=D_MODEL, tk=D_FF, outs=[F32],
                            extras=[(du2, (tm_e, D_MODEL), _tile_ij)], epilogue=lambda acc, e: (ALPHA * e + acc,),
                            hosted=[d_wd.reshape(4, D_FF // 4, D_MODEL)])
    dh1, got_gu = _matmul(dup, wu, mode="nt", name="dh1_up", tm=TM, tn=D_MODEL, tk=D_FF, outs=[F32],
                          extras=[(dh1, (tm_e, D_MODEL), _tile_ij)], epilogue=lambda acc, e: (e + acc,),
                          hosted=[d_wgu.reshape(D_MODEL, 4, gu_w).transpose(1, 0, 2)])

    def ln1_bwd(i, dh_ref, u_ref, g_ref, du_ref, acc_ref):
        xh, r = _ln_stats(u_ref[...])
        dh = dh_ref[...]
        du_ref[...] = _ln_bwd(dh, xh, r, g_ref[...])
        _acc_rows(i, acc_ref, {0: jnp.sum(dh * xh, axis=0, keepdims=True), 1: jnp.sum(dh, axis=0, keepdims=True)})

    du1, acc_ln1 = _rowwise(ln1_bwd, "ln1_bwd", S, TR, [(dh1, "t"), (u1, "t"), (ln1_g, "f")],
                            [((S, D_MODEL), F32, "t"), ((8, D_MODEL), F32, "f")])
    d_wo = _matmul(on, du1, mode="tn", name="dw_out", tm=D_MODEL, tn=D_MODEL, tk=TM, outs=[BF16])
    don, got_out = _matmul(du1, wo, mode="nt", name="d_on", tm=TM, tn=D_MODEL, tk=D_MODEL, outs=[F32],
                           hosted=[d_wo.reshape(4, D_MODEL // 4, D_MODEL)])

    def rms_bwd(i, don_ref, osb_ref, ofx_ref, g_ref, he_ref, het_ref, dosb_ref, dofx_ref, acc_ref):
        o = jnp.concatenate([osb_ref[...], ofx_ref[...]], axis=1)
        hev, hetv = he_ref[...], het_ref[...]
        r = lax.rsqrt(_head_sums(o * o, hev, hetv) * (1.0 / HEAD_DIM) + RMS_EPS)
        dn = don_ref[...]
        dg = dn * g_ref[...]
        do = r * dg - o * (r * r * r) * (_head_sums(dg * o, hev, hetv) * (1.0 / HEAD_DIM))
        dosb_ref[...] = do[:, :GROUP_W]
        dofx_ref[...] = do[:, GROUP_W:]
        _acc_rows(i, acc_ref, {4: jnp.sum(dn * o * r, axis=0, keepdims=True)})

    do_sb, do_fx, acc_rms = _rowwise(
        rms_bwd, "rms_bwd", S, TR, [(don, "t"), (o_sb, "t"), (o_fx, "t"), (g_row, "f"), (he, "f"), (het, "f")],
        [((S, GROUP_W), F32, "t"), ((S, GROUP_W), F32, "t"), ((8, D_MODEL), F32, "f")])

    dq_sb, dk_sb, dv_sb = _sb_bwd(proj, 0, do_sb, st_sb, jmin_sb, BQ)
    jstart_fx2 = jnp.minimum(jstart_fx[:, 0::2], jstart_fx[:, 1::2])
    dq_fx, dk_fx, dv_fx, dc = _fox_bwd(proj, 12, do_fx, o_fx, st_fx, c_col, c_row, jstart_fx2, 2 * BQ, BQ)
    dfl, dbf = _fgate_bwd(dc[:, :2, :].reshape(N_FOX, S), lf)
    dp_sb = jnp.concatenate([dq_sb, dk_sb, dv_sb], axis=1)
    dp_fx = jnp.concatenate([dq_fx, dk_fx, dv_fx], axis=1)

    d_wsb = _matmul(x2, dp_sb, mode="tn", name="dw_in_sb", tm=D_MODEL, tn=QKV_W // 2, tk=TM, outs=[BF16])
    d_wfx = _matmul(x2, dp_fx, mode="tn", name="dw_in_fx", tm=D_MODEL, tn=QKV_W // 2, tk=TM, outs=[BF16])
    d_wft = _matmul(dfl, x2, mode="nn", name="dw_in_f", tm=N_FOX, tn=D_MODEL, tk=TM, outs=[BF16])
    d_wi = jnp.concatenate([d_wsb, d_wfx, d_wft.T], axis=1)
    dx, got_in = _matmul(dp_sb, w_sb, mode="nt", name="dx_sb", tm=TM, tn=D_MODEL, tk=QKV_W // 2, outs=[F32],
                         extras=[(du1, (tm_e, D_MODEL), _tile_ij)], epilogue=lambda acc, e: (ALPHA * e + acc,),
                         hosted=[d_wi.reshape(D_MODEL, 4, in_w).transpose(1, 0, 2)])
    dx = _matmul(dp_fx, w_fx, mode="nt", name="dx_fx", tm=TM, tn=D_MODEL, tk=QKV_W // 2, outs=[F32],
                 extras=[(dx, (tm_e, D_MODEL), _tile_ij)], epilogue=lambda acc, e: (e + acc,))
    dx = _matmul(dfl, wft, mode="tn", name="dx_f", tm=TM, tn=D_MODEL, tk=N_FOX, outs=[F32],
                 extras=[(dx, (tm_e, D_MODEL), _tile_ij)], epilogue=lambda acc, e: (e + acc,))

    got = [got_in, got_out, got_gu, got_down]
    big_names = ("w_in", "w_out", "w_gate_up", "w_down")
    halves = [_sum_parts(p, "sum_" + nm, tr) for nm, p, tr in zip(big_names, got, (256, 128, 128, 176))]
    grads = _sibling_swap(halves)
    big = {}
    for nm, g, w, m, v, tr in zip(big_names, grads, (w_in, w_out, w_gate_up, w_down),
                                  (m_w_in, m_w_out, m_w_gate_up, m_w_down),
                                  (v_w_in, v_w_out, v_w_gate_up, v_w_down), (256, 256, 256, 176)):
        big[nm] = [r[None] for r in [g] + list(_adamw_call(g, w[0], m[0], v[0], "adamw_" + nm, tr))]

    small = acc_ln2 + acc_ln1 + acc_rms
    small = small + jnp.pad(dbf.reshape(1, N_FOX), ((5, 2), (0, D_MODEL - N_FOX)))
    (small_all,) = _exchange([small], False)
    sw = _pack_small(ln1_g, ln1_b, ln2_g, ln2_b, g_sb, g_fox, b_f)
    sm = _pack_small(m_ln1_g, m_ln1_b, m_ln2_g, m_ln2_b, m_g_sb, m_g_fox, m_b_f)
    sv = _pack_small(v_ln1_g, v_ln1_b, v_ln2_g, v_ln2_b, v_g_sb, v_g_fox, v_b_f)
    sg, sd, snm, snv, loss_blk = _sum_adamw_small(small_all, sw, sm, sv)
    sg, sd, snm, snv = _unpack_small(sg), _unpack_small(sd), _unpack_small(snm), _unpack_small(snv)

    names = ["w_in", "b_f", "g_sb", "g_fox", "w_out", "ln1_g", "ln1_b", "ln2_g", "ln2_b", "w_gate_up", "w_down"]
    outs = [loss_blk[0, 0], dx.reshape(1, S, D_MODEL)]
    for k, table in enumerate((sg, sd, snm, snv)):
        outs += [big[n][k] if n in big else table[n] for n in names]
    return tuple(outs)
```

```python
import functools

import numpy as np
import jax
import jax.numpy as jnp
from jax import lax
from jax.experimental import pallas as pl
from jax.experimental.pallas import tpu as pltpu

F32 = jnp.float32
BF16 = jnp.bfloat16

D_MODEL = 1024
HEAD_DIM = 64
LANES = 128
N_PAIRS = 4
GROUP_W = 512
QKV_W = 3072
D_FF = 2816
N_FOX = 8
ALPHA = 2.0 ** 0.25
LN_EPS = 1e-5
RMS_EPS = 1e-6
SCALE = HEAD_DIM ** -0.5
NEG_BIG = -1e30
FOX_SKIP = 30.0
SB_STOP = -105.0
ADAM_LR, ADAM_B1, ADAM_B2, ADAM_EPS, ADAM_WD, ADAM_STEP = 0.001, 0.9, 0.999, 1e-08, 0.01, 10
KV_SLOTS = 4
SCAN_GROUP = 8
ATTN_BLOCK = 256
VMEM_BIG = 56 * 1024 * 1024
MESH = pl.DeviceIdType.MESH

_NN = (((1,), (0,)), ((), ()))
_NT = (((1,), (1,)), ((), ()))
_TN = (((0,), (0,)), ((), ()))


def _dot(a, b, dims=_NN):
    return lax.dot_general(a, b, dims, preferred_element_type=F32)


def _split_dot(x, t):
    hi = x.astype(BF16)
    lo = (x - hi.astype(F32)).astype(BF16)
    return _dot(hi, t) + _dot(lo, t)


def _softplus(z):
    return jnp.maximum(z, 0.0) + jnp.log1p(jnp.exp(-jnp.abs(z)))


def _sigmoid(x):
    return 0.5 * jnp.tanh(0.5 * x) + 0.5


def _col(v, h):
    lane = lax.broadcasted_iota(jnp.int32, v.shape, 1)
    return jnp.sum(jnp.where(lane == h, v, 0.0), axis=1, keepdims=True)


def _two_sum(hi, lo, b):
    s = hi + b
    bb = s - hi
    err = (hi - (s - bb)) + (b - bb)
    return s, lo + err


def _params(vmem=None):
    return pltpu.CompilerParams(vmem_limit_bytes=vmem) if vmem else None


def _matmul(a, b, *, mode, name, tm, tn, tk, outs, extras=(), epilogue=None, vmem=None, hosted=()):
    if mode == "nn":
        (M, K), (_, N) = a.shape, b.shape
    elif mode == "nt":
        (M, K), (N, _) = a.shape, b.shape
    else:
        (K, M), (_, N) = a.shape, b.shape
    tm, tn, tk = min(tm, M), min(tn, N), min(tk, K)
    assert M % tm == 0 and N % tn == 0 and K % tk == 0, (name, M, N, K, tm, tn, tk)
    nk = K // tk
    dims = {"nn": _NN, "nt": _NT, "tn": _TN}[mode]
    if mode == "tn":
        a_spec = pl.BlockSpec((tk, tm), lambda i, j, k: (k, i))
    else:
        a_spec = pl.BlockSpec((tm, tk), lambda i, j, k: (i, k))
    if mode == "nt":
        b_spec = pl.BlockSpec((tn, tk), lambda i, j, k: (j, k))
    else:
        b_spec = pl.BlockSpec((tk, tn), lambda i, j, k: (k, j))
    ex_specs = [pl.BlockSpec(bs, (lambda i, j, k, f=f: f(i, j))) for (_, bs, f) in extras]
    ne, no, nh = len(extras), len(outs), len(hosted)
    if epilogue is None:
        epilogue = lambda acc: (acc,)
    gi, gj = M // tm, N // tn
    host_shapes, host_sems = _exchange_shapes(hosted, True) if nh else ([], [])

    def body(a_ref, b_ref, *rest):
        ex_refs, host_ins = rest[:ne], rest[ne:ne + nh]
        out_refs, host_outs = rest[ne + nh:ne + nh + no], rest[ne + nh + no:ne + 2 * nh + no]
        acc = rest[ne + 2 * nh + no]
        i, j, k = pl.program_id(0), pl.program_id(1), pl.program_id(2)
        if nh:
            start, wait = _exchange_copies(host_ins, host_outs, *rest[ne + 2 * nh + no + 1:], True, hosted)
            pl.when(jnp.logical_and(jnp.logical_and(i == 0, j == 0), k == 0))(start)

        @pl.when(k == 0)
        def _():
            acc[...] = jnp.zeros_like(acc)

        acc[...] += _dot(a_ref[...].astype(BF16), b_ref[...].astype(BF16), dims)

        @pl.when(k == nk - 1)
        def _():
            res = epilogue(acc[...], *[e[...] for e in ex_refs])
            for r, o in zip(res, out_refs):
                o[...] = r.astype(o.dtype)

        if nh:
            pl.when(jnp.logical_and(jnp.logical_and(i == gi - 1, j == gj - 1), k == nk - 1))(wait)

    res = pl.pallas_call(
        body, name=name, grid=(gi, gj, nk),
        in_specs=[a_spec, b_spec] + ex_specs + [_HBM] * nh,
        out_specs=[pl.BlockSpec((tm, tn), lambda i, j, k: (i, j)) for _ in outs] + [_HBM] * nh,
        out_shape=[jax.ShapeDtypeStruct((M, N), d) for d in outs] + host_shapes,
        scratch_shapes=[pltpu.VMEM((tm, tn), F32)] + host_sems,
        compiler_params=_params(vmem),
    )(a, b, *[e[0] for e in extras], *hosted)
    return res[0] if no + nh == 1 else res


def _tile_ij(i, j):
    return (i, j)


def _rowwise(fn, name, rows, tm, ins, outs, vmem=None):
    tm = min(tm, rows)
    assert rows % tm == 0

    def spec(shape, kind):
        if kind == "t":
            return pl.BlockSpec((tm,) + tuple(shape[1:]), lambda i: (i,) + (0,) * (len(shape) - 1))
        return pl.BlockSpec(tuple(shape), lambda i: (0,) * len(shape))

    def body(*refs):
        fn(pl.program_id(0), *refs)

    return pl.pallas_call(
        body, name=name, grid=(rows // tm,),
        in_specs=[spec(a.shape, k) for a, k in ins],
        out_specs=[spec(s, k) for s, _, k in outs],
        out_shape=[jax.ShapeDtypeStruct(s, d) for s, d, _ in outs],
        compiler_params=_params(vmem),
    )(*[a for a, _ in ins])


def _ln_stats(u):
    mu = jnp.mean(u, axis=-1, keepdims=True)
    d = u - mu
    var = jnp.mean(d * d, axis=-1, keepdims=True)
    r = lax.rsqrt(var + LN_EPS)
    return d * r, r


def _ln_bwd(dh, xh, r, g):
    dxh = dh * g
    m1 = jnp.mean(dxh, axis=-1, keepdims=True)
    m2 = jnp.mean(dxh * xh, axis=-1, keepdims=True)
    return r * (dxh - m1 - xh * m2)


def _acc_rows(i, ref, rows):
    @pl.when(i == 0)
    def _():
        ref[...] = jnp.zeros_like(ref)
    for r, v in rows.items():
        ref[pl.ds(r, 1), :] += v


def _head_sums(v, he, het):
    return _split_dot(_split_dot(v, he), het)


def _fgate_fwd(x, wft, bf_col, tm):
    S = x.shape[0]
    tm = min(tm, S)

    def body(wft_ref, bf_ref, x_ref, lf_ref):
        f = _dot(wft_ref[...], x_ref[...].astype(BF16), _NT) + bf_ref[...]
        lf_ref[...] = -_softplus(-f)

    return pl.pallas_call(
        body, name="fgate_fwd", grid=(S // tm,),
        in_specs=[pl.BlockSpec((N_FOX, D_MODEL), lambda i: (0, 0)), pl.BlockSpec((N_FOX, 1), lambda i: (0, 0)),
                  pl.BlockSpec((tm, D_MODEL), lambda i: (i, 0))],
        out_specs=pl.BlockSpec((N_FOX, tm), lambda i: (0, i)),
        out_shape=jax.ShapeDtypeStruct((N_FOX, S), F32),
    )(wft, bf_col, x)


def _chunk_scan(v, reverse):
    lane = lax.broadcasted_iota(jnp.int32, v.shape, 1)
    sh = 1
    while sh < LANES:
        if reverse:
            v = v + jnp.where(lane < LANES - sh, pltpu.roll(v, LANES - sh, 1), 0.0)
        else:
            v = v + jnp.where(lane >= sh, pltpu.roll(v, sh, 1), 0.0)
        sh *= 2
    return v


def _cumsum_fwd(lf):
    n, S = lf.shape
    nc = S // LANES

    grp = min(SCAN_GROUP, nc)

    def body(lf_ref, c_ref):
        def step(gi, carry):
            sls = [pl.ds(pl.multiple_of((gi * grp + g) * LANES, LANES), LANES) for g in range(grp)]
            vs = [_chunk_scan(lf_ref[:, sl], False) for sl in sls]
            tots = [_col(v, LANES - 1) for v in vs]
            for sl, v, t in zip(sls, vs, tots):
                c_ref[:, sl] = v + carry
                carry = carry + t
            return carry
        lax.fori_loop(0, nc // grp, step, jnp.zeros((n, 1), F32))

    return pl.pallas_call(body, name="cumsum_fwd", out_shape=jax.ShapeDtypeStruct((n, S), F32))(lf)


def _fgate_bwd(dc, lf):
    n, S = dc.shape
    nc = S // LANES

    grp = min(SCAN_GROUP, nc)

    def body(dc_ref, lf_ref, dfl_ref, dbf_ref):
        def step(t, carry):
            car, tot = carry
            gi = nc // grp - 1 - t
            sls = [pl.ds(pl.multiple_of((gi * grp + g) * LANES, LANES), LANES) for g in range(grp)]
            vs = [_chunk_scan(dc_ref[:, sl], True) for sl in sls]
            firsts = [_col(v, 0) for v in vs]
            for sl, v, f in reversed(list(zip(sls, vs, firsts))):
                dfl = (v + car) * (1.0 - jnp.exp(lf_ref[:, sl]))
                dfl_ref[:, sl] = dfl
                tot = tot + jnp.sum(dfl, axis=1, keepdims=True)
                car = car + f
            return car, tot
        _, tot = lax.fori_loop(0, nc // grp, step, (jnp.zeros((n, 1), F32), jnp.zeros((n, 1), F32)))
        dbf_ref[...] = tot

    return pl.pallas_call(body, name="fgate_bwd",
                          out_shape=[jax.ShapeDtypeStruct((n, S), F32), jax.ShapeDtypeStruct((n, 1), F32)])(dc, lf)


def _tri_matrices(b):
    r = np.arange(b)
    tfwd = (r[:, None] <= r[None, :]).astype(np.float32)
    return jnp.asarray(tfwd, BF16), jnp.asarray(tfwd.T, BF16)


def _kv_copies(kv_hbm, kbuf, vbuf, sems, sem0, pair_col, bq, j, slot):
    rows = pl.ds(pl.multiple_of(j * bq, bq), bq)

    def cols(c):
        return pl.ds(pl.multiple_of((pair_col + c) * LANES, LANES), LANES)

    return (pltpu.make_async_copy(kv_hbm.at[rows, cols(4)], kbuf.at[slot], sems.at[0, sem0 + slot]),
            pltpu.make_async_copy(kv_hbm.at[rows, cols(8)], vbuf.at[slot], sems.at[1, sem0 + slot]))


def _first_two_up(first_block, per=1):
    def blocks(pair, blk):
        first = first_block(pair, blk)
        return first, first + 1, first + 1 <= per * blk + per - 1
    return blocks


def _first_two_down(pair, blk):
    return blk, blk - 1, blk > 0


def _start_two(fetch, pair, first, second, has_second, ahead):
    for cp in fetch(first, 0, pair, ahead):
        cp.start()

    @pl.when(has_second)
    def _():
        for cp in fetch(second, 1, pair, ahead):
            cp.start()


def _kv_fetcher(kv_hbm, kbuf, vbuf, sems, ns, col0, bq, p, i, nq, blocks):
    base = lax.rem(p * nq + i, 2) * ns
    own = (kbuf.at[pl.ds(base, ns)], vbuf.at[pl.ds(base, ns)])
    other = (kbuf.at[pl.ds(ns - base, ns)], vbuf.at[pl.ds(ns - base, ns)])

    def fetch(j, slot, pair=p, ahead=False):
        kb, vb = other if ahead else own
        return _kv_copies(kv_hbm, kb, vb, sems, ns - base if ahead else base, col0 + pair, bq, j, slot)

    pl.when(jnp.logical_and(p == 0, i == 0))(lambda: _start_two(fetch, p, *blocks(p, i), False))
    wrap = i == nq - 1

    @pl.when(jnp.logical_not(jnp.logical_and(wrap, p == N_PAIRS - 1)))
    def _():
        pair, blk = jnp.where(wrap, p + 1, p), jnp.where(wrap, 0, i + 1)
        _start_two(fetch, pair, *blocks(pair, blk), True)

    return fetch, own[0], own[1]


def _masked_pair(v, lane_is_a, scale=1.0):
    v = v.astype(F32) * scale
    return jnp.where(lane_is_a, v, 0.0).astype(BF16), jnp.where(lane_is_a, 0.0, v).astype(BF16)


def _sb_fwd(proj, col0, bq, shards=()):
    S = proj.shape[0]
    bq = min(bq, S)
    nq = S // bq
    _, trev = _tri_matrices(bq)
    nh = len(shards)
    gather_shapes, gather_sems = _gather_shapes(shards) if nh else ([], [])

    def body(q_ref, kv_hbm, trev_ref, *rest):
        o_ref, st_ref, jmin_ref = rest[nh:nh + 3]
        acc_a, acc_b, qa, qb, rs, kbuf, vbuf, sems = rest[2 * nh + 3:2 * nh + 11]
        p, i = pl.program_id(0), pl.program_id(1)
        if nh:
            gather_start, gather_wait = _gather_copies(rest[:nh], rest[nh + 3:2 * nh + 3], *rest[2 * nh + 11:])
            pl.when(jnp.logical_and(p == 0, i == 0))(gather_start)
        fetch, kbuf, vbuf = _kv_fetcher(kv_hbm, kbuf, vbuf, sems, 2, col0, bq, p, i, nq, _first_two_down)
        is_a = lax.broadcasted_iota(jnp.int32, (bq, LANES), 1) < HEAD_DIM
        acc_a[...] = jnp.zeros_like(acc_a)
        acc_b[...] = jnp.zeros_like(acc_b)
        rs[...] = jnp.zeros_like(rs)
        qa[...], qb[...] = _masked_pair(q_ref[...], is_a, SCALE)

        def tiles(blocks):
            hs, qs, accs, trev_m = (0, 1), (qa, qb), (acc_a, acc_b), trev_ref[...]
            kv = [(kbuf[s], vbuf[s]) for s, _ in blocks]
            bh = [(b, h) for b in range(len(blocks)) for h in hs]
            tri = lax.broadcasted_iota(jnp.int32, (bq, bq), 0) > lax.broadcasted_iota(jnp.int32, (bq, bq), 1)
            z = {(b, h): _dot(qs[h][...], kv[b][0], _NT) for b, h in bh}
            lk = {(b, h): -_softplus(z[b, h]) for b, h in bh}
            lk = {(b, h): jnp.where(tri, lk[b, h], 0.0) if blocks[b][1] else lk[b, h] for b, h in bh}
            suf = {(b, h): _split_dot(lk[b, h], trev_m) for b, h in bh}
            tot = {(b, h): jnp.sum(lk[b, h], axis=1, keepdims=True) for b, h in bh}
            right = {}
            for h in hs:
                r = rs[2 * h] + rs[2 * h + 1]
                for b in range(len(blocks)):
                    right[b, h] = r
                    r = r + tot[b, h]
            w = {(b, h): jnp.exp(z[b, h] + suf[b, h] + right[b, h]) for b, h in bh}
            w = {(b, h): jnp.where(tri, w[b, h], 0.0) if blocks[b][1] else w[b, h] for b, h in bh}
            pv = {(b, h): _dot(w[b, h].astype(BF16), kv[b][1]) for b, h in bh}
            for h in hs:
                accs[h][...] += sum([pv[b, h] for b in range(1, len(blocks))], pv[0, h])
                hi, lo = rs[2 * h], rs[2 * h + 1]
                for b in range(len(blocks)):
                    hi, lo = _two_sum(hi, lo, tot[b, h])
                rs[2 * h], rs[2 * h + 1] = hi, lo

        def live():
            return (jnp.max(jnp.maximum(rs[0], rs[2])) > SB_STOP).astype(jnp.int32)

        for cp in fetch(i, 0):
            cp.wait()
        pl.when(i == 0)(functools.partial(tiles, [(0, True)]))

        @pl.when(i > 0)
        def _():
            for cp in fetch(i - 1, 1):
                cp.wait()
            tiles([(0, True), (1, False)])

        def step(carry):
            j, _ = carry
            slot = lax.rem(i - j, 2)
            for cp in fetch(j, slot):
                cp.start()
            for cp in fetch(j, slot):
                cp.wait()
            tiles([(slot, False)])
            return j - 1, live()

        j_end, _ = lax.while_loop(lambda c: jnp.logical_and(c[0] >= 0, c[1] > 0), step, (i - 2, live()))
        jmin_ref[p, i] = jnp.maximum(j_end + 1, 0)
        o_ref[...] = jnp.where(is_a, acc_a[...], acc_b[...])
        lane8 = lax.broadcasted_iota(jnp.int32, (bq, 8), 1)
        st = jnp.zeros((bq, 8), F32)
        for c, src in enumerate((0, 2, 1, 3)):
            st = jnp.where(lane8 == c, rs[src], st)
        st_ref[0] = st
        if nh:
            pl.when(jnp.logical_and(p == N_PAIRS - 1, i == nq - 1))(gather_wait)

    return pl.pallas_call(
        body, name="sb_fwd", grid=(N_PAIRS, nq),
        in_specs=[pl.BlockSpec((bq, LANES), lambda p, i: (i, col0 + p)),
                  pl.BlockSpec(memory_space=pl.ANY),
                  pl.BlockSpec((bq, bq), lambda p, i: (0, 0))] + [_HBM] * nh,
        out_specs=[pl.BlockSpec((bq, LANES), lambda p, i: (i, p)),
                   pl.BlockSpec((1, bq, 8), lambda p, i: (p, i, 0)),
                   pl.BlockSpec(memory_space=pltpu.SMEM)] + [_HBM] * nh,
        out_shape=[jax.ShapeDtypeStruct((S, GROUP_W), F32), jax.ShapeDtypeStruct((N_PAIRS, S, 8), F32),
                   jax.ShapeDtypeStruct((N_PAIRS, nq), jnp.int32)] + gather_shapes,
        scratch_shapes=[pltpu.VMEM((bq, LANES), F32), pltpu.VMEM((bq, LANES), F32),
                        pltpu.VMEM((bq, LANES), BF16), pltpu.VMEM((bq, LANES), BF16),
                        pltpu.VMEM((4, bq, 1), F32),
                        pltpu.VMEM((4, bq, LANES), BF16), pltpu.VMEM((4, bq, LANES), BF16),
                        pltpu.SemaphoreType.DMA((2, 4))] + gather_sems,
    )(proj, proj, trev, *shards)


def _sb_bwd(proj, col0, do, st, jmin, bq):
    S = proj.shape[0]
    bq = min(bq, S)
    nq = S // bq
    tfwd, trev = _tri_matrices(bq)

    def body(jmin_ref, q_ref, kv_hbm, do_ref, st_ref, tfwd_ref, trev_ref,
             dq_ref, dk_out, dv_out, dq_a, dq_b, qa, qb, doa, dob, rs, kbuf, vbuf, sems, dk_ref, dv_ref):
        p, i = pl.program_id(0), pl.program_id(1)
        j0 = jmin_ref[p, i]
        first_two = _first_two_up(lambda pair, blk: jmin_ref[pair, blk])
        fetch, kbuf, vbuf = _kv_fetcher(kv_hbm, kbuf, vbuf, sems, KV_SLOTS, col0, bq, p, i, nq, first_two)
        is_a = lax.broadcasted_iota(jnp.int32, (bq, LANES), 1) < HEAD_DIM

        @pl.when(i == 0)
        def _():
            dk_ref[...] = jnp.zeros_like(dk_ref)
            dv_ref[...] = jnp.zeros_like(dv_ref)

        dq_a[...] = jnp.zeros_like(dq_a)
        dq_b[...] = jnp.zeros_like(dq_b)
        rs[...] = jnp.zeros_like(rs)
        st_v = st_ref[0]
        for h in range(2):
            rs[6 + 2 * h], rs[7 + 2 * h] = _col(st_v, h), _col(st_v, 2 + h)
        qa[...], qb[...] = _masked_pair(q_ref[...], is_a, SCALE)
        doa[...], dob[...] = _masked_pair(do_ref[...], is_a)

        def tiles(blocks):
            hs, qs, dos, dqs = (0, 1), (qa, qb), (doa, dob), (dq_a, dq_b)
            tfwd_m, trev_m = tfwd_ref[...], trev_ref[...]
            kv = [(kbuf[s], vbuf[s]) for _, s, _ in blocks]
            nb = len(blocks)
            bh = [(b, h) for b in range(nb) for h in hs]
            tri = lax.broadcasted_iota(jnp.int32, (bq, bq), 0) > lax.broadcasted_iota(jnp.int32, (bq, bq), 1)

            def mask(x, b):
                return jnp.where(tri, x, 0.0) if blocks[b][2] else x

            z = {(b, h): _dot(qs[h][...], kv[b][0], _NT) for b, h in bh}
            dw = {(b, h): _dot(dos[h][...], kv[b][1], _NT) for b, h in bh}
            lk = {(b, h): mask(-_softplus(z[b, h]), b) for b, h in bh}
            suf = {(b, h): _split_dot(lk[b, h], trev_m) for b, h in bh}
            tot = {(b, h): jnp.sum(lk[b, h], axis=1, keepdims=True) for b, h in bh}
            pre = {}
            for h in hs:
                run = (rs[3 * h], rs[3 * h + 1])
                for b in range(nb):
                    run = _two_sum(run[0], run[1], tot[b, h])
                    pre[b, h] = run
            right = {(b, h): (rs[6 + 2 * h] - pre[b, h][0]) + (rs[7 + 2 * h] - pre[b, h][1]) for b, h in bh}
            w = {(b, h): mask(jnp.exp(z[b, h] + suf[b, h] + right[b, h]), b) for b, h in bh}
            g = {(b, h): dw[b, h] * w[b, h] for b, h in bh}
            gpre = {(b, h): _split_dot(g[b, h], tfwd_m) for b, h in bh}
            gtot = {(b, h): jnp.sum(g[b, h], axis=1, keepdims=True) for b, h in bh}
            gleft = {}
            for h in hs:
                run = rs[3 * h + 2]
                for b in range(nb):
                    gleft[b, h] = run
                    run = run + gtot[b, h]
                gleft[nb, h] = run
            dz = {(b, h): mask(g[b, h] - jnp.exp(z[b, h] + lk[b, h]) * (gpre[b, h] + gleft[b, h]), b) for b, h in bh}
            dzb = {(b, h): dz[b, h].astype(BF16) for b, h in bh}
            wb = {(b, h): w[b, h].astype(BF16) for b, h in bh}
            dqc = {(b, h): _dot(dzb[b, h], kv[b][0]) for b, h in bh}
            dkc = {(b, h): _dot(dzb[b, h], qs[h][...], _TN) for b, h in bh}
            dvc = {(b, h): _dot(wb[b, h], dos[h][...], _TN) for b, h in bh}
            for h in hs:
                rs[3 * h], rs[3 * h + 1] = pre[nb - 1, h]
                rs[3 * h + 2] = gleft[nb, h]
                dqs[h][...] += sum([dqc[b, h] for b in range(1, nb)], dqc[0, h])
            for b, (j, _, _) in enumerate(blocks):
                rows = pl.ds(pl.multiple_of(j * bq, bq), bq)
                dk_ref[rows, :] += dkc[b, 0] + dkc[b, 1]
                dv_ref[rows, :] += dvc[b, 0] + dvc[b, 1]

        def single(j, slot, masked):
            tiles([(j, slot, masked)])

        def wait(j):
            slot = lax.rem(j - j0, KV_SLOTS)
            for cp in fetch(j, slot):
                cp.wait()
            return slot

        _walk_up(fetch, j0, i, i, single, stop=jnp.maximum(i - 1, j0))

        @pl.when(j0 < i)
        def _():
            tiles([(i - 1, wait(i - 1), False), (i, wait(i), True)])

        @pl.when(j0 == i)
        def _():
            tiles([(i, wait(i), True)])

        dq_ref[...] = (jnp.where(is_a, dq_a[...], dq_b[...]) * SCALE).astype(BF16)

        @pl.when(i == nq - 1)
        def _():
            dk_out[...] = dk_ref[...].astype(BF16)
            dv_out[...] = dv_ref[...].astype(BF16)

    grid_spec = pltpu.PrefetchScalarGridSpec(
        num_scalar_prefetch=1, grid=(N_PAIRS, nq),
        in_specs=[pl.BlockSpec((bq, LANES), lambda p, i, jm: (i, col0 + p)),
                  pl.BlockSpec(memory_space=pl.ANY),
                  pl.BlockSpec((bq, LANES), lambda p, i, jm: (i, p)),
                  pl.BlockSpec((1, bq, 8), lambda p, i, jm: (p, i, 0)),
                  pl.BlockSpec((bq, bq), lambda p, i, jm: (0, 0)),
                  pl.BlockSpec((bq, bq), lambda p, i, jm: (0, 0))],
        out_specs=[pl.BlockSpec((bq, LANES), lambda p, i, jm: (i, p)),
                   pl.BlockSpec((S, LANES), lambda p, i, jm: (0, p)),
                   pl.BlockSpec((S, LANES), lambda p, i, jm: (0, p))],
        scratch_shapes=[pltpu.VMEM((bq, LANES), F32), pltpu.VMEM((bq, LANES), F32)]
        + [pltpu.VMEM((bq, LANES), BF16)] * 4 + [pltpu.VMEM((10, bq, 1), F32)]
        + [pltpu.VMEM((2 * KV_SLOTS, bq, LANES), BF16)] * 2 + [pltpu.SemaphoreType.DMA((2, 2 * KV_SLOTS))]
        + [pltpu.VMEM((S, LANES), F32)] * 2)
    return pl.pallas_call(
        body, name="sb_bwd", grid_spec=grid_spec,
        out_shape=[jax.ShapeDtypeStruct((S, GROUP_W), BF16)] * 3,
        compiler_params=_params(VMEM_BIG),
    )(jmin, proj, proj, do, st, tfwd, trev)


def _walk_up(fetch, j0, diag, last, tile, stop=None):
    ahead = KV_SLOTS - 1
    stop = last + 1 if stop is None else stop

    def start(j):
        @pl.when(j <= last)
        def _():
            for cp in fetch(j, lax.rem(j - j0, KV_SLOTS)):
                cp.start()

    for d in range(2, ahead):
        start(j0 + d)

    def step(j, carry):
        slot = lax.rem(j - j0, KV_SLOTS)
        for cp in fetch(j, slot):
            cp.wait()
        start(j + ahead)
        pl.when(j >= diag)(functools.partial(tile, j, slot, True))
        pl.when(j < diag)(functools.partial(tile, j, slot, False))
        return carry

    lax.fori_loop(j0, stop, step, 0)


def _causal(bq, bk, i, j):
    row = lax.broadcasted_iota(jnp.int32, (bq, bk), 0)
    col = lax.broadcasted_iota(jnp.int32, (bq, bk), 1)
    return col - row <= i * bq - j * bk


def _by_heads(j, first_a, first_b, heads):
    on_a, on_b = j >= first_a, j >= first_b
    pl.when(jnp.logical_and(on_a, on_b))(functools.partial(heads, (0, 1)))
    pl.when(jnp.logical_and(on_a, jnp.logical_not(on_b)))(functools.partial(heads, (0,)))
    pl.when(jnp.logical_and(on_b, jnp.logical_not(on_a)))(functools.partial(heads, (1,)))


def _fox_row_norms(proj, col0, tm):
    S = proj.shape[0]
    tm = min(tm, S)
    head_of = np.arange(GROUP_W) // HEAD_DIM
    he_t = jnp.asarray((np.arange(2 * N_PAIRS)[:, None] == head_of[None, :]).astype(np.float32), BF16)

    def body(q_ref, k_ref, he_ref, qn_ref, kn_ref, d_ref):
        q, k, he = q_ref[...].astype(F32), k_ref[...].astype(F32), he_ref[...]

        def head_sums_t(x):
            hi = x.astype(BF16)
            lo = (x - hi.astype(F32)).astype(BF16)
            return _dot(he, hi, _NT) + _dot(he, lo, _NT)

        qn_ref[...] = jnp.sqrt(head_sums_t(q * q))
        kn_ref[...] = jnp.sqrt(head_sums_t(k * k))
        d_ref[...] = SCALE * head_sums_t(q * k)

    wide = GROUP_W // LANES
    return pl.pallas_call(
        body, name="fox_row_norms", grid=(S // tm,),
        in_specs=[pl.BlockSpec((tm, GROUP_W), lambda i: (i, col0 // wide)),
                  pl.BlockSpec((tm, GROUP_W), lambda i: (i, (col0 + 4) // wide)),
                  pl.BlockSpec((2 * N_PAIRS, GROUP_W), lambda i: (0, 0))],
        out_specs=[pl.BlockSpec((2 * N_PAIRS, tm), lambda i: (0, i))] * 3,
        out_shape=[jax.ShapeDtypeStruct((2 * N_PAIRS, S), F32)] * 3)(proj, proj, he_t)


def _fox_start_blocks(qn, kn, d, c, bq, bk):
    nh, S = c.shape
    nq, nk = S // bq, S // bk
    top = SCALE * qn * kn.max(axis=1, keepdims=True) - d + c
    top = top.reshape(nh, nq, bq).max(axis=2)
    c_last = c[:, bk - 1::bk]
    live = top[:, :, None] - c_last[:, None, :] >= -FOX_SKIP

    def first_block(lv):
        first = jnp.where(lv.any(axis=2), jnp.argmax(lv, axis=2), nk)
        return jnp.minimum(first, (bq // bk) * jnp.arange(nq)[None, :]).astype(jnp.int32)

    return jnp.concatenate([first_block(live.reshape(N_PAIRS, 2, nq, nk).any(axis=1)), first_block(live)], axis=0)


def _fox_fwd(proj, col0, c_col, c_row, jstart, bq, bk):
    S = proj.shape[0]
    nq, per = S // bq, bq // bk

    def body(js_ref, q_ref, kv_hbm, cc_ref, cr_ref, o_ref, st_ref, acc_a, acc_b, qa, qb, ml, kbuf, vbuf, sems):
        p, i = pl.program_id(0), pl.program_id(1)
        j0 = js_ref[p, i]
        first_two = _first_two_up(lambda pair, blk: js_ref[pair, blk], per)
        fetch, kbuf, vbuf = _kv_fetcher(kv_hbm, kbuf, vbuf, sems, KV_SLOTS, col0, bk, p, i, nq, first_two)
        is_a = lax.broadcasted_iota(jnp.int32, (bq, LANES), 1) < HEAD_DIM
        acc_a[...] = jnp.zeros_like(acc_a)
        acc_b[...] = jnp.zeros_like(acc_b)
        ml[0] = jnp.full((bq, 1), NEG_BIG, F32)
        ml[2] = jnp.full((bq, 1), NEG_BIG, F32)
        ml[1] = jnp.zeros((bq, 1), F32)
        ml[3] = jnp.zeros((bq, 1), F32)
        cc = cc_ref[0]
        ml[4], ml[5] = _col(cc, 0), _col(cc, 1)
        qa[...], qb[...] = _masked_pair(q_ref[...], is_a, SCALE)

        def tile(j, slot, masked):
            k, v = kbuf[slot], vbuf[slot]
            cols = pl.ds(pl.multiple_of(j * bk, bk), bk)
            if masked:
                tri = _causal(bq, bk, i, j)

            def heads(hs):
                qs, accs = (qa, qb), (acc_a, acc_b)
                s = {h: _dot(qs[h][...], k, _NT) - cr_ref[0, pl.ds(h, 1), cols] for h in hs}
                if masked:
                    s = {h: jnp.where(tri, s[h], NEG_BIG) for h in hs}
                top = {h: jnp.max(s[h], axis=1, keepdims=True) for h in hs}
                m_new = {h: jnp.maximum(ml[2 * h], top[h] + ml[4 + h]) for h in hs}
                a = {h: jnp.exp(ml[2 * h] - m_new[h]) for h in hs}
                pr = {h: jnp.exp(s[h] - (m_new[h] - ml[4 + h])) for h in hs}
                tot = {h: jnp.sum(pr[h], axis=1, keepdims=True) for h in hs}
                pv = {h: _dot(pr[h].astype(BF16), v) for h in hs}
                for h in hs:
                    ml[2 * h] = m_new[h]
                    ml[2 * h + 1] = a[h] * ml[2 * h + 1] + tot[h]
                    accs[h][...] = a[h] * accs[h][...] + pv[h]

            _by_heads(j, js_ref[N_PAIRS + 2 * p, i], js_ref[N_PAIRS + 2 * p + 1, i], heads)

        _walk_up(fetch, j0, per * i, per * i + per - 1, tile)
        o_ref[...] = jnp.where(is_a, acc_a[...] / ml[1], acc_b[...] / ml[3])
        lane8 = lax.broadcasted_iota(jnp.int32, (bq, 8), 1)
        st = jnp.where(lane8 == 0, ml[0] + jnp.log(ml[1]), 0.0)
        st_ref[0] = jnp.where(lane8 == 1, ml[2] + jnp.log(ml[3]), st)

    grid_spec = pltpu.PrefetchScalarGridSpec(
        num_scalar_prefetch=1, grid=(N_PAIRS, nq),
        in_specs=[pl.BlockSpec((bq, LANES), lambda p, i, js: (i, col0 + p)),
                  pl.BlockSpec(memory_space=pl.ANY),
                  pl.BlockSpec((1, bq, 8), lambda p, i, js: (p, i, 0)),
                  pl.BlockSpec((1, 8, S), lambda p, i, js: (p, 0, 0))],
        out_specs=[pl.BlockSpec((bq, LANES), lambda p, i, js: (i, p)),
                   pl.BlockSpec((1, bq, 8), lambda p, i, js: (p, i, 0))],
        scratch_shapes=[pltpu.VMEM((bq, LANES), F32), pltpu.VMEM((bq, LANES), F32),
                        pltpu.VMEM((bq, LANES), BF16), pltpu.VMEM((bq, LANES), BF16),
                        pltpu.VMEM((6, bq, 1), F32),
                        pltpu.VMEM((2 * KV_SLOTS, bk, LANES), BF16), pltpu.VMEM((2 * KV_SLOTS, bk, LANES), BF16),
                        pltpu.SemaphoreType.DMA((2, 2 * KV_SLOTS))])
    return pl.pallas_call(
        body, name="fox_fwd", grid_spec=grid_spec,
        out_shape=[jax.ShapeDtypeStruct((S, GROUP_W), F32), jax.ShapeDtypeStruct((N_PAIRS, S, 8), F32)],
    )(jstart, proj, proj, c_col, c_row)


def _fox_bwd(proj, col0, do, o, st, c_col, c_row, jstart, bq, bk):
    S = proj.shape[0]
    nq, per = S // bq, bq // bk

    def body(js_ref, q_ref, kv_hbm, do_ref, o_ref, st_ref, cc_ref, cr_ref,
             dq_ref, dk_out, dv_out, dc_ref, dq_a, dq_b, qa, qb, doa, dob, dd, kbuf, vbuf, sems, dk_ref, dv_ref):
        p, i = pl.program_id(0), pl.program_id(1)
        j0 = js_ref[p, i]
        first_two = _first_two_up(lambda pair, blk: js_ref[pair, blk], per)
        fetch, kbuf, vbuf = _kv_fetcher(kv_hbm, kbuf, vbuf, sems, KV_SLOTS, col0, bk, p, i, nq, first_two)
        is_a = lax.broadcasted_iota(jnp.int32, (bq, LANES), 1) < HEAD_DIM

        @pl.when(i == 0)
        def _():
            dk_ref[...] = jnp.zeros_like(dk_ref)
            dv_ref[...] = jnp.zeros_like(dv_ref)
            dc_ref[...] = jnp.zeros_like(dc_ref)

        dq_a[...] = jnp.zeros_like(dq_a)
        dq_b[...] = jnp.zeros_like(dq_b)
        qa[...], qb[...] = _masked_pair(q_ref[...], is_a, SCALE)
        dov = do_ref[...]
        doa[...], dob[...] = _masked_pair(dov, is_a)
        prod = dov * o_ref[...]
        dd[0] = jnp.sum(jnp.where(is_a, prod, 0.0), axis=1, keepdims=True)
        dd[1] = jnp.sum(jnp.where(is_a, 0.0, prod), axis=1, keepdims=True)
        dd[2] = jnp.zeros((bq, 1), F32)
        dd[3] = jnp.zeros((bq, 1), F32)
        cc, st_v = cc_ref[0], st_ref[0]
        dd[4], dd[5] = _col(cc, 0) - _col(st_v, 0), _col(cc, 1) - _col(st_v, 1)

        def tile(j, slot, masked):
            k, v = kbuf[slot], vbuf[slot]
            if masked:
                tri = _causal(bq, bk, i, j)
            cols = pl.ds(pl.multiple_of(j * bk, bk), bk)

            def heads(hs):
                qs, dos, dqs = (qa, qb), (doa, dob), (dq_a, dq_b)
                z = {h: _dot(qs[h][...], k, _NT) for h in hs}
                dp = {h: _dot(dos[h][...], v, _NT) for h in hs}
                pr = {h: jnp.exp(z[h] - cr_ref[0, pl.ds(h, 1), cols] + dd[4 + h]) for h in hs}
                if masked:
                    pr = {h: jnp.where(tri, pr[h], 0.0) for h in hs}
                ds = {h: pr[h] * (dp[h] - dd[h]) for h in hs}
                csum = {h: jnp.sum(ds[h], axis=0, keepdims=True) for h in hs}
                rsum = {h: jnp.sum(ds[h], axis=1, keepdims=True) for h in hs}
                dsb = {h: ds[h].astype(BF16) for h in hs}
                prb = {h: pr[h].astype(BF16) for h in hs}
                dqc = {h: _dot(dsb[h], k) for h in hs}
                dkc = [_dot(dsb[h], qs[h][...], _TN) for h in hs]
                dvc = [_dot(prb[h], dos[h][...], _TN) for h in hs]
                for h in hs:
                    dc_ref[0, pl.ds(h, 1), cols] -= csum[h]
                    dd[2 + h] += rsum[h]
                    dqs[h][...] += dqc[h]
                dk_ref[cols, :] += sum(dkc[1:], dkc[0])
                dv_ref[cols, :] += sum(dvc[1:], dvc[0])

            _by_heads(j, js_ref[N_PAIRS + 2 * p, i], js_ref[N_PAIRS + 2 * p + 1, i], heads)

        _walk_up(fetch, j0, per * i, per * i + per - 1, tile)
        dq_ref[...] = (jnp.where(is_a, dq_a[...], dq_b[...]) * SCALE).astype(BF16)
        eye = lax.broadcasted_iota(jnp.int32, (bq, bq), 0) == lax.broadcasted_iota(jnp.int32, (bq, bq), 1)
        own = pl.ds(pl.multiple_of(i * bq, bq), bq)
        for h in range(2):
            dc_ref[0, pl.ds(h, 1), own] += jnp.sum(jnp.where(eye, dd[2 + h], 0.0), axis=0, keepdims=True)

        @pl.when(i == nq - 1)
        def _():
            dk_out[...] = dk_ref[...].astype(BF16)
            dv_out[...] = dv_ref[...].astype(BF16)

    grid_spec = pltpu.PrefetchScalarGridSpec(
        num_scalar_prefetch=1, grid=(N_PAIRS, nq),
        in_specs=[pl.BlockSpec((bq, LANES), lambda p, i, js: (i, col0 + p)),
                  pl.BlockSpec(memory_space=pl.ANY),
                  pl.BlockSpec((bq, LANES), lambda p, i, js: (i, p)),
                  pl.BlockSpec((bq, LANES), lambda p, i, js: (i, p)),
                  pl.BlockSpec((1, bq, 8), lambda p, i, js: (p, i, 0)),
                  pl.BlockSpec((1, bq, 8), lambda p, i, js: (p, i, 0)),
                  pl.BlockSpec((1, 8, S), lambda p, i, js: (p, 0, 0))],
        out_specs=[pl.BlockSpec((bq, LANES), lambda p, i, js: (i, p)),
                   pl.BlockSpec((S, LANES), lambda p, i, js: (0, p)),
                   pl.BlockSpec((S, LANES), lambda p, i, js: (0, p)),
                   pl.BlockSpec((1, 8, S), lambda p, i, js: (p, 0, 0))],
        scratch_shapes=[pltpu.VMEM((bq, LANES), F32), pltpu.VMEM((bq, LANES), F32)]
        + [pltpu.VMEM((bq, LANES), BF16)] * 4 + [pltpu.VMEM((6, bq, 1), F32)]
        + [pltpu.VMEM((2 * KV_SLOTS, bk, LANES), BF16)] * 2 + [pltpu.SemaphoreType.DMA((2, 2 * KV_SLOTS))]
        + [pltpu.VMEM((S, LANES), F32)] * 2)
    return pl.pallas_call(
        body, name="fox_bwd", grid_spec=grid_spec,
        out_shape=[jax.ShapeDtypeStruct((S, GROUP_W), BF16)] * 3 + [jax.ShapeDtypeStruct((N_PAIRS, 8, S), F32)],
        compiler_params=_params(VMEM_BIG),
    )(jstart, proj, proj, do, o, st, c_col, c_row)


_HBM = pl.BlockSpec(memory_space=pltpu.HBM)


def _coords():
    return lax.axis_index("x"), lax.axis_index("y"), lax.axis_index("c")


def _gather_copies(ins, outs, send_sems, recv_sems, loc_sems):
    n = len(ins)
    x, y, c = _coords()
    mine = 2 * x + y
    chips = [(1 - x, y), (x, 1 - y), (1 - x, 1 - y)]

    def copy(w, r, slab, to):
        return pltpu.make_async_remote_copy(
            src_ref=ins[w], dst_ref=outs[w].at[slab], send_sem=send_sems.at[3 * w + r],
            recv_sem=recv_sems.at[3 * w + r], device_id=to, device_id_type=MESH)

    def own():
        local = [pltpu.make_async_copy(ins[w], outs[w].at[mine], loc_sems.at[w]) for w in range(n)]
        return local, [copy(w, r, mine, (cx, cy, c)) for w in range(n) for r, (cx, cy) in enumerate(chips)]

    def start():
        local, sends = own()
        for cp in local + sends:
            cp.start()

    def wait():
        local, sends = own()
        for w in range(n):
            for r, (cx, cy) in enumerate(chips):
                copy(w, r, 2 * cx + cy, (cx, cy, c)).wait_recv()
        for cp in sends:
            cp.wait_send()
        for cp in local:
            cp.wait()

    return start, wait


def _gather_shapes(shards):
    n = len(shards)
    return ([jax.ShapeDtypeStruct((4,) + s.shape, s.dtype) for s in shards],
            [pltpu.SemaphoreType.DMA((3 * n,)), pltpu.SemaphoreType.DMA((3 * n,)), pltpu.SemaphoreType.DMA((n,))])


def _allgather_chips(shards):
    n = len(shards)

    def body(*refs):
        start, wait = _gather_copies(refs[:n], refs[n:2 * n], *refs[2 * n:])
        start()
        wait()

    out_shape, sems = _gather_shapes(shards)
    return pl.pallas_call(body, name="allgather_weights", in_specs=[_HBM] * n, out_specs=[_HBM] * n,
                          out_shape=out_shape, scratch_shapes=sems)(*shards)


def _exchange_copies(ins, outs, send_sems, recv_sems, loc_sems, per_chip, parts):
    n = len(parts)
    half = [p.shape[1] // 2 for p in parts] if per_chip else None
    x, y, c = _coords()
    me = 4 * x + 2 * y + c
    peers = [(x ^ fx, y ^ fy, c ^ fc) for fx in (0, 1) for fy in (0, 1) for fc in (0, 1)][1:]

    def src(w, dev):
        if not per_chip:
            return ins[w]
        return ins[w].at[2 * dev[0] + dev[1], pl.ds(pl.multiple_of(dev[2] * half[w], 16), half[w]), :]

    def copy(w, r, source, slab, to):
        return pltpu.make_async_remote_copy(
            src_ref=source, dst_ref=outs[w].at[slab], send_sem=send_sems.at[7 * w + r],
            recv_sem=recv_sems.at[7 * w + r], device_id=to, device_id_type=MESH)

    def own():
        local = [pltpu.make_async_copy(src(w, (x, y, c)), outs[w].at[me], loc_sems.at[w]) for w in range(n)]
        return local, [copy(w, r, src(w, dev), me, dev) for w in range(n) for r, dev in enumerate(peers)]

    def start():
        local, sends = own()
        for cp in local + sends:
            cp.start()

    def wait():
        local, sends = own()
        for w in range(n):
            for r, dev in enumerate(peers):
                copy(w, r, src(w, dev), 4 * dev[0] + 2 * dev[1] + dev[2], dev).wait_recv()
        for cp in sends:
            cp.wait_send()
        for cp in local:
            cp.wait()

    return start, wait


def _exchange_shapes(parts, per_chip):
    n = len(parts)
    return ([jax.ShapeDtypeStruct((8, p.shape[1] // 2, p.shape[2]) if per_chip else (8,) + p.shape, p.dtype)
             for p in parts],
            [pltpu.SemaphoreType.DMA((7 * n,)), pltpu.SemaphoreType.DMA((7 * n,)), pltpu.SemaphoreType.DMA((n,))])


def _exchange(parts, per_chip):
    n = len(parts)

    def body(*refs):
        start, wait = _exchange_copies(refs[:n], refs[n:2 * n], *refs[2 * n:], per_chip, parts)
        start()
        wait()

    out_shape, sems = _exchange_shapes(parts, per_chip)
    return pl.pallas_call(body, name="exchange_per_chip" if per_chip else "exchange_all",
                          in_specs=[_HBM] * n, out_specs=[_HBM] * n, out_shape=out_shape, scratch_shapes=sems)(*parts)


def _sibling_swap(halves):
    n = len(halves)

    def body(*refs):
        ins, outs = refs[:n], refs[n:2 * n]
        send_sems, recv_sems, loc_sems = refs[2 * n:]
        x, y, c = _coords()

        def rows(w, core):
            rh = halves[w].shape[0]
            return outs[w].at[pl.ds(pl.multiple_of(core * rh, 8), rh), :]

        def copy(w, core):
            return pltpu.make_async_remote_copy(
                src_ref=ins[w], dst_ref=rows(w, core), send_sem=send_sems.at[w], recv_sem=recv_sems.at[w],
                device_id=(x, y, 1 - c), device_id_type=MESH)

        local = [pltpu.make_async_copy(ins[w], rows(w, c), loc_sems.at[w]) for w in range(n)]
        sends = [copy(w, c) for w in range(n)]
        for cp in local + sends:
            cp.start()
        for w in range(n):
            copy(w, 1 - c).wait_recv()
        for cp in sends:
            cp.wait_send()
        for cp in local:
            cp.wait()

    vmem = pl.BlockSpec(memory_space=pltpu.VMEM)
    return pl.pallas_call(
        body, name="sibling_swap", in_specs=[vmem] * n, out_specs=[vmem] * n,
        out_shape=[jax.ShapeDtypeStruct((2 * h.shape[0], h.shape[1]), h.dtype) for h in halves],
        scratch_shapes=[pltpu.SemaphoreType.DMA((n,)), pltpu.SemaphoreType.DMA((n,)), pltpu.SemaphoreType.DMA((n,))],
    )(*halves)


def _adamw(w, g, m, v):
    m = ADAM_B1 * m + (1.0 - ADAM_B1) * g
    v = ADAM_B2 * v + (1.0 - ADAM_B2) * (g * g)
    m_hat = m / (1.0 - ADAM_B1 ** ADAM_STEP)
    v_hat = v / (1.0 - ADAM_B2 ** ADAM_STEP)
    delta = -ADAM_LR * (m_hat / (jnp.sqrt(v_hat) + ADAM_EPS) + ADAM_WD * w)
    return delta, m, v


def _sum_parts(parts, name, tr):
    _, R, C = parts.shape
    assert R % tr == 0

    def body(p_ref, g_ref):
        g = p_ref[0].astype(F32)
        for d in range(1, 8):
            g = g + p_ref[d].astype(F32)
        g_ref[...] = g

    return pl.pallas_call(
        body, name=name, grid=(R // tr,),
        in_specs=[pl.BlockSpec((8, tr, C), lambda i: (0, i, 0))],
        out_specs=pl.BlockSpec((tr, C), lambda i: (i, 0)), out_shape=jax.ShapeDtypeStruct((R, C), F32),
    )(parts)


def _adamw_call(g, w, m, v, name, tr):
    R, C = w.shape
    assert R % tr == 0

    def body(g_ref, w_ref, m_ref, v_ref, d_ref, nm_ref, nv_ref):
        d_ref[...], nm_ref[...], nv_ref[...] = _adamw(w_ref[...], g_ref[...], m_ref[...], v_ref[...])

    tile = pl.BlockSpec((tr, C), lambda i: (i, 0))
    return pl.pallas_call(
        body, name=name, grid=(R // tr,), in_specs=[tile] * 4,
        out_specs=[tile] * 3, out_shape=[jax.ShapeDtypeStruct((R, C), F32)] * 3,
    )(g, w, m, v)


def _sum_adamw_small(parts, w, m, v):
    def body(p_ref, w_ref, m_ref, v_ref, g_ref, d_ref, nm_ref, nv_ref, loss_ref):
        g = p_ref[0]
        for d in range(1, 8):
            g = g + p_ref[d]
        g_ref[...] = g
        d_ref[...], nm_ref[...], nv_ref[...] = _adamw(w_ref[...], g, m_ref[...], v_ref[...])
        row = lax.broadcasted_iota(jnp.int32, g.shape, 0)
        per_row = jnp.sum(jnp.where(row == 6, g, 0.0), axis=1, keepdims=True)
        loss_ref[...] = jnp.zeros((8, LANES), F32) + jnp.sum(per_row, axis=0, keepdims=True)

    return pl.pallas_call(
        body, name="sum_adamw_small",
        out_shape=[jax.ShapeDtypeStruct((8, D_MODEL), F32)] * 4 + [jax.ShapeDtypeStruct((8, LANES), F32)],
    )(parts, w, m, v)


def _pack_small(ln1_g, ln1_b, ln2_g, ln2_b, g_sb, g_fox, b_f):
    row5 = jnp.pad(b_f.reshape(1, N_FOX), ((0, 0), (0, D_MODEL - N_FOX)))
    rows = [ln1_g.reshape(1, -1), ln1_b.reshape(1, -1), ln2_g.reshape(1, -1), ln2_b.reshape(1, -1),
            jnp.concatenate([g_sb.reshape(1, -1), g_fox.reshape(1, -1)], axis=1), row5,
            jnp.zeros((2, D_MODEL), F32)]
    return jnp.concatenate(rows, axis=0)


def _unpack_small(p):
    return {"ln1_g": p[0:1], "ln1_b": p[1:2], "ln2_g": p[2:3], "ln2_b": p[3:4], "g_sb": p[4:5, :GROUP_W],
            "g_fox": p[4:5, GROUP_W:], "b_f": p[5:6, :N_FOX]}


def kernel(x, w_in, b_f, g_sb, g_fox, w_out, ln1_g, ln1_b, ln2_g, ln2_b, w_gate_up, w_down, loss_target, m_w_in, m_b_f, m_g_sb, m_g_fox, m_w_out, m_ln1_g, m_ln1_b, m_ln2_g, m_ln2_b, m_w_gate_up, m_w_down, v_w_in, v_b_f, v_g_sb, v_g_fox, v_w_out, v_ln1_g, v_ln1_b, v_ln2_g, v_ln2_b, v_w_gate_up, v_w_down):
    S = x.shape[1]
    x2 = x.reshape(S, D_MODEL)
    tgt = loss_target.reshape(S, D_MODEL)
    TM = 1024
    TR = 512
    BQ = ATTN_BLOCK
    in_w = w_in.shape[2]
    gu_w = w_gate_up.shape[2]

    shards = [w_in[0].astype(BF16), w_out[0].astype(BF16), w_gate_up[0].astype(BF16), w_down[0].astype(BF16)]
    (wi_s,) = _allgather_chips(shards[:1])
    wi = wi_s.transpose(1, 0, 2).reshape(D_MODEL, 4 * in_w)
    w_sb, w_fx = wi[:, :QKV_W // 2], wi[:, QKV_W // 2:QKV_W]
    wqkv = wi[:, :QKV_W]
    wft = wi[:, QKV_W:].T
    proj = _matmul(x2, wqkv, mode="nn", name="proj", tm=TM, tn=512, tk=D_MODEL, outs=[BF16])
    g_row = jnp.concatenate([g_sb, g_fox], axis=1)
    hid = np.arange(D_MODEL) // HEAD_DIM
    he_np = (hid[:, None] == np.arange(LANES)[None, :]).astype(np.float32)
    he, het = jnp.asarray(he_np, BF16), jnp.asarray(he_np.T, BF16)

    lf = _fgate_fwd(x2, wft, b_f.reshape(N_FOX, 1), TM)
    c = _cumsum_fwd(lf)
    c_pair = c.reshape(N_PAIRS, 2, S)
    c_row = jnp.pad(c_pair, ((0, 0), (0, 6), (0, 0)))
    c_col = jnp.pad(c_pair.transpose(0, 2, 1), ((0, 0), (0, 0), (0, 6)))

    o_sb, st_sb, jmin_sb, wo_s, wgu_s, wd_s = _sb_fwd(proj, 0, BQ, shards[1:])
    wo = wo_s.reshape(D_MODEL, D_MODEL)
    wgu = wgu_s.transpose(1, 0, 2).reshape(D_MODEL, 2 * D_FF)
    wg, wu = wgu[:, :D_FF], wgu[:, D_FF:]
    wd = wd_s.reshape(D_FF, D_MODEL)
    jstart_fx = _fox_start_blocks(*_fox_row_norms(proj, 12, TR), c, BQ, BQ)
    o_fx, st_fx = _fox_fwd(proj, 12, c_col, c_row, jstart_fx, BQ, BQ)

    def attn_post(i, osb_ref, ofx_ref, g_ref, he_ref, het_ref, on_ref):
        o = jnp.concatenate([osb_ref[...], ofx_ref[...]], axis=1)
        ms = _head_sums(o * o, he_ref[...], het_ref[...]) * (1.0 / HEAD_DIM)
        on_ref[...] = (o * lax.rsqrt(ms + RMS_EPS) * g_ref[...]).astype(BF16)

    (on,) = _rowwise(attn_post, "attn_post", S, TR,
                     [(o_sb, "t"), (o_fx, "t"), (g_row, "f"), (he, "f"), (het, "f")],
                     [((S, D_MODEL), BF16, "t")])

    u1 = _matmul(on, wo, mode="nn", name="mix", tm=TM, tn=D_MODEL, tk=D_MODEL, outs=[F32],
                 extras=[(x2, (TM if S >= TM else S, D_MODEL), _tile_ij)],
                 epilogue=lambda acc, xv: (ALPHA * xv + acc,))

    def ln1_fwd(i, u_ref, g_ref, b_ref, h_ref):
        xh, _ = _ln_stats(u_ref[...])
        h_ref[...] = xh * g_ref[...] + b_ref[...]

    (h1,) = _rowwise(ln1_fwd, "ln1_fwd", S, TR, [(u1, "t"), (ln1_g, "f"), (ln1_b, "f")], [((S, D_MODEL), F32, "t")])

    tm_e = TM if S >= TM else S
    n_ff = D_FF // 256

    def gate_up_body(h_ref, wg_ref, wu_ref, g_ref, u_ref, a_ref):
        h = h_ref[...].astype(BF16)
        g, u = _dot(h, wg_ref[...]), _dot(h, wu_ref[...])
        g_ref[...] = g.astype(BF16)
        u_ref[...] = u.astype(BF16)
        a_ref[...] = (g * _sigmoid(g) * u).astype(BF16)

    tm_g = min(2 * TM, S)
    ff_tile = pl.BlockSpec((tm_g, 256), lambda i, j: (i, j))
    gate, up, act = pl.pallas_call(
        gate_up_body, name="gate_up_act", grid=(S // tm_g, n_ff),
        in_specs=[pl.BlockSpec((tm_g, D_MODEL), lambda i, j: (i, 0)),
                  pl.BlockSpec((D_MODEL, 256), lambda i, j: (0, j)),
                  pl.BlockSpec((D_MODEL, 256), lambda i, j: (0, j + n_ff))],
        out_specs=[ff_tile] * 3, out_shape=[jax.ShapeDtypeStruct((S, D_FF), BF16)] * 3)(h1, wgu, wgu)

    u2 = _matmul(act, wd, mode="nn", name="ffn_down", tm=TM, tn=D_MODEL, tk=D_FF, outs=[F32],
                 extras=[(h1, (TM if S >= TM else S, D_MODEL), _tile_ij)],
                 epilogue=lambda acc, hv: (ALPHA * hv + acc,))

    def ln2_loss(i, u_ref, t_ref, g_ref, b_ref, du_ref, acc_ref):
        xh, r = _ln_stats(u_ref[...])
        g = g_ref[...]
        err = xh * g + b_ref[...] - t_ref[...]
        dy = err * (1.0 / D_MODEL)
        du_ref[...] = _ln_bwd(dy, xh, r, g)
        _acc_rows(i, acc_ref, {2: jnp.sum(dy * xh, axis=0, keepdims=True), 3: jnp.sum(dy, axis=0, keepdims=True),
                               6: jnp.sum(err * err, axis=0, keepdims=True) * (0.5 / D_MODEL)})

    du2, acc_ln2 = _rowwise(ln2_loss, "ln2_loss", S, TR, [(u2, "t"), (tgt, "t"), (ln2_g, "f"), (ln2_b, "f")],
                            [((S, D_MODEL), F32, "t"), ((8, D_MODEL), F32, "f")])

    d_wd = _matmul(act, du2, mode="tn", name="dw_down", tm=1408, tn=D_MODEL, tk=TM, outs=[BF16])

    def dgu_epilogue(da, g, u):
        g, u = g.astype(F32), u.astype(F32)
        s = _sigmoid(g)
        return da * u * (s * (1.0 + g * (1.0 - s))), da * (g * s)

    dgate, dup = _matmul(du2, wd, mode="nt", name="d_act", tm=tm_g, tn=256, tk=D_MODEL, outs=[BF16, BF16],
                         extras=[(gate, (tm_g, 256), _tile_ij), (up, (tm_g, 256), _tile_ij)],
                         epilogue=dgu_epilogue)
    d_wg = _matmul(h1, dgate, mode="tn", name="dw_gate", tm=D_MODEL, tn=1408, tk=TM, outs=[BF16])
    d_wu = _matmul(h1, dup, mode="tn", name="dw_up", tm=D_MODEL, tn=1408, tk=TM, outs=[BF16])
    d_wgu = jnp.concatenate([d_wg, d_wu], axis=1)
    dh1, got_down = _matmul(dgate, wg, mode="nt", name="dh1_gate", tm=TM, tn=D_MODEL, tk=D_FF, outs=[F32],
                            extras=[(du2, (tm_e, D_MODEL), _tile_ij)], epilogue=lambda acc, e: (ALPHA * e + acc,),
                            hosted=[d_wd.reshape(4, D_FF // 4, D_MODEL)])
    dh1, got_gu = _matmul(dup, wu, mode="nt", name="dh1_up", tm=TM, tn=D_MODEL, tk=D_FF, outs=[F32],
                          extras=[(dh1, (tm_e, D_MODEL), _tile_ij)], epilogue=lambda acc, e: (e + acc,),
                          hosted=[d_wgu.reshape(D_MODEL, 4, gu_w).transpose(1, 0, 2)])

    def ln1_bwd(i, dh_ref, u_ref, g_ref, du_ref, acc_ref):
        xh, r = _ln_stats(u_ref[...])
        dh = dh_ref[...]
        du_ref[...] = _ln_bwd(dh, xh, r, g_ref[...])
        _acc_rows(i, acc_ref, {0: jnp.sum(dh * xh, axis=0, keepdims=True), 1: jnp.sum(dh, axis=0, keepdims=True)})

    du1, acc_ln1 = _rowwise(ln1_bwd, "ln1_bwd", S, TR, [(dh1, "t"), (u1, "t"), (ln1_g, "f")],
                            [((S, D_MODEL), F32, "t"), ((8, D_MODEL), F32, "f")])
    d_wo = _matmul(on, du1, mode="tn", name="dw_out", tm=D_MODEL, tn=D_MODEL, tk=TM, outs=[BF16])
    don, got_out = _matmul(du1, wo, mode="nt", name="d_on", tm=TM, tn=D_MODEL, tk=D_MODEL, outs=[F32],
                           hosted=[d_wo.reshape(4, D_MODEL // 4, D_MODEL)])

    def rms_bwd(i, don_ref, osb_ref, ofx_ref, g_ref, he_ref, het_ref, dosb_ref, dofx_ref, acc_ref):
        o = jnp.concatenate([osb_ref[...], ofx_ref[...]], axis=1)
        hev, hetv = he_ref[...], het_ref[...]
        r = lax.rsqrt(_head_sums(o * o, hev, hetv) * (1.0 / HEAD_DIM) + RMS_EPS)
        dn = don_ref[...]
        dg = dn * g_ref[...]
        do = r * dg - o * (r * r * r) * (_head_sums(dg * o, hev, hetv) * (1.0 / HEAD_DIM))
        dosb_ref[...] = do[:, :GROUP_W]
        dofx_ref[...] = do[:, GROUP_W:]
        _acc_rows(i, acc_ref, {4: jnp.sum(dn * o * r, axis=0, keepdims=True)})

    do_sb, do_fx, acc_rms = _rowwise(
        rms_bwd, "rms_bwd", S, TR, [(don, "t"), (o_sb, "t"), (o_fx, "t"), (g_row, "f"), (he, "f"), (het, "f")],
        [((S, GROUP_W), F32, "t"), ((S, GROUP_W), F32, "t"), ((8, D_MODEL), F32, "f")])

    dq_sb, dk_sb, dv_sb = _sb_bwd(proj, 0, do_sb, st_sb, jmin_sb, BQ)
    jstart_fx2 = jnp.minimum(jstart_fx[:, 0::2], jstart_fx[:, 1::2])
    dq_fx, dk_fx, dv_fx, dc = _fox_bwd(proj, 12, do_fx, o_fx, st_fx, c_col, c_row, jstart_fx2, 2 * BQ, BQ)
    dfl, dbf = _fgate_bwd(dc[:, :2, :].reshape(N_FOX, S), lf)
    dp_sb = jnp.concatenate([dq_sb, dk_sb, dv_sb], axis=1)
    dp_fx = jnp.concatenate([dq_fx, dk_fx, dv_fx], axis=1)

    d_wsb = _matmul(x2, dp_sb, mode="tn", name="dw_in_sb", tm=D_MODEL, tn=QKV_W // 2, tk=TM, outs=[BF16])
    d_wfx = _matmul(x2, dp_fx, mode="tn", name="dw_in_fx", tm=D_MODEL, tn=QKV_W // 2, tk=TM, outs=[BF16])
    d_wft = _matmul(dfl, x2, mode="nn", name="dw_in_f", tm=N_FOX, tn=D_MODEL, tk=TM, outs=[BF16])
    d_wi = jnp.concatenate([d_wsb, d_wfx, d_wft.T], axis=1)
    dx, got_in = _matmul(dp_sb, w_sb, mode="nt", name="dx_sb", tm=TM, tn=D_MODEL, tk=QKV_W // 2, outs=[F32],
                         extras=[(du1, (tm_e, D_MODEL), _tile_ij)], epilogue=lambda acc, e: (ALPHA * e + acc,),
                         hosted=[d_wi.reshape(D_MODEL, 4, in_w).transpose(1, 0, 2)])
    dx = _matmul(dp_fx, w_fx, mode="nt", name="dx_fx", tm=TM, tn=D_MODEL, tk=QKV_W // 2, outs=[F32],
                 extras=[(dx, (tm_e, D_MODEL), _tile_ij)], epilogue=lambda acc, e: (e + acc,))
    dx = _matmul(dfl, wft, mode="tn", name="dx_f", tm=TM, tn=D_MODEL, tk=N_FOX, outs=[F32],
                 extras=[(dx, (tm_e, D_MODEL), _tile_ij)], epilogue=lambda acc, e: (e + acc,))

    got = [got_in, got_out, got_gu, got_down]
    big_names = ("w_in", "w_out", "w_gate_up", "w_down")
    halves = [_sum_parts(p, "sum_" + nm, tr) for nm, p, tr in zip(big_names, got, (256, 128, 128, 176))]
    grads = _sibling_swap(halves)
    big = {}
    for nm, g, w, m, v, tr in zip(big_names, grads, (w_in, w_out, w_gate_up, w_down),
                                  (m_w_in, m_w_out, m_w_gate_up, m_w_down),
                                  (v_w_in, v_w_out, v_w_gate_up, v_w_down), (256, 256, 256, 176)):
        big[nm] = [r[None] for r in [g] + list(_adamw_call(g, w[0], m[0], v[0], "adamw_" + nm, tr))]

    small = acc_ln2 + acc_ln1 + acc_rms
    small = small + jnp.pad(dbf.reshape(1, N_FOX), ((5, 2), (0, D_MODEL - N_FOX)))
    (small_all,) = _exchange([small], False)
    sw = _pack_small(ln1_g, ln1_b, ln2_g, ln2_b, g_sb, g_fox, b_f)
    sm = _pack_small(m_ln1_g, m_ln1_b, m_ln2_g, m_ln2_b, m_g_sb, m_g_fox, m_b_f)
    sv = _pack_small(v_ln1_g, v_ln1_b, v_ln2_g, v_ln2_b, v_g_sb, v_g_fox, v_b_f)
    sg, sd, snm, snv, loss_blk = _sum_adamw_small(small_all, sw, sm, sv)
    sg, sd, snm, snv = _unpack_small(sg), _unpack_small(sd), _unpack_small(snm), _unpack_small(snv)

    names = ["w_in", "b_f", "g_sb", "g_fox", "w_out", "ln1_g", "ln1_b", "ln2_g", "ln2_b", "w_gate_up", "w_down"]
    outs = [loss_blk[0, 0], dx.reshape(1, S, D_MODEL)]
    for k, table in enumerate((sg, sd, snm, snv)):
        outs += [big[n][k] if n in big else table[n] for n in names]
    return tuple(outs)
```

```python
import functools

import numpy as np
import jax
import jax.numpy as jnp
from jax import lax
from jax.experimental import pallas as pl
from jax.experimental.pallas import tpu as pltpu

F32 = jnp.float32
BF16 = jnp.bfloat16

D_MODEL = 1024
HEAD_DIM = 64
LANES = 128
N_PAIRS = 4
GROUP_W = 512
QKV_W = 3072
D_FF = 2816
N_FOX = 8
ALPHA = 2.0 ** 0.25
LN_EPS = 1e-5
RMS_EPS = 1e-6
SCALE = HEAD_DIM ** -0.5
NEG_BIG = -1e30
FOX_SKIP = 30.0
SB_STOP = -105.0
ADAM_LR, ADAM_B1, ADAM_B2, ADAM_EPS, ADAM_WD, ADAM_STEP = 0.001, 0.9, 0.999, 1e-08, 0.01, 10
KV_SLOTS = 4
SCAN_GROUP = 8
ATTN_BLOCK = 256
VMEM_BIG = 56 * 1024 * 1024
MESH = pl.DeviceIdType.MESH

_NN = (((1,), (0,)), ((), ()))
_NT = (((1,), (1,)), ((), ()))
_TN = (((0,), (0,)), ((), ()))


def _dot(a, b, dims=_NN):
    return lax.dot_general(a, b, dims, preferred_element_type=F32)


def _split_dot(x, t):
    hi = x.astype(BF16)
    lo = (x - hi.astype(F32)).astype(BF16)
    return _dot(hi, t) + _dot(lo, t)


def _softplus(z):
    return jnp.maximum(z, 0.0) + jnp.log1p(jnp.exp(-jnp.abs(z)))


def _sigmoid(x):
    return 0.5 * jnp.tanh(0.5 * x) + 0.5


def _col(v, h):
    lane = lax.broadcasted_iota(jnp.int32, v.shape, 1)
    return jnp.sum(jnp.where(lane == h, v, 0.0), axis=1, keepdims=True)


def _two_sum(hi, lo, b):
    s = hi + b
    bb = s - hi
    err = (hi - (s - bb)) + (b - bb)
    return s, lo + err


def _params(vmem=None):
    return pltpu.CompilerParams(vmem_limit_bytes=vmem) if vmem else None


def _matmul(a, b, *, mode, name, tm, tn, tk, outs, extras=(), epilogue=None, vmem=None, hosted=()):
    if mode == "nn":
        (M, K), (_, N) = a.shape, b.shape
    elif mode == "nt":
        (M, K), (N, _) = a.shape, b.shape
    else:
        (K, M), (_, N) = a.shape, b.shape
    tm, tn, tk = min(tm, M), min(tn, N), min(tk, K)
    assert M % tm == 0 and N % tn == 0 and K % tk == 0, (name, M, N, K, tm, tn, tk)
    nk = K // tk
    dims = {"nn": _NN, "nt": _NT, "tn": _TN}[mode]
    if mode == "tn":
        a_spec = pl.BlockSpec((tk, tm), lambda i, j, k: (k, i))
    else:
        a_spec = pl.BlockSpec((tm, tk), lambda i, j, k: (i, k))
    if mode == "nt":
        b_spec = pl.BlockSpec((tn, tk), lambda i, j, k: (j, k))
    else:
        b_spec = pl.BlockSpec((tk, tn), lambda i, j, k: (k, j))
    ex_specs = [pl.BlockSpec(bs, (lambda i, j, k, f=f: f(i, j))) for (_, bs, f) in extras]
    ne, no, nh = len(extras), len(outs), len(hosted)
    if epilogue is None:
        epilogue = lambda acc: (acc,)
    gi, gj = M // tm, N // tn
    host_shapes, host_sems = _exchange_shapes(hosted, True) if nh else ([], [])

    def body(a_ref, b_ref, *rest):
        ex_refs, host_ins = rest[:ne], rest[ne:ne + nh]
        out_refs, host_outs = rest[ne + nh:ne + nh + no], rest[ne + nh + no:ne + 2 * nh + no]
        acc = rest[ne + 2 * nh + no]
        i, j, k = pl.program_id(0), pl.program_id(1), pl.program_id(2)
        if nh:
            start, wait = _exchange_copies(host_ins, host_outs, *rest[ne + 2 * nh + no + 1:], True, hosted)
            pl.when(jnp.logical_and(jnp.logical_and(i == 0, j == 0), k == 0))(start)

        @pl.when(k == 0)
        def _():
            acc[...] = jnp.zeros_like(acc)

        acc[...] += _dot(a_ref[...].astype(BF16), b_ref[...].astype(BF16), dims)

        @pl.when(k == nk - 1)
        def _():
            res = epilogue(acc[...], *[e[...] for e in ex_refs])
            for r, o in zip(res, out_refs):
                o[...] = r.astype(o.dtype)

        if nh:
            pl.when(jnp.logical_and(jnp.logical_and(i == gi - 1, j == gj - 1), k == nk - 1))(wait)

    res = pl.pallas_call(
        body, name=name, grid=(gi, gj, nk),
        in_specs=[a_spec, b_spec] + ex_specs + [_HBM] * nh,
        out_specs=[pl.BlockSpec((tm, tn), lambda i, j, k: (i, j)) for _ in outs] + [_HBM] * nh,
        out_shape=[jax.ShapeDtypeStruct((M, N), d) for d in outs] + host_shapes,
        scratch_shapes=[pltpu.VMEM((tm, tn), F32)] + host_sems,
        compiler_params=_params(vmem),
    )(a, b, *[e[0] for e in extras], *hosted)
    return res[0] if no + nh == 1 else res


def _tile_ij(i, j):
    return (i, j)


def _rowwise(fn, name, rows, tm, ins, outs, vmem=None):
    tm = min(tm, rows)
    assert rows % tm == 0

    def spec(shape, kind):
        if kind == "t":
            return pl.BlockSpec((tm,) + tuple(shape[1:]), lambda i: (i,) + (0,) * (len(shape) - 1))
        return pl.BlockSpec(tuple(shape), lambda i: (0,) * len(shape))

    def body(*refs):
        fn(pl.program_id(0), *refs)

    return pl.pallas_call(
        body, name=name, grid=(rows // tm,),
        in_specs=[spec(a.shape, k) for a, k in ins],
        out_specs=[spec(s, k) for s, _, k in outs],
        out_shape=[jax.ShapeDtypeStruct(s, d) for s, d, _ in outs],
        compiler_params=_params(vmem),
    )(*[a for a, _ in ins])


def _ln_stats(u):
    mu = jnp.mean(u, axis=-1, keepdims=True)
    d = u - mu
    var = jnp.mean(d * d, axis=-1, keepdims=True)
    r = lax.rsqrt(var + LN_EPS)
    return d * r, r


def _ln_bwd(dh, xh, r, g):
    dxh = dh * g
    m1 = jnp.mean(dxh, axis=-1, keepdims=True)
    m2 = jnp.mean(dxh * xh, axis=-1, keepdims=True)
    return r * (dxh - m1 - xh * m2)


def _acc_rows(i, ref, rows):
    @pl.when(i == 0)
    def _():
        ref[...] = jnp.zeros_like(ref)
    for r, v in rows.items():
        ref[pl.ds(r, 1), :] += v


def _head_sums(v, he, het):
    return _split_dot(_split_dot(v, he), het)


def _fgate_fwd(x, wft, bf_col, tm):
    S = x.shape[0]
    tm = min(tm, S)

    def body(wft_ref, bf_ref, x_ref, lf_ref):
        f = _dot(wft_ref[...], x_ref[...].astype(BF16), _NT) + bf_ref[...]
        lf_ref[...] = -_softplus(-f)

    return pl.pallas_call(
        body, name="fgate_fwd", grid=(S // tm,),
        in_specs=[pl.BlockSpec((N_FOX, D_MODEL), lambda i: (0, 0)), pl.BlockSpec((N_FOX, 1), lambda i: (0, 0)),
                  pl.BlockSpec((tm, D_MODEL), lambda i: (i, 0))],
        out_specs=pl.BlockSpec((N_FOX, tm), lambda i: (0, i)),
        out_shape=jax.ShapeDtypeStruct((N_FOX, S), F32),
    )(wft, bf_col, x)


def _chunk_scan(v, reverse):
    lane = lax.broadcasted_iota(jnp.int32, v.shape, 1)
    sh = 1
    while sh < LANES:
        if reverse:
            v = v + jnp.where(lane < LANES - sh, pltpu.roll(v, LANES - sh, 1), 0.0)
        else:
            v = v + jnp.where(lane >= sh, pltpu.roll(v, sh, 1), 0.0)
        sh *= 2
    return v


def _cumsum_fwd(lf):
    n, S = lf.shape
    nc = S // LANES

    grp = min(SCAN_GROUP, nc)

    def body(lf_ref, c_ref):
        def step(gi, carry):
            sls = [pl.ds(pl.multiple_of((gi * grp + g) * LANES, LANES), LANES) for g in range(grp)]
            vs = [_chunk_scan(lf_ref[:, sl], False) for sl in sls]
            tots = [_col(v, LANES - 1) for v in vs]
            for sl, v, t in zip(sls, vs, tots):
                c_ref[:, sl] = v + carry
                carry = carry + t
            return carry
        lax.fori_loop(0, nc // grp, step, jnp.zeros((n, 1), F32))

    return pl.pallas_call(body, name="cumsum_fwd", out_shape=jax.ShapeDtypeStruct((n, S), F32))(lf)


def _fgate_bwd(dc, lf):
    n, S = dc.shape
    nc = S // LANES

    grp = min(SCAN_GROUP, nc)

    def body(dc_ref, lf_ref, dfl_ref, dbf_ref):
        def step(t, carry):
            car, tot = carry
            gi = nc // grp - 1 - t
            sls = [pl.ds(pl.multiple_of((gi * grp + g) * LANES, LANES), LANES) for g in range(grp)]
            vs = [_chunk_scan(dc_ref[:, sl], True) for sl in sls]
            firsts = [_col(v, 0) for v in vs]
            for sl, v, f in reversed(list(zip(sls, vs, firsts))):
                dfl = (v + car) * (1.0 - jnp.exp(lf_ref[:, sl]))
                dfl_ref[:, sl] = dfl
                tot = tot + jnp.sum(dfl, axis=1, keepdims=True)
                car = car + f
            return car, tot
        _, tot = lax.fori_loop(0, nc // grp, step, (jnp.zeros((n, 1), F32), jnp.zeros((n, 1), F32)))
        dbf_ref[...] = tot

    return pl.pallas_call(body, name="fgate_bwd",
                          out_shape=[jax.ShapeDtypeStruct((n, S), F32), jax.ShapeDtypeStruct((n, 1), F32)])(dc, lf)


def _tri_matrices(b):
    r = np.arange(b)
    tfwd = (r[:, None] <= r[None, :]).astype(np.float32)
    return jnp.asarray(tfwd, BF16), jnp.asarray(tfwd.T, BF16)


def _kv_copies(kv_hbm, kbuf, vbuf, sems, sem0, pair_col, bq, j, slot):
    rows = pl.ds(pl.multiple_of(j * bq, bq), bq)

    def cols(c):
        return pl.ds(pl.multiple_of((pair_col + c) * LANES, LANES), LANES)

    return (pltpu.make_async_copy(kv_hbm.at[rows, cols(4)], kbuf.at[slot], sems.at[0, sem0 + slot]),
            pltpu.make_async_copy(kv_hbm.at[rows, cols(8)], vbuf.at[slot], sems.at[1, sem0 + slot]))


def _first_two_up(first_block, per=1):
    def blocks(pair, blk):
        first = first_block(pair, blk)
        return first, first + 1, first + 1 <= per * blk + per - 1
    return blocks


def _first_two_down(pair, blk):
    return blk, blk - 1, blk > 0


def _start_two(fetch, pair, first, second, has_second, ahead):
    for cp in fetch(first, 0, pair, ahead):
        cp.start()

    @pl.when(has_second)
    def _():
        for cp in fetch(second, 1, pair, ahead):
            cp.start()


def _kv_fetcher(kv_hbm, kbuf, vbuf, sems, ns, col0, bq, p, i, nq, blocks):
    base = lax.rem(p * nq + i, 2) * ns
    own = (kbuf.at[pl.ds(base, ns)], vbuf.at[pl.ds(base, ns)])
    other = (kbuf.at[pl.ds(ns - base, ns)], vbuf.at[pl.ds(ns - base, ns)])

    def fetch(j, slot, pair=p, ahead=False):
        kb, vb = other if ahead else own
        return _kv_copies(kv_hbm, kb, vb, sems, ns - base if ahead else base, col0 + pair, bq, j, slot)

    pl.when(jnp.logical_and(p == 0, i == 0))(lambda: _start_two(fetch, p, *blocks(p, i), False))
    wrap = i == nq - 1

    @pl.when(jnp.logical_not(jnp.logical_and(wrap, p == N_PAIRS - 1)))
    def _():
        pair, blk = jnp.where(wrap, p + 1, p), jnp.where(wrap, 0, i + 1)
        _start_two(fetch, pair, *blocks(pair, blk), True)

    return fetch, own[0], own[1]


def _masked_pair(v, lane_is_a, scale=1.0):
    v = v.astype(F32) * scale
    return jnp.where(lane_is_a, v, 0.0).astype(BF16), jnp.where(lane_is_a, 0.0, v).astype(BF16)


def _sb_fwd(proj, col0, bq, shards=()):
    S = proj.shape[0]
    bq = min(bq, S)
    nq = S // bq
    _, trev = _tri_matrices(bq)
    nh = len(shards)
    gather_shapes, gather_sems = _gather_shapes(shards) if nh else ([], [])

    def body(q_ref, kv_hbm, trev_ref, *rest):
        o_ref, st_ref, jmin_ref = rest[nh:nh + 3]
        acc_a, acc_b, qa, qb, rs, kbuf, vbuf, sems = rest[2 * nh + 3:2 * nh + 11]
        p, i = pl.program_id(0), pl.program_id(1)
        if nh:
            gather_start, gather_wait = _gather_copies(rest[:nh], rest[nh + 3:2 * nh + 3], *rest[2 * nh + 11:])
            pl.when(jnp.logical_and(p == 0, i == 0))(gather_start)
        fetch, kbuf, vbuf = _kv_fetcher(kv_hbm, kbuf, vbuf, sems, 2, col0, bq, p, i, nq, _first_two_down)
        is_a = lax.broadcasted_iota(jnp.int32, (bq, LANES), 1) < HEAD_DIM
        acc_a[...] = jnp.zeros_like(acc_a)
        acc_b[...] = jnp.zeros_like(acc_b)
        rs[...] = jnp.zeros_like(rs)
        qa[...], qb[...] = _masked_pair(q_ref[...], is_a, SCALE)

        def tiles(blocks):
            hs, qs, accs, trev_m = (0, 1), (qa, qb), (acc_a, acc_b), trev_ref[...]
            kv = [(kbuf[s], vbuf[s]) for s, _ in blocks]
            bh = [(b, h) for b in range(len(blocks)) for h in hs]
            tri = lax.broadcasted_iota(jnp.int32, (bq, bq), 0) > lax.broadcasted_iota(jnp.int32, (bq, bq), 1)
            z = {(b, h): _dot(qs[h][...], kv[b][0], _NT) for b, h in bh}
            lk = {(b, h): -_softplus(z[b, h]) for b, h in bh}
            lk = {(b, h): jnp.where(tri, lk[b, h], 0.0) if blocks[b][1] else lk[b, h] for b, h in bh}
            suf = {(b, h): _split_dot(lk[b, h], trev_m) for b, h in bh}
            tot = {(b, h): jnp.sum(lk[b, h], axis=1, keepdims=True) for b, h in bh}
            right = {}
            for h in hs:
                r = rs[2 * h] + rs[2 * h + 1]
                for b in range(len(blocks)):
                    right[b, h] = r
                    r = r + tot[b, h]
            w = {(b, h): jnp.exp(z[b, h] + suf[b, h] + right[b, h]) for b, h in bh}
            w = {(b, h): jnp.where(tri, w[b, h], 0.0) if blocks[b][1] else w[b, h] for b, h in bh}
            pv = {(b, h): _dot(w[b, h].astype(BF16), kv[b][1]) for b, h in bh}
            for h in hs:
                accs[h][...] += sum([pv[b, h] for b in range(1, len(blocks))], pv[0, h])
                hi, lo = rs[2 * h], rs[2 * h + 1]
                for b in range(len(blocks)):
                    hi, lo = _two_sum(hi, lo, tot[b, h])
                rs[2 * h], rs[2 * h + 1] = hi, lo

        def live():
            return (jnp.max(jnp.maximum(rs[0], rs[2])) > SB_STOP).astype(jnp.int32)

        for cp in fetch(i, 0):
            cp.wait()
        pl.when(i == 0)(functools.partial(tiles, [(0, True)]))

        @pl.when(i > 0)
        def _():
            for cp in fetch(i - 1, 1):
                cp.wait()
            tiles([(0, True), (1, False)])

        def step(carry):
            j, _ = carry
            slot = lax.rem(i - j, 2)
            for cp in fetch(j, slot):
                cp.start()
            for cp in fetch(j, slot):
                cp.wait()
            tiles([(slot, False)])
            return j - 1, live()

        j_end, _ = lax.while_loop(lambda c: jnp.logical_and(c[0] >= 0, c[1] > 0), step, (i - 2, live()))
        jmin_ref[p, i] = jnp.maximum(j_end + 1, 0)
        o_ref[...] = jnp.where(is_a, acc_a[...], acc_b[...])
        lane8 = lax.broadcasted_iota(jnp.int32, (bq, 8), 1)
        st = jnp.zeros((bq, 8), F32)
        for c, src in enumerate((0, 2, 1, 3)):
            st = jnp.where(lane8 == c, rs[src], st)
        st_ref[0] = st
        if nh:
            pl.when(jnp.logical_and(p == N_PAIRS - 1, i == nq - 1))(gather_wait)

    return pl.pallas_call(
        body, name="sb_fwd", grid=(N_PAIRS, nq),
        in_specs=[pl.BlockSpec((bq, LANES), lambda p, i: (i, col0 + p)),
                  pl.BlockSpec(memory_space=pl.ANY),
                  pl.BlockSpec((bq, bq), lambda p, i: (0, 0))] + [_HBM] * nh,
        out_specs=[pl.BlockSpec((bq, LANES), lambda p, i: (i, p)),
                   pl.BlockSpec((1, bq, 8), lambda p, i: (p, i, 0)),
                   pl.BlockSpec(memory_space=pltpu.SMEM)] + [_HBM] * nh,
        out_shape=[jax.ShapeDtypeStruct((S, GROUP_W), F32), jax.ShapeDtypeStruct((N_PAIRS, S, 8), F32),
                   jax.ShapeDtypeStruct((N_PAIRS, nq), jnp.int32)] + gather_shapes,
        scratch_shapes=[pltpu.VMEM((bq, LANES), F32), pltpu.VMEM((bq, LANES), F32),
                        pltpu.VMEM((bq, LANES), BF16), pltpu.VMEM((bq, LANES), BF16),
                        pltpu.VMEM((4, bq, 1), F32),
                        pltpu.VMEM((4, bq, LANES), BF16), pltpu.VMEM((4, bq, LANES), BF16),
                        pltpu.SemaphoreType.DMA((2, 4))] + gather_sems,
    )(proj, proj, trev, *shards)


def _sb_bwd(proj, col0, do, st, jmin, bq):
    S = proj.shape[0]
    bq = min(bq, S)
    nq = S // bq
    tfwd, trev = _tri_matrices(bq)

    def body(jmin_ref, q_ref, kv_hbm, do_ref, st_ref, tfwd_ref, trev_ref,
             dq_ref, dk_out, dv_out, dq_a, dq_b, qa, qb, doa, dob, rs, kbuf, vbuf, sems, dk_ref, dv_ref):
        p, i = pl.program_id(0), pl.program_id(1)
        j0 = jmin_ref[p, i]
        first_two = _first_two_up(lambda pair, blk: jmin_ref[pair, blk])
        fetch, kbuf, vbuf = _kv_fetcher(kv_hbm, kbuf, vbuf, sems, KV_SLOTS, col0, bq, p, i, nq, first_two)
        is_a = lax.broadcasted_iota(jnp.int32, (bq, LANES), 1) < HEAD_DIM

        @pl.when(i == 0)
        def _():
            dk_ref[...] = jnp.zeros_like(dk_ref)
            dv_ref[...] = jnp.zeros_like(dv_ref)

        dq_a[...] = jnp.zeros_like(dq_a)
        dq_b[...] = jnp.zeros_like(dq_b)
        rs[...] = jnp.zeros_like(rs)
        st_v = st_ref[0]
        for h in range(2):
            rs[6 + 2 * h], rs[7 + 2 * h] = _col(st_v, h), _col(st_v, 2 + h)
        qa[...], qb[...] = _masked_pair(q_ref[...], is_a, SCALE)
        doa[...], dob[...] = _masked_pair(do_ref[...], is_a)

        def tiles(blocks):
            hs, qs, dos, dqs = (0, 1), (qa, qb), (doa, dob), (dq_a, dq_b)
            tfwd_m, trev_m = tfwd_ref[...], trev_ref[...]
            kv = [(kbuf[s], vbuf[s]) for _, s, _ in blocks]
            nb = len(blocks)
            bh = [(b, h) for b in range(nb) for h in hs]
            tri = lax.broadcasted_iota(jnp.int32, (bq, bq), 0) > lax.broadcasted_iota(jnp.int32, (bq, bq), 1)

            def mask(x, b):
                return jnp.where(tri, x, 0.0) if blocks[b][2] else x

            z = {(b, h): _dot(qs[h][...], kv[b][0], _NT) for b, h in bh}
            dw = {(b, h): _dot(dos[h][...], kv[b][1], _NT) for b, h in bh}
            lk = {(b, h): mask(-_softplus(z[b, h]), b) for b, h in bh}
            suf = {(b, h): _split_dot(lk[b, h], trev_m) for b, h in bh}
            tot = {(b, h): jnp.sum(lk[b, h], axis=1, keepdims=True) for b, h in bh}
            pre = {}
            for h in hs:
                run = (rs[3 * h], rs[3 * h + 1])
                for b in range(nb):
                    run = _two_sum(run[0], run[1], tot[b, h])
                    pre[b, h] = run
            right = {(b, h): (rs[6 + 2 * h] - pre[b, h][0]) + (rs[7 + 2 * h] - pre[b, h][1]) for b, h in bh}
            w = {(b, h): mask(jnp.exp(z[b, h] + suf[b, h] + right[b, h]), b) for b, h in bh}
            g = {(b, h): dw[b, h] * w[b, h] for b, h in bh}
            gpre = {(b, h): _split_dot(g[b, h], tfwd_m) for b, h in bh}
            gtot = {(b, h): jnp.sum(g[b, h], axis=1, keepdims=True) for b, h in bh}
            gleft = {}
            for h in hs:
                run = rs[3 * h + 2]
                for b in range(nb):
                    gleft[b, h] = run
                    run = run + gtot[b, h]
                gleft[nb, h] = run
            dz = {(b, h): mask(g[b, h] - jnp.exp(z[b, h] + lk[b, h]) * (gpre[b, h] + gleft[b, h]), b) for b, h in bh}
            dzb = {(b, h): dz[b, h].astype(BF16) for b, h in bh}
            wb = {(b, h): w[b, h].astype(BF16) for b, h in bh}
            dqc = {(b, h): _dot(dzb[b, h], kv[b][0]) for b, h in bh}
            dkc = {(b, h): _dot(dzb[b, h], qs[h][...], _TN) for b, h in bh}
            dvc = {(b, h): _dot(wb[b, h], dos[h][...], _TN) for b, h in bh}
            for h in hs:
                rs[3 * h], rs[3 * h + 1] = pre[nb - 1, h]
                rs[3 * h + 2] = gleft[nb, h]
                dqs[h][...] += sum([dqc[b, h] for b in range(1, nb)], dqc[0, h])
            for b, (j, _, _) in enumerate(blocks):
                rows = pl.ds(pl.multiple_of(j * bq, bq), bq)
                dk_ref[rows, :] += dkc[b, 0] + dkc[b, 1]
                dv_ref[rows, :] += dvc[b, 0] + dvc[b, 1]

        def single(j, slot, masked):
            tiles([(j, slot, masked)])

        def wait(j):
            slot = lax.rem(j - j0, KV_SLOTS)
            for cp in fetch(j, slot):
                cp.wait()
            return slot

        _walk_up(fetch, j0, i, i, single, stop=jnp.maximum(i - 1, j0))

        @pl.when(j0 < i)
        def _():
            tiles([(i - 1, wait(i - 1), False), (i, wait(i), True)])

        @pl.when(j0 == i)
        def _():
            tiles([(i, wait(i), True)])

        dq_ref[...] = (jnp.where(is_a, dq_a[...], dq_b[...]) * SCALE).astype(BF16)

        @pl.when(i == nq - 1)
        def _():
            dk_out[...] = dk_ref[...].astype(BF16)
            dv_out[...] = dv_ref[...].astype(BF16)

    grid_spec = pltpu.PrefetchScalarGridSpec(
        num_scalar_prefetch=1, grid=(N_PAIRS, nq),
        in_specs=[pl.BlockSpec((bq, LANES), lambda p, i, jm: (i, col0 + p)),
                  pl.BlockSpec(memory_space=pl.ANY),
                  pl.BlockSpec((bq, LANES), lambda p, i, jm: (i, p)),
                  pl.BlockSpec((1, bq, 8), lambda p, i, jm: (p, i, 0)),
                  pl.BlockSpec((bq, bq), lambda p, i, jm: (0, 0)),
                  pl.BlockSpec((bq, bq), lambda p, i, jm: (0, 0))],
        out_specs=[pl.BlockSpec((bq, LANES), lambda p, i, jm: (i, p)),
                   pl.BlockSpec((S, LANES), lambda p, i, jm: (0, p)),
                   pl.BlockSpec((S, LANES), lambda p, i, jm: (0, p))],
        scratch_shapes=[pltpu.VMEM((bq, LANES), F32), pltpu.VMEM((bq, LANES), F32)]
        + [pltpu.VMEM((bq, LANES), BF16)] * 4 + [pltpu.VMEM((10, bq, 1), F32)]
        + [pltpu.VMEM((2 * KV_SLOTS, bq, LANES), BF16)] * 2 + [pltpu.SemaphoreType.DMA((2, 2 * KV_SLOTS))]
        + [pltpu.VMEM((S, LANES), F32)] * 2)
    return pl.pallas_call(
        body, name="sb_bwd", grid_spec=grid_spec,
        out_shape=[jax.ShapeDtypeStruct((S, GROUP_W), BF16)] * 3,
        compiler_params=_params(VMEM_BIG),
    )(jmin, proj, proj, do, st, tfwd, trev)


def _walk_up(fetch, j0, diag, last, tile, stop=None):
    ahead = KV_SLOTS - 1
    stop = last + 1 if stop is None else stop

    def start(j):
        @pl.when(j <= last)
        def _():
            for cp in fetch(j, lax.rem(j - j0, KV_SLOTS)):
                cp.start()

    for d in range(2, ahead):
        start(j0 + d)

    def step(j, carry):
        slot = lax.rem(j - j0, KV_SLOTS)
        for cp in fetch(j, slot):
            cp.wait()
        start(j + ahead)
        pl.when(j >= diag)(functools.partial(tile, j, slot, True))
        pl.when(j < diag)(functools.partial(tile, j, slot, False))
        return carry

    lax.fori_loop(j0, stop, step, 0)


def _causal(bq, bk, i, j):
    row = lax.broadcasted_iota(jnp.int32, (bq, bk), 0)
    col = lax.broadcasted_iota(jnp.int32, (bq, bk), 1)
    return col - row <= i * bq - j * bk


def _by_heads(j, first_a, first_b, heads):
    on_a, on_b = j >= first_a, j >= first_b
    pl.when(jnp.logical_and(on_a, on_b))(functools.partial(heads, (0, 1)))
    pl.when(jnp.logical_and(on_a, jnp.logical_not(on_b)))(functools.partial(heads, (0,)))
    pl.when(jnp.logical_and(on_b, jnp.logical_not(on_a)))(functools.partial(heads, (1,)))


def _fox_row_norms(proj, col0, tm):
    S = proj.shape[0]
    tm = min(tm, S)
    head_of = np.arange(GROUP_W) // HEAD_DIM
    he_t = jnp.asarray((np.arange(2 * N_PAIRS)[:, None] == head_of[None, :]).astype(np.float32), BF16)

    def body(q_ref, k_ref, he_ref, qn_ref, kn_ref, d_ref):
        q, k, he = q_ref[...].astype(F32), k_ref[...].astype(F32), he_ref[...]

        def head_sums_t(x):
            hi = x.astype(BF16)
            lo = (x - hi.astype(F32)).astype(BF16)
            return _dot(he, hi, _NT) + _dot(he, lo, _NT)

        qn_ref[...] = jnp.sqrt(head_sums_t(q * q))
        kn_ref[...] = jnp.sqrt(head_sums_t(k * k))
        d_ref[...] = SCALE * head_sums_t(q * k)

    wide = GROUP_W // LANES
    return pl.pallas_call(
        body, name="fox_row_norms", grid=(S // tm,),
        in_specs=[pl.BlockSpec((tm, GROUP_W), lambda i: (i, col0 // wide)),
                  pl.BlockSpec((tm, GROUP_W), lambda i: (i, (col0 + 4) // wide)),
                  pl.BlockSpec((2 * N_PAIRS, GROUP_W), lambda i: (0, 0))],
        out_specs=[pl.BlockSpec((2 * N_PAIRS, tm), lambda i: (0, i))] * 3,
        out_shape=[jax.ShapeDtypeStruct((2 * N_PAIRS, S), F32)] * 3)(proj, proj, he_t)


def _fox_start_blocks(qn, kn, d, c, bq, bk):
    nh, S = c.shape
    nq, nk = S // bq, S // bk
    top = SCALE * qn * kn.max(axis=1, keepdims=True) - d + c
    top = top.reshape(nh, nq, bq).max(axis=2)
    c_last = c[:, bk - 1::bk]
    live = top[:, :, None] - c_last[:, None, :] >= -FOX_SKIP

    def first_block(lv):
        first = jnp.where(lv.any(axis=2), jnp.argmax(lv, axis=2), nk)
        return jnp.minimum(first, (bq // bk) * jnp.arange(nq)[None, :]).astype(jnp.int32)

    return jnp.concatenate([first_block(live.reshape(N_PAIRS, 2, nq, nk).any(axis=1)), first_block(live)], axis=0)


def _fox_fwd(proj, col0, c_col, c_row, jstart, bq, bk):
    S = proj.shape[0]
    nq, per = S // bq, bq // bk

    def body(js_ref, q_ref, kv_hbm, cc_ref, cr_ref, o_ref, st_ref, acc_a, acc_b, qa, qb, ml, kbuf, vbuf, sems):
        p, i = pl.program_id(0), pl.program_id(1)
        j0 = js_ref[p, i]
        first_two = _first_two_up(lambda pair, blk: js_ref[pair, blk], per)
        fetch, kbuf, vbuf = _kv_fetcher(kv_hbm, kbuf, vbuf, sems, KV_SLOTS, col0, bk, p, i, nq, first_two)
        is_a = lax.broadcasted_iota(jnp.int32, (bq, LANES), 1) < HEAD_DIM
        acc_a[...] = jnp.zeros_like(acc_a)
        acc_b[...] = jnp.zeros_like(acc_b)
        ml[0] = jnp.full((bq, 1), NEG_BIG, F32)
        ml[2] = jnp.full((bq, 1), NEG_BIG, F32)
        ml[1] = jnp.zeros((bq, 1), F32)
        ml[3] = jnp.zeros((bq, 1), F32)
        cc = cc_ref[0]
        ml[4], ml[5] = _col(cc, 0), _col(cc, 1)
        qa[...], qb[...] = _masked_pair(q_ref[...], is_a, SCALE)

        def tile(j, slot, masked):
            k, v = kbuf[slot], vbuf[slot]
            cols = pl.ds(pl.multiple_of(j * bk, bk), bk)
            if masked:
                tri = _causal(bq, bk, i, j)

            def heads(hs):
                qs, accs = (qa, qb), (acc_a, acc_b)
                s = {h: _dot(qs[h][...], k, _NT) - cr_ref[0, pl.ds(h, 1), cols] for h in hs}
                if masked:
                    s = {h: jnp.where(tri, s[h], NEG_BIG) for h in hs}
                top = {h: jnp.max(s[h], axis=1, keepdims=True) for h in hs}
                m_new = {h: jnp.maximum(ml[2 * h], top[h] + ml[4 + h]) for h in hs}
                a = {h: jnp.exp(ml[2 * h] - m_new[h]) for h in hs}
                pr = {h: jnp.exp(s[h] - (m_new[h] - ml[4 + h])) for h in hs}
                tot = {h: jnp.sum(pr[h], axis=1, keepdims=True) for h in hs}
                pv = {h: _dot(pr[h].astype(BF16), v) for h in hs}
                for h in hs:
                    ml[2 * h] = m_new[h]
                    ml[2 * h + 1] = a[h] * ml[2 * h + 1] + tot[h]
                    accs[h][...] = a[h] * accs[h][...] + pv[h]

            _by_heads(j, js_ref[N_PAIRS + 2 * p, i], js_ref[N_PAIRS + 2 * p + 1, i], heads)

        _walk_up(fetch, j0, per * i, per * i + per - 1, tile)
        o_ref[...] = jnp.where(is_a, acc_a[...] / ml[1], acc_b[...] / ml[3])
        lane8 = lax.broadcasted_iota(jnp.int32, (bq, 8), 1)
        st = jnp.where(lane8 == 0, ml[0] + jnp.log(ml[1]), 0.0)
        st_ref[0] = jnp.where(lane8 == 1, ml[2] + jnp.log(ml[3]), st)

    grid_spec = pltpu.PrefetchScalarGridSpec(
        num_scalar_prefetch=1, grid=(N_PAIRS, nq),
        in_specs=[pl.BlockSpec((bq, LANES), lambda p, i, js: (i, col0 + p)),
                  pl.BlockSpec(memory_space=pl.ANY),
                  pl.BlockSpec((1, bq, 8), lambda p, i, js: (p, i, 0)),
                  pl.BlockSpec((1, 8, S), lambda p, i, js: (p, 0, 0))],
        out_specs=[pl.BlockSpec((bq, LANES), lambda p, i, js: (i, p)),
                   pl.BlockSpec((1, bq, 8), lambda p, i, js: (p, i, 0))],
        scratch_shapes=[pltpu.VMEM((bq, LANES), F32), pltpu.VMEM((bq, LANES), F32),
                        pltpu.VMEM((bq, LANES), BF16), pltpu.VMEM((bq, LANES), BF16),
                        pltpu.VMEM((6, bq, 1), F32),
                        pltpu.VMEM((2 * KV_SLOTS, bk, LANES), BF16), pltpu.VMEM((2 * KV_SLOTS, bk, LANES), BF16),
                        pltpu.SemaphoreType.DMA((2, 2 * KV_SLOTS))])
    return pl.pallas_call(
        body, name="fox_fwd", grid_spec=grid_spec,
        out_shape=[jax.ShapeDtypeStruct((S, GROUP_W), F32), jax.ShapeDtypeStruct((N_PAIRS, S, 8), F32)],
    )(jstart, proj, proj, c_col, c_row)


def _fox_bwd(proj, col0, do, o, st, c_col, c_row, jstart, bq, bk):
    S = proj.shape[0]
    nq, per = S // bq, bq // bk

    def body(js_ref, q_ref, kv_hbm, do_ref, o_ref, st_ref, cc_ref, cr_ref,
             dq_ref, dk_out, dv_out, dc_ref, dq_a, dq_b, qa, qb, doa, dob, dd, kbuf, vbuf, sems, dk_ref, dv_ref):
        p, i = pl.program_id(0), pl.program_id(1)
        j0 = js_ref[p, i]
        first_two = _first_two_up(lambda pair, blk: js_ref[pair, blk], per)
        fetch, kbuf, vbuf = _kv_fetcher(kv_hbm, kbuf, vbuf, sems, KV_SLOTS, col0, bk, p, i, nq, first_two)
        is_a = lax.broadcasted_iota(jnp.int32, (bq, LANES), 1) < HEAD_DIM

        @pl.when(i == 0)
        def _():
            dk_ref[...] = jnp.zeros_like(dk_ref)
            dv_ref[...] = jnp.zeros_like(dv_ref)
            dc_ref[...] = jnp.zeros_like(dc_ref)

        dq_a[...] = jnp.zeros_like(dq_a)
        dq_b[...] = jnp.zeros_like(dq_b)
        qa[...], qb[...] = _masked_pair(q_ref[...], is_a, SCALE)
        dov = do_ref[...]
        doa[...], dob[...] = _masked_pair(dov, is_a)
        prod = dov * o_ref[...]
        dd[0] = jnp.sum(jnp.where(is_a, prod, 0.0), axis=1, keepdims=True)
        dd[1] = jnp.sum(jnp.where(is_a, 0.0, prod), axis=1, keepdims=True)
        dd[2] = jnp.zeros((bq, 1), F32)
        dd[3] = jnp.zeros((bq, 1), F32)
        cc, st_v = cc_ref[0], st_ref[0]
        dd[4], dd[5] = _col(cc, 0) - _col(st_v, 0), _col(cc, 1) - _col(st_v, 1)

        def tile(j, slot, masked):
            k, v = kbuf[slot], vbuf[slot]
            if masked:
                tri = _causal(bq, bk, i, j)
            cols = pl.ds(pl.multiple_of(j * bk, bk), bk)

            def heads(hs):
                qs, dos, dqs = (qa, qb), (doa, dob), (dq_a, dq_b)
                z = {h: _dot(qs[h][...], k, _NT) for h in hs}
                dp = {h: _dot(dos[h][...], v, _NT) for h in hs}
                pr = {h: jnp.exp(z[h] - cr_ref[0, pl.ds(h, 1), cols] + dd[4 + h]) for h in hs}
                if masked:
                    pr = {h: jnp.where(tri, pr[h], 0.0) for h in hs}
                ds = {h: pr[h] * (dp[h] - dd[h]) for h in hs}
                csum = {h: jnp.sum(ds[h], axis=0, keepdims=True) for h in hs}
                rsum = {h: jnp.sum(ds[h], axis=1, keepdims=True) for h in hs}
                dsb = {h: ds[h].astype(BF16) for h in hs}
                prb = {h: pr[h].astype(BF16) for h in hs}
                dqc = {h: _dot(dsb[h], k) for h in hs}
                dkc = [_dot(dsb[h], qs[h][...], _TN) for h in hs]
                dvc = [_dot(prb[h], dos[h][...], _TN) for h in hs]
                for h in hs:
                    dc_ref[0, pl.ds(h, 1), cols] -= csum[h]
                    dd[2 + h] += rsum[h]
                    dqs[h][...] += dqc[h]
                dk_ref[cols, :] += sum(dkc[1:], dkc[0])
                dv_ref[cols, :] += sum(dvc[1:], dvc[0])

            _by_heads(j, js_ref[N_PAIRS + 2 * p, i], js_ref[N_PAIRS + 2 * p + 1, i], heads)

        _walk_up(fetch, j0, per * i, per * i + per - 1, tile)
        dq_ref[...] = (jnp.where(is_a, dq_a[...], dq_b[...]) * SCALE).astype(BF16)
        eye = lax.broadcasted_iota(jnp.int32, (bq, bq), 0) == lax.broadcasted_iota(jnp.int32, (bq, bq), 1)
        own = pl.ds(pl.multiple_of(i * bq, bq), bq)
        for h in range(2):
            dc_ref[0, pl.ds(h, 1), own] += jnp.sum(jnp.where(eye, dd[2 + h], 0.0), axis=0, keepdims=True)

        @pl.when(i == nq - 1)
        def _():
            dk_out[...] = dk_ref[...].astype(BF16)
            dv_out[...] = dv_ref[...].astype(BF16)

    grid_spec = pltpu.PrefetchScalarGridSpec(
        num_scalar_prefetch=1, grid=(N_PAIRS, nq),
        in_specs=[pl.BlockSpec((bq, LANES), lambda p, i, js: (i, col0 + p)),
                  pl.BlockSpec(memory_space=pl.ANY),
                  pl.BlockSpec((bq, LANES), lambda p, i, js: (i, p)),
                  pl.BlockSpec((bq, LANES), lambda p, i, js: (i, p)),
                  pl.BlockSpec((1, bq, 8), lambda p, i, js: (p, i, 0)),
                  pl.BlockSpec((1, bq, 8), lambda p, i, js: (p, i, 0)),
                  pl.BlockSpec((1, 8, S), lambda p, i, js: (p, 0, 0))],
        out_specs=[pl.BlockSpec((bq, LANES), lambda p, i, js: (i, p)),
                   pl.BlockSpec((S, LANES), lambda p, i, js: (0, p)),
                   pl.BlockSpec((S, LANES), lambda p, i, js: (0, p)),
                   pl.BlockSpec((1, 8, S), lambda p, i, js: (p, 0, 0))],
        scratch_shapes=[pltpu.VMEM((bq, LANES), F32), pltpu.VMEM((bq, LANES), F32)]
        + [pltpu.VMEM((bq, LANES), BF16)] * 4 + [pltpu.VMEM((6, bq, 1), F32)]
        + [pltpu.VMEM((2 * KV_SLOTS, bk, LANES), BF16)] * 2 + [pltpu.SemaphoreType.DMA((2, 2 * KV_SLOTS))]
        + [pltpu.VMEM((S, LANES), F32)] * 2)
    return pl.pallas_call(
        body, name="fox_bwd", grid_spec=grid_spec,
        out_shape=[jax.ShapeDtypeStruct((S, GROUP_W), BF16)] * 3 + [jax.ShapeDtypeStruct((N_PAIRS, 8, S), F32)],
        compiler_params=_params(VMEM_BIG),
    )(jstart, proj, proj, do, o, st, c_col, c_row)


_HBM = pl.BlockSpec(memory_space=pltpu.HBM)


def _coords():
    return lax.axis_index("x"), lax.axis_index("y"), lax.axis_index("c")


def _gather_copies(ins, outs, send_sems, recv_sems, loc_sems):
    n = len(ins)
    x, y, c = _coords()
    mine = 2 * x + y
    chips = [(1 - x, y), (x, 1 - y), (1 - x, 1 - y)]

    def copy(w, r, slab, to):
        return pltpu.make_async_remote_copy(
            src_ref=ins[w], dst_ref=outs[w].at[slab], send_sem=send_sems.at[3 * w + r],
            recv_sem=recv_sems.at[3 * w + r], device_id=to, device_id_type=MESH)

    def own():
        local = [pltpu.make_async_copy(ins[w], outs[w].at[mine], loc_sems.at[w]) for w in range(n)]
        return local, [copy(w, r, mine, (cx, cy, c)) for w in range(n) for r, (cx, cy) in enumerate(chips)]

    def start():
        local, sends = own()
        for cp in local + sends:
            cp.start()

    def wait():
        local, sends = own()
        for w in range(n):
            for r, (cx, cy) in enumerate(chips):
                copy(w, r, 2 * cx + cy, (cx, cy, c)).wait_recv()
        for cp in sends:
            cp.wait_send()
        for cp in local:
            cp.wait()

    return start, wait


def _gather_shapes(shards):
    n = len(shards)
    return ([jax.ShapeDtypeStruct((4,) + s.shape, s.dtype) for s in shards],
            [pltpu.SemaphoreType.DMA((3 * n,)), pltpu.SemaphoreType.DMA((3 * n,)), pltpu.SemaphoreType.DMA((n,))])


def _allgather_chips(shards):
    n = len(shards)

    def body(*refs):
        start, wait = _gather_copies(refs[:n], refs[n:2 * n], *refs[2 * n:])
        start()
        wait()

    out_shape, sems = _gather_shapes(shards)
    return pl.pallas_call(body, name="allgather_weights", in_specs=[_HBM] * n, out_specs=[_HBM] * n,
                          out_shape=out_shape, scratch_shapes=sems)(*shards)


def _exchange_copies(ins, outs, send_sems, recv_sems, loc_sems, per_chip, parts):
    n = len(parts)
    half = [p.shape[1] // 2 for p in parts] if per_chip else None
    x, y, c = _coords()
    me = 4 * x + 2 * y + c
    peers = [(x ^ fx, y ^ fy, c ^ fc) for fx in (0, 1) for fy in (0, 1) for fc in (0, 1)][1:]

    def src(w, dev):
        if not per_chip:
            return ins[w]
        return ins[w].at[2 * dev[0] + dev[1], pl.ds(pl.multiple_of(dev[2] * half[w], 16), half[w]), :]

    def copy(w, r, source, slab, to):
        return pltpu.make_async_remote_copy(
            src_ref=source, dst_ref=outs[w].at[slab], send_sem=send_sems.at[7 * w + r],
            recv_sem=recv_sems.at[7 * w + r], device_id=to, device_id_type=MESH)

    def own():
        local = [pltpu.make_async_copy(src(w, (x, y, c)), outs[w].at[me], loc_sems.at[w]) for w in range(n)]
        return local, [copy(w, r, src(w, dev), me, dev) for w in range(n) for r, dev in enumerate(peers)]

    def start():
        local, sends = own()
        for cp in local + sends:
            cp.start()

    def wait():
        local, sends = own()
        for w in range(n):
            for r, dev in enumerate(peers):
                copy(w, r, src(w, dev), 4 * dev[0] + 2 * dev[1] + dev[2], dev).wait_recv()
        for cp in sends:
            cp.wait_send()
        for cp in local:
            cp.wait()

    return start, wait


def _exchange_shapes(parts, per_chip):
    n = len(parts)
    return ([jax.ShapeDtypeStruct((8, p.shape[1] // 2, p.shape[2]) if per_chip else (8,) + p.shape, p.dtype)
             for p in parts],
            [pltpu.SemaphoreType.DMA((7 * n,)), pltpu.SemaphoreType.DMA((7 * n,)), pltpu.SemaphoreType.DMA((n,))])


def _exchange(parts, per_chip):
    n = len(parts)

    def body(*refs):
        start, wait = _exchange_copies(refs[:n], refs[n:2 * n], *refs[2 * n:], per_chip, parts)
        start()
        wait()

    out_shape, sems = _exchange_shapes(parts, per_chip)
    return pl.pallas_call(body, name="exchange_per_chip" if per_chip else "exchange_all",
                          in_specs=[_HBM] * n, out_specs=[_HBM] * n, out_shape=out_shape, scratch_shapes=sems)(*parts)


def _sibling_swap(halves):
    n = len(halves)

    def body(*refs):
        ins, outs = refs[:n], refs[n:2 * n]
        send_sems, recv_sems, loc_sems = refs[2 * n:]
        x, y, c = _coords()

        def rows(w, core):
            rh = halves[w].shape[0]
            return outs[w].at[pl.ds(pl.multiple_of(core * rh, 8), rh), :]

        def copy(w, core):
            return pltpu.make_async_remote_copy(
                src_ref=ins[w], dst_ref=rows(w, core), send_sem=send_sems.at[w], recv_sem=recv_sems.at[w],
                device_id=(x, y, 1 - c), device_id_type=MESH)

        local = [pltpu.make_async_copy(ins[w], rows(w, c), loc_sems.at[w]) for w in range(n)]
        sends = [copy(w, c) for w in range(n)]
        for cp in local + sends:
            cp.start()
        for w in range(n):
            copy(w, 1 - c).wait_recv()
        for cp in sends:
            cp.wait_send()
        for cp in local:
            cp.wait()

    vmem = pl.BlockSpec(memory_space=pltpu.VMEM)
    return pl.pallas_call(
        body, name="sibling_swap", in_specs=[vmem] * n, out_specs=[vmem] * n,
        out_shape=[jax.ShapeDtypeStruct((2 * h.shape[0], h.shape[1]), h.dtype) for h in halves],
        scratch_shapes=[pltpu.SemaphoreType.DMA((n,)), pltpu.SemaphoreType.DMA((n,)), pltpu.SemaphoreType.DMA((n,))],
    )(*halves)


def _adamw(w, g, m, v):
    m = ADAM_B1 * m + (1.0 - ADAM_B1) * g
    v = ADAM_B2 * v + (1.0 - ADAM_B2) * (g * g)
    m_hat = m / (1.0 - ADAM_B1 ** ADAM_STEP)
    v_hat = v / (1.0 - ADAM_B2 ** ADAM_STEP)
    delta = -ADAM_LR * (m_hat / (jnp.sqrt(v_hat) + ADAM_EPS) + ADAM_WD * w)
    return delta, m, v


def _sum_parts(parts, name, tr):
    _, R, C = parts.shape
    assert R % tr == 0

    def body(p_ref, g_ref):
        g = p_ref[0].astype(F32)
        for d in range(1, 8):
            g = g + p_ref[d].astype(F32)
        g_ref[...] = g

    return pl.pallas_call(
        body, name=name, grid=(R // tr,),
        in_specs=[pl.BlockSpec((8, tr, C), lambda i: (0, i, 0))],
        out_specs=pl.BlockSpec((tr, C), lambda i: (i, 0)), out_shape=jax.ShapeDtypeStruct((R, C), F32),
    )(parts)


def _adamw_call(g, w, m, v, name, tr):
    R, C = w.shape
    assert R % tr == 0

    def body(g_ref, w_ref, m_ref, v_ref, d_ref, nm_ref, nv_ref):
        d_ref[...], nm_ref[...], nv_ref[...] = _adamw(w_ref[...], g_ref[...], m_ref[...], v_ref[...])

    tile = pl.BlockSpec((tr, C), lambda i: (i, 0))
    return pl.pallas_call(
        body, name=name, grid=(R // tr,), in_specs=[tile] * 4,
        out_specs=[tile] * 3, out_shape=[jax.ShapeDtypeStruct((R, C), F32)] * 3,
    )(g, w, m, v)


def _sum_adamw_small(parts, w, m, v):
    def body(p_ref, w_ref, m_ref, v_ref, g_ref, d_ref, nm_ref, nv_ref, loss_ref):
        g = p_ref[0]
        for d in range(1, 8):
            g = g + p_ref[d]
        g_ref[...] = g
        d_ref[...], nm_ref[...], nv_ref[...] = _adamw(w_ref[...], g, m_ref[...], v_ref[...])
        row = lax.broadcasted_iota(jnp.int32, g.shape, 0)
        per_row = jnp.sum(jnp.where(row == 6, g, 0.0), axis=1, keepdims=True)
        loss_ref[...] = jnp.zeros((8, LANES), F32) + jnp.sum(per_row, axis=0, keepdims=True)

    return pl.pallas_call(
        body, name="sum_adamw_small",
        out_shape=[jax.ShapeDtypeStruct((8, D_MODEL), F32)] * 4 + [jax.ShapeDtypeStruct((8, LANES), F32)],
    )(parts, w, m, v)


def _pack_small(ln1_g, ln1_b, ln2_g, ln2_b, g_sb, g_fox, b_f):
    row5 = jnp.pad(b_f.reshape(1, N_FOX), ((0, 0), (0, D_MODEL - N_FOX)))
    rows = [ln1_g.reshape(1, -1), ln1_b.reshape(1, -1), ln2_g.reshape(1, -1), ln2_b.reshape(1, -1),
            jnp.concatenate([g_sb.reshape(1, -1), g_fox.reshape(1, -1)], axis=1), row5,
            jnp.zeros((2, D_MODEL), F32)]
    return jnp.concatenate(rows, axis=0)


def _unpack_small(p):
    return {"ln1_g": p[0:1], "ln1_b": p[1:2], "ln2_g": p[2:3], "ln2_b": p[3:4], "g_sb": p[4:5, :GROUP_W],
            "g_fox": p[4:5, GROUP_W:], "b_f": p[5:6, :N_FOX]}


def kernel(x, w_in, b_f, g_sb, g_fox, w_out, ln1_g, ln1_b, ln2_g, ln2_b, w_gate_up, w_down, loss_target, m_w_in, m_b_f, m_g_sb, m_g_fox, m_w_out, m_ln1_g, m_ln1_b, m_ln2_g, m_ln2_b, m_w_gate_up, m_w_down, v_w_in, v_b_f, v_g_sb, v_g_fox, v_w_out, v_ln1_g, v_ln1_b, v_ln2_g, v_ln2_b, v_w_gate_up, v_w_down):
    S = x.shape[1]
    x2 = x.reshape(S, D_MODEL)
    tgt = loss_target.reshape(S, D_MODEL)
    TM = 1024
    TR = 512
    BQ = ATTN_BLOCK
    in_w = w_in.shape[2]
    gu_w = w_gate_up.shape[2]

    shards = [w_in[0].astype(BF16), w_out[0].astype(BF16), w_gate_up[0].astype(BF16), w_down[0].astype(BF16)]
    (wi_s,) = _allgather_chips(shards[:1])
    wi = wi_s.transpose(1, 0, 2).reshape(D_MODEL, 4 * in_w)
    w_sb, w_fx = wi[:, :QKV_W // 2], wi[:, QKV_W // 2:QKV_W]
    wqkv = wi[:, :QKV_W]
    wft = wi[:, QKV_W:].T
    proj = _matmul(x2, wqkv, mode="nn", name="proj", tm=TM, tn=512, tk=D_MODEL, outs=[BF16])
    g_row = jnp.concatenate([g_sb, g_fox], axis=1)
    hid = np.arange(D_MODEL) // HEAD_DIM
    he_np = (hid[:, None] == np.arange(LANES)[None, :]).astype(np.float32)
    he, het = jnp.asarray(he_np, BF16), jnp.asarray(he_np.T, BF16)

    lf = _fgate_fwd(x2, wft, b_f.reshape(N_FOX, 1), TM)
    c = _cumsum_fwd(lf)
    c_pair = c.reshape(N_PAIRS, 2, S)
    c_row = jnp.pad(c_pair, ((0, 0), (0, 6), (0, 0)))
    c_col = jnp.pad(c_pair.transpose(0, 2, 1), ((0, 0), (0, 0), (0, 6)))

    o_sb, st_sb, jmin_sb, wo_s, wgu_s, wd_s = _sb_fwd(proj, 0, BQ, shards[1:])
    wo = wo_s.reshape(D_MODEL, D_MODEL)
    wgu = wgu_s.transpose(1, 0, 2).reshape(D_MODEL, 2 * D_FF)
    wg, wu = wgu[:, :D_FF], wgu[:, D_FF:]
    wd = wd_s.reshape(D_FF, D_MODEL)
    jstart_fx = _fox_start_blocks(*_fox_row_norms(proj, 12, TR), c, BQ, BQ)
    o_fx, st_fx = _fox_fwd(proj, 12, c_col, c_row, jstart_fx, BQ, BQ)

    def attn_post(i, osb_ref, ofx_ref, g_ref, he_ref, het_ref, on_ref):
        o = jnp.concatenate([osb_ref[...], ofx_ref[...]], axis=1)
        ms = _head_sums(o * o, he_ref[...], het_ref[...]) * (1.0 / HEAD_DIM)
        on_ref[...] = (o * lax.rsqrt(ms + RMS_EPS) * g_ref[...]).astype(BF16)

    (on,) = _rowwise(attn_post, "attn_post", S, TR,
                     [(o_sb, "t"), (o_fx, "t"), (g_row, "f"), (he, "f"), (het, "f")],
                     [((S, D_MODEL), BF16, "t")])

    def mix_ln1(acc, xv, g, b):
        u = ALPHA * xv + acc
        xh, _ = _ln_stats(u)
        return u, xh * g + b

    row = lambda i, j: (0, 0)
    u1, h1 = _matmul(on, wo, mode="nn", name="mix_ln1", tm=TM, tn=D_MODEL, tk=D_MODEL, outs=[F32, F32],
                     extras=[(x2, (TM if S >= TM else S, D_MODEL), _tile_ij),
                             (ln1_g, (1, D_MODEL), row), (ln1_b, (1, D_MODEL), row)],
                     epilogue=mix_ln1)

    tm_e = TM if S >= TM else S
    n_ff = D_FF // 256

    def gate_up_body(h_ref, wg_ref, wu_ref, g_ref, u_ref, a_ref):
        h = h_ref[...].astype(BF16)
        g, u = _dot(h, wg_ref[...]), _dot(h, wu_ref[...])
        g_ref[...] = g.astype(BF16)
        u_ref[...] = u.astype(BF16)
        a_ref[...] = (g * _sigmoid(g) * u).astype(BF16)

    tm_g = min(2 * TM, S)
    ff_tile = pl.BlockSpec((tm_g, 256), lambda i, j: (i, j))
    gate, up, act = pl.pallas_call(
        gate_up_body, name="gate_up_act", grid=(S // tm_g, n_ff),
        in_specs=[pl.BlockSpec((tm_g, D_MODEL), lambda i, j: (i, 0)),
                  pl.BlockSpec((D_MODEL, 256), lambda i, j: (0, j)),
                  pl.BlockSpec((D_MODEL, 256), lambda i, j: (0, j + n_ff))],
        out_specs=[ff_tile] * 3, out_shape=[jax.ShapeDtypeStruct((S, D_FF), BF16)] * 3)(h1, wgu, wgu)

    u2 = _matmul(act, wd, mode="nn", name="ffn_down", tm=TM, tn=D_MODEL, tk=D_FF, outs=[F32],
                 extras=[(h1, (TM if S >= TM else S, D_MODEL), _tile_ij)],
                 epilogue=lambda acc, hv: (ALPHA * hv + acc,))

    def ln2_loss(i, u_ref, t_ref, g_ref, b_ref, du_ref, acc_ref):
        xh, r = _ln_stats(u_ref[...])
        g = g_ref[...]
        err = xh * g + b_ref[...] - t_ref[...]
        dy = err * (1.0 / D_MODEL)
        du_ref[...] = _ln_bwd(dy, xh, r, g)
        _acc_rows(i, acc_ref, {2: jnp.sum(dy * xh, axis=0, keepdims=True), 3: jnp.sum(dy, axis=0, keepdims=True),
                               6: jnp.sum(err * err, axis=0, keepdims=True) * (0.5 / D_MODEL)})

    du2, acc_ln2 = _rowwise(ln2_loss, "ln2_loss", S, TR, [(u2, "t"), (tgt, "t"), (ln2_g, "f"), (ln2_b, "f")],
                            [((S, D_MODEL), F32, "t"), ((8, D_MODEL), F32, "f")])

    d_wd = _matmul(act, du2, mode="tn", name="dw_down", tm=1408, tn=D_MODEL, tk=TM, outs=[BF16])

    def dgu_epilogue(da, g, u):
        g, u = g.astype(F32), u.astype(F32)
        s = _sigmoid(g)
        return da * u * (s * (1.0 + g * (1.0 - s))), da * (g * s)

    dgate, dup = _matmul(du2, wd, mode="nt", name="d_act", tm=TM, tn=1408, tk=D_MODEL, outs=[BF16, BF16],
                         extras=[(gate, (tm_e, 1408), _tile_ij), (up, (tm_e, 1408), _tile_ij)],
                         epilogue=dgu_epilogue)
    d_wg = _matmul(h1, dgate, mode="tn", name="dw_gate", tm=D_MODEL, tn=1408, tk=TM, outs=[BF16])
    d_wu = _matmul(h1, dup, mode="tn", name="dw_up", tm=D_MODEL, tn=1408, tk=TM, outs=[BF16])
    d_wgu = jnp.concatenate([d_wg, d_wu], axis=1)
    dh1, got_down = _matmul(dgate, wg, mode="nt", name="dh1_gate", tm=TM, tn=D_MODEL, tk=D_FF, outs=[F32],
                            extras=[(du2, (tm_e, D_MODEL), _tile_ij)], epilogue=lambda acc, e: (ALPHA * e + acc,),
                            hosted=[d_wd.reshape(4, D_FF // 4, D_MODEL)])
    dh1, got_gu = _matmul(dup, wu, mode="nt", name="dh1_up", tm=TM, tn=D_MODEL, tk=D_FF, outs=[F32],
                          extras=[(dh1, (tm_e, D_MODEL), _tile_ij)], epilogue=lambda acc, e: (e + acc,),
                          hosted=[d_wgu.reshape(D_MODEL, 4, gu_w).transpose(1, 0, 2)])

    def ln1_bwd(i, dh_ref, u_ref, g_ref, du_ref, acc_ref):
        xh, r = _ln_stats(u_ref[...])
        dh = dh_ref[...]
        du_ref[...] = _ln_bwd(dh, xh, r, g_ref[...])
        _acc_rows(i, acc_ref, {0: jnp.sum(dh * xh, axis=0, keepdims=True), 1: jnp.sum(dh, axis=0, keepdims=True)})

    du1, acc_ln1 = _rowwise(ln1_bwd, "ln1_bwd", S, TR, [(dh1, "t"), (u1, "t"), (ln1_g, "f")],
                            [((S, D_MODEL), F32, "t"), ((8, D_MODEL), F32, "f")])
    d_wo = _matmul(on, du1, mode="tn", name="dw_out", tm=D_MODEL, tn=D_MODEL, tk=TM, outs=[BF16])
    don, got_out = _matmul(du1, wo, mode="nt", name="d_on", tm=TM, tn=D_MODEL, tk=D_MODEL, outs=[F32],
                           hosted=[d_wo.reshape(4, D_MODEL // 4, D_MODEL)])

    def rms_bwd(i, don_ref, osb_ref, ofx_ref, g_ref, he_ref, het_ref, dosb_ref, dofx_ref, acc_ref):
        o = jnp.concatenate([osb_ref[...], ofx_ref[...]], axis=1)
        hev, hetv = he_ref[...], het_ref[...]
        r = lax.rsqrt(_head_sums(o * o, hev, hetv) * (1.0 / HEAD_DIM) + RMS_EPS)
        dn = don_ref[...]
        dg = dn * g_ref[...]
        do = r * dg - o * (r * r * r) * (_head_sums(dg * o, hev, hetv) * (1.0 / HEAD_DIM))
        dosb_ref[...] = do[:, :GROUP_W]
        dofx_ref[...] = do[:, GROUP_W:]
        _acc_rows(i, acc_ref, {4: jnp.sum(dn * o * r, axis=0, keepdims=True)})

    do_sb, do_fx, acc_rms = _rowwise(
        rms_bwd, "rms_bwd", S, TR, [(don, "t"), (o_sb, "t"), (o_fx, "t"), (g_row, "f"), (he, "f"), (het, "f")],
        [((S, GROUP_W), F32, "t"), ((S, GROUP_W), F32, "t"), ((8, D_MODEL), F32, "f")])

    dq_sb, dk_sb, dv_sb = _sb_bwd(proj, 0, do_sb, st_sb, jmin_sb, BQ)
    jstart_fx2 = jnp.minimum(jstart_fx[:, 0::2], jstart_fx[:, 1::2])
    dq_fx, dk_fx, dv_fx, dc = _fox_bwd(proj, 12, do_fx, o_fx, st_fx, c_col, c_row, jstart_fx2, 2 * BQ, BQ)
    dfl, dbf = _fgate_bwd(dc[:, :2, :].reshape(N_FOX, S), lf)
    dp_sb = jnp.concatenate([dq_sb, dk_sb, dv_sb], axis=1)
    dp_fx = jnp.concatenate([dq_fx, dk_fx, dv_fx], axis=1)

    d_wsb = _matmul(x2, dp_sb, mode="tn", name="dw_in_sb", tm=D_MODEL, tn=QKV_W // 2, tk=TM, outs=[BF16])
    d_wfx = _matmul(x2, dp_fx, mode="tn", name="dw_in_fx", tm=D_MODEL, tn=QKV_W // 2, tk=TM, outs=[BF16])
    d_wft = _matmul(dfl, x2, mode="nn", name="dw_in_f", tm=N_FOX, tn=D_MODEL, tk=TM, outs=[BF16])
    d_wi = jnp.concatenate([d_wsb, d_wfx, d_wft.T], axis=1)
    dx, got_in = _matmul(dp_sb, w_sb, mode="nt", name="dx_sb", tm=TM, tn=D_MODEL, tk=QKV_W // 2, outs=[F32],
                         extras=[(du1, (tm_e, D_MODEL), _tile_ij)], epilogue=lambda acc, e: (ALPHA * e + acc,),
                         hosted=[d_wi.reshape(D_MODEL, 4, in_w).transpose(1, 0, 2)])
    dx = _matmul(dp_fx, w_fx, mode="nt", name="dx_fx", tm=TM, tn=D_MODEL, tk=QKV_W // 2, outs=[F32],
                 extras=[(dx, (tm_e, D_MODEL), _tile_ij)], epilogue=lambda acc, e: (e + acc,))
    dx = _matmul(dfl, wft, mode="tn", name="dx_f", tm=TM, tn=D_MODEL, tk=N_FOX, outs=[F32],
                 extras=[(dx, (tm_e, D_MODEL), _tile_ij)], epilogue=lambda acc, e: (e + acc,))

    got = [got_in, got_out, got_gu, got_down]
    big_names = ("w_in", "w_out", "w_gate_up", "w_down")
    halves = [_sum_parts(p, "sum_" + nm, tr) for nm, p, tr in zip(big_names, got, (256, 128, 128, 176))]
    grads = _sibling_swap(halves)
    big = {}
    for nm, g, w, m, v, tr in zip(big_names, grads, (w_in, w_out, w_gate_up, w_down),
                                  (m_w_in, m_w_out, m_w_gate_up, m_w_down),
                                  (v_w_in, v_w_out, v_w_gate_up, v_w_down), (256, 256, 256, 176)):
        big[nm] = [r[None] for r in [g] + list(_adamw_call(g, w[0], m[0], v[0], "adamw_" + nm, tr))]

    small = acc_ln2 + acc_ln1 + acc_rms
    small = small + jnp.pad(dbf.reshape(1, N_FOX), ((5, 2), (0, D_MODEL - N_FOX)))
    (small_all,) = _exchange([small], False)
    sw = _pack_small(ln1_g, ln1_b, ln2_g, ln2_b, g_sb, g_fox, b_f)
    sm = _pack_small(m_ln1_g, m_ln1_b, m_ln2_g, m_ln2_b, m_g_sb, m_g_fox, m_b_f)
    sv = _pack_small(v_ln1_g, v_ln1_b, v_ln2_g, v_ln2_b, v_g_sb, v_g_fox, v_b_f)
    sg, sd, snm, snv, loss_blk = _sum_adamw_small(small_all, sw, sm, sv)
    sg, sd, snm, snv = _unpack_small(sg), _unpack_small(sd), _unpack_small(snm), _unpack_small(snv)

    names = ["w_in", "b_f", "g_sb", "g_fox", "w_out", "ln1_g", "ln1_b", "ln2_g", "ln2_b", "w_gate_up", "w_down"]
    outs = [loss_blk[0, 0], dx.reshape(1, S, D_MODEL)]
    for k, table in enumerate((sg, sd, snm, snv)):
        outs += [big[n][k] if n in big else table[n] for n in names]
    return tuple(outs)
```

```python
import functools

import numpy as np
import jax
import jax.numpy as jnp
from jax import lax
from jax.experimental import pallas as pl
from jax.experimental.pallas import tpu as pltpu

F32 = jnp.float32
BF16 = jnp.bfloat16

D_MODEL = 1024
HEAD_DIM = 64
LANES = 128
N_PAIRS = 4
GROUP_W = 512
QKV_W = 3072
D_FF = 2816
N_FOX = 8
ALPHA = 2.0 ** 0.25
LN_EPS = 1e-5
RMS_EPS = 1e-6
SCALE = HEAD_DIM ** -0.5
NEG_BIG = -1e30
FOX_SKIP = 30.0
SB_STOP = -105.0
ADAM_LR, ADAM_B1, ADAM_B2, ADAM_EPS, ADAM_WD, ADAM_STEP = 0.001, 0.9, 0.999, 1e-08, 0.01, 10
KV_SLOTS = 4
SCAN_GROUP = 8
ATTN_BLOCK = 256
VMEM_BIG = 56 * 1024 * 1024
MESH = pl.DeviceIdType.MESH

_NN = (((1,), (0,)), ((), ()))
_NT = (((1,), (1,)), ((), ()))
_TN = (((0,), (0,)), ((), ()))


def _dot(a, b, dims=_NN):
    return lax.dot_general(a, b, dims, preferred_element_type=F32)


def _split_dot(x, t):
    hi = x.astype(BF16)
    lo = (x - hi.astype(F32)).astype(BF16)
    return _dot(hi, t) + _dot(lo, t)


def _softplus(z):
    return jnp.maximum(z, 0.0) + jnp.log1p(jnp.exp(-jnp.abs(z)))


def _sigmoid(x):
    return 0.5 * jnp.tanh(0.5 * x) + 0.5


def _col(v, h):
    lane = lax.broadcasted_iota(jnp.int32, v.shape, 1)
    return jnp.sum(jnp.where(lane == h, v, 0.0), axis=1, keepdims=True)


def _two_sum(hi, lo, b):
    s = hi + b
    bb = s - hi
    err = (hi - (s - bb)) + (b - bb)
    return s, lo + err


def _params(vmem=None):
    return pltpu.CompilerParams(vmem_limit_bytes=vmem) if vmem else None


def _matmul(a, b, *, mode, name, tm, tn, tk, outs, extras=(), epilogue=None, vmem=None, hosted=()):
    if mode == "nn":
        (M, K), (_, N) = a.shape, b.shape
    elif mode == "nt":
        (M, K), (N, _) = a.shape, b.shape
    else:
        (K, M), (_, N) = a.shape, b.shape
    tm, tn, tk = min(tm, M), min(tn, N), min(tk, K)
    assert M % tm == 0 and N % tn == 0 and K % tk == 0, (name, M, N, K, tm, tn, tk)
    nk = K // tk
    dims = {"nn": _NN, "nt": _NT, "tn": _TN}[mode]
    if mode == "tn":
        a_spec = pl.BlockSpec((tk, tm), lambda i, j, k: (k, i))
    else:
        a_spec = pl.BlockSpec((tm, tk), lambda i, j, k: (i, k))
    if mode == "nt":
        b_spec = pl.BlockSpec((tn, tk), lambda i, j, k: (j, k))
    else:
        b_spec = pl.BlockSpec((tk, tn), lambda i, j, k: (k, j))
    ex_specs = [pl.BlockSpec(bs, (lambda i, j, k, f=f: f(i, j))) for (_, bs, f) in extras]
    ne, no, nh = len(extras), len(outs), len(hosted)
    if epilogue is None:
        epilogue = lambda acc: (acc,)
    gi, gj = M // tm, N // tn
    host_shapes, host_sems = _exchange_shapes(hosted, True) if nh else ([], [])

    def body(a_ref, b_ref, *rest):
        ex_refs, host_ins = rest[:ne], rest[ne:ne + nh]
        out_refs, host_outs = rest[ne + nh:ne + nh + no], rest[ne + nh + no:ne + 2 * nh + no]
        acc = rest[ne + 2 * nh + no]
        i, j, k = pl.program_id(0), pl.program_id(1), pl.program_id(2)
        if nh:
            start, wait = _exchange_copies(host_ins, host_outs, *rest[ne + 2 * nh + no + 1:], True, hosted)
            pl.when(jnp.logical_and(jnp.logical_and(i == 0, j == 0), k == 0))(start)

        @pl.when(k == 0)
        def _():
            acc[...] = jnp.zeros_like(acc)

        acc[...] += _dot(a_ref[...].astype(BF16), b_ref[...].astype(BF16), dims)

        @pl.when(k == nk - 1)
        def _():
            res = epilogue(acc[...], *[e[...] for e in ex_refs])
            for r, o in zip(res, out_refs):
                o[...] = r.astype(o.dtype)

        if nh:
            pl.when(jnp.logical_and(jnp.logical_and(i == gi - 1, j == gj - 1), k == nk - 1))(wait)

    res = pl.pallas_call(
        body, name=name, grid=(gi, gj, nk),
        in_specs=[a_spec, b_spec] + ex_specs + [_HBM] * nh,
        out_specs=[pl.BlockSpec((tm, tn), lambda i, j, k: (i, j)) for _ in outs] + [_HBM] * nh,
        out_shape=[jax.ShapeDtypeStruct((M, N), d) for d in outs] + host_shapes,
        scratch_shapes=[pltpu.VMEM((tm, tn), F32)] + host_sems,
        compiler_params=_params(vmem),
    )(a, b, *[e[0] for e in extras], *hosted)
    return res[0] if no + nh == 1 else res


def _tile_ij(i, j):
    return (i, j)


def _rowwise(fn, name, rows, tm, ins, outs, vmem=None):
    tm = min(tm, rows)
    assert rows % tm == 0

    def spec(shape, kind):
        if kind == "t":
            return pl.BlockSpec((tm,) + tuple(shape[1:]), lambda i: (i,) + (0,) * (len(shape) - 1))
        return pl.BlockSpec(tuple(shape), lambda i: (0,) * len(shape))

    def body(*refs):
        fn(pl.program_id(0), *refs)

    return pl.pallas_call(
        body, name=name, grid=(rows // tm,),
        in_specs=[spec(a.shape, k) for a, k in ins],
        out_specs=[spec(s, k) for s, _, k in outs],
        out_shape=[jax.ShapeDtypeStruct(s, d) for s, d, _ in outs],
        compiler_params=_params(vmem),
    )(*[a for a, _ in ins])


def _ln_stats(u):
    mu = jnp.mean(u, axis=-1, keepdims=True)
    d = u - mu
    var = jnp.mean(d * d, axis=-1, keepdims=True)
    r = lax.rsqrt(var + LN_EPS)
    return d * r, r


def _ln_bwd(dh, xh, r, g):
    dxh = dh * g
    m1 = jnp.mean(dxh, axis=-1, keepdims=True)
    m2 = jnp.mean(dxh * xh, axis=-1, keepdims=True)
    return r * (dxh - m1 - xh * m2)


def _acc_rows(i, ref, rows):
    @pl.when(i == 0)
    def _():
        ref[...] = jnp.zeros_like(ref)
    for r, v in rows.items():
        ref[pl.ds(r, 1), :] += v


def _head_sums(v, he, het):
    return _split_dot(_split_dot(v, he), het)


def _fgate_fwd(x, wft, bf_col, tm):
    S = x.shape[0]
    tm = min(tm, S)

    def body(wft_ref, bf_ref, x_ref, lf_ref):
        f = _dot(wft_ref[...], x_ref[...].astype(BF16), _NT) + bf_ref[...]
        lf_ref[...] = -_softplus(-f)

    return pl.pallas_call(
        body, name="fgate_fwd", grid=(S // tm,),
        in_specs=[pl.BlockSpec((N_FOX, D_MODEL), lambda i: (0, 0)), pl.BlockSpec((N_FOX, 1), lambda i: (0, 0)),
                  pl.BlockSpec((tm, D_MODEL), lambda i: (i, 0))],
        out_specs=pl.BlockSpec((N_FOX, tm), lambda i: (0, i)),
        out_shape=jax.ShapeDtypeStruct((N_FOX, S), F32),
    )(wft, bf_col, x)


def _chunk_scan(v, reverse):
    lane = lax.broadcasted_iota(jnp.int32, v.shape, 1)
    sh = 1
    while sh < LANES:
        if reverse:
            v = v + jnp.where(lane < LANES - sh, pltpu.roll(v, LANES - sh, 1), 0.0)
        else:
            v = v + jnp.where(lane >= sh, pltpu.roll(v, sh, 1), 0.0)
        sh *= 2
    return v


def _cumsum_fwd(lf):
    n, S = lf.shape
    nc = S // LANES

    grp = min(SCAN_GROUP, nc)

    def body(lf_ref, c_ref):
        def step(gi, carry):
            sls = [pl.ds(pl.multiple_of((gi * grp + g) * LANES, LANES), LANES) for g in range(grp)]
            vs = [_chunk_scan(lf_ref[:, sl], False) for sl in sls]
            tots = [_col(v, LANES - 1) for v in vs]
            for sl, v, t in zip(sls, vs, tots):
                c_ref[:, sl] = v + carry
                carry = carry + t
            return carry
        lax.fori_loop(0, nc // grp, step, jnp.zeros((n, 1), F32))

    return pl.pallas_call(body, name="cumsum_fwd", out_shape=jax.ShapeDtypeStruct((n, S), F32))(lf)


def _fgate_bwd(dc, lf):
    n, S = dc.shape
    nc = S // LANES

    grp = min(SCAN_GROUP, nc)

    def body(dc_ref, lf_ref, dfl_ref, dbf_ref):
        def step(t, carry):
            car, tot = carry
            gi = nc // grp - 1 - t
            sls = [pl.ds(pl.multiple_of((gi * grp + g) * LANES, LANES), LANES) for g in range(grp)]
            vs = [_chunk_scan(dc_ref[:, sl], True) for sl in sls]
            firsts = [_col(v, 0) for v in vs]
            for sl, v, f in reversed(list(zip(sls, vs, firsts))):
                dfl = (v + car) * (1.0 - jnp.exp(lf_ref[:, sl]))
                dfl_ref[:, sl] = dfl
                tot = tot + jnp.sum(dfl, axis=1, keepdims=True)
                car = car + f
            return car, tot
        _, tot = lax.fori_loop(0, nc // grp, step, (jnp.zeros((n, 1), F32), jnp.zeros((n, 1), F32)))
        dbf_ref[...] = tot

    return pl.pallas_call(body, name="fgate_bwd",
                          out_shape=[jax.ShapeDtypeStruct((n, S), F32), jax.ShapeDtypeStruct((n, 1), F32)])(dc, lf)


def _tri_matrices(b):
    r = np.arange(b)
    tfwd = (r[:, None] <= r[None, :]).astype(np.float32)
    return jnp.asarray(tfwd, BF16), jnp.asarray(tfwd.T, BF16)


def _kv_copies(kv_hbm, kbuf, vbuf, sems, sem0, pair_col, bq, j, slot):
    rows = pl.ds(pl.multiple_of(j * bq, bq), bq)

    def cols(c):
        return pl.ds(pl.multiple_of((pair_col + c) * LANES, LANES), LANES)

    return (pltpu.make_async_copy(kv_hbm.at[rows, cols(4)], kbuf.at[slot], sems.at[0, sem0 + slot]),
            pltpu.make_async_copy(kv_hbm.at[rows, cols(8)], vbuf.at[slot], sems.at[1, sem0 + slot]))


def _first_two_up(first_block, per=1):
    def blocks(pair, blk):
        first = first_block(pair, blk)
        return first, first + 1, first + 1 <= per * blk + per - 1
    return blocks


def _first_two_down(pair, blk):
    return blk, blk - 1, blk > 0


def _start_two(fetch, pair, first, second, has_second, ahead):
    for cp in fetch(first, 0, pair, ahead):
        cp.start()

    @pl.when(has_second)
    def _():
        for cp in fetch(second, 1, pair, ahead):
            cp.start()


def _kv_fetcher(kv_hbm, kbuf, vbuf, sems, ns, col0, bq, p, i, nq, blocks):
    base = lax.rem(p * nq + i, 2) * ns
    own = (kbuf.at[pl.ds(base, ns)], vbuf.at[pl.ds(base, ns)])
    other = (kbuf.at[pl.ds(ns - base, ns)], vbuf.at[pl.ds(ns - base, ns)])

    def fetch(j, slot, pair=p, ahead=False):
        kb, vb = other if ahead else own
        return _kv_copies(kv_hbm, kb, vb, sems, ns - base if ahead else base, col0 + pair, bq, j, slot)

    pl.when(jnp.logical_and(p == 0, i == 0))(lambda: _start_two(fetch, p, *blocks(p, i), False))
    wrap = i == nq - 1

    @pl.when(jnp.logical_not(jnp.logical_and(wrap, p == N_PAIRS - 1)))
    def _():
        pair, blk = jnp.where(wrap, p + 1, p), jnp.where(wrap, 0, i + 1)
        _start_two(fetch, pair, *blocks(pair, blk), True)

    return fetch, own[0], own[1]


def _masked_pair(v, lane_is_a, scale=1.0):
    v = v.astype(F32) * scale
    return jnp.where(lane_is_a, v, 0.0).astype(BF16), jnp.where(lane_is_a, 0.0, v).astype(BF16)


def _sb_fwd(proj, col0, bq, shards=()):
    S = proj.shape[0]
    bq = min(bq, S)
    nq = S // bq
    _, trev = _tri_matrices(bq)
    nh = len(shards)
    gather_shapes, gather_sems = _gather_shapes(shards) if nh else ([], [])

    def body(q_ref, kv_hbm, trev_ref, *rest):
        o_ref, st_ref, jmin_ref = rest[nh:nh + 3]
        acc_a, acc_b, qa, qb, rs, kbuf, vbuf, sems = rest[2 * nh + 3:2 * nh + 11]
        p, i = pl.program_id(0), pl.program_id(1)
        if nh:
            gather_start, gather_wait = _gather_copies(rest[:nh], rest[nh + 3:2 * nh + 3], *rest[2 * nh + 11:])
            pl.when(jnp.logical_and(p == 0, i == 0))(gather_start)
        fetch, kbuf, vbuf = _kv_fetcher(kv_hbm, kbuf, vbuf, sems, 2, col0, bq, p, i, nq, _first_two_down)
        is_a = lax.broadcasted_iota(jnp.int32, (bq, LANES), 1) < HEAD_DIM
        acc_a[...] = jnp.zeros_like(acc_a)
        acc_b[...] = jnp.zeros_like(acc_b)
        rs[...] = jnp.zeros_like(rs)
        qa[...], qb[...] = _masked_pair(q_ref[...], is_a, SCALE)

        def tiles(blocks):
            hs, qs, accs, trev_m = (0, 1), (qa, qb), (acc_a, acc_b), trev_ref[...]
            kv = [(kbuf[s], vbuf[s]) for s, _ in blocks]
            bh = [(b, h) for b in range(len(blocks)) for h in hs]
            tri = lax.broadcasted_iota(jnp.int32, (bq, bq), 0) > lax.broadcasted_iota(jnp.int32, (bq, bq), 1)
            z = {(b, h): _dot(qs[h][...], kv[b][0], _NT) for b, h in bh}
            lk = {(b, h): -_softplus(z[b, h]) for b, h in bh}
            lk = {(b, h): jnp.where(tri, lk[b, h], 0.0) if blocks[b][1] else lk[b, h] for b, h in bh}
            suf = {(b, h): _split_dot(lk[b, h], trev_m) for b, h in bh}
            tot = {(b, h): jnp.sum(lk[b, h], axis=1, keepdims=True) for b, h in bh}
            right = {}
            for h in hs:
                r = rs[2 * h] + rs[2 * h + 1]
                for b in range(len(blocks)):
                    right[b, h] = r
                    r = r + tot[b, h]
            w = {(b, h): jnp.exp(z[b, h] + suf[b, h] + right[b, h]) for b, h in bh}
            w = {(b, h): jnp.where(tri, w[b, h], 0.0) if blocks[b][1] else w[b, h] for b, h in bh}
            pv = {(b, h): _dot(w[b, h].astype(BF16), kv[b][1]) for b, h in bh}
            for h in hs:
                accs[h][...] += sum([pv[b, h] for b in range(1, len(blocks))], pv[0, h])
                hi, lo = rs[2 * h], rs[2 * h + 1]
                for b in range(len(blocks)):
                    hi, lo = _two_sum(hi, lo, tot[b, h])
                rs[2 * h], rs[2 * h + 1] = hi, lo

        def live():
            return (jnp.max(jnp.maximum(rs[0], rs[2])) > SB_STOP).astype(jnp.int32)

        for cp in fetch(i, 0):
            cp.wait()
        pl.when(i == 0)(functools.partial(tiles, [(0, True)]))

        @pl.when(i > 0)
        def _():
            for cp in fetch(i - 1, 1):
                cp.wait()
            tiles([(0, True), (1, False)])

        def step(carry):
            j, _ = carry
            slot = lax.rem(i - j, 2)
            for cp in fetch(j, slot):
                cp.start()
            for cp in fetch(j, slot):
                cp.wait()
            tiles([(slot, False)])
            return j - 1, live()

        j_end, _ = lax.while_loop(lambda c: jnp.logical_and(c[0] >= 0, c[1] > 0), step, (i - 2, live()))
        jmin_ref[p, i] = jnp.maximum(j_end + 1, 0)
        o_ref[...] = jnp.where(is_a, acc_a[...], acc_b[...])
        lane8 = lax.broadcasted_iota(jnp.int32, (bq, 8), 1)
        st = jnp.zeros((bq, 8), F32)
        for c, src in enumerate((0, 2, 1, 3)):
            st = jnp.where(lane8 == c, rs[src], st)
        st_ref[0] = st
        if nh:
            pl.when(jnp.logical_and(p == N_PAIRS - 1, i == nq - 1))(gather_wait)

    return pl.pallas_call(
        body, name="sb_fwd", grid=(N_PAIRS, nq),
        in_specs=[pl.BlockSpec((bq, LANES), lambda p, i: (i, col0 + p)),
                  pl.BlockSpec(memory_space=pl.ANY),
                  pl.BlockSpec((bq, bq), lambda p, i: (0, 0))] + [_HBM] * nh,
        out_specs=[pl.BlockSpec((bq, LANES), lambda p, i: (i, p)),
                   pl.BlockSpec((1, bq, 8), lambda p, i: (p, i, 0)),
                   pl.BlockSpec(memory_space=pltpu.SMEM)] + [_HBM] * nh,
        out_shape=[jax.ShapeDtypeStruct((S, GROUP_W), F32), jax.ShapeDtypeStruct((N_PAIRS, S, 8), F32),
                   jax.ShapeDtypeStruct((N_PAIRS, nq), jnp.int32)] + gather_shapes,
        scratch_shapes=[pltpu.VMEM((bq, LANES), F32), pltpu.VMEM((bq, LANES), F32),
                        pltpu.VMEM((bq, LANES), BF16), pltpu.VMEM((bq, LANES), BF16),
                        pltpu.VMEM((4, bq, 1), F32),
                        pltpu.VMEM((4, bq, LANES), BF16), pltpu.VMEM((4, bq, LANES), BF16),
                        pltpu.SemaphoreType.DMA((2, 4))] + gather_sems,
    )(proj, proj, trev, *shards)


def _sb_bwd(proj, col0, do, st, jmin, bq):
    S = proj.shape[0]
    bq = min(bq, S)
    nq = S // bq
    tfwd, trev = _tri_matrices(bq)

    def body(jmin_ref, q_ref, kv_hbm, do_ref, st_ref, tfwd_ref, trev_ref,
             dq_ref, dk_out, dv_out, dq_a, dq_b, qa, qb, doa, dob, rs, kbuf, vbuf, sems, dk_ref, dv_ref):
        p, i = pl.program_id(0), pl.program_id(1)
        j0 = jmin_ref[p, i]
        first_two = _first_two_up(lambda pair, blk: jmin_ref[pair, blk])
        fetch, kbuf, vbuf = _kv_fetcher(kv_hbm, kbuf, vbuf, sems, KV_SLOTS, col0, bq, p, i, nq, first_two)
        is_a = lax.broadcasted_iota(jnp.int32, (bq, LANES), 1) < HEAD_DIM

        @pl.when(i == 0)
        def _():
            dk_ref[...] = jnp.zeros_like(dk_ref)
            dv_ref[...] = jnp.zeros_like(dv_ref)

        dq_a[...] = jnp.zeros_like(dq_a)
        dq_b[...] = jnp.zeros_like(dq_b)
        rs[...] = jnp.zeros_like(rs)
        st_v = st_ref[0]
        for h in range(2):
            rs[6 + 2 * h], rs[7 + 2 * h] = _col(st_v, h), _col(st_v, 2 + h)
        qa[...], qb[...] = _masked_pair(q_ref[...], is_a, SCALE)
        doa[...], dob[...] = _masked_pair(do_ref[...], is_a)

        def tiles(blocks):
            hs, qs, dos, dqs = (0, 1), (qa, qb), (doa, dob), (dq_a, dq_b)
            tfwd_m, trev_m = tfwd_ref[...], trev_ref[...]
            kv = [(kbuf[s], vbuf[s]) for _, s, _ in blocks]
            nb = len(blocks)
            bh = [(b, h) for b in range(nb) for h in hs]
            tri = lax.broadcasted_iota(jnp.int32, (bq, bq), 0) > lax.broadcasted_iota(jnp.int32, (bq, bq), 1)

            def mask(x, b):
                return jnp.where(tri, x, 0.0) if blocks[b][2] else x

            z = {(b, h): _dot(qs[h][...], kv[b][0], _NT) for b, h in bh}
            dw = {(b, h): _dot(dos[h][...], kv[b][1], _NT) for b, h in bh}
            lk = {(b, h): mask(-_softplus(z[b, h]), b) for b, h in bh}
            suf = {(b, h): _split_dot(lk[b, h], trev_m) for b, h in bh}
            tot = {(b, h): jnp.sum(lk[b, h], axis=1, keepdims=True) for b, h in bh}
            pre = {}
            for h in hs:
                run = (rs[3 * h], rs[3 * h + 1])
                for b in range(nb):
                    run = _two_sum(run[0], run[1], tot[b, h])
                    pre[b, h] = run
            right = {(b, h): (rs[6 + 2 * h] - pre[b, h][0]) + (rs[7 + 2 * h] - pre[b, h][1]) for b, h in bh}
            w = {(b, h): mask(jnp.exp(z[b, h] + suf[b, h] + right[b, h]), b) for b, h in bh}
            g = {(b, h): dw[b, h] * w[b, h] for b, h in bh}
            gpre = {(b, h): _split_dot(g[b, h], tfwd_m) for b, h in bh}
            gtot = {(b, h): jnp.sum(g[b, h], axis=1, keepdims=True) for b, h in bh}
            gleft = {}
            for h in hs:
                run = rs[3 * h + 2]
                for b in range(nb):
                    gleft[b, h] = run
                    run = run + gtot[b, h]
                gleft[nb, h] = run
            dz = {(b, h): mask(g[b, h] - jnp.exp(z[b, h] + lk[b, h]) * (gpre[b, h] + gleft[b, h]), b) for b, h in bh}
            dzb = {(b, h): dz[b, h].astype(BF16) for b, h in bh}
            wb = {(b, h): w[b, h].astype(BF16) for b, h in bh}
            dqc = {(b, h): _dot(dzb[b, h], kv[b][0]) for b, h in bh}
            dkc = {(b, h): _dot(dzb[b, h], qs[h][...], _TN) for b, h in bh}
            dvc = {(b, h): _dot(wb[b, h], dos[h][...], _TN) for b, h in bh}
            for h in hs:
                rs[3 * h], rs[3 * h + 1] = pre[nb - 1, h]
                rs[3 * h + 2] = gleft[nb, h]
                dqs[h][...] += sum([dqc[b, h] for b in range(1, nb)], dqc[0, h])
            for b, (j, _, _) in enumerate(blocks):
                rows = pl.ds(pl.multiple_of(j * bq, bq), bq)
                dk_ref[rows, :] += dkc[b, 0] + dkc[b, 1]
                dv_ref[rows, :] += dvc[b, 0] + dvc[b, 1]

        def single(j, slot, masked):
            tiles([(j, slot, masked)])

        def wait(j):
            slot = lax.rem(j - j0, KV_SLOTS)
            for cp in fetch(j, slot):
                cp.wait()
            return slot

        _walk_up(fetch, j0, i, i, single, stop=jnp.maximum(i - 1, j0))

        @pl.when(j0 < i)
        def _():
            tiles([(i - 1, wait(i - 1), False), (i, wait(i), True)])

        @pl.when(j0 == i)
        def _():
            tiles([(i, wait(i), True)])

        dq_ref[...] = (jnp.where(is_a, dq_a[...], dq_b[...]) * SCALE).astype(BF16)

        @pl.when(i == nq - 1)
        def _():
            dk_out[...] = dk_ref[...].astype(BF16)
            dv_out[...] = dv_ref[...].astype(BF16)

    grid_spec = pltpu.PrefetchScalarGridSpec(
        num_scalar_prefetch=1, grid=(N_PAIRS, nq),
        in_specs=[pl.BlockSpec((bq, LANES), lambda p, i, jm: (i, col0 + p)),
                  pl.BlockSpec(memory_space=pl.ANY),
                  pl.BlockSpec((bq, LANES), lambda p, i, jm: (i, p)),
                  pl.BlockSpec((1, bq, 8), lambda p, i, jm: (p, i, 0)),
                  pl.BlockSpec((bq, bq), lambda p, i, jm: (0, 0)),
                  pl.BlockSpec((bq, bq), lambda p, i, jm: (0, 0))],
        out_specs=[pl.BlockSpec((bq, LANES), lambda p, i, jm: (i, p)),
                   pl.BlockSpec((S, LANES), lambda p, i, jm: (0, p)),
                   pl.BlockSpec((S, LANES), lambda p, i, jm: (0, p))],
        scratch_shapes=[pltpu.VMEM((bq, LANES), F32), pltpu.VMEM((bq, LANES), F32)]
        + [pltpu.VMEM((bq, LANES), BF16)] * 4 + [pltpu.VMEM((10, bq, 1), F32)]
        + [pltpu.VMEM((2 * KV_SLOTS, bq, LANES), BF16)] * 2 + [pltpu.SemaphoreType.DMA((2, 2 * KV_SLOTS))]
        + [pltpu.VMEM((S, LANES), F32)] * 2)
    return pl.pallas_call(
        body, name="sb_bwd", grid_spec=grid_spec,
        out_shape=[jax.ShapeDtypeStruct((S, GROUP_W), BF16)] * 3,
        compiler_params=_params(VMEM_BIG),
    )(jmin, proj, proj, do, st, tfwd, trev)


def _walk_up(fetch, j0, diag, last, tile, stop=None):
    ahead = KV_SLOTS - 1
    stop = last + 1 if stop is None else stop

    def start(j):
        @pl.when(j <= last)
        def _():
            for cp in fetch(j, lax.rem(j - j0, KV_SLOTS)):
                cp.start()

    for d in range(2, ahead):
        start(j0 + d)

    def step(j, carry):
        slot = lax.rem(j - j0, KV_SLOTS)
        for cp in fetch(j, slot):
            cp.wait()
        start(j + ahead)
        pl.when(j >= diag)(functools.partial(tile, j, slot, True))
        pl.when(j < diag)(functools.partial(tile, j, slot, False))
        return carry

    lax.fori_loop(j0, stop, step, 0)


def _causal(bq, bk, i, j):
    row = lax.broadcasted_iota(jnp.int32, (bq, bk), 0)
    col = lax.broadcasted_iota(jnp.int32, (bq, bk), 1)
    return col - row <= i * bq - j * bk


def _by_heads(j, first_a, first_b, heads):
    on_a, on_b = j >= first_a, j >= first_b
    pl.when(jnp.logical_and(on_a, on_b))(functools.partial(heads, (0, 1)))
    pl.when(jnp.logical_and(on_a, jnp.logical_not(on_b)))(functools.partial(heads, (0,)))
    pl.when(jnp.logical_and(on_b, jnp.logical_not(on_a)))(functools.partial(heads, (1,)))


def _fox_row_norms(proj, col0, tm):
    S = proj.shape[0]
    tm = min(tm, S)
    head_of = np.arange(GROUP_W) // HEAD_DIM
    he_t = jnp.asarray((np.arange(2 * N_PAIRS)[:, None] == head_of[None, :]).astype(np.float32), BF16)

    def body(q_ref, k_ref, he_ref, qn_ref, kn_ref, d_ref):
        q, k, he = q_ref[...].astype(F32), k_ref[...].astype(F32), he_ref[...]

        def head_sums_t(x):
            hi = x.astype(BF16)
            lo = (x - hi.astype(F32)).astype(BF16)
            return _dot(he, hi, _NT) + _dot(he, lo, _NT)

        qn_ref[...] = jnp.sqrt(head_sums_t(q * q))
        kn_ref[...] = jnp.sqrt(head_sums_t(k * k))
        d_ref[...] = SCALE * head_sums_t(q * k)

    wide = GROUP_W // LANES
    return pl.pallas_call(
        body, name="fox_row_norms", grid=(S // tm,),
        in_specs=[pl.BlockSpec((tm, GROUP_W), lambda i: (i, col0 // wide)),
                  pl.BlockSpec((tm, GROUP_W), lambda i: (i, (col0 + 4) // wide)),
                  pl.BlockSpec((2 * N_PAIRS, GROUP_W), lambda i: (0, 0))],
        out_specs=[pl.BlockSpec((2 * N_PAIRS, tm), lambda i: (0, i))] * 3,
        out_shape=[jax.ShapeDtypeStruct((2 * N_PAIRS, S), F32)] * 3)(proj, proj, he_t)


def _fox_start_blocks(qn, kn, d, c, bq, bk):
    nh, S = c.shape
    nq, nk = S // bq, S // bk
    top = SCALE * qn * kn.max(axis=1, keepdims=True) - d + c
    top = top.reshape(nh, nq, bq).max(axis=2)
    c_last = c[:, bk - 1::bk]
    live = top[:, :, None] - c_last[:, None, :] >= -FOX_SKIP

    def first_block(lv):
        first = jnp.where(lv.any(axis=2), jnp.argmax(lv, axis=2), nk)
        return jnp.minimum(first, (bq // bk) * jnp.arange(nq)[None, :]).astype(jnp.int32)

    return jnp.concatenate([first_block(live.reshape(N_PAIRS, 2, nq, nk).any(axis=1)), first_block(live)], axis=0)


def _fox_fwd(proj, col0, c_col, c_row, jstart, bq, bk):
    S = proj.shape[0]
    nq, per = S // bq, bq // bk

    def body(js_ref, q_ref, kv_hbm, cc_ref, cr_ref, o_ref, st_ref, acc_a, acc_b, qa, qb, ml, kbuf, vbuf, sems):
        p, i = pl.program_id(0), pl.program_id(1)
        j0 = js_ref[p, i]
        first_two = _first_two_up(lambda pair, blk: js_ref[pair, blk], per)
        fetch, kbuf, vbuf = _kv_fetcher(kv_hbm, kbuf, vbuf, sems, KV_SLOTS, col0, bk, p, i, nq, first_two)
        is_a = lax.broadcasted_iota(jnp.int32, (bq, LANES), 1) < HEAD_DIM
        acc_a[...] = jnp.zeros_like(acc_a)
        acc_b[...] = jnp.zeros_like(acc_b)
        ml[0] = jnp.full((bq, 1), NEG_BIG, F32)
        ml[2] = jnp.full((bq, 1), NEG_BIG, F32)
        ml[1] = jnp.zeros((bq, 1), F32)
        ml[3] = jnp.zeros((bq, 1), F32)
        cc = cc_ref[0]
        ml[4], ml[5] = _col(cc, 0), _col(cc, 1)
        qa[...], qb[...] = _masked_pair(q_ref[...], is_a, SCALE)

        def tile(j, slot, masked):
            k, v = kbuf[slot], vbuf[slot]
            cols = pl.ds(pl.multiple_of(j * bk, bk), bk)
            if masked:
                tri = _causal(bq, bk, i, j)

            def heads(hs):
                qs, accs = (qa, qb), (acc_a, acc_b)
                s = {h: _dot(qs[h][...], k, _NT) - cr_ref[0, pl.ds(h, 1), cols] for h in hs}
                if masked:
                    s = {h: jnp.where(tri, s[h], NEG_BIG) for h in hs}
                top = {h: jnp.max(s[h], axis=1, keepdims=True) for h in hs}
                m_new = {h: jnp.maximum(ml[2 * h], top[h] + ml[4 + h]) for h in hs}
                a = {h: jnp.exp(ml[2 * h] - m_new[h]) for h in hs}
                pr = {h: jnp.exp(s[h] - (m_new[h] - ml[4 + h])) for h in hs}
                tot = {h: jnp.sum(pr[h], axis=1, keepdims=True) for h in hs}
                pv = {h: _dot(pr[h].astype(BF16), v) for h in hs}
                for h in hs:
                    ml[2 * h] = m_new[h]
                    ml[2 * h + 1] = a[h] * ml[2 * h + 1] + tot[h]
                    accs[h][...] = a[h] * accs[h][...] + pv[h]

            _by_heads(j, js_ref[N_PAIRS + 2 * p, i], js_ref[N_PAIRS + 2 * p + 1, i], heads)

        _walk_up(fetch, j0, per * i, per * i + per - 1, tile)
        o_ref[...] = jnp.where(is_a, acc_a[...] / ml[1], acc_b[...] / ml[3])
        lane8 = lax.broadcasted_iota(jnp.int32, (bq, 8), 1)
        st = jnp.where(lane8 == 0, ml[0] + jnp.log(ml[1]), 0.0)
        st_ref[0] = jnp.where(lane8 == 1, ml[2] + jnp.log(ml[3]), st)

    grid_spec = pltpu.PrefetchScalarGridSpec(
        num_scalar_prefetch=1, grid=(N_PAIRS, nq),
        in_specs=[pl.BlockSpec((bq, LANES), lambda p, i, js: (i, col0 + p)),
                  pl.BlockSpec(memory_space=pl.ANY),
                  pl.BlockSpec((1, bq, 8), lambda p, i, js: (p, i, 0)),
                  pl.BlockSpec((1, 8, S), lambda p, i, js: (p, 0, 0))],
        out_specs=[pl.BlockSpec((bq, LANES), lambda p, i, js: (i, p)),
                   pl.BlockSpec((1, bq, 8), lambda p, i, js: (p, i, 0))],
        scratch_shapes=[pltpu.VMEM((bq, LANES), F32), pltpu.VMEM((bq, LANES), F32),
                        pltpu.VMEM((bq, LANES), BF16), pltpu.VMEM((bq, LANES), BF16),
                        pltpu.VMEM((6, bq, 1), F32),
                        pltpu.VMEM((2 * KV_SLOTS, bk, LANES), BF16), pltpu.VMEM((2 * KV_SLOTS, bk, LANES), BF16),
                        pltpu.SemaphoreType.DMA((2, 2 * KV_SLOTS))])
    return pl.pallas_call(
        body, name="fox_fwd", grid_spec=grid_spec,
        out_shape=[jax.ShapeDtypeStruct((S, GROUP_W), F32), jax.ShapeDtypeStruct((N_PAIRS, S, 8), F32)],
    )(jstart, proj, proj, c_col, c_row)


def _fox_bwd(proj, col0, do, o, st, c_col, c_row, jstart, bq, bk):
    S = proj.shape[0]
    nq, per = S // bq, bq // bk

    def body(js_ref, q_ref, kv_hbm, do_ref, o_ref, st_ref, cc_ref, cr_ref,
             dq_ref, dk_out, dv_out, dc_ref, dq_a, dq_b, qa, qb, doa, dob, dd, kbuf, vbuf, sems, dk_ref, dv_ref):
        p, i = pl.program_id(0), pl.program_id(1)
        j0 = js_ref[p, i]
        first_two = _first_two_up(lambda pair, blk: js_ref[pair, blk], per)
        fetch, kbuf, vbuf = _kv_fetcher(kv_hbm, kbuf, vbuf, sems, KV_SLOTS, col0, bk, p, i, nq, first_two)
        is_a = lax.broadcasted_iota(jnp.int32, (bq, LANES), 1) < HEAD_DIM

        @pl.when(i == 0)
        def _():
            dk_ref[...] = jnp.zeros_like(dk_ref)
            dv_ref[...] = jnp.zeros_like(dv_ref)
            dc_ref[...] = jnp.zeros_like(dc_ref)

        dq_a[...] = jnp.zeros_like(dq_a)
        dq_b[...] = jnp.zeros_like(dq_b)
        qa[...], qb[...] = _masked_pair(q_ref[...], is_a, SCALE)
        dov = do_ref[...]
        doa[...], dob[...] = _masked_pair(dov, is_a)
        prod = dov * o_ref[...]
        dd[0] = jnp.sum(jnp.where(is_a, prod, 0.0), axis=1, keepdims=True)
        dd[1] = jnp.sum(jnp.where(is_a, 0.0, prod), axis=1, keepdims=True)
        dd[2] = jnp.zeros((bq, 1), F32)
        dd[3] = jnp.zeros((bq, 1), F32)
        cc, st_v = cc_ref[0], st_ref[0]
        dd[4], dd[5] = _col(cc, 0) - _col(st_v, 0), _col(cc, 1) - _col(st_v, 1)

        def tile(j, slot, masked):
            k, v = kbuf[slot], vbuf[slot]
            if masked:
                tri = _causal(bq, bk, i, j)
            cols = pl.ds(pl.multiple_of(j * bk, bk), bk)

            def heads(hs):
                qs, dos, dqs = (qa, qb), (doa, dob), (dq_a, dq_b)
                z = {h: _dot(qs[h][...], k, _NT) for h in hs}
                dp = {h: _dot(dos[h][...], v, _NT) for h in hs}
                pr = {h: jnp.exp(z[h] - cr_ref[0, pl.ds(h, 1), cols] + dd[4 + h]) for h in hs}
                if masked:
                    pr = {h: jnp.where(tri, pr[h], 0.0) for h in hs}
                ds = {h: pr[h] * (dp[h] - dd[h]) for h in hs}
                csum = {h: jnp.sum(ds[h], axis=0, keepdims=True) for h in hs}
                rsum = {h: jnp.sum(ds[h], axis=1, keepdims=True) for h in hs}
                dsb = {h: ds[h].astype(BF16) for h in hs}
                prb = {h: pr[h].astype(BF16) for h in hs}
                dqc = {h: _dot(dsb[h], k) for h in hs}
                dkc = [_dot(dsb[h], qs[h][...], _TN) for h in hs]
                dvc = [_dot(prb[h], dos[h][...], _TN) for h in hs]
                for h in hs:
                    dc_ref[0, pl.ds(h, 1), cols] -= csum[h]
                    dd[2 + h] += rsum[h]
                    dqs[h][...] += dqc[h]
                dk_ref[cols, :] += sum(dkc[1:], dkc[0])
                dv_ref[cols, :] += sum(dvc[1:], dvc[0])

            _by_heads(j, js_ref[N_PAIRS + 2 * p, i], js_ref[N_PAIRS + 2 * p + 1, i], heads)

        _walk_up(fetch, j0, per * i, per * i + per - 1, tile)
        dq_ref[...] = (jnp.where(is_a, dq_a[...], dq_b[...]) * SCALE).astype(BF16)
        eye = lax.broadcasted_iota(jnp.int32, (bq, bq), 0) == lax.broadcasted_iota(jnp.int32, (bq, bq), 1)
        own = pl.ds(pl.multiple_of(i * bq, bq), bq)
        for h in range(2):
            dc_ref[0, pl.ds(h, 1), own] += jnp.sum(jnp.where(eye, dd[2 + h], 0.0), axis=0, keepdims=True)

        @pl.when(i == nq - 1)
        def _():
            dk_out[...] = dk_ref[...].astype(BF16)
            dv_out[...] = dv_ref[...].astype(BF16)

    grid_spec = pltpu.PrefetchScalarGridSpec(
        num_scalar_prefetch=1, grid=(N_PAIRS, nq),
        in_specs=[pl.BlockSpec((bq, LANES), lambda p, i, js: (i, col0 + p)),
                  pl.BlockSpec(memory_space=pl.ANY),
                  pl.BlockSpec((bq, LANES), lambda p, i, js: (i, p)),
                  pl.BlockSpec((bq, LANES), lambda p, i, js: (i, p)),
                  pl.BlockSpec((1, bq, 8), lambda p, i, js: (p, i, 0)),
                  pl.BlockSpec((1, bq, 8), lambda p, i, js: (p, i, 0)),
                  pl.BlockSpec((1, 8, S), lambda p, i, js: (p, 0, 0))],
        out_specs=[pl.BlockSpec((bq, LANES), lambda p, i, js: (i, p)),
                   pl.BlockSpec((S, LANES), lambda p, i, js: (0, p)),
                   pl.BlockSpec((S, LANES), lambda p, i, js: (0, p)),
                   pl.BlockSpec((1, 8, S), lambda p, i, js: (p, 0, 0))],
        scratch_shapes=[pltpu.VMEM((bq, LANES), F32), pltpu.VMEM((bq, LANES), F32)]
        + [pltpu.VMEM((bq, LANES), BF16)] * 4 + [pltpu.VMEM((6, bq, 1), F32)]
        + [pltpu.VMEM((2 * KV_SLOTS, bk, LANES), BF16)] * 2 + [pltpu.SemaphoreType.DMA((2, 2 * KV_SLOTS))]
        + [pltpu.VMEM((S, LANES), F32)] * 2)
    return pl.pallas_call(
        body, name="fox_bwd", grid_spec=grid_spec,
        out_shape=[jax.ShapeDtypeStruct((S, GROUP_W), BF16)] * 3 + [jax.ShapeDtypeStruct((N_PAIRS, 8, S), F32)],
        compiler_params=_params(VMEM_BIG),
    )(jstart, proj, proj, do, o, st, c_col, c_row)


_HBM = pl.BlockSpec(memory_space=pltpu.HBM)


def _coords():
    return lax.axis_index("x"), lax.axis_index("y"), lax.axis_index("c")


def _gather_copies(ins, outs, send_sems, recv_sems, loc_sems):
    n = len(ins)
    x, y, c = _coords()
    mine = 2 * x + y
    chips = [(1 - x, y), (x, 1 - y), (1 - x, 1 - y)]

    def copy(w, r, slab, to):
        return pltpu.make_async_remote_copy(
            src_ref=ins[w], dst_ref=outs[w].at[slab], send_sem=send_sems.at[3 * w + r],
            recv_sem=recv_sems.at[3 * w + r], device_id=to, device_id_type=MESH)

    def own():
        local = [pltpu.make_async_copy(ins[w], outs[w].at[mine], loc_sems.at[w]) for w in range(n)]
        return local, [copy(w, r, mine, (cx, cy, c)) for w in range(n) for r, (cx, cy) in enumerate(chips)]

    def start():
        local, sends = own()
        for cp in local + sends:
            cp.start()

    def wait():
        local, sends = own()
        for w in range(n):
            for r, (cx, cy) in enumerate(chips):
                copy(w, r, 2 * cx + cy, (cx, cy, c)).wait_recv()
        for cp in sends:
            cp.wait_send()
        for cp in local:
            cp.wait()

    return start, wait


def _gather_shapes(shards):
    n = len(shards)
    return ([jax.ShapeDtypeStruct((4,) + s.shape, s.dtype) for s in shards],
            [pltpu.SemaphoreType.DMA((3 * n,)), pltpu.SemaphoreType.DMA((3 * n,)), pltpu.SemaphoreType.DMA((n,))])


def _allgather_chips(shards):
    n = len(shards)

    def body(*refs):
        start, wait = _gather_copies(refs[:n], refs[n:2 * n], *refs[2 * n:])
        start()
        wait()

    out_shape, sems = _gather_shapes(shards)
    return pl.pallas_call(body, name="allgather_weights", in_specs=[_HBM] * n, out_specs=[_HBM] * n,
                          out_shape=out_shape, scratch_shapes=sems)(*shards)


def _exchange_copies(ins, outs, send_sems, recv_sems, loc_sems, per_chip, parts):
    n = len(parts)
    half = [p.shape[1] // 2 for p in parts] if per_chip else None
    x, y, c = _coords()
    me = 4 * x + 2 * y + c
    peers = [(x ^ fx, y ^ fy, c ^ fc) for fx in (0, 1) for fy in (0, 1) for fc in (0, 1)][1:]

    def src(w, dev):
        if not per_chip:
            return ins[w]
        return ins[w].at[2 * dev[0] + dev[1], pl.ds(pl.multiple_of(dev[2] * half[w], 16), half[w]), :]

    def copy(w, r, source, slab, to):
        return pltpu.make_async_remote_copy(
            src_ref=source, dst_ref=outs[w].at[slab], send_sem=send_sems.at[7 * w + r],
            recv_sem=recv_sems.at[7 * w + r], device_id=to, device_id_type=MESH)

    def own():
        local = [pltpu.make_async_copy(src(w, (x, y, c)), outs[w].at[me], loc_sems.at[w]) for w in range(n)]
        return local, [copy(w, r, src(w, dev), me, dev) for w in range(n) for r, dev in enumerate(peers)]

    def start():
        local, sends = own()
        for cp in local + sends:
            cp.start()

    def wait():
        local, sends = own()
        for w in range(n):
            for r, dev in enumerate(peers):
                copy(w, r, src(w, dev), 4 * dev[0] + 2 * dev[1] + dev[2], dev).wait_recv()
        for cp in sends:
            cp.wait_send()
        for cp in local:
            cp.wait()

    return start, wait


def _exchange_shapes(parts, per_chip):
    n = len(parts)
    return ([jax.ShapeDtypeStruct((8, p.shape[1] // 2, p.shape[2]) if per_chip else (8,) + p.shape, p.dtype)
             for p in parts],
            [pltpu.SemaphoreType.DMA((7 * n,)), pltpu.SemaphoreType.DMA((7 * n,)), pltpu.SemaphoreType.DMA((n,))])


def _exchange(parts, per_chip):
    n = len(parts)

    def body(*refs):
        start, wait = _exchange_copies(refs[:n], refs[n:2 * n], *refs[2 * n:], per_chip, parts)
        start()
        wait()

    out_shape, sems = _exchange_shapes(parts, per_chip)
    return pl.pallas_call(body, name="exchange_per_chip" if per_chip else "exchange_all",
                          in_specs=[_HBM] * n, out_specs=[_HBM] * n, out_shape=out_shape, scratch_shapes=sems)(*parts)


def _sibling_swap(halves):
    n = len(halves)

    def body(*refs):
        ins, outs = refs[:n], refs[n:2 * n]
        send_sems, recv_sems, loc_sems = refs[2 * n:]
        x, y, c = _coords()

        def rows(w, core):
            rh = halves[w].shape[0]
            return outs[w].at[pl.ds(pl.multiple_of(core * rh, 8), rh), :]

        def copy(w, core):
            return pltpu.make_async_remote_copy(
                src_ref=ins[w], dst_ref=rows(w, core), send_sem=send_sems.at[w], recv_sem=recv_sems.at[w],
                device_id=(x, y, 1 - c), device_id_type=MESH)

        local = [pltpu.make_async_copy(ins[w], rows(w, c), loc_sems.at[w]) for w in range(n)]
        sends = [copy(w, c) for w in range(n)]
        for cp in local + sends:
            cp.start()
        for w in range(n):
            copy(w, 1 - c).wait_recv()
        for cp in sends:
            cp.wait_send()
        for cp in local:
            cp.wait()

    vmem = pl.BlockSpec(memory_space=pltpu.VMEM)
    return pl.pallas_call(
        body, name="sibling_swap", in_specs=[vmem] * n, out_specs=[vmem] * n,
        out_shape=[jax.ShapeDtypeStruct((2 * h.shape[0], h.shape[1]), h.dtype) for h in halves],
        scratch_shapes=[pltpu.SemaphoreType.DMA((n,)), pltpu.SemaphoreType.DMA((n,)), pltpu.SemaphoreType.DMA((n,))],
    )(*halves)


def _adamw(w, g, m, v):
    m = ADAM_B1 * m + (1.0 - ADAM_B1) * g
    v = ADAM_B2 * v + (1.0 - ADAM_B2) * (g * g)
    m_hat = m / (1.0 - ADAM_B1 ** ADAM_STEP)
    v_hat = v / (1.0 - ADAM_B2 ** ADAM_STEP)
    delta = -ADAM_LR * (m_hat / (jnp.sqrt(v_hat) + ADAM_EPS) + ADAM_WD * w)
    return delta, m, v


def _sum_parts(parts, name, tr):
    _, R, C = parts.shape
    assert R % tr == 0

    def body(p_ref, g_ref):
        g = p_ref[0].astype(F32)
        for d in range(1, 8):
            g = g + p_ref[d].astype(F32)
        g_ref[...] = g

    return pl.pallas_call(
        body, name=name, grid=(R // tr,),
        in_specs=[pl.BlockSpec((8, tr, C), lambda i: (0, i, 0))],
        out_specs=pl.BlockSpec((tr, C), lambda i: (i, 0)), out_shape=jax.ShapeDtypeStruct((R, C), F32),
    )(parts)


def _adamw_call(g, w, m, v, name, tr):
    R, C = w.shape
    assert R % tr == 0

    def body(g_ref, w_ref, m_ref, v_ref, d_ref, nm_ref, nv_ref):
        d_ref[...], nm_ref[...], nv_ref[...] = _adamw(w_ref[...], g_ref[...], m_ref[...], v_ref[...])

    tile = pl.BlockSpec((tr, C), lambda i: (i, 0))
    return pl.pallas_call(
        body, name=name, grid=(R // tr,), in_specs=[tile] * 4,
        out_specs=[tile] * 3, out_shape=[jax.ShapeDtypeStruct((R, C), F32)] * 3,
    )(g, w, m, v)


def _sum_adamw_small(parts, w, m, v):
    def body(p_ref, w_ref, m_ref, v_ref, g_ref, d_ref, nm_ref, nv_ref, loss_ref):
        g = p_ref[0]
        for d in range(1, 8):
            g = g + p_ref[d]
        g_ref[...] = g
        d_ref[...], nm_ref[...], nv_ref[...] = _adamw(w_ref[...], g, m_ref[...], v_ref[...])
        row = lax.broadcasted_iota(jnp.int32, g.shape, 0)
        per_row = jnp.sum(jnp.where(row == 6, g, 0.0), axis=1, keepdims=True)
        loss_ref[...] = jnp.zeros((8, LANES), F32) + jnp.sum(per_row, axis=0, keepdims=True)

    return pl.pallas_call(
        body, name="sum_adamw_small",
        out_shape=[jax.ShapeDtypeStruct((8, D_MODEL), F32)] * 4 + [jax.ShapeDtypeStruct((8, LANES), F32)],
    )(parts, w, m, v)


def _pack_small(ln1_g, ln1_b, ln2_g, ln2_b, g_sb, g_fox, b_f):
    row5 = jnp.pad(b_f.reshape(1, N_FOX), ((0, 0), (0, D_MODEL - N_FOX)))
    rows = [ln1_g.reshape(1, -1), ln1_b.reshape(1, -1), ln2_g.reshape(1, -1), ln2_b.reshape(1, -1),
            jnp.concatenate([g_sb.reshape(1, -1), g_fox.reshape(1, -1)], axis=1), row5,
            jnp.zeros((2, D_MODEL), F32)]
    return jnp.concatenate(rows, axis=0)


def _unpack_small(p):
    return {"ln1_g": p[0:1], "ln1_b": p[1:2], "ln2_g": p[2:3], "ln2_b": p[3:4], "g_sb": p[4:5, :GROUP_W],
            "g_fox": p[4:5, GROUP_W:], "b_f": p[5:6, :N_FOX]}


def kernel(x, w_in, b_f, g_sb, g_fox, w_out, ln1_g, ln1_b, ln2_g, ln2_b, w_gate_up, w_down, loss_target, m_w_in, m_b_f, m_g_sb, m_g_fox, m_w_out, m_ln1_g, m_ln1_b, m_ln2_g, m_ln2_b, m_w_gate_up, m_w_down, v_w_in, v_b_f, v_g_sb, v_g_fox, v_w_out, v_ln1_g, v_ln1_b, v_ln2_g, v_ln2_b, v_w_gate_up, v_w_down):
    S = x.shape[1]
    x2 = x.reshape(S, D_MODEL)
    tgt = loss_target.reshape(S, D_MODEL)
    TM = 1024
    TR = 512
    BQ = ATTN_BLOCK
    in_w = w_in.shape[2]
    gu_w = w_gate_up.shape[2]

    shards = [w_in[0].astype(BF16), w_out[0].astype(BF16), w_gate_up[0].astype(BF16), w_down[0].astype(BF16)]
    (wi_s,) = _allgather_chips(shards[:1])
    wi = wi_s.transpose(1, 0, 2).reshape(D_MODEL, 4 * in_w)
    w_sb, w_fx = wi[:, :QKV_W // 2], wi[:, QKV_W // 2:QKV_W]
    wqkv = wi[:, :QKV_W]
    wft = wi[:, QKV_W:].T
    proj = _matmul(x2, wqkv, mode="nn", name="proj", tm=TM, tn=512, tk=D_MODEL, outs=[BF16])
    g_row = jnp.concatenate([g_sb, g_fox], axis=1)
    hid = np.arange(D_MODEL) // HEAD_DIM
    he_np = (hid[:, None] == np.arange(LANES)[None, :]).astype(np.float32)
    he, het = jnp.asarray(he_np, BF16), jnp.asarray(he_np.T, BF16)

    lf = _fgate_fwd(x2, wft, b_f.reshape(N_FOX, 1), TM)
    c = _cumsum_fwd(lf)
    c_pair = c.reshape(N_PAIRS, 2, S)
    c_row = jnp.pad(c_pair, ((0, 0), (0, 6), (0, 0)))
    c_col = jnp.pad(c_pair.transpose(0, 2, 1), ((0, 0), (0, 0), (0, 6)))

    o_sb, st_sb, jmin_sb, wo_s, wgu_s, wd_s = _sb_fwd(proj, 0, BQ, shards[1:])
    wo = wo_s.reshape(D_MODEL, D_MODEL)
    wgu = wgu_s.transpose(1, 0, 2).reshape(D_MODEL, 2 * D_FF)
    wg, wu = wgu[:, :D_FF], wgu[:, D_FF:]
    wd = wd_s.reshape(D_FF, D_MODEL)
    jstart_fx = _fox_start_blocks(*_fox_row_norms(proj, 12, TR), c, BQ, BQ)
    o_fx, st_fx = _fox_fwd(proj, 12, c_col, c_row, jstart_fx, BQ, BQ)

    def attn_post(i, osb_ref, ofx_ref, g_ref, he_ref, het_ref, on_ref):
        o = jnp.concatenate([osb_ref[...], ofx_ref[...]], axis=1)
        ms = _head_sums(o * o, he_ref[...], het_ref[...]) * (1.0 / HEAD_DIM)
        on_ref[...] = (o * lax.rsqrt(ms + RMS_EPS) * g_ref[...]).astype(BF16)

    (on,) = _rowwise(attn_post, "attn_post", S, TR,
                     [(o_sb, "t"), (o_fx, "t"), (g_row, "f"), (he, "f"), (het, "f")],
                     [((S, D_MODEL), BF16, "t")])

    def mix_ln1(acc, xv, g, b):
        u = ALPHA * xv + acc
        xh, _ = _ln_stats(u)
        return u, xh * g + b

    row = lambda i, j: (0, 0)
    u1, h1 = _matmul(on, wo, mode="nn", name="mix_ln1", tm=TM, tn=D_MODEL, tk=D_MODEL, outs=[F32, F32],
                     extras=[(x2, (TM if S >= TM else S, D_MODEL), _tile_ij),
                             (ln1_g, (1, D_MODEL), row), (ln1_b, (1, D_MODEL), row)],
                     epilogue=mix_ln1)

    tm_e = TM if S >= TM else S
    n_ff = D_FF // 256

    def gate_up_body(h_ref, wg_ref, wu_ref, g_ref, u_ref, a_ref):
        h = h_ref[...].astype(BF16)
        g, u = _dot(h, wg_ref[...]), _dot(h, wu_ref[...])
        g_ref[...] = g.astype(BF16)
        u_ref[...] = u.astype(BF16)
        a_ref[...] = (g * _sigmoid(g) * u).astype(BF16)

    tm_g = min(2 * TM, S)
    ff_tile = pl.BlockSpec((tm_g, 256), lambda i, j: (i, j))
    gate, up, act = pl.pallas_call(
        gate_up_body, name="gate_up_act", grid=(S // tm_g, n_ff),
        in_specs=[pl.BlockSpec((tm_g, D_MODEL), lambda i, j: (i, 0)),
                  pl.BlockSpec((D_MODEL, 256), lambda i, j: (0, j)),
                  pl.BlockSpec((D_MODEL, 256), lambda i, j: (0, j + n_ff))],
        out_specs=[ff_tile] * 3, out_shape=[jax.ShapeDtypeStruct((S, D_FF), BF16)] * 3)(h1, wgu, wgu)

    u2 = _matmul(act, wd, mode="nn", name="ffn_down", tm=TM, tn=D_MODEL, tk=D_FF, outs=[F32],
                 extras=[(h1, (TM if S >= TM else S, D_MODEL), _tile_ij)],
                 epilogue=lambda acc, hv: (ALPHA * hv + acc,))

    def ln2_loss(i, u_ref, t_ref, g_ref, b_ref, du_ref, acc_ref):
        xh, r = _ln_stats(u_ref[...])
        g = g_ref[...]
        err = xh * g + b_ref[...] - t_ref[...]
        dy = err * (1.0 / D_MODEL)
        du_ref[...] = _ln_bwd(dy, xh, r, g)
        _acc_rows(i, acc_ref, {2: jnp.sum(dy * xh, axis=0, keepdims=True), 3: jnp.sum(dy, axis=0, keepdims=True),
                               6: jnp.sum(err * err, axis=0, keepdims=True) * (0.5 / D_MODEL)})

    du2, acc_ln2 = _rowwise(ln2_loss, "ln2_loss", S, TR, [(u2, "t"), (tgt, "t"), (ln2_g, "f"), (ln2_b, "f")],
                            [((S, D_MODEL), F32, "t"), ((8, D_MODEL), F32, "f")])

    d_wd = _matmul(act, du2, mode="tn", name="dw_down", tm=1408, tn=D_MODEL, tk=TM, outs=[BF16])

    def dgu_epilogue(da, g, u):
        g, u = g.astype(F32), u.astype(F32)
        s = _sigmoid(g)
        return da * u * (s * (1.0 + g * (1.0 - s))), da * (g * s)

    dgate, dup = _matmul(du2, wd, mode="nt", name="d_act", tm=TM, tn=1408, tk=D_MODEL, outs=[BF16, BF16],
                         extras=[(gate, (tm_e, 1408), _tile_ij), (up, (tm_e, 1408), _tile_ij)],
                         epilogue=dgu_epilogue)
    d_wg = _matmul(h1, dgate, mode="tn", name="dw_gate", tm=D_MODEL, tn=1408, tk=TM, outs=[BF16])
    d_wu = _matmul(h1, dup, mode="tn", name="dw_up", tm=D_MODEL, tn=1408, tk=TM, outs=[BF16])
    d_wgu = jnp.concatenate([d_wg, d_wu], axis=1)
    dh1, got_down = _matmul(dgate, wg, mode="nt", name="dh1_gate", tm=TM, tn=D_MODEL, tk=D_FF, outs=[F32],
                            extras=[(du2, (tm_e, D_MODEL), _tile_ij)], epilogue=lambda acc, e: (ALPHA * e + acc,),
                            hosted=[d_wd.reshape(4, D_FF // 4, D_MODEL)])
    dh1, got_gu = _matmul(dup, wu, mode="nt", name="dh1_up", tm=TM, tn=D_MODEL, tk=D_FF, outs=[F32],
                          extras=[(dh1, (tm_e, D_MODEL), _tile_ij)], epilogue=lambda acc, e: (e + acc,),
                          hosted=[d_wgu.reshape(D_MODEL, 4, gu_w).transpose(1, 0, 2)])

    def ln1_bwd(i, dh_ref, u_ref, g_ref, du_ref, acc_ref):
        xh, r = _ln_stats(u_ref[...])
        dh = dh_ref[...]
        du_ref[...] = _ln_bwd(dh, xh, r, g_ref[...])
        _acc_rows(i, acc_ref, {0: jnp.sum(dh * xh, axis=0, keepdims=True), 1: jnp.sum(dh, axis=0, keepdims=True)})

    du1, acc_ln1 = _rowwise(ln1_bwd, "ln1_bwd", S, TR, [(dh1, "t"), (u1, "t"), (ln1_g, "f")],
                            [((S, D_MODEL), F32, "t"), ((8, D_MODEL), F32, "f")])
    d_wo = _matmul(on, du1, mode="tn", name="dw_out", tm=D_MODEL, tn=D_MODEL, tk=TM, outs=[BF16])
    don, got_out = _matmul(du1, wo, mode="nt", name="d_on", tm=TM, tn=D_MODEL, tk=D_MODEL, outs=[F32],
                           hosted=[d_wo.reshape(4, D_MODEL // 4, D_MODEL)])

    def rms_bwd(i, don_ref, osb_ref, ofx_ref, g_ref, he_ref, het_ref, dosb_ref, dofx_ref, acc_ref):
        o = jnp.concatenate([osb_ref[...], ofx_ref[...]], axis=1)
        hev, hetv = he_ref[...], het_ref[...]
        r = lax.rsqrt(_head_sums(o * o, hev, hetv) * (1.0 / HEAD_DIM) + RMS_EPS)
        dn = don_ref[...]
        dg = dn * g_ref[...]
        do = r * dg - o * (r * r * r) * (_head_sums(dg * o, hev, hetv) * (1.0 / HEAD_DIM))
        dosb_ref[...] = do[:, :GROUP_W]
        dofx_ref[...] = do[:, GROUP_W:]
        _acc_rows(i, acc_ref, {4: jnp.sum(dn * o * r, axis=0, keepdims=True)})

    do_sb, do_fx, acc_rms = _rowwise(
        rms_bwd, "rms_bwd", S, TR, [(don, "t"), (o_sb, "t"), (o_fx, "t"), (g_row, "f"), (he, "f"), (het, "f")],
        [((S, GROUP_W), F32, "t"), ((S, GROUP_W), F32, "t"), ((8, D_MODEL), F32, "f")])

    dq_sb, dk_sb, dv_sb = _sb_bwd(proj, 0, do_sb, st_sb, jmin_sb, BQ)
    jstart_fx2 = jnp.minimum(jstart_fx[:, 0::2], jstart_fx[:, 1::2])
    dq_fx, dk_fx, dv_fx, dc = _fox_bwd(proj, 12, do_fx, o_fx, st_fx, c_col, c_row, jstart_fx2, 2 * BQ, BQ)
    dfl, dbf = _fgate_bwd(dc[:, :2, :].reshape(N_FOX, S), lf)
    dp_sb = jnp.concatenate([dq_sb, dk_sb, dv_sb], axis=1)
    dp_fx = jnp.concatenate([dq_fx, dk_fx, dv_fx], axis=1)

    d_wsb = _matmul(x2, dp_sb, mode="tn", name="dw_in_sb", tm=D_MODEL, tn=QKV_W // 2, tk=TM, outs=[BF16])
    d_wfx = _matmul(x2, dp_fx, mode="tn", name="dw_in_fx", tm=D_MODEL, tn=QKV_W // 2, tk=TM, outs=[BF16])
    d_wft = _matmul(dfl, x2, mode="nn", name="dw_in_f", tm=N_FOX, tn=D_MODEL, tk=TM, outs=[BF16])
    d_wi = jnp.concatenate([d_wsb, d_wfx, d_wft.T], axis=1)
    dx, got_in = _matmul(dp_sb, w_sb, mode="nt", name="dx_sb", tm=TM, tn=D_MODEL, tk=QKV_W // 2, outs=[F32],
                         extras=[(du1, (tm_e, D_MODEL), _tile_ij)], epilogue=lambda acc, e: (ALPHA * e + acc,),
                         hosted=[d_wi.reshape(D_MODEL, 4, in_w).transpose(1, 0, 2)])
    dx = _matmul(dp_fx, w_fx, mode="nt", name="dx_fx", tm=TM, tn=D_MODEL, tk=QKV_W // 2, outs=[F32],
                 extras=[(dx, (tm_e, D_MODEL), _tile_ij), (dfl, (N_FOX, tm_e), lambda i, j: (0, i)),
                         (wft, (N_FOX, D_MODEL), lambda i, j: (0, 0))],
                 epilogue=lambda acc, e, df, wf: (e + acc + _dot(df.astype(BF16), wf, _TN),))

    got = [got_in, got_out, got_gu, got_down]
    big_names = ("w_in", "w_out", "w_gate_up", "w_down")
    halves = [_sum_parts(p, "sum_" + nm, tr) for nm, p, tr in zip(big_names, got, (256, 128, 128, 176))]
    grads = _sibling_swap(halves)
    big = {}
    for nm, g, w, m, v, tr in zip(big_names, grads, (w_in, w_out, w_gate_up, w_down),
                                  (m_w_in, m_w_out, m_w_gate_up, m_w_down),
                                  (v_w_in, v_w_out, v_w_gate_up, v_w_down), (256, 256, 256, 176)):
        big[nm] = [r[None] for r in [g] + list(_adamw_call(g, w[0], m[0], v[0], "adamw_" + nm, tr))]

    small = acc_ln2 + acc_ln1 + acc_rms
    small = small + jnp.pad(dbf.reshape(1, N_FOX), ((5, 2), (0, D_MODEL - N_FOX)))
    (small_all,) = _exchange([small], False)
    sw = _pack_small(ln1_g, ln1_b, ln2_g, ln2_b, g_sb, g_fox, b_f)
    sm = _pack_small(m_ln1_g, m_ln1_b, m_ln2_g, m_ln2_b, m_g_sb, m_g_fox, m_b_f)
    sv = _pack_small(v_ln1_g, v_ln1_b, v_ln2_g, v_ln2_b, v_g_sb, v_g_fox, v_b_f)
    sg, sd, snm, snv, loss_blk = _sum_adamw_small(small_all, sw, sm, sv)
    sg, sd, snm, snv = _unpack_small(sg), _unpack_small(sd), _unpack_small(snm), _unpack_small(snv)

    names = ["w_in", "b_f", "g_sb", "g_fox", "w_out", "ln1_g", "ln1_b", "ln2_g", "ln2_b", "w_gate_up", "w_down"]
    outs = [loss_blk[0, 0], dx.reshape(1, S, D_MODEL)]
    for k, table in enumerate((sg, sd, snm, snv)):
        outs += [big[n][k] if n in big else table[n] for n in names]
    return tuple(outs)
```
